```python
import jax, jax.numpy as jnp
from jax import lax
import numpy as np

D_MODEL = 1024
BATCH = 8
SEQ = 8192
DEPTH = 1

N_Q_HEADS = 16
N_KV_HEADS = 2
HEAD_DIM = 64
Q_PER_KV = N_Q_HEADS // N_KV_HEADS
WINDOW = 128
BLOCK = 128
ATTN_WIDTH = N_Q_HEADS * HEAD_DIM
KV_WIDTH = N_KV_HEADS * HEAD_DIM
POOL_WINDOWS = (2, 4, 8, 16)
N_POOL_GROUPS = len(POOL_WINDOWS)
POOL_WIDTH = 512
POOL_GROUP = POOL_WIDTH // N_POOL_GROUPS
D_FF = 2816
NORM_EPS = 1e-6
IN_SPLITS = tuple(int(s) for s in np.cumsum([ATTN_WIDTH, KV_WIDTH, KV_WIDTH, POOL_WIDTH, D_MODEL]))
IN_WIDTH = ATTN_WIDTH + 2 * KV_WIDTH + POOL_WIDTH + 2 * D_MODEL

kernel_name = "hybrid_swa_sink_alibi_pool_macaron"


def alibi_slopes():
    h = np.arange(1, N_Q_HEADS + 1, dtype=np.float32)
    return jnp.asarray(2.0 ** (-8.0 * h / N_Q_HEADS), dtype=jnp.float32).reshape(N_KV_HEADS, Q_PER_KV)


def rmsnorm(x, g):
    xf = x.astype(jnp.float32)
    y = xf * lax.rsqrt(jnp.mean(xf * xf, axis=-1, keepdims=True) + NORM_EPS)
    return (y * g.astype(jnp.float32)).astype(x.dtype)


def swiglu(x, w_up, w_down):
    a, b = jnp.split(x @ w_up, 2, axis=-1)
    return (jax.nn.silu(a) * b) @ w_down


def sliding_window_attention(q, k, v, sinks):
    B, S, _ = q.shape
    nb = S // BLOCK
    q = q.reshape(B, nb, BLOCK, N_KV_HEADS, Q_PER_KV, HEAD_DIM)
    k = k.reshape(B, S, N_KV_HEADS, HEAD_DIM)
    v = v.reshape(B, S, N_KV_HEADS, HEAD_DIM)
    pad = jnp.zeros((B, BLOCK, N_KV_HEADS, HEAD_DIM), k.dtype)

    def band(t):
        cur = t.reshape(B, nb, BLOCK, N_KV_HEADS, HEAD_DIM)
        prev = jnp.concatenate([pad, t[:, :S - BLOCK]], axis=1).reshape(B, nb, BLOCK, N_KV_HEADS, HEAD_DIM)
        return jnp.concatenate([prev, cur], axis=2)

    kb, vb = band(k), band(v)
    scale = HEAD_DIM ** -0.5
    scores = jnp.einsum('bnqhgd,bnkhd->bnhgqk', q, kb, preferred_element_type=jnp.float32) * scale
    qi = jnp.arange(BLOCK)[:, None] + BLOCK
    kj = jnp.arange(2 * BLOCK)[None, :]
    dist = (qi - kj)
    blk = jnp.arange(nb)[:, None, None]
    valid = (dist >= 0)[None] & (dist < WINDOW)[None] & (blk * BLOCK - BLOCK + kj[None] >= 0)
    slopes = alibi_slopes()[:, :, None, None]
    scores = scores - slopes * dist.astype(jnp.float32)
    scores = jnp.where(valid[None, :, None, None], scores, -jnp.inf)
    sink = sinks.astype(jnp.float32).reshape(N_KV_HEADS, Q_PER_KV)[:, :, None, None]
    m = jnp.maximum(jnp.max(scores, axis=-1, keepdims=True), sink)
    p = jnp.exp(scores - m)
    probs = p / (jnp.sum(p, axis=-1, keepdims=True) + jnp.exp(sink - m))
    out = jnp.einsum('bnhgqk,bnkhd->bnqhgd', probs.astype(vb.dtype), vb)
    return out.reshape(B, S, ATTN_WIDTH)


def multiscale_pool(z, w_mix, scale):
    B, S, _ = z.shape
    zf = z.astype(jnp.float32)
    c = jnp.concatenate([jnp.zeros((B, 1, POOL_WIDTH), jnp.float32), jnp.cumsum(zf, axis=1)], axis=1)
    t = jnp.arange(S)
    outs = []
    for gi, w in enumerate(POOL_WINDOWS):
        cg = c[:, :, gi * POOL_GROUP:(gi + 1) * POOL_GROUP]
        prev = jnp.concatenate([jnp.zeros((B, w - 1, POOL_GROUP), jnp.float32), cg[:, :S - w + 1]], axis=1)
        cnt = jnp.minimum(t + 1, w).astype(jnp.float32)[None, :, None]
        outs.append((cg[:, 1:] - prev) / cnt)
    pooled = (jnp.concatenate(outs, axis=-1) - zf).astype(z.dtype)
    pooled = pooled.reshape(B, S, N_POOL_GROUPS, POOL_GROUP)
    mixed = jnp.einsum('bsgc,gcd->bsgd', pooled, w_mix).reshape(B, S, POOL_WIDTH)
    return mixed * scale


def _fwd_setup_inputs(seed: int = 0) -> dict:
    key = jax.random.key(seed)
    ks = jax.random.split(key, 20)
    f32 = jnp.float32

    def nrm(k, shape, fan_in):
        return jax.random.normal(k, shape, f32) * (fan_in ** -0.5)

    def gain(k, shape):
        return 1.0 + 0.02 * jax.random.normal(k, shape, f32)

    L = DEPTH
    return {
        "x": jax.random.normal(ks[0], (BATCH, SEQ, D_MODEL), f32),
        "ffn1_norm": gain(ks[1], (L, D_MODEL)),
        "ffn1_w_up": nrm(ks[2], (L, D_MODEL, 2 * D_FF), D_MODEL),
        "ffn1_w_down": nrm(ks[3], (L, D_FF, D_MODEL), D_FF),
        "mix_norm": gain(ks[4], (L, D_MODEL)),
        "w_in": nrm(ks[5], (L, D_MODEL, IN_WIDTH), D_MODEL),
        "sinks": jax.random.normal(ks[6], (L, N_Q_HEADS), f32),
        "w_attn_up": nrm(ks[7], (L, ATTN_WIDTH, D_MODEL), ATTN_WIDTH),
        "pool_w_mix": nrm(ks[8], (L, N_POOL_GROUPS, POOL_GROUP, POOL_GROUP), POOL_GROUP),
        "pool_scale": gain(ks[9], (L, POOL_WIDTH)),
        "w_pool_up": nrm(ks[10], (L, POOL_WIDTH, D_MODEL), POOL_WIDTH),
        "w_out": nrm(ks[11], (L, D_MODEL, D_MODEL), D_MODEL),
        "ffn2_norm": gain(ks[12], (L, D_MODEL)),
        "ffn2_w_up": nrm(ks[13], (L, D_MODEL, 2 * D_FF), D_MODEL),
        "ffn2_w_down": nrm(ks[14], (L, D_FF, D_MODEL), D_FF),
        "final_norm": gain(ks[15], (D_MODEL,)),
    }


def _fwd_reference(x, ffn1_norm, ffn1_w_up, ffn1_w_down, mix_norm, w_in, sinks, w_attn_up,
              pool_w_mix, pool_scale, w_pool_up, w_out, ffn2_norm, ffn2_w_up, ffn2_w_down,
              final_norm):
    h = x
    for l in range(DEPTH):
        h = h + 0.5 * swiglu(rmsnorm(h, ffn1_norm[l]), ffn1_w_up[l], ffn1_w_down[l])
        u = rmsnorm(h, mix_norm[l])
        q, k, v, z, g_attn, g_pool = jnp.split(u @ w_in[l], IN_SPLITS, axis=-1)
        a = sliding_window_attention(q, k, v, sinks[l]) @ w_attn_up[l]
        p = multiscale_pool(z, pool_w_mix[l], pool_scale[l]) @ w_pool_up[l]
        merged = jax.nn.sigmoid(g_attn) * a + jax.nn.sigmoid(g_pool) * p
        h = h + merged @ w_out[l]
        h = h + 0.5 * swiglu(rmsnorm(h, ffn2_norm[l]), ffn2_w_up[l], ffn2_w_down[l])
    return rmsnorm(h, final_norm)


import jax as _jax
import jax.numpy as _jnp

TWIN_FORMAT = 'train_step'
FWD_PARAMS = ['x', 'ffn1_norm', 'ffn1_w_up', 'ffn1_w_down', 'mix_norm', 'w_in', 'sinks', 'w_attn_up', 'pool_w_mix', 'pool_scale', 'w_pool_up', 'w_out', 'ffn2_norm', 'ffn2_w_up', 'ffn2_w_down', 'final_norm']
TWIN_WEIGHTS = ['ffn1_norm', 'ffn1_w_up', 'ffn1_w_down', 'mix_norm', 'w_in', 'sinks', 'w_attn_up', 'pool_w_mix', 'pool_scale', 'w_pool_up', 'w_out', 'ffn2_norm', 'ffn2_w_up', 'ffn2_w_down', 'final_norm']
TWIN_DIFF_INPUT = 'x'
TWIN_INPUTS = ['x', 'ffn1_norm', 'ffn1_w_up', 'ffn1_w_down', 'mix_norm', 'w_in', 'sinks', 'w_attn_up', 'pool_w_mix', 'pool_scale', 'w_pool_up', 'w_out', 'ffn2_norm', 'ffn2_w_up', 'ffn2_w_down', 'final_norm', 'loss_target', 'm_ffn1_norm', 'm_ffn1_w_up', 'm_ffn1_w_down', 'm_mix_norm', 'm_w_in', 'm_sinks', 'm_w_attn_up', 'm_pool_w_mix', 'm_pool_scale', 'm_w_pool_up', 'm_w_out', 'm_ffn2_norm', 'm_ffn2_w_up', 'm_ffn2_w_down', 'm_final_norm', 'v_ffn1_norm', 'v_ffn1_w_up', 'v_ffn1_w_down', 'v_mix_norm', 'v_w_in', 'v_sinks', 'v_w_attn_up', 'v_pool_w_mix', 'v_pool_scale', 'v_w_pool_up', 'v_w_out', 'v_ffn2_norm', 'v_ffn2_w_up', 'v_ffn2_w_down', 'v_final_norm']
TWIN_OUTPUTS = ['loss', 'grad_x', 'grad_ffn1_norm', 'grad_ffn1_w_up', 'grad_ffn1_w_down', 'grad_mix_norm', 'grad_w_in', 'grad_sinks', 'grad_w_attn_up', 'grad_pool_w_mix', 'grad_pool_scale', 'grad_w_pool_up', 'grad_w_out', 'grad_ffn2_norm', 'grad_ffn2_w_up', 'grad_ffn2_w_down', 'grad_final_norm', 'delta_ffn1_norm', 'delta_ffn1_w_up', 'delta_ffn1_w_down', 'delta_mix_norm', 'delta_w_in', 'delta_sinks', 'delta_w_attn_up', 'delta_pool_w_mix', 'delta_pool_scale', 'delta_w_pool_up', 'delta_w_out', 'delta_ffn2_norm', 'delta_ffn2_w_up', 'delta_ffn2_w_down', 'delta_final_norm', 'new_m_ffn1_norm', 'new_m_ffn1_w_up', 'new_m_ffn1_w_down', 'new_m_mix_norm', 'new_m_w_in', 'new_m_sinks', 'new_m_w_attn_up', 'new_m_pool_w_mix', 'new_m_pool_scale', 'new_m_w_pool_up', 'new_m_w_out', 'new_m_ffn2_norm', 'new_m_ffn2_w_up', 'new_m_ffn2_w_down', 'new_m_final_norm', 'new_v_ffn1_norm', 'new_v_ffn1_w_up', 'new_v_ffn1_w_down', 'new_v_mix_norm', 'new_v_w_in', 'new_v_sinks', 'new_v_w_attn_up', 'new_v_pool_w_mix', 'new_v_pool_scale', 'new_v_w_pool_up', 'new_v_w_out', 'new_v_ffn2_norm', 'new_v_ffn2_w_up', 'new_v_ffn2_w_down', 'new_v_final_norm']
TWIN_LEAF_KINDS = {'loss': 'loss', 'grad_x': 'grad_x', 'grad_ffn1_norm': 'grad_w', 'grad_ffn1_w_up': 'grad_w', 'grad_ffn1_w_down': 'grad_w', 'grad_mix_norm': 'grad_w', 'grad_w_in': 'grad_w', 'grad_sinks': 'grad_w', 'grad_w_attn_up': 'grad_w', 'grad_pool_w_mix': 'grad_w', 'grad_pool_scale': 'grad_w', 'grad_w_pool_up': 'grad_w', 'grad_w_out': 'grad_w', 'grad_ffn2_norm': 'grad_w', 'grad_ffn2_w_up': 'grad_w', 'grad_ffn2_w_down': 'grad_w', 'grad_final_norm': 'grad_w', 'delta_ffn1_norm': 'delta_w', 'delta_ffn1_w_up': 'delta_w', 'delta_ffn1_w_down': 'delta_w', 'delta_mix_norm': 'delta_w', 'delta_w_in': 'delta_w', 'delta_sinks': 'delta_w', 'delta_w_attn_up': 'delta_w', 'delta_pool_w_mix': 'delta_w', 'delta_pool_scale': 'delta_w', 'delta_w_pool_up': 'delta_w', 'delta_w_out': 'delta_w', 'delta_ffn2_norm': 'delta_w', 'delta_ffn2_w_up': 'delta_w', 'delta_ffn2_w_down': 'delta_w', 'delta_final_norm': 'delta_w', 'new_m_ffn1_norm': 'new_m', 'new_m_ffn1_w_up': 'new_m', 'new_m_ffn1_w_down': 'new_m', 'new_m_mix_norm': 'new_m', 'new_m_w_in': 'new_m', 'new_m_sinks': 'new_m', 'new_m_w_attn_up': 'new_m', 'new_m_pool_w_mix': 'new_m', 'new_m_pool_scale': 'new_m', 'new_m_w_pool_up': 'new_m', 'new_m_w_out': 'new_m', 'new_m_ffn2_norm': 'new_m', 'new_m_ffn2_w_up': 'new_m', 'new_m_ffn2_w_down': 'new_m', 'new_m_final_norm': 'new_m', 'new_v_ffn1_norm': 'new_v', 'new_v_ffn1_w_up': 'new_v', 'new_v_ffn1_w_down': 'new_v', 'new_v_mix_norm': 'new_v', 'new_v_w_in': 'new_v', 'new_v_sinks': 'new_v', 'new_v_w_attn_up': 'new_v', 'new_v_pool_w_mix': 'new_v', 'new_v_pool_scale': 'new_v', 'new_v_w_pool_up': 'new_v', 'new_v_w_out': 'new_v', 'new_v_ffn2_norm': 'new_v', 'new_v_ffn2_w_up': 'new_v', 'new_v_ffn2_w_down': 'new_v', 'new_v_final_norm': 'new_v'}


def _forward(args):
    return _fwd_reference(*[args[k] for k in FWD_PARAMS])


def _output_shape():
    def fwd():
        inp = _fwd_setup_inputs(0)
        return _fwd_reference(*[inp[k] for k in FWD_PARAMS])
    out = _jax.eval_shape(fwd)
    return out.shape, out.dtype

N_MICROBATCH = 1
ADAM_LR = 0.001
ADAM_B1 = 0.9
ADAM_B2 = 0.999
ADAM_EPS = 1e-08
ADAM_WD = 0.01
ADAM_STEP = 10
PER_EXAMPLE_BATCH_AXIS = {'x': 0, 'loss_target': 0}
SHARED_INPUTS = []
_WEIGHT_DTYPES = {'ffn1_norm': _jnp.float32, 'ffn1_w_up': _jnp.float32, 'ffn1_w_down': _jnp.float32, 'mix_norm': _jnp.float32, 'w_in': _jnp.float32, 'sinks': _jnp.float32, 'w_attn_up': _jnp.float32, 'pool_w_mix': _jnp.float32, 'pool_scale': _jnp.float32, 'w_pool_up': _jnp.float32, 'w_out': _jnp.float32, 'ffn2_norm': _jnp.float32, 'ffn2_w_up': _jnp.float32, 'ffn2_w_down': _jnp.float32, 'final_norm': _jnp.float32}
MOMENT_SCALE = {'ffn1_norm': 1.158416e-01, 'ffn1_w_up': 4.648178e-02, 'ffn1_w_down': 7.570324e-02, 'mix_norm': 1.271610e-01, 'w_in': 6.518703e-02, 'sinks': 5.216046e-02, 'w_attn_up': 3.621386e-02, 'pool_w_mix': 1.529441e-01, 'pool_scale': 1.537049e-01, 'w_pool_up': 1.078795e-01, 'w_out': 1.127262e-01, 'ffn2_norm': 9.376869e-02, 'ffn2_w_up': 3.845426e-02, 'ffn2_w_down': 6.298106e-02, 'final_norm': 6.400826e+01}


def _to_microbatches(a, axis):
    t = _jnp.moveaxis(a, axis, 0)
    t = t.reshape((N_MICROBATCH, t.shape[0] // N_MICROBATCH) + t.shape[1:])
    return _jnp.moveaxis(t, 1, axis + 1)


def setup_inputs(seed: int = 0) -> dict:
    inp = _fwd_setup_inputs(seed)
    key = _jax.random.fold_in(_jax.random.key(seed), 7919)
    shape, _ = _output_shape()
    out = dict(inp)
    out["loss_target"] = _jax.random.normal(_jax.random.fold_in(key, 0), shape, _jnp.float32)
    for i, name in enumerate(TWIN_WEIGHTS):
        w = inp[name].astype(_jnp.float32)
        if MOMENT_SCALE is None:
            s = _jnp.sqrt(_jnp.mean(_jnp.square(w)) + 1e-30)
        else:
            s = MOMENT_SCALE[name]
        km, kv = _jax.random.split(_jax.random.fold_in(key, i + 1))
        out[name] = w
        out["m_" + name] = s * _jax.random.normal(km, w.shape, _jnp.float32)
        out["v_" + name] = (s * s) * _jax.random.uniform(kv, w.shape, _jnp.float32, 0.5, 1.5)
    if N_MICROBATCH > 1:
        for name, axis in PER_EXAMPLE_BATCH_AXIS.items():
            out[name] = _to_microbatches(out[name], axis)
    return {'x': out['x'], 'ffn1_norm': out['ffn1_norm'], 'ffn1_w_up': out['ffn1_w_up'], 'ffn1_w_down': out['ffn1_w_down'], 'mix_norm': out['mix_norm'], 'w_in': out['w_in'], 'sinks': out['sinks'], 'w_attn_up': out['w_attn_up'], 'pool_w_mix': out['pool_w_mix'], 'pool_scale': out['pool_scale'], 'w_pool_up': out['w_pool_up'], 'w_out': out['w_out'], 'ffn2_norm': out['ffn2_norm'], 'ffn2_w_up': out['ffn2_w_up'], 'ffn2_w_down': out['ffn2_w_down'], 'final_norm': out['final_norm'], 'loss_target': out['loss_target'], 'm_ffn1_norm': out['m_ffn1_norm'], 'm_ffn1_w_up': out['m_ffn1_w_up'], 'm_ffn1_w_down': out['m_ffn1_w_down'], 'm_mix_norm': out['m_mix_norm'], 'm_w_in': out['m_w_in'], 'm_sinks': out['m_sinks'], 'm_w_attn_up': out['m_w_attn_up'], 'm_pool_w_mix': out['m_pool_w_mix'], 'm_pool_scale': out['m_pool_scale'], 'm_w_pool_up': out['m_w_pool_up'], 'm_w_out': out['m_w_out'], 'm_ffn2_norm': out['m_ffn2_norm'], 'm_ffn2_w_up': out['m_ffn2_w_up'], 'm_ffn2_w_down': out['m_ffn2_w_down'], 'm_final_norm': out['m_final_norm'], 'v_ffn1_norm': out['v_ffn1_norm'], 'v_ffn1_w_up': out['v_ffn1_w_up'], 'v_ffn1_w_down': out['v_ffn1_w_down'], 'v_mix_norm': out['v_mix_norm'], 'v_w_in': out['v_w_in'], 'v_sinks': out['v_sinks'], 'v_w_attn_up': out['v_w_attn_up'], 'v_pool_w_mix': out['v_pool_w_mix'], 'v_pool_scale': out['v_pool_scale'], 'v_w_pool_up': out['v_w_pool_up'], 'v_w_out': out['v_w_out'], 'v_ffn2_norm': out['v_ffn2_norm'], 'v_ffn2_w_up': out['v_ffn2_w_up'], 'v_ffn2_w_down': out['v_ffn2_w_down'], 'v_final_norm': out['v_final_norm']}


def _loss(weights, diff, rest, loss_target):
    with _jax.named_scope("forward"):
        args = {**rest, TWIN_DIFF_INPUT: diff, **{k: w.astype(_WEIGHT_DTYPES[k]) for k, w in weights.items()}}
        y = _forward(args)
    with _jax.named_scope("loss_head"):
        err = _jnp.square(y.astype(_jnp.float32) - loss_target)
        return 0.5 * _jnp.sum(_jnp.mean(err, axis=-1)) if err.ndim else 0.5 * err


def _adamw(w, g, m, v):
    m = ADAM_B1 * m + (1.0 - ADAM_B1) * g
    v = ADAM_B2 * v + (1.0 - ADAM_B2) * _jnp.square(g)
    m_hat = m / (1.0 - ADAM_B1 ** ADAM_STEP)
    v_hat = v / (1.0 - ADAM_B2 ** ADAM_STEP)
    delta = -ADAM_LR * (m_hat / (_jnp.sqrt(v_hat) + ADAM_EPS) + ADAM_WD * w)
    return delta, m, v


def reference(x, ffn1_norm, ffn1_w_up, ffn1_w_down, mix_norm, w_in, sinks, w_attn_up, pool_w_mix, pool_scale, w_pool_up, w_out, ffn2_norm, ffn2_w_up, ffn2_w_down, final_norm, loss_target, m_ffn1_norm, m_ffn1_w_up, m_ffn1_w_down, m_mix_norm, m_w_in, m_sinks, m_w_attn_up, m_pool_w_mix, m_pool_scale, m_w_pool_up, m_w_out, m_ffn2_norm, m_ffn2_w_up, m_ffn2_w_down, m_final_norm, v_ffn1_norm, v_ffn1_w_up, v_ffn1_w_down, v_mix_norm, v_w_in, v_sinks, v_w_attn_up, v_pool_w_mix, v_pool_scale, v_w_pool_up, v_w_out, v_ffn2_norm, v_ffn2_w_up, v_ffn2_w_down, v_final_norm):
    given = dict(x=x, ffn1_norm=ffn1_norm, ffn1_w_up=ffn1_w_up, ffn1_w_down=ffn1_w_down, mix_norm=mix_norm, w_in=w_in, sinks=sinks, w_attn_up=w_attn_up, pool_w_mix=pool_w_mix, pool_scale=pool_scale, w_pool_up=w_pool_up, w_out=w_out, ffn2_norm=ffn2_norm, ffn2_w_up=ffn2_w_up, ffn2_w_down=ffn2_w_down, final_norm=final_norm, loss_target=loss_target, m_ffn1_norm=m_ffn1_norm, m_ffn1_w_up=m_ffn1_w_up, m_ffn1_w_down=m_ffn1_w_down, m_mix_norm=m_mix_norm, m_w_in=m_w_in, m_sinks=m_sinks, m_w_attn_up=m_w_attn_up, m_pool_w_mix=m_pool_w_mix, m_pool_scale=m_pool_scale, m_w_pool_up=m_w_pool_up, m_w_out=m_w_out, m_ffn2_norm=m_ffn2_norm, m_ffn2_w_up=m_ffn2_w_up, m_ffn2_w_down=m_ffn2_w_down, m_final_norm=m_final_norm, v_ffn1_norm=v_ffn1_norm, v_ffn1_w_up=v_ffn1_w_up, v_ffn1_w_down=v_ffn1_w_down, v_mix_norm=v_mix_norm, v_w_in=v_w_in, v_sinks=v_sinks, v_w_attn_up=v_w_attn_up, v_pool_w_mix=v_pool_w_mix, v_pool_scale=v_pool_scale, v_w_pool_up=v_w_pool_up, v_w_out=v_w_out, v_ffn2_norm=v_ffn2_norm, v_ffn2_w_up=v_ffn2_w_up, v_ffn2_w_down=v_ffn2_w_down, v_final_norm=v_final_norm)
    weights = {n: given[n] for n in TWIN_WEIGHTS}
    shared = {n: given[n] for n in SHARED_INPUTS}
    per_example = {n: given[n] for n in ['x']}
    grad_fn = _jax.value_and_grad(_loss, argnums=(0, 1))

    def one_microbatch(ex, loss_target):
        ex = dict(ex)
        diff = ex.pop(TWIN_DIFF_INPUT)
        return grad_fn(weights, diff, {**shared, **ex}, loss_target)

    if N_MICROBATCH == 1:
        loss, (grad_w, grad_x) = one_microbatch(per_example, given["loss_target"])
    else:
        def body(carry, xs):
            loss_sum, grad_sum = carry
            l_k, (gw_k, gx_k) = one_microbatch(xs[0], xs[1])
            with _jax.named_scope("update"):
                return (loss_sum + l_k, _jax.tree.map(_jnp.add, grad_sum, gw_k)), gx_k

        init = (_jnp.zeros((), _jnp.float32), _jax.tree.map(_jnp.zeros_like, weights))
        (loss, grad_w), grad_x = _jax.lax.scan(body, init, (per_example, given["loss_target"]))
    with _jax.named_scope("update"):
        delta_w, new_m, new_v = {}, {}, {}
        for n in TWIN_WEIGHTS:
            delta_w[n], new_m[n], new_v[n] = _adamw(weights[n], grad_w[n], given["m_" + n], given["v_" + n])
    return (loss, grad_x, *[grad_w[n] for n in TWIN_WEIGHTS], *[delta_w[n] for n in TWIN_WEIGHTS],
            *[new_m[n] for n in TWIN_WEIGHTS], *[new_v[n] for n in TWIN_WEIGHTS])
```

```python
import functools

import jax
import jax.numpy as jnp
import numpy as np
from jax import lax
from jax.experimental import pallas as pl
from jax.experimental.pallas import tpu as pltpu

F32 = jnp.float32
BF16 = jnp.bfloat16

D_MODEL = 1024
D_FF = 2816
N_Q_HEADS = 16
N_KV_HEADS = 2
Q_PER_KV = N_Q_HEADS // N_KV_HEADS
HEAD_DIM = 64
BLOCK = 128
ATTN_WIDTH = N_Q_HEADS * HEAD_DIM
KV_WIDTH = N_KV_HEADS * HEAD_DIM
POOL_WINDOWS = (2, 4, 8, 16)
POOL_GROUP = 128
POOL_WIDTH = 512
HALO = 16
IN_WIDTH = ATTN_WIDTH + 2 * KV_WIDTH + POOL_WIDTH + 2 * D_MODEL
OFF_KV = ATTN_WIDTH
OFF_Z = ATTN_WIDTH + 2 * KV_WIDTH
OFF_GATE = OFF_Z + POOL_WIDTH
NORM_EPS = 1e-6
ADAM_LR = 0.001
ADAM_B1 = 0.9
ADAM_B2 = 0.999
ADAM_EPS = 1e-08
ADAM_WD = 0.01
ADAM_STEP = 10

N_DEV = 8
N_CHIP = 4
LANES = 128
FF_CHUNK = 256
VMEM_LIMIT = 56 * 1024 * 1024
MESH = pl.DeviceIdType.MESH


def _nn(a, b):
    return jnp.dot(a, b, preferred_element_type=F32)


def _nt(a, b):
    return lax.dot_general(a, b, (((1,), (1,)), ((), ())), preferred_element_type=F32)


def _tn(a, b):
    return lax.dot_general(a, b, (((0,), (0,)), ((), ())), preferred_element_type=F32)


def _params(*sem):
    return pltpu.CompilerParams(dimension_semantics=sem, vmem_limit_bytes=VMEM_LIMIT)


def _resident(shape):
    return pl.BlockSpec(shape, lambda *_: (0,) * len(shape), pipeline_mode=pl.Buffered(1))


def _rows(tm, cols):
    return pl.BlockSpec((tm, cols), lambda i: (i, 0))


def _rms_fwd(xv, g):
    r = lax.rsqrt(jnp.mean(xv * xv, axis=-1, keepdims=True) + NORM_EPS)
    return xv * r, r


def _rms_bwd(dn, xh, r, g):
    dxh = dn * g
    dx = r * (dxh - xh * jnp.mean(dxh * xh, axis=-1, keepdims=True))
    return dx, jnp.sum(dn * xh, axis=0, keepdims=True)


def _ffn_fwd(x, g, wup_t, wdown, tm):
    t, d = x.shape
    f = wdown.shape[0]

    def body(x_ref, g_ref, wup_ref, wdn_ref, h_ref, ab_ref, n_ref, act_ref):
        xv = x_ref[...]
        xh, _ = _rms_fwd(xv, g_ref[...])
        n = (xh * g_ref[...]).astype(BF16)
        n_ref[...] = n
        for c in range(f // FF_CHUNK):
            lo, hi = c * FF_CHUNK, (c + 1) * FF_CHUNK
            a = _nt(n, wup_ref[lo:hi, :])
            b = _nt(n, wup_ref[f + lo:f + hi, :])
            ab_ref[:, lo:hi] = a.astype(BF16)
            ab_ref[:, f + lo:f + hi] = b.astype(BF16)
            act_ref[:, lo:hi] = (a * jax.nn.sigmoid(a) * b).astype(BF16)
        h_ref[...] = xv + 0.5 * _nn(act_ref[...], wdn_ref[...])

    return pl.pallas_call(
        body, name="ffn_fwd", grid=(t // tm,),
        in_specs=[_rows(tm, d), _resident((1, d)), _resident((2 * f, d)), _resident((f, d))],
        out_specs=[_rows(tm, d), _rows(tm, 2 * f), _rows(tm, d)],
        out_shape=[jax.ShapeDtypeStruct((t, d), F32), jax.ShapeDtypeStruct((t, 2 * f), BF16),
                   jax.ShapeDtypeStruct((t, d), BF16)],
        scratch_shapes=[pltpu.VMEM((tm, f), BF16)],
        compiler_params=_params("parallel"),
    )(x, g, wup_t, wdown)


def _ffn_bwd(dh, x, g, ab, wup_t, wdown, tm):
    t, d = x.shape
    f = wdown.shape[0]

    def body(dh_ref, x_ref, g_ref, ab_ref, wup_ref, wdn_ref, dx_ref, dab_ref, act_ref, dhb_ref, dg_ref):
        dhv = dh_ref[...]
        dhb = dhv.astype(BF16)
        dhb_ref[...] = dhb
        for c in range(f // FF_CHUNK):
            lo, hi = c * FF_CHUNK, (c + 1) * FF_CHUNK
            dact = 0.5 * _nt(dhb, wdn_ref[lo:hi, :])
            a = ab_ref[:, lo:hi].astype(F32)
            b = ab_ref[:, f + lo:f + hi].astype(F32)
            s = jax.nn.sigmoid(a)
            sil = a * s
            act_ref[:, lo:hi] = (sil * b).astype(BF16)
            dab_ref[:, lo:hi] = (dact * b * (s * (1.0 + a * (1.0 - s)))).astype(BF16)
            dab_ref[:, f + lo:f + hi] = (dact * sil).astype(BF16)
        dn = _nn(dab_ref[...], wup_ref[...])
        xv = x_ref[...]
        xh, r = _rms_fwd(xv, g_ref[...])
        dx, dg = _rms_bwd(dn, xh, r, g_ref[...])
        dx_ref[...] = dhv + dx

        @pl.when(pl.program_id(0) == 0)
        def _():
            dg_ref[...] = jnp.zeros_like(dg_ref)

        dg_ref[...] += dg

    return pl.pallas_call(
        body, name="ffn_bwd", grid=(t // tm,),
        in_specs=[_rows(tm, d), _rows(tm, d), _resident((1, d)), _rows(tm, 2 * f), _resident((2 * f, d)),
                  _resident((f, d))],
        out_specs=[_rows(tm, d), _rows(tm, 2 * f), _rows(tm, f), _rows(tm, d), pl.BlockSpec((1, d), lambda i: (0, 0))],
        out_shape=[jax.ShapeDtypeStruct((t, d), F32), jax.ShapeDtypeStruct((t, 2 * f), BF16),
                   jax.ShapeDtypeStruct((t, f), BF16), jax.ShapeDtypeStruct((t, d), BF16),
                   jax.ShapeDtypeStruct((1, d), F32)],
        compiler_params=_params("arbitrary"),
    )(dh, x, g, ab, wup_t, wdown)


def _wgrad(lhs, rhs, scale, bm, tk, name):
    t, m = lhs.shape
    n = rhs.shape[1]
    steps = t // tk

    def body(l_ref, r_ref, o_ref):
        @pl.when(pl.program_id(1) == 0)
        def _():
            o_ref[...] = jnp.zeros_like(o_ref)

        o_ref[...] += _tn(l_ref[...], r_ref[...])
        if scale != 1.0:
            @pl.when(pl.program_id(1) == steps - 1)
            def _():
                o_ref[...] = scale * o_ref[...]

    return pl.pallas_call(
        body, name=name, grid=(m // bm, steps),
        in_specs=[pl.BlockSpec((tk, bm), lambda i, k: (k, i)), pl.BlockSpec((tk, n), lambda i, k: (k, 0))],
        out_specs=pl.BlockSpec((bm, n), lambda i, k: (i, 0)),
        out_shape=jax.ShapeDtypeStruct((m, n), F32),
        compiler_params=_params("parallel", "arbitrary"),
    )(lhs, rhs)


def _mix_in_fwd(h, g, win_t, tm):
    t, d = h.shape

    def body(h_ref, g_ref, w_ref, u_ref, q_ref, kv_ref, z_ref, gate_ref):
        xh, _ = _rms_fwd(h_ref[...], g_ref[...])
        u = (xh * g_ref[...]).astype(BF16)
        u_ref[...] = u
        q_ref[...] = _nt(u, w_ref[0:OFF_KV, :]).astype(BF16)
        kv_ref[...] = _nt(u, w_ref[OFF_KV:OFF_Z, :]).astype(BF16)
        z_ref[...] = _nt(u, w_ref[OFF_Z:OFF_GATE, :])
        gate_ref[...] = _nt(u, w_ref[OFF_GATE:IN_WIDTH, :])

    return pl.pallas_call(
        body, name="mix_in_fwd", grid=(t // tm,),
        in_specs=[_rows(tm, d), _resident((1, d)), _resident((IN_WIDTH, d))],
        out_specs=[_rows(tm, d), _rows(tm, ATTN_WIDTH), _rows(tm, 2 * KV_WIDTH), _rows(tm, POOL_WIDTH),
                   _rows(tm, 2 * D_MODEL)],
        out_shape=[jax.ShapeDtypeStruct((t, d), BF16), jax.ShapeDtypeStruct((t, ATTN_WIDTH), BF16),
                   jax.ShapeDtypeStruct((t, 2 * KV_WIDTH), BF16), jax.ShapeDtypeStruct((t, POOL_WIDTH), F32),
                   jax.ShapeDtypeStruct((t, 2 * D_MODEL), F32)],
        compiler_params=_params("parallel"),
    )(h, g, win_t)


def _attn_bias():
    heads = np.arange(1, N_Q_HEADS + 1, dtype=np.float32)
    slopes = (2.0 ** (-8.0 * heads / N_Q_HEADS)).astype(np.float32).reshape(N_KV_HEADS, Q_PER_KV)
    dist = (np.arange(BLOCK)[:, None] + BLOCK) - np.arange(2 * BLOCK)[None, :]
    valid = (dist >= 0) & (dist < BLOCK)
    bias = -slopes[:, :, None, None] * dist.astype(np.float32)[None, None]
    bias = np.where(valid[None, None], bias, -np.inf).astype(np.float32)
    return jnp.asarray(bias.reshape(N_KV_HEADS, Q_PER_KV * BLOCK, 2 * BLOCK))


def _low_half(shape):
    return lax.broadcasted_iota(jnp.int32, shape, len(shape) - 1) < HEAD_DIM


def _both_halves(band, kv_head):
    low = _low_half(band.shape)
    swapped = pltpu.roll(band, HEAD_DIM, 1)
    return jnp.where(low, band, swapped) if kv_head == 0 else jnp.where(low, swapped, band)


def _stack_heads(ref, kv_head, scale=None):
    parts = []
    for gq in range(Q_PER_KV):
        col = LANES * (kv_head * (Q_PER_KV // 2) + gq // 2)
        v = ref[:, col:col + LANES]
        if scale is not None:
            v = v * scale
        low = _low_half(v.shape)
        parts.append(jnp.where(low if gq % 2 == 0 else ~low, v, jnp.zeros_like(v)))
    return jnp.concatenate(parts, axis=0)


def _unstack_heads(stack, pair):
    even = stack[(2 * pair) * BLOCK:(2 * pair + 1) * BLOCK, :]
    odd = stack[(2 * pair + 1) * BLOCK:(2 * pair + 2) * BLOCK, :]
    return jnp.where(_low_half(even.shape), even, odd)


def _attn_probs(q_ref, kband, bias_ref, sink_ref, kv_head, first):
    kk = _both_halves(kband, kv_head)
    qs = _stack_heads(q_ref, kv_head, scale=HEAD_DIM ** -0.5)
    s = _nt(qs, kk) + bias_ref[kv_head]
    col = lax.broadcasted_iota(jnp.int32, s.shape, 1)
    s = jnp.where(jnp.logical_and(first, col < BLOCK), -jnp.inf, s)
    sink = sink_ref[kv_head]
    m = jnp.maximum(jnp.max(s, axis=-1, keepdims=True), sink)
    p = jnp.exp(s - m)
    psink = jnp.exp(sink - m)
    inv = 1.0 / (jnp.sum(p, axis=-1, keepdims=True) + psink)
    return p * inv, psink * inv, qs, kk


def _attn_fwd(q, kv, bias, sinkrows):
    t = q.shape[0]

    def body(q_ref, kvc_ref, kvp_ref, bias_ref, sink_ref, o_ref):
        first = pl.program_id(0) == 0
        kband = jnp.concatenate([kvp_ref[:, 0:LANES], kvc_ref[:, 0:LANES]], axis=0)
        vband = jnp.concatenate([kvp_ref[:, LANES:2 * LANES], kvc_ref[:, LANES:2 * LANES]], axis=0)
        for hk in range(N_KV_HEADS):
            probs, _, _, _ = _attn_probs(q_ref, kband, bias_ref, sink_ref, hk, first)
            o = _nn(probs.astype(BF16), _both_halves(vband, hk))
            for pair in range(Q_PER_KV // 2):
                col = LANES * (hk * (Q_PER_KV // 2) + pair)
                o_ref[:, col:col + LANES] = _unstack_heads(o, pair).astype(BF16)

    return pl.pallas_call(
        body, name="attn_fwd", grid=(t // BLOCK,),
        in_specs=[_rows(BLOCK, ATTN_WIDTH), _rows(BLOCK, 2 * KV_WIDTH),
                  pl.BlockSpec((BLOCK, 2 * KV_WIDTH), lambda i: (jnp.maximum(i - 1, 0), 0)),
                  _resident(bias.shape), _resident(sinkrows.shape)],
        out_specs=_rows(BLOCK, ATTN_WIDTH),
        out_shape=jax.ShapeDtypeStruct((t, ATTN_WIDTH), BF16),
        compiler_params=_params("parallel"),
    )(q, kv, kv, bias, sinkrows)


def _pool_counts(tm, width):
    row = pl.program_id(0) * tm + lax.broadcasted_iota(jnp.int32, (tm, 1), 0)
    return jnp.minimum(row + 1, width).astype(F32)


def _trailing_sums(zz, group):
    s = zz
    for k in range(group + 1):
        s = s + pltpu.roll(s, 1 << k, 0)
    return s


def _leading_sums(zz, group):
    rows = zz.shape[0]
    s = zz
    for k in range(group + 1):
        s = s + pltpu.roll(s, rows - (1 << k), 0)
    return s


def _mix_out_fwd(attn, z, gate, h, wattn, wmix, scale, wpool_t, wout, tm):
    t, d = h.shape

    def body(attn_ref, z_ref, halo_ref, gate_ref, h_ref, wattn_ref, wmix_ref, scale_ref, wpool_ref, wout_ref,
             h2_ref, a_ref, p_ref, merged_ref, ms_ref, pooled_ref):
        halo = jnp.where(pl.program_id(0) == 0, 0.0, halo_ref[...])
        for gi, width in enumerate(POOL_WINDOWS):
            lo, hi = gi * POOL_GROUP, (gi + 1) * POOL_GROUP
            zg = z_ref[:, lo:hi]
            sums = _trailing_sums(jnp.concatenate([halo[:, lo:hi], zg], axis=0), gi)[HALO:, :]
            pooled = (sums / _pool_counts(tm, width) - zg).astype(BF16)
            pooled_ref[:, lo:hi] = pooled
            ms_ref[:, lo:hi] = (_nn(pooled, wmix_ref[gi]) * scale_ref[:, lo:hi]).astype(BF16)
        p = _nt(ms_ref[...], wpool_ref[...])
        a = _nn(attn_ref[...], wattn_ref[...])
        a_ref[...] = a
        p_ref[...] = p
        merged = (jax.nn.sigmoid(gate_ref[:, 0:d]) * a + jax.nn.sigmoid(gate_ref[:, d:2 * d]) * p).astype(BF16)
        merged_ref[...] = merged
        h2_ref[...] = h_ref[...] + _nn(merged, wout_ref[...])

    halo_spec = pl.BlockSpec((HALO, POOL_WIDTH), lambda i: (jnp.maximum(i * (tm // HALO) - 1, 0), 0))
    return pl.pallas_call(
        body, name="mix_out_fwd", grid=(t // tm,),
        in_specs=[_rows(tm, ATTN_WIDTH), _rows(tm, POOL_WIDTH), halo_spec, _rows(tm, 2 * d), _rows(tm, d),
                  _resident(wattn.shape), _resident(wmix.shape), _resident(scale.shape), _resident(wpool_t.shape),
                  _resident(wout.shape)],
        out_specs=[_rows(tm, d), _rows(tm, d), _rows(tm, d), _rows(tm, d), _rows(tm, POOL_WIDTH),
                   _rows(tm, POOL_WIDTH)],
        out_shape=[jax.ShapeDtypeStruct((t, d), F32), jax.ShapeDtypeStruct((t, d), F32),
                   jax.ShapeDtypeStruct((t, d), F32), jax.ShapeDtypeStruct((t, d), BF16),
                   jax.ShapeDtypeStruct((t, POOL_WIDTH), BF16), jax.ShapeDtypeStruct((t, POOL_WIDTH), BF16)],
        compiler_params=_params("parallel"),
    )(attn, z, z, gate, h, wattn, wmix, scale, wpool_t, wout)


def _loss_head(h, g, target, tm):
    t, d = h.shape

    def body(h_ref, g_ref, tgt_ref, loss_ref, dh_ref, dg_ref):
        xh, r = _rms_fwd(h_ref[...], g_ref[...])
        err = xh * g_ref[...] - tgt_ref[...]
        part = 0.5 * jnp.sum(jnp.mean(err * err, axis=-1, keepdims=True), axis=0, keepdims=True)
        dx, dg = _rms_bwd(err * (1.0 / d), xh, r, g_ref[...])
        dh_ref[...] = dx

        @pl.when(pl.program_id(0) == 0)
        def _():
            dg_ref[...] = jnp.zeros_like(dg_ref)
            loss_ref[...] = jnp.zeros_like(loss_ref)

        dg_ref[...] += dg
        loss_ref[...] += jnp.broadcast_to(part, loss_ref.shape)

    return pl.pallas_call(
        body, name="loss_head", grid=(t // tm,),
        in_specs=[_rows(tm, d), _resident((1, d)), _rows(tm, d)],
        out_specs=[pl.BlockSpec((1, LANES), lambda i: (0, 0)), _rows(tm, d), pl.BlockSpec((1, d), lambda i: (0, 0))],
        out_shape=[jax.ShapeDtypeStruct((1, LANES), F32), jax.ShapeDtypeStruct((t, d), F32),
                   jax.ShapeDtypeStruct((1, d), F32)],
        compiler_params=_params("arbitrary"),
    )(h, g, target)


def _mix_out_bwd(dh, gate, a, p, pooled, wattn, wmix, scale, wpool_t, wout, tm):
    t, d = dh.shape

    def body(dh_ref, gate_ref, a_ref, p_ref, pooled_ref, wattn_ref, wmix_ref, scale_ref, wpool_ref, wout_ref,
             dhb_ref, dab_ref, dpb_ref, dattn_ref, dgate_ref, dpooled_ref, dwmix_ref, dscale_ref):
        @pl.when(pl.program_id(0) == 0)
        def _():
            dwmix_ref[...] = jnp.zeros_like(dwmix_ref)
            dscale_ref[...] = jnp.zeros_like(dscale_ref)

        dhb = dh_ref[...].astype(BF16)
        dhb_ref[...] = dhb
        dm = _nt(dhb, wout_ref[...])
        sa = jax.nn.sigmoid(gate_ref[:, 0:d])
        sp = jax.nn.sigmoid(gate_ref[:, d:2 * d])
        da = (dm * sa).astype(BF16)
        dp = (dm * sp).astype(BF16)
        dab_ref[...] = da
        dpb_ref[...] = dp
        dgate_ref[:, 0:d] = (dm * a_ref[...] * (sa * (1.0 - sa))).astype(BF16)
        dgate_ref[:, d:2 * d] = (dm * p_ref[...] * (sp * (1.0 - sp))).astype(BF16)
        dattn_ref[...] = _nt(da, wattn_ref[...]).astype(BF16)
        dms = _nn(dp, wpool_ref[...])
        for gi in range(len(POOL_WINDOWS)):
            lo, hi = gi * POOL_GROUP, (gi + 1) * POOL_GROUP
            pooled_g = pooled_ref[:, lo:hi]
            mixed = _nn(pooled_g, wmix_ref[gi])
            dscale_ref[:, lo:hi] += jnp.sum(dms[:, lo:hi] * mixed, axis=0, keepdims=True)
            dmixed = (dms[:, lo:hi] * scale_ref[:, lo:hi]).astype(BF16)
            dwmix_ref[gi] += _tn(pooled_g, dmixed)
            dpooled_ref[:, lo:hi] = _nt(dmixed, wmix_ref[gi])

    acc = lambda shape: pl.BlockSpec(shape, lambda i: (0,) * len(shape))
    return pl.pallas_call(
        body, name="mix_out_bwd", grid=(t // tm,),
        in_specs=[_rows(tm, d), _rows(tm, 2 * d), _rows(tm, d), _rows(tm, d), _rows(tm, POOL_WIDTH),
                  _resident(wattn.shape), _resident(wmix.shape), _resident(scale.shape), _resident(wpool_t.shape),
                  _resident(wout.shape)],
        out_specs=[_rows(tm, d), _rows(tm, d), _rows(tm, d), _rows(tm, ATTN_WIDTH), _rows(tm, 2 * d),
                   _rows(tm, POOL_WIDTH), acc(wmix.shape), acc((1, POOL_WIDTH))],
        out_shape=[jax.ShapeDtypeStruct((t, d), BF16), jax.ShapeDtypeStruct((t, d), BF16),
                   jax.ShapeDtypeStruct((t, d), BF16), jax.ShapeDtypeStruct((t, ATTN_WIDTH), BF16),
                   jax.ShapeDtypeStruct((t, 2 * d), BF16), jax.ShapeDtypeStruct((t, POOL_WIDTH), F32),
                   jax.ShapeDtypeStruct(wmix.shape, F32), jax.ShapeDtypeStruct((1, POOL_WIDTH), F32)],
        compiler_params=_params("arbitrary"),
    )(dh, gate, a, p, pooled, wattn, wmix, scale, wpool_t, wout)


def _fold_halves(x):
    return x + pltpu.roll(x, HEAD_DIM, 1)


def _attn_bwd(q, kv, dattn, bias, sinkrows):
    t = q.shape[0]

    def body(q_ref, kvc_ref, kvp_ref, do_ref, bias_ref, sink_ref, dq_ref, dkv_own_ref, dkv_prev_ref, dsink_ref):
        first = pl.program_id(0) == 0

        @pl.when(first)
        def _():
            dsink_ref[...] = jnp.zeros_like(dsink_ref)

        kband = jnp.concatenate([kvp_ref[:, 0:LANES], kvc_ref[:, 0:LANES]], axis=0)
        vband = jnp.concatenate([kvp_ref[:, LANES:2 * LANES], kvc_ref[:, LANES:2 * LANES]], axis=0)
        lane = lax.broadcasted_iota(jnp.int32, (1, LANES), 1)
        dk_heads, dv_heads = [], []
        dsink = jnp.zeros((1, LANES), F32)
        for hk in range(N_KV_HEADS):
            probs, psink, qs, kk = _attn_probs(q_ref, kband, bias_ref, sink_ref, hk, first)
            dos = _stack_heads(do_ref, hk)
            dprobs = _nt(dos, _both_halves(vband, hk))
            rowdot = jnp.sum(probs * dprobs, axis=-1, keepdims=True)
            ds = (probs * (dprobs - rowdot)).astype(BF16)
            dsink_rows = -psink * rowdot
            for gq in range(Q_PER_KV):
                head_sum = jnp.sum(dsink_rows[gq * BLOCK:(gq + 1) * BLOCK, :], axis=0, keepdims=True)
                dsink = dsink + jnp.where(lane == hk * Q_PER_KV + gq, head_sum, 0.0)
            dv_heads.append(_fold_halves(_tn(probs.astype(BF16), dos)))
            dk_heads.append(_fold_halves(_tn(ds, qs)))
            dqs = _nn(ds, kk)
            for pair in range(Q_PER_KV // 2):
                col = LANES * (hk * (Q_PER_KV // 2) + pair)
                dq_ref[:, col:col + LANES] = (_unstack_heads(dqs, pair) * HEAD_DIM ** -0.5).astype(BF16)
        low = _low_half(dk_heads[0].shape)
        dkv = jnp.concatenate([jnp.where(low, dk_heads[0], dk_heads[1]), jnp.where(low, dv_heads[0], dv_heads[1])],
                              axis=1)
        dkv_prev_ref[...] = dkv[0:BLOCK, :]
        dkv_own_ref[...] = dkv[BLOCK:2 * BLOCK, :]
        dsink_ref[...] += dsink

    return pl.pallas_call(
        body, name="attn_bwd", grid=(t // BLOCK,),
        in_specs=[_rows(BLOCK, ATTN_WIDTH), _rows(BLOCK, 2 * KV_WIDTH),
                  pl.BlockSpec((BLOCK, 2 * KV_WIDTH), lambda i: (jnp.maximum(i - 1, 0), 0)),
                  _rows(BLOCK, ATTN_WIDTH), _resident(bias.shape), _resident(sinkrows.shape)],
        out_specs=[_rows(BLOCK, ATTN_WIDTH), _rows(BLOCK, 2 * KV_WIDTH), _rows(BLOCK, 2 * KV_WIDTH),
                   pl.BlockSpec((1, LANES), lambda i: (0, 0))],
        out_shape=[jax.ShapeDtypeStruct((t, ATTN_WIDTH), BF16), jax.ShapeDtypeStruct((t, 2 * KV_WIDTH), F32),
                   jax.ShapeDtypeStruct((t, 2 * KV_WIDTH), F32), jax.ShapeDtypeStruct((1, LANES), F32)],
        compiler_params=_params("arbitrary"),
    )(q, kv, kv, dattn, bias, sinkrows)


def _mix_in_bwd(dq, dkv_own, dkv_prev, dpooled, dgate, h, g, win_t, dh_res, tm):
    t, d = h.shape
    nt = t // tm

    def body(dq_ref, own_ref, prev_ref, prev_next_ref, dpool_ref, halo_ref, dgate_ref, h_ref, g_ref, w_ref, res_ref,
             dproj_ref, dh_ref, dg_ref):
        i = pl.program_id(0)
        last = i == nt - 1
        dproj_ref[:, 0:OFF_KV] = dq_ref[...]
        from_next = jnp.where(last, 0.0, prev_next_ref[...])
        if tm > BLOCK:
            from_next = jnp.concatenate([prev_ref[BLOCK:tm, :], from_next], axis=0)
        dproj_ref[:, OFF_KV:OFF_Z] = (own_ref[...] + from_next).astype(BF16)
        halo = jnp.where(last, 0.0, halo_ref[...])
        for gi, width in enumerate(POOL_WINDOWS):
            lo, hi = gi * POOL_GROUP, (gi + 1) * POOL_GROUP
            dpg = dpool_ref[:, lo:hi]
            scaled = jnp.concatenate([dpg / _pool_counts(tm, width), halo[:, lo:hi] / float(width)], axis=0)
            dz = _leading_sums(scaled, gi)[0:tm, :] - dpg
            dproj_ref[:, OFF_Z + lo:OFF_Z + hi] = dz.astype(BF16)
        dproj_ref[:, OFF_GATE:IN_WIDTH] = dgate_ref[...]
        du = _nn(dproj_ref[...], w_ref[...])
        xh, r = _rms_fwd(h_ref[...], g_ref[...])
        dx, dg = _rms_bwd(du, xh, r, g_ref[...])
        dh_ref[...] = res_ref[...] + dx

        @pl.when(i == 0)
        def _():
            dg_ref[...] = jnp.zeros_like(dg_ref)

        dg_ref[...] += dg

    per = tm // BLOCK
    next_block = pl.BlockSpec((BLOCK, 2 * KV_WIDTH), lambda i: (jnp.minimum((i + 1) * per, t // BLOCK - 1), 0))
    next_halo = pl.BlockSpec((HALO, POOL_WIDTH), lambda i: (jnp.minimum((i + 1) * (tm // HALO), t // HALO - 1), 0))
    return pl.pallas_call(
        body, name="mix_in_bwd", grid=(nt,),
        in_specs=[_rows(tm, ATTN_WIDTH), _rows(tm, 2 * KV_WIDTH), _rows(tm, 2 * KV_WIDTH), next_block,
                  _rows(tm, POOL_WIDTH), next_halo, _rows(tm, 2 * d), _rows(tm, d), _resident((1, d)),
                  _resident((IN_WIDTH, d)), _rows(tm, d)],
        out_specs=[_rows(tm, IN_WIDTH), _rows(tm, d), pl.BlockSpec((1, d), lambda i: (0, 0))],
        out_shape=[jax.ShapeDtypeStruct((t, IN_WIDTH), BF16), jax.ShapeDtypeStruct((t, d), F32),
                   jax.ShapeDtypeStruct((1, d), F32)],
        compiler_params=_params("arbitrary"),
    )(dq, dkv_own, dkv_prev, dkv_prev, dpooled, dpooled, dgate, h, g, win_t, dh_res)


def _tile(t, want):
    return min(want, t)


def _local_step(x, target, norms, sinks, wmix, scale, big):
    t = x.shape[0]
    tm_f, tm_b = _tile(t, 512), _tile(t, 256)
    g1, gm, g2, gf = norms
    bias = _attn_bias()
    sinkrows = jnp.repeat(sinks.reshape(N_KV_HEADS, Q_PER_KV, 1), BLOCK, axis=1).reshape(
        N_KV_HEADS, Q_PER_KV * BLOCK, 1)
    wmix_b = wmix.astype(BF16)

    h1, ab1, n1 = _ffn_fwd(x, g1, big["wup1_t"], big["wdown1"], tm_f)
    u, q, kv, z, gate = _mix_in_fwd(h1, gm, big["win_t"], tm_f)
    attn = _attn_fwd(q, kv, bias, sinkrows)
    h2, a, p, merged, ms, pooled = _mix_out_fwd(attn, z, gate, h1, big["wattn"], wmix_b, scale, big["wpool_t"],
                                                big["wout"], tm_b)
    h3, ab2, n2 = _ffn_fwd(h2, g2, big["wup2_t"], big["wdown2"], tm_f)
    loss, dh3, dgf = _loss_head(h3, gf, target, tm_f)

    dh2, dab2, act2, dhb3, dg2 = _ffn_bwd(dh3, h2, g2, ab2, big["wup2_t"], big["wdown2"], tm_b)
    dhb2, da_b, dp_b, dattn, dgate, dpooled, dwmix, dscale = _mix_out_bwd(
        dh2, gate, a, p, pooled, big["wattn"], wmix_b, scale, big["wpool_t"], big["wout"], tm_b)
    dq, dkv_own, dkv_prev, dsinks = _attn_bwd(q, kv, dattn, bias, sinkrows)
    dproj, dh1, dgm = _mix_in_bwd(dq, dkv_own, dkv_prev, dpooled, dgate, h1, gm, big["win_t"], dh2, tm_b)
    dx, dab1, act1, dhb1, dg1 = _ffn_bwd(dh1, x, g1, ab1, big["wup1_t"], big["wdown1"], tm_b)

    tk = _tile(t, 1024)
    grads = {
        "wup1_t": _wgrad(dab1, n1, 1.0, 512, tk, "wgrad_up1"),
        "wdown1": _wgrad(act1, dhb1, 0.5, 256, tk, "wgrad_down1"),
        "win_t": _wgrad(dproj, u, 1.0, 256, tk, "wgrad_in"),
        "wattn": _wgrad(attn, da_b, 1.0, 256, tk, "wgrad_attn"),
        "wpool_t": _wgrad(dp_b, ms, 1.0, 256, tk, "wgrad_pool"),
        "wout": _wgrad(merged, dhb2, 1.0, 256, tk, "wgrad_out"),
        "wup2_t": _wgrad(dab2, n2, 1.0, 512, tk, "wgrad_up2"),
        "wdown2": _wgrad(act2, dhb3, 0.5, 256, tk, "wgrad_down2"),
    }
    small = {"g1": dg1, "gm": dgm, "g2": dg2, "gf": dgf, "sinks": dsinks, "wmix": dwmix, "scale": dscale}
    return loss, dx, grads, small


BIG = (("wup1_t", "ffn1_w_up", True), ("wdown1", "ffn1_w_down", False), ("win_t", "w_in", True),
       ("wattn", "w_attn_up", False), ("wpool_t", "w_pool_up", True), ("wout", "w_out", False),
       ("wup2_t", "ffn2_w_up", True), ("wdown2", "ffn2_w_down", False))
ANY = pl.BlockSpec(memory_space=pl.ANY)
WIRE = BF16


def _place():
    return lax.axis_index("x"), lax.axis_index("y"), lax.axis_index("c")


def _all_gather(shards):
    n = len(shards)

    def body(*refs):
        ins, outs = refs[:n], refs[n:2 * n]
        send_sems, recv_sems, local_sems = refs[2 * n:]
        x, y, c = _place()
        me, sibling = (x, y, c), (x, y, 1 - c)
        chips = [(1 - x, y), (x, 1 - y), (1 - x, 1 - y)]

        def rows(w, px, py, pc):
            r = ins[w].shape[0]
            return outs[w].at[pl.ds((4 * px + 2 * py + pc) * r, r), :]

        def copy(w, k, block, to, src=None):
            return pltpu.make_async_remote_copy(
                src_ref=rows(w, *block) if src is None else src, dst_ref=rows(w, *block),
                send_sem=send_sems.at[w, k], recv_sem=recv_sems.at[w, k], device_id=to, device_id_type=MESH)

        started = []
        mine = [pltpu.make_async_copy(ins[w], rows(w, *me), local_sems.at[w]) for w in range(n)]
        for w in range(n):
            mine[w].start()
            first = [copy(w, 0, me, sibling, src=ins[w])]
            first += [copy(w, 1 + j, me, (*chip, c), src=ins[w]) for j, chip in enumerate(chips)]
            for cp in first:
                cp.start()
            started += first
        for w in range(n):
            for j, chip in enumerate(chips):
                copy(w, 1 + j, (*chip, c), me).wait_recv()
                passed = copy(w, 4 + j, (*chip, c), sibling)
                passed.start()
                started.append(passed)
        for w in range(n):
            copy(w, 0, sibling, me).wait_recv()
            for j, chip in enumerate(chips):
                copy(w, 4 + j, (*chip, 1 - c), me).wait_recv()
        for cp in started:
            cp.wait_send()
        for w in range(n):
            mine[w].wait()

    return pl.pallas_call(
        body, name="all_gather_weights", in_specs=[ANY] * n, out_specs=[ANY] * n,
        out_shape=[jax.ShapeDtypeStruct((N_DEV * s.shape[0], s.shape[1]), s.dtype) for s in shards],
        scratch_shapes=[pltpu.SemaphoreType.DMA((n, N_DEV - 1)), pltpu.SemaphoreType.DMA((n, N_DEV - 1)),
                        pltpu.SemaphoreType.DMA((n,))],
    )(*shards)


def _exchange_siblings(grads, small):
    n = len(grads)
    srows = small.shape[0]

    def body(*refs):
        ins, small_ref = refs[:n], refs[n]
        outs, small_all = refs[n + 1:2 * n + 1], refs[2 * n + 1]
        send_sems, recv_sems, small_send, small_recv, local_sem = refs[2 * n + 2:]
        x, y, c = _place()
        sibling = (x, y, 1 - c)
        for w in range(n):
            r = outs[w].shape[0] // N_CHIP
            for chip in range(N_CHIP):
                pltpu.make_async_remote_copy(
                    src_ref=ins[w].at[pl.ds((2 * chip + 1 - c) * r, r), :], dst_ref=outs[w].at[pl.ds(chip * r, r), :],
                    send_sem=send_sems.at[w], recv_sem=recv_sems.at[w], device_id=sibling, device_id_type=MESH).start()

        def slot(px, py, pc):
            return small_all.at[pl.ds((4 * px + 2 * py + pc) * srows, srows), :]

        def peer(k):
            return x ^ (k >> 2), y ^ ((k >> 1) & 1), c ^ (k & 1)

        def small_copy(k, landing):
            return pltpu.make_async_remote_copy(
                src_ref=small_ref, dst_ref=landing, send_sem=small_send.at[k - 1], recv_sem=small_recv.at[k - 1],
                device_id=peer(k), device_id_type=MESH)

        mine = pltpu.make_async_copy(small_ref, slot(x, y, c), local_sem)
        mine.start()
        for k in range(1, N_DEV):
            small_copy(k, slot(x, y, c)).start()
        for w in range(n):
            whole = outs[w]
            pltpu.make_async_remote_copy(
                src_ref=ins[w].at[pl.ds(0, whole.shape[0]), :], dst_ref=whole, send_sem=send_sems.at[w],
                recv_sem=recv_sems.at[w], device_id=sibling, device_id_type=MESH).wait()
        for k in range(1, N_DEV):
            small_copy(k, slot(*peer(k))).wait()
        mine.wait()

    return pl.pallas_call(
        body, name="exchange_siblings", in_specs=[ANY] * (n + 1), out_specs=[ANY] * (n + 1),
        out_shape=[jax.ShapeDtypeStruct((g.shape[0] // 2, g.shape[1]), g.dtype) for g in grads]
        + [jax.ShapeDtypeStruct((N_DEV * srows, LANES), small.dtype)],
        scratch_shapes=[pltpu.SemaphoreType.DMA((n,)), pltpu.SemaphoreType.DMA((n,)),
                        pltpu.SemaphoreType.DMA((N_DEV - 1,)), pltpu.SemaphoreType.DMA((N_DEV - 1,)),
                        pltpu.SemaphoreType.DMA],
    )(*grads, small)


def _pair_sum(grad, got, core, name):
    r, cols = got.shape[0] // N_CHIP, got.shape[1]

    def body(core_ref, g_ref, s_ref, o_ref):
        del core_ref
        o_ref[...] = (g_ref[...] + s_ref[...]).astype(o_ref.dtype)

    return pl.pallas_call(
        body, name=name,
        grid_spec=pltpu.PrefetchScalarGridSpec(
            num_scalar_prefetch=1, grid=(N_CHIP,),
            in_specs=[pl.BlockSpec((None, None, r, cols), lambda q, core_ref: (q, core_ref[0], 0, 0)),
                      pl.BlockSpec((None, r, cols), lambda q, core_ref: (q, 0, 0))],
            out_specs=pl.BlockSpec((None, r, cols), lambda q, core_ref: (q, 0, 0))),
        out_shape=jax.ShapeDtypeStruct((N_CHIP, r, cols), WIRE),
        compiler_params=_params("parallel"),
    )(core, grad.reshape(N_CHIP, 2, r, cols), got.reshape(N_CHIP, r, cols))


def _exchange_chips(parts):
    n = len(parts)

    def body(*refs):
        ins, outs = refs[:n], refs[n:2 * n]
        send_sems, recv_sems, local_sems = refs[2 * n:]
        x, y, c = _place()
        my_chip = 2 * x + y
        peers = [(1 - x, y), (x, 1 - y), (1 - x, 1 - y)]
        mine = [pltpu.make_async_copy(ins[w].at[my_chip], outs[w].at[my_chip], local_sems.at[w]) for w in range(n)]

        def copy(w, k):
            px, py = peers[k]
            return pltpu.make_async_remote_copy(
                src_ref=ins[w].at[2 * px + py], dst_ref=outs[w].at[my_chip], send_sem=send_sems.at[w, k],
                recv_sem=recv_sems.at[w, k], device_id=(px, py, c), device_id_type=MESH)

        def arrival(w, k):
            px, py = peers[k]
            return pltpu.make_async_remote_copy(
                src_ref=ins[w].at[my_chip], dst_ref=outs[w].at[2 * px + py], send_sem=send_sems.at[w, k],
                recv_sem=recv_sems.at[w, k], device_id=(px, py, c), device_id_type=MESH)

        for w in range(n):
            mine[w].start()
            for k in range(len(peers)):
                copy(w, k).start()
        for w in range(n):
            for k in range(len(peers)):
                arrival(w, k).wait_recv()
                copy(w, k).wait_send()
            mine[w].wait()

    return pl.pallas_call(
        body, name="exchange_chips", in_specs=[ANY] * n, out_specs=[ANY] * n,
        out_shape=[jax.ShapeDtypeStruct(p.shape, p.dtype) for p in parts],
        scratch_shapes=[pltpu.SemaphoreType.DMA((n, N_CHIP - 1)), pltpu.SemaphoreType.DMA((n, N_CHIP - 1)),
                        pltpu.SemaphoreType.DMA((n,))],
    )(*parts)


def _chip_sum(got, name):
    _, r, cols = got.shape
    tr = r if r <= 512 else r // 2

    def body(g_ref, o_ref):
        acc = g_ref[0].astype(F32)
        for q in range(1, N_CHIP):
            acc = acc + g_ref[q].astype(F32)
        o_ref[...] = acc

    return pl.pallas_call(
        body, name=name, grid=(r // tr,),
        in_specs=[pl.BlockSpec((N_CHIP, tr, cols), lambda i: (0, i, 0))],
        out_specs=pl.BlockSpec((tr, cols), lambda i: (i, 0)),
        out_shape=jax.ShapeDtypeStruct((r, cols), F32),
        compiler_params=_params("parallel"),
    )(got)


def _adamw_math(w, g, m, v):
    m = ADAM_B1 * m + (1.0 - ADAM_B1) * g
    v = ADAM_B2 * v + (1.0 - ADAM_B2) * (g * g)
    m_hat = m / (1.0 - ADAM_B1 ** ADAM_STEP)
    v_hat = v / (1.0 - ADAM_B2 ** ADAM_STEP)
    return -ADAM_LR * (m_hat / (jnp.sqrt(v_hat) + ADAM_EPS) + ADAM_WD * w), m, v


def _adamw(w, g, m, v, name):
    r, cols = w.shape
    tr = r if r <= 512 else 256

    def body(w_ref, g_ref, m_ref, v_ref, d_ref, m2_ref, v2_ref):
        d_ref[...], m2_ref[...], v2_ref[...] = _adamw_math(w_ref[...], g_ref[...], m_ref[...], v_ref[...])

    spec = pl.BlockSpec((tr, cols), lambda i: (i, 0))
    return pl.pallas_call(
        body, name=name, grid=(r // tr,), in_specs=[spec] * 4, out_specs=[spec] * 3,
        out_shape=[jax.ShapeDtypeStruct((r, cols), F32)] * 3, compiler_params=_params("parallel"),
    )(w, g, m, v)


def _small_update(gathered, w, m, v):
    rows = w.shape[0]

    def body(all_ref, w_ref, m_ref, v_ref, g_ref, d_ref, m2_ref, v2_ref):
        g = all_ref[0]
        for dev in range(1, N_DEV):
            g = g + all_ref[dev]
        g_ref[...] = g
        d_ref[...], m2_ref[...], v2_ref[...] = _adamw_math(w_ref[...], g, m_ref[...], v_ref[...])

    return pl.pallas_call(
        body, name="small_update", out_shape=[jax.ShapeDtypeStruct((rows, LANES), F32)] * 4,
        compiler_params=pltpu.CompilerParams(vmem_limit_bytes=VMEM_LIMIT),
    )(gathered.reshape(N_DEV, rows, LANES), w, m, v)


SMALL = (("pool_w_mix", 512), ("ffn1_norm", 8), ("mix_norm", 8), ("ffn2_norm", 8), ("final_norm", 8),
         ("pool_scale", 8), ("sinks", 8), ("loss", 8))
SMALL_ROWS = sum(rows for _, rows in SMALL)


def _pack_small(parts):
    out = []
    for name, rows in SMALL:
        flat = parts[name].astype(F32).reshape(-1)
        out.append(jnp.pad(flat, (0, rows * LANES - flat.shape[0])).reshape(rows, LANES))
    return jnp.concatenate(out, axis=0)


def _unpack_small(packed, shapes):
    out, row = {}, 0
    for name, rows in SMALL:
        shape = shapes[name]
        size = int(np.prod(shape)) if shape else 1
        out[name] = packed[row:row + rows].reshape(-1)[:size].reshape(shape)
        row += rows
    return out


def kernel(x, ffn1_norm, ffn1_w_up, ffn1_w_down, mix_norm, w_in, sinks, w_attn_up, pool_w_mix, pool_scale, w_pool_up, w_out, ffn2_norm, ffn2_w_up, ffn2_w_down, final_norm, loss_target, m_ffn1_norm, m_ffn1_w_up, m_ffn1_w_down, m_mix_norm, m_w_in, m_sinks, m_w_attn_up, m_pool_w_mix, m_pool_scale, m_w_pool_up, m_w_out, m_ffn2_norm, m_ffn2_w_up, m_ffn2_w_down, m_final_norm, v_ffn1_norm, v_ffn1_w_up, v_ffn1_w_down, v_mix_norm, v_w_in, v_sinks, v_w_attn_up, v_pool_w_mix, v_pool_scale, v_w_pool_up, v_w_out, v_ffn2_norm, v_ffn2_w_up, v_ffn2_w_down, v_final_norm):
    args = dict(locals())
    weight_names = ("ffn1_norm", "ffn1_w_up", "ffn1_w_down", "mix_norm", "w_in", "sinks", "w_attn_up", "pool_w_mix",
                    "pool_scale", "w_pool_up", "w_out", "ffn2_norm", "ffn2_w_up", "ffn2_w_down", "final_norm")

    shards = [(args[p][0].T if tr else args[p][0]).astype(BF16) for _, p, tr in BIG]
    gathered = dict(zip((k for k, _, _ in BIG), _all_gather(shards)))

    norms = (ffn1_norm, mix_norm, ffn2_norm, final_norm.reshape(1, D_MODEL))
    loss_lanes, dx, grads, small = _local_step(x[0], loss_target[0], norms, sinks, pool_w_mix[0], pool_scale, gathered)

    small_parts = {"pool_w_mix": small["wmix"], "ffn1_norm": small["g1"], "mix_norm": small["gm"],
                   "ffn2_norm": small["g2"], "final_norm": small["gf"], "pool_scale": small["scale"],
                   "sinks": small["sinks"][:, :N_Q_HEADS], "loss": loss_lanes[:, :1]}
    full = [grads[k] for k, _, _ in BIG]
    *from_sibling, small_all = _exchange_siblings(full, _pack_small(small_parts))
    core = lax.axis_index("c").astype(jnp.int32).reshape(1)
    parts = [_pair_sum(g, s, core, "pair_sum_" + k) for g, s, (k, _, _) in zip(full, from_sibling, BIG)]
    from_chips = _exchange_chips(parts)

    grad, delta, new_m, new_v = {}, {}, {}, {}
    for got, (k, p, tr) in zip(from_chips, BIG):
        g = _chip_sum(got, "chip_sum_" + k)
        g = g.T if tr else g
        d, m2, v2 = _adamw(args[p][0], g, args["m_" + p][0], args["v_" + p][0], "adamw_" + k)
        grad[p], delta[p], new_m[p], new_v[p] = g[None], d[None], m2[None], v2[None]

    shapes = {name: args[name].shape for name, _ in SMALL if name != "loss"}
    shapes["loss"] = ()
    packed = {pre: _pack_small({**{name: args[pre + name] for name, _ in SMALL if name != "loss"},
                                "loss": jnp.zeros((), F32)}) for pre in ("", "m_", "v_")}
    g_s, d_s, m_s, v_s = _small_update(small_all, packed[""], packed["m_"], packed["v_"])
    g_small, d_small, m_small, v_small = (_unpack_small(a, shapes) for a in (g_s, d_s, m_s, v_s))
    for name, _ in SMALL:
        if name != "loss":
            grad[name], delta[name], new_m[name], new_v[name] = (
                g_small[name], d_small[name], m_small[name], v_small[name])

    return (g_small["loss"], dx[None], *[grad[n] for n in weight_names], *[delta[n] for n in weight_names],
            *[new_m[n] for n in weight_names], *[new_v[n] for n in weight_names])
```

```python
import functools

import jax
import jax.numpy as jnp
import numpy as np
from jax import lax
from jax.experimental import pallas as pl
from jax.experimental.pallas import tpu as pltpu

F32 = jnp.float32
BF16 = jnp.bfloat16

D_MODEL = 1024
D_FF = 2816
N_Q_HEADS = 16
N_KV_HEADS = 2
Q_PER_KV = N_Q_HEADS // N_KV_HEADS
HEAD_DIM = 64
BLOCK = 128
ATTN_WIDTH = N_Q_HEADS * HEAD_DIM
KV_WIDTH = N_KV_HEADS * HEAD_DIM
POOL_WINDOWS = (2, 4, 8, 16)
POOL_GROUP = 128
POOL_WIDTH = 512
HALO = 16
IN_WIDTH = ATTN_WIDTH + 2 * KV_WIDTH + POOL_WIDTH + 2 * D_MODEL
OFF_KV = ATTN_WIDTH
OFF_Z = ATTN_WIDTH + 2 * KV_WIDTH
OFF_GATE = OFF_Z + POOL_WIDTH
NORM_EPS = 1e-6
ADAM_LR = 0.001
ADAM_B1 = 0.9
ADAM_B2 = 0.999
ADAM_EPS = 1e-08
ADAM_WD = 0.01
ADAM_STEP = 10

N_DEV = 8
N_CHIP = 4
LANES = 128
FF_CHUNK = 256
VMEM_LIMIT = 56 * 1024 * 1024
MESH = pl.DeviceIdType.MESH


def _nn(a, b):
    return jnp.dot(a, b, preferred_element_type=F32)


def _nt(a, b):
    return lax.dot_general(a, b, (((1,), (1,)), ((), ())), preferred_element_type=F32)


def _tn(a, b):
    return lax.dot_general(a, b, (((0,), (0,)), ((), ())), preferred_element_type=F32)


def _params(*sem):
    return pltpu.CompilerParams(dimension_semantics=sem, vmem_limit_bytes=VMEM_LIMIT)


def _resident(shape):
    return pl.BlockSpec(shape, lambda *_: (0,) * len(shape), pipeline_mode=pl.Buffered(1))


def _rows(tm, cols):
    return pl.BlockSpec((tm, cols), lambda i: (i, 0))


def _rms_fwd(xv, g):
    r = lax.rsqrt(jnp.mean(xv * xv, axis=-1, keepdims=True) + NORM_EPS)
    return xv * r, r


def _rms_bwd(dn, xh, r, g):
    dxh = dn * g
    dx = r * (dxh - xh * jnp.mean(dxh * xh, axis=-1, keepdims=True))
    return dx, jnp.sum(dn * xh, axis=0, keepdims=True)


def _ffn_fwd(x, g, wup_t, wdown, tm):
    t, d = x.shape
    f = wdown.shape[0]

    def body(x_ref, g_ref, wup_ref, wdn_ref, h_ref, ab_ref, n_ref, act_ref):
        xv = x_ref[...]
        xh, _ = _rms_fwd(xv, g_ref[...])
        n = (xh * g_ref[...]).astype(BF16)
        n_ref[...] = n
        for c in range(f // FF_CHUNK):
            lo, hi = c * FF_CHUNK, (c + 1) * FF_CHUNK
            a = _nt(n, wup_ref[lo:hi, :])
            b = _nt(n, wup_ref[f + lo:f + hi, :])
            ab_ref[:, lo:hi] = a.astype(BF16)
            ab_ref[:, f + lo:f + hi] = b.astype(BF16)
            act_ref[:, lo:hi] = (a * jax.nn.sigmoid(a) * b).astype(BF16)
        h_ref[...] = xv + 0.5 * _nn(act_ref[...], wdn_ref[...])

    return pl.pallas_call(
        body, name="ffn_fwd", grid=(t // tm,),
        in_specs=[_rows(tm, d), _resident((1, d)), _resident((2 * f, d)), _resident((f, d))],
        out_specs=[_rows(tm, d), _rows(tm, 2 * f), _rows(tm, d)],
        out_shape=[jax.ShapeDtypeStruct((t, d), F32), jax.ShapeDtypeStruct((t, 2 * f), BF16),
                   jax.ShapeDtypeStruct((t, d), BF16)],
        scratch_shapes=[pltpu.VMEM((tm, f), BF16)],
        compiler_params=_params("parallel"),
    )(x, g, wup_t, wdown)


def _ffn_bwd(dh, x, g, ab, wup_t, wdown, tm):
    t, d = x.shape
    f = wdown.shape[0]

    def body(dh_ref, x_ref, g_ref, ab_ref, wup_ref, wdn_ref, dx_ref, dab_ref, act_ref, dhb_ref, dg_ref):
        dhv = dh_ref[...]
        dhb = dhv.astype(BF16)
        dhb_ref[...] = dhb
        for c in range(f // FF_CHUNK):
            lo, hi = c * FF_CHUNK, (c + 1) * FF_CHUNK
            dact = 0.5 * _nt(dhb, wdn_ref[lo:hi, :])
            a = ab_ref[:, lo:hi].astype(F32)
            b = ab_ref[:, f + lo:f + hi].astype(F32)
            s = jax.nn.sigmoid(a)
            sil = a * s
            act_ref[:, lo:hi] = (sil * b).astype(BF16)
            dab_ref[:, lo:hi] = (dact * b * (s * (1.0 + a * (1.0 - s)))).astype(BF16)
            dab_ref[:, f + lo:f + hi] = (dact * sil).astype(BF16)
        dn = _nn(dab_ref[...], wup_ref[...])
        xv = x_ref[...]
        xh, r = _rms_fwd(xv, g_ref[...])
        dx, dg = _rms_bwd(dn, xh, r, g_ref[...])
        dx_ref[...] = dhv + dx

        @pl.when(pl.program_id(0) == 0)
        def _():
            dg_ref[...] = jnp.zeros_like(dg_ref)

        dg_ref[...] += dg

    return pl.pallas_call(
        body, name="ffn_bwd", grid=(t // tm,),
        in_specs=[_rows(tm, d), _rows(tm, d), _resident((1, d)), _rows(tm, 2 * f), _resident((2 * f, d)),
                  _resident((f, d))],
        out_specs=[_rows(tm, d), _rows(tm, 2 * f), _rows(tm, f), _rows(tm, d), pl.BlockSpec((1, d), lambda i: (0, 0))],
        out_shape=[jax.ShapeDtypeStruct((t, d), F32), jax.ShapeDtypeStruct((t, 2 * f), BF16),
                   jax.ShapeDtypeStruct((t, f), BF16), jax.ShapeDtypeStruct((t, d), BF16),
                   jax.ShapeDtypeStruct((1, d), F32)],
        compiler_params=_params("arbitrary"),
    )(dh, x, g, ab, wup_t, wdown)


def _wgrad(lhs, rhs, scale, bm, tk, name):
    t, m = lhs.shape
    n = rhs.shape[1]
    steps = t // tk
    chunk = bm if bm <= 2048 else bm // 2

    def body(l_ref, r_ref, o_ref):
        @pl.when(pl.program_id(1) == 0)
        def _():
            o_ref[...] = jnp.zeros_like(o_ref)

        for lo in range(0, bm, chunk):
            o_ref[lo:lo + chunk, :] += _tn(l_ref[:, lo:lo + chunk], r_ref[...])
        if scale != 1.0:
            @pl.when(pl.program_id(1) == steps - 1)
            def _():
                o_ref[...] = scale * o_ref[...]

    return pl.pallas_call(
        body, name=name, grid=(m // bm, steps),
        in_specs=[pl.BlockSpec((tk, bm), lambda i, k: (k, i)), pl.BlockSpec((tk, n), lambda i, k: (k, 0))],
        out_specs=pl.BlockSpec((bm, n), lambda i, k: (i, 0)),
        out_shape=jax.ShapeDtypeStruct((m, n), F32),
        compiler_params=_params("parallel", "arbitrary"),
    )(lhs, rhs)


def _mix_in_fwd(h, g, win_t, tm):
    t, d = h.shape

    def body(h_ref, g_ref, w_ref, u_ref, q_ref, kv_ref, z_ref, gate_ref):
        xh, _ = _rms_fwd(h_ref[...], g_ref[...])
        u = (xh * g_ref[...]).astype(BF16)
        u_ref[...] = u
        q_ref[...] = _nt(u, w_ref[0:OFF_KV, :]).astype(BF16)
        kv_ref[...] = _nt(u, w_ref[OFF_KV:OFF_Z, :]).astype(BF16)
        z_ref[...] = _nt(u, w_ref[OFF_Z:OFF_GATE, :])
        gate_ref[...] = _nt(u, w_ref[OFF_GATE:IN_WIDTH, :])

    return pl.pallas_call(
        body, name="mix_in_fwd", grid=(t // tm,),
        in_specs=[_rows(tm, d), _resident((1, d)), _resident((IN_WIDTH, d))],
        out_specs=[_rows(tm, d), _rows(tm, ATTN_WIDTH), _rows(tm, 2 * KV_WIDTH), _rows(tm, POOL_WIDTH),
                   _rows(tm, 2 * D_MODEL)],
        out_shape=[jax.ShapeDtypeStruct((t, d), BF16), jax.ShapeDtypeStruct((t, ATTN_WIDTH), BF16),
                   jax.ShapeDtypeStruct((t, 2 * KV_WIDTH), BF16), jax.ShapeDtypeStruct((t, POOL_WIDTH), F32),
                   jax.ShapeDtypeStruct((t, 2 * D_MODEL), F32)],
        compiler_params=_params("parallel"),
    )(h, g, win_t)


ALIBI_SLOPES = tuple(float(s) for s in (2.0 ** (-8.0 * np.arange(1, N_Q_HEADS + 1, dtype=np.float32) / N_Q_HEADS)))


def _attn_dist():
    return jnp.asarray(((np.arange(BLOCK)[:, None] - np.arange(BLOCK)[None, :]) % BLOCK).astype(np.float32))


def _own_block():
    shape = (BLOCK, BLOCK)
    return lax.broadcasted_iota(jnp.int32, shape, 1) <= lax.broadcasted_iota(jnp.int32, shape, 0)


def _fold(band2, own):
    return jnp.where(own, band2[:, BLOCK:], band2[:, :BLOCK])


def _unfold(x, own):
    zero = jnp.zeros_like(x)
    return jnp.concatenate([jnp.where(own, zero, x), jnp.where(own, x, zero)], axis=1)


def _low_half(shape):
    return lax.broadcasted_iota(jnp.int32, shape, len(shape) - 1) < HEAD_DIM


def _both_halves(band, kv_head):
    low = _low_half(band.shape)
    swapped = pltpu.roll(band, HEAD_DIM, 1)
    return jnp.where(low, band, swapped) if kv_head == 0 else jnp.where(low, swapped, band)


def _own_half(ref, head):
    v = ref[:, LANES * (head // 2):LANES * (head // 2 + 1)]
    low = _low_half(v.shape)
    return jnp.where(low if head % 2 == 0 else jnp.logical_not(low), v, jnp.zeros_like(v))


def _head_scores(q_ref, kk, dist, head, first, own):
    s2 = _nt(_own_half(q_ref, head) * HEAD_DIM ** -0.5, kk)
    before = jnp.where(first, -jnp.inf, s2[:, :BLOCK])
    return jnp.where(own, s2[:, BLOCK:], before) - ALIBI_SLOPES[head] * dist


def _softmax_sink(s, sink):
    m = jnp.maximum(jnp.max(s, axis=-1, keepdims=True), sink)
    p = jnp.exp(s - m)
    psink = jnp.exp(sink - m)
    inv = 1.0 / (jnp.sum(p, axis=-1, keepdims=True) + psink)
    return p * inv, psink * inv


def _bands(kvc_ref, kvp_ref):
    kband = jnp.concatenate([kvp_ref[:, 0:LANES], kvc_ref[:, 0:LANES]], axis=0)
    vband = jnp.concatenate([kvp_ref[:, LANES:2 * LANES], kvc_ref[:, LANES:2 * LANES]], axis=0)
    return ([_both_halves(kband, hk) for hk in range(N_KV_HEADS)],
            [_both_halves(vband, hk) for hk in range(N_KV_HEADS)])


SMEM = pl.BlockSpec(memory_space=pltpu.SMEM)
HEADS = range(N_Q_HEADS)


def _attn_fwd(q, kv, dist, sinks):
    t = q.shape[0]

    def body(q_ref, kvc_ref, kvp_ref, dist_ref, sink_ref, o_ref, s_scr, p_scr):
        first = pl.program_id(0) == 0
        own = _own_block()
        dist_v = dist_ref[...]
        kk, vv = _bands(kvc_ref, kvp_ref)
        for head in HEADS:
            s_scr[head] = _head_scores(q_ref, kk[head // Q_PER_KV], dist_v, head, first, own)
        for head in HEADS:
            probs, _ = _softmax_sink(s_scr[head], sink_ref[head])
            p_scr[head] = _unfold(probs.astype(BF16), own)
        for pair in range(N_Q_HEADS // 2):
            even = _nn(p_scr[2 * pair], vv[2 * pair // Q_PER_KV])
            odd = _nn(p_scr[2 * pair + 1], vv[2 * pair // Q_PER_KV])
            o_ref[:, LANES * pair:LANES * (pair + 1)] = jnp.where(_low_half(even.shape), even, odd).astype(BF16)

    return pl.pallas_call(
        body, name="attn_fwd", grid=(t // BLOCK,),
        in_specs=[_rows(BLOCK, ATTN_WIDTH), _rows(BLOCK, 2 * KV_WIDTH),
                  pl.BlockSpec((BLOCK, 2 * KV_WIDTH), lambda i: (jnp.maximum(i - 1, 0), 0)),
                  _resident(dist.shape), SMEM],
        out_specs=_rows(BLOCK, ATTN_WIDTH),
        out_shape=jax.ShapeDtypeStruct((t, ATTN_WIDTH), BF16),
        scratch_shapes=[pltpu.VMEM((N_Q_HEADS, BLOCK, BLOCK), F32), pltpu.VMEM((N_Q_HEADS, BLOCK, 2 * BLOCK), BF16)],
        compiler_params=_params("parallel"),
    )(q, kv, kv, dist, sinks)


def _pool_counts(tm, width):
    row = pl.program_id(0) * tm + lax.broadcasted_iota(jnp.int32, (tm, 1), 0)
    return jnp.minimum(row + 1, width).astype(F32)


def _trailing_sums(zz, group):
    s = zz
    for k in range(group + 1):
        s = s + pltpu.roll(s, 1 << k, 0)
    return s


def _leading_sums(zz, group):
    rows = zz.shape[0]
    s = zz
    for k in range(group + 1):
        s = s + pltpu.roll(s, rows - (1 << k), 0)
    return s


def _mix_out_fwd(attn, z, gate, h, wattn, wmix, scale, wpool_t, wout, tm):
    t, d = h.shape

    def body(attn_ref, z_ref, halo_ref, gate_ref, h_ref, wattn_ref, wmix_ref, scale_ref, wpool_ref, wout_ref,
             h2_ref, a_ref, p_ref, merged_ref, ms_ref, pooled_ref):
        halo = jnp.where(pl.program_id(0) == 0, 0.0, halo_ref[...])
        for gi, width in enumerate(POOL_WINDOWS):
            lo, hi = gi * POOL_GROUP, (gi + 1) * POOL_GROUP
            zg = z_ref[:, lo:hi]
            sums = _trailing_sums(jnp.concatenate([halo[:, lo:hi], zg], axis=0), gi)[HALO:, :]
            pooled = (sums / _pool_counts(tm, width) - zg).astype(BF16)
            pooled_ref[:, lo:hi] = pooled
            ms_ref[:, lo:hi] = (_nn(pooled, wmix_ref[gi]) * scale_ref[:, lo:hi]).astype(BF16)
        p = _nt(ms_ref[...], wpool_ref[...])
        a = _nn(attn_ref[...], wattn_ref[...])
        a_ref[...] = a
        p_ref[...] = p
        merged = (jax.nn.sigmoid(gate_ref[:, 0:d]) * a + jax.nn.sigmoid(gate_ref[:, d:2 * d]) * p).astype(BF16)
        merged_ref[...] = merged
        h2_ref[...] = h_ref[...] + _nn(merged, wout_ref[...])

    halo_spec = pl.BlockSpec((HALO, POOL_WIDTH), lambda i: (jnp.maximum(i * (tm // HALO) - 1, 0), 0))
    return pl.pallas_call(
        body, name="mix_out_fwd", grid=(t // tm,),
        in_specs=[_rows(tm, ATTN_WIDTH), _rows(tm, POOL_WIDTH), halo_spec, _rows(tm, 2 * d), _rows(tm, d),
                  _resident(wattn.shape), _resident(wmix.shape), _resident(scale.shape), _resident(wpool_t.shape),
                  _resident(wout.shape)],
        out_specs=[_rows(tm, d), _rows(tm, d), _rows(tm, d), _rows(tm, d), _rows(tm, POOL_WIDTH),
                   _rows(tm, POOL_WIDTH)],
        out_shape=[jax.ShapeDtypeStruct((t, d), F32), jax.ShapeDtypeStruct((t, d), F32),
                   jax.ShapeDtypeStruct((t, d), F32), jax.ShapeDtypeStruct((t, d), BF16),
                   jax.ShapeDtypeStruct((t, POOL_WIDTH), BF16), jax.ShapeDtypeStruct((t, POOL_WIDTH), BF16)],
        compiler_params=_params("parallel"),
    )(attn, z, z, gate, h, wattn, wmix, scale, wpool_t, wout)


def _loss_head(h, g, target, tm):
    t, d = h.shape

    def body(h_ref, g_ref, tgt_ref, loss_ref, dh_ref, dg_ref):
        xh, r = _rms_fwd(h_ref[...], g_ref[...])
        err = xh * g_ref[...] - tgt_ref[...]
        part = 0.5 * jnp.sum(jnp.mean(err * err, axis=-1, keepdims=True), axis=0, keepdims=True)
        dx, dg = _rms_bwd(err * (1.0 / d), xh, r, g_ref[...])
        dh_ref[...] = dx

        @pl.when(pl.program_id(0) == 0)
        def _():
            dg_ref[...] = jnp.zeros_like(dg_ref)
            loss_ref[...] = jnp.zeros_like(loss_ref)

        dg_ref[...] += dg
        loss_ref[...] += jnp.broadcast_to(part, loss_ref.shape)

    return pl.pallas_call(
        body, name="loss_head", grid=(t // tm,),
        in_specs=[_rows(tm, d), _resident((1, d)), _rows(tm, d)],
        out_specs=[pl.BlockSpec((1, LANES), lambda i: (0, 0)), _rows(tm, d), pl.BlockSpec((1, d), lambda i: (0, 0))],
        out_shape=[jax.ShapeDtypeStruct((1, LANES), F32), jax.ShapeDtypeStruct((t, d), F32),
                   jax.ShapeDtypeStruct((1, d), F32)],
        compiler_params=_params("arbitrary"),
    )(h, g, target)


def _mix_out_bwd(dh, gate, a, p, pooled, wattn, wmix, scale, wpool_t, wout, tm):
    t, d = dh.shape

    def body(dh_ref, gate_ref, a_ref, p_ref, pooled_ref, wattn_ref, wmix_ref, scale_ref, wpool_ref, wout_ref,
             dhb_ref, dab_ref, dpb_ref, dattn_ref, dgate_ref, dpooled_ref, dwmix_ref, dscale_ref):
        @pl.when(pl.program_id(0) == 0)
        def _():
            dwmix_ref[...] = jnp.zeros_like(dwmix_ref)
            dscale_ref[...] = jnp.zeros_like(dscale_ref)

        dhb = dh_ref[...].astype(BF16)
        dhb_ref[...] = dhb
        dm = _nt(dhb, wout_ref[...])
        sa = jax.nn.sigmoid(gate_ref[:, 0:d])
        sp = jax.nn.sigmoid(gate_ref[:, d:2 * d])
        da = (dm * sa).astype(BF16)
        dp = (dm * sp).astype(BF16)
        dab_ref[...] = da
        dpb_ref[...] = dp
        dgate_ref[:, 0:d] = (dm * a_ref[...] * (sa * (1.0 - sa))).astype(BF16)
        dgate_ref[:, d:2 * d] = (dm * p_ref[...] * (sp * (1.0 - sp))).astype(BF16)
        dattn_ref[...] = _nt(da, wattn_ref[...]).astype(BF16)
        dms = _nn(dp, wpool_ref[...])
        for gi in range(len(POOL_WINDOWS)):
            lo, hi = gi * POOL_GROUP, (gi + 1) * POOL_GROUP
            pooled_g = pooled_ref[:, lo:hi]
            mixed = _nn(pooled_g, wmix_ref[gi])
            dscale_ref[:, lo:hi] += jnp.sum(dms[:, lo:hi] * mixed, axis=0, keepdims=True)
            dmixed = (dms[:, lo:hi] * scale_ref[:, lo:hi]).astype(BF16)
            dwmix_ref[gi] += _tn(pooled_g, dmixed)
            dpooled_ref[:, lo:hi] = _nt(dmixed, wmix_ref[gi])

    acc = lambda shape: pl.BlockSpec(shape, lambda i: (0,) * len(shape))
    return pl.pallas_call(
        body, name="mix_out_bwd", grid=(t // tm,),
        in_specs=[_rows(tm, d), _rows(tm, 2 * d), _rows(tm, d), _rows(tm, d), _rows(tm, POOL_WIDTH),
                  _resident(wattn.shape), _resident(wmix.shape), _resident(scale.shape), _resident(wpool_t.shape),
                  _resident(wout.shape)],
        out_specs=[_rows(tm, d), _rows(tm, d), _rows(tm, d), _rows(tm, ATTN_WIDTH), _rows(tm, 2 * d),
                   _rows(tm, POOL_WIDTH), acc(wmix.shape), acc((1, POOL_WIDTH))],
        out_shape=[jax.ShapeDtypeStruct((t, d), BF16), jax.ShapeDtypeStruct((t, d), BF16),
                   jax.ShapeDtypeStruct((t, d), BF16), jax.ShapeDtypeStruct((t, ATTN_WIDTH), BF16),
                   jax.ShapeDtypeStruct((t, 2 * d), BF16), jax.ShapeDtypeStruct((t, POOL_WIDTH), F32),
                   jax.ShapeDtypeStruct(wmix.shape, F32), jax.ShapeDtypeStruct((1, POOL_WIDTH), F32)],
        compiler_params=_params("arbitrary"),
    )(dh, gate, a, p, pooled, wattn, wmix, scale, wpool_t, wout)


def _fold_halves(x):
    return x + pltpu.roll(x, HEAD_DIM, 1)


def _attn_bwd(q, kv, dattn, dist, sinks):
    t = q.shape[0]

    def body(q_ref, kvc_ref, kvp_ref, do_ref, dist_ref, sink_ref, dq_ref, dkv_own_ref, dkv_prev_ref, dsink_ref,
             s_scr, dp_scr, p_scr, ds_scr):
        first = pl.program_id(0) == 0

        @pl.when(first)
        def _():
            dsink_ref[...] = jnp.zeros_like(dsink_ref)

        own = _own_block()
        dist_v = dist_ref[...]
        kk, vv = _bands(kvc_ref, kvp_ref)
        lane = lax.broadcasted_iota(jnp.int32, (1, LANES), 1)
        for head in HEADS:
            hk = head // Q_PER_KV
            s_scr[head] = _head_scores(q_ref, kk[hk], dist_v, head, first, own)
            dp_scr[head] = _fold(_nt(_own_half(do_ref, head), vv[hk]), own)
        dsink = jnp.zeros((1, LANES), F32)
        for head in HEADS:
            probs, psink = _softmax_sink(s_scr[head], sink_ref[head])
            dprobs = dp_scr[head]
            rowdot = jnp.sum(probs * dprobs, axis=-1, keepdims=True)
            p_scr[head] = _unfold(probs.astype(BF16), own)
            ds_scr[head] = _unfold((probs * (dprobs - rowdot)).astype(BF16), own)
            dsink = dsink + jnp.where(lane == head, jnp.sum(-psink * rowdot, axis=0, keepdims=True), 0.0)
        dk_heads, dv_heads = [], []
        for hk in range(N_KV_HEADS):
            dk_t = jnp.zeros((LANES, 2 * BLOCK), F32)
            dv_t = jnp.zeros((LANES, 2 * BLOCK), F32)
            for pair in range(Q_PER_KV // 2):
                col = LANES * (hk * (Q_PER_KV // 2) + pair)
                q_t = (q_ref[:, col:col + LANES] * HEAD_DIM ** -0.5).T
                do_t = do_ref[:, col:col + LANES].T
                dqs = []
                for head in (hk * Q_PER_KV + 2 * pair, hk * Q_PER_KV + 2 * pair + 1):
                    mine = (lax.broadcasted_iota(jnp.int32, q_t.shape, 0) < HEAD_DIM) == (head % 2 == 0)
                    dv_t = dv_t + _nn(jnp.where(mine, do_t, jnp.zeros_like(do_t)), p_scr[head])
                    dk_t = dk_t + _nn(jnp.where(mine, q_t, jnp.zeros_like(q_t)), ds_scr[head])
                    dqs.append(_nn(ds_scr[head], kk[hk]))
                dq_pair = jnp.where(_low_half(dqs[0].shape), dqs[0], dqs[1])
                dq_ref[:, col:col + LANES] = (dq_pair * HEAD_DIM ** -0.5).astype(BF16)
            dk_heads.append(_fold_halves(dk_t.T))
            dv_heads.append(_fold_halves(dv_t.T))
        low = _low_half(dk_heads[0].shape)
        dkv = jnp.concatenate([jnp.where(low, dk_heads[0], dk_heads[1]), jnp.where(low, dv_heads[0], dv_heads[1])],
                              axis=1)
        dkv_prev_ref[...] = dkv[0:BLOCK, :]
        dkv_own_ref[...] = dkv[BLOCK:2 * BLOCK, :]
        dsink_ref[...] += dsink

    return pl.pallas_call(
        body, name="attn_bwd", grid=(t // BLOCK,),
        in_specs=[_rows(BLOCK, ATTN_WIDTH), _rows(BLOCK, 2 * KV_WIDTH),
                  pl.BlockSpec((BLOCK, 2 * KV_WIDTH), lambda i: (jnp.maximum(i - 1, 0), 0)),
                  _rows(BLOCK, ATTN_WIDTH), _resident(dist.shape), SMEM],
        out_specs=[_rows(BLOCK, ATTN_WIDTH), _rows(BLOCK, 2 * KV_WIDTH), _rows(BLOCK, 2 * KV_WIDTH),
                   pl.BlockSpec((1, LANES), lambda i: (0, 0))],
        out_shape=[jax.ShapeDtypeStruct((t, ATTN_WIDTH), BF16), jax.ShapeDtypeStruct((t, 2 * KV_WIDTH), F32),
                   jax.ShapeDtypeStruct((t, 2 * KV_WIDTH), F32), jax.ShapeDtypeStruct((1, LANES), F32)],
        scratch_shapes=[pltpu.VMEM((N_Q_HEADS, BLOCK, BLOCK), F32), pltpu.VMEM((N_Q_HEADS, BLOCK, BLOCK), F32),
                        pltpu.VMEM((N_Q_HEADS, BLOCK, 2 * BLOCK), BF16),
                        pltpu.VMEM((N_Q_HEADS, BLOCK, 2 * BLOCK), BF16)],
        compiler_params=_params("arbitrary"),
    )(q, kv, kv, dattn, dist, sinks)


def _mix_in_bwd(dq, dkv_own, dkv_prev, dpooled, dgate, h, g, win_t, dh_res, tm):
    t, d = h.shape
    nt = t // tm

    def body(dq_ref, own_ref, prev_ref, prev_next_ref, dpool_ref, halo_ref, dgate_ref, h_ref, g_ref, w_ref, res_ref,
             dproj_ref, dh_ref, dg_ref):
        i = pl.program_id(0)
        last = i == nt - 1
        dproj_ref[:, 0:OFF_KV] = dq_ref[...]
        from_next = jnp.where(last, 0.0, prev_next_ref[...])
        if tm > BLOCK:
            from_next = jnp.concatenate([prev_ref[BLOCK:tm, :], from_next], axis=0)
        dproj_ref[:, OFF_KV:OFF_Z] = (own_ref[...] + from_next).astype(BF16)
        halo = jnp.where(last, 0.0, halo_ref[...])
        for gi, width in enumerate(POOL_WINDOWS):
            lo, hi = gi * POOL_GROUP, (gi + 1) * POOL_GROUP
            dpg = dpool_ref[:, lo:hi]
            scaled = jnp.concatenate([dpg / _pool_counts(tm, width), halo[:, lo:hi] / float(width)], axis=0)
            dz = _leading_sums(scaled, gi)[0:tm, :] - dpg
            dproj_ref[:, OFF_Z + lo:OFF_Z + hi] = dz.astype(BF16)
        dproj_ref[:, OFF_GATE:IN_WIDTH] = dgate_ref[...]
        du = _nn(dproj_ref[...], w_ref[...])
        xh, r = _rms_fwd(h_ref[...], g_ref[...])
        dx, dg = _rms_bwd(du, xh, r, g_ref[...])
        dh_ref[...] = res_ref[...] + dx

        @pl.when(i == 0)
        def _():
            dg_ref[...] = jnp.zeros_like(dg_ref)

        dg_ref[...] += dg

    per = tm // BLOCK
    next_block = pl.BlockSpec((BLOCK, 2 * KV_WIDTH), lambda i: (jnp.minimum((i + 1) * per, t // BLOCK - 1), 0))
    next_halo = pl.BlockSpec((HALO, POOL_WIDTH), lambda i: (jnp.minimum((i + 1) * (tm // HALO), t // HALO - 1), 0))
    return pl.pallas_call(
        body, name="mix_in_bwd", grid=(nt,),
        in_specs=[_rows(tm, ATTN_WIDTH), _rows(tm, 2 * KV_WIDTH), _rows(tm, 2 * KV_WIDTH), next_block,
                  _rows(tm, POOL_WIDTH), next_halo, _rows(tm, 2 * d), _rows(tm, d), _resident((1, d)),
                  _resident((IN_WIDTH, d)), _rows(tm, d)],
        out_specs=[_rows(tm, IN_WIDTH), _rows(tm, d), pl.BlockSpec((1, d), lambda i: (0, 0))],
        out_shape=[jax.ShapeDtypeStruct((t, IN_WIDTH), BF16), jax.ShapeDtypeStruct((t, d), F32),
                   jax.ShapeDtypeStruct((1, d), F32)],
        compiler_params=_params("arbitrary"),
    )(dq, dkv_own, dkv_prev, dkv_prev, dpooled, dpooled, dgate, h, g, win_t, dh_res)


def _tile(t, want):
    return min(want, t)


def _local_step(x, target, norms, sinks, wmix, scale, big):
    t = x.shape[0]
    tm_f, tm_b = _tile(t, 512), _tile(t, 256)
    g1, gm, g2, gf = norms
    dist = _attn_dist()
    sinks = sinks.reshape(N_Q_HEADS)
    wmix_b = wmix.astype(BF16)

    h1, ab1, n1 = _ffn_fwd(x, g1, big["wup1_t"], big["wdown1"], tm_f)
    u, q, kv, z, gate = _mix_in_fwd(h1, gm, big["win_t"], tm_f)
    attn = _attn_fwd(q, kv, dist, sinks)
    h2, a, p, merged, ms, pooled = _mix_out_fwd(attn, z, gate, h1, big["wattn"], wmix_b, scale, big["wpool_t"],
                                                big["wout"], tm_b)
    h3, ab2, n2 = _ffn_fwd(h2, g2, big["wup2_t"], big["wdown2"], tm_f)
    loss, dh3, dgf = _loss_head(h3, gf, target, tm_f)

    dh2, dab2, act2, dhb3, dg2 = _ffn_bwd(dh3, h2, g2, ab2, big["wup2_t"], big["wdown2"], tm_b)
    dhb2, da_b, dp_b, dattn, dgate, dpooled, dwmix, dscale = _mix_out_bwd(
        dh2, gate, a, p, pooled, big["wattn"], wmix_b, scale, big["wpool_t"], big["wout"], tm_b)
    dq, dkv_own, dkv_prev, dsinks = _attn_bwd(q, kv, dattn, dist, sinks)
    dproj, dh1, dgm = _mix_in_bwd(dq, dkv_own, dkv_prev, dpooled, dgate, h1, gm, big["win_t"], dh2, tm_b)
    dx, dab1, act1, dhb1, dg1 = _ffn_bwd(dh1, x, g1, ab1, big["wup1_t"], big["wdown1"], tm_b)

    tk = _tile(t, 1024)
    grads = {
        "wup1_t": _wgrad(dab1, n1, 1.0, D_FF, tk, "wgrad_up1"),
        "wdown1": _wgrad(act1, dhb1, 0.5, D_FF, tk, "wgrad_down1"),
        "win_t": _wgrad(dproj, u, 1.0, IN_WIDTH // 2, tk, "wgrad_in"),
        "wattn": _wgrad(attn, da_b, 1.0, D_MODEL, tk, "wgrad_attn"),
        "wpool_t": _wgrad(dp_b, ms, 1.0, D_MODEL, tk, "wgrad_pool"),
        "wout": _wgrad(merged, dhb2, 1.0, D_MODEL, tk, "wgrad_out"),
        "wup2_t": _wgrad(dab2, n2, 1.0, D_FF, tk, "wgrad_up2"),
        "wdown2": _wgrad(act2, dhb3, 0.5, D_FF, tk, "wgrad_down2"),
    }
    small = {"g1": dg1, "gm": dgm, "g2": dg2, "gf": dgf, "sinks": dsinks, "wmix": dwmix, "scale": dscale}
    return loss, dx, grads, small


BIG = (("wup1_t", "ffn1_w_up", True), ("wdown1", "ffn1_w_down", False), ("win_t", "w_in", True),
       ("wattn", "w_attn_up", False), ("wpool_t", "w_pool_up", True), ("wout", "w_out", False),
       ("wup2_t", "ffn2_w_up", True), ("wdown2", "ffn2_w_down", False))
ANY = pl.BlockSpec(memory_space=pl.ANY)
WIRE = BF16


def _place():
    return lax.axis_index("x"), lax.axis_index("y"), lax.axis_index("c")


def _all_gather(shards):
    n = len(shards)

    def body(*refs):
        ins, outs = refs[:n], refs[n:2 * n]
        send_sems, recv_sems, local_sems = refs[2 * n:]
        x, y, c = _place()
        me, sibling = (x, y, c), (x, y, 1 - c)
        chips = [(1 - x, y), (x, 1 - y), (1 - x, 1 - y)]

        def rows(w, px, py, pc):
            r = ins[w].shape[0]
            return outs[w].at[pl.ds((4 * px + 2 * py + pc) * r, r), :]

        def copy(w, k, block, to, src=None):
            return pltpu.make_async_remote_copy(
                src_ref=rows(w, *block) if src is None else src, dst_ref=rows(w, *block),
                send_sem=send_sems.at[w, k], recv_sem=recv_sems.at[w, k], device_id=to, device_id_type=MESH)

        started = []
        mine = [pltpu.make_async_copy(ins[w], rows(w, *me), local_sems.at[w]) for w in range(n)]
        for w in range(n):
            mine[w].start()
            first = [copy(w, 0, me, sibling, src=ins[w])]
            first += [copy(w, 1 + j, me, (*chip, c), src=ins[w]) for j, chip in enumerate(chips)]
            for cp in first:
                cp.start()
            started += first
        for w in range(n):
            for j, chip in enumerate(chips):
                copy(w, 1 + j, (*chip, c), me).wait_recv()
                passed = copy(w, 4 + j, (*chip, c), sibling)
                passed.start()
                started.append(passed)
        for w in range(n):
            copy(w, 0, sibling, me).wait_recv()
            for j, chip in enumerate(chips):
                copy(w, 4 + j, (*chip, 1 - c), me).wait_recv()
        for cp in started:
            cp.wait_send()
        for w in range(n):
            mine[w].wait()

    return pl.pallas_call(
        body, name="all_gather_weights", in_specs=[ANY] * n, out_specs=[ANY] * n,
        out_shape=[jax.ShapeDtypeStruct((N_DEV * s.shape[0], s.shape[1]), s.dtype) for s in shards],
        scratch_shapes=[pltpu.SemaphoreType.DMA((n, N_DEV - 1)), pltpu.SemaphoreType.DMA((n, N_DEV - 1)),
                        pltpu.SemaphoreType.DMA((n,))],
    )(*shards)


def _exchange_siblings(grads, small):
    n = len(grads)
    srows = small.shape[0]

    def body(*refs):
        ins, small_ref = refs[:n], refs[n]
        outs, small_all = refs[n + 1:2 * n + 1], refs[2 * n + 1]
        send_sems, recv_sems, small_send, small_recv, local_sem = refs[2 * n + 2:]
        x, y, c = _place()
        sibling = (x, y, 1 - c)
        for w in range(n):
            r = outs[w].shape[0] // N_CHIP
            for chip in range(N_CHIP):
                pltpu.make_async_remote_copy(
                    src_ref=ins[w].at[pl.ds((2 * chip + 1 - c) * r, r), :], dst_ref=outs[w].at[pl.ds(chip * r, r), :],
                    send_sem=send_sems.at[w], recv_sem=recv_sems.at[w], device_id=sibling, device_id_type=MESH).start()

        def slot(px, py, pc):
            return small_all.at[pl.ds((4 * px + 2 * py + pc) * srows, srows), :]

        def peer(k):
            return x ^ (k >> 2), y ^ ((k >> 1) & 1), c ^ (k & 1)

        def small_copy(k, landing):
            return pltpu.make_async_remote_copy(
                src_ref=small_ref, dst_ref=landing, send_sem=small_send.at[k - 1], recv_sem=small_recv.at[k - 1],
                device_id=peer(k), device_id_type=MESH)

        mine = pltpu.make_async_copy(small_ref, slot(x, y, c), local_sem)
        mine.start()
        for k in range(1, N_DEV):
            small_copy(k, slot(x, y, c)).start()
        for w in range(n):
            whole = outs[w]
            pltpu.make_async_remote_copy(
                src_ref=ins[w].at[pl.ds(0, whole.shape[0]), :], dst_ref=whole, send_sem=send_sems.at[w],
                recv_sem=recv_sems.at[w], device_id=sibling, device_id_type=MESH).wait()
        for k in range(1, N_DEV):
            small_copy(k, slot(*peer(k))).wait()
        mine.wait()

    return pl.pallas_call(
        body, name="exchange_siblings", in_specs=[ANY] * (n + 1), out_specs=[ANY] * (n + 1),
        out_shape=[jax.ShapeDtypeStruct((g.shape[0] // 2, g.shape[1]), g.dtype) for g in grads]
        + [jax.ShapeDtypeStruct((N_DEV * srows, LANES), small.dtype)],
        scratch_shapes=[pltpu.SemaphoreType.DMA((n,)), pltpu.SemaphoreType.DMA((n,)),
                        pltpu.SemaphoreType.DMA((N_DEV - 1,)), pltpu.SemaphoreType.DMA((N_DEV - 1,)),
                        pltpu.SemaphoreType.DMA],
    )(*grads, small)


def _pair_sum(grad, got, core, name):
    r, cols = got.shape[0] // N_CHIP, got.shape[1]

    def body(core_ref, g_ref, s_ref, o_ref):
        del core_ref
        o_ref[...] = (g_ref[...] + s_ref[...]).astype(o_ref.dtype)

    return pl.pallas_call(
        body, name=name,
        grid_spec=pltpu.PrefetchScalarGridSpec(
            num_scalar_prefetch=1, grid=(N_CHIP,),
            in_specs=[pl.BlockSpec((None, None, r, cols), lambda q, core_ref: (q, core_ref[0], 0, 0)),
                      pl.BlockSpec((None, r, cols), lambda q, core_ref: (q, 0, 0))],
            out_specs=pl.BlockSpec((None, r, cols), lambda q, core_ref: (q, 0, 0))),
        out_shape=jax.ShapeDtypeStruct((N_CHIP, r, cols), WIRE),
        compiler_params=_params("parallel"),
    )(core, grad.reshape(N_CHIP, 2, r, cols), got.reshape(N_CHIP, r, cols))


def _exchange_chips(parts):
    n = len(parts)

    def body(*refs):
        ins, outs = refs[:n], refs[n:2 * n]
        send_sems, recv_sems, local_sems = refs[2 * n:]
        x, y, c = _place()
        my_chip = 2 * x + y
        peers = [(1 - x, y), (x, 1 - y), (1 - x, 1 - y)]
        mine = [pltpu.make_async_copy(ins[w].at[my_chip], outs[w].at[my_chip], local_sems.at[w]) for w in range(n)]

        def copy(w, k):
            px, py = peers[k]
            return pltpu.make_async_remote_copy(
                src_ref=ins[w].at[2 * px + py], dst_ref=outs[w].at[my_chip], send_sem=send_sems.at[w, k],
                recv_sem=recv_sems.at[w, k], device_id=(px, py, c), device_id_type=MESH)

        def arrival(w, k):
            px, py = peers[k]
            return pltpu.make_async_remote_copy(
                src_ref=ins[w].at[my_chip], dst_ref=outs[w].at[2 * px + py], send_sem=send_sems.at[w, k],
                recv_sem=recv_sems.at[w, k], device_id=(px, py, c), device_id_type=MESH)

        for w in range(n):
            mine[w].start()
            for k in range(len(peers)):
                copy(w, k).start()
        for w in range(n):
            for k in range(len(peers)):
                arrival(w, k).wait_recv()
                copy(w, k).wait_send()
            mine[w].wait()

    return pl.pallas_call(
        body, name="exchange_chips", in_specs=[ANY] * n, out_specs=[ANY] * n,
        out_shape=[jax.ShapeDtypeStruct(p.shape, p.dtype) for p in parts],
        scratch_shapes=[pltpu.SemaphoreType.DMA((n, N_CHIP - 1)), pltpu.SemaphoreType.DMA((n, N_CHIP - 1)),
                        pltpu.SemaphoreType.DMA((n,))],
    )(*parts)


def _chip_sum(got, name):
    _, r, cols = got.shape
    tr = r if r <= 512 else r // 2

    def body(g_ref, o_ref):
        acc = g_ref[0].astype(F32)
        for q in range(1, N_CHIP):
            acc = acc + g_ref[q].astype(F32)
        o_ref[...] = acc

    return pl.pallas_call(
        body, name=name, grid=(r // tr,),
        in_specs=[pl.BlockSpec((N_CHIP, tr, cols), lambda i: (0, i, 0))],
        out_specs=pl.BlockSpec((tr, cols), lambda i: (i, 0)),
        out_shape=jax.ShapeDtypeStruct((r, cols), F32),
        compiler_params=_params("parallel"),
    )(got)


def _adamw_math(w, g, m, v):
    m = ADAM_B1 * m + (1.0 - ADAM_B1) * g
    v = ADAM_B2 * v + (1.0 - ADAM_B2) * (g * g)
    m_hat = m / (1.0 - ADAM_B1 ** ADAM_STEP)
    v_hat = v / (1.0 - ADAM_B2 ** ADAM_STEP)
    return -ADAM_LR * (m_hat / (jnp.sqrt(v_hat) + ADAM_EPS) + ADAM_WD * w), m, v


def _adamw(w, g, m, v, name):
    r, cols = w.shape
    tr = r if r <= 512 else 256

    def body(w_ref, g_ref, m_ref, v_ref, d_ref, m2_ref, v2_ref):
        d_ref[...], m2_ref[...], v2_ref[...] = _adamw_math(w_ref[...], g_ref[...], m_ref[...], v_ref[...])

    spec = pl.BlockSpec((tr, cols), lambda i: (i, 0))
    return pl.pallas_call(
        body, name=name, grid=(r // tr,), in_specs=[spec] * 4, out_specs=[spec] * 3,
        out_shape=[jax.ShapeDtypeStruct((r, cols), F32)] * 3, compiler_params=_params("parallel"),
    )(w, g, m, v)


def _small_update(gathered, w, m, v):
    rows = w.shape[0]

    def body(all_ref, w_ref, m_ref, v_ref, g_ref, d_ref, m2_ref, v2_ref):
        g = all_ref[0]
        for dev in range(1, N_DEV):
            g = g + all_ref[dev]
        g_ref[...] = g
        d_ref[...], m2_ref[...], v2_ref[...] = _adamw_math(w_ref[...], g, m_ref[...], v_ref[...])

    return pl.pallas_call(
        body, name="small_update", out_shape=[jax.ShapeDtypeStruct((rows, LANES), F32)] * 4,
        compiler_params=pltpu.CompilerParams(vmem_limit_bytes=VMEM_LIMIT),
    )(gathered.reshape(N_DEV, rows, LANES), w, m, v)


SMALL = (("pool_w_mix", 512), ("ffn1_norm", 8), ("mix_norm", 8), ("ffn2_norm", 8), ("final_norm", 8),
         ("pool_scale", 8), ("sinks", 8), ("loss", 8))
SMALL_ROWS = sum(rows for _, rows in SMALL)


def _pack_small(parts):
    out = []
    for name, rows in SMALL:
        flat = parts[name].astype(F32).reshape(-1)
        out.append(jnp.pad(flat, (0, rows * LANES - flat.shape[0])).reshape(rows, LANES))
    return jnp.concatenate(out, axis=0)


def _unpack_small(packed, shapes):
    out, row = {}, 0
    for name, rows in SMALL:
        shape = shapes[name]
        size = int(np.prod(shape)) if shape else 1
        out[name] = packed[row:row + rows].reshape(-1)[:size].reshape(shape)
        row += rows
    return out


def kernel(x, ffn1_norm, ffn1_w_up, ffn1_w_down, mix_norm, w_in, sinks, w_attn_up, pool_w_mix, pool_scale, w_pool_up, w_out, ffn2_norm, ffn2_w_up, ffn2_w_down, final_norm, loss_target, m_ffn1_norm, m_ffn1_w_up, m_ffn1_w_down, m_mix_norm, m_w_in, m_sinks, m_w_attn_up, m_pool_w_mix, m_pool_scale, m_w_pool_up, m_w_out, m_ffn2_norm, m_ffn2_w_up, m_ffn2_w_down, m_final_norm, v_ffn1_norm, v_ffn1_w_up, v_ffn1_w_down, v_mix_norm, v_w_in, v_sinks, v_w_attn_up, v_pool_w_mix, v_pool_scale, v_w_pool_up, v_w_out, v_ffn2_norm, v_ffn2_w_up, v_ffn2_w_down, v_final_norm):
    args = dict(locals())
    weight_names = ("ffn1_norm", "ffn1_w_up", "ffn1_w_down", "mix_norm", "w_in", "sinks", "w_attn_up", "pool_w_mix",
                    "pool_scale", "w_pool_up", "w_out", "ffn2_norm", "ffn2_w_up", "ffn2_w_down", "final_norm")

    shards = [(args[p][0].T if tr else args[p][0]).astype(BF16) for _, p, tr in BIG]
    gathered = dict(zip((k for k, _, _ in BIG), _all_gather(shards)))

    norms = (ffn1_norm, mix_norm, ffn2_norm, final_norm.reshape(1, D_MODEL))
    loss_lanes, dx, grads, small = _local_step(x[0], loss_target[0], norms, sinks, pool_w_mix[0], pool_scale, gathered)

    small_parts = {"pool_w_mix": small["wmix"], "ffn1_norm": small["g1"], "mix_norm": small["gm"],
                   "ffn2_norm": small["g2"], "final_norm": small["gf"], "pool_scale": small["scale"],
                   "sinks": small["sinks"][:, :N_Q_HEADS], "loss": loss_lanes[:, :1]}
    full = [grads[k] for k, _, _ in BIG]
    *from_sibling, small_all = _exchange_siblings(full, _pack_small(small_parts))
    core = lax.axis_index("c").astype(jnp.int32).reshape(1)
    parts = [_pair_sum(g, s, core, "pair_sum_" + k) for g, s, (k, _, _) in zip(full, from_sibling, BIG)]
    from_chips = _exchange_chips(parts)

    grad, delta, new_m, new_v = {}, {}, {}, {}
    for got, (k, p, tr) in zip(from_chips, BIG):
        g = _chip_sum(got, "chip_sum_" + k)
        g = g.T if tr else g
        d, m2, v2 = _adamw(args[p][0], g, args["m_" + p][0], args["v_" + p][0], "adamw_" + k)
        grad[p], delta[p], new_m[p], new_v[p] = g[None], d[None], m2[None], v2[None]

    shapes = {name: args[name].shape for name, _ in SMALL if name != "loss"}
    shapes["loss"] = ()
    packed = {pre: _pack_small({**{name: args[pre + name] for name, _ in SMALL if name != "loss"},
                                "loss": jnp.zeros((), F32)}) for pre in ("", "m_", "v_")}
    g_s, d_s, m_s, v_s = _small_update(small_all, packed[""], packed["m_"], packed["v_"])
    g_small, d_small, m_small, v_small = (_unpack_small(a, shapes) for a in (g_s, d_s, m_s, v_s))
    for name, _ in SMALL:
        if name != "loss":
            grad[name], delta[name], new_m[name], new_v[name] = (
                g_small[name], d_small[name], m_small[name], v_small[name])

    return (g_small["loss"], dx[None], *[grad[n] for n in weight_names], *[delta[n] for n in weight_names],
            *[new_m[n] for n in weight_names], *[new_v[n] for n in weight_names])
```

```python
import functools

import jax
import jax.numpy as jnp
import numpy as np
from jax import lax
from jax.experimental import pallas as pl
from jax.experimental.pallas import tpu as pltpu

F32 = jnp.float32
BF16 = jnp.bfloat16

D_MODEL = 1024
D_FF = 2816
N_Q_HEADS = 16
N_KV_HEADS = 2
Q_PER_KV = N_Q_HEADS // N_KV_HEADS
HEAD_DIM = 64
BLOCK = 128
ATTN_WIDTH = N_Q_HEADS * HEAD_DIM
KV_WIDTH = N_KV_HEADS * HEAD_DIM
POOL_WINDOWS = (2, 4, 8, 16)
POOL_GROUP = 128
POOL_WIDTH = 512
HALO = 16
IN_WIDTH = ATTN_WIDTH + 2 * KV_WIDTH + POOL_WIDTH + 2 * D_MODEL
OFF_KV = ATTN_WIDTH
OFF_Z = ATTN_WIDTH + 2 * KV_WIDTH
OFF_GATE = OFF_Z + POOL_WIDTH
NORM_EPS = 1e-6
ADAM_LR = 0.001
ADAM_B1 = 0.9
ADAM_B2 = 0.999
ADAM_EPS = 1e-08
ADAM_WD = 0.01
ADAM_STEP = 10

N_DEV = 8
N_CHIP = 4
LANES = 128
FF_CHUNK = 256
VMEM_LIMIT = 56 * 1024 * 1024
MESH = pl.DeviceIdType.MESH


def _nn(a, b):
    return jnp.dot(a, b, preferred_element_type=F32)


def _nt(a, b):
    return lax.dot_general(a, b, (((1,), (1,)), ((), ())), preferred_element_type=F32)


def _tn(a, b):
    return lax.dot_general(a, b, (((0,), (0,)), ((), ())), preferred_element_type=F32)


def _params(*sem):
    return pltpu.CompilerParams(dimension_semantics=sem, vmem_limit_bytes=VMEM_LIMIT)


def _resident(shape):
    return pl.BlockSpec(shape, lambda *_: (0,) * len(shape), pipeline_mode=pl.Buffered(1))


def _rows(tm, cols):
    return pl.BlockSpec((tm, cols), lambda i: (i, 0))


class _Carry:
    def __init__(self, inputs, out_shape, scratch, start, finish, middle=None):
        self.inputs, self.out_shape, self.scratch = list(inputs), list(out_shape), list(scratch)
        self.start, self.finish, self.middle = start, finish, middle


def _merge(a, b):
    ia, oa, sa = len(a.inputs), len(a.out_shape), len(a.scratch)

    def both(fa, fb):
        def run(ins, outs, sems):
            if fa is not None:
                fa(ins[:ia], outs[:oa], sems[:sa])
            if fb is not None:
                fb(ins[ia:], outs[oa:], sems[sa:])
        return run

    middle = both(a.middle, b.middle) if (a.middle or b.middle) else None
    return _Carry(a.inputs + b.inputs, a.out_shape + b.out_shape, a.scratch + b.scratch, both(a.start, b.start),
                  both(a.finish, b.finish), middle)


def _launch(body, args, carry=None, *, name, grid, in_specs, out_specs, out_shape, scratch_shapes=(), semantics):
    in_specs, out_specs, out_shape, scratch_shapes = list(in_specs), list(out_specs), list(out_shape), list(scratch_shapes)
    if carry is None:
        res = pl.pallas_call(body, name=name, grid=grid, in_specs=in_specs, out_specs=out_specs, out_shape=out_shape,
                             scratch_shapes=scratch_shapes, compiler_params=_params(*semantics))(*args)
        return list(res), []
    ni, no, ns = len(in_specs), len(out_specs), len(scratch_shapes)
    ci, co = len(carry.inputs), len(carry.out_shape)
    total = int(np.prod(grid))

    def full(*refs):
        own_in, c_in = refs[:ni], refs[ni:ni + ci]
        own_out, c_out = refs[ni + ci:ni + ci + no], refs[ni + ci + no:ni + ci + no + co]
        own_scr, c_sem = refs[ni + ci + no + co:ni + ci + no + co + ns], refs[ni + ci + no + co + ns:]
        step = 0
        for axis, size in enumerate(grid):
            step = step * size + pl.program_id(axis)
        pl.when(step == 0)(lambda: carry.start(c_in, c_out, c_sem))
        if carry.middle is not None:
            pl.when(step == total // 2)(lambda: carry.middle(c_in, c_out, c_sem))
        body(*own_in, *own_out, *own_scr)
        pl.when(step == total - 1)(lambda: carry.finish(c_in, c_out, c_sem))

    res = pl.pallas_call(
        full, name=name, grid=grid, in_specs=in_specs + [ANY] * ci, out_specs=out_specs + [ANY] * co,
        out_shape=out_shape + carry.out_shape, scratch_shapes=scratch_shapes + carry.scratch,
        compiler_params=_params(*(["arbitrary"] * len(grid))),
    )(*args, *carry.inputs)
    return list(res[:no]), list(res[no:])


def _rms_fwd(xv, g):
    r = lax.rsqrt(jnp.mean(xv * xv, axis=-1, keepdims=True) + NORM_EPS)
    return xv * r, r


def _rms_bwd(dn, xh, r, g):
    dxh = dn * g
    dx = r * (dxh - xh * jnp.mean(dxh * xh, axis=-1, keepdims=True))
    return dx, jnp.sum(dn * xh, axis=0, keepdims=True)


def _ffn_fwd(x, g, wup_t, wdown, tm, carry=None):
    t, d = x.shape
    f = wdown.shape[0]

    def body(x_ref, g_ref, wup_ref, wdn_ref, h_ref, ab_ref, n_ref, act_ref):
        xv = x_ref[...]
        xh, _ = _rms_fwd(xv, g_ref[...])
        n = (xh * g_ref[...]).astype(BF16)
        n_ref[...] = n
        for c in range(f // FF_CHUNK):
            lo, hi = c * FF_CHUNK, (c + 1) * FF_CHUNK
            a = _nt(n, wup_ref[lo:hi, :])
            b = _nt(n, wup_ref[f + lo:f + hi, :])
            ab_ref[:, lo:hi] = a.astype(BF16)
            ab_ref[:, f + lo:f + hi] = b.astype(BF16)
            act_ref[:, lo:hi] = (a * jax.nn.sigmoid(a) * b).astype(BF16)
        h_ref[...] = xv + 0.5 * _nn(act_ref[...], wdn_ref[...])

    return _launch(
        body, (x, g, wup_t, wdown), carry, name="ffn_fwd", grid=(t // tm,),
        in_specs=[_rows(tm, d), _resident((1, d)), _resident((2 * f, d)), _resident((f, d))],
        out_specs=[_rows(tm, d), _rows(tm, 2 * f), _rows(tm, d)],
        out_shape=[jax.ShapeDtypeStruct((t, d), F32), jax.ShapeDtypeStruct((t, 2 * f), BF16),
                   jax.ShapeDtypeStruct((t, d), BF16)],
        scratch_shapes=[pltpu.VMEM((tm, f), BF16)], semantics=("parallel",))


def _ffn_bwd(dh, x, g, ab, wup_t, wdown, tm, carry=None):
    t, d = x.shape
    f = wdown.shape[0]

    def body(dh_ref, x_ref, g_ref, ab_ref, wup_ref, wdn_ref, dx_ref, dab_ref, act_ref, dhb_ref, dg_ref):
        dhv = dh_ref[...]
        dhb = dhv.astype(BF16)
        dhb_ref[...] = dhb
        for c in range(f // FF_CHUNK):
            lo, hi = c * FF_CHUNK, (c + 1) * FF_CHUNK
            dact = 0.5 * _nt(dhb, wdn_ref[lo:hi, :])
            a = ab_ref[:, lo:hi].astype(F32)
            b = ab_ref[:, f + lo:f + hi].astype(F32)
            s = jax.nn.sigmoid(a)
            sil = a * s
            act_ref[:, lo:hi] = (sil * b).astype(BF16)
            dab_ref[:, lo:hi] = (dact * b * (s * (1.0 + a * (1.0 - s)))).astype(BF16)
            dab_ref[:, f + lo:f + hi] = (dact * sil).astype(BF16)
        dn = _nn(dab_ref[...], wup_ref[...])
        xv = x_ref[...]
        xh, r = _rms_fwd(xv, g_ref[...])
        dx, dg = _rms_bwd(dn, xh, r, g_ref[...])
        dx_ref[...] = dhv + dx

        @pl.when(pl.program_id(0) == 0)
        def _():
            dg_ref[...] = jnp.zeros_like(dg_ref)

        dg_ref[...] += dg

    return _launch(
        body, (dh, x, g, ab, wup_t, wdown), carry, name="ffn_bwd", grid=(t // tm,),
        in_specs=[_rows(tm, d), _rows(tm, d), _resident((1, d)), _rows(tm, 2 * f), _resident((2 * f, d)),
                  _resident((f, d))],
        out_specs=[_rows(tm, d), _rows(tm, 2 * f), _rows(tm, f), _rows(tm, d), pl.BlockSpec((1, d), lambda i: (0, 0))],
        out_shape=[jax.ShapeDtypeStruct((t, d), F32), jax.ShapeDtypeStruct((t, 2 * f), BF16),
                   jax.ShapeDtypeStruct((t, f), BF16), jax.ShapeDtypeStruct((t, d), BF16),
                   jax.ShapeDtypeStruct((1, d), F32)],
        semantics=("arbitrary",))


def _wgrad(lhs, rhs, scale, bm, tk, name, carry=None):
    t, m = lhs.shape
    n = rhs.shape[1]
    steps = t // tk
    chunk = bm if bm <= 2048 else bm // 2

    def body(l_ref, r_ref, o_ref, acc_ref):
        @pl.when(pl.program_id(1) == 0)
        def _():
            acc_ref[...] = jnp.zeros_like(acc_ref)

        for lo in range(0, bm, chunk):
            acc_ref[lo:lo + chunk, :] += _tn(l_ref[:, lo:lo + chunk], r_ref[...])

        @pl.when(pl.program_id(1) == steps - 1)
        def _():
            o_ref[...] = (scale * acc_ref[...]).astype(o_ref.dtype)

    return _launch(
        body, (lhs, rhs), carry, name=name, grid=(m // bm, steps),
        in_specs=[pl.BlockSpec((tk, bm), lambda i, k: (k, i)), pl.BlockSpec((tk, n), lambda i, k: (k, 0))],
        out_specs=[pl.BlockSpec((bm, n), lambda i, k: (i, 0))],
        out_shape=[jax.ShapeDtypeStruct((m, n), WIRE)],
        scratch_shapes=[pltpu.VMEM((bm, n), F32)], semantics=("parallel", "arbitrary"))


def _mix_in_fwd(h, g, win_t, tm):
    t, d = h.shape

    def body(h_ref, g_ref, w_ref, u_ref, q_ref, kv_ref, z_ref, gate_ref):
        xh, _ = _rms_fwd(h_ref[...], g_ref[...])
        u = (xh * g_ref[...]).astype(BF16)
        u_ref[...] = u
        q_ref[...] = _nt(u, w_ref[0:OFF_KV, :]).astype(BF16)
        kv_ref[...] = _nt(u, w_ref[OFF_KV:OFF_Z, :]).astype(BF16)
        z_ref[...] = _nt(u, w_ref[OFF_Z:OFF_GATE, :])
        gate_ref[...] = _nt(u, w_ref[OFF_GATE:IN_WIDTH, :])

    return pl.pallas_call(
        body, name="mix_in_fwd", grid=(t // tm,),
        in_specs=[_rows(tm, d), _resident((1, d)), _resident((IN_WIDTH, d))],
        out_specs=[_rows(tm, d), _rows(tm, ATTN_WIDTH), _rows(tm, 2 * KV_WIDTH), _rows(tm, POOL_WIDTH),
                   _rows(tm, 2 * D_MODEL)],
        out_shape=[jax.ShapeDtypeStruct((t, d), BF16), jax.ShapeDtypeStruct((t, ATTN_WIDTH), BF16),
                   jax.ShapeDtypeStruct((t, 2 * KV_WIDTH), BF16), jax.ShapeDtypeStruct((t, POOL_WIDTH), F32),
                   jax.ShapeDtypeStruct((t, 2 * D_MODEL), F32)],
        compiler_params=_params("parallel"),
    )(h, g, win_t)


ALIBI_SLOPES = tuple(float(s) for s in (2.0 ** (-8.0 * np.arange(1, N_Q_HEADS + 1, dtype=np.float32) / N_Q_HEADS)))


def _attn_dist():
    return jnp.asarray(((np.arange(BLOCK)[:, None] - np.arange(BLOCK)[None, :]) % BLOCK).astype(np.float32))


def _own_block():
    shape = (BLOCK, BLOCK)
    return lax.broadcasted_iota(jnp.int32, shape, 1) <= lax.broadcasted_iota(jnp.int32, shape, 0)


def _fold(band2, own):
    return jnp.where(own, band2[:, BLOCK:], band2[:, :BLOCK])


def _unfold(x, own):
    zero = jnp.zeros_like(x)
    return jnp.concatenate([jnp.where(own, zero, x), jnp.where(own, x, zero)], axis=1)


def _low_half(shape):
    return lax.broadcasted_iota(jnp.int32, shape, len(shape) - 1) < HEAD_DIM


def _both_halves(band, kv_head):
    low = _low_half(band.shape)
    swapped = pltpu.roll(band, HEAD_DIM, 1)
    return jnp.where(low, band, swapped) if kv_head == 0 else jnp.where(low, swapped, band)


def _own_half(ref, head):
    v = ref[:, LANES * (head // 2):LANES * (head // 2 + 1)]
    low = _low_half(v.shape)
    return jnp.where(low if head % 2 == 0 else jnp.logical_not(low), v, jnp.zeros_like(v))


def _head_scores(q_ref, kk, dist, head, first, own):
    s2 = _nt(_own_half(q_ref, head) * HEAD_DIM ** -0.5, kk)
    before = jnp.where(first, -jnp.inf, s2[:, :BLOCK])
    return jnp.where(own, s2[:, BLOCK:], before) - ALIBI_SLOPES[head] * dist


def _softmax_sink(s, sink):
    m = jnp.maximum(jnp.max(s, axis=-1, keepdims=True), sink)
    p = jnp.exp(s - m)
    psink = jnp.exp(sink - m)
    inv = 1.0 / (jnp.sum(p, axis=-1, keepdims=True) + psink)
    return p * inv, psink * inv


def _bands(kvc_ref, kvp_ref):
    kband = jnp.concatenate([kvp_ref[:, 0:LANES], kvc_ref[:, 0:LANES]], axis=0)
    vband = jnp.concatenate([kvp_ref[:, LANES:2 * LANES], kvc_ref[:, LANES:2 * LANES]], axis=0)
    return ([_both_halves(kband, hk) for hk in range(N_KV_HEADS)],
            [_both_halves(vband, hk) for hk in range(N_KV_HEADS)])


SMEM = pl.BlockSpec(memory_space=pltpu.SMEM)
HEADS = range(N_Q_HEADS)


def _attn_fwd(q, kv, dist, sinks):
    t = q.shape[0]

    def body(q_ref, kvc_ref, kvp_ref, dist_ref, sink_ref, o_ref, s_scr, p_scr):
        first = pl.program_id(0) == 0
        own = _own_block()
        dist_v = dist_ref[...]
        kk, vv = _bands(kvc_ref, kvp_ref)
        for head in HEADS:
            s_scr[head] = _head_scores(q_ref, kk[head // Q_PER_KV], dist_v, head, first, own)
        for head in HEADS:
            probs, _ = _softmax_sink(s_scr[head], sink_ref[head])
            p_scr[head] = _unfold(probs.astype(BF16), own)
        for pair in range(N_Q_HEADS // 2):
            even = _nn(p_scr[2 * pair], vv[2 * pair // Q_PER_KV])
            odd = _nn(p_scr[2 * pair + 1], vv[2 * pair // Q_PER_KV])
            o_ref[:, LANES * pair:LANES * (pair + 1)] = jnp.where(_low_half(even.shape), even, odd).astype(BF16)

    return pl.pallas_call(
        body, name="attn_fwd", grid=(t // BLOCK,),
        in_specs=[_rows(BLOCK, ATTN_WIDTH), _rows(BLOCK, 2 * KV_WIDTH),
                  pl.BlockSpec((BLOCK, 2 * KV_WIDTH), lambda i: (jnp.maximum(i - 1, 0), 0)),
                  _resident(dist.shape), SMEM],
        out_specs=_rows(BLOCK, ATTN_WIDTH),
        out_shape=jax.ShapeDtypeStruct((t, ATTN_WIDTH), BF16),
        scratch_shapes=[pltpu.VMEM((N_Q_HEADS, BLOCK, BLOCK), F32), pltpu.VMEM((N_Q_HEADS, BLOCK, 2 * BLOCK), BF16)],
        compiler_params=_params("parallel"),
    )(q, kv, kv, dist, sinks)


def _pool_counts(tm, width):
    row = pl.program_id(0) * tm + lax.broadcasted_iota(jnp.int32, (tm, 1), 0)
    return jnp.minimum(row + 1, width).astype(F32)


def _trailing_sums(zz, group):
    s = zz
    for k in range(group + 1):
        s = s + pltpu.roll(s, 1 << k, 0)
    return s


def _leading_sums(zz, group):
    rows = zz.shape[0]
    s = zz
    for k in range(group + 1):
        s = s + pltpu.roll(s, rows - (1 << k), 0)
    return s


def _mix_out_fwd(attn, z, gate, h, wattn, wmix, scale, wpool_t, wout, tm):
    t, d = h.shape

    def body(attn_ref, z_ref, halo_ref, gate_ref, h_ref, wattn_ref, wmix_ref, scale_ref, wpool_ref, wout_ref,
             h2_ref, a_ref, p_ref, merged_ref, ms_ref, pooled_ref):
        halo = jnp.where(pl.program_id(0) == 0, 0.0, halo_ref[...])
        for gi, width in enumerate(POOL_WINDOWS):
            lo, hi = gi * POOL_GROUP, (gi + 1) * POOL_GROUP
            zg = z_ref[:, lo:hi]
            sums = _trailing_sums(jnp.concatenate([halo[:, lo:hi], zg], axis=0), gi)[HALO:, :]
            pooled = (sums / _pool_counts(tm, width) - zg).astype(BF16)
            pooled_ref[:, lo:hi] = pooled
            ms_ref[:, lo:hi] = (_nn(pooled, wmix_ref[gi]) * scale_ref[:, lo:hi]).astype(BF16)
        p = _nt(ms_ref[...], wpool_ref[...])
        a = _nn(attn_ref[...], wattn_ref[...])
        a_ref[...] = a
        p_ref[...] = p
        merged = (jax.nn.sigmoid(gate_ref[:, 0:d]) * a + jax.nn.sigmoid(gate_ref[:, d:2 * d]) * p).astype(BF16)
        merged_ref[...] = merged
        h2_ref[...] = h_ref[...] + _nn(merged, wout_ref[...])

    halo_spec = pl.BlockSpec((HALO, POOL_WIDTH), lambda i: (jnp.maximum(i * (tm // HALO) - 1, 0), 0))
    return pl.pallas_call(
        body, name="mix_out_fwd", grid=(t // tm,),
        in_specs=[_rows(tm, ATTN_WIDTH), _rows(tm, POOL_WIDTH), halo_spec, _rows(tm, 2 * d), _rows(tm, d),
                  _resident(wattn.shape), _resident(wmix.shape), _resident(scale.shape), _resident(wpool_t.shape),
                  _resident(wout.shape)],
        out_specs=[_rows(tm, d), _rows(tm, d), _rows(tm, d), _rows(tm, d), _rows(tm, POOL_WIDTH),
                   _rows(tm, POOL_WIDTH)],
        out_shape=[jax.ShapeDtypeStruct((t, d), F32), jax.ShapeDtypeStruct((t, d), F32),
                   jax.ShapeDtypeStruct((t, d), F32), jax.ShapeDtypeStruct((t, d), BF16),
                   jax.ShapeDtypeStruct((t, POOL_WIDTH), BF16), jax.ShapeDtypeStruct((t, POOL_WIDTH), BF16)],
        compiler_params=_params("parallel"),
    )(attn, z, z, gate, h, wattn, wmix, scale, wpool_t, wout)


def _loss_head(h, g, target, tm):
    t, d = h.shape

    def body(h_ref, g_ref, tgt_ref, loss_ref, dh_ref, dg_ref):
        xh, r = _rms_fwd(h_ref[...], g_ref[...])
        err = xh * g_ref[...] - tgt_ref[...]
        part = 0.5 * jnp.sum(jnp.mean(err * err, axis=-1, keepdims=True), axis=0, keepdims=True)
        dx, dg = _rms_bwd(err * (1.0 / d), xh, r, g_ref[...])
        dh_ref[...] = dx

        @pl.when(pl.program_id(0) == 0)
        def _():
            dg_ref[...] = jnp.zeros_like(dg_ref)
            loss_ref[...] = jnp.zeros_like(loss_ref)

        dg_ref[...] += dg
        loss_ref[...] += jnp.broadcast_to(part, loss_ref.shape)

    return pl.pallas_call(
        body, name="loss_head", grid=(t // tm,),
        in_specs=[_rows(tm, d), _resident((1, d)), _rows(tm, d)],
        out_specs=[pl.BlockSpec((1, LANES), lambda i: (0, 0)), _rows(tm, d), pl.BlockSpec((1, d), lambda i: (0, 0))],
        out_shape=[jax.ShapeDtypeStruct((1, LANES), F32), jax.ShapeDtypeStruct((t, d), F32),
                   jax.ShapeDtypeStruct((1, d), F32)],
        compiler_params=_params("arbitrary"),
    )(h, g, target)


def _mix_out_bwd(dh, gate, a, p, pooled, wattn, wmix, scale, wpool_t, wout, tm):
    t, d = dh.shape

    def body(dh_ref, gate_ref, a_ref, p_ref, pooled_ref, wattn_ref, wmix_ref, scale_ref, wpool_ref, wout_ref,
             dhb_ref, dab_ref, dpb_ref, dattn_ref, dgate_ref, dpooled_ref, dwmix_ref, dscale_ref):
        @pl.when(pl.program_id(0) == 0)
        def _():
            dwmix_ref[...] = jnp.zeros_like(dwmix_ref)
            dscale_ref[...] = jnp.zeros_like(dscale_ref)

        dhb = dh_ref[...].astype(BF16)
        dhb_ref[...] = dhb
        dm = _nt(dhb, wout_ref[...])
        sa = jax.nn.sigmoid(gate_ref[:, 0:d])
        sp = jax.nn.sigmoid(gate_ref[:, d:2 * d])
        da = (dm * sa).astype(BF16)
        dp = (dm * sp).astype(BF16)
        dab_ref[...] = da
        dpb_ref[...] = dp
        dgate_ref[:, 0:d] = (dm * a_ref[...] * (sa * (1.0 - sa))).astype(BF16)
        dgate_ref[:, d:2 * d] = (dm * p_ref[...] * (sp * (1.0 - sp))).astype(BF16)
        dattn_ref[...] = _nt(da, wattn_ref[...]).astype(BF16)
        dms = _nn(dp, wpool_ref[...])
        for gi in range(len(POOL_WINDOWS)):
            lo, hi = gi * POOL_GROUP, (gi + 1) * POOL_GROUP
            pooled_g = pooled_ref[:, lo:hi]
            mixed = _nn(pooled_g, wmix_ref[gi])
            dscale_ref[:, lo:hi] += jnp.sum(dms[:, lo:hi] * mixed, axis=0, keepdims=True)
            dmixed = (dms[:, lo:hi] * scale_ref[:, lo:hi]).astype(BF16)
            dwmix_ref[gi] += _tn(pooled_g, dmixed)
            dpooled_ref[:, lo:hi] = _nt(dmixed, wmix_ref[gi])

    acc = lambda shape: pl.BlockSpec(shape, lambda i: (0,) * len(shape))
    return pl.pallas_call(
        body, name="mix_out_bwd", grid=(t // tm,),
        in_specs=[_rows(tm, d), _rows(tm, 2 * d), _rows(tm, d), _rows(tm, d), _rows(tm, POOL_WIDTH),
                  _resident(wattn.shape), _resident(wmix.shape), _resident(scale.shape), _resident(wpool_t.shape),
                  _resident(wout.shape)],
        out_specs=[_rows(tm, d), _rows(tm, d), _rows(tm, d), _rows(tm, ATTN_WIDTH), _rows(tm, 2 * d),
                   _rows(tm, POOL_WIDTH), acc(wmix.shape), acc((1, POOL_WIDTH))],
        out_shape=[jax.ShapeDtypeStruct((t, d), BF16), jax.ShapeDtypeStruct((t, d), BF16),
                   jax.ShapeDtypeStruct((t, d), BF16), jax.ShapeDtypeStruct((t, ATTN_WIDTH), BF16),
                   jax.ShapeDtypeStruct((t, 2 * d), BF16), jax.ShapeDtypeStruct((t, POOL_WIDTH), F32),
                   jax.ShapeDtypeStruct(wmix.shape, F32), jax.ShapeDtypeStruct((1, POOL_WIDTH), F32)],
        compiler_params=_params("arbitrary"),
    )(dh, gate, a, p, pooled, wattn, wmix, scale, wpool_t, wout)


def _fold_halves(x):
    return x + pltpu.roll(x, HEAD_DIM, 1)


def _attn_bwd(q, kv, dattn, dist, sinks, carry=None):
    t = q.shape[0]

    def body(q_ref, kvc_ref, kvp_ref, do_ref, dist_ref, sink_ref, dq_ref, dkv_own_ref, dkv_prev_ref, dsink_ref,
             s_scr, dp_scr, p_scr, ds_scr):
        first = pl.program_id(0) == 0

        @pl.when(first)
        def _():
            dsink_ref[...] = jnp.zeros_like(dsink_ref)

        own = _own_block()
        dist_v = dist_ref[...]
        kk, vv = _bands(kvc_ref, kvp_ref)
        lane = lax.broadcasted_iota(jnp.int32, (1, LANES), 1)
        for head in HEADS:
            hk = head // Q_PER_KV
            s_scr[head] = _head_scores(q_ref, kk[hk], dist_v, head, first, own)
            dp_scr[head] = _fold(_nt(_own_half(do_ref, head), vv[hk]), own)
        dsink = jnp.zeros((1, LANES), F32)
        for head in HEADS:
            probs, psink = _softmax_sink(s_scr[head], sink_ref[head])
            dprobs = dp_scr[head]
            rowdot = jnp.sum(probs * dprobs, axis=-1, keepdims=True)
            p_scr[head] = _unfold(probs.astype(BF16), own)
            ds_scr[head] = _unfold((probs * (dprobs - rowdot)).astype(BF16), own)
            dsink = dsink + jnp.where(lane == head, jnp.sum(-psink * rowdot, axis=0, keepdims=True), 0.0)
        dk_heads, dv_heads = [], []
        for hk in range(N_KV_HEADS):
            dk_t = jnp.zeros((LANES, 2 * BLOCK), F32)
            dv_t = jnp.zeros((LANES, 2 * BLOCK), F32)
            for pair in range(Q_PER_KV // 2):
                col = LANES * (hk * (Q_PER_KV // 2) + pair)
                q_t = (q_ref[:, col:col + LANES] * HEAD_DIM ** -0.5).T
                do_t = do_ref[:, col:col + LANES].T
                dqs = []
                for head in (hk * Q_PER_KV + 2 * pair, hk * Q_PER_KV + 2 * pair + 1):
                    mine = (lax.broadcasted_iota(jnp.int32, q_t.shape, 0) < HEAD_DIM) == (head % 2 == 0)
                    dv_t = dv_t + _nn(jnp.where(mine, do_t, jnp.zeros_like(do_t)), p_scr[head])
                    dk_t = dk_t + _nn(jnp.where(mine, q_t, jnp.zeros_like(q_t)), ds_scr[head])
                    dqs.append(_nn(ds_scr[head], kk[hk]))
                dq_pair = jnp.where(_low_half(dqs[0].shape), dqs[0], dqs[1])
                dq_ref[:, col:col + LANES] = (dq_pair * HEAD_DIM ** -0.5).astype(BF16)
            dk_heads.append(_fold_halves(dk_t.T))
            dv_heads.append(_fold_halves(dv_t.T))
        low = _low_half(dk_heads[0].shape)
        dkv = jnp.concatenate([jnp.where(low, dk_heads[0], dk_heads[1]), jnp.where(low, dv_heads[0], dv_heads[1])],
                              axis=1)
        dkv_prev_ref[...] = dkv[0:BLOCK, :]
        dkv_own_ref[...] = dkv[BLOCK:2 * BLOCK, :]
        dsink_ref[...] += dsink

    return _launch(
        body, (q, kv, kv, dattn, dist, sinks), carry, name="attn_bwd", grid=(t // BLOCK,),
        in_specs=[_rows(BLOCK, ATTN_WIDTH), _rows(BLOCK, 2 * KV_WIDTH),
                  pl.BlockSpec((BLOCK, 2 * KV_WIDTH), lambda i: (jnp.maximum(i - 1, 0), 0)),
                  _rows(BLOCK, ATTN_WIDTH), _resident(dist.shape), SMEM],
        out_specs=[_rows(BLOCK, ATTN_WIDTH), _rows(BLOCK, 2 * KV_WIDTH), _rows(BLOCK, 2 * KV_WIDTH),
                   pl.BlockSpec((1, LANES), lambda i: (0, 0))],
        out_shape=[jax.ShapeDtypeStruct((t, ATTN_WIDTH), BF16), jax.ShapeDtypeStruct((t, 2 * KV_WIDTH), F32),
                   jax.ShapeDtypeStruct((t, 2 * KV_WIDTH), F32), jax.ShapeDtypeStruct((1, LANES), F32)],
        scratch_shapes=[pltpu.VMEM((N_Q_HEADS, BLOCK, BLOCK), F32), pltpu.VMEM((N_Q_HEADS, BLOCK, BLOCK), F32),
                        pltpu.VMEM((N_Q_HEADS, BLOCK, 2 * BLOCK), BF16),
                        pltpu.VMEM((N_Q_HEADS, BLOCK, 2 * BLOCK), BF16)],
        semantics=("arbitrary",))


def _mix_in_bwd(dq, dkv_own, dkv_prev, dpooled, dgate, h, g, win_t, dh_res, tm, carry=None):
    t, d = h.shape
    nt = t // tm

    def body(dq_ref, own_ref, prev_ref, prev_next_ref, dpool_ref, halo_ref, dgate_ref, h_ref, g_ref, w_ref, res_ref,
             dproj_ref, dh_ref, dg_ref):
        i = pl.program_id(0)
        last = i == nt - 1
        dproj_ref[:, 0:OFF_KV] = dq_ref[...]
        from_next = jnp.where(last, 0.0, prev_next_ref[...])
        if tm > BLOCK:
            from_next = jnp.concatenate([prev_ref[BLOCK:tm, :], from_next], axis=0)
        dproj_ref[:, OFF_KV:OFF_Z] = (own_ref[...] + from_next).astype(BF16)
        halo = jnp.where(last, 0.0, halo_ref[...])
        for gi, width in enumerate(POOL_WINDOWS):
            lo, hi = gi * POOL_GROUP, (gi + 1) * POOL_GROUP
            dpg = dpool_ref[:, lo:hi]
            scaled = jnp.concatenate([dpg / _pool_counts(tm, width), halo[:, lo:hi] / float(width)], axis=0)
            dz = _leading_sums(scaled, gi)[0:tm, :] - dpg
            dproj_ref[:, OFF_Z + lo:OFF_Z + hi] = dz.astype(BF16)
        dproj_ref[:, OFF_GATE:IN_WIDTH] = dgate_ref[...]
        du = _nn(dproj_ref[...], w_ref[...])
        xh, r = _rms_fwd(h_ref[...], g_ref[...])
        dx, dg = _rms_bwd(du, xh, r, g_ref[...])
        dh_ref[...] = res_ref[...] + dx

        @pl.when(i == 0)
        def _():
            dg_ref[...] = jnp.zeros_like(dg_ref)

        dg_ref[...] += dg

    per = tm // BLOCK
    next_block = pl.BlockSpec((BLOCK, 2 * KV_WIDTH), lambda i: (jnp.minimum((i + 1) * per, t // BLOCK - 1), 0))
    next_halo = pl.BlockSpec((HALO, POOL_WIDTH), lambda i: (jnp.minimum((i + 1) * (tm // HALO), t // HALO - 1), 0))
    return _launch(
        body, (dq, dkv_own, dkv_prev, dkv_prev, dpooled, dpooled, dgate, h, g, win_t, dh_res), carry,
        name="mix_in_bwd", grid=(nt,),
        in_specs=[_rows(tm, ATTN_WIDTH), _rows(tm, 2 * KV_WIDTH), _rows(tm, 2 * KV_WIDTH), next_block,
                  _rows(tm, POOL_WIDTH), next_halo, _rows(tm, 2 * d), _rows(tm, d), _resident((1, d)),
                  _resident((IN_WIDTH, d)), _rows(tm, d)],
        out_specs=[_rows(tm, IN_WIDTH), _rows(tm, d), pl.BlockSpec((1, d), lambda i: (0, 0))],
        out_shape=[jax.ShapeDtypeStruct((t, IN_WIDTH), BF16), jax.ShapeDtypeStruct((t, d), F32),
                   jax.ShapeDtypeStruct((1, d), F32)],
        semantics=("arbitrary",))


BIG = (("wup1_t", "ffn1_w_up", True), ("wdown1", "ffn1_w_down", False), ("win_t", "w_in", True),
       ("wattn", "w_attn_up", False), ("wpool_t", "w_pool_up", True), ("wout", "w_out", False),
       ("wup2_t", "ffn2_w_up", True), ("wdown2", "ffn2_w_down", False))
ANY = pl.BlockSpec(memory_space=pl.ANY)
WIRE = BF16


def _place():
    return lax.axis_index("x"), lax.axis_index("y"), lax.axis_index("c")


def _peer(k):
    x, y, c = _place()
    return x ^ (k >> 2), y ^ ((k >> 1) & 1), c ^ (k & 1)


def _index(px, py, pc):
    return 4 * px + 2 * py + pc


def _gather_carry(shards):
    n = len(shards)

    def tools(ins, outs, sems):
        send_sems, recv_sems, local_sems = sems
        x, y, c = _place()
        chips = [(1 - x, y), (x, 1 - y), (1 - x, 1 - y)]

        def rows(w, px, py, pc):
            r = ins[w].shape[0]
            return outs[w].at[pl.ds(_index(px, py, pc) * r, r), :]

        def copy(w, k, block, to, src=None):
            return pltpu.make_async_remote_copy(
                src_ref=rows(w, *block) if src is None else src, dst_ref=rows(w, *block),
                send_sem=send_sems.at[w, k], recv_sem=recv_sems.at[w, k], device_id=to, device_id_type=MESH)

        def own(w):
            return ([pltpu.make_async_copy(ins[w], rows(w, x, y, c), local_sems.at[w]),
                     copy(w, 0, (x, y, c), (x, y, 1 - c), src=ins[w])]
                    + [copy(w, 1 + j, (x, y, c), (*chip, c), src=ins[w]) for j, chip in enumerate(chips)])

        def passed(w, j):
            return copy(w, 4 + j, (*chips[j], c), (x, y, 1 - c))

        return (x, y, c), chips, copy, own, passed

    def start(ins, outs, sems):
        _, _, _, own, _ = tools(ins, outs, sems)
        for w in range(n):
            for cp in own(w):
                cp.start()

    def middle(ins, outs, sems):
        (x, y, c), chips, copy, _, passed = tools(ins, outs, sems)
        for w in range(n):
            for j, chip in enumerate(chips):
                copy(w, 1 + j, (*chip, c), (x, y, c)).wait_recv()
                passed(w, j).start()

    def finish(ins, outs, sems):
        (x, y, c), chips, copy, own, passed = tools(ins, outs, sems)
        for w in range(n):
            copy(w, 0, (x, y, 1 - c), (x, y, c)).wait_recv()
            for j, chip in enumerate(chips):
                copy(w, 4 + j, (*chip, 1 - c), (x, y, c)).wait_recv()
        for w in range(n):
            mine, *sent = own(w)
            for cp in sent + [passed(w, j) for j in range(len(chips))]:
                cp.wait_send()
            mine.wait()

    return _Carry(
        shards, [jax.ShapeDtypeStruct((N_DEV * s.shape[0], s.shape[1]), s.dtype) for s in shards],
        [pltpu.SemaphoreType.DMA((n, N_DEV - 1)), pltpu.SemaphoreType.DMA((n, N_DEV - 1)),
         pltpu.SemaphoreType.DMA((n,))], start, finish, middle)


def _scatter_carry(grads):
    n = len(grads)

    def tools(ins, outs, sems):
        send_sems, recv_sems, local_sems = sems
        me = _index(*_place())

        def block(ref, dev):
            r = ref.shape[0] // N_DEV
            return ref.at[pl.ds(dev * r, r), :]

        def copy(w, k, landing):
            to = _peer(k)
            return pltpu.make_async_remote_copy(
                src_ref=block(ins[w], _index(*to)), dst_ref=block(outs[w], landing), send_sem=send_sems.at[w, k - 1],
                recv_sem=recv_sems.at[w, k - 1], device_id=to, device_id_type=MESH)

        def mine(w):
            return pltpu.make_async_copy(block(ins[w], me), block(outs[w], me), local_sems.at[w])

        return me, copy, mine

    def start(ins, outs, sems):
        me, copy, mine = tools(ins, outs, sems)
        for w in range(n):
            mine(w).start()
            for k in range(1, N_DEV):
                copy(w, k, me).start()

    def finish(ins, outs, sems):
        _, copy, mine = tools(ins, outs, sems)
        for w in range(n):
            for k in range(1, N_DEV):
                copy(w, k, _index(*_peer(k))).wait()
            mine(w).wait()

    return _Carry(
        grads, [jax.ShapeDtypeStruct(g.shape, g.dtype) for g in grads],
        [pltpu.SemaphoreType.DMA((n, N_DEV - 1)), pltpu.SemaphoreType.DMA((n, N_DEV - 1)),
         pltpu.SemaphoreType.DMA((n,))], start, finish)


def _small_carry(small):
    srows = small.shape[0]

    def tools(ins, outs, sems):
        send_sems, recv_sems, local_sem = sems
        me = _index(*_place())

        def slot(dev):
            return outs[0].at[pl.ds(dev * srows, srows), :]

        def copy(k, landing):
            return pltpu.make_async_remote_copy(
                src_ref=ins[0], dst_ref=slot(landing), send_sem=send_sems.at[k - 1], recv_sem=recv_sems.at[k - 1],
                device_id=_peer(k), device_id_type=MESH)

        return me, copy, pltpu.make_async_copy(ins[0], slot(me), local_sem)

    def start(ins, outs, sems):
        me, copy, mine = tools(ins, outs, sems)
        mine.start()
        for k in range(1, N_DEV):
            copy(k, me).start()

    def finish(ins, outs, sems):
        _, copy, mine = tools(ins, outs, sems)
        for k in range(1, N_DEV):
            copy(k, _index(*_peer(k))).wait()
        mine.wait()

    return _Carry([small], [jax.ShapeDtypeStruct((N_DEV * srows, LANES), small.dtype)],
                  [pltpu.SemaphoreType.DMA((N_DEV - 1,)), pltpu.SemaphoreType.DMA((N_DEV - 1,)),
                   pltpu.SemaphoreType.DMA], start, finish)


def _exchange(carry, name):
    ci = len(carry.inputs)
    co = len(carry.out_shape)

    def body(*refs):
        parts = refs[:ci], refs[ci:ci + co], refs[ci + co:]
        carry.start(*parts)
        if carry.middle is not None:
            carry.middle(*parts)
        carry.finish(*parts)

    return list(pl.pallas_call(body, name=name, in_specs=[ANY] * ci, out_specs=[ANY] * co, out_shape=carry.out_shape,
                               scratch_shapes=carry.scratch)(*carry.inputs))


def _device_sum(got, name):
    r, cols = got.shape[0] // N_DEV, got.shape[1]
    tr = r if r <= 256 else r // 2

    def body(g_ref, o_ref):
        acc = g_ref[0].astype(F32)
        for dev in range(1, N_DEV):
            acc = acc + g_ref[dev].astype(F32)
        o_ref[...] = acc

    return pl.pallas_call(
        body, name=name, grid=(r // tr,),
        in_specs=[pl.BlockSpec((N_DEV, tr, cols), lambda i: (0, i, 0))],
        out_specs=pl.BlockSpec((tr, cols), lambda i: (i, 0)),
        out_shape=jax.ShapeDtypeStruct((r, cols), F32),
        compiler_params=_params("parallel"),
    )(got.reshape(N_DEV, r, cols))


def _adamw_math(w, g, m, v):
    m = ADAM_B1 * m + (1.0 - ADAM_B1) * g
    v = ADAM_B2 * v + (1.0 - ADAM_B2) * (g * g)
    m_hat = m / (1.0 - ADAM_B1 ** ADAM_STEP)
    v_hat = v / (1.0 - ADAM_B2 ** ADAM_STEP)
    return -ADAM_LR * (m_hat / (jnp.sqrt(v_hat) + ADAM_EPS) + ADAM_WD * w), m, v


def _adamw(w, g, m, v, name):
    r, cols = w.shape
    tr = r if r <= 512 else 256

    def body(w_ref, g_ref, m_ref, v_ref, d_ref, m2_ref, v2_ref):
        d_ref[...], m2_ref[...], v2_ref[...] = _adamw_math(w_ref[...], g_ref[...], m_ref[...], v_ref[...])

    spec = pl.BlockSpec((tr, cols), lambda i: (i, 0))
    return pl.pallas_call(
        body, name=name, grid=(r // tr,), in_specs=[spec] * 4, out_specs=[spec] * 3,
        out_shape=[jax.ShapeDtypeStruct((r, cols), F32)] * 3, compiler_params=_params("parallel"),
    )(w, g, m, v)


def _small_update(gathered, w, m, v):
    rows = w.shape[0]

    def body(all_ref, w_ref, m_ref, v_ref, g_ref, d_ref, m2_ref, v2_ref):
        g = all_ref[0]
        for dev in range(1, N_DEV):
            g = g + all_ref[dev]
        g_ref[...] = g
        d_ref[...], m2_ref[...], v2_ref[...] = _adamw_math(w_ref[...], g, m_ref[...], v_ref[...])

    return pl.pallas_call(
        body, name="small_update", out_shape=[jax.ShapeDtypeStruct((rows, LANES), F32)] * 4,
        compiler_params=pltpu.CompilerParams(vmem_limit_bytes=VMEM_LIMIT),
    )(gathered.reshape(N_DEV, rows, LANES), w, m, v)


SMALL = (("pool_w_mix", 512), ("ffn1_norm", 8), ("mix_norm", 8), ("ffn2_norm", 8), ("final_norm", 8),
         ("pool_scale", 8), ("sinks", 8), ("loss", 8))
SMALL_ROWS = sum(rows for _, rows in SMALL)


def _pack_small(parts):
    out = []
    for name, rows in SMALL:
        flat = parts[name].astype(F32).reshape(-1)
        out.append(jnp.pad(flat, (0, rows * LANES - flat.shape[0])).reshape(rows, LANES))
    return jnp.concatenate(out, axis=0)


def _unpack_small(packed, shapes):
    out, row = {}, 0
    for name, rows in SMALL:
        shape = shapes[name]
        size = int(np.prod(shape)) if shape else 1
        out[name] = packed[row:row + rows].reshape(-1)[:size].reshape(shape)
        row += rows
    return out


def kernel(x, ffn1_norm, ffn1_w_up, ffn1_w_down, mix_norm, w_in, sinks, w_attn_up, pool_w_mix, pool_scale, w_pool_up, w_out, ffn2_norm, ffn2_w_up, ffn2_w_down, final_norm, loss_target, m_ffn1_norm, m_ffn1_w_up, m_ffn1_w_down, m_mix_norm, m_w_in, m_sinks, m_w_attn_up, m_pool_w_mix, m_pool_scale, m_w_pool_up, m_w_out, m_ffn2_norm, m_ffn2_w_up, m_ffn2_w_down, m_final_norm, v_ffn1_norm, v_ffn1_w_up, v_ffn1_w_down, v_mix_norm, v_w_in, v_sinks, v_w_attn_up, v_pool_w_mix, v_pool_scale, v_w_pool_up, v_w_out, v_ffn2_norm, v_ffn2_w_up, v_ffn2_w_down, v_final_norm):
    args = dict(locals())
    weight_names = ("ffn1_norm", "ffn1_w_up", "ffn1_w_down", "mix_norm", "w_in", "sinks", "w_attn_up", "pool_w_mix",
                    "pool_scale", "w_pool_up", "w_out", "ffn2_norm", "ffn2_w_up", "ffn2_w_down", "final_norm")

    shard = {k: (args[p][0].T if tr else args[p][0]).astype(BF16) for k, p, tr in BIG}
    later = [k for k, _, _ in BIG[2:]]
    big = dict(zip(("wup1_t", "wdown1"), _exchange(_gather_carry([shard["wup1_t"], shard["wdown1"]]), "gather_ffn1")))

    xs, target = x[0], loss_target[0]
    t = xs.shape[0]
    tm_f, tm_b, tk = min(512, t), min(256, t), min(1024, t)
    g1, gm, g2, gf = ffn1_norm, mix_norm, ffn2_norm, final_norm.reshape(1, D_MODEL)
    dist = _attn_dist()
    sink_v = sinks.reshape(N_Q_HEADS)
    wmix_b = pool_w_mix[0].astype(BF16)

    (h1, ab1, n1), rest = _ffn_fwd(xs, g1, big["wup1_t"], big["wdown1"], tm_f, _gather_carry([shard[k] for k in later]))
    big.update(zip(later, rest))
    u, q, kv, z, gate = _mix_in_fwd(h1, gm, big["win_t"], tm_f)
    attn = _attn_fwd(q, kv, dist, sink_v)
    h2, a, p, merged, ms, pooled = _mix_out_fwd(attn, z, gate, h1, big["wattn"], wmix_b, pool_scale, big["wpool_t"],
                                                big["wout"], tm_b)
    (h3, ab2, n2), _ = _ffn_fwd(h2, g2, big["wup2_t"], big["wdown2"], tm_f)
    loss_lanes, dh3, dgf = _loss_head(h3, gf, target, tm_f)

    got = {}
    (dh2, dab2, act2, dhb3, dg2), _ = _ffn_bwd(dh3, h2, g2, ab2, big["wup2_t"], big["wdown2"], tm_b)
    (gw_up2,), _ = _wgrad(dab2, n2, 1.0, D_FF, tk, "wgrad_up2")
    (gw_down2,), _ = _wgrad(act2, dhb3, 0.5, D_FF, tk, "wgrad_down2")
    dhb2, da_b, dp_b, dattn, dgate, dpooled, dwmix, dscale = _mix_out_bwd(
        dh2, gate, a, p, pooled, big["wattn"], wmix_b, pool_scale, big["wpool_t"], big["wout"], tm_b)
    (gw_out,), _ = _wgrad(merged, dhb2, 1.0, D_MODEL, tk, "wgrad_out")
    (gw_attn,), _ = _wgrad(attn, da_b, 1.0, D_MODEL, tk, "wgrad_attn")
    (gw_pool,), _ = _wgrad(dp_b, ms, 1.0, D_MODEL, tk, "wgrad_pool")
    (dq, dkv_own, dkv_prev, dsinks), (got["wup2_t"], got["wdown2"]) = _attn_bwd(
        q, kv, dattn, dist, sink_v, _scatter_carry([gw_up2, gw_down2]))
    (dproj, dh1, dgm), (got["wout"], got["wattn"], got["wpool_t"]) = _mix_in_bwd(
        dq, dkv_own, dkv_prev, dpooled, dgate, h1, gm, big["win_t"], dh2, tm_b,
        _scatter_carry([gw_out, gw_attn, gw_pool]))
    (gw_in,), _ = _wgrad(dproj, u, 1.0, IN_WIDTH // 2, tk, "wgrad_in")
    (dx, dab1, act1, dhb1, dg1), (got["win_t"],) = _ffn_bwd(dh1, xs, g1, ab1, big["wup1_t"], big["wdown1"], tm_b,
                                                           _scatter_carry([gw_in]))
    (gw_down1,), _ = _wgrad(act1, dhb1, 0.5, D_FF, tk, "wgrad_down1")
    small_parts = {"pool_w_mix": dwmix, "ffn1_norm": dg1, "mix_norm": dgm, "ffn2_norm": dg2, "final_norm": dgf,
                   "pool_scale": dscale, "sinks": dsinks[:, :N_Q_HEADS], "loss": loss_lanes[:, :1]}
    (gw_up1,), (got["wdown1"], small_all) = _wgrad(
        dab1, n1, 1.0, D_FF, tk, "wgrad_up1", _merge(_scatter_carry([gw_down1]), _small_carry(_pack_small(small_parts))))
    (got["wup1_t"],) = _exchange(_scatter_carry([gw_up1]), "scatter_up1")

    grad, delta, new_m, new_v = {}, {}, {}, {}
    for k, p, tr in BIG:
        g = _device_sum(got[k], "device_sum_" + k)
        g = g.T if tr else g
        d, m2, v2 = _adamw(args[p][0], g, args["m_" + p][0], args["v_" + p][0], "adamw_" + k)
        grad[p], delta[p], new_m[p], new_v[p] = g[None], d[None], m2[None], v2[None]

    shapes = {name: args[name].shape for name, _ in SMALL if name != "loss"}
    shapes["loss"] = ()
    packed = {pre: _pack_small({**{name: args[pre + name] for name, _ in SMALL if name != "loss"},
                                "loss": jnp.zeros((), F32)}) for pre in ("", "m_", "v_")}
    g_s, d_s, m_s, v_s = _small_update(small_all, packed[""], packed["m_"], packed["v_"])
    g_small, d_small, m_small, v_small = (_unpack_small(a, shapes) for a in (g_s, d_s, m_s, v_s))
    for name, _ in SMALL:
        if name != "loss":
            grad[name], delta[name], new_m[name], new_v[name] = (
                g_small[name], d_small[name], m_small[name], v_small[name])

    return (g_small["loss"], dx[None], *[grad[n] for n in weight_names], *[delta[n] for n in weight_names],
            *[new_m[n] for n in weight_names], *[new_v[n] for n in weight_names])
```

```python
import functools

import jax
import jax.numpy as jnp
import numpy as np
from jax import lax
from jax.experimental import pallas as pl
from jax.experimental.pallas import tpu as pltpu

F32 = jnp.float32
BF16 = jnp.bfloat16

D_MODEL = 1024
D_FF = 2816
N_Q_HEADS = 16
N_KV_HEADS = 2
Q_PER_KV = N_Q_HEADS // N_KV_HEADS
HEAD_DIM = 64
BLOCK = 128
ATTN_WIDTH = N_Q_HEADS * HEAD_DIM
KV_WIDTH = N_KV_HEADS * HEAD_DIM
POOL_WINDOWS = (2, 4, 8, 16)
POOL_GROUP = 128
POOL_WIDTH = 512
HALO = 16
IN_WIDTH = ATTN_WIDTH + 2 * KV_WIDTH + POOL_WIDTH + 2 * D_MODEL
OFF_KV = ATTN_WIDTH
OFF_Z = ATTN_WIDTH + 2 * KV_WIDTH
OFF_GATE = OFF_Z + POOL_WIDTH
NORM_EPS = 1e-6
ADAM_LR = 0.001
ADAM_B1 = 0.9
ADAM_B2 = 0.999
ADAM_EPS = 1e-08
ADAM_WD = 0.01
ADAM_STEP = 10

N_DEV = 8
N_CHIP = 4
LANES = 128
FF_CHUNK = 256
VMEM_LIMIT = 56 * 1024 * 1024
MESH = pl.DeviceIdType.MESH


def _nn(a, b):
    return jnp.dot(a, b, preferred_element_type=F32)


def _nt(a, b):
    return lax.dot_general(a, b, (((1,), (1,)), ((), ())), preferred_element_type=F32)


def _tn(a, b):
    return lax.dot_general(a, b, (((0,), (0,)), ((), ())), preferred_element_type=F32)


def _params(*sem):
    return pltpu.CompilerParams(dimension_semantics=sem, vmem_limit_bytes=VMEM_LIMIT)


def _resident(shape):
    return pl.BlockSpec(shape, lambda *_: (0,) * len(shape), pipeline_mode=pl.Buffered(1))


def _rows(tm, cols):
    return pl.BlockSpec((tm, cols), lambda i: (i, 0))


class _Carry:
    def __init__(self, inputs, out_shape, scratch, start, finish, middles=()):
        self.inputs, self.out_shape, self.scratch = list(inputs), list(out_shape), list(scratch)
        self.start, self.finish, self.middles = start, finish, list(middles)


def _merge(a, b):
    ia, oa, sa = len(a.inputs), len(a.out_shape), len(a.scratch)

    def first(fn):
        return lambda ins, outs, sems: fn(ins[:ia], outs[:oa], sems[:sa])

    def second(fn):
        return lambda ins, outs, sems: fn(ins[ia:], outs[oa:], sems[sa:])

    def both(fa, fb):
        def run(ins, outs, sems):
            first(fa)(ins, outs, sems)
            second(fb)(ins, outs, sems)
        return run

    middles = [(f, first(fn)) for f, fn in a.middles] + [(f, second(fn)) for f, fn in b.middles]
    return _Carry(a.inputs + b.inputs, a.out_shape + b.out_shape, a.scratch + b.scratch, both(a.start, b.start),
                  both(a.finish, b.finish), middles)


def _launch(body, args, carry=None, *, name, grid, in_specs, out_specs, out_shape, scratch_shapes=(), semantics):
    in_specs, out_specs, out_shape, scratch_shapes = list(in_specs), list(out_specs), list(out_shape), list(scratch_shapes)
    if carry is None:
        res = pl.pallas_call(body, name=name, grid=grid, in_specs=in_specs, out_specs=out_specs, out_shape=out_shape,
                             scratch_shapes=scratch_shapes, compiler_params=_params(*semantics))(*args)
        return list(res), []
    ni, no, ns = len(in_specs), len(out_specs), len(scratch_shapes)
    ci, co = len(carry.inputs), len(carry.out_shape)
    total = int(np.prod(grid))

    def full(*refs):
        own_in, c_in = refs[:ni], refs[ni:ni + ci]
        own_out, c_out = refs[ni + ci:ni + ci + no], refs[ni + ci + no:ni + ci + no + co]
        own_scr, c_sem = refs[ni + ci + no + co:ni + ci + no + co + ns], refs[ni + ci + no + co + ns:]
        step = 0
        for axis, size in enumerate(grid):
            step = step * size + pl.program_id(axis)
        pl.when(step == 0)(lambda: carry.start(c_in, c_out, c_sem))
        for fraction, fn in carry.middles:
            at = min(total - 1, int(fraction * total) + 1)
            pl.when(step == at)(lambda fn=fn: fn(c_in, c_out, c_sem))
        body(*own_in, *own_out, *own_scr)
        pl.when(step == total - 1)(lambda: carry.finish(c_in, c_out, c_sem))

    res = pl.pallas_call(
        full, name=name, grid=grid, in_specs=in_specs + [ANY] * ci, out_specs=out_specs + [ANY] * co,
        out_shape=out_shape + carry.out_shape, scratch_shapes=scratch_shapes + carry.scratch,
        compiler_params=_params(*(["arbitrary"] * len(grid))),
    )(*args, *carry.inputs)
    return list(res[:no]), list(res[no:])


def _rms_fwd(xv, g):
    r = lax.rsqrt(jnp.mean(xv * xv, axis=-1, keepdims=True) + NORM_EPS)
    return xv * r, r


def _rms_bwd(dn, xh, r, g):
    dxh = dn * g
    dx = r * (dxh - xh * jnp.mean(dxh * xh, axis=-1, keepdims=True))
    return dx, jnp.sum(dn * xh, axis=0, keepdims=True)


def _ffn_fwd(x, g, wup_t, wdown, tm, carry=None):
    t, d = x.shape
    f = wdown.shape[0]

    def body(x_ref, g_ref, wup_ref, wdn_ref, h_ref, ab_ref, n_ref, act_ref):
        xv = x_ref[...]
        xh, _ = _rms_fwd(xv, g_ref[...])
        n = (xh * g_ref[...]).astype(BF16)
        n_ref[...] = n
        for c in range(f // FF_CHUNK):
            lo, hi = c * FF_CHUNK, (c + 1) * FF_CHUNK
            a = _nt(n, wup_ref[lo:hi, :])
            b = _nt(n, wup_ref[f + lo:f + hi, :])
            ab_ref[:, lo:hi] = a.astype(BF16)
            ab_ref[:, f + lo:f + hi] = b.astype(BF16)
            act_ref[:, lo:hi] = (a * jax.nn.sigmoid(a) * b).astype(BF16)
        h_ref[...] = xv + 0.5 * _nn(act_ref[...], wdn_ref[...])

    return _launch(
        body, (x, g, wup_t, wdown), carry, name="ffn_fwd", grid=(t // tm,),
        in_specs=[_rows(tm, d), _resident((1, d)), _resident((2 * f, d)), _resident((f, d))],
        out_specs=[_rows(tm, d), _rows(tm, 2 * f), _rows(tm, d), _rows(tm, f)],
        out_shape=[jax.ShapeDtypeStruct((t, d), F32), jax.ShapeDtypeStruct((t, 2 * f), BF16),
                   jax.ShapeDtypeStruct((t, d), BF16), jax.ShapeDtypeStruct((t, f), BF16)],
        semantics=("parallel",))


def _ffn_bwd_hidden(dhb, ab, wdown, tm, carry=None):
    t, d = dhb.shape
    f = wdown.shape[0]

    def body(dh_ref, ab_ref, wdn_ref, dab_ref):
        dhb_v = dh_ref[...]
        for c in range(f // FF_CHUNK):
            lo, hi = c * FF_CHUNK, (c + 1) * FF_CHUNK
            dact = 0.5 * _nt(dhb_v, wdn_ref[lo:hi, :])
            a = ab_ref[:, lo:hi].astype(F32)
            b = ab_ref[:, f + lo:f + hi].astype(F32)
            s = jax.nn.sigmoid(a)
            dab_ref[:, lo:hi] = (dact * b * (s * (1.0 + a * (1.0 - s)))).astype(BF16)
            dab_ref[:, f + lo:f + hi] = (dact * (a * s)).astype(BF16)

    return _launch(
        body, (dhb, ab, wdown), carry, name="ffn_bwd_hidden", grid=(t // tm,),
        in_specs=[_rows(tm, d), _rows(tm, 2 * f), _resident((f, d))], out_specs=[_rows(tm, 2 * f)],
        out_shape=[jax.ShapeDtypeStruct((t, 2 * f), BF16)], semantics=("parallel",))


def _ffn_bwd_input(dab, dh, x, g, wup_t, tm):
    t, d = x.shape
    f2 = wup_t.shape[0]

    def body(dab_ref, dh_ref, x_ref, g_ref, wup_ref, dx_ref, dg_ref):
        dn = _nn(dab_ref[...], wup_ref[...])
        xh, r = _rms_fwd(x_ref[...], g_ref[...])
        dx, dg = _rms_bwd(dn, xh, r, g_ref[...])
        dx_ref[...] = dh_ref[...] + dx

        @pl.when(pl.program_id(0) == 0)
        def _():
            dg_ref[...] = jnp.zeros_like(dg_ref)

        dg_ref[...] += dg

    return pl.pallas_call(
        body, name="ffn_bwd_input", grid=(t // tm,),
        in_specs=[_rows(tm, f2), _rows(tm, d), _rows(tm, d), _resident((1, d)), _resident((f2, d))],
        out_specs=[_rows(tm, d), pl.BlockSpec((1, d), lambda i: (0, 0))],
        out_shape=[jax.ShapeDtypeStruct((t, d), F32), jax.ShapeDtypeStruct((1, d), F32)],
        compiler_params=_params("arbitrary"),
    )(dab, dh, x, g, wup_t)


def _wgrad(lhs, rhs, scale, bm, tk, name, carry=None):
    t, m = lhs.shape
    n = rhs.shape[1]
    steps = t // tk
    chunk = bm if bm <= 2048 else bm // 2

    def body(l_ref, r_ref, o_ref, acc_ref):
        @pl.when(pl.program_id(1) == 0)
        def _():
            acc_ref[...] = jnp.zeros_like(acc_ref)

        for lo in range(0, bm, chunk):
            acc_ref[lo:lo + chunk, :] += _tn(l_ref[:, lo:lo + chunk], r_ref[...])

        @pl.when(pl.program_id(1) == steps - 1)
        def _():
            o_ref[...] = (scale * acc_ref[...]).astype(o_ref.dtype)

    return _launch(
        body, (lhs, rhs), carry, name=name, grid=(m // bm, steps),
        in_specs=[pl.BlockSpec((tk, bm), lambda i, k: (k, i)), pl.BlockSpec((tk, n), lambda i, k: (k, 0))],
        out_specs=[pl.BlockSpec((bm, n), lambda i, k: (i, 0))],
        out_shape=[jax.ShapeDtypeStruct((m, n), WIRE)],
        scratch_shapes=[pltpu.VMEM((bm, n), F32)], semantics=("parallel", "arbitrary"))


def _mix_in_fwd(h, g, win_t, tm):
    t, d = h.shape

    def body(h_ref, g_ref, w_ref, u_ref, q_ref, kv_ref, z_ref, gate_ref):
        xh, _ = _rms_fwd(h_ref[...], g_ref[...])
        u = (xh * g_ref[...]).astype(BF16)
        u_ref[...] = u
        q_ref[...] = _nt(u, w_ref[0:OFF_KV, :]).astype(BF16)
        kv_ref[...] = _nt(u, w_ref[OFF_KV:OFF_Z, :]).astype(BF16)
        z_ref[...] = _nt(u, w_ref[OFF_Z:OFF_GATE, :])
        gate_ref[...] = _nt(u, w_ref[OFF_GATE:IN_WIDTH, :])

    return pl.pallas_call(
        body, name="mix_in_fwd", grid=(t // tm,),
        in_specs=[_rows(tm, d), _resident((1, d)), _resident((IN_WIDTH, d))],
        out_specs=[_rows(tm, d), _rows(tm, ATTN_WIDTH), _rows(tm, 2 * KV_WIDTH), _rows(tm, POOL_WIDTH),
                   _rows(tm, 2 * D_MODEL)],
        out_shape=[jax.ShapeDtypeStruct((t, d), BF16), jax.ShapeDtypeStruct((t, ATTN_WIDTH), BF16),
                   jax.ShapeDtypeStruct((t, 2 * KV_WIDTH), BF16), jax.ShapeDtypeStruct((t, POOL_WIDTH), F32),
                   jax.ShapeDtypeStruct((t, 2 * D_MODEL), F32)],
        compiler_params=_params("parallel"),
    )(h, g, win_t)


ALIBI_SLOPES = tuple(float(s) for s in (2.0 ** (-8.0 * np.arange(1, N_Q_HEADS + 1, dtype=np.float32) / N_Q_HEADS)))


def _attn_dist():
    return jnp.asarray(((np.arange(BLOCK)[:, None] - np.arange(BLOCK)[None, :]) % BLOCK).astype(np.float32))


def _own_block():
    shape = (BLOCK, BLOCK)
    return lax.broadcasted_iota(jnp.int32, shape, 1) <= lax.broadcasted_iota(jnp.int32, shape, 0)


def _fold(band2, own):
    return jnp.where(own, band2[:, BLOCK:], band2[:, :BLOCK])


def _unfold(x, own):
    zero = jnp.zeros_like(x)
    return jnp.concatenate([jnp.where(own, zero, x), jnp.where(own, x, zero)], axis=1)


def _low_half(shape):
    return lax.broadcasted_iota(jnp.int32, shape, len(shape) - 1) < HEAD_DIM


def _both_halves(band, kv_head):
    low = _low_half(band.shape)
    swapped = pltpu.roll(band, HEAD_DIM, 1)
    return jnp.where(low, band, swapped) if kv_head == 0 else jnp.where(low, swapped, band)


def _own_half(ref, head):
    v = ref[:, LANES * (head // 2):LANES * (head // 2 + 1)]
    low = _low_half(v.shape)
    return jnp.where(low if head % 2 == 0 else jnp.logical_not(low), v, jnp.zeros_like(v))


def _head_scores(q_ref, kk, dist, head, first, own):
    s2 = _nt(_own_half(q_ref, head) * HEAD_DIM ** -0.5, kk)
    before = jnp.where(first, -jnp.inf, s2[:, :BLOCK])
    return jnp.where(own, s2[:, BLOCK:], before) - ALIBI_SLOPES[head] * dist


def _softmax_sink(s, sink):
    m = jnp.maximum(jnp.max(s, axis=-1, keepdims=True), sink)
    p = jnp.exp(s - m)
    psink = jnp.exp(sink - m)
    inv = 1.0 / (jnp.sum(p, axis=-1, keepdims=True) + psink)
    return p * inv, psink * inv


def _bands(kvc_ref, kvp_ref):
    kband = jnp.concatenate([kvp_ref[:, 0:LANES], kvc_ref[:, 0:LANES]], axis=0)
    vband = jnp.concatenate([kvp_ref[:, LANES:2 * LANES], kvc_ref[:, LANES:2 * LANES]], axis=0)
    return ([_both_halves(kband, hk) for hk in range(N_KV_HEADS)],
            [_both_halves(vband, hk) for hk in range(N_KV_HEADS)])


SMEM = pl.BlockSpec(memory_space=pltpu.SMEM)
HEADS = range(N_Q_HEADS)


def _attn_fwd(q, kv, dist, sinks):
    t = q.shape[0]

    def body(q_ref, kvc_ref, kvp_ref, dist_ref, sink_ref, o_ref, s_scr, p_scr):
        first = pl.program_id(0) == 0
        own = _own_block()
        dist_v = dist_ref[...]
        kk, vv = _bands(kvc_ref, kvp_ref)
        for head in HEADS:
            s_scr[head] = _head_scores(q_ref, kk[head // Q_PER_KV], dist_v, head, first, own)
        for head in HEADS:
            probs, _ = _softmax_sink(s_scr[head], sink_ref[head])
            p_scr[head] = _unfold(probs.astype(BF16), own)
        for pair in range(N_Q_HEADS // 2):
            even = _nn(p_scr[2 * pair], vv[2 * pair // Q_PER_KV])
            odd = _nn(p_scr[2 * pair + 1], vv[2 * pair // Q_PER_KV])
            o_ref[:, LANES * pair:LANES * (pair + 1)] = jnp.where(_low_half(even.shape), even, odd).astype(BF16)

    return pl.pallas_call(
        body, name="attn_fwd", grid=(t // BLOCK,),
        in_specs=[_rows(BLOCK, ATTN_WIDTH), _rows(BLOCK, 2 * KV_WIDTH),
                  pl.BlockSpec((BLOCK, 2 * KV_WIDTH), lambda i: (jnp.maximum(i - 1, 0), 0)),
                  _resident(dist.shape), SMEM],
        out_specs=_rows(BLOCK, ATTN_WIDTH),
        out_shape=jax.ShapeDtypeStruct((t, ATTN_WIDTH), BF16),
        scratch_shapes=[pltpu.VMEM((N_Q_HEADS, BLOCK, BLOCK), F32), pltpu.VMEM((N_Q_HEADS, BLOCK, 2 * BLOCK), BF16)],
        compiler_params=_params("parallel"),
    )(q, kv, kv, dist, sinks)


def _pool_counts(tm, width):
    row = pl.program_id(0) * tm + lax.broadcasted_iota(jnp.int32, (tm, 1), 0)
    return jnp.minimum(row + 1, width).astype(F32)


def _trailing_sums(zz, group):
    s = zz
    for k in range(group + 1):
        s = s + pltpu.roll(s, 1 << k, 0)
    return s


def _leading_sums(zz, group):
    rows = zz.shape[0]
    s = zz
    for k in range(group + 1):
        s = s + pltpu.roll(s, rows - (1 << k), 0)
    return s


def _mix_out_fwd(attn, z, gate, h, wattn, wmix, scale, wpool_t, wout, tm):
    t, d = h.shape

    def body(attn_ref, z_ref, halo_ref, gate_ref, h_ref, wattn_ref, wmix_ref, scale_ref, wpool_ref, wout_ref,
             h2_ref, a_ref, p_ref, merged_ref, ms_ref, pooled_ref):
        halo = jnp.where(pl.program_id(0) == 0, 0.0, halo_ref[...])
        for gi, width in enumerate(POOL_WINDOWS):
            lo, hi = gi * POOL_GROUP, (gi + 1) * POOL_GROUP
            zg = z_ref[:, lo:hi]
            sums = _trailing_sums(jnp.concatenate([halo[:, lo:hi], zg], axis=0), gi)[HALO:, :]
            pooled = (sums / _pool_counts(tm, width) - zg).astype(BF16)
            pooled_ref[:, lo:hi] = pooled
            ms_ref[:, lo:hi] = (_nn(pooled, wmix_ref[gi]) * scale_ref[:, lo:hi]).astype(BF16)
        p = _nt(ms_ref[...], wpool_ref[...])
        a = _nn(attn_ref[...], wattn_ref[...])
        a_ref[...] = a
        p_ref[...] = p
        merged = (jax.nn.sigmoid(gate_ref[:, 0:d]) * a + jax.nn.sigmoid(gate_ref[:, d:2 * d]) * p).astype(BF16)
        merged_ref[...] = merged
        h2_ref[...] = h_ref[...] + _nn(merged, wout_ref[...])

    halo_spec = pl.BlockSpec((HALO, POOL_WIDTH), lambda i: (jnp.maximum(i * (tm // HALO) - 1, 0), 0))
    return pl.pallas_call(
        body, name="mix_out_fwd", grid=(t // tm,),
        in_specs=[_rows(tm, ATTN_WIDTH), _rows(tm, POOL_WIDTH), halo_spec, _rows(tm, 2 * d), _rows(tm, d),
                  _resident(wattn.shape), _resident(wmix.shape), _resident(scale.shape), _resident(wpool_t.shape),
                  _resident(wout.shape)],
        out_specs=[_rows(tm, d), _rows(tm, d), _rows(tm, d), _rows(tm, d), _rows(tm, POOL_WIDTH),
                   _rows(tm, POOL_WIDTH)],
        out_shape=[jax.ShapeDtypeStruct((t, d), F32), jax.ShapeDtypeStruct((t, d), F32),
                   jax.ShapeDtypeStruct((t, d), F32), jax.ShapeDtypeStruct((t, d), BF16),
                   jax.ShapeDtypeStruct((t, POOL_WIDTH), BF16), jax.ShapeDtypeStruct((t, POOL_WIDTH), BF16)],
        compiler_params=_params("parallel"),
    )(attn, z, z, gate, h, wattn, wmix, scale, wpool_t, wout)


def _loss_head(h, g, target, tm):
    t, d = h.shape

    def body(h_ref, g_ref, tgt_ref, loss_ref, dh_ref, dhb_ref, dg_ref):
        xh, r = _rms_fwd(h_ref[...], g_ref[...])
        err = xh * g_ref[...] - tgt_ref[...]
        part = 0.5 * jnp.sum(jnp.mean(err * err, axis=-1, keepdims=True), axis=0, keepdims=True)
        dx, dg = _rms_bwd(err * (1.0 / d), xh, r, g_ref[...])
        dh_ref[...] = dx
        dhb_ref[...] = dx.astype(BF16)

        @pl.when(pl.program_id(0) == 0)
        def _():
            dg_ref[...] = jnp.zeros_like(dg_ref)
            loss_ref[...] = jnp.zeros_like(loss_ref)

        dg_ref[...] += dg
        loss_ref[...] += jnp.broadcast_to(part, loss_ref.shape)

    return pl.pallas_call(
        body, name="loss_head", grid=(t // tm,),
        in_specs=[_rows(tm, d), _resident((1, d)), _rows(tm, d)],
        out_specs=[pl.BlockSpec((1, LANES), lambda i: (0, 0)), _rows(tm, d), _rows(tm, d),
                   pl.BlockSpec((1, d), lambda i: (0, 0))],
        out_shape=[jax.ShapeDtypeStruct((1, LANES), F32), jax.ShapeDtypeStruct((t, d), F32),
                   jax.ShapeDtypeStruct((t, d), BF16), jax.ShapeDtypeStruct((1, d), F32)],
        compiler_params=_params("arbitrary"),
    )(h, g, target)


def _mix_out_bwd(dh, gate, a, p, pooled, wattn, wmix, scale, wpool_t, wout, tm):
    t, d = dh.shape

    def body(dh_ref, gate_ref, a_ref, p_ref, pooled_ref, wattn_ref, wmix_ref, scale_ref, wpool_ref, wout_ref,
             dhb_ref, dab_ref, dpb_ref, dattn_ref, dgate_ref, dpooled_ref, dwmix_ref, dscale_ref):
        @pl.when(pl.program_id(0) == 0)
        def _():
            dwmix_ref[...] = jnp.zeros_like(dwmix_ref)
            dscale_ref[...] = jnp.zeros_like(dscale_ref)

        dhb = dh_ref[...].astype(BF16)
        dhb_ref[...] = dhb
        dm = _nt(dhb, wout_ref[...])
        sa = jax.nn.sigmoid(gate_ref[:, 0:d])
        sp = jax.nn.sigmoid(gate_ref[:, d:2 * d])
        da = (dm * sa).astype(BF16)
        dp = (dm * sp).astype(BF16)
        dab_ref[...] = da
        dpb_ref[...] = dp
        dgate_ref[:, 0:d] = (dm * a_ref[...] * (sa * (1.0 - sa))).astype(BF16)
        dgate_ref[:, d:2 * d] = (dm * p_ref[...] * (sp * (1.0 - sp))).astype(BF16)
        dattn_ref[...] = _nt(da, wattn_ref[...]).astype(BF16)
        dms = _nn(dp, wpool_ref[...])
        for gi in range(len(POOL_WINDOWS)):
            lo, hi = gi * POOL_GROUP, (gi + 1) * POOL_GROUP
            pooled_g = pooled_ref[:, lo:hi]
            mixed = _nn(pooled_g, wmix_ref[gi])
            dscale_ref[:, lo:hi] += jnp.sum(dms[:, lo:hi] * mixed, axis=0, keepdims=True)
            dmixed = (dms[:, lo:hi] * scale_ref[:, lo:hi]).astype(BF16)
            dwmix_ref[gi] += _tn(pooled_g, dmixed)
            dpooled_ref[:, lo:hi] = _nt(dmixed, wmix_ref[gi])

    acc = lambda shape: pl.BlockSpec(shape, lambda i: (0,) * len(shape))
    return pl.pallas_call(
        body, name="mix_out_bwd", grid=(t // tm,),
        in_specs=[_rows(tm, d), _rows(tm, 2 * d), _rows(tm, d), _rows(tm, d), _rows(tm, POOL_WIDTH),
                  _resident(wattn.shape), _resident(wmix.shape), _resident(scale.shape), _resident(wpool_t.shape),
                  _resident(wout.shape)],
        out_specs=[_rows(tm, d), _rows(tm, d), _rows(tm, d), _rows(tm, ATTN_WIDTH), _rows(tm, 2 * d),
                   _rows(tm, POOL_WIDTH), acc(wmix.shape), acc((1, POOL_WIDTH))],
        out_shape=[jax.ShapeDtypeStruct((t, d), BF16), jax.ShapeDtypeStruct((t, d), BF16),
                   jax.ShapeDtypeStruct((t, d), BF16), jax.ShapeDtypeStruct((t, ATTN_WIDTH), BF16),
                   jax.ShapeDtypeStruct((t, 2 * d), BF16), jax.ShapeDtypeStruct((t, POOL_WIDTH), F32),
                   jax.ShapeDtypeStruct(wmix.shape, F32), jax.ShapeDtypeStruct((1, POOL_WIDTH), F32)],
        compiler_params=_params("arbitrary"),
    )(dh, gate, a, p, pooled, wattn, wmix, scale, wpool_t, wout)


def _fold_halves(x):
    return x + pltpu.roll(x, HEAD_DIM, 1)


def _attn_bwd(q, kv, dattn, dist, sinks, carry=None):
    t = q.shape[0]

    def body(q_ref, kvc_ref, kvp_ref, do_ref, dist_ref, sink_ref, dq_ref, dkv_own_ref, dkv_prev_ref, dsink_ref,
             s_scr, dp_scr, p_scr, ds_scr):
        first = pl.program_id(0) == 0

        @pl.when(first)
        def _():
            dsink_ref[...] = jnp.zeros_like(dsink_ref)

        own = _own_block()
        dist_v = dist_ref[...]
        kk, vv = _bands(kvc_ref, kvp_ref)
        lane = lax.broadcasted_iota(jnp.int32, (1, LANES), 1)
        for head in HEADS:
            hk = head // Q_PER_KV
            s_scr[head] = _head_scores(q_ref, kk[hk], dist_v, head, first, own)
            dp_scr[head] = _fold(_nt(_own_half(do_ref, head), vv[hk]), own)
        dsink = jnp.zeros((1, LANES), F32)
        for head in HEADS:
            probs, psink = _softmax_sink(s_scr[head], sink_ref[head])
            dprobs = dp_scr[head]
            rowdot = jnp.sum(probs * dprobs, axis=-1, keepdims=True)
            p_scr[head] = _unfold(probs.astype(BF16), own)
            ds_scr[head] = _unfold((probs * (dprobs - rowdot)).astype(BF16), own)
            dsink = dsink + jnp.where(lane == head, jnp.sum(-psink * rowdot, axis=0, keepdims=True), 0.0)
        dk_heads, dv_heads = [], []
        for hk in range(N_KV_HEADS):
            dk_t = jnp.zeros((LANES, 2 * BLOCK), F32)
            dv_t = jnp.zeros((LANES, 2 * BLOCK), F32)
            for pair in range(Q_PER_KV // 2):
                col = LANES * (hk * (Q_PER_KV // 2) + pair)
                q_t = (q_ref[:, col:col + LANES] * HEAD_DIM ** -0.5).T
                do_t = do_ref[:, col:col + LANES].T
                dqs = []
                for head in (hk * Q_PER_KV + 2 * pair, hk * Q_PER_KV + 2 * pair + 1):
                    mine = (lax.broadcasted_iota(jnp.int32, q_t.shape, 0) < HEAD_DIM) == (head % 2 == 0)
                    dv_t = dv_t + _nn(jnp.where(mine, do_t, jnp.zeros_like(do_t)), p_scr[head])
                    dk_t = dk_t + _nn(jnp.where(mine, q_t, jnp.zeros_like(q_t)), ds_scr[head])
                    dqs.append(_nn(ds_scr[head], kk[hk]))
                dq_pair = jnp.where(_low_half(dqs[0].shape), dqs[0], dqs[1])
                dq_ref[:, col:col + LANES] = (dq_pair * HEAD_DIM ** -0.5).astype(BF16)
            dk_heads.append(_fold_halves(dk_t.T))
            dv_heads.append(_fold_halves(dv_t.T))
        low = _low_half(dk_heads[0].shape)
        dkv = jnp.concatenate([jnp.where(low, dk_heads[0], dk_heads[1]), jnp.where(low, dv_heads[0], dv_heads[1])],
                              axis=1)
        dkv_prev_ref[...] = dkv[0:BLOCK, :]
        dkv_own_ref[...] = dkv[BLOCK:2 * BLOCK, :]
        dsink_ref[...] += dsink

    return _launch(
        body, (q, kv, kv, dattn, dist, sinks), carry, name="attn_bwd", grid=(t // BLOCK,),
        in_specs=[_rows(BLOCK, ATTN_WIDTH), _rows(BLOCK, 2 * KV_WIDTH),
                  pl.BlockSpec((BLOCK, 2 * KV_WIDTH), lambda i: (jnp.maximum(i - 1, 0), 0)),
                  _rows(BLOCK, ATTN_WIDTH), _resident(dist.shape), SMEM],
        out_specs=[_rows(BLOCK, ATTN_WIDTH), _rows(BLOCK, 2 * KV_WIDTH), _rows(BLOCK, 2 * KV_WIDTH),
                   pl.BlockSpec((1, LANES), lambda i: (0, 0))],
        out_shape=[jax.ShapeDtypeStruct((t, ATTN_WIDTH), BF16), jax.ShapeDtypeStruct((t, 2 * KV_WIDTH), F32),
                   jax.ShapeDtypeStruct((t, 2 * KV_WIDTH), F32), jax.ShapeDtypeStruct((1, LANES), F32)],
        scratch_shapes=[pltpu.VMEM((N_Q_HEADS, BLOCK, BLOCK), F32), pltpu.VMEM((N_Q_HEADS, BLOCK, BLOCK), F32),
                        pltpu.VMEM((N_Q_HEADS, BLOCK, 2 * BLOCK), BF16),
                        pltpu.VMEM((N_Q_HEADS, BLOCK, 2 * BLOCK), BF16)],
        semantics=("arbitrary",))


def _mix_in_bwd(dq, dkv_own, dkv_prev, dpooled, dgate, h, g, win_t, dh_res, tm, carry=None):
    t, d = h.shape
    nt = t // tm

    def body(dq_ref, own_ref, prev_ref, prev_next_ref, dpool_ref, halo_ref, dgate_ref, h_ref, g_ref, w_ref, res_ref,
             dproj_ref, dh_ref, dhb_ref, dg_ref):
        i = pl.program_id(0)
        last = i == nt - 1
        dproj_ref[:, 0:OFF_KV] = dq_ref[...]
        from_next = jnp.where(last, 0.0, prev_next_ref[...])
        if tm > BLOCK:
            from_next = jnp.concatenate([prev_ref[BLOCK:tm, :], from_next], axis=0)
        dproj_ref[:, OFF_KV:OFF_Z] = (own_ref[...] + from_next).astype(BF16)
        halo = jnp.where(last, 0.0, halo_ref[...])
        for gi, width in enumerate(POOL_WINDOWS):
            lo, hi = gi * POOL_GROUP, (gi + 1) * POOL_GROUP
            dpg = dpool_ref[:, lo:hi]
            scaled = jnp.concatenate([dpg / _pool_counts(tm, width), halo[:, lo:hi] / float(width)], axis=0)
            dz = _leading_sums(scaled, gi)[0:tm, :] - dpg
            dproj_ref[:, OFF_Z + lo:OFF_Z + hi] = dz.astype(BF16)
        dproj_ref[:, OFF_GATE:IN_WIDTH] = dgate_ref[...]
        du = _nn(dproj_ref[...], w_ref[...])
        xh, r = _rms_fwd(h_ref[...], g_ref[...])
        dx, dg = _rms_bwd(du, xh, r, g_ref[...])
        dh = res_ref[...] + dx
        dh_ref[...] = dh
        dhb_ref[...] = dh.astype(BF16)

        @pl.when(i == 0)
        def _():
            dg_ref[...] = jnp.zeros_like(dg_ref)

        dg_ref[...] += dg

    per = tm // BLOCK
    next_block = pl.BlockSpec((BLOCK, 2 * KV_WIDTH), lambda i: (jnp.minimum((i + 1) * per, t // BLOCK - 1), 0))
    next_halo = pl.BlockSpec((HALO, POOL_WIDTH), lambda i: (jnp.minimum((i + 1) * (tm // HALO), t // HALO - 1), 0))
    return _launch(
        body, (dq, dkv_own, dkv_prev, dkv_prev, dpooled, dpooled, dgate, h, g, win_t, dh_res), carry,
        name="mix_in_bwd", grid=(nt,),
        in_specs=[_rows(tm, ATTN_WIDTH), _rows(tm, 2 * KV_WIDTH), _rows(tm, 2 * KV_WIDTH), next_block,
                  _rows(tm, POOL_WIDTH), next_halo, _rows(tm, 2 * d), _rows(tm, d), _resident((1, d)),
                  _resident((IN_WIDTH, d)), _rows(tm, d)],
        out_specs=[_rows(tm, IN_WIDTH), _rows(tm, d), _rows(tm, d), pl.BlockSpec((1, d), lambda i: (0, 0))],
        out_shape=[jax.ShapeDtypeStruct((t, IN_WIDTH), BF16), jax.ShapeDtypeStruct((t, d), F32),
                   jax.ShapeDtypeStruct((t, d), BF16), jax.ShapeDtypeStruct((1, d), F32)],
        semantics=("arbitrary",))


BIG = (("wup1_t", "ffn1_w_up", True), ("wdown1", "ffn1_w_down", False), ("win_t", "w_in", True),
       ("wattn", "w_attn_up", False), ("wpool_t", "w_pool_up", True), ("wout", "w_out", False),
       ("wup2_t", "ffn2_w_up", True), ("wdown2", "ffn2_w_down", False))
ANY = pl.BlockSpec(memory_space=pl.ANY)
WIRE = BF16


def _place():
    return lax.axis_index("x"), lax.axis_index("y"), lax.axis_index("c")


def _peer(k):
    x, y, c = _place()
    return x ^ (k >> 2), y ^ ((k >> 1) & 1), c ^ (k & 1)


def _index(px, py, pc):
    return 4 * px + 2 * py + pc


def _gather_carry(shards):
    n = len(shards)

    def tools(ins, outs, sems):
        send_sems, recv_sems, local_sems = sems
        x, y, c = _place()
        chips = [(1 - x, y), (x, 1 - y), (1 - x, 1 - y)]

        def rows(w, px, py, pc):
            r = ins[w].shape[0]
            return outs[w].at[pl.ds(_index(px, py, pc) * r, r), :]

        def copy(w, k, block, to, src=None):
            return pltpu.make_async_remote_copy(
                src_ref=rows(w, *block) if src is None else src, dst_ref=rows(w, *block),
                send_sem=send_sems.at[w, k], recv_sem=recv_sems.at[w, k], device_id=to, device_id_type=MESH)

        def own(w):
            return ([pltpu.make_async_copy(ins[w], rows(w, x, y, c), local_sems.at[w]),
                     copy(w, 0, (x, y, c), (x, y, 1 - c), src=ins[w])]
                    + [copy(w, 1 + j, (x, y, c), (*chip, c), src=ins[w]) for j, chip in enumerate(chips)])

        def passed(w, j):
            return copy(w, 4 + j, (*chips[j], c), (x, y, 1 - c))

        return (x, y, c), chips, copy, own, passed

    def start(ins, outs, sems):
        _, _, _, own, _ = tools(ins, outs, sems)
        for w in range(n):
            for cp in own(w):
                cp.start()

    def forward(w):
        def run(ins, outs, sems):
            (x, y, c), chips, copy, _, passed = tools(ins, outs, sems)
            for j, chip in enumerate(chips):
                copy(w, 1 + j, (*chip, c), (x, y, c)).wait_recv()
                passed(w, j).start()
        return run

    sizes = np.cumsum([s.size for s in shards]) / sum(s.size for s in shards)
    middles = [(float(sizes[w]), forward(w)) for w in range(n)]

    def finish(ins, outs, sems):
        (x, y, c), chips, copy, own, passed = tools(ins, outs, sems)
        for w in range(n):
            copy(w, 0, (x, y, 1 - c), (x, y, c)).wait_recv()
            for j, chip in enumerate(chips):
                copy(w, 4 + j, (*chip, 1 - c), (x, y, c)).wait_recv()
        for w in range(n):
            mine, *sent = own(w)
            for cp in sent + [passed(w, j) for j in range(len(chips))]:
                cp.wait_send()
            mine.wait()

    return _Carry(
        shards, [jax.ShapeDtypeStruct((N_DEV * s.shape[0], s.shape[1]), s.dtype) for s in shards],
        [pltpu.SemaphoreType.DMA((n, N_DEV - 1)), pltpu.SemaphoreType.DMA((n, N_DEV - 1)),
         pltpu.SemaphoreType.DMA((n,))], start, finish, middles)


def _scatter_carry(grads):
    n = len(grads)

    def tools(ins, outs, sems):
        send_sems, recv_sems, local_sems = sems
        me = _index(*_place())

        def block(ref, dev):
            r = ref.shape[0] // N_DEV
            return ref.at[pl.ds(dev * r, r), :]

        def copy(w, k, landing):
            to = _peer(k)
            return pltpu.make_async_remote_copy(
                src_ref=block(ins[w], _index(*to)), dst_ref=block(outs[w], landing), send_sem=send_sems.at[w, k - 1],
                recv_sem=recv_sems.at[w, k - 1], device_id=to, device_id_type=MESH)

        def mine(w):
            return pltpu.make_async_copy(block(ins[w], me), block(outs[w], me), local_sems.at[w])

        return me, copy, mine

    def start(ins, outs, sems):
        me, copy, mine = tools(ins, outs, sems)
        for w in range(n):
            mine(w).start()
            for k in range(1, N_DEV):
                copy(w, k, me).start()

    def finish(ins, outs, sems):
        _, copy, mine = tools(ins, outs, sems)
        for w in range(n):
            for k in range(1, N_DEV):
                copy(w, k, _index(*_peer(k))).wait()
            mine(w).wait()

    return _Carry(
        grads, [jax.ShapeDtypeStruct(g.shape, g.dtype) for g in grads],
        [pltpu.SemaphoreType.DMA((n, N_DEV - 1)), pltpu.SemaphoreType.DMA((n, N_DEV - 1)),
         pltpu.SemaphoreType.DMA((n,))], start, finish)


def _small_carry(small):
    srows = small.shape[0]

    def tools(ins, outs, sems):
        send_sems, recv_sems, local_sem = sems
        me = _index(*_place())

        def slot(dev):
            return outs[0].at[pl.ds(dev * srows, srows), :]

        def copy(k, landing):
            return pltpu.make_async_remote_copy(
                src_ref=ins[0], dst_ref=slot(landing), send_sem=send_sems.at[k - 1], recv_sem=recv_sems.at[k - 1],
                device_id=_peer(k), device_id_type=MESH)

        return me, copy, pltpu.make_async_copy(ins[0], slot(me), local_sem)

    def start(ins, outs, sems):
        me, copy, mine = tools(ins, outs, sems)
        mine.start()
        for k in range(1, N_DEV):
            copy(k, me).start()

    def finish(ins, outs, sems):
        _, copy, mine = tools(ins, outs, sems)
        for k in range(1, N_DEV):
            copy(k, _index(*_peer(k))).wait()
        mine.wait()

    return _Carry([small], [jax.ShapeDtypeStruct((N_DEV * srows, LANES), small.dtype)],
                  [pltpu.SemaphoreType.DMA((N_DEV - 1,)), pltpu.SemaphoreType.DMA((N_DEV - 1,)),
                   pltpu.SemaphoreType.DMA], start, finish)


def _exchange(carry, name):
    ci = len(carry.inputs)
    co = len(carry.out_shape)

    def body(*refs):
        parts = refs[:ci], refs[ci:ci + co], refs[ci + co:]
        carry.start(*parts)
        for _, fn in carry.middles:
            fn(*parts)
        carry.finish(*parts)

    return list(pl.pallas_call(body, name=name, in_specs=[ANY] * ci, out_specs=[ANY] * co, out_shape=carry.out_shape,
                               scratch_shapes=carry.scratch)(*carry.inputs))


def _device_sum(got, name):
    r, cols = got.shape[0] // N_DEV, got.shape[1]
    tr = r if r <= 256 else r // 2

    def body(g_ref, o_ref):
        acc = g_ref[0].astype(F32)
        for dev in range(1, N_DEV):
            acc = acc + g_ref[dev].astype(F32)
        o_ref[...] = acc

    return pl.pallas_call(
        body, name=name, grid=(r // tr,),
        in_specs=[pl.BlockSpec((N_DEV, tr, cols), lambda i: (0, i, 0))],
        out_specs=pl.BlockSpec((tr, cols), lambda i: (i, 0)),
        out_shape=jax.ShapeDtypeStruct((r, cols), F32),
        compiler_params=_params("parallel"),
    )(got.reshape(N_DEV, r, cols))


def _adamw_math(w, g, m, v):
    m = ADAM_B1 * m + (1.0 - ADAM_B1) * g
    v = ADAM_B2 * v + (1.0 - ADAM_B2) * (g * g)
    m_hat = m / (1.0 - ADAM_B1 ** ADAM_STEP)
    v_hat = v / (1.0 - ADAM_B2 ** ADAM_STEP)
    return -ADAM_LR * (m_hat / (jnp.sqrt(v_hat) + ADAM_EPS) + ADAM_WD * w), m, v


def _adamw(w, g, m, v, name):
    r, cols = w.shape
    tr = r if r <= 512 else 256

    def body(w_ref, g_ref, m_ref, v_ref, d_ref, m2_ref, v2_ref):
        d_ref[...], m2_ref[...], v2_ref[...] = _adamw_math(w_ref[...], g_ref[...], m_ref[...], v_ref[...])

    spec = pl.BlockSpec((tr, cols), lambda i: (i, 0))
    return pl.pallas_call(
        body, name=name, grid=(r // tr,), in_specs=[spec] * 4, out_specs=[spec] * 3,
        out_shape=[jax.ShapeDtypeStruct((r, cols), F32)] * 3, compiler_params=_params("parallel"),
    )(w, g, m, v)


def _small_update(gathered, w, m, v):
    rows = w.shape[0]

    def body(all_ref, w_ref, m_ref, v_ref, g_ref, d_ref, m2_ref, v2_ref):
        g = all_ref[0]
        for dev in range(1, N_DEV):
            g = g + all_ref[dev]
        g_ref[...] = g
        d_ref[...], m2_ref[...], v2_ref[...] = _adamw_math(w_ref[...], g, m_ref[...], v_ref[...])

    return pl.pallas_call(
        body, name="small_update", out_shape=[jax.ShapeDtypeStruct((rows, LANES), F32)] * 4,
        compiler_params=pltpu.CompilerParams(vmem_limit_bytes=VMEM_LIMIT),
    )(gathered.reshape(N_DEV, rows, LANES), w, m, v)


SMALL = (("pool_w_mix", 512), ("ffn1_norm", 8), ("mix_norm", 8), ("ffn2_norm", 8), ("final_norm", 8),
         ("pool_scale", 8), ("sinks", 8), ("loss", 8))
SMALL_ROWS = sum(rows for _, rows in SMALL)


def _pack_small(parts):
    out = []
    for name, rows in SMALL:
        flat = parts[name].astype(F32).reshape(-1)
        out.append(jnp.pad(flat, (0, rows * LANES - flat.shape[0])).reshape(rows, LANES))
    return jnp.concatenate(out, axis=0)


def _unpack_small(packed, shapes):
    out, row = {}, 0
    for name, rows in SMALL:
        shape = shapes[name]
        size = int(np.prod(shape)) if shape else 1
        out[name] = packed[row:row + rows].reshape(-1)[:size].reshape(shape)
        row += rows
    return out


def kernel(x, ffn1_norm, ffn1_w_up, ffn1_w_down, mix_norm, w_in, sinks, w_attn_up, pool_w_mix, pool_scale, w_pool_up, w_out, ffn2_norm, ffn2_w_up, ffn2_w_down, final_norm, loss_target, m_ffn1_norm, m_ffn1_w_up, m_ffn1_w_down, m_mix_norm, m_w_in, m_sinks, m_w_attn_up, m_pool_w_mix, m_pool_scale, m_w_pool_up, m_w_out, m_ffn2_norm, m_ffn2_w_up, m_ffn2_w_down, m_final_norm, v_ffn1_norm, v_ffn1_w_up, v_ffn1_w_down, v_mix_norm, v_w_in, v_sinks, v_w_attn_up, v_pool_w_mix, v_pool_scale, v_w_pool_up, v_w_out, v_ffn2_norm, v_ffn2_w_up, v_ffn2_w_down, v_final_norm):
    args = dict(locals())
    weight_names = ("ffn1_norm", "ffn1_w_up", "ffn1_w_down", "mix_norm", "w_in", "sinks", "w_attn_up", "pool_w_mix",
                    "pool_scale", "w_pool_up", "w_out", "ffn2_norm", "ffn2_w_up", "ffn2_w_down", "final_norm")

    shard = {k: (args[p][0].T if tr else args[p][0]).astype(BF16) for k, p, tr in BIG}
    later = [k for k, _, _ in BIG[2:]]
    big = dict(zip(("wup1_t", "wdown1"), _exchange(_gather_carry([shard["wup1_t"], shard["wdown1"]]), "gather_ffn1")))

    xs, target = x[0], loss_target[0]
    t = xs.shape[0]
    tm_f, tm_b, tk = min(512, t), min(256, t), min(1024, t)
    g1, gm, g2, gf = ffn1_norm, mix_norm, ffn2_norm, final_norm.reshape(1, D_MODEL)
    dist = _attn_dist()
    sink_v = sinks.reshape(N_Q_HEADS)
    wmix_b = pool_w_mix[0].astype(BF16)

    (h1, ab1, n1, act1), rest = _ffn_fwd(xs, g1, big["wup1_t"], big["wdown1"], tm_f,
                                         _gather_carry([shard[k] for k in later]))
    big.update(zip(later, rest))
    u, q, kv, z, gate = _mix_in_fwd(h1, gm, big["win_t"], tm_f)
    attn = _attn_fwd(q, kv, dist, sink_v)
    h2, a, p, merged, ms, pooled = _mix_out_fwd(attn, z, gate, h1, big["wattn"], wmix_b, pool_scale, big["wpool_t"],
                                                big["wout"], tm_b)
    (h3, ab2, n2, act2), _ = _ffn_fwd(h2, g2, big["wup2_t"], big["wdown2"], tm_f)
    loss_lanes, dh3, dhb3, dgf = _loss_head(h3, gf, target, tm_f)

    got = {}
    (gw_down2,), _ = _wgrad(act2, dhb3, 0.5, D_FF, tk, "wgrad_down2")
    (dab2,), (got["wdown2"],) = _ffn_bwd_hidden(dhb3, ab2, big["wdown2"], tm_f, _scatter_carry([gw_down2]))
    dh2, dg2 = _ffn_bwd_input(dab2, dh3, h2, g2, big["wup2_t"], tm_f)
    (gw_up2,), _ = _wgrad(dab2, n2, 1.0, D_FF, tk, "wgrad_up2")
    dhb2, da_b, dp_b, dattn, dgate, dpooled, dwmix, dscale = _mix_out_bwd(
        dh2, gate, a, p, pooled, big["wattn"], wmix_b, pool_scale, big["wpool_t"], big["wout"], tm_b)
    (gw_out,), _ = _wgrad(merged, dhb2, 1.0, D_MODEL, tk, "wgrad_out")
    (gw_attn,), _ = _wgrad(attn, da_b, 1.0, D_MODEL, tk, "wgrad_attn")
    (gw_pool,), _ = _wgrad(dp_b, ms, 1.0, D_MODEL, tk, "wgrad_pool")
    (dq, dkv_own, dkv_prev, dsinks), (got["wup2_t"],) = _attn_bwd(q, kv, dattn, dist, sink_v, _scatter_carry([gw_up2]))
    (dproj, dh1, dhb1, dgm), (got["wout"], got["wattn"], got["wpool_t"]) = _mix_in_bwd(
        dq, dkv_own, dkv_prev, dpooled, dgate, h1, gm, big["win_t"], dh2, tm_b,
        _scatter_carry([gw_out, gw_attn, gw_pool]))
    (gw_in,), _ = _wgrad(dproj, u, 1.0, IN_WIDTH // 2, tk, "wgrad_in")
    (gw_down1,), _ = _wgrad(act1, dhb1, 0.5, D_FF, tk, "wgrad_down1")
    (dab1,), (got["win_t"],) = _ffn_bwd_hidden(dhb1, ab1, big["wdown1"], tm_f, _scatter_carry([gw_in]))
    dx, dg1 = _ffn_bwd_input(dab1, dh1, xs, g1, big["wup1_t"], tm_f)
    small_parts = {"pool_w_mix": dwmix, "ffn1_norm": dg1, "mix_norm": dgm, "ffn2_norm": dg2, "final_norm": dgf,
                   "pool_scale": dscale, "sinks": dsinks[:, :N_Q_HEADS], "loss": loss_lanes[:, :1]}
    (gw_up1,), (got["wdown1"], small_all) = _wgrad(
        dab1, n1, 1.0, D_FF, tk, "wgrad_up1", _merge(_scatter_carry([gw_down1]), _small_carry(_pack_small(small_parts))))
    (got["wup1_t"],) = _exchange(_scatter_carry([gw_up1]), "scatter_up1")

    grad, delta, new_m, new_v = {}, {}, {}, {}
    for k, p, tr in BIG:
        g = _device_sum(got[k], "device_sum_" + k)
        g = g.T if tr else g
        d, m2, v2 = _adamw(args[p][0], g, args["m_" + p][0], args["v_" + p][0], "adamw_" + k)
        grad[p], delta[p], new_m[p], new_v[p] = g[None], d[None], m2[None], v2[None]

    shapes = {name: args[name].shape for name, _ in SMALL if name != "loss"}
    shapes["loss"] = ()
    packed = {pre: _pack_small({**{name: args[pre + name] for name, _ in SMALL if name != "loss"},
                                "loss": jnp.zeros((), F32)}) for pre in ("", "m_", "v_")}
    g_s, d_s, m_s, v_s = _small_update(small_all, packed[""], packed["m_"], packed["v_"])
    g_small, d_small, m_small, v_small = (_unpack_small(a, shapes) for a in (g_s, d_s, m_s, v_s))
    for name, _ in SMALL:
        if name != "loss":
            grad[name], delta[name], new_m[name], new_v[name] = (
                g_small[name], d_small[name], m_small[name], v_small[name])

    return (g_small["loss"], dx[None], *[grad[n] for n in weight_names], *[delta[n] for n in weight_names],
            *[new_m[n] for n in weight_names], *[new_v[n] for n in weight_names])
```

```python
import functools

import jax
import jax.numpy as jnp
import numpy as np
from jax import lax
from jax.experimental import pallas as pl
from jax.experimental.pallas import tpu as pltpu

F32 = jnp.float32
BF16 = jnp.bfloat16

D_MODEL = 1024
D_FF = 2816
N_Q_HEADS = 16
N_KV_HEADS = 2
Q_PER_KV = N_Q_HEADS // N_KV_HEADS
HEAD_DIM = 64
BLOCK = 128
ATTN_WIDTH = N_Q_HEADS * HEAD_DIM
KV_WIDTH = N_KV_HEADS * HEAD_DIM
POOL_WINDOWS = (2, 4, 8, 16)
POOL_GROUP = 128
POOL_WIDTH = 512
HALO = 16
IN_WIDTH = ATTN_WIDTH + 2 * KV_WIDTH + POOL_WIDTH + 2 * D_MODEL
OFF_KV = ATTN_WIDTH
OFF_Z = ATTN_WIDTH + 2 * KV_WIDTH
OFF_GATE = OFF_Z + POOL_WIDTH
NORM_EPS = 1e-6
ADAM_LR = 0.001
ADAM_B1 = 0.9
ADAM_B2 = 0.999
ADAM_EPS = 1e-08
ADAM_WD = 0.01
ADAM_STEP = 10

N_DEV = 8
N_CHIP = 4
LANES = 128
FF_CHUNK = 256
SLAB = 32
VMEM_LIMIT = 56 * 1024 * 1024
MESH = pl.DeviceIdType.MESH


def _nn(a, b):
    return jnp.dot(a, b, preferred_element_type=F32)


def _nt(a, b):
    return lax.dot_general(a, b, (((1,), (1,)), ((), ())), preferred_element_type=F32)


def _tn(a, b):
    return lax.dot_general(a, b, (((0,), (0,)), ((), ())), preferred_element_type=F32)


def _params(*sem):
    return pltpu.CompilerParams(dimension_semantics=sem, vmem_limit_bytes=VMEM_LIMIT)


def _resident(shape):
    return pl.BlockSpec(shape, lambda *_: (0,) * len(shape), pipeline_mode=pl.Buffered(1))


def _rows(tm, cols):
    return pl.BlockSpec((tm, cols), lambda i: (i, 0))


class _Carry:
    def __init__(self, inputs, out_shape, scratch, start, finish, middles=()):
        self.inputs, self.out_shape, self.scratch = list(inputs), list(out_shape), list(scratch)
        self.start, self.finish, self.middles = start, finish, list(middles)


def _merge(a, b):
    ia, oa, sa = len(a.inputs), len(a.out_shape), len(a.scratch)

    def first(fn):
        return lambda ins, outs, sems: fn(ins[:ia], outs[:oa], sems[:sa])

    def second(fn):
        return lambda ins, outs, sems: fn(ins[ia:], outs[oa:], sems[sa:])

    def both(fa, fb):
        def run(ins, outs, sems):
            first(fa)(ins, outs, sems)
            second(fb)(ins, outs, sems)
        return run

    middles = [(f, first(fn)) for f, fn in a.middles] + [(f, second(fn)) for f, fn in b.middles]
    return _Carry(a.inputs + b.inputs, a.out_shape + b.out_shape, a.scratch + b.scratch, both(a.start, b.start),
                  both(a.finish, b.finish), middles)


def _launch(body, args, carry=None, *, name, grid, in_specs, out_specs, out_shape, scratch_shapes=(), semantics):
    in_specs, out_specs, out_shape, scratch_shapes = list(in_specs), list(out_specs), list(out_shape), list(scratch_shapes)
    if carry is None:
        res = pl.pallas_call(body, name=name, grid=grid, in_specs=in_specs, out_specs=out_specs, out_shape=out_shape,
                             scratch_shapes=scratch_shapes, compiler_params=_params(*semantics))(*args)
        return list(res), []
    ni, no, ns = len(in_specs), len(out_specs), len(scratch_shapes)
    ci, co = len(carry.inputs), len(carry.out_shape)
    total = int(np.prod(grid))

    def full(*refs):
        own_in, c_in = refs[:ni], refs[ni:ni + ci]
        own_out, c_out = refs[ni + ci:ni + ci + no], refs[ni + ci + no:ni + ci + no + co]
        own_scr, c_sem = refs[ni + ci + no + co:ni + ci + no + co + ns], refs[ni + ci + no + co + ns:]
        step = 0
        for axis, size in enumerate(grid):
            step = step * size + pl.program_id(axis)
        pl.when(step == 0)(lambda: carry.start(c_in, c_out, c_sem))
        for fraction, fn in carry.middles:
            at = min(total - 1, int(fraction * total) + 1)
            pl.when(step == at)(lambda fn=fn: fn(c_in, c_out, c_sem))
        body(*own_in, *own_out, *own_scr)
        pl.when(step == total - 1)(lambda: carry.finish(c_in, c_out, c_sem))

    res = pl.pallas_call(
        full, name=name, grid=grid, in_specs=in_specs + [ANY] * ci, out_specs=out_specs + [ANY] * co,
        out_shape=out_shape + carry.out_shape, scratch_shapes=scratch_shapes + carry.scratch,
        compiler_params=_params(*(["arbitrary"] * len(grid))),
    )(*args, *carry.inputs)
    return list(res[:no]), list(res[no:])


def _rms_fwd(xv, g):
    r = lax.rsqrt(jnp.mean(xv * xv, axis=-1, keepdims=True) + NORM_EPS)
    return xv * r, r


def _rms_bwd(dn, xh, r, g):
    dxh = dn * g
    dx = r * (dxh - xh * jnp.mean(dxh * xh, axis=-1, keepdims=True))
    return dx, jnp.sum(dn * xh, axis=0, keepdims=True)


def _ffn_fwd(x, g, wup_t, wdown, tm, carry=None):
    t, d = x.shape
    f = wdown.shape[0]

    def body(x_ref, g_ref, wup_ref, wdn_ref, h_ref, ab_ref, n_ref, act_ref):
        xv = x_ref[...]
        xh, _ = _rms_fwd(xv, g_ref[...])
        n = (xh * g_ref[...]).astype(BF16)
        n_ref[...] = n
        for c in range(f // FF_CHUNK):
            lo, hi = c * FF_CHUNK, (c + 1) * FF_CHUNK
            a = _nt(n, wup_ref[lo:hi, :])
            b = _nt(n, wup_ref[f + lo:f + hi, :])
            ab_ref[:, lo:hi] = a.astype(BF16)
            ab_ref[:, f + lo:f + hi] = b.astype(BF16)
            act_ref[:, lo:hi] = (a * jax.nn.sigmoid(a) * b).astype(BF16)
        h_ref[...] = xv + 0.5 * _nn(act_ref[...], wdn_ref[...])

    return _launch(
        body, (x, g, wup_t, wdown), carry, name="ffn_fwd", grid=(t // tm,),
        in_specs=[_rows(tm, d), _resident((1, d)), _resident((2 * f, d)), _resident((f, d))],
        out_specs=[_rows(tm, d), _rows(tm, 2 * f), _rows(tm, d), _rows(tm, f)],
        out_shape=[jax.ShapeDtypeStruct((t, d), F32), jax.ShapeDtypeStruct((t, 2 * f), BF16),
                   jax.ShapeDtypeStruct((t, d), BF16), jax.ShapeDtypeStruct((t, f), BF16)],
        semantics=("parallel",))


def _ffn_bwd_hidden(dhb, ab, wdown, tm, carry=None):
    t, d = dhb.shape
    f = wdown.shape[0]

    def body(dh_ref, ab_ref, wdn_ref, dab_ref, dact_ref):
        half = dh_ref[...] * 0.5
        for c in range(f // FF_CHUNK):
            lo, hi = c * FF_CHUNK, (c + 1) * FF_CHUNK
            dact_ref[...] = _nt(half, wdn_ref[lo:hi, :])

            def slab(i, carry_):
                rows = pl.ds(pl.multiple_of(i * SLAB, SLAB), SLAB)
                a = ab_ref[rows, lo:hi].astype(F32)
                b = ab_ref[rows, f + lo:f + hi].astype(F32)
                s = jax.nn.sigmoid(a)
                ds_ = dact_ref[rows, :] * s
                dab_ref[rows, lo:hi] = (ds_ * b * (1.0 + a * (1.0 - s))).astype(BF16)
                dab_ref[rows, f + lo:f + hi] = (ds_ * a).astype(BF16)
                return carry_

            lax.fori_loop(0, tm // SLAB, slab, 0, unroll=True)

    return _launch(
        body, (dhb, ab, wdown), carry, name="ffn_bwd_hidden", grid=(t // tm,),
        in_specs=[_rows(tm, d), _rows(tm, 2 * f), _resident((f, d))], out_specs=[_rows(tm, 2 * f)],
        out_shape=[jax.ShapeDtypeStruct((t, 2 * f), BF16)], scratch_shapes=[pltpu.VMEM((tm, FF_CHUNK), F32)],
        semantics=("parallel",))


def _ffn_bwd_input(dab, dh, x, g, wup_t, tm):
    t, d = x.shape
    f2 = wup_t.shape[0]

    def body(dab_ref, dh_ref, x_ref, g_ref, wup_ref, dx_ref, dg_ref):
        dn = _nn(dab_ref[...], wup_ref[...])
        xh, r = _rms_fwd(x_ref[...], g_ref[...])
        dx, dg = _rms_bwd(dn, xh, r, g_ref[...])
        dx_ref[...] = dh_ref[...] + dx

        @pl.when(pl.program_id(0) == 0)
        def _():
            dg_ref[...] = jnp.zeros_like(dg_ref)

        dg_ref[...] += dg

    return pl.pallas_call(
        body, name="ffn_bwd_input", grid=(t // tm,),
        in_specs=[_rows(tm, f2), _rows(tm, d), _rows(tm, d), _resident((1, d)), _resident((f2, d))],
        out_specs=[_rows(tm, d), pl.BlockSpec((1, d), lambda i: (0, 0))],
        out_shape=[jax.ShapeDtypeStruct((t, d), F32), jax.ShapeDtypeStruct((1, d), F32)],
        compiler_params=_params("arbitrary"),
    )(dab, dh, x, g, wup_t)


def _wgrad(lhs, rhs, scale, bm, tk, name, carry=None):
    t, m = lhs.shape
    n = rhs.shape[1]
    steps = t // tk
    chunk = bm if bm <= 2048 else bm // 2

    def body(l_ref, r_ref, o_ref, acc_ref):
        @pl.when(pl.program_id(1) == 0)
        def _():
            acc_ref[...] = jnp.zeros_like(acc_ref)

        for lo in range(0, bm, chunk):
            acc_ref[lo:lo + chunk, :] += _tn(l_ref[:, lo:lo + chunk], r_ref[...])

        @pl.when(pl.program_id(1) == steps - 1)
        def _():
            o_ref[...] = (scale * acc_ref[...]).astype(o_ref.dtype)

    return _launch(
        body, (lhs, rhs), carry, name=name, grid=(m // bm, steps),
        in_specs=[pl.BlockSpec((tk, bm), lambda i, k: (k, i)), pl.BlockSpec((tk, n), lambda i, k: (k, 0))],
        out_specs=[pl.BlockSpec((bm, n), lambda i, k: (i, 0))],
        out_shape=[jax.ShapeDtypeStruct((m, n), WIRE)],
        scratch_shapes=[pltpu.VMEM((bm, n), F32)], semantics=("parallel", "arbitrary"))


def _mix_in_fwd(h, g, win_t, tm):
    t, d = h.shape

    def body(h_ref, g_ref, w_ref, u_ref, q_ref, kv_ref, z_ref, gate_ref):
        xh, _ = _rms_fwd(h_ref[...], g_ref[...])
        u = (xh * g_ref[...]).astype(BF16)
        u_ref[...] = u
        q_ref[...] = _nt(u, w_ref[0:OFF_KV, :]).astype(BF16)
        kv_ref[...] = _nt(u, w_ref[OFF_KV:OFF_Z, :]).astype(BF16)
        z_ref[...] = _nt(u, w_ref[OFF_Z:OFF_GATE, :])
        gate_ref[...] = _nt(u, w_ref[OFF_GATE:IN_WIDTH, :])

    return pl.pallas_call(
        body, name="mix_in_fwd", grid=(t // tm,),
        in_specs=[_rows(tm, d), _resident((1, d)), _resident((IN_WIDTH, d))],
        out_specs=[_rows(tm, d), _rows(tm, ATTN_WIDTH), _rows(tm, 2 * KV_WIDTH), _rows(tm, POOL_WIDTH),
                   _rows(tm, 2 * D_MODEL)],
        out_shape=[jax.ShapeDtypeStruct((t, d), BF16), jax.ShapeDtypeStruct((t, ATTN_WIDTH), BF16),
                   jax.ShapeDtypeStruct((t, 2 * KV_WIDTH), BF16), jax.ShapeDtypeStruct((t, POOL_WIDTH), F32),
                   jax.ShapeDtypeStruct((t, 2 * D_MODEL), F32)],
        compiler_params=_params("parallel"),
    )(h, g, win_t)


ALIBI_SLOPES = tuple(float(s) for s in (2.0 ** (-8.0 * np.arange(1, N_Q_HEADS + 1, dtype=np.float32) / N_Q_HEADS)))


def _attn_dist():
    return jnp.asarray(((np.arange(BLOCK)[:, None] - np.arange(BLOCK)[None, :]) % BLOCK).astype(np.float32))


def _own_block():
    shape = (BLOCK, BLOCK)
    return lax.broadcasted_iota(jnp.int32, shape, 1) <= lax.broadcasted_iota(jnp.int32, shape, 0)


def _fold(band2, own):
    return jnp.where(own, band2[:, BLOCK:], band2[:, :BLOCK])


def _unfold(x, own):
    zero = jnp.zeros_like(x)
    return jnp.concatenate([jnp.where(own, zero, x), jnp.where(own, x, zero)], axis=1)


def _low_half(shape):
    return lax.broadcasted_iota(jnp.int32, shape, len(shape) - 1) < HEAD_DIM


def _both_halves(band, kv_head):
    low = _low_half(band.shape)
    swapped = pltpu.roll(band, HEAD_DIM, 1)
    return jnp.where(low, band, swapped) if kv_head == 0 else jnp.where(low, swapped, band)


def _own_half(ref, head):
    v = ref[:, LANES * (head // 2):LANES * (head // 2 + 1)]
    low = _low_half(v.shape)
    return jnp.where(low if head % 2 == 0 else jnp.logical_not(low), v, jnp.zeros_like(v))


def _head_scores(q_ref, kk, dist, head, first, own):
    s2 = _nt(_own_half(q_ref, head) * HEAD_DIM ** -0.5, kk)
    before = jnp.where(first, -jnp.inf, s2[:, :BLOCK])
    return jnp.where(own, s2[:, BLOCK:], before) - ALIBI_SLOPES[head] * dist


def _softmax_sink(s, sink):
    m = jnp.maximum(jnp.max(s, axis=-1, keepdims=True), sink)
    p = jnp.exp(s - m)
    psink = jnp.exp(sink - m)
    inv = 1.0 / (jnp.sum(p, axis=-1, keepdims=True) + psink)
    return p * inv, psink * inv


def _bands(kvc_ref, kvp_ref):
    kband = jnp.concatenate([kvp_ref[:, 0:LANES], kvc_ref[:, 0:LANES]], axis=0)
    vband = jnp.concatenate([kvp_ref[:, LANES:2 * LANES], kvc_ref[:, LANES:2 * LANES]], axis=0)
    return ([_both_halves(kband, hk) for hk in range(N_KV_HEADS)],
            [_both_halves(vband, hk) for hk in range(N_KV_HEADS)])


SMEM = pl.BlockSpec(memory_space=pltpu.SMEM)
HEADS = range(N_Q_HEADS)


def _attn_fwd(q, kv, dist, sinks):
    t = q.shape[0]

    def body(q_ref, kvc_ref, kvp_ref, dist_ref, sink_ref, o_ref, s_scr, p_scr):
        first = pl.program_id(0) == 0
        own = _own_block()
        dist_v = dist_ref[...]
        kk, vv = _bands(kvc_ref, kvp_ref)
        for head in HEADS:
            s_scr[head] = _head_scores(q_ref, kk[head // Q_PER_KV], dist_v, head, first, own)
        for head in HEADS:
            probs, _ = _softmax_sink(s_scr[head], sink_ref[head])
            p_scr[head] = _unfold(probs.astype(BF16), own)
        for pair in range(N_Q_HEADS // 2):
            even = _nn(p_scr[2 * pair], vv[2 * pair // Q_PER_KV])
            odd = _nn(p_scr[2 * pair + 1], vv[2 * pair // Q_PER_KV])
            o_ref[:, LANES * pair:LANES * (pair + 1)] = jnp.where(_low_half(even.shape), even, odd).astype(BF16)

    return pl.pallas_call(
        body, name="attn_fwd", grid=(t // BLOCK,),
        in_specs=[_rows(BLOCK, ATTN_WIDTH), _rows(BLOCK, 2 * KV_WIDTH),
                  pl.BlockSpec((BLOCK, 2 * KV_WIDTH), lambda i: (jnp.maximum(i - 1, 0), 0)),
                  _resident(dist.shape), SMEM],
        out_specs=_rows(BLOCK, ATTN_WIDTH),
        out_shape=jax.ShapeDtypeStruct((t, ATTN_WIDTH), BF16),
        scratch_shapes=[pltpu.VMEM((N_Q_HEADS, BLOCK, BLOCK), F32), pltpu.VMEM((N_Q_HEADS, BLOCK, 2 * BLOCK), BF16)],
        compiler_params=_params("parallel"),
    )(q, kv, kv, dist, sinks)


def _pool_counts(tm, width):
    row = pl.program_id(0) * tm + lax.broadcasted_iota(jnp.int32, (tm, 1), 0)
    return jnp.minimum(row + 1, width).astype(F32)


def _trailing_sums(zz, group):
    s = zz
    for k in range(group + 1):
        s = s + pltpu.roll(s, 1 << k, 0)
    return s


def _leading_sums(zz, group):
    rows = zz.shape[0]
    s = zz
    for k in range(group + 1):
        s = s + pltpu.roll(s, rows - (1 << k), 0)
    return s


def _mix_out_fwd(attn, z, gate, h, wattn, wmix, scale, wpool_t, wout, tm):
    t, d = h.shape

    def body(attn_ref, z_ref, halo_ref, gate_ref, h_ref, wattn_ref, wmix_ref, scale_ref, wpool_ref, wout_ref,
             h2_ref, a_ref, p_ref, merged_ref, ms_ref, pooled_ref):
        halo = jnp.where(pl.program_id(0) == 0, 0.0, halo_ref[...])
        for gi, width in enumerate(POOL_WINDOWS):
            lo, hi = gi * POOL_GROUP, (gi + 1) * POOL_GROUP
            zg = z_ref[:, lo:hi]
            sums = _trailing_sums(jnp.concatenate([halo[:, lo:hi], zg], axis=0), gi)[HALO:, :]
            pooled = (sums / _pool_counts(tm, width) - zg).astype(BF16)
            pooled_ref[:, lo:hi] = pooled
            ms_ref[:, lo:hi] = (_nn(pooled, wmix_ref[gi]) * scale_ref[:, lo:hi]).astype(BF16)
        p = _nt(ms_ref[...], wpool_ref[...])
        a = _nn(attn_ref[...], wattn_ref[...])
        a_ref[...] = a
        p_ref[...] = p
        merged = (jax.nn.sigmoid(gate_ref[:, 0:d]) * a + jax.nn.sigmoid(gate_ref[:, d:2 * d]) * p).astype(BF16)
        merged_ref[...] = merged
        h2_ref[...] = h_ref[...] + _nn(merged, wout_ref[...])

    halo_spec = pl.BlockSpec((HALO, POOL_WIDTH), lambda i: (jnp.maximum(i * (tm // HALO) - 1, 0), 0))
    return pl.pallas_call(
        body, name="mix_out_fwd", grid=(t // tm,),
        in_specs=[_rows(tm, ATTN_WIDTH), _rows(tm, POOL_WIDTH), halo_spec, _rows(tm, 2 * d), _rows(tm, d),
                  _resident(wattn.shape), _resident(wmix.shape), _resident(scale.shape), _resident(wpool_t.shape),
                  _resident(wout.shape)],
        out_specs=[_rows(tm, d), _rows(tm, d), _rows(tm, d), _rows(tm, d), _rows(tm, POOL_WIDTH),
                   _rows(tm, POOL_WIDTH)],
        out_shape=[jax.ShapeDtypeStruct((t, d), F32), jax.ShapeDtypeStruct((t, d), F32),
                   jax.ShapeDtypeStruct((t, d), F32), jax.ShapeDtypeStruct((t, d), BF16),
                   jax.ShapeDtypeStruct((t, POOL_WIDTH), BF16), jax.ShapeDtypeStruct((t, POOL_WIDTH), BF16)],
        compiler_params=_params("parallel"),
    )(attn, z, z, gate, h, wattn, wmix, scale, wpool_t, wout)


def _loss_head(h, g, target, tm):
    t, d = h.shape

    def body(h_ref, g_ref, tgt_ref, loss_ref, dh_ref, dhb_ref, dg_ref):
        xh, r = _rms_fwd(h_ref[...], g_ref[...])
        err = xh * g_ref[...] - tgt_ref[...]
        part = 0.5 * jnp.sum(jnp.mean(err * err, axis=-1, keepdims=True), axis=0, keepdims=True)
        dx, dg = _rms_bwd(err * (1.0 / d), xh, r, g_ref[...])
        dh_ref[...] = dx
        dhb_ref[...] = dx.astype(BF16)

        @pl.when(pl.program_id(0) == 0)
        def _():
            dg_ref[...] = jnp.zeros_like(dg_ref)
            loss_ref[...] = jnp.zeros_like(loss_ref)

        dg_ref[...] += dg
        loss_ref[...] += jnp.broadcast_to(part, loss_ref.shape)

    return pl.pallas_call(
        body, name="loss_head", grid=(t // tm,),
        in_specs=[_rows(tm, d), _resident((1, d)), _rows(tm, d)],
        out_specs=[pl.BlockSpec((1, LANES), lambda i: (0, 0)), _rows(tm, d), _rows(tm, d),
                   pl.BlockSpec((1, d), lambda i: (0, 0))],
        out_shape=[jax.ShapeDtypeStruct((1, LANES), F32), jax.ShapeDtypeStruct((t, d), F32),
                   jax.ShapeDtypeStruct((t, d), BF16), jax.ShapeDtypeStruct((1, d), F32)],
        compiler_params=_params("arbitrary"),
    )(h, g, target)


def _mix_out_bwd(dh, gate, a, p, pooled, wattn, wmix, scale, wpool_t, wout, tm):
    t, d = dh.shape

    def body(dh_ref, gate_ref, a_ref, p_ref, pooled_ref, wattn_ref, wmix_ref, scale_ref, wpool_ref, wout_ref,
             dhb_ref, dab_ref, dpb_ref, dattn_ref, dgate_ref, dpooled_ref, dwmix_ref, dscale_ref):
        @pl.when(pl.program_id(0) == 0)
        def _():
            dwmix_ref[...] = jnp.zeros_like(dwmix_ref)
            dscale_ref[...] = jnp.zeros_like(dscale_ref)

        dhb = dh_ref[...].astype(BF16)
        dhb_ref[...] = dhb
        dm = _nt(dhb, wout_ref[...])
        sa = jax.nn.sigmoid(gate_ref[:, 0:d])
        sp = jax.nn.sigmoid(gate_ref[:, d:2 * d])
        da = (dm * sa).astype(BF16)
        dp = (dm * sp).astype(BF16)
        dab_ref[...] = da
        dpb_ref[...] = dp
        dgate_ref[:, 0:d] = (dm * a_ref[...] * (sa * (1.0 - sa))).astype(BF16)
        dgate_ref[:, d:2 * d] = (dm * p_ref[...] * (sp * (1.0 - sp))).astype(BF16)
        dattn_ref[...] = _nt(da, wattn_ref[...]).astype(BF16)
        dms = _nn(dp, wpool_ref[...])
        for gi in range(len(POOL_WINDOWS)):
            lo, hi = gi * POOL_GROUP, (gi + 1) * POOL_GROUP
            pooled_g = pooled_ref[:, lo:hi]
            mixed = _nn(pooled_g, wmix_ref[gi])
            dscale_ref[:, lo:hi] += jnp.sum(dms[:, lo:hi] * mixed, axis=0, keepdims=True)
            dmixed = (dms[:, lo:hi] * scale_ref[:, lo:hi]).astype(BF16)
            dwmix_ref[gi] += _tn(pooled_g, dmixed)
            dpooled_ref[:, lo:hi] = _nt(dmixed, wmix_ref[gi])

    acc = lambda shape: pl.BlockSpec(shape, lambda i: (0,) * len(shape))
    return pl.pallas_call(
        body, name="mix_out_bwd", grid=(t // tm,),
        in_specs=[_rows(tm, d), _rows(tm, 2 * d), _rows(tm, d), _rows(tm, d), _rows(tm, POOL_WIDTH),
                  _resident(wattn.shape), _resident(wmix.shape), _resident(scale.shape), _resident(wpool_t.shape),
                  _resident(wout.shape)],
        out_specs=[_rows(tm, d), _rows(tm, d), _rows(tm, d), _rows(tm, ATTN_WIDTH), _rows(tm, 2 * d),
                   _rows(tm, POOL_WIDTH), acc(wmix.shape), acc((1, POOL_WIDTH))],
        out_shape=[jax.ShapeDtypeStruct((t, d), BF16), jax.ShapeDtypeStruct((t, d), BF16),
                   jax.ShapeDtypeStruct((t, d), BF16), jax.ShapeDtypeStruct((t, ATTN_WIDTH), BF16),
                   jax.ShapeDtypeStruct((t, 2 * d), BF16), jax.ShapeDtypeStruct((t, POOL_WIDTH), F32),
                   jax.ShapeDtypeStruct(wmix.shape, F32), jax.ShapeDtypeStruct((1, POOL_WIDTH), F32)],
        compiler_params=_params("arbitrary"),
    )(dh, gate, a, p, pooled, wattn, wmix, scale, wpool_t, wout)


def _fold_halves(x):
    return x + pltpu.roll(x, HEAD_DIM, 1)


def _attn_bwd(q, kv, dattn, dist, sinks, carry=None):
    t = q.shape[0]

    def body(q_ref, kvc_ref, kvp_ref, do_ref, dist_ref, sink_ref, dq_ref, dkv_own_ref, dkv_prev_ref, dsink_ref,
             s_scr, dp_scr, p_scr, ds_scr):
        first = pl.program_id(0) == 0

        @pl.when(first)
        def _():
            dsink_ref[...] = jnp.zeros_like(dsink_ref)

        own = _own_block()
        dist_v = dist_ref[...]
        kk, vv = _bands(kvc_ref, kvp_ref)
        lane = lax.broadcasted_iota(jnp.int32, (1, LANES), 1)
        for head in HEADS:
            hk = head // Q_PER_KV
            s_scr[head] = _head_scores(q_ref, kk[hk], dist_v, head, first, own)
            dp_scr[head] = _fold(_nt(_own_half(do_ref, head), vv[hk]), own)
        dsink = jnp.zeros((1, LANES), F32)
        for head in HEADS:
            probs, psink = _softmax_sink(s_scr[head], sink_ref[head])
            dprobs = dp_scr[head]
            rowdot = jnp.sum(probs * dprobs, axis=-1, keepdims=True)
            p_scr[head] = _unfold(probs.astype(BF16), own)
            ds_scr[head] = _unfold((probs * (dprobs - rowdot)).astype(BF16), own)
            dsink = dsink + jnp.where(lane == head, jnp.sum(-psink * rowdot, axis=0, keepdims=True), 0.0)
        dk_heads, dv_heads = [], []
        for hk in range(N_KV_HEADS):
            dk_t = jnp.zeros((LANES, 2 * BLOCK), F32)
            dv_t = jnp.zeros((LANES, 2 * BLOCK), F32)
            for pair in range(Q_PER_KV // 2):
                col = LANES * (hk * (Q_PER_KV // 2) + pair)
                q_t = (q_ref[:, col:col + LANES] * HEAD_DIM ** -0.5).T
                do_t = do_ref[:, col:col + LANES].T
                dqs = []
                for head in (hk * Q_PER_KV + 2 * pair, hk * Q_PER_KV + 2 * pair + 1):
                    mine = (lax.broadcasted_iota(jnp.int32, q_t.shape, 0) < HEAD_DIM) == (head % 2 == 0)
                    dv_t = dv_t + _nn(jnp.where(mine, do_t, jnp.zeros_like(do_t)), p_scr[head])
                    dk_t = dk_t + _nn(jnp.where(mine, q_t, jnp.zeros_like(q_t)), ds_scr[head])
                    dqs.append(_nn(ds_scr[head], kk[hk]))
                dq_pair = jnp.where(_low_half(dqs[0].shape), dqs[0], dqs[1])
                dq_ref[:, col:col + LANES] = (dq_pair * HEAD_DIM ** -0.5).astype(BF16)
            dk_heads.append(_fold_halves(dk_t.T))
            dv_heads.append(_fold_halves(dv_t.T))
        low = _low_half(dk_heads[0].shape)
        dkv = jnp.concatenate([jnp.where(low, dk_heads[0], dk_heads[1]), jnp.where(low, dv_heads[0], dv_heads[1])],
                              axis=1)
        dkv_prev_ref[...] = dkv[0:BLOCK, :]
        dkv_own_ref[...] = dkv[BLOCK:2 * BLOCK, :]
        dsink_ref[...] += dsink

    return _launch(
        body, (q, kv, kv, dattn, dist, sinks), carry, name="attn_bwd", grid=(t // BLOCK,),
        in_specs=[_rows(BLOCK, ATTN_WIDTH), _rows(BLOCK, 2 * KV_WIDTH),
                  pl.BlockSpec((BLOCK, 2 * KV_WIDTH), lambda i: (jnp.maximum(i - 1, 0), 0)),
                  _rows(BLOCK, ATTN_WIDTH), _resident(dist.shape), SMEM],
        out_specs=[_rows(BLOCK, ATTN_WIDTH), _rows(BLOCK, 2 * KV_WIDTH), _rows(BLOCK, 2 * KV_WIDTH),
                   pl.BlockSpec((1, LANES), lambda i: (0, 0))],
        out_shape=[jax.ShapeDtypeStruct((t, ATTN_WIDTH), BF16), jax.ShapeDtypeStruct((t, 2 * KV_WIDTH), F32),
                   jax.ShapeDtypeStruct((t, 2 * KV_WIDTH), F32), jax.ShapeDtypeStruct((1, LANES), F32)],
        scratch_shapes=[pltpu.VMEM((N_Q_HEADS, BLOCK, BLOCK), F32), pltpu.VMEM((N_Q_HEADS, BLOCK, BLOCK), F32),
                        pltpu.VMEM((N_Q_HEADS, BLOCK, 2 * BLOCK), BF16),
                        pltpu.VMEM((N_Q_HEADS, BLOCK, 2 * BLOCK), BF16)],
        semantics=("arbitrary",))


def _mix_in_bwd(dq, dkv_own, dkv_prev, dpooled, dgate, h, g, win_t, dh_res, tm, carry=None):
    t, d = h.shape
    nt = t // tm

    def body(dq_ref, own_ref, prev_ref, prev_next_ref, dpool_ref, halo_ref, dgate_ref, h_ref, g_ref, w_ref, res_ref,
             dproj_ref, dh_ref, dhb_ref, dg_ref):
        i = pl.program_id(0)
        last = i == nt - 1
        dproj_ref[:, 0:OFF_KV] = dq_ref[...]
        from_next = jnp.where(last, 0.0, prev_next_ref[...])
        if tm > BLOCK:
            from_next = jnp.concatenate([prev_ref[BLOCK:tm, :], from_next], axis=0)
        dproj_ref[:, OFF_KV:OFF_Z] = (own_ref[...] + from_next).astype(BF16)
        halo = jnp.where(last, 0.0, halo_ref[...])
        for gi, width in enumerate(POOL_WINDOWS):
            lo, hi = gi * POOL_GROUP, (gi + 1) * POOL_GROUP
            dpg = dpool_ref[:, lo:hi]
            scaled = jnp.concatenate([dpg / _pool_counts(tm, width), halo[:, lo:hi] / float(width)], axis=0)
            dz = _leading_sums(scaled, gi)[0:tm, :] - dpg
            dproj_ref[:, OFF_Z + lo:OFF_Z + hi] = dz.astype(BF16)
        dproj_ref[:, OFF_GATE:IN_WIDTH] = dgate_ref[...]
        du = _nn(dproj_ref[...], w_ref[...])
        xh, r = _rms_fwd(h_ref[...], g_ref[...])
        dx, dg = _rms_bwd(du, xh, r, g_ref[...])
        dh = res_ref[...] + dx
        dh_ref[...] = dh
        dhb_ref[...] = dh.astype(BF16)

        @pl.when(i == 0)
        def _():
            dg_ref[...] = jnp.zeros_like(dg_ref)

        dg_ref[...] += dg

    per = tm // BLOCK
    next_block = pl.BlockSpec((BLOCK, 2 * KV_WIDTH), lambda i: (jnp.minimum((i + 1) * per, t // BLOCK - 1), 0))
    next_halo = pl.BlockSpec((HALO, POOL_WIDTH), lambda i: (jnp.minimum((i + 1) * (tm // HALO), t // HALO - 1), 0))
    return _launch(
        body, (dq, dkv_own, dkv_prev, dkv_prev, dpooled, dpooled, dgate, h, g, win_t, dh_res), carry,
        name="mix_in_bwd", grid=(nt,),
        in_specs=[_rows(tm, ATTN_WIDTH), _rows(tm, 2 * KV_WIDTH), _rows(tm, 2 * KV_WIDTH), next_block,
                  _rows(tm, POOL_WIDTH), next_halo, _rows(tm, 2 * d), _rows(tm, d), _resident((1, d)),
                  _resident((IN_WIDTH, d)), _rows(tm, d)],
        out_specs=[_rows(tm, IN_WIDTH), _rows(tm, d), _rows(tm, d), pl.BlockSpec((1, d), lambda i: (0, 0))],
        out_shape=[jax.ShapeDtypeStruct((t, IN_WIDTH), BF16), jax.ShapeDtypeStruct((t, d), F32),
                   jax.ShapeDtypeStruct((t, d), BF16), jax.ShapeDtypeStruct((1, d), F32)],
        semantics=("arbitrary",))


BIG = (("wup1_t", "ffn1_w_up", True), ("wdown1", "ffn1_w_down", False), ("win_t", "w_in", True),
       ("wattn", "w_attn_up", False), ("wpool_t", "w_pool_up", True), ("wout", "w_out", False),
       ("wup2_t", "ffn2_w_up", True), ("wdown2", "ffn2_w_down", False))
ANY = pl.BlockSpec(memory_space=pl.ANY)
WIRE = BF16


def _place():
    return lax.axis_index("x"), lax.axis_index("y"), lax.axis_index("c")


def _peer(k):
    x, y, c = _place()
    return x ^ (k >> 2), y ^ ((k >> 1) & 1), c ^ (k & 1)


def _index(px, py, pc):
    return 4 * px + 2 * py + pc


def _gather_carry(shards):
    n = len(shards)

    def tools(ins, outs, sems):
        send_sems, recv_sems, local_sems = sems
        x, y, c = _place()
        chips = [(1 - x, y), (x, 1 - y), (1 - x, 1 - y)]

        def rows(w, px, py, pc):
            r = ins[w].shape[0]
            return outs[w].at[pl.ds(_index(px, py, pc) * r, r), :]

        def copy(w, k, block, to, src=None):
            return pltpu.make_async_remote_copy(
                src_ref=rows(w, *block) if src is None else src, dst_ref=rows(w, *block),
                send_sem=send_sems.at[w, k], recv_sem=recv_sems.at[w, k], device_id=to, device_id_type=MESH)

        def own(w):
            return ([pltpu.make_async_copy(ins[w], rows(w, x, y, c), local_sems.at[w]),
                     copy(w, 0, (x, y, c), (x, y, 1 - c), src=ins[w])]
                    + [copy(w, 1 + j, (x, y, c), (*chip, c), src=ins[w]) for j, chip in enumerate(chips)])

        def passed(w, j):
            return copy(w, 4 + j, (*chips[j], c), (x, y, 1 - c))

        return (x, y, c), chips, copy, own, passed

    def start(ins, outs, sems):
        _, _, _, own, _ = tools(ins, outs, sems)
        for w in range(n):
            for cp in own(w):
                cp.start()

    def forward(w):
        def run(ins, outs, sems):
            (x, y, c), chips, copy, _, passed = tools(ins, outs, sems)
            for j, chip in enumerate(chips):
                copy(w, 1 + j, (*chip, c), (x, y, c)).wait_recv()
                passed(w, j).start()
        return run

    sizes = np.cumsum([s.size for s in shards]) / sum(s.size for s in shards)
    middles = [(float(sizes[w]), forward(w)) for w in range(n)]

    def finish(ins, outs, sems):
        (x, y, c), chips, copy, own, passed = tools(ins, outs, sems)
        for w in range(n):
            copy(w, 0, (x, y, 1 - c), (x, y, c)).wait_recv()
            for j, chip in enumerate(chips):
                copy(w, 4 + j, (*chip, 1 - c), (x, y, c)).wait_recv()
        for w in range(n):
            mine, *sent = own(w)
            for cp in sent + [passed(w, j) for j in range(len(chips))]:
                cp.wait_send()
            mine.wait()

    return _Carry(
        shards, [jax.ShapeDtypeStruct((N_DEV * s.shape[0], s.shape[1]), s.dtype) for s in shards],
        [pltpu.SemaphoreType.DMA((n, N_DEV - 1)), pltpu.SemaphoreType.DMA((n, N_DEV - 1)),
         pltpu.SemaphoreType.DMA((n,))], start, finish, middles)


def _scatter_carry(grads):
    n = len(grads)

    def tools(ins, outs, sems):
        send_sems, recv_sems, local_sems = sems
        me = _index(*_place())

        def block(ref, dev):
            r = ref.shape[0] // N_DEV
            return ref.at[pl.ds(dev * r, r), :]

        def copy(w, k, landing):
            to = _peer(k)
            return pltpu.make_async_remote_copy(
                src_ref=block(ins[w], _index(*to)), dst_ref=block(outs[w], landing), send_sem=send_sems.at[w, k - 1],
                recv_sem=recv_sems.at[w, k - 1], device_id=to, device_id_type=MESH)

        def mine(w):
            return pltpu.make_async_copy(block(ins[w], me), block(outs[w], me), local_sems.at[w])

        return me, copy, mine

    def start(ins, outs, sems):
        me, copy, mine = tools(ins, outs, sems)
        for w in range(n):
            mine(w).start()
            for k in range(1, N_DEV):
                copy(w, k, me).start()

    def finish(ins, outs, sems):
        _, copy, mine = tools(ins, outs, sems)
        for w in range(n):
            for k in range(1, N_DEV):
                copy(w, k, _index(*_peer(k))).wait()
            mine(w).wait()

    return _Carry(
        grads, [jax.ShapeDtypeStruct(g.shape, g.dtype) for g in grads],
        [pltpu.SemaphoreType.DMA((n, N_DEV - 1)), pltpu.SemaphoreType.DMA((n, N_DEV - 1)),
         pltpu.SemaphoreType.DMA((n,))], start, finish)


def _small_carry(small):
    srows = small.shape[0]

    def tools(ins, outs, sems):
        send_sems, recv_sems, local_sem = sems
        me = _index(*_place())

        def slot(dev):
            return outs[0].at[pl.ds(dev * srows, srows), :]

        def copy(k, landing):
            return pltpu.make_async_remote_copy(
                src_ref=ins[0], dst_ref=slot(landing), send_sem=send_sems.at[k - 1], recv_sem=recv_sems.at[k - 1],
                device_id=_peer(k), device_id_type=MESH)

        return me, copy, pltpu.make_async_copy(ins[0], slot(me), local_sem)

    def start(ins, outs, sems):
        me, copy, mine = tools(ins, outs, sems)
        mine.start()
        for k in range(1, N_DEV):
            copy(k, me).start()

    def finish(ins, outs, sems):
        _, copy, mine = tools(ins, outs, sems)
        for k in range(1, N_DEV):
            copy(k, _index(*_peer(k))).wait()
        mine.wait()

    return _Carry([small], [jax.ShapeDtypeStruct((N_DEV * srows, LANES), small.dtype)],
                  [pltpu.SemaphoreType.DMA((N_DEV - 1,)), pltpu.SemaphoreType.DMA((N_DEV - 1,)),
                   pltpu.SemaphoreType.DMA], start, finish)


def _exchange(carry, name):
    ci = len(carry.inputs)
    co = len(carry.out_shape)

    def body(*refs):
        parts = refs[:ci], refs[ci:ci + co], refs[ci + co:]
        carry.start(*parts)
        for _, fn in carry.middles:
            fn(*parts)
        carry.finish(*parts)

    return list(pl.pallas_call(body, name=name, in_specs=[ANY] * ci, out_specs=[ANY] * co, out_shape=carry.out_shape,
                               scratch_shapes=carry.scratch)(*carry.inputs))


def _adamw_math(w, g, m, v):
    m = ADAM_B1 * m + (1.0 - ADAM_B1) * g
    v = ADAM_B2 * v + (1.0 - ADAM_B2) * (g * g)
    m_hat = m / (1.0 - ADAM_B1 ** ADAM_STEP)
    v_hat = v / (1.0 - ADAM_B2 ** ADAM_STEP)
    return -ADAM_LR * (m_hat / (jnp.sqrt(v_hat) + ADAM_EPS) + ADAM_WD * w), m, v


def _sum_adamw(got, w, m, v, transposed, name):
    r, cols = got.shape[0] // N_DEV, got.shape[1]
    if transposed:
        tile = cols if cols <= 512 else 256
        got_spec = pl.BlockSpec((N_DEV, r, tile), lambda i: (0, 0, i))
        spec, steps = pl.BlockSpec((tile, r), lambda i: (i, 0)), cols // tile
    else:
        tile = r if r <= 256 else r // 2
        got_spec = pl.BlockSpec((N_DEV, tile, cols), lambda i: (0, i, 0))
        spec, steps = pl.BlockSpec((tile, cols), lambda i: (i, 0)), r // tile

    def body(got_ref, w_ref, m_ref, v_ref, g_ref, d_ref, m2_ref, v2_ref):
        acc = got_ref[0].astype(F32)
        for dev in range(1, N_DEV):
            acc = acc + got_ref[dev].astype(F32)
        g = acc.T if transposed else acc
        g_ref[...] = g
        d_ref[...], m2_ref[...], v2_ref[...] = _adamw_math(w_ref[...], g, m_ref[...], v_ref[...])

    return pl.pallas_call(
        body, name=name, grid=(steps,), in_specs=[got_spec, spec, spec, spec], out_specs=[spec] * 4,
        out_shape=[jax.ShapeDtypeStruct(w.shape, F32)] * 4, compiler_params=_params("parallel"),
    )(got.reshape(N_DEV, r, cols), w, m, v)


def _small_update(gathered, w, m, v):
    rows = w.shape[0]

    def body(all_ref, w_ref, m_ref, v_ref, g_ref, d_ref, m2_ref, v2_ref):
        g = all_ref[0]
        for dev in range(1, N_DEV):
            g = g + all_ref[dev]
        g_ref[...] = g
        d_ref[...], m2_ref[...], v2_ref[...] = _adamw_math(w_ref[...], g, m_ref[...], v_ref[...])

    return pl.pallas_call(
        body, name="small_update", out_shape=[jax.ShapeDtypeStruct((rows, LANES), F32)] * 4,
        compiler_params=pltpu.CompilerParams(vmem_limit_bytes=VMEM_LIMIT),
    )(gathered.reshape(N_DEV, rows, LANES), w, m, v)


SMALL = (("pool_w_mix", 512), ("ffn1_norm", 8), ("mix_norm", 8), ("ffn2_norm", 8), ("final_norm", 8),
         ("pool_scale", 8), ("sinks", 8), ("loss", 8))
SMALL_ROWS = sum(rows for _, rows in SMALL)


def _pack_small(parts):
    out = []
    for name, rows in SMALL:
        flat = parts[name].astype(F32).reshape(-1)
        out.append(jnp.pad(flat, (0, rows * LANES - flat.shape[0])).reshape(rows, LANES))
    return jnp.concatenate(out, axis=0)


def _unpack_small(packed, shapes):
    out, row = {}, 0
    for name, rows in SMALL:
        shape = shapes[name]
        size = int(np.prod(shape)) if shape else 1
        out[name] = packed[row:row + rows].reshape(-1)[:size].reshape(shape)
        row += rows
    return out


def kernel(x, ffn1_norm, ffn1_w_up, ffn1_w_down, mix_norm, w_in, sinks, w_attn_up, pool_w_mix, pool_scale, w_pool_up, w_out, ffn2_norm, ffn2_w_up, ffn2_w_down, final_norm, loss_target, m_ffn1_norm, m_ffn1_w_up, m_ffn1_w_down, m_mix_norm, m_w_in, m_sinks, m_w_attn_up, m_pool_w_mix, m_pool_scale, m_w_pool_up, m_w_out, m_ffn2_norm, m_ffn2_w_up, m_ffn2_w_down, m_final_norm, v_ffn1_norm, v_ffn1_w_up, v_ffn1_w_down, v_mix_norm, v_w_in, v_sinks, v_w_attn_up, v_pool_w_mix, v_pool_scale, v_w_pool_up, v_w_out, v_ffn2_norm, v_ffn2_w_up, v_ffn2_w_down, v_final_norm):
    args = dict(locals())
    weight_names = ("ffn1_norm", "ffn1_w_up", "ffn1_w_down", "mix_norm", "w_in", "sinks", "w_attn_up", "pool_w_mix",
                    "pool_scale", "w_pool_up", "w_out", "ffn2_norm", "ffn2_w_up", "ffn2_w_down", "final_norm")

    shard = {k: (args[p][0].T if tr else args[p][0]).astype(BF16) for k, p, tr in BIG}
    later = [k for k, _, _ in BIG[2:]]
    big = dict(zip(("wup1_t", "wdown1"), _exchange(_gather_carry([shard["wup1_t"], shard["wdown1"]]), "gather_ffn1")))

    xs, target = x[0], loss_target[0]
    t = xs.shape[0]
    tm_f, tm_b, tk = min(512, t), min(256, t), min(1024, t)
    g1, gm, g2, gf = ffn1_norm, mix_norm, ffn2_norm, final_norm.reshape(1, D_MODEL)
    dist = _attn_dist()
    sink_v = sinks.reshape(N_Q_HEADS)
    wmix_b = pool_w_mix[0].astype(BF16)

    (h1, ab1, n1, act1), rest = _ffn_fwd(xs, g1, big["wup1_t"], big["wdown1"], tm_f,
                                         _gather_carry([shard[k] for k in later]))
    big.update(zip(later, rest))
    u, q, kv, z, gate = _mix_in_fwd(h1, gm, big["win_t"], tm_f)
    attn = _attn_fwd(q, kv, dist, sink_v)
    h2, a, p, merged, ms, pooled = _mix_out_fwd(attn, z, gate, h1, big["wattn"], wmix_b, pool_scale, big["wpool_t"],
                                                big["wout"], tm_b)
    (h3, ab2, n2, act2), _ = _ffn_fwd(h2, g2, big["wup2_t"], big["wdown2"], tm_f)
    loss_lanes, dh3, dhb3, dgf = _loss_head(h3, gf, target, tm_f)

    got = {}
    (gw_down2,), _ = _wgrad(act2, dhb3, 0.5, D_FF, tk, "wgrad_down2")
    (dab2,), (got["wdown2"],) = _ffn_bwd_hidden(dhb3, ab2, big["wdown2"], tm_f, _scatter_carry([gw_down2]))
    dh2, dg2 = _ffn_bwd_input(dab2, dh3, h2, g2, big["wup2_t"], tm_f)
    (gw_up2,), _ = _wgrad(dab2, n2, 1.0, D_FF, tk, "wgrad_up2")
    dhb2, da_b, dp_b, dattn, dgate, dpooled, dwmix, dscale = _mix_out_bwd(
        dh2, gate, a, p, pooled, big["wattn"], wmix_b, pool_scale, big["wpool_t"], big["wout"], tm_b)
    (gw_out,), _ = _wgrad(merged, dhb2, 1.0, D_MODEL, tk, "wgrad_out")
    (gw_attn,), _ = _wgrad(attn, da_b, 1.0, D_MODEL, tk, "wgrad_attn")
    (gw_pool,), _ = _wgrad(dp_b, ms, 1.0, D_MODEL, tk, "wgrad_pool")
    (dq, dkv_own, dkv_prev, dsinks), (got["wup2_t"],) = _attn_bwd(q, kv, dattn, dist, sink_v, _scatter_carry([gw_up2]))
    (dproj, dh1, dhb1, dgm), (got["wout"], got["wattn"], got["wpool_t"]) = _mix_in_bwd(
        dq, dkv_own, dkv_prev, dpooled, dgate, h1, gm, big["win_t"], dh2, tm_b,
        _scatter_carry([gw_out, gw_attn, gw_pool]))
    (gw_in,), _ = _wgrad(dproj, u, 1.0, IN_WIDTH // 2, tk, "wgrad_in")
    (gw_down1,), _ = _wgrad(act1, dhb1, 0.5, D_FF, tk, "wgrad_down1")
    (dab1,), (got["win_t"],) = _ffn_bwd_hidden(dhb1, ab1, big["wdown1"], tm_f, _scatter_carry([gw_in]))
    dx, dg1 = _ffn_bwd_input(dab1, dh1, xs, g1, big["wup1_t"], tm_f)
    small_parts = {"pool_w_mix": dwmix, "ffn1_norm": dg1, "mix_norm": dgm, "ffn2_norm": dg2, "final_norm": dgf,
                   "pool_scale": dscale, "sinks": dsinks[:, :N_Q_HEADS], "loss": loss_lanes[:, :1]}
    (gw_up1,), (got["wdown1"], small_all) = _wgrad(
        dab1, n1, 1.0, D_FF, tk, "wgrad_up1", _merge(_scatter_carry([gw_down1]), _small_carry(_pack_small(small_parts))))
    (got["wup1_t"],) = _exchange(_scatter_carry([gw_up1]), "scatter_up1")

    grad, delta, new_m, new_v = {}, {}, {}, {}
    for k, p, tr in BIG:
        g, d, m2, v2 = _sum_adamw(got[k], args[p][0], args["m_" + p][0], args["v_" + p][0], tr, "adamw_" + k)
        grad[p], delta[p], new_m[p], new_v[p] = g[None], d[None], m2[None], v2[None]

    shapes = {name: args[name].shape for name, _ in SMALL if name != "loss"}
    shapes["loss"] = ()
    packed = {pre: _pack_small({**{name: args[pre + name] for name, _ in SMALL if name != "loss"},
                                "loss": jnp.zeros((), F32)}) for pre in ("", "m_", "v_")}
    g_s, d_s, m_s, v_s = _small_update(small_all, packed[""], packed["m_"], packed["v_"])
    g_small, d_small, m_small, v_small = (_unpack_small(a, shapes) for a in (g_s, d_s, m_s, v_s))
    for name, _ in SMALL:
        if name != "loss":
            grad[name], delta[name], new_m[name], new_v[name] = (
                g_small[name], d_small[name], m_small[name], v_small[name])

    return (g_small["loss"], dx[None], *[grad[n] for n in weight_names], *[delta[n] for n in weight_names],
            *[new_m[n] for n in weight_names], *[new_v[n] for n in weight_names])
```

```python
import functools

import jax
import jax.numpy as jnp
import numpy as np
from jax import lax
from jax.experimental import pallas as pl
from jax.experimental.pallas import tpu as pltpu

F32 = jnp.float32
BF16 = jnp.bfloat16

D_MODEL = 1024
D_FF = 2816
N_Q_HEADS = 16
N_KV_HEADS = 2
Q_PER_KV = N_Q_HEADS // N_KV_HEADS
HEAD_DIM = 64
BLOCK = 128
ATTN_WIDTH = N_Q_HEADS * HEAD_DIM
KV_WIDTH = N_KV_HEADS * HEAD_DIM
POOL_WINDOWS = (2, 4, 8, 16)
POOL_GROUP = 128
POOL_WIDTH = 512
HALO = 16
IN_WIDTH = ATTN_WIDTH + 2 * KV_WIDTH + POOL_WIDTH + 2 * D_MODEL
OFF_KV = ATTN_WIDTH
OFF_Z = ATTN_WIDTH + 2 * KV_WIDTH
OFF_GATE = OFF_Z + POOL_WIDTH
NORM_EPS = 1e-6
ADAM_LR = 0.001
ADAM_B1 = 0.9
ADAM_B2 = 0.999
ADAM_EPS = 1e-08
ADAM_WD = 0.01
ADAM_STEP = 10

N_DEV = 8
N_CHIP = 4
LANES = 128
FF_CHUNK = 256
SLAB = 32
VMEM_LIMIT = 56 * 1024 * 1024
MESH = pl.DeviceIdType.MESH


def _nn(a, b):
    return jnp.dot(a, b, preferred_element_type=F32)


def _nt(a, b):
    return lax.dot_general(a, b, (((1,), (1,)), ((), ())), preferred_element_type=F32)


def _tn(a, b):
    return lax.dot_general(a, b, (((0,), (0,)), ((), ())), preferred_element_type=F32)


def _params(*sem):
    return pltpu.CompilerParams(dimension_semantics=sem, vmem_limit_bytes=VMEM_LIMIT)


def _resident(shape):
    return pl.BlockSpec(shape, lambda *_: (0,) * len(shape), pipeline_mode=pl.Buffered(1))


def _rows(tm, cols):
    return pl.BlockSpec((tm, cols), lambda i: (i, 0))


class _Carry:
    def __init__(self, inputs, out_shape, scratch, start, finish, middles=()):
        self.inputs, self.out_shape, self.scratch = list(inputs), list(out_shape), list(scratch)
        self.start, self.finish, self.middles = start, finish, list(middles)


def _merge(a, b):
    ia, oa, sa = len(a.inputs), len(a.out_shape), len(a.scratch)

    def first(fn):
        return lambda ins, outs, sems: fn(ins[:ia], outs[:oa], sems[:sa])

    def second(fn):
        return lambda ins, outs, sems: fn(ins[ia:], outs[oa:], sems[sa:])

    def both(fa, fb):
        def run(ins, outs, sems):
            first(fa)(ins, outs, sems)
            second(fb)(ins, outs, sems)
        return run

    middles = [(f, first(fn)) for f, fn in a.middles] + [(f, second(fn)) for f, fn in b.middles]
    return _Carry(a.inputs + b.inputs, a.out_shape + b.out_shape, a.scratch + b.scratch, both(a.start, b.start),
                  both(a.finish, b.finish), middles)


def _launch(body, args, carry=None, *, name, grid, in_specs, out_specs, out_shape, scratch_shapes=(), semantics):
    in_specs, out_specs, out_shape, scratch_shapes = list(in_specs), list(out_specs), list(out_shape), list(scratch_shapes)
    if carry is None:
        res = pl.pallas_call(body, name=name, grid=grid, in_specs=in_specs, out_specs=out_specs, out_shape=out_shape,
                             scratch_shapes=scratch_shapes, compiler_params=_params(*semantics))(*args)
        return list(res), []
    ni, no, ns = len(in_specs), len(out_specs), len(scratch_shapes)
    ci, co = len(carry.inputs), len(carry.out_shape)
    total = int(np.prod(grid))

    def full(*refs):
        own_in, c_in = refs[:ni], refs[ni:ni + ci]
        own_out, c_out = refs[ni + ci:ni + ci + no], refs[ni + ci + no:ni + ci + no + co]
        own_scr, c_sem = refs[ni + ci + no + co:ni + ci + no + co + ns], refs[ni + ci + no + co + ns:]
        step = 0
        for axis, size in enumerate(grid):
            step = step * size + pl.program_id(axis)
        pl.when(step == 0)(lambda: carry.start(c_in, c_out, c_sem))
        for fraction, fn in carry.middles:
            at = min(total - 1, int(fraction * total) + 1)
            pl.when(step == at)(lambda fn=fn: fn(c_in, c_out, c_sem))
        body(*own_in, *own_out, *own_scr)
        pl.when(step == total - 1)(lambda: carry.finish(c_in, c_out, c_sem))

    res = pl.pallas_call(
        full, name=name, grid=grid, in_specs=in_specs + [ANY] * ci, out_specs=out_specs + [ANY] * co,
        out_shape=out_shape + carry.out_shape, scratch_shapes=scratch_shapes + carry.scratch,
        compiler_params=_params(*(["arbitrary"] * len(grid))),
    )(*args, *carry.inputs)
    return list(res[:no]), list(res[no:])


def _rms_fwd(xv, g):
    r = lax.rsqrt(jnp.mean(xv * xv, axis=-1, keepdims=True) + NORM_EPS)
    return xv * r, r


def _rms_bwd(dn, xh, r, g):
    dxh = dn * g
    dx = r * (dxh - xh * jnp.mean(dxh * xh, axis=-1, keepdims=True))
    return dx, jnp.sum(dn * xh, axis=0, keepdims=True)


def _ffn_fwd(x, g, wup_t, wdown, tm, carry=None):
    t, d = x.shape
    f = wdown.shape[0]

    def body(x_ref, g_ref, wup_ref, wdn_ref, h_ref, ab_ref, n_ref, act_ref):
        xv = x_ref[...]
        xh, _ = _rms_fwd(xv, g_ref[...])
        n = (xh * g_ref[...]).astype(BF16)
        n_ref[...] = n
        for c in range(f // FF_CHUNK):
            lo, hi = c * FF_CHUNK, (c + 1) * FF_CHUNK
            a = _nt(n, wup_ref[lo:hi, :])
            b = _nt(n, wup_ref[f + lo:f + hi, :])
            ab_ref[:, lo:hi] = a.astype(BF16)
            ab_ref[:, f + lo:f + hi] = b.astype(BF16)
            act_ref[:, lo:hi] = (a * jax.nn.sigmoid(a) * b).astype(BF16)
        h_ref[...] = xv + 0.5 * _nn(act_ref[...], wdn_ref[...])

    return _launch(
        body, (x, g, wup_t, wdown), carry, name="ffn_fwd", grid=(t // tm,),
        in_specs=[_rows(tm, d), _resident((1, d)), _resident((2 * f, d)), _resident((f, d))],
        out_specs=[_rows(tm, d), _rows(tm, 2 * f), _rows(tm, d), _rows(tm, f)],
        out_shape=[jax.ShapeDtypeStruct((t, d), F32), jax.ShapeDtypeStruct((t, 2 * f), BF16),
                   jax.ShapeDtypeStruct((t, d), BF16), jax.ShapeDtypeStruct((t, f), BF16)],
        semantics=("parallel",))


def _ffn_up(x, g, wup_t, tm, carry=None):
    t, d = x.shape
    f = wup_t.shape[0] // 2

    def body(x_ref, g_ref, wup_ref, n_ref, ab_ref, act_ref):
        xh, _ = _rms_fwd(x_ref[...], g_ref[...])
        n = (xh * g_ref[...]).astype(BF16)
        n_ref[...] = n
        for c in range(f // FF_CHUNK):
            lo, hi = c * FF_CHUNK, (c + 1) * FF_CHUNK
            a = _nt(n, wup_ref[lo:hi, :])
            b = _nt(n, wup_ref[f + lo:f + hi, :])
            ab_ref[:, lo:hi] = a.astype(BF16)
            ab_ref[:, f + lo:f + hi] = b.astype(BF16)
            act_ref[:, lo:hi] = (a * jax.nn.sigmoid(a) * b).astype(BF16)

    return _launch(
        body, (x, g, wup_t), carry, name="ffn_up", grid=(t // tm,),
        in_specs=[_rows(tm, d), _resident((1, d)), _resident((2 * f, d))],
        out_specs=[_rows(tm, d), _rows(tm, 2 * f), _rows(tm, f)],
        out_shape=[jax.ShapeDtypeStruct((t, d), BF16), jax.ShapeDtypeStruct((t, 2 * f), BF16),
                   jax.ShapeDtypeStruct((t, f), BF16)],
        semantics=("parallel",))


def _ffn_down(x, act, wdown, tm):
    t, d = x.shape
    f = wdown.shape[0]

    def body(x_ref, act_ref, wdn_ref, h_ref):
        h_ref[...] = x_ref[...] + 0.5 * _nn(act_ref[...], wdn_ref[...])

    return pl.pallas_call(
        body, name="ffn_down", grid=(t // tm,),
        in_specs=[_rows(tm, d), _rows(tm, f), _resident((f, d))], out_specs=_rows(tm, d),
        out_shape=jax.ShapeDtypeStruct((t, d), F32), compiler_params=_params("parallel"),
    )(x, act, wdown)


def _ffn_bwd_hidden(dhb, ab, wdown, tm, carry=None):
    t, d = dhb.shape
    f = wdown.shape[0]

    def body(dh_ref, ab_ref, wdn_ref, dab_ref, dact_ref):
        half = dh_ref[...] * 0.5
        for c in range(f // FF_CHUNK):
            lo, hi = c * FF_CHUNK, (c + 1) * FF_CHUNK
            dact_ref[...] = _nt(half, wdn_ref[lo:hi, :])

            def slab(i, carry_):
                rows = pl.ds(pl.multiple_of(i * SLAB, SLAB), SLAB)
                a = ab_ref[rows, lo:hi].astype(F32)
                b = ab_ref[rows, f + lo:f + hi].astype(F32)
                s = jax.nn.sigmoid(a)
                ds_ = dact_ref[rows, :] * s
                dab_ref[rows, lo:hi] = (ds_ * b * (1.0 + a * (1.0 - s))).astype(BF16)
                dab_ref[rows, f + lo:f + hi] = (ds_ * a).astype(BF16)
                return carry_

            lax.fori_loop(0, tm // SLAB, slab, 0, unroll=True)

    return _launch(
        body, (dhb, ab, wdown), carry, name="ffn_bwd_hidden", grid=(t // tm,),
        in_specs=[_rows(tm, d), _rows(tm, 2 * f), _resident((f, d))], out_specs=[_rows(tm, 2 * f)],
        out_shape=[jax.ShapeDtypeStruct((t, 2 * f), BF16)], scratch_shapes=[pltpu.VMEM((tm, FF_CHUNK), F32)],
        semantics=("parallel",))


def _ffn_bwd_input(dab, dh, x, g, wup_t, tm):
    t, d = x.shape
    f2 = wup_t.shape[0]

    def body(dab_ref, dh_ref, x_ref, g_ref, wup_ref, dx_ref, dg_ref):
        dn = _nn(dab_ref[...], wup_ref[...])
        xh, r = _rms_fwd(x_ref[...], g_ref[...])
        dx, dg = _rms_bwd(dn, xh, r, g_ref[...])
        dx_ref[...] = dh_ref[...] + dx

        @pl.when(pl.program_id(0) == 0)
        def _():
            dg_ref[...] = jnp.zeros_like(dg_ref)

        dg_ref[...] += dg

    return pl.pallas_call(
        body, name="ffn_bwd_input", grid=(t // tm,),
        in_specs=[_rows(tm, f2), _rows(tm, d), _rows(tm, d), _resident((1, d)), _resident((f2, d))],
        out_specs=[_rows(tm, d), pl.BlockSpec((1, d), lambda i: (0, 0))],
        out_shape=[jax.ShapeDtypeStruct((t, d), F32), jax.ShapeDtypeStruct((1, d), F32)],
        compiler_params=_params("arbitrary"),
    )(dab, dh, x, g, wup_t)


def _wgrad(lhs, rhs, scale, bm, tk, name, carry=None, part=(0, 1)):
    t, m = lhs.shape
    n = rhs.shape[1] // part[1]
    col = part[0]
    steps = t // tk
    chunk = bm if bm <= 2048 else bm // 2

    def body(l_ref, r_ref, o_ref, acc_ref):
        @pl.when(pl.program_id(1) == 0)
        def _():
            acc_ref[...] = jnp.zeros_like(acc_ref)

        for lo in range(0, bm, chunk):
            acc_ref[lo:lo + chunk, :] += _tn(l_ref[:, lo:lo + chunk], r_ref[...])

        @pl.when(pl.program_id(1) == steps - 1)
        def _():
            o_ref[...] = (scale * acc_ref[...]).astype(o_ref.dtype)

    return _launch(
        body, (lhs, rhs), carry, name=name, grid=(m // bm, steps),
        in_specs=[pl.BlockSpec((tk, bm), lambda i, k: (k, i)), pl.BlockSpec((tk, n), lambda i, k: (k, col))],
        out_specs=[pl.BlockSpec((bm, n), lambda i, k: (i, 0))],
        out_shape=[jax.ShapeDtypeStruct((m, n), WIRE)],
        scratch_shapes=[pltpu.VMEM((bm, n), F32)], semantics=("parallel", "arbitrary"))


def _mix_in_fwd(h, g, win_t, tm):
    t, d = h.shape

    def body(h_ref, g_ref, w_ref, u_ref, q_ref, kv_ref, z_ref, gate_ref):
        xh, _ = _rms_fwd(h_ref[...], g_ref[...])
        u = (xh * g_ref[...]).astype(BF16)
        u_ref[...] = u
        q_ref[...] = _nt(u, w_ref[0:OFF_KV, :]).astype(BF16)
        kv_ref[...] = _nt(u, w_ref[OFF_KV:OFF_Z, :]).astype(BF16)
        z_ref[...] = _nt(u, w_ref[OFF_Z:OFF_GATE, :])
        gate_ref[...] = _nt(u, w_ref[OFF_GATE:IN_WIDTH, :])

    return pl.pallas_call(
        body, name="mix_in_fwd", grid=(t // tm,),
        in_specs=[_rows(tm, d), _resident((1, d)), _resident((IN_WIDTH, d))],
        out_specs=[_rows(tm, d), _rows(tm, ATTN_WIDTH), _rows(tm, 2 * KV_WIDTH), _rows(tm, POOL_WIDTH),
                   _rows(tm, 2 * D_MODEL)],
        out_shape=[jax.ShapeDtypeStruct((t, d), BF16), jax.ShapeDtypeStruct((t, ATTN_WIDTH), BF16),
                   jax.ShapeDtypeStruct((t, 2 * KV_WIDTH), BF16), jax.ShapeDtypeStruct((t, POOL_WIDTH), F32),
                   jax.ShapeDtypeStruct((t, 2 * D_MODEL), F32)],
        compiler_params=_params("parallel"),
    )(h, g, win_t)


ALIBI_SLOPES = tuple(float(s) for s in (2.0 ** (-8.0 * np.arange(1, N_Q_HEADS + 1, dtype=np.float32) / N_Q_HEADS)))


def _attn_dist():
    return jnp.asarray(((np.arange(BLOCK)[:, None] - np.arange(BLOCK)[None, :]) % BLOCK).astype(np.float32))


def _own_block():
    shape = (BLOCK, BLOCK)
    return lax.broadcasted_iota(jnp.int32, shape, 1) <= lax.broadcasted_iota(jnp.int32, shape, 0)


def _fold(band2, own):
    return jnp.where(own, band2[:, BLOCK:], band2[:, :BLOCK])


def _unfold(x, own):
    zero = jnp.zeros_like(x)
    return jnp.concatenate([jnp.where(own, zero, x), jnp.where(own, x, zero)], axis=1)


def _low_half(shape):
    return lax.broadcasted_iota(jnp.int32, shape, len(shape) - 1) < HEAD_DIM


def _both_halves(band, kv_head):
    low = _low_half(band.shape)
    swapped = pltpu.roll(band, HEAD_DIM, 1)
    return jnp.where(low, band, swapped) if kv_head == 0 else jnp.where(low, swapped, band)


def _own_half(ref, head):
    v = ref[:, LANES * (head // 2):LANES * (head // 2 + 1)]
    low = _low_half(v.shape)
    return jnp.where(low if head % 2 == 0 else jnp.logical_not(low), v, jnp.zeros_like(v))


def _head_scores(q_ref, kk, dist, head, first, own):
    s2 = _nt(_own_half(q_ref, head) * HEAD_DIM ** -0.5, kk)
    before = jnp.where(first, -jnp.inf, s2[:, :BLOCK])
    return jnp.where(own, s2[:, BLOCK:], before) - ALIBI_SLOPES[head] * dist


def _softmax_sink(s, sink):
    m = jnp.maximum(jnp.max(s, axis=-1, keepdims=True), sink)
    p = jnp.exp(s - m)
    psink = jnp.exp(sink - m)
    inv = 1.0 / (jnp.sum(p, axis=-1, keepdims=True) + psink)
    return p * inv, psink * inv


def _bands(kvc_ref, kvp_ref):
    kband = jnp.concatenate([kvp_ref[:, 0:LANES], kvc_ref[:, 0:LANES]], axis=0)
    vband = jnp.concatenate([kvp_ref[:, LANES:2 * LANES], kvc_ref[:, LANES:2 * LANES]], axis=0)
    return ([_both_halves(kband, hk) for hk in range(N_KV_HEADS)],
            [_both_halves(vband, hk) for hk in range(N_KV_HEADS)])


SMEM = pl.BlockSpec(memory_space=pltpu.SMEM)
HEADS = range(N_Q_HEADS)


def _attn_fwd(q, kv, dist, sinks):
    t = q.shape[0]

    def body(q_ref, kvc_ref, kvp_ref, dist_ref, sink_ref, o_ref, s_scr, p_scr):
        first = pl.program_id(0) == 0
        own = _own_block()
        dist_v = dist_ref[...]
        kk, vv = _bands(kvc_ref, kvp_ref)
        for head in HEADS:
            s_scr[head] = _head_scores(q_ref, kk[head // Q_PER_KV], dist_v, head, first, own)
        for head in HEADS:
            probs, _ = _softmax_sink(s_scr[head], sink_ref[head])
            p_scr[head] = _unfold(probs.astype(BF16), own)
        for pair in range(N_Q_HEADS // 2):
            even = _nn(p_scr[2 * pair], vv[2 * pair // Q_PER_KV])
            odd = _nn(p_scr[2 * pair + 1], vv[2 * pair // Q_PER_KV])
            o_ref[:, LANES * pair:LANES * (pair + 1)] = jnp.where(_low_half(even.shape), even, odd).astype(BF16)

    return pl.pallas_call(
        body, name="attn_fwd", grid=(t // BLOCK,),
        in_specs=[_rows(BLOCK, ATTN_WIDTH), _rows(BLOCK, 2 * KV_WIDTH),
                  pl.BlockSpec((BLOCK, 2 * KV_WIDTH), lambda i: (jnp.maximum(i - 1, 0), 0)),
                  _resident(dist.shape), SMEM],
        out_specs=_rows(BLOCK, ATTN_WIDTH),
        out_shape=jax.ShapeDtypeStruct((t, ATTN_WIDTH), BF16),
        scratch_shapes=[pltpu.VMEM((N_Q_HEADS, BLOCK, BLOCK), F32), pltpu.VMEM((N_Q_HEADS, BLOCK, 2 * BLOCK), BF16)],
        compiler_params=_params("parallel"),
    )(q, kv, kv, dist, sinks)


def _pool_counts(tm, width):
    row = pl.program_id(0) * tm + lax.broadcasted_iota(jnp.int32, (tm, 1), 0)
    return jnp.minimum(row + 1, width).astype(F32)


def _trailing_sums(zz, group):
    s = zz
    for k in range(group + 1):
        s = s + pltpu.roll(s, 1 << k, 0)
    return s


def _leading_sums(zz, group):
    rows = zz.shape[0]
    s = zz
    for k in range(group + 1):
        s = s + pltpu.roll(s, rows - (1 << k), 0)
    return s


def _mix_out_fwd(attn, z, gate, h, wattn, wmix, scale, wpool_t, wout, tm, carry=None):
    t, d = h.shape

    def body(attn_ref, z_ref, halo_ref, gate_ref, h_ref, wattn_ref, wmix_ref, scale_ref, wpool_ref, wout_ref,
             h2_ref, a_ref, p_ref, merged_ref, ms_ref, pooled_ref):
        halo = jnp.where(pl.program_id(0) == 0, 0.0, halo_ref[...])
        for gi, width in enumerate(POOL_WINDOWS):
            lo, hi = gi * POOL_GROUP, (gi + 1) * POOL_GROUP
            zg = z_ref[:, lo:hi]
            sums = _trailing_sums(jnp.concatenate([halo[:, lo:hi], zg], axis=0), gi)[HALO:, :]
            pooled = (sums / _pool_counts(tm, width) - zg).astype(BF16)
            pooled_ref[:, lo:hi] = pooled
            ms_ref[:, lo:hi] = (_nn(pooled, wmix_ref[gi]) * scale_ref[:, lo:hi]).astype(BF16)
        p = _nt(ms_ref[...], wpool_ref[...])
        a = _nn(attn_ref[...], wattn_ref[...])
        a_ref[...] = a
        p_ref[...] = p
        merged = (jax.nn.sigmoid(gate_ref[:, 0:d]) * a + jax.nn.sigmoid(gate_ref[:, d:2 * d]) * p).astype(BF16)
        merged_ref[...] = merged
        h2_ref[...] = h_ref[...] + _nn(merged, wout_ref[...])

    halo_spec = pl.BlockSpec((HALO, POOL_WIDTH), lambda i: (jnp.maximum(i * (tm // HALO) - 1, 0), 0))
    return _launch(
        body, (attn, z, z, gate, h, wattn, wmix, scale, wpool_t, wout), carry, name="mix_out_fwd", grid=(t // tm,),
        in_specs=[_rows(tm, ATTN_WIDTH), _rows(tm, POOL_WIDTH), halo_spec, _rows(tm, 2 * d), _rows(tm, d),
                  _resident(wattn.shape), _resident(wmix.shape), _resident(scale.shape), _resident(wpool_t.shape),
                  _resident(wout.shape)],
        out_specs=[_rows(tm, d), _rows(tm, d), _rows(tm, d), _rows(tm, d), _rows(tm, POOL_WIDTH),
                   _rows(tm, POOL_WIDTH)],
        out_shape=[jax.ShapeDtypeStruct((t, d), F32), jax.ShapeDtypeStruct((t, d), F32),
                   jax.ShapeDtypeStruct((t, d), F32), jax.ShapeDtypeStruct((t, d), BF16),
                   jax.ShapeDtypeStruct((t, POOL_WIDTH), BF16), jax.ShapeDtypeStruct((t, POOL_WIDTH), BF16)],
        semantics=("parallel",))


def _loss_head(h, g, target, tm):
    t, d = h.shape

    def body(h_ref, g_ref, tgt_ref, loss_ref, dh_ref, dhb_ref, dg_ref):
        xh, r = _rms_fwd(h_ref[...], g_ref[...])
        err = xh * g_ref[...] - tgt_ref[...]
        part = 0.5 * jnp.sum(jnp.mean(err * err, axis=-1, keepdims=True), axis=0, keepdims=True)
        dx, dg = _rms_bwd(err * (1.0 / d), xh, r, g_ref[...])
        dh_ref[...] = dx
        dhb_ref[...] = dx.astype(BF16)

        @pl.when(pl.program_id(0) == 0)
        def _():
            dg_ref[...] = jnp.zeros_like(dg_ref)
            loss_ref[...] = jnp.zeros_like(loss_ref)

        dg_ref[...] += dg
        loss_ref[...] += jnp.broadcast_to(part, loss_ref.shape)

    return pl.pallas_call(
        body, name="loss_head", grid=(t // tm,),
        in_specs=[_rows(tm, d), _resident((1, d)), _rows(tm, d)],
        out_specs=[pl.BlockSpec((1, LANES), lambda i: (0, 0)), _rows(tm, d), _rows(tm, d),
                   pl.BlockSpec((1, d), lambda i: (0, 0))],
        out_shape=[jax.ShapeDtypeStruct((1, LANES), F32), jax.ShapeDtypeStruct((t, d), F32),
                   jax.ShapeDtypeStruct((t, d), BF16), jax.ShapeDtypeStruct((1, d), F32)],
        compiler_params=_params("arbitrary"),
    )(h, g, target)


def _mix_out_bwd(dh, gate, a, p, pooled, wattn, wmix, scale, wpool_t, wout, tm):
    t, d = dh.shape

    def body(dh_ref, gate_ref, a_ref, p_ref, pooled_ref, wattn_ref, wmix_ref, scale_ref, wpool_ref, wout_ref,
             dhb_ref, dab_ref, dpb_ref, dattn_ref, dgate_ref, dpooled_ref, dwmix_ref, dscale_ref):
        @pl.when(pl.program_id(0) == 0)
        def _():
            dwmix_ref[...] = jnp.zeros_like(dwmix_ref)
            dscale_ref[...] = jnp.zeros_like(dscale_ref)

        dhb = dh_ref[...].astype(BF16)
        dhb_ref[...] = dhb
        dm = _nt(dhb, wout_ref[...])
        sa = jax.nn.sigmoid(gate_ref[:, 0:d])
        sp = jax.nn.sigmoid(gate_ref[:, d:2 * d])
        da = (dm * sa).astype(BF16)
        dp = (dm * sp).astype(BF16)
        dab_ref[...] = da
        dpb_ref[...] = dp
        dgate_ref[:, 0:d] = (dm * a_ref[...] * (sa * (1.0 - sa))).astype(BF16)
        dgate_ref[:, d:2 * d] = (dm * p_ref[...] * (sp * (1.0 - sp))).astype(BF16)
        dattn_ref[...] = _nt(da, wattn_ref[...]).astype(BF16)
        dms = _nn(dp, wpool_ref[...])
        for gi in range(len(POOL_WINDOWS)):
            lo, hi = gi * POOL_GROUP, (gi + 1) * POOL_GROUP
            pooled_g = pooled_ref[:, lo:hi]
            mixed = _nn(pooled_g, wmix_ref[gi])
            dscale_ref[:, lo:hi] += jnp.sum(dms[:, lo:hi] * mixed, axis=0, keepdims=True)
            dmixed = (dms[:, lo:hi] * scale_ref[:, lo:hi]).astype(BF16)
            dwmix_ref[gi] += _tn(pooled_g, dmixed)
            dpooled_ref[:, lo:hi] = _nt(dmixed, wmix_ref[gi])

    acc = lambda shape: pl.BlockSpec(shape, lambda i: (0,) * len(shape))
    return pl.pallas_call(
        body, name="mix_out_bwd", grid=(t // tm,),
        in_specs=[_rows(tm, d), _rows(tm, 2 * d), _rows(tm, d), _rows(tm, d), _rows(tm, POOL_WIDTH),
                  _resident(wattn.shape), _resident(wmix.shape), _resident(scale.shape), _resident(wpool_t.shape),
                  _resident(wout.shape)],
        out_specs=[_rows(tm, d), _rows(tm, d), _rows(tm, d), _rows(tm, ATTN_WIDTH), _rows(tm, 2 * d),
                   _rows(tm, POOL_WIDTH), acc(wmix.shape), acc((1, POOL_WIDTH))],
        out_shape=[jax.ShapeDtypeStruct((t, d), BF16), jax.ShapeDtypeStruct((t, d), BF16),
                   jax.ShapeDtypeStruct((t, d), BF16), jax.ShapeDtypeStruct((t, ATTN_WIDTH), BF16),
                   jax.ShapeDtypeStruct((t, 2 * d), BF16), jax.ShapeDtypeStruct((t, POOL_WIDTH), F32),
                   jax.ShapeDtypeStruct(wmix.shape, F32), jax.ShapeDtypeStruct((1, POOL_WIDTH), F32)],
        compiler_params=_params("arbitrary"),
    )(dh, gate, a, p, pooled, wattn, wmix, scale, wpool_t, wout)


def _fold_halves(x):
    return x + pltpu.roll(x, HEAD_DIM, 1)


def _attn_bwd(q, kv, dattn, dist, sinks, carry=None):
    t = q.shape[0]

    def body(q_ref, kvc_ref, kvp_ref, do_ref, dist_ref, sink_ref, dq_ref, dkv_own_ref, dkv_prev_ref, dsink_ref,
             s_scr, dp_scr, p_scr, ds_scr):
        first = pl.program_id(0) == 0

        @pl.when(first)
        def _():
            dsink_ref[...] = jnp.zeros_like(dsink_ref)

        own = _own_block()
        dist_v = dist_ref[...]
        kk, vv = _bands(kvc_ref, kvp_ref)
        lane = lax.broadcasted_iota(jnp.int32, (1, LANES), 1)
        for head in HEADS:
            hk = head // Q_PER_KV
            s_scr[head] = _head_scores(q_ref, kk[hk], dist_v, head, first, own)
            dp_scr[head] = _fold(_nt(_own_half(do_ref, head), vv[hk]), own)
        dsink = jnp.zeros((1, LANES), F32)
        for head in HEADS:
            probs, psink = _softmax_sink(s_scr[head], sink_ref[head])
            dprobs = dp_scr[head]
            rowdot = jnp.sum(probs * dprobs, axis=-1, keepdims=True)
            p_scr[head] = _unfold(probs.astype(BF16), own)
            ds_scr[head] = _unfold((probs * (dprobs - rowdot)).astype(BF16), own)
            dsink = dsink + jnp.where(lane == head, jnp.sum(-psink * rowdot, axis=0, keepdims=True), 0.0)
        dk_heads, dv_heads = [], []
        for hk in range(N_KV_HEADS):
            dk_t = jnp.zeros((LANES, 2 * BLOCK), F32)
            dv_t = jnp.zeros((LANES, 2 * BLOCK), F32)
            for pair in range(Q_PER_KV // 2):
                col = LANES * (hk * (Q_PER_KV // 2) + pair)
                q_t = (q_ref[:, col:col + LANES] * HEAD_DIM ** -0.5).T
                do_t = do_ref[:, col:col + LANES].T
                dqs = []
                for head in (hk * Q_PER_KV + 2 * pair, hk * Q_PER_KV + 2 * pair + 1):
                    mine = (lax.broadcasted_iota(jnp.int32, q_t.shape, 0) < HEAD_DIM) == (head % 2 == 0)
                    dv_t = dv_t + _nn(jnp.where(mine, do_t, jnp.zeros_like(do_t)), p_scr[head])
                    dk_t = dk_t + _nn(jnp.where(mine, q_t, jnp.zeros_like(q_t)), ds_scr[head])
                    dqs.append(_nn(ds_scr[head], kk[hk]))
                dq_pair = jnp.where(_low_half(dqs[0].shape), dqs[0], dqs[1])
                dq_ref[:, col:col + LANES] = (dq_pair * HEAD_DIM ** -0.5).astype(BF16)
            dk_heads.append(_fold_halves(dk_t.T))
            dv_heads.append(_fold_halves(dv_t.T))
        low = _low_half(dk_heads[0].shape)
        dkv = jnp.concatenate([jnp.where(low, dk_heads[0], dk_heads[1]), jnp.where(low, dv_heads[0], dv_heads[1])],
                              axis=1)
        dkv_prev_ref[...] = dkv[0:BLOCK, :]
        dkv_own_ref[...] = dkv[BLOCK:2 * BLOCK, :]
        dsink_ref[...] += dsink

    return _launch(
        body, (q, kv, kv, dattn, dist, sinks), carry, name="attn_bwd", grid=(t // BLOCK,),
        in_specs=[_rows(BLOCK, ATTN_WIDTH), _rows(BLOCK, 2 * KV_WIDTH),
                  pl.BlockSpec((BLOCK, 2 * KV_WIDTH), lambda i: (jnp.maximum(i - 1, 0), 0)),
                  _rows(BLOCK, ATTN_WIDTH), _resident(dist.shape), SMEM],
        out_specs=[_rows(BLOCK, ATTN_WIDTH), _rows(BLOCK, 2 * KV_WIDTH), _rows(BLOCK, 2 * KV_WIDTH),
                   pl.BlockSpec((1, LANES), lambda i: (0, 0))],
        out_shape=[jax.ShapeDtypeStruct((t, ATTN_WIDTH), BF16), jax.ShapeDtypeStruct((t, 2 * KV_WIDTH), F32),
                   jax.ShapeDtypeStruct((t, 2 * KV_WIDTH), F32), jax.ShapeDtypeStruct((1, LANES), F32)],
        scratch_shapes=[pltpu.VMEM((N_Q_HEADS, BLOCK, BLOCK), F32), pltpu.VMEM((N_Q_HEADS, BLOCK, BLOCK), F32),
                        pltpu.VMEM((N_Q_HEADS, BLOCK, 2 * BLOCK), BF16),
                        pltpu.VMEM((N_Q_HEADS, BLOCK, 2 * BLOCK), BF16)],
        semantics=("arbitrary",))


def _mix_in_bwd(dq, dkv_own, dkv_prev, dpooled, dgate, h, g, win_t, dh_res, tm, carry=None):
    t, d = h.shape
    nt = t // tm

    def body(dq_ref, own_ref, prev_ref, prev_next_ref, dpool_ref, halo_ref, dgate_ref, h_ref, g_ref, w_ref, res_ref,
             dproj_ref, dh_ref, dhb_ref, dg_ref):
        i = pl.program_id(0)
        last = i == nt - 1
        dproj_ref[:, 0:OFF_KV] = dq_ref[...]
        from_next = jnp.where(last, 0.0, prev_next_ref[...])
        if tm > BLOCK:
            from_next = jnp.concatenate([prev_ref[BLOCK:tm, :], from_next], axis=0)
        dproj_ref[:, OFF_KV:OFF_Z] = (own_ref[...] + from_next).astype(BF16)
        halo = jnp.where(last, 0.0, halo_ref[...])
        for gi, width in enumerate(POOL_WINDOWS):
            lo, hi = gi * POOL_GROUP, (gi + 1) * POOL_GROUP
            dpg = dpool_ref[:, lo:hi]
            scaled = jnp.concatenate([dpg / _pool_counts(tm, width), halo[:, lo:hi] / float(width)], axis=0)
            dz = _leading_sums(scaled, gi)[0:tm, :] - dpg
            dproj_ref[:, OFF_Z + lo:OFF_Z + hi] = dz.astype(BF16)
        dproj_ref[:, OFF_GATE:IN_WIDTH] = dgate_ref[...]
        du = _nn(dproj_ref[...], w_ref[...])
        xh, r = _rms_fwd(h_ref[...], g_ref[...])
        dx, dg = _rms_bwd(du, xh, r, g_ref[...])
        dh = res_ref[...] + dx
        dh_ref[...] = dh
        dhb_ref[...] = dh.astype(BF16)

        @pl.when(i == 0)
        def _():
            dg_ref[...] = jnp.zeros_like(dg_ref)

        dg_ref[...] += dg

    per = tm // BLOCK
    next_block = pl.BlockSpec((BLOCK, 2 * KV_WIDTH), lambda i: (jnp.minimum((i + 1) * per, t // BLOCK - 1), 0))
    next_halo = pl.BlockSpec((HALO, POOL_WIDTH), lambda i: (jnp.minimum((i + 1) * (tm // HALO), t // HALO - 1), 0))
    return _launch(
        body, (dq, dkv_own, dkv_prev, dkv_prev, dpooled, dpooled, dgate, h, g, win_t, dh_res), carry,
        name="mix_in_bwd", grid=(nt,),
        in_specs=[_rows(tm, ATTN_WIDTH), _rows(tm, 2 * KV_WIDTH), _rows(tm, 2 * KV_WIDTH), next_block,
                  _rows(tm, POOL_WIDTH), next_halo, _rows(tm, 2 * d), _rows(tm, d), _resident((1, d)),
                  _resident((IN_WIDTH, d)), _rows(tm, d)],
        out_specs=[_rows(tm, IN_WIDTH), _rows(tm, d), _rows(tm, d), pl.BlockSpec((1, d), lambda i: (0, 0))],
        out_shape=[jax.ShapeDtypeStruct((t, IN_WIDTH), BF16), jax.ShapeDtypeStruct((t, d), F32),
                   jax.ShapeDtypeStruct((t, d), BF16), jax.ShapeDtypeStruct((1, d), F32)],
        semantics=("arbitrary",))


BIG = (("wup1_t", "ffn1_w_up", True), ("wdown1", "ffn1_w_down", False), ("win_t", "w_in", True),
       ("wattn", "w_attn_up", False), ("wpool_t", "w_pool_up", True), ("wout", "w_out", False),
       ("wup2_t", "ffn2_w_up", True), ("wdown2", "ffn2_w_down", False))
ANY = pl.BlockSpec(memory_space=pl.ANY)
WIRE = BF16


def _place():
    return lax.axis_index("x"), lax.axis_index("y"), lax.axis_index("c")


def _peer(k):
    x, y, c = _place()
    return x ^ (k >> 2), y ^ ((k >> 1) & 1), c ^ (k & 1)


def _index(px, py, pc):
    return 4 * px + 2 * py + pc


def _gather_carry(shards):
    n = len(shards)

    def tools(ins, outs, sems):
        send_sems, recv_sems, local_sems = sems
        x, y, c = _place()
        chips = [(1 - x, y), (x, 1 - y), (1 - x, 1 - y)]

        def rows(w, px, py, pc):
            r = ins[w].shape[0]
            return outs[w].at[pl.ds(_index(px, py, pc) * r, r), :]

        def copy(w, k, block, to, src=None):
            return pltpu.make_async_remote_copy(
                src_ref=rows(w, *block) if src is None else src, dst_ref=rows(w, *block),
                send_sem=send_sems.at[w, k], recv_sem=recv_sems.at[w, k], device_id=to, device_id_type=MESH)

        def own(w):
            return ([pltpu.make_async_copy(ins[w], rows(w, x, y, c), local_sems.at[w]),
                     copy(w, 0, (x, y, c), (x, y, 1 - c), src=ins[w])]
                    + [copy(w, 1 + j, (x, y, c), (*chip, c), src=ins[w]) for j, chip in enumerate(chips)])

        def passed(w, j):
            return copy(w, 4 + j, (*chips[j], c), (x, y, 1 - c))

        return (x, y, c), chips, copy, own, passed

    def start(ins, outs, sems):
        _, _, _, own, _ = tools(ins, outs, sems)
        for w in range(n):
            for cp in own(w):
                cp.start()

    def forward(w):
        def run(ins, outs, sems):
            (x, y, c), chips, copy, _, passed = tools(ins, outs, sems)
            for j, chip in enumerate(chips):
                copy(w, 1 + j, (*chip, c), (x, y, c)).wait_recv()
                passed(w, j).start()
        return run

    sizes = np.cumsum([s.size for s in shards]) / sum(s.size for s in shards)
    middles = [(float(sizes[w]), forward(w)) for w in range(n)]

    def finish(ins, outs, sems):
        (x, y, c), chips, copy, own, passed = tools(ins, outs, sems)
        for w in range(n):
            copy(w, 0, (x, y, 1 - c), (x, y, c)).wait_recv()
            for j, chip in enumerate(chips):
                copy(w, 4 + j, (*chip, 1 - c), (x, y, c)).wait_recv()
        for w in range(n):
            mine, *sent = own(w)
            for cp in sent + [passed(w, j) for j in range(len(chips))]:
                cp.wait_send()
            mine.wait()

    return _Carry(
        shards, [jax.ShapeDtypeStruct((N_DEV * s.shape[0], s.shape[1]), s.dtype) for s in shards],
        [pltpu.SemaphoreType.DMA((n, N_DEV - 1)), pltpu.SemaphoreType.DMA((n, N_DEV - 1)),
         pltpu.SemaphoreType.DMA((n,))], start, finish, middles)


def _scatter_carry(grads):
    n = len(grads)

    def tools(ins, outs, sems):
        send_sems, recv_sems, local_sems = sems
        me = _index(*_place())

        def block(ref, dev):
            r = ref.shape[0] // N_DEV
            return ref.at[pl.ds(dev * r, r), :]

        def copy(w, k, landing):
            to = _peer(k)
            return pltpu.make_async_remote_copy(
                src_ref=block(ins[w], _index(*to)), dst_ref=block(outs[w], landing), send_sem=send_sems.at[w, k - 1],
                recv_sem=recv_sems.at[w, k - 1], device_id=to, device_id_type=MESH)

        def mine(w):
            return pltpu.make_async_copy(block(ins[w], me), block(outs[w], me), local_sems.at[w])

        return me, copy, mine

    def start(ins, outs, sems):
        me, copy, mine = tools(ins, outs, sems)
        for w in range(n):
            mine(w).start()
            for k in range(1, N_DEV):
                copy(w, k, me).start()

    def finish(ins, outs, sems):
        _, copy, mine = tools(ins, outs, sems)
        for w in range(n):
            for k in range(1, N_DEV):
                copy(w, k, _index(*_peer(k))).wait()
            mine(w).wait()

    return _Carry(
        grads, [jax.ShapeDtypeStruct(g.shape, g.dtype) for g in grads],
        [pltpu.SemaphoreType.DMA((n, N_DEV - 1)), pltpu.SemaphoreType.DMA((n, N_DEV - 1)),
         pltpu.SemaphoreType.DMA((n,))], start, finish)


def _small_carry(small):
    srows = small.shape[0]

    def tools(ins, outs, sems):
        send_sems, recv_sems, local_sem = sems
        me = _index(*_place())

        def slot(dev):
            return outs[0].at[pl.ds(dev * srows, srows), :]

        def copy(k, landing):
            return pltpu.make_async_remote_copy(
                src_ref=ins[0], dst_ref=slot(landing), send_sem=send_sems.at[k - 1], recv_sem=recv_sems.at[k - 1],
                device_id=_peer(k), device_id_type=MESH)

        return me, copy, pltpu.make_async_copy(ins[0], slot(me), local_sem)

    def start(ins, outs, sems):
        me, copy, mine = tools(ins, outs, sems)
        mine.start()
        for k in range(1, N_DEV):
            copy(k, me).start()

    def finish(ins, outs, sems):
        _, copy, mine = tools(ins, outs, sems)
        for k in range(1, N_DEV):
            copy(k, _index(*_peer(k))).wait()
        mine.wait()

    return _Carry([small], [jax.ShapeDtypeStruct((N_DEV * srows, LANES), small.dtype)],
                  [pltpu.SemaphoreType.DMA((N_DEV - 1,)), pltpu.SemaphoreType.DMA((N_DEV - 1,)),
                   pltpu.SemaphoreType.DMA], start, finish)


def _exchange(carry, name):
    ci = len(carry.inputs)
    co = len(carry.out_shape)

    def body(*refs):
        parts = refs[:ci], refs[ci:ci + co], refs[ci + co:]
        carry.start(*parts)
        for _, fn in carry.middles:
            fn(*parts)
        carry.finish(*parts)

    return list(pl.pallas_call(body, name=name, in_specs=[ANY] * ci, out_specs=[ANY] * co, out_shape=carry.out_shape,
                               scratch_shapes=carry.scratch)(*carry.inputs))


def _adamw_math(w, g, m, v):
    m = ADAM_B1 * m + (1.0 - ADAM_B1) * g
    v = ADAM_B2 * v + (1.0 - ADAM_B2) * (g * g)
    m_hat = m / (1.0 - ADAM_B1 ** ADAM_STEP)
    v_hat = v / (1.0 - ADAM_B2 ** ADAM_STEP)
    return -ADAM_LR * (m_hat / (jnp.sqrt(v_hat) + ADAM_EPS) + ADAM_WD * w), m, v


def _sum_adamw(got, w, m, v, transposed, name):
    parts = list(got) if isinstance(got, (list, tuple)) else [got]
    r = parts[0].shape[0] // N_DEV
    cols = sum(part.shape[1] for part in parts)
    if transposed:
        (only,) = parts
        tile = cols if cols <= 512 else 256
        got_specs = [pl.BlockSpec((N_DEV, r, tile), lambda i: (0, 0, i))]
        spec, steps = pl.BlockSpec((tile, r), lambda i: (i, 0)), cols // tile
    else:
        tile = r if r <= 256 else r // 2
        got_specs = [pl.BlockSpec((N_DEV, tile, part.shape[1]), lambda i: (0, i, 0)) for part in parts]
        spec, steps = pl.BlockSpec((tile, cols), lambda i: (i, 0)), r // tile
    n = len(parts)

    def body(*refs):
        w_ref, m_ref, v_ref, g_ref, d_ref, m2_ref, v2_ref = refs[n:]
        sums = []
        for got_ref in refs[:n]:
            acc = got_ref[0].astype(F32)
            for dev in range(1, N_DEV):
                acc = acc + got_ref[dev].astype(F32)
            sums.append(acc)
        g = sums[0].T if transposed else (sums[0] if n == 1 else jnp.concatenate(sums, axis=1))
        g_ref[...] = g
        d_ref[...], m2_ref[...], v2_ref[...] = _adamw_math(w_ref[...], g, m_ref[...], v_ref[...])

    return pl.pallas_call(
        body, name=name, grid=(steps,), in_specs=got_specs + [spec, spec, spec], out_specs=[spec] * 4,
        out_shape=[jax.ShapeDtypeStruct(w.shape, F32)] * 4, compiler_params=_params("parallel"),
    )(*[part.reshape(N_DEV, r, part.shape[1]) for part in parts], w, m, v)


def _small_update(gathered, w, m, v):
    rows = w.shape[0]

    def body(all_ref, w_ref, m_ref, v_ref, g_ref, d_ref, m2_ref, v2_ref):
        g = all_ref[0]
        for dev in range(1, N_DEV):
            g = g + all_ref[dev]
        g_ref[...] = g
        d_ref[...], m2_ref[...], v2_ref[...] = _adamw_math(w_ref[...], g, m_ref[...], v_ref[...])

    return pl.pallas_call(
        body, name="small_update", out_shape=[jax.ShapeDtypeStruct((rows, LANES), F32)] * 4,
        compiler_params=pltpu.CompilerParams(vmem_limit_bytes=VMEM_LIMIT),
    )(gathered.reshape(N_DEV, rows, LANES), w, m, v)


SMALL = (("pool_w_mix", 512), ("ffn1_norm", 8), ("mix_norm", 8), ("ffn2_norm", 8), ("final_norm", 8),
         ("pool_scale", 8), ("sinks", 8), ("loss", 8))
SMALL_ROWS = sum(rows for _, rows in SMALL)


def _pack_small(parts):
    out = []
    for name, rows in SMALL:
        flat = parts[name].astype(F32).reshape(-1)
        out.append(jnp.pad(flat, (0, rows * LANES - flat.shape[0])).reshape(rows, LANES))
    return jnp.concatenate(out, axis=0)


def _unpack_small(packed, shapes):
    out, row = {}, 0
    for name, rows in SMALL:
        shape = shapes[name]
        size = int(np.prod(shape)) if shape else 1
        out[name] = packed[row:row + rows].reshape(-1)[:size].reshape(shape)
        row += rows
    return out


def kernel(x, ffn1_norm, ffn1_w_up, ffn1_w_down, mix_norm, w_in, sinks, w_attn_up, pool_w_mix, pool_scale, w_pool_up, w_out, ffn2_norm, ffn2_w_up, ffn2_w_down, final_norm, loss_target, m_ffn1_norm, m_ffn1_w_up, m_ffn1_w_down, m_mix_norm, m_w_in, m_sinks, m_w_attn_up, m_pool_w_mix, m_pool_scale, m_w_pool_up, m_w_out, m_ffn2_norm, m_ffn2_w_up, m_ffn2_w_down, m_final_norm, v_ffn1_norm, v_ffn1_w_up, v_ffn1_w_down, v_mix_norm, v_w_in, v_sinks, v_w_attn_up, v_pool_w_mix, v_pool_scale, v_w_pool_up, v_w_out, v_ffn2_norm, v_ffn2_w_up, v_ffn2_w_down, v_final_norm):
    args = dict(locals())
    weight_names = ("ffn1_norm", "ffn1_w_up", "ffn1_w_down", "mix_norm", "w_in", "sinks", "w_attn_up", "pool_w_mix",
                    "pool_scale", "w_pool_up", "w_out", "ffn2_norm", "ffn2_w_up", "ffn2_w_down", "final_norm")

    shard = {k: (args[p][0].T if tr else args[p][0]).astype(BF16) for k, p, tr in BIG}
    early, late = ["wdown1", "win_t", "wattn", "wpool_t", "wout"], ["wup2_t", "wdown2"]
    big = {"wup1_t": _exchange(_gather_carry([shard["wup1_t"]]), "gather_up1")[0]}

    xs, target = x[0], loss_target[0]
    t = xs.shape[0]
    tm_f, tm_b, tk = min(512, t), min(256, t), min(1024, t)
    g1, gm, g2, gf = ffn1_norm, mix_norm, ffn2_norm, final_norm.reshape(1, D_MODEL)
    dist = _attn_dist()
    sink_v = sinks.reshape(N_Q_HEADS)
    wmix_b = pool_w_mix[0].astype(BF16)

    (n1, ab1, act1), rest = _ffn_up(xs, g1, big["wup1_t"], tm_f, _gather_carry([shard[k] for k in early]))
    big.update(zip(early, rest))
    h1 = _ffn_down(xs, act1, big["wdown1"], tm_f)
    u, q, kv, z, gate = _mix_in_fwd(h1, gm, big["win_t"], tm_f)
    attn = _attn_fwd(q, kv, dist, sink_v)
    (h2, a, p, merged, ms, pooled), rest = _mix_out_fwd(
        attn, z, gate, h1, big["wattn"], wmix_b, pool_scale, big["wpool_t"], big["wout"], tm_b,
        _gather_carry([shard[k] for k in late]))
    big.update(zip(late, rest))
    (h3, ab2, n2, act2), _ = _ffn_fwd(h2, g2, big["wup2_t"], big["wdown2"], tm_f)
    loss_lanes, dh3, dhb3, dgf = _loss_head(h3, gf, target, tm_f)

    got = {}
    (gw_down2,), _ = _wgrad(act2, dhb3, 0.5, D_FF, tk, "wgrad_down2")
    (dab2,), (got["wdown2"],) = _ffn_bwd_hidden(dhb3, ab2, big["wdown2"], tm_f, _scatter_carry([gw_down2]))
    dh2, dg2 = _ffn_bwd_input(dab2, dh3, h2, g2, big["wup2_t"], tm_f)
    (gw_up2,), _ = _wgrad(dab2, n2, 1.0, D_FF, tk, "wgrad_up2")
    dhb2, da_b, dp_b, dattn, dgate, dpooled, dwmix, dscale = _mix_out_bwd(
        dh2, gate, a, p, pooled, big["wattn"], wmix_b, pool_scale, big["wpool_t"], big["wout"], tm_b)
    (gw_out,), _ = _wgrad(merged, dhb2, 1.0, D_MODEL, tk, "wgrad_out")
    (gw_attn,), _ = _wgrad(attn, da_b, 1.0, D_MODEL, tk, "wgrad_attn")
    (gw_pool,), _ = _wgrad(dp_b, ms, 1.0, D_MODEL, tk, "wgrad_pool")
    (dq, dkv_own, dkv_prev, dsinks), (got["wup2_t"],) = _attn_bwd(q, kv, dattn, dist, sink_v, _scatter_carry([gw_up2]))
    (dproj, dh1, dhb1, dgm), (got["wout"], got["wattn"], got["wpool_t"]) = _mix_in_bwd(
        dq, dkv_own, dkv_prev, dpooled, dgate, h1, gm, big["win_t"], dh2, tm_b,
        _scatter_carry([gw_out, gw_attn, gw_pool]))
    (gw_in,), _ = _wgrad(dproj, u, 1.0, IN_WIDTH // 2, tk, "wgrad_in")
    (gw_down1,), _ = _wgrad(act1, dhb1, 0.5, D_FF, tk, "wgrad_down1")
    (dab1,), (got["win_t"],) = _ffn_bwd_hidden(dhb1, ab1, big["wdown1"], tm_f, _scatter_carry([gw_in]))
    dx, dg1 = _ffn_bwd_input(dab1, dh1, xs, g1, big["wup1_t"], tm_f)
    small_parts = {"pool_w_mix": dwmix, "ffn1_norm": dg1, "mix_norm": dgm, "ffn2_norm": dg2, "final_norm": dgf,
                   "pool_scale": dscale, "sinks": dsinks[:, :N_Q_HEADS], "loss": loss_lanes[:, :1]}
    (gw_up1_lo,), (got["wdown1"], small_all) = _wgrad(
        dab1, n1, 1.0, D_FF, tk, "wgrad_up1_lo",
        _merge(_scatter_carry([gw_down1]), _small_carry(_pack_small(small_parts))), part=(0, 2))
    (gw_up1_hi,), (got_lo,) = _wgrad(dab1, n1, 1.0, D_FF, tk, "wgrad_up1_hi", _scatter_carry([gw_up1_lo]), part=(1, 2))
    got["wup1_t"] = [got_lo] + _exchange(_scatter_carry([gw_up1_hi]), "scatter_up1_hi")

    grad, delta, new_m, new_v = {}, {}, {}, {}
    for k, p, tr in BIG:
        outside = tr and args[p].shape[-1] % LANES != 0
        turn = (lambda a: a.T) if outside else (lambda a: a)
        res = _sum_adamw(got[k], turn(args[p][0]), turn(args["m_" + p][0]), turn(args["v_" + p][0]),
                         tr and not outside, "adamw_" + k)
        grad[p], delta[p], new_m[p], new_v[p] = (turn(a)[None] for a in res)

    shapes = {name: args[name].shape for name, _ in SMALL if name != "loss"}
    shapes["loss"] = ()
    packed = {pre: _pack_small({**{name: args[pre + name] for name, _ in SMALL if name != "loss"},
                                "loss": jnp.zeros((), F32)}) for pre in ("", "m_", "v_")}
    g_s, d_s, m_s, v_s = _small_update(small_all, packed[""], packed["m_"], packed["v_"])
    g_small, d_small, m_small, v_small = (_unpack_small(a, shapes) for a in (g_s, d_s, m_s, v_s))
    for name, _ in SMALL:
        if name != "loss":
            grad[name], delta[name], new_m[name], new_v[name] = (
                g_small[name], d_small[name], m_small[name], v_small[name])

    return (g_small["loss"], dx[None], *[grad[n] for n in weight_names], *[delta[n] for n in weight_names],
            *[new_m[n] for n in weight_names], *[new_v[n] for n in weight_names])
```

```python
import functools

import jax
import jax.numpy as jnp
import numpy as np
from jax import lax
from jax.experimental import pallas as pl
from jax.experimental.pallas import tpu as pltpu

F32 = jnp.float32
BF16 = jnp.bfloat16

D_MODEL = 1024
D_FF = 2816
N_Q_HEADS = 16
N_KV_HEADS = 2
Q_PER_KV = N_Q_HEADS // N_KV_HEADS
HEAD_DIM = 64
BLOCK = 128
ATTN_WIDTH = N_Q_HEADS * HEAD_DIM
KV_WIDTH = N_KV_HEADS * HEAD_DIM
POOL_WINDOWS = (2, 4, 8, 16)
POOL_GROUP = 128
POOL_WIDTH = 512
HALO = 16
IN_WIDTH = ATTN_WIDTH + 2 * KV_WIDTH + POOL_WIDTH + 2 * D_MODEL
OFF_KV = ATTN_WIDTH
OFF_Z = ATTN_WIDTH + 2 * KV_WIDTH
OFF_GATE = OFF_Z + POOL_WIDTH
NORM_EPS = 1e-6
ADAM_LR = 0.001
ADAM_B1 = 0.9
ADAM_B2 = 0.999
ADAM_EPS = 1e-08
ADAM_WD = 0.01
ADAM_STEP = 10

N_DEV = 8
N_CHIP = 4
LANES = 128
FF_CHUNK = 256
SLAB = 32
VMEM_LIMIT = 56 * 1024 * 1024
MESH = pl.DeviceIdType.MESH


def _nn(a, b):
    return jnp.dot(a, b, preferred_element_type=F32)


def _nt(a, b):
    return lax.dot_general(a, b, (((1,), (1,)), ((), ())), preferred_element_type=F32)


def _tn(a, b):
    return lax.dot_general(a, b, (((0,), (0,)), ((), ())), preferred_element_type=F32)


def _params(*sem):
    return pltpu.CompilerParams(dimension_semantics=sem, vmem_limit_bytes=VMEM_LIMIT)


def _resident(shape):
    return pl.BlockSpec(shape, lambda *_: (0,) * len(shape), pipeline_mode=pl.Buffered(1))


def _rows(tm, cols):
    return pl.BlockSpec((tm, cols), lambda i: (i, 0))


class _Carry:
    def __init__(self, inputs, out_shape, scratch, start, finish, middles=()):
        self.inputs, self.out_shape, self.scratch = list(inputs), list(out_shape), list(scratch)
        self.start, self.finish, self.middles = start, finish, list(middles)


def _launch(body, args, carry=None, *, name, grid, in_specs, out_specs, out_shape, scratch_shapes=(), semantics):
    in_specs, out_specs, out_shape, scratch_shapes = list(in_specs), list(out_specs), list(out_shape), list(scratch_shapes)
    if carry is None:
        res = pl.pallas_call(body, name=name, grid=grid, in_specs=in_specs, out_specs=out_specs, out_shape=out_shape,
                             scratch_shapes=scratch_shapes, compiler_params=_params(*semantics))(*args)
        return list(res), []
    ni, no, ns = len(in_specs), len(out_specs), len(scratch_shapes)
    ci, co = len(carry.inputs), len(carry.out_shape)
    total = int(np.prod(grid))

    def full(*refs):
        own_in, c_in = refs[:ni], refs[ni:ni + ci]
        own_out, c_out = refs[ni + ci:ni + ci + no], refs[ni + ci + no:ni + ci + no + co]
        own_scr, c_sem = refs[ni + ci + no + co:ni + ci + no + co + ns], refs[ni + ci + no + co + ns:]
        step = 0
        for axis, size in enumerate(grid):
            step = step * size + pl.program_id(axis)
        pl.when(step == 0)(lambda: carry.start(c_in, c_out, c_sem))
        for fraction, fn in carry.middles:
            at = min(total - 1, int(fraction * total) + 1)
            pl.when(step == at)(lambda fn=fn: fn(c_in, c_out, c_sem))
        body(*own_in, *own_out, *own_scr)
        pl.when(step == total - 1)(lambda: carry.finish(c_in, c_out, c_sem))

    res = pl.pallas_call(
        full, name=name, grid=grid, in_specs=in_specs + [ANY] * ci, out_specs=out_specs + [ANY] * co,
        out_shape=out_shape + carry.out_shape, scratch_shapes=scratch_shapes + carry.scratch,
        compiler_params=_params(*(["arbitrary"] * len(grid))),
    )(*args, *carry.inputs)
    return list(res[:no]), list(res[no:])


def _rms_fwd(xv, g):
    r = lax.rsqrt(jnp.mean(xv * xv, axis=-1, keepdims=True) + NORM_EPS)
    return xv * r, r


def _rms_bwd(dn, xh, r, g):
    dxh = dn * g
    dx = r * (dxh - xh * jnp.mean(dxh * xh, axis=-1, keepdims=True))
    return dx, jnp.sum(dn * xh, axis=0, keepdims=True)


def _ffn_fwd(x, g, wup_t, wdown, tm, carry=None):
    t, d = x.shape
    f = wdown.shape[0]

    def body(x_ref, g_ref, wup_ref, wdn_ref, h_ref, ab_ref, n_ref, act_ref):
        xv = x_ref[...]
        xh, _ = _rms_fwd(xv, g_ref[...])
        n = (xh * g_ref[...]).astype(BF16)
        n_ref[...] = n
        for c in range(f // FF_CHUNK):
            lo, hi = c * FF_CHUNK, (c + 1) * FF_CHUNK
            a = _nt(n, wup_ref[lo:hi, :])
            b = _nt(n, wup_ref[f + lo:f + hi, :])
            ab_ref[:, lo:hi] = a.astype(BF16)
            ab_ref[:, f + lo:f + hi] = b.astype(BF16)
            act_ref[:, lo:hi] = (a * jax.nn.sigmoid(a) * b).astype(BF16)
        h_ref[...] = xv + 0.5 * _nn(act_ref[...], wdn_ref[...])

    return _launch(
        body, (x, g, wup_t, wdown), carry, name="ffn_fwd", grid=(t // tm,),
        in_specs=[_rows(tm, d), _resident((1, d)), _resident((2 * f, d)), _resident((f, d))],
        out_specs=[_rows(tm, d), _rows(tm, 2 * f), _rows(tm, d), _rows(tm, f)],
        out_shape=[jax.ShapeDtypeStruct((t, d), F32), jax.ShapeDtypeStruct((t, 2 * f), BF16),
                   jax.ShapeDtypeStruct((t, d), BF16), jax.ShapeDtypeStruct((t, f), BF16)],
        semantics=("parallel",))


def _ffn_up(x, g, wup_t, tm, carry=None):
    t, d = x.shape
    f = wup_t.shape[0] // 2

    def body(x_ref, g_ref, wup_ref, n_ref, ab_ref, act_ref):
        xh, _ = _rms_fwd(x_ref[...], g_ref[...])
        n = (xh * g_ref[...]).astype(BF16)
        n_ref[...] = n
        for c in range(f // FF_CHUNK):
            lo, hi = c * FF_CHUNK, (c + 1) * FF_CHUNK
            a = _nt(n, wup_ref[lo:hi, :])
            b = _nt(n, wup_ref[f + lo:f + hi, :])
            ab_ref[:, lo:hi] = a.astype(BF16)
            ab_ref[:, f + lo:f + hi] = b.astype(BF16)
            act_ref[:, lo:hi] = (a * jax.nn.sigmoid(a) * b).astype(BF16)

    return _launch(
        body, (x, g, wup_t), carry, name="ffn_up", grid=(t // tm,),
        in_specs=[_rows(tm, d), _resident((1, d)), _resident((2 * f, d))],
        out_specs=[_rows(tm, d), _rows(tm, 2 * f), _rows(tm, f)],
        out_shape=[jax.ShapeDtypeStruct((t, d), BF16), jax.ShapeDtypeStruct((t, 2 * f), BF16),
                   jax.ShapeDtypeStruct((t, f), BF16)],
        semantics=("parallel",))


def _ffn_down(x, act, wdown, tm):
    t, d = x.shape
    f = wdown.shape[0]

    def body(x_ref, act_ref, wdn_ref, h_ref):
        h_ref[...] = x_ref[...] + 0.5 * _nn(act_ref[...], wdn_ref[...])

    return pl.pallas_call(
        body, name="ffn_down", grid=(t // tm,),
        in_specs=[_rows(tm, d), _rows(tm, f), _resident((f, d))], out_specs=_rows(tm, d),
        out_shape=jax.ShapeDtypeStruct((t, d), F32), compiler_params=_params("parallel"),
    )(x, act, wdown)


def _ffn_bwd_hidden(dhb, ab, wdown, tm, carry=None, turned=False):
    t, d = dhb.shape
    f = wdown.shape[0]

    def body(dh_ref, ab_ref, wdn_ref, dab_ref, *rest):
        dact_ref = rest[-1]
        half = dh_ref[...] * 0.5
        for c in range(f // FF_CHUNK):
            lo, hi = c * FF_CHUNK, (c + 1) * FF_CHUNK
            dact_ref[...] = _nt(half, wdn_ref[lo:hi, :])

            def slab(i, carry_):
                rows = pl.ds(pl.multiple_of(i * SLAB, SLAB), SLAB)
                a = ab_ref[rows, lo:hi].astype(F32)
                b = ab_ref[rows, f + lo:f + hi].astype(F32)
                s = jax.nn.sigmoid(a)
                ds_ = dact_ref[rows, :] * s
                dab_ref[rows, lo:hi] = (ds_ * b * (1.0 + a * (1.0 - s))).astype(BF16)
                dab_ref[rows, f + lo:f + hi] = (ds_ * a).astype(BF16)
                return carry_

            lax.fori_loop(0, tm // SLAB, slab, 0, unroll=True)
            if turned:
                rest[0][lo:hi, :] = dab_ref[:, lo:hi].T
                rest[0][f + lo:f + hi, :] = dab_ref[:, f + lo:f + hi].T

    extra_specs = [pl.BlockSpec((2 * f, tm), lambda i: (0, i))] if turned else []
    extra_shapes = [jax.ShapeDtypeStruct((2 * f, t), BF16)] if turned else []
    return _launch(
        body, (dhb, ab, wdown), carry, name="ffn_bwd_hidden", grid=(t // tm,),
        in_specs=[_rows(tm, d), _rows(tm, 2 * f), _resident((f, d))], out_specs=[_rows(tm, 2 * f)] + extra_specs,
        out_shape=[jax.ShapeDtypeStruct((t, 2 * f), BF16)] + extra_shapes,
        scratch_shapes=[pltpu.VMEM((tm, FF_CHUNK), F32)], semantics=("parallel",))


def _ffn_bwd_input(dab, dh, x, g, wup_t, tm):
    t, d = x.shape
    f2 = wup_t.shape[0]

    def body(dab_ref, dh_ref, x_ref, g_ref, wup_ref, dx_ref, dg_ref):
        dn = _nn(dab_ref[...], wup_ref[...])
        xh, r = _rms_fwd(x_ref[...], g_ref[...])
        dx, dg = _rms_bwd(dn, xh, r, g_ref[...])
        dx_ref[...] = dh_ref[...] + dx

        @pl.when(pl.program_id(0) == 0)
        def _():
            dg_ref[...] = jnp.zeros_like(dg_ref)

        dg_ref[...] += dg

    return pl.pallas_call(
        body, name="ffn_bwd_input", grid=(t // tm,),
        in_specs=[_rows(tm, f2), _rows(tm, d), _rows(tm, d), _resident((1, d)), _resident((f2, d))],
        out_specs=[_rows(tm, d), pl.BlockSpec((1, d), lambda i: (0, 0))],
        out_shape=[jax.ShapeDtypeStruct((t, d), F32), jax.ShapeDtypeStruct((1, d), F32)],
        compiler_params=_params("arbitrary"),
    )(dab, dh, x, g, wup_t)


def _wgrad(lhs, rhs, scale, bm, tk, name, carry=None, part=(0, 1), turned=False):
    t, m = lhs.shape[::-1] if turned else lhs.shape
    n = rhs.shape[1] // part[1]
    col = part[0]
    steps = t // tk
    chunk = bm if bm <= 2048 else bm // 2

    def body(l_ref, r_ref, o_ref, acc_ref):
        @pl.when(pl.program_id(1) == 0)
        def _():
            acc_ref[...] = jnp.zeros_like(acc_ref)

        for lo in range(0, bm, chunk):
            if turned:
                acc_ref[lo:lo + chunk, :] += _nn(l_ref[lo:lo + chunk, :], r_ref[...])
            else:
                acc_ref[lo:lo + chunk, :] += _tn(l_ref[:, lo:lo + chunk], r_ref[...])

        @pl.when(pl.program_id(1) == steps - 1)
        def _():
            o_ref[...] = (scale * acc_ref[...]).astype(o_ref.dtype)

    return _launch(
        body, (lhs, rhs), carry, name=name, grid=(m // bm, steps),
        in_specs=[pl.BlockSpec((bm, tk), lambda i, k: (i, k)) if turned else pl.BlockSpec((tk, bm), lambda i, k: (k, i)),
                  pl.BlockSpec((tk, n), lambda i, k: (k, col))],
        out_specs=[pl.BlockSpec((bm, n), lambda i, k: (i, 0))],
        out_shape=[jax.ShapeDtypeStruct((m, n), WIRE)],
        scratch_shapes=[pltpu.VMEM((bm, n), F32)], semantics=("parallel", "arbitrary"))


def _mix_in_fwd(h, g, win_t, tm):
    t, d = h.shape

    def body(h_ref, g_ref, w_ref, u_ref, q_ref, kv_ref, z_ref, gate_ref):
        xh, _ = _rms_fwd(h_ref[...], g_ref[...])
        u = (xh * g_ref[...]).astype(BF16)
        u_ref[...] = u
        q_ref[...] = _nt(u, w_ref[0:OFF_KV, :]).astype(BF16)
        kv_ref[...] = _nt(u, w_ref[OFF_KV:OFF_Z, :]).astype(BF16)
        z_ref[...] = _nt(u, w_ref[OFF_Z:OFF_GATE, :])
        gate_ref[...] = _nt(u, w_ref[OFF_GATE:IN_WIDTH, :])

    return pl.pallas_call(
        body, name="mix_in_fwd", grid=(t // tm,),
        in_specs=[_rows(tm, d), _resident((1, d)), _resident((IN_WIDTH, d))],
        out_specs=[_rows(tm, d), _rows(tm, ATTN_WIDTH), _rows(tm, 2 * KV_WIDTH), _rows(tm, POOL_WIDTH),
                   _rows(tm, 2 * D_MODEL)],
        out_shape=[jax.ShapeDtypeStruct((t, d), BF16), jax.ShapeDtypeStruct((t, ATTN_WIDTH), BF16),
                   jax.ShapeDtypeStruct((t, 2 * KV_WIDTH), BF16), jax.ShapeDtypeStruct((t, POOL_WIDTH), F32),
                   jax.ShapeDtypeStruct((t, 2 * D_MODEL), F32)],
        compiler_params=_params("parallel"),
    )(h, g, win_t)


ALIBI_SLOPES = tuple(float(s) for s in (2.0 ** (-8.0 * np.arange(1, N_Q_HEADS + 1, dtype=np.float32) / N_Q_HEADS)))


def _attn_dist():
    return jnp.asarray(((np.arange(BLOCK)[:, None] - np.arange(BLOCK)[None, :]) % BLOCK).astype(np.float32))


def _own_block():
    shape = (BLOCK, BLOCK)
    return lax.broadcasted_iota(jnp.int32, shape, 1) <= lax.broadcasted_iota(jnp.int32, shape, 0)


def _fold(band2, own):
    return jnp.where(own, band2[:, BLOCK:], band2[:, :BLOCK])


def _unfold(x, own):
    zero = jnp.zeros_like(x)
    return jnp.concatenate([jnp.where(own, zero, x), jnp.where(own, x, zero)], axis=1)


def _low_half(shape):
    return lax.broadcasted_iota(jnp.int32, shape, len(shape) - 1) < HEAD_DIM


def _both_halves(band, kv_head):
    low = _low_half(band.shape)
    swapped = pltpu.roll(band, HEAD_DIM, 1)
    return jnp.where(low, band, swapped) if kv_head == 0 else jnp.where(low, swapped, band)


def _own_half(ref, head):
    v = ref[:, LANES * (head // 2):LANES * (head // 2 + 1)]
    low = _low_half(v.shape)
    return jnp.where(low if head % 2 == 0 else jnp.logical_not(low), v, jnp.zeros_like(v))


def _head_scores(q_ref, kk, dist, head, first, own):
    s2 = _nt(_own_half(q_ref, head) * HEAD_DIM ** -0.5, kk)
    before = jnp.where(first, -jnp.inf, s2[:, :BLOCK])
    return jnp.where(own, s2[:, BLOCK:], before) - ALIBI_SLOPES[head] * dist


def _softmax_sink(s, sink):
    m = jnp.maximum(jnp.max(s, axis=-1, keepdims=True), sink)
    p = jnp.exp(s - m)
    psink = jnp.exp(sink - m)
    inv = 1.0 / (jnp.sum(p, axis=-1, keepdims=True) + psink)
    return p * inv, psink * inv


def _bands(kvc_ref, kvp_ref):
    kband = jnp.concatenate([kvp_ref[:, 0:LANES], kvc_ref[:, 0:LANES]], axis=0)
    vband = jnp.concatenate([kvp_ref[:, LANES:2 * LANES], kvc_ref[:, LANES:2 * LANES]], axis=0)
    return ([_both_halves(kband, hk) for hk in range(N_KV_HEADS)],
            [_both_halves(vband, hk) for hk in range(N_KV_HEADS)])


SMEM = pl.BlockSpec(memory_space=pltpu.SMEM)
HEADS = range(N_Q_HEADS)


def _attn_fwd(q, kv, dist, sinks):
    t = q.shape[0]

    def body(q_ref, kvc_ref, kvp_ref, dist_ref, sink_ref, o_ref, s_scr, p_scr):
        first = pl.program_id(0) == 0
        own = _own_block()
        dist_v = dist_ref[...]
        kk, vv = _bands(kvc_ref, kvp_ref)
        for head in HEADS:
            s_scr[head] = _head_scores(q_ref, kk[head // Q_PER_KV], dist_v, head, first, own)
        for head in HEADS:
            probs, _ = _softmax_sink(s_scr[head], sink_ref[head])
            p_scr[head] = _unfold(probs.astype(BF16), own)
        for pair in range(N_Q_HEADS // 2):
            even = _nn(p_scr[2 * pair], vv[2 * pair // Q_PER_KV])
            odd = _nn(p_scr[2 * pair + 1], vv[2 * pair // Q_PER_KV])
            o_ref[:, LANES * pair:LANES * (pair + 1)] = jnp.where(_low_half(even.shape), even, odd).astype(BF16)

    return pl.pallas_call(
        body, name="attn_fwd", grid=(t // BLOCK,),
        in_specs=[_rows(BLOCK, ATTN_WIDTH), _rows(BLOCK, 2 * KV_WIDTH),
                  pl.BlockSpec((BLOCK, 2 * KV_WIDTH), lambda i: (jnp.maximum(i - 1, 0), 0)),
                  _resident(dist.shape), SMEM],
        out_specs=_rows(BLOCK, ATTN_WIDTH),
        out_shape=jax.ShapeDtypeStruct((t, ATTN_WIDTH), BF16),
        scratch_shapes=[pltpu.VMEM((N_Q_HEADS, BLOCK, BLOCK), F32), pltpu.VMEM((N_Q_HEADS, BLOCK, 2 * BLOCK), BF16)],
        compiler_params=_params("parallel"),
    )(q, kv, kv, dist, sinks)


def _pool_counts(tm, width):
    row = pl.program_id(0) * tm + lax.broadcasted_iota(jnp.int32, (tm, 1), 0)
    return jnp.minimum(row + 1, width).astype(F32)


def _trailing_sums(zz, group):
    s = zz
    for k in range(group + 1):
        s = s + pltpu.roll(s, 1 << k, 0)
    return s


def _leading_sums(zz, group):
    rows = zz.shape[0]
    s = zz
    for k in range(group + 1):
        s = s + pltpu.roll(s, rows - (1 << k), 0)
    return s


def _mix_out_fwd(attn, z, gate, h, wattn, wmix, scale, wpool_t, wout, tm, carry=None):
    t, d = h.shape

    def body(attn_ref, z_ref, halo_ref, gate_ref, h_ref, wattn_ref, wmix_ref, scale_ref, wpool_ref, wout_ref,
             h2_ref, a_ref, p_ref, merged_ref, ms_ref, pooled_ref):
        halo = jnp.where(pl.program_id(0) == 0, 0.0, halo_ref[...])
        for gi, width in enumerate(POOL_WINDOWS):
            lo, hi = gi * POOL_GROUP, (gi + 1) * POOL_GROUP
            zg = z_ref[:, lo:hi]
            sums = _trailing_sums(jnp.concatenate([halo[:, lo:hi], zg], axis=0), gi)[HALO:, :]
            pooled = (sums / _pool_counts(tm, width) - zg).astype(BF16)
            pooled_ref[:, lo:hi] = pooled
            ms_ref[:, lo:hi] = (_nn(pooled, wmix_ref[gi]) * scale_ref[:, lo:hi]).astype(BF16)
        p = _nt(ms_ref[...], wpool_ref[...])
        a = _nn(attn_ref[...], wattn_ref[...])
        a_ref[...] = a
        p_ref[...] = p
        merged = (jax.nn.sigmoid(gate_ref[:, 0:d]) * a + jax.nn.sigmoid(gate_ref[:, d:2 * d]) * p).astype(BF16)
        merged_ref[...] = merged
        h2_ref[...] = h_ref[...] + _nn(merged, wout_ref[...])

    halo_spec = pl.BlockSpec((HALO, POOL_WIDTH), lambda i: (jnp.maximum(i * (tm // HALO) - 1, 0), 0))
    return _launch(
        body, (attn, z, z, gate, h, wattn, wmix, scale, wpool_t, wout), carry, name="mix_out_fwd", grid=(t // tm,),
        in_specs=[_rows(tm, ATTN_WIDTH), _rows(tm, POOL_WIDTH), halo_spec, _rows(tm, 2 * d), _rows(tm, d),
                  _resident(wattn.shape), _resident(wmix.shape), _resident(scale.shape), _resident(wpool_t.shape),
                  _resident(wout.shape)],
        out_specs=[_rows(tm, d), _rows(tm, d), _rows(tm, d), _rows(tm, d), _rows(tm, POOL_WIDTH),
                   _rows(tm, POOL_WIDTH)],
        out_shape=[jax.ShapeDtypeStruct((t, d), F32), jax.ShapeDtypeStruct((t, d), F32),
                   jax.ShapeDtypeStruct((t, d), F32), jax.ShapeDtypeStruct((t, d), BF16),
                   jax.ShapeDtypeStruct((t, POOL_WIDTH), BF16), jax.ShapeDtypeStruct((t, POOL_WIDTH), BF16)],
        semantics=("parallel",))


def _loss_head(h, g, target, tm):
    t, d = h.shape

    def body(h_ref, g_ref, tgt_ref, loss_ref, dh_ref, dhb_ref, dg_ref):
        xh, r = _rms_fwd(h_ref[...], g_ref[...])
        err = xh * g_ref[...] - tgt_ref[...]
        part = 0.5 * jnp.sum(jnp.mean(err * err, axis=-1, keepdims=True), axis=0, keepdims=True)
        dx, dg = _rms_bwd(err * (1.0 / d), xh, r, g_ref[...])
        dh_ref[...] = dx
        dhb_ref[...] = dx.astype(BF16)

        @pl.when(pl.program_id(0) == 0)
        def _():
            dg_ref[...] = jnp.zeros_like(dg_ref)
            loss_ref[...] = jnp.zeros_like(loss_ref)

        dg_ref[...] += dg
        loss_ref[...] += jnp.broadcast_to(part, loss_ref.shape)

    return pl.pallas_call(
        body, name="loss_head", grid=(t // tm,),
        in_specs=[_rows(tm, d), _resident((1, d)), _rows(tm, d)],
        out_specs=[pl.BlockSpec((1, LANES), lambda i: (0, 0)), _rows(tm, d), _rows(tm, d),
                   pl.BlockSpec((1, d), lambda i: (0, 0))],
        out_shape=[jax.ShapeDtypeStruct((1, LANES), F32), jax.ShapeDtypeStruct((t, d), F32),
                   jax.ShapeDtypeStruct((t, d), BF16), jax.ShapeDtypeStruct((1, d), F32)],
        compiler_params=_params("arbitrary"),
    )(h, g, target)


def _mix_out_bwd(dh, gate, a, p, pooled, wattn, wmix, scale, wpool_t, wout, tm):
    t, d = dh.shape

    def body(dh_ref, gate_ref, a_ref, p_ref, pooled_ref, wattn_ref, wmix_ref, scale_ref, wpool_ref, wout_ref,
             dhb_ref, dab_ref, dpb_ref, dattn_ref, dgate_ref, dpooled_ref, dwmix_ref, dscale_ref):
        @pl.when(pl.program_id(0) == 0)
        def _():
            dwmix_ref[...] = jnp.zeros_like(dwmix_ref)
            dscale_ref[...] = jnp.zeros_like(dscale_ref)

        dhb = dh_ref[...].astype(BF16)
        dhb_ref[...] = dhb
        dm = _nt(dhb, wout_ref[...])
        sa = jax.nn.sigmoid(gate_ref[:, 0:d])
        sp = jax.nn.sigmoid(gate_ref[:, d:2 * d])
        da = (dm * sa).astype(BF16)
        dp = (dm * sp).astype(BF16)
        dab_ref[...] = da
        dpb_ref[...] = dp
        dgate_ref[:, 0:d] = (dm * a_ref[...] * (sa * (1.0 - sa))).astype(BF16)
        dgate_ref[:, d:2 * d] = (dm * p_ref[...] * (sp * (1.0 - sp))).astype(BF16)
        dattn_ref[...] = _nt(da, wattn_ref[...]).astype(BF16)
        dms = _nn(dp, wpool_ref[...])
        for gi in range(len(POOL_WINDOWS)):
            lo, hi = gi * POOL_GROUP, (gi + 1) * POOL_GROUP
            pooled_g = pooled_ref[:, lo:hi]
            mixed = _nn(pooled_g, wmix_ref[gi])
            dscale_ref[:, lo:hi] += jnp.sum(dms[:, lo:hi] * mixed, axis=0, keepdims=True)
            dmixed = (dms[:, lo:hi] * scale_ref[:, lo:hi]).astype(BF16)
            dwmix_ref[gi] += _tn(pooled_g, dmixed)
            dpooled_ref[:, lo:hi] = _nt(dmixed, wmix_ref[gi])

    acc = lambda shape: pl.BlockSpec(shape, lambda i: (0,) * len(shape))
    return pl.pallas_call(
        body, name="mix_out_bwd", grid=(t // tm,),
        in_specs=[_rows(tm, d), _rows(tm, 2 * d), _rows(tm, d), _rows(tm, d), _rows(tm, POOL_WIDTH),
                  _resident(wattn.shape), _resident(wmix.shape), _resident(scale.shape), _resident(wpool_t.shape),
                  _resident(wout.shape)],
        out_specs=[_rows(tm, d), _rows(tm, d), _rows(tm, d), _rows(tm, ATTN_WIDTH), _rows(tm, 2 * d),
                   _rows(tm, POOL_WIDTH), acc(wmix.shape), acc((1, POOL_WIDTH))],
        out_shape=[jax.ShapeDtypeStruct((t, d), BF16), jax.ShapeDtypeStruct((t, d), BF16),
                   jax.ShapeDtypeStruct((t, d), BF16), jax.ShapeDtypeStruct((t, ATTN_WIDTH), BF16),
                   jax.ShapeDtypeStruct((t, 2 * d), BF16), jax.ShapeDtypeStruct((t, POOL_WIDTH), F32),
                   jax.ShapeDtypeStruct(wmix.shape, F32), jax.ShapeDtypeStruct((1, POOL_WIDTH), F32)],
        compiler_params=_params("arbitrary"),
    )(dh, gate, a, p, pooled, wattn, wmix, scale, wpool_t, wout)


def _fold_halves(x):
    return x + pltpu.roll(x, HEAD_DIM, 1)


def _attn_bwd(q, kv, dattn, dist, sinks, carry=None):
    t = q.shape[0]

    def body(q_ref, kvc_ref, kvp_ref, do_ref, dist_ref, sink_ref, dq_ref, dkv_own_ref, dkv_prev_ref, dsink_ref,
             s_scr, dp_scr, p_scr, ds_scr):
        first = pl.program_id(0) == 0

        @pl.when(first)
        def _():
            dsink_ref[...] = jnp.zeros_like(dsink_ref)

        own = _own_block()
        dist_v = dist_ref[...]
        kk, vv = _bands(kvc_ref, kvp_ref)
        lane = lax.broadcasted_iota(jnp.int32, (1, LANES), 1)
        for head in HEADS:
            hk = head // Q_PER_KV
            s_scr[head] = _head_scores(q_ref, kk[hk], dist_v, head, first, own)
            dp_scr[head] = _fold(_nt(_own_half(do_ref, head), vv[hk]), own)
        dsink = jnp.zeros((1, LANES), F32)
        for head in HEADS:
            probs, psink = _softmax_sink(s_scr[head], sink_ref[head])
            dprobs = dp_scr[head]
            rowdot = jnp.sum(probs * dprobs, axis=-1, keepdims=True)
            p_scr[head] = _unfold(probs.astype(BF16), own)
            ds_scr[head] = _unfold((probs * (dprobs - rowdot)).astype(BF16), own)
            dsink = dsink + jnp.where(lane == head, jnp.sum(-psink * rowdot, axis=0, keepdims=True), 0.0)
        dk_heads, dv_heads = [], []
        for hk in range(N_KV_HEADS):
            dk_t = jnp.zeros((LANES, 2 * BLOCK), F32)
            dv_t = jnp.zeros((LANES, 2 * BLOCK), F32)
            for pair in range(Q_PER_KV // 2):
                col = LANES * (hk * (Q_PER_KV // 2) + pair)
                q_t = (q_ref[:, col:col + LANES] * HEAD_DIM ** -0.5).T
                do_t = do_ref[:, col:col + LANES].T
                dqs = []
                for head in (hk * Q_PER_KV + 2 * pair, hk * Q_PER_KV + 2 * pair + 1):
                    mine = (lax.broadcasted_iota(jnp.int32, q_t.shape, 0) < HEAD_DIM) == (head % 2 == 0)
                    dv_t = dv_t + _nn(jnp.where(mine, do_t, jnp.zeros_like(do_t)), p_scr[head])
                    dk_t = dk_t + _nn(jnp.where(mine, q_t, jnp.zeros_like(q_t)), ds_scr[head])
                    dqs.append(_nn(ds_scr[head], kk[hk]))
                dq_pair = jnp.where(_low_half(dqs[0].shape), dqs[0], dqs[1])
                dq_ref[:, col:col + LANES] = (dq_pair * HEAD_DIM ** -0.5).astype(BF16)
            dk_heads.append(_fold_halves(dk_t.T))
            dv_heads.append(_fold_halves(dv_t.T))
        low = _low_half(dk_heads[0].shape)
        dkv = jnp.concatenate([jnp.where(low, dk_heads[0], dk_heads[1]), jnp.where(low, dv_heads[0], dv_heads[1])],
                              axis=1)
        dkv_prev_ref[...] = dkv[0:BLOCK, :]
        dkv_own_ref[...] = dkv[BLOCK:2 * BLOCK, :]
        dsink_ref[...] += dsink

    return _launch(
        body, (q, kv, kv, dattn, dist, sinks), carry, name="attn_bwd", grid=(t // BLOCK,),
        in_specs=[_rows(BLOCK, ATTN_WIDTH), _rows(BLOCK, 2 * KV_WIDTH),
                  pl.BlockSpec((BLOCK, 2 * KV_WIDTH), lambda i: (jnp.maximum(i - 1, 0), 0)),
                  _rows(BLOCK, ATTN_WIDTH), _resident(dist.shape), SMEM],
        out_specs=[_rows(BLOCK, ATTN_WIDTH), _rows(BLOCK, 2 * KV_WIDTH), _rows(BLOCK, 2 * KV_WIDTH),
                   pl.BlockSpec((1, LANES), lambda i: (0, 0))],
        out_shape=[jax.ShapeDtypeStruct((t, ATTN_WIDTH), BF16), jax.ShapeDtypeStruct((t, 2 * KV_WIDTH), F32),
                   jax.ShapeDtypeStruct((t, 2 * KV_WIDTH), F32), jax.ShapeDtypeStruct((1, LANES), F32)],
        scratch_shapes=[pltpu.VMEM((N_Q_HEADS, BLOCK, BLOCK), F32), pltpu.VMEM((N_Q_HEADS, BLOCK, BLOCK), F32),
                        pltpu.VMEM((N_Q_HEADS, BLOCK, 2 * BLOCK), BF16),
                        pltpu.VMEM((N_Q_HEADS, BLOCK, 2 * BLOCK), BF16)],
        semantics=("arbitrary",))


def _mix_in_bwd(dq, dkv_own, dkv_prev, dpooled, dgate, h, g, win_t, dh_res, tm, carry=None):
    t, d = h.shape
    nt = t // tm

    def body(dq_ref, own_ref, prev_ref, prev_next_ref, dpool_ref, halo_ref, dgate_ref, h_ref, g_ref, w_ref, res_ref,
             dproj_ref, dh_ref, dhb_ref, dg_ref):
        i = pl.program_id(0)
        last = i == nt - 1
        dproj_ref[:, 0:OFF_KV] = dq_ref[...]
        from_next = jnp.where(last, 0.0, prev_next_ref[...])
        if tm > BLOCK:
            from_next = jnp.concatenate([prev_ref[BLOCK:tm, :], from_next], axis=0)
        dproj_ref[:, OFF_KV:OFF_Z] = (own_ref[...] + from_next).astype(BF16)
        halo = jnp.where(last, 0.0, halo_ref[...])
        for gi, width in enumerate(POOL_WINDOWS):
            lo, hi = gi * POOL_GROUP, (gi + 1) * POOL_GROUP
            dpg = dpool_ref[:, lo:hi]
            scaled = jnp.concatenate([dpg / _pool_counts(tm, width), halo[:, lo:hi] / float(width)], axis=0)
            dz = _leading_sums(scaled, gi)[0:tm, :] - dpg
            dproj_ref[:, OFF_Z + lo:OFF_Z + hi] = dz.astype(BF16)
        dproj_ref[:, OFF_GATE:IN_WIDTH] = dgate_ref[...]
        du = _nn(dproj_ref[...], w_ref[...])
        xh, r = _rms_fwd(h_ref[...], g_ref[...])
        dx, dg = _rms_bwd(du, xh, r, g_ref[...])
        dh = res_ref[...] + dx
        dh_ref[...] = dh
        dhb_ref[...] = dh.astype(BF16)

        @pl.when(i == 0)
        def _():
            dg_ref[...] = jnp.zeros_like(dg_ref)

        dg_ref[...] += dg

    per = tm // BLOCK
    next_block = pl.BlockSpec((BLOCK, 2 * KV_WIDTH), lambda i: (jnp.minimum((i + 1) * per, t // BLOCK - 1), 0))
    next_halo = pl.BlockSpec((HALO, POOL_WIDTH), lambda i: (jnp.minimum((i + 1) * (tm // HALO), t // HALO - 1), 0))
    return _launch(
        body, (dq, dkv_own, dkv_prev, dkv_prev, dpooled, dpooled, dgate, h, g, win_t, dh_res), carry,
        name="mix_in_bwd", grid=(nt,),
        in_specs=[_rows(tm, ATTN_WIDTH), _rows(tm, 2 * KV_WIDTH), _rows(tm, 2 * KV_WIDTH), next_block,
                  _rows(tm, POOL_WIDTH), next_halo, _rows(tm, 2 * d), _rows(tm, d), _resident((1, d)),
                  _resident((IN_WIDTH, d)), _rows(tm, d)],
        out_specs=[_rows(tm, IN_WIDTH), _rows(tm, d), _rows(tm, d), pl.BlockSpec((1, d), lambda i: (0, 0))],
        out_shape=[jax.ShapeDtypeStruct((t, IN_WIDTH), BF16), jax.ShapeDtypeStruct((t, d), F32),
                   jax.ShapeDtypeStruct((t, d), BF16), jax.ShapeDtypeStruct((1, d), F32)],
        semantics=("arbitrary",))


BIG = (("wup1_t", "ffn1_w_up", True), ("wdown1", "ffn1_w_down", False), ("win_t", "w_in", True),
       ("wattn", "w_attn_up", False), ("wpool_t", "w_pool_up", True), ("wout", "w_out", False),
       ("wup2_t", "ffn2_w_up", True), ("wdown2", "ffn2_w_down", False))
ANY = pl.BlockSpec(memory_space=pl.ANY)
WIRE = BF16


def _place():
    return lax.axis_index("x"), lax.axis_index("y"), lax.axis_index("c")


def _peer(k):
    x, y, c = _place()
    return x ^ (k >> 2), y ^ ((k >> 1) & 1), c ^ (k & 1)


def _index(px, py, pc):
    return 4 * px + 2 * py + pc


def _gather_carry(shards):
    n = len(shards)

    def tools(ins, outs, sems):
        send_sems, recv_sems, local_sems = sems
        x, y, c = _place()
        chips = [(1 - x, y), (x, 1 - y), (1 - x, 1 - y)]

        def rows(w, px, py, pc):
            r = ins[w].shape[0]
            return outs[w].at[pl.ds(_index(px, py, pc) * r, r), :]

        def copy(w, k, block, to, src=None):
            return pltpu.make_async_remote_copy(
                src_ref=rows(w, *block) if src is None else src, dst_ref=rows(w, *block),
                send_sem=send_sems.at[w, k], recv_sem=recv_sems.at[w, k], device_id=to, device_id_type=MESH)

        def own(w):
            return ([pltpu.make_async_copy(ins[w], rows(w, x, y, c), local_sems.at[w]),
                     copy(w, 0, (x, y, c), (x, y, 1 - c), src=ins[w])]
                    + [copy(w, 1 + j, (x, y, c), (*chip, c), src=ins[w]) for j, chip in enumerate(chips)])

        def passed(w, j):
            return copy(w, 4 + j, (*chips[j], c), (x, y, 1 - c))

        return (x, y, c), chips, copy, own, passed

    def start(ins, outs, sems):
        _, _, _, own, _ = tools(ins, outs, sems)
        for w in range(n):
            for cp in own(w):
                cp.start()

    def forward(w):
        def run(ins, outs, sems):
            (x, y, c), chips, copy, _, passed = tools(ins, outs, sems)
            for j, chip in enumerate(chips):
                copy(w, 1 + j, (*chip, c), (x, y, c)).wait_recv()
                passed(w, j).start()
        return run

    sizes = np.cumsum([s.size for s in shards]) / sum(s.size for s in shards)
    middles = [(float(sizes[w]), forward(w)) for w in range(n)]

    def finish(ins, outs, sems):
        (x, y, c), chips, copy, own, passed = tools(ins, outs, sems)
        for w in range(n):
            copy(w, 0, (x, y, 1 - c), (x, y, c)).wait_recv()
            for j, chip in enumerate(chips):
                copy(w, 4 + j, (*chip, 1 - c), (x, y, c)).wait_recv()
        for w in range(n):
            mine, *sent = own(w)
            for cp in sent + [passed(w, j) for j in range(len(chips))]:
                cp.wait_send()
            mine.wait()

    return _Carry(
        shards, [jax.ShapeDtypeStruct((N_DEV * s.shape[0], s.shape[1]), s.dtype) for s in shards],
        [pltpu.SemaphoreType.DMA((n, N_DEV - 1)), pltpu.SemaphoreType.DMA((n, N_DEV - 1)),
         pltpu.SemaphoreType.DMA((n,))], start, finish, middles)


def _scatter_carry(grads):
    n = len(grads)

    def tools(ins, outs, sems):
        send_sems, recv_sems, local_sems = sems
        me = _index(*_place())

        def block(ref, dev):
            r = ref.shape[0] // N_DEV
            return ref.at[pl.ds(dev * r, r), :]

        def copy(w, k, landing):
            to = _peer(k)
            return pltpu.make_async_remote_copy(
                src_ref=block(ins[w], _index(*to)), dst_ref=block(outs[w], landing), send_sem=send_sems.at[w, k - 1],
                recv_sem=recv_sems.at[w, k - 1], device_id=to, device_id_type=MESH)

        def mine(w):
            return pltpu.make_async_copy(block(ins[w], me), block(outs[w], me), local_sems.at[w])

        return me, copy, mine

    def start(ins, outs, sems):
        me, copy, mine = tools(ins, outs, sems)
        for w in range(n):
            mine(w).start()
            for k in range(1, N_DEV):
                copy(w, k, me).start()

    def finish(ins, outs, sems):
        _, copy, mine = tools(ins, outs, sems)
        for w in range(n):
            for k in range(1, N_DEV):
                copy(w, k, _index(*_peer(k))).wait()
            mine(w).wait()

    return _Carry(
        grads, [jax.ShapeDtypeStruct(g.shape, g.dtype) for g in grads],
        [pltpu.SemaphoreType.DMA((n, N_DEV - 1)), pltpu.SemaphoreType.DMA((n, N_DEV - 1)),
         pltpu.SemaphoreType.DMA((n,))], start, finish)


def _small_carry(small):
    srows = small.shape[0]

    def tools(ins, outs, sems):
        send_sems, recv_sems, local_sem = sems
        me = _index(*_place())

        def slot(dev):
            return outs[0].at[pl.ds(dev * srows, srows), :]

        def copy(k, landing):
            return pltpu.make_async_remote_copy(
                src_ref=ins[0], dst_ref=slot(landing), send_sem=send_sems.at[k - 1], recv_sem=recv_sems.at[k - 1],
                device_id=_peer(k), device_id_type=MESH)

        return me, copy, pltpu.make_async_copy(ins[0], slot(me), local_sem)

    def start(ins, outs, sems):
        me, copy, mine = tools(ins, outs, sems)
        mine.start()
        for k in range(1, N_DEV):
            copy(k, me).start()

    def finish(ins, outs, sems):
        _, copy, mine = tools(ins, outs, sems)
        for k in range(1, N_DEV):
            copy(k, _index(*_peer(k))).wait()
        mine.wait()

    return _Carry([small], [jax.ShapeDtypeStruct((N_DEV * srows, LANES), small.dtype)],
                  [pltpu.SemaphoreType.DMA((N_DEV - 1,)), pltpu.SemaphoreType.DMA((N_DEV - 1,)),
                   pltpu.SemaphoreType.DMA], start, finish)


def _exchange(carry, name):
    ci = len(carry.inputs)
    co = len(carry.out_shape)

    def body(*refs):
        parts = refs[:ci], refs[ci:ci + co], refs[ci + co:]
        carry.start(*parts)
        for _, fn in carry.middles:
            fn(*parts)
        carry.finish(*parts)

    return list(pl.pallas_call(body, name=name, in_specs=[ANY] * ci, out_specs=[ANY] * co, out_shape=carry.out_shape,
                               scratch_shapes=carry.scratch)(*carry.inputs))


def _adamw_math(w, g, m, v):
    m = ADAM_B1 * m + (1.0 - ADAM_B1) * g
    v = ADAM_B2 * v + (1.0 - ADAM_B2) * (g * g)
    m_hat = m / (1.0 - ADAM_B1 ** ADAM_STEP)
    v_hat = v / (1.0 - ADAM_B2 ** ADAM_STEP)
    return -ADAM_LR * (m_hat / (jnp.sqrt(v_hat) + ADAM_EPS) + ADAM_WD * w), m, v


def _sum_adamw(got, w, m, v, transposed, name):
    parts = list(got) if isinstance(got, (list, tuple)) else [got]
    r = parts[0].shape[0] // N_DEV
    cols = sum(part.shape[1] for part in parts)
    if transposed:
        (only,) = parts
        tile = cols if cols <= 512 else 256
        got_specs = [pl.BlockSpec((N_DEV, r, tile), lambda i: (0, 0, i))]
        spec, steps = pl.BlockSpec((tile, r), lambda i: (i, 0)), cols // tile
    else:
        tile = r if r <= 256 else r // 2
        got_specs = [pl.BlockSpec((N_DEV, tile, part.shape[1]), lambda i: (0, i, 0)) for part in parts]
        spec, steps = pl.BlockSpec((tile, cols), lambda i: (i, 0)), r // tile
    n = len(parts)

    def body(*refs):
        w_ref, m_ref, v_ref, g_ref, d_ref, m2_ref, v2_ref = refs[n:]
        sums = []
        for got_ref in refs[:n]:
            acc = got_ref[0].astype(F32)
            for dev in range(1, N_DEV):
                acc = acc + got_ref[dev].astype(F32)
            sums.append(acc)
        g = sums[0].T if transposed else (sums[0] if n == 1 else jnp.concatenate(sums, axis=1))
        g_ref[...] = g
        d_ref[...], m2_ref[...], v2_ref[...] = _adamw_math(w_ref[...], g, m_ref[...], v_ref[...])

    return pl.pallas_call(
        body, name=name, grid=(steps,), in_specs=got_specs + [spec, spec, spec], out_specs=[spec] * 4,
        out_shape=[jax.ShapeDtypeStruct(w.shape, F32)] * 4, compiler_params=_params("parallel"),
    )(*[part.reshape(N_DEV, r, part.shape[1]) for part in parts], w, m, v)


def _small_update(gathered, w, m, v):
    rows = w.shape[0]

    def body(all_ref, w_ref, m_ref, v_ref, g_ref, d_ref, m2_ref, v2_ref):
        g = all_ref[0]
        for dev in range(1, N_DEV):
            g = g + all_ref[dev]
        g_ref[...] = g
        d_ref[...], m2_ref[...], v2_ref[...] = _adamw_math(w_ref[...], g, m_ref[...], v_ref[...])

    return pl.pallas_call(
        body, name="small_update", out_shape=[jax.ShapeDtypeStruct((rows, LANES), F32)] * 4,
        compiler_params=pltpu.CompilerParams(vmem_limit_bytes=VMEM_LIMIT),
    )(gathered.reshape(N_DEV, rows, LANES), w, m, v)


SMALL = (("pool_w_mix", 512), ("ffn1_norm", 8), ("mix_norm", 8), ("ffn2_norm", 8), ("final_norm", 8),
         ("pool_scale", 8), ("sinks", 8), ("loss", 8))
SMALL_ROWS = sum(rows for _, rows in SMALL)


def _pack_small(parts):
    out = []
    for name, rows in SMALL:
        flat = parts[name].astype(F32).reshape(-1)
        out.append(jnp.pad(flat, (0, rows * LANES - flat.shape[0])).reshape(rows, LANES))
    return jnp.concatenate(out, axis=0)


def _unpack_small(packed, shapes):
    out, row = {}, 0
    for name, rows in SMALL:
        shape = shapes[name]
        size = int(np.prod(shape)) if shape else 1
        out[name] = packed[row:row + rows].reshape(-1)[:size].reshape(shape)
        row += rows
    return out


def kernel(x, ffn1_norm, ffn1_w_up, ffn1_w_down, mix_norm, w_in, sinks, w_attn_up, pool_w_mix, pool_scale, w_pool_up, w_out, ffn2_norm, ffn2_w_up, ffn2_w_down, final_norm, loss_target, m_ffn1_norm, m_ffn1_w_up, m_ffn1_w_down, m_mix_norm, m_w_in, m_sinks, m_w_attn_up, m_pool_w_mix, m_pool_scale, m_w_pool_up, m_w_out, m_ffn2_norm, m_ffn2_w_up, m_ffn2_w_down, m_final_norm, v_ffn1_norm, v_ffn1_w_up, v_ffn1_w_down, v_mix_norm, v_w_in, v_sinks, v_w_attn_up, v_pool_w_mix, v_pool_scale, v_w_pool_up, v_w_out, v_ffn2_norm, v_ffn2_w_up, v_ffn2_w_down, v_final_norm):
    args = dict(locals())
    weight_names = ("ffn1_norm", "ffn1_w_up", "ffn1_w_down", "mix_norm", "w_in", "sinks", "w_attn_up", "pool_w_mix",
                    "pool_scale", "w_pool_up", "w_out", "ffn2_norm", "ffn2_w_up", "ffn2_w_down", "final_norm")

    shard = {k: (args[p][0].T if tr else args[p][0]).astype(BF16) for k, p, tr in BIG}
    early, late = ["wdown1", "win_t", "wattn", "wpool_t", "wout"], ["wup2_t", "wdown2"]
    big = {"wup1_t": _exchange(_gather_carry([shard["wup1_t"]]), "gather_up1")[0]}

    xs, target = x[0], loss_target[0]
    t = xs.shape[0]
    tm_f, tm_b, tk = min(512, t), min(512, t), min(1024, t)
    g1, gm, g2, gf = ffn1_norm, mix_norm, ffn2_norm, final_norm.reshape(1, D_MODEL)
    dist = _attn_dist()
    sink_v = sinks.reshape(N_Q_HEADS)
    wmix_b = pool_w_mix[0].astype(BF16)

    (n1, ab1, act1), rest = _ffn_up(xs, g1, big["wup1_t"], tm_f, _gather_carry([shard[k] for k in early]))
    big.update(zip(early, rest))
    h1 = _ffn_down(xs, act1, big["wdown1"], tm_f)
    u, q, kv, z, gate = _mix_in_fwd(h1, gm, big["win_t"], tm_f)
    attn = _attn_fwd(q, kv, dist, sink_v)
    (h2, a, p, merged, ms, pooled), rest = _mix_out_fwd(
        attn, z, gate, h1, big["wattn"], wmix_b, pool_scale, big["wpool_t"], big["wout"], tm_b,
        _gather_carry([shard[k] for k in late]))
    big.update(zip(late, rest))
    (h3, ab2, n2, act2), _ = _ffn_fwd(h2, g2, big["wup2_t"], big["wdown2"], tm_f)
    loss_lanes, dh3, dhb3, dgf = _loss_head(h3, gf, target, tm_f)

    got = {}
    (gw_down2,), _ = _wgrad(act2, dhb3, 0.5, D_FF, tk, "wgrad_down2")
    (dab2,), (got["wdown2"],) = _ffn_bwd_hidden(dhb3, ab2, big["wdown2"], tm_f, _scatter_carry([gw_down2]))
    dh2, dg2 = _ffn_bwd_input(dab2, dh3, h2, g2, big["wup2_t"], tm_f)
    (gw_up2,), _ = _wgrad(dab2, n2, 1.0, D_FF, tk, "wgrad_up2")
    dhb2, da_b, dp_b, dattn, dgate, dpooled, dwmix, dscale = _mix_out_bwd(
        dh2, gate, a, p, pooled, big["wattn"], wmix_b, pool_scale, big["wpool_t"], big["wout"], tm_b)
    (gw_out,), _ = _wgrad(merged, dhb2, 1.0, D_MODEL, tk, "wgrad_out")
    (gw_attn,), _ = _wgrad(attn, da_b, 1.0, D_MODEL, tk, "wgrad_attn")
    (gw_pool,), _ = _wgrad(dp_b, ms, 1.0, D_MODEL, tk, "wgrad_pool")
    (dq, dkv_own, dkv_prev, dsinks), (got["wup2_t"],) = _attn_bwd(q, kv, dattn, dist, sink_v, _scatter_carry([gw_up2]))
    (dproj, dh1, dhb1, dgm), (got["wout"], got["wattn"], got["wpool_t"]) = _mix_in_bwd(
        dq, dkv_own, dkv_prev, dpooled, dgate, h1, gm, big["win_t"], dh2, tm_b,
        _scatter_carry([gw_out, gw_attn, gw_pool]))
    (gw_down1,), _ = _wgrad(act1, dhb1, 0.5, D_FF, tk, "wgrad_down1")
    (gw_in,), (got["wdown1"],) = _wgrad(dproj, u, 1.0, IN_WIDTH // 2, tk, "wgrad_in", _scatter_carry([gw_down1]))
    (dab1, dab1_t), (got["win_t"],) = _ffn_bwd_hidden(dhb1, ab1, big["wdown1"], tm_f, _scatter_carry([gw_in]),
                                                      turned=True)
    dx, dg1 = _ffn_bwd_input(dab1, dh1, xs, g1, big["wup1_t"], tm_f)
    small_parts = {"pool_w_mix": dwmix, "ffn1_norm": dg1, "mix_norm": dgm, "ffn2_norm": dg2, "final_norm": dgf,
                   "pool_scale": dscale, "sinks": dsinks[:, :N_Q_HEADS], "loss": loss_lanes[:, :1]}
    pieces = 4
    carry, got["wup1_t"] = _small_carry(_pack_small(small_parts)), []
    for i in range(pieces):
        (piece,), landed = _wgrad(dab1_t, n1, 1.0, D_FF, tk, "wgrad_up1_%d" % i, carry, part=(i, pieces), turned=True)
        if i == 0:
            (small_all,) = landed
        else:
            got["wup1_t"] += landed
        carry = _scatter_carry([piece])
    got["wup1_t"] += _exchange(carry, "scatter_up1_last")

    grad, delta, new_m, new_v = {}, {}, {}, {}
    for k, p, tr in BIG:
        outside = tr and args[p].shape[-1] % LANES != 0
        turn = (lambda a: a.T) if outside else (lambda a: a)
        res = _sum_adamw(got[k], turn(args[p][0]), turn(args["m_" + p][0]), turn(args["v_" + p][0]),
                         tr and not outside, "adamw_" + k)
        grad[p], delta[p], new_m[p], new_v[p] = (turn(a)[None] for a in res)

    shapes = {name: args[name].shape for name, _ in SMALL if name != "loss"}
    shapes["loss"] = ()
    packed = {pre: _pack_small({**{name: args[pre + name] for name, _ in SMALL if name != "loss"},
                                "loss": jnp.zeros((), F32)}) for pre in ("", "m_", "v_")}
    g_s, d_s, m_s, v_s = _small_update(small_all, packed[""], packed["m_"], packed["v_"])
    g_small, d_small, m_small, v_small = (_unpack_small(a, shapes) for a in (g_s, d_s, m_s, v_s))
    for name, _ in SMALL:
        if name != "loss":
            grad[name], delta[name], new_m[name], new_v[name] = (
                g_small[name], d_small[name], m_small[name], v_small[name])

    return (g_small["loss"], dx[None], *[grad[n] for n in weight_names], *[delta[n] for n in weight_names],
            *[new_m[n] for n in weight_names], *[new_v[n] for n in weight_names])
```

```python
import functools

import jax
import jax.numpy as jnp
import numpy as np
from jax import lax
from jax.experimental import pallas as pl
from jax.experimental.pallas import tpu as pltpu

F32 = jnp.float32
BF16 = jnp.bfloat16

D_MODEL = 1024
D_FF = 2816
N_Q_HEADS = 16
N_KV_HEADS = 2
Q_PER_KV = N_Q_HEADS // N_KV_HEADS
HEAD_DIM = 64
BLOCK = 128
ATTN_WIDTH = N_Q_HEADS * HEAD_DIM
KV_WIDTH = N_KV_HEADS * HEAD_DIM
POOL_WINDOWS = (2, 4, 8, 16)
POOL_GROUP = 128
POOL_WIDTH = 512
HALO = 16
IN_WIDTH = ATTN_WIDTH + 2 * KV_WIDTH + POOL_WIDTH + 2 * D_MODEL
OFF_KV = ATTN_WIDTH
OFF_Z = ATTN_WIDTH + 2 * KV_WIDTH
OFF_GATE = OFF_Z + POOL_WIDTH
NORM_EPS = 1e-6
ADAM_LR = 0.001
ADAM_B1 = 0.9
ADAM_B2 = 0.999
ADAM_EPS = 1e-08
ADAM_WD = 0.01
ADAM_STEP = 10

N_DEV = 8
N_CHIP = 4
LANES = 128
FF_CHUNK = 256
SLAB = 32
VMEM_LIMIT = 56 * 1024 * 1024
MESH = pl.DeviceIdType.MESH


def _nn(a, b):
    return jnp.dot(a, b, preferred_element_type=F32)


def _nt(a, b):
    return lax.dot_general(a, b, (((1,), (1,)), ((), ())), preferred_element_type=F32)


def _tn(a, b):
    return lax.dot_general(a, b, (((0,), (0,)), ((), ())), preferred_element_type=F32)


def _params(*sem):
    return pltpu.CompilerParams(dimension_semantics=sem, vmem_limit_bytes=VMEM_LIMIT)


def _resident(shape):
    return pl.BlockSpec(shape, lambda *_: (0,) * len(shape), pipeline_mode=pl.Buffered(1))


def _rows(tm, cols):
    return pl.BlockSpec((tm, cols), lambda i: (i, 0))


class _Carry:
    def __init__(self, inputs, out_shape, scratch, start, finish, middles=()):
        self.inputs, self.out_shape, self.scratch = list(inputs), list(out_shape), list(scratch)
        self.start, self.finish, self.middles = start, finish, list(middles)


def _launch(body, args, carry=None, *, name, grid, in_specs, out_specs, out_shape, scratch_shapes=(), semantics):
    in_specs, out_specs, out_shape, scratch_shapes = list(in_specs), list(out_specs), list(out_shape), list(scratch_shapes)
    if carry is None:
        res = pl.pallas_call(body, name=name, grid=grid, in_specs=in_specs, out_specs=out_specs, out_shape=out_shape,
                             scratch_shapes=scratch_shapes, compiler_params=_params(*semantics))(*args)
        return list(res), []
    ni, no, ns = len(in_specs), len(out_specs), len(scratch_shapes)
    ci, co = len(carry.inputs), len(carry.out_shape)
    total = int(np.prod(grid))

    def full(*refs):
        own_in, c_in = refs[:ni], refs[ni:ni + ci]
        own_out, c_out = refs[ni + ci:ni + ci + no], refs[ni + ci + no:ni + ci + no + co]
        own_scr, c_sem = refs[ni + ci + no + co:ni + ci + no + co + ns], refs[ni + ci + no + co + ns:]
        step = 0
        for axis, size in enumerate(grid):
            step = step * size + pl.program_id(axis)
        pl.when(step == 0)(lambda: carry.start(c_in, c_out, c_sem))
        for fraction, fn in carry.middles:
            at = min(total - 1, int(fraction * total) + 1)
            pl.when(step == at)(lambda fn=fn: fn(c_in, c_out, c_sem))
        body(*own_in, *own_out, *own_scr)
        pl.when(step == total - 1)(lambda: carry.finish(c_in, c_out, c_sem))

    res = pl.pallas_call(
        full, name=name, grid=grid, in_specs=in_specs + [ANY] * ci, out_specs=out_specs + [ANY] * co,
        out_shape=out_shape + carry.out_shape, scratch_shapes=scratch_shapes + carry.scratch,
        compiler_params=_params(*(["arbitrary"] * len(grid))),
    )(*args, *carry.inputs)
    return list(res[:no]), list(res[no:])


def _rms_fwd(xv, g):
    r = lax.rsqrt(jnp.mean(xv * xv, axis=-1, keepdims=True) + NORM_EPS)
    return xv * r, r


def _rms_bwd(dn, xh, r, g):
    dxh = dn * g
    dx = r * (dxh - xh * jnp.mean(dxh * xh, axis=-1, keepdims=True))
    return dx, jnp.sum(dn * xh, axis=0, keepdims=True)


def _ffn_loss(x, g, wup_t, wdown, gf, target, tm):
    t, d = x.shape
    f = wdown.shape[0]

    def body(x_ref, g_ref, wup_ref, wdn_ref, gf_ref, tgt_ref, ab_ref, n_ref, act_ref, loss_ref, dh_ref, dhb_ref, dg_ref):
        xv = x_ref[...]
        xh, _ = _rms_fwd(xv, g_ref[...])
        n = (xh * g_ref[...]).astype(BF16)
        n_ref[...] = n
        for c in range(f // FF_CHUNK):
            lo, hi = c * FF_CHUNK, (c + 1) * FF_CHUNK
            a = _nt(n, wup_ref[lo:hi, :])
            b = _nt(n, wup_ref[f + lo:f + hi, :])
            ab_ref[:, lo:hi] = a.astype(BF16)
            ab_ref[:, f + lo:f + hi] = b.astype(BF16)
            act_ref[:, lo:hi] = (a * jax.nn.sigmoid(a) * b).astype(BF16)
        h = xv + 0.5 * _nn(act_ref[...], wdn_ref[...])
        yh, r = _rms_fwd(h, gf_ref[...])
        err = yh * gf_ref[...] - tgt_ref[...]
        part = 0.5 * jnp.sum(jnp.mean(err * err, axis=-1, keepdims=True), axis=0, keepdims=True)
        dh, dg = _rms_bwd(err * (1.0 / d), yh, r, gf_ref[...])
        dh_ref[...] = dh
        dhb_ref[...] = dh.astype(BF16)

        @pl.when(pl.program_id(0) == 0)
        def _():
            dg_ref[...] = jnp.zeros_like(dg_ref)
            loss_ref[...] = jnp.zeros_like(loss_ref)

        dg_ref[...] += dg
        loss_ref[...] += jnp.broadcast_to(part, loss_ref.shape)

    return pl.pallas_call(
        body, name="ffn_loss", grid=(t // tm,),
        in_specs=[_rows(tm, d), _resident((1, d)), _resident((2 * f, d)), _resident((f, d)), _resident((1, d)),
                  _rows(tm, d)],
        out_specs=[_rows(tm, 2 * f), _rows(tm, d), _rows(tm, f), pl.BlockSpec((1, LANES), lambda i: (0, 0)),
                   _rows(tm, d), _rows(tm, d), pl.BlockSpec((1, d), lambda i: (0, 0))],
        out_shape=[jax.ShapeDtypeStruct((t, 2 * f), BF16), jax.ShapeDtypeStruct((t, d), BF16),
                   jax.ShapeDtypeStruct((t, f), BF16), jax.ShapeDtypeStruct((1, LANES), F32),
                   jax.ShapeDtypeStruct((t, d), F32), jax.ShapeDtypeStruct((t, d), BF16),
                   jax.ShapeDtypeStruct((1, d), F32)],
        compiler_params=_params("arbitrary"),
    )(x, g, wup_t, wdown, gf, target)


def _ffn_up(x, g, wup_t, tm, carry=None):
    t, d = x.shape
    f = wup_t.shape[0] // 2

    def body(x_ref, g_ref, wup_ref, n_ref, ab_ref, act_ref):
        xh, _ = _rms_fwd(x_ref[...], g_ref[...])
        n = (xh * g_ref[...]).astype(BF16)
        n_ref[...] = n
        for c in range(f // FF_CHUNK):
            lo, hi = c * FF_CHUNK, (c + 1) * FF_CHUNK
            a = _nt(n, wup_ref[lo:hi, :])
            b = _nt(n, wup_ref[f + lo:f + hi, :])
            ab_ref[:, lo:hi] = a.astype(BF16)
            ab_ref[:, f + lo:f + hi] = b.astype(BF16)
            act_ref[:, lo:hi] = (a * jax.nn.sigmoid(a) * b).astype(BF16)

    return _launch(
        body, (x, g, wup_t), carry, name="ffn_up", grid=(t // tm,),
        in_specs=[_rows(tm, d), _resident((1, d)), _resident((2 * f, d))],
        out_specs=[_rows(tm, d), _rows(tm, 2 * f), _rows(tm, f)],
        out_shape=[jax.ShapeDtypeStruct((t, d), BF16), jax.ShapeDtypeStruct((t, 2 * f), BF16),
                   jax.ShapeDtypeStruct((t, f), BF16)],
        semantics=("parallel",))


def _ffn_down(x, act, wdown, tm):
    t, d = x.shape
    f = wdown.shape[0]

    def body(x_ref, act_ref, wdn_ref, h_ref):
        h_ref[...] = x_ref[...] + 0.5 * _nn(act_ref[...], wdn_ref[...])

    return pl.pallas_call(
        body, name="ffn_down", grid=(t // tm,),
        in_specs=[_rows(tm, d), _rows(tm, f), _resident((f, d))], out_specs=_rows(tm, d),
        out_shape=jax.ShapeDtypeStruct((t, d), F32), compiler_params=_params("parallel"),
    )(x, act, wdown)


def _ffn_bwd_hidden(dhb, ab, wdown, tm, carry=None, turned=False):
    t, d = dhb.shape
    f = wdown.shape[0]

    def body(dh_ref, ab_ref, wdn_ref, dab_ref, *rest):
        dact_ref = rest[-1]
        half = dh_ref[...] * 0.5
        for c in range(f // FF_CHUNK):
            lo, hi = c * FF_CHUNK, (c + 1) * FF_CHUNK
            dact_ref[...] = _nt(half, wdn_ref[lo:hi, :])

            def slab(i, carry_):
                rows = pl.ds(pl.multiple_of(i * SLAB, SLAB), SLAB)
                a = ab_ref[rows, lo:hi].astype(F32)
                b = ab_ref[rows, f + lo:f + hi].astype(F32)
                s = jax.nn.sigmoid(a)
                ds_ = dact_ref[rows, :] * s
                dab_ref[rows, lo:hi] = (ds_ * b * (1.0 + a * (1.0 - s))).astype(BF16)
                dab_ref[rows, f + lo:f + hi] = (ds_ * a).astype(BF16)
                return carry_

            lax.fori_loop(0, tm // SLAB, slab, 0, unroll=True)
            if turned:
                rest[0][lo:hi, :] = dab_ref[:, lo:hi].T
                rest[0][f + lo:f + hi, :] = dab_ref[:, f + lo:f + hi].T

    extra_specs = [pl.BlockSpec((2 * f, tm), lambda i: (0, i))] if turned else []
    extra_shapes = [jax.ShapeDtypeStruct((2 * f, t), BF16)] if turned else []
    return _launch(
        body, (dhb, ab, wdown), carry, name="ffn_bwd_hidden", grid=(t // tm,),
        in_specs=[_rows(tm, d), _rows(tm, 2 * f), _resident((f, d))], out_specs=[_rows(tm, 2 * f)] + extra_specs,
        out_shape=[jax.ShapeDtypeStruct((t, 2 * f), BF16)] + extra_shapes,
        scratch_shapes=[pltpu.VMEM((tm, FF_CHUNK), F32)], semantics=("parallel",))


def _ffn_bwd_input(dab, dh, x, g, wup_t, tm):
    t, d = x.shape
    f2 = wup_t.shape[0]

    def body(dab_ref, dh_ref, x_ref, g_ref, wup_ref, dx_ref, dg_ref):
        dn = _nn(dab_ref[...], wup_ref[...])
        xh, r = _rms_fwd(x_ref[...], g_ref[...])
        dx, dg = _rms_bwd(dn, xh, r, g_ref[...])
        dx_ref[...] = dh_ref[...] + dx

        @pl.when(pl.program_id(0) == 0)
        def _():
            dg_ref[...] = jnp.zeros_like(dg_ref)

        dg_ref[...] += dg

    return pl.pallas_call(
        body, name="ffn_bwd_input", grid=(t // tm,),
        in_specs=[_rows(tm, f2), _rows(tm, d), _rows(tm, d), _resident((1, d)), _resident((f2, d))],
        out_specs=[_rows(tm, d), pl.BlockSpec((1, d), lambda i: (0, 0))],
        out_shape=[jax.ShapeDtypeStruct((t, d), F32), jax.ShapeDtypeStruct((1, d), F32)],
        compiler_params=_params("arbitrary"),
    )(dab, dh, x, g, wup_t)


def _wgrad(lhs, rhs, scale, bm, tk, name, carry=None, part=(0, 1), turned=False):
    t, m = lhs.shape[::-1] if turned else lhs.shape
    n = rhs.shape[1] // part[1]
    col = part[0]
    steps = t // tk
    chunk = bm if bm <= 2048 else bm // 2

    def body(l_ref, r_ref, o_ref, acc_ref):
        @pl.when(pl.program_id(1) == 0)
        def _():
            acc_ref[...] = jnp.zeros_like(acc_ref)

        for lo in range(0, bm, chunk):
            if turned:
                acc_ref[lo:lo + chunk, :] += _nn(l_ref[lo:lo + chunk, :], r_ref[...])
            else:
                acc_ref[lo:lo + chunk, :] += _tn(l_ref[:, lo:lo + chunk], r_ref[...])

        @pl.when(pl.program_id(1) == steps - 1)
        def _():
            o_ref[...] = (scale * acc_ref[...]).astype(o_ref.dtype)

    return _launch(
        body, (lhs, rhs), carry, name=name, grid=(m // bm, steps),
        in_specs=[pl.BlockSpec((bm, tk), lambda i, k: (i, k)) if turned else pl.BlockSpec((tk, bm), lambda i, k: (k, i)),
                  pl.BlockSpec((tk, n), lambda i, k: (k, col))],
        out_specs=[pl.BlockSpec((bm, n), lambda i, k: (i, 0))],
        out_shape=[jax.ShapeDtypeStruct((m, n), WIRE)],
        scratch_shapes=[pltpu.VMEM((bm, n), F32)], semantics=("parallel", "arbitrary"))


def _mix_in_fwd(h, g, win_t, tm):
    t, d = h.shape

    def body(h_ref, g_ref, w_ref, u_ref, q_ref, kv_ref, z_ref, gate_ref):
        xh, _ = _rms_fwd(h_ref[...], g_ref[...])
        u = (xh * g_ref[...]).astype(BF16)
        u_ref[...] = u
        q_ref[...] = _nt(u, w_ref[0:OFF_KV, :]).astype(BF16)
        kv_ref[...] = _nt(u, w_ref[OFF_KV:OFF_Z, :]).astype(BF16)
        z_ref[...] = _nt(u, w_ref[OFF_Z:OFF_GATE, :])
        gate_ref[...] = _nt(u, w_ref[OFF_GATE:IN_WIDTH, :])

    return pl.pallas_call(
        body, name="mix_in_fwd", grid=(t // tm,),
        in_specs=[_rows(tm, d), _resident((1, d)), _resident((IN_WIDTH, d))],
        out_specs=[_rows(tm, d), _rows(tm, ATTN_WIDTH), _rows(tm, 2 * KV_WIDTH), _rows(tm, POOL_WIDTH),
                   _rows(tm, 2 * D_MODEL)],
        out_shape=[jax.ShapeDtypeStruct((t, d), BF16), jax.ShapeDtypeStruct((t, ATTN_WIDTH), BF16),
                   jax.ShapeDtypeStruct((t, 2 * KV_WIDTH), BF16), jax.ShapeDtypeStruct((t, POOL_WIDTH), F32),
                   jax.ShapeDtypeStruct((t, 2 * D_MODEL), F32)],
        compiler_params=_params("parallel"),
    )(h, g, win_t)


ALIBI_SLOPES = tuple(float(s) for s in (2.0 ** (-8.0 * np.arange(1, N_Q_HEADS + 1, dtype=np.float32) / N_Q_HEADS)))


def _attn_dist():
    return jnp.asarray(((np.arange(BLOCK)[:, None] - np.arange(BLOCK)[None, :]) % BLOCK).astype(np.float32))


def _own_block():
    shape = (BLOCK, BLOCK)
    return lax.broadcasted_iota(jnp.int32, shape, 1) <= lax.broadcasted_iota(jnp.int32, shape, 0)


def _fold(band2, own):
    return jnp.where(own, band2[:, BLOCK:], band2[:, :BLOCK])


def _unfold(x, own):
    zero = jnp.zeros_like(x)
    return jnp.concatenate([jnp.where(own, zero, x), jnp.where(own, x, zero)], axis=1)


def _low_half(shape):
    return lax.broadcasted_iota(jnp.int32, shape, len(shape) - 1) < HEAD_DIM


def _both_halves(band, kv_head):
    low = _low_half(band.shape)
    swapped = pltpu.roll(band, HEAD_DIM, 1)
    return jnp.where(low, band, swapped) if kv_head == 0 else jnp.where(low, swapped, band)


def _pair_rows(ref, pair, scale=None):
    v = ref[:, LANES * pair:LANES * (pair + 1)]
    if scale is not None:
        v = v * scale
    low, zero = _low_half(v.shape), jnp.zeros_like(v)
    return jnp.concatenate([jnp.where(low, v, zero), jnp.where(low, zero, v)], axis=0)


def _per_head(even, odd):
    return jnp.where(lax.broadcasted_iota(jnp.int32, (2 * BLOCK, 1), 0) < BLOCK, even, odd)


def _twice(x):
    return jnp.concatenate([x, x], axis=0)


def _pair_scores(q_ref, kk, dist2, pair, first, own2):
    s2 = _nt(_pair_rows(q_ref, pair, HEAD_DIM ** -0.5), kk)
    before = jnp.where(first, -jnp.inf, s2[:, :BLOCK])
    slopes = _per_head(ALIBI_SLOPES[2 * pair], ALIBI_SLOPES[2 * pair + 1])
    return jnp.where(own2, s2[:, BLOCK:], before) - slopes * dist2


def _heads_of(stack):
    return jnp.where(_low_half((BLOCK, LANES)), stack[:BLOCK], stack[BLOCK:])


def _softmax_sink(s, sink):
    m = jnp.maximum(jnp.max(s, axis=-1, keepdims=True), sink)
    p = jnp.exp(s - m)
    psink = jnp.exp(sink - m)
    inv = 1.0 / (jnp.sum(p, axis=-1, keepdims=True) + psink)
    return p * inv, psink * inv


def _bands(kvc_ref, kvp_ref):
    kband = jnp.concatenate([kvp_ref[:, 0:LANES], kvc_ref[:, 0:LANES]], axis=0)
    vband = jnp.concatenate([kvp_ref[:, LANES:2 * LANES], kvc_ref[:, LANES:2 * LANES]], axis=0)
    return ([_both_halves(kband, hk) for hk in range(N_KV_HEADS)],
            [_both_halves(vband, hk) for hk in range(N_KV_HEADS)])


SMEM = pl.BlockSpec(memory_space=pltpu.SMEM)
PAIRS = range(N_Q_HEADS // 2)
PAIRS_PER_KV = Q_PER_KV // 2


def _attn_fwd(q, kv, dist, sinks):
    t = q.shape[0]

    def body(q_ref, kvc_ref, kvp_ref, dist_ref, sink_ref, o_ref, s_scr, p_scr):
        first = pl.program_id(0) == 0
        own2 = _twice(_own_block())
        dist2 = _twice(dist_ref[...])
        kk, vv = _bands(kvc_ref, kvp_ref)
        for pair in PAIRS:
            s_scr[pair] = _pair_scores(q_ref, kk[pair // PAIRS_PER_KV], dist2, pair, first, own2)
        for pair in PAIRS:
            probs, _ = _softmax_sink(s_scr[pair], _per_head(sink_ref[2 * pair], sink_ref[2 * pair + 1]))
            p_scr[pair] = _unfold(probs.astype(BF16), own2)
        for pair in PAIRS:
            out = _nn(p_scr[pair], vv[pair // PAIRS_PER_KV])
            o_ref[:, LANES * pair:LANES * (pair + 1)] = _heads_of(out).astype(BF16)

    return pl.pallas_call(
        body, name="attn_fwd", grid=(t // BLOCK,),
        in_specs=[_rows(BLOCK, ATTN_WIDTH), _rows(BLOCK, 2 * KV_WIDTH),
                  pl.BlockSpec((BLOCK, 2 * KV_WIDTH), lambda i: (jnp.maximum(i - 1, 0), 0)),
                  _resident(dist.shape), SMEM],
        out_specs=_rows(BLOCK, ATTN_WIDTH),
        out_shape=jax.ShapeDtypeStruct((t, ATTN_WIDTH), BF16),
        scratch_shapes=[pltpu.VMEM((len(PAIRS), 2 * BLOCK, BLOCK), F32),
                        pltpu.VMEM((len(PAIRS), 2 * BLOCK, 2 * BLOCK), BF16)],
        compiler_params=_params("parallel"),
    )(q, kv, kv, dist, sinks)


def _pool_counts(tm, width):
    row = pl.program_id(0) * tm + lax.broadcasted_iota(jnp.int32, (tm, 1), 0)
    return jnp.minimum(row + 1, width).astype(F32)


def _trailing_sums(zz, group):
    s = zz
    for k in range(group + 1):
        s = s + pltpu.roll(s, 1 << k, 0)
    return s


def _leading_sums(zz, group):
    rows = zz.shape[0]
    s = zz
    for k in range(group + 1):
        s = s + pltpu.roll(s, rows - (1 << k), 0)
    return s


def _mix_out_fwd(attn, z, gate, h, wattn, wmix, scale, wpool_t, wout, tm, carry=None):
    t, d = h.shape

    def body(attn_ref, z_ref, halo_ref, gate_ref, h_ref, wattn_ref, wmix_ref, scale_ref, wpool_ref, wout_ref,
             h2_ref, a_ref, p_ref, merged_ref, ms_ref, pooled_ref):
        halo = jnp.where(pl.program_id(0) == 0, 0.0, halo_ref[...])
        for gi, width in enumerate(POOL_WINDOWS):
            lo, hi = gi * POOL_GROUP, (gi + 1) * POOL_GROUP
            zg = z_ref[:, lo:hi]
            sums = _trailing_sums(jnp.concatenate([halo[:, lo:hi], zg], axis=0), gi)[HALO:, :]
            pooled = (sums / _pool_counts(tm, width) - zg).astype(BF16)
            pooled_ref[:, lo:hi] = pooled
            ms_ref[:, lo:hi] = (_nn(pooled, wmix_ref[gi]) * scale_ref[:, lo:hi]).astype(BF16)
        p = _nt(ms_ref[...], wpool_ref[...])
        a = _nn(attn_ref[...], wattn_ref[...])
        a_ref[...] = a
        p_ref[...] = p
        merged = (jax.nn.sigmoid(gate_ref[:, 0:d]) * a + jax.nn.sigmoid(gate_ref[:, d:2 * d]) * p).astype(BF16)
        merged_ref[...] = merged
        h2_ref[...] = h_ref[...] + _nn(merged, wout_ref[...])

    halo_spec = pl.BlockSpec((HALO, POOL_WIDTH), lambda i: (jnp.maximum(i * (tm // HALO) - 1, 0), 0))
    return _launch(
        body, (attn, z, z, gate, h, wattn, wmix, scale, wpool_t, wout), carry, name="mix_out_fwd", grid=(t // tm,),
        in_specs=[_rows(tm, ATTN_WIDTH), _rows(tm, POOL_WIDTH), halo_spec, _rows(tm, 2 * d), _rows(tm, d),
                  _resident(wattn.shape), _resident(wmix.shape), _resident(scale.shape), _resident(wpool_t.shape),
                  _resident(wout.shape)],
        out_specs=[_rows(tm, d), _rows(tm, d), _rows(tm, d), _rows(tm, d), _rows(tm, POOL_WIDTH),
                   _rows(tm, POOL_WIDTH)],
        out_shape=[jax.ShapeDtypeStruct((t, d), F32), jax.ShapeDtypeStruct((t, d), F32),
                   jax.ShapeDtypeStruct((t, d), F32), jax.ShapeDtypeStruct((t, d), BF16),
                   jax.ShapeDtypeStruct((t, POOL_WIDTH), BF16), jax.ShapeDtypeStruct((t, POOL_WIDTH), BF16)],
        semantics=("parallel",))


def _mix_out_bwd(dh, gate, a, p, pooled, wattn, wmix, scale, wpool_t, wout, tm):
    t, d = dh.shape

    def body(dh_ref, gate_ref, a_ref, p_ref, pooled_ref, wattn_ref, wmix_ref, scale_ref, wpool_ref, wout_ref,
             dhb_ref, dab_ref, dpb_ref, dattn_ref, dgate_ref, dpooled_ref, dwmix_ref, dscale_ref):
        @pl.when(pl.program_id(0) == 0)
        def _():
            dwmix_ref[...] = jnp.zeros_like(dwmix_ref)
            dscale_ref[...] = jnp.zeros_like(dscale_ref)

        dhb = dh_ref[...].astype(BF16)
        dhb_ref[...] = dhb
        dm = _nt(dhb, wout_ref[...])
        sa = jax.nn.sigmoid(gate_ref[:, 0:d])
        sp = jax.nn.sigmoid(gate_ref[:, d:2 * d])
        da = (dm * sa).astype(BF16)
        dp = (dm * sp).astype(BF16)
        dab_ref[...] = da
        dpb_ref[...] = dp
        dgate_ref[:, 0:d] = (dm * a_ref[...] * (sa * (1.0 - sa))).astype(BF16)
        dgate_ref[:, d:2 * d] = (dm * p_ref[...] * (sp * (1.0 - sp))).astype(BF16)
        dattn_ref[...] = _nt(da, wattn_ref[...]).astype(BF16)
        dms = _nn(dp, wpool_ref[...])
        for gi in range(len(POOL_WINDOWS)):
            lo, hi = gi * POOL_GROUP, (gi + 1) * POOL_GROUP
            pooled_g = pooled_ref[:, lo:hi]
            mixed = _nn(pooled_g, wmix_ref[gi])
            dscale_ref[:, lo:hi] += jnp.sum(dms[:, lo:hi] * mixed, axis=0, keepdims=True)
            dmixed = (dms[:, lo:hi] * scale_ref[:, lo:hi]).astype(BF16)
            dwmix_ref[gi] += _tn(pooled_g, dmixed)
            dpooled_ref[:, lo:hi] = _nt(dmixed, wmix_ref[gi])

    acc = lambda shape: pl.BlockSpec(shape, lambda i: (0,) * len(shape))
    return pl.pallas_call(
        body, name="mix_out_bwd", grid=(t // tm,),
        in_specs=[_rows(tm, d), _rows(tm, 2 * d), _rows(tm, d), _rows(tm, d), _rows(tm, POOL_WIDTH),
                  _resident(wattn.shape), _resident(wmix.shape), _resident(scale.shape), _resident(wpool_t.shape),
                  _resident(wout.shape)],
        out_specs=[_rows(tm, d), _rows(tm, d), _rows(tm, d), _rows(tm, ATTN_WIDTH), _rows(tm, 2 * d),
                   _rows(tm, POOL_WIDTH), acc(wmix.shape), acc((1, POOL_WIDTH))],
        out_shape=[jax.ShapeDtypeStruct((t, d), BF16), jax.ShapeDtypeStruct((t, d), BF16),
                   jax.ShapeDtypeStruct((t, d), BF16), jax.ShapeDtypeStruct((t, ATTN_WIDTH), BF16),
                   jax.ShapeDtypeStruct((t, 2 * d), BF16), jax.ShapeDtypeStruct((t, POOL_WIDTH), F32),
                   jax.ShapeDtypeStruct(wmix.shape, F32), jax.ShapeDtypeStruct((1, POOL_WIDTH), F32)],
        compiler_params=_params("arbitrary"),
    )(dh, gate, a, p, pooled, wattn, wmix, scale, wpool_t, wout)


def _fold_halves(x):
    return x + pltpu.roll(x, HEAD_DIM, 1)


def _attn_bwd(q, kv, dattn, dist, sinks, carry=None):
    t = q.shape[0]

    def body(q_ref, kvc_ref, kvp_ref, do_ref, dist_ref, sink_ref, dq_ref, dkv_own_ref, dkv_prev_ref, dsink_ref,
             s_scr, dp_scr, p_scr, ds_scr):
        first = pl.program_id(0) == 0

        @pl.when(first)
        def _():
            dsink_ref[...] = jnp.zeros_like(dsink_ref)

        own2 = _twice(_own_block())
        dist2 = _twice(dist_ref[...])
        kk, vv = _bands(kvc_ref, kvp_ref)
        lane = lax.broadcasted_iota(jnp.int32, (1, LANES), 1)
        for pair in PAIRS:
            hk = pair // PAIRS_PER_KV
            s_scr[pair] = _pair_scores(q_ref, kk[hk], dist2, pair, first, own2)
            dp_scr[pair] = _fold(_nt(_pair_rows(do_ref, pair), vv[hk]), own2)
        dsink = jnp.zeros((1, LANES), F32)
        for pair in PAIRS:
            probs, psink = _softmax_sink(s_scr[pair], _per_head(sink_ref[2 * pair], sink_ref[2 * pair + 1]))
            dprobs = dp_scr[pair]
            rowdot = jnp.sum(probs * dprobs, axis=-1, keepdims=True)
            p_scr[pair] = _unfold(probs.astype(BF16), own2)
            ds_scr[pair] = _unfold((probs * (dprobs - rowdot)).astype(BF16), own2)
            dsink_rows = -psink * rowdot
            for half, head in enumerate((2 * pair, 2 * pair + 1)):
                head_sum = jnp.sum(dsink_rows[half * BLOCK:(half + 1) * BLOCK], axis=0, keepdims=True)
                dsink = dsink + jnp.where(lane == head, head_sum, 0.0)
        dk_heads, dv_heads = [], []
        top = lax.broadcasted_iota(jnp.int32, (LANES, BLOCK), 0) < HEAD_DIM

        def by_head(x_t):
            zero = jnp.zeros_like(x_t)
            return jnp.concatenate([jnp.where(top, x_t, zero), jnp.where(top, zero, x_t)], axis=1)

        for hk in range(N_KV_HEADS):
            dk_t = jnp.zeros((LANES, 2 * BLOCK), F32)
            dv_t = jnp.zeros((LANES, 2 * BLOCK), F32)
            for pair in range(hk * PAIRS_PER_KV, (hk + 1) * PAIRS_PER_KV):
                cols = slice(LANES * pair, LANES * (pair + 1))
                dv_t = dv_t + _nn(by_head(do_ref[:, cols].T), p_scr[pair])
                dk_t = dk_t + _nn(by_head((q_ref[:, cols] * HEAD_DIM ** -0.5).T), ds_scr[pair])
                dq_ref[:, cols] = (_heads_of(_nn(ds_scr[pair], kk[hk])) * HEAD_DIM ** -0.5).astype(BF16)
            dk_heads.append(_fold_halves(dk_t.T))
            dv_heads.append(_fold_halves(dv_t.T))
        low = _low_half(dk_heads[0].shape)
        dkv = jnp.concatenate([jnp.where(low, dk_heads[0], dk_heads[1]), jnp.where(low, dv_heads[0], dv_heads[1])],
                              axis=1)
        dkv_prev_ref[...] = dkv[0:BLOCK, :]
        dkv_own_ref[...] = dkv[BLOCK:2 * BLOCK, :]
        dsink_ref[...] += dsink

    return _launch(
        body, (q, kv, kv, dattn, dist, sinks), carry, name="attn_bwd", grid=(t // BLOCK,),
        in_specs=[_rows(BLOCK, ATTN_WIDTH), _rows(BLOCK, 2 * KV_WIDTH),
                  pl.BlockSpec((BLOCK, 2 * KV_WIDTH), lambda i: (jnp.maximum(i - 1, 0), 0)),
                  _rows(BLOCK, ATTN_WIDTH), _resident(dist.shape), SMEM],
        out_specs=[_rows(BLOCK, ATTN_WIDTH), _rows(BLOCK, 2 * KV_WIDTH), _rows(BLOCK, 2 * KV_WIDTH),
                   pl.BlockSpec((1, LANES), lambda i: (0, 0))],
        out_shape=[jax.ShapeDtypeStruct((t, ATTN_WIDTH), BF16), jax.ShapeDtypeStruct((t, 2 * KV_WIDTH), F32),
                   jax.ShapeDtypeStruct((t, 2 * KV_WIDTH), F32), jax.ShapeDtypeStruct((1, LANES), F32)],
        scratch_shapes=[pltpu.VMEM((len(PAIRS), 2 * BLOCK, BLOCK), F32), pltpu.VMEM((len(PAIRS), 2 * BLOCK, BLOCK), F32),
                        pltpu.VMEM((len(PAIRS), 2 * BLOCK, 2 * BLOCK), BF16),
                        pltpu.VMEM((len(PAIRS), 2 * BLOCK, 2 * BLOCK), BF16)],
        semantics=("arbitrary",))


def _mix_in_bwd(dq, dkv_own, dkv_prev, dpooled, dgate, h, g, win_t, dh_res, tm, carry=None):
    t, d = h.shape
    nt = t // tm

    def body(dq_ref, own_ref, prev_ref, prev_next_ref, dpool_ref, halo_ref, dgate_ref, h_ref, g_ref, w_ref, res_ref,
             dproj_ref, dh_ref, dhb_ref, dg_ref):
        i = pl.program_id(0)
        last = i == nt - 1
        dproj_ref[:, 0:OFF_KV] = dq_ref[...]
        from_next = jnp.where(last, 0.0, prev_next_ref[...])
        if tm > BLOCK:
            from_next = jnp.concatenate([prev_ref[BLOCK:tm, :], from_next], axis=0)
        dproj_ref[:, OFF_KV:OFF_Z] = (own_ref[...] + from_next).astype(BF16)
        halo = jnp.where(last, 0.0, halo_ref[...])
        for gi, width in enumerate(POOL_WINDOWS):
            lo, hi = gi * POOL_GROUP, (gi + 1) * POOL_GROUP
            dpg = dpool_ref[:, lo:hi]
            scaled = jnp.concatenate([dpg / _pool_counts(tm, width), halo[:, lo:hi] / float(width)], axis=0)
            dz = _leading_sums(scaled, gi)[0:tm, :] - dpg
            dproj_ref[:, OFF_Z + lo:OFF_Z + hi] = dz.astype(BF16)
        dproj_ref[:, OFF_GATE:IN_WIDTH] = dgate_ref[...]
        du = _nn(dproj_ref[...], w_ref[...])
        xh, r = _rms_fwd(h_ref[...], g_ref[...])
        dx, dg = _rms_bwd(du, xh, r, g_ref[...])
        dh = res_ref[...] + dx
        dh_ref[...] = dh
        dhb_ref[...] = dh.astype(BF16)

        @pl.when(i == 0)
        def _():
            dg_ref[...] = jnp.zeros_like(dg_ref)

        dg_ref[...] += dg

    per = tm // BLOCK
    next_block = pl.BlockSpec((BLOCK, 2 * KV_WIDTH), lambda i: (jnp.minimum((i + 1) * per, t // BLOCK - 1), 0))
    next_halo = pl.BlockSpec((HALO, POOL_WIDTH), lambda i: (jnp.minimum((i + 1) * (tm // HALO), t // HALO - 1), 0))
    return _launch(
        body, (dq, dkv_own, dkv_prev, dkv_prev, dpooled, dpooled, dgate, h, g, win_t, dh_res), carry,
        name="mix_in_bwd", grid=(nt,),
        in_specs=[_rows(tm, ATTN_WIDTH), _rows(tm, 2 * KV_WIDTH), _rows(tm, 2 * KV_WIDTH), next_block,
                  _rows(tm, POOL_WIDTH), next_halo, _rows(tm, 2 * d), _rows(tm, d), _resident((1, d)),
                  _resident((IN_WIDTH, d)), _rows(tm, d)],
        out_specs=[_rows(tm, IN_WIDTH), _rows(tm, d), _rows(tm, d), pl.BlockSpec((1, d), lambda i: (0, 0))],
        out_shape=[jax.ShapeDtypeStruct((t, IN_WIDTH), BF16), jax.ShapeDtypeStruct((t, d), F32),
                   jax.ShapeDtypeStruct((t, d), BF16), jax.ShapeDtypeStruct((1, d), F32)],
        semantics=("arbitrary",))


BIG = (("wup1_t", "ffn1_w_up", True), ("wdown1", "ffn1_w_down", False), ("win_t", "w_in", True),
       ("wattn", "w_attn_up", False), ("wpool_t", "w_pool_up", True), ("wout", "w_out", False),
       ("wup2_t", "ffn2_w_up", True), ("wdown2", "ffn2_w_down", False))
ANY = pl.BlockSpec(memory_space=pl.ANY)
WIRE = BF16


def _place():
    return lax.axis_index("x"), lax.axis_index("y"), lax.axis_index("c")


def _peer(k):
    x, y, c = _place()
    return x ^ (k >> 2), y ^ ((k >> 1) & 1), c ^ (k & 1)


def _index(px, py, pc):
    return 4 * px + 2 * py + pc


def _gather_carry(shards):
    n = len(shards)

    def tools(ins, outs, sems):
        send_sems, recv_sems, local_sems = sems
        x, y, c = _place()
        chips = [(1 - x, y), (x, 1 - y), (1 - x, 1 - y)]

        def rows(w, px, py, pc):
            r = ins[w].shape[0]
            return outs[w].at[pl.ds(_index(px, py, pc) * r, r), :]

        def copy(w, k, block, to, src=None):
            return pltpu.make_async_remote_copy(
                src_ref=rows(w, *block) if src is None else src, dst_ref=rows(w, *block),
                send_sem=send_sems.at[w, k], recv_sem=recv_sems.at[w, k], device_id=to, device_id_type=MESH)

        def own(w):
            return ([pltpu.make_async_copy(ins[w], rows(w, x, y, c), local_sems.at[w]),
                     copy(w, 0, (x, y, c), (x, y, 1 - c), src=ins[w])]
                    + [copy(w, 1 + j, (x, y, c), (*chip, c), src=ins[w]) for j, chip in enumerate(chips)])

        def passed(w, j):
            return copy(w, 4 + j, (*chips[j], c), (x, y, 1 - c))

        return (x, y, c), chips, copy, own, passed

    def start(ins, outs, sems):
        _, _, _, own, _ = tools(ins, outs, sems)
        for w in range(n):
            for cp in own(w):
                cp.start()

    def forward(w):
        def run(ins, outs, sems):
            (x, y, c), chips, copy, _, passed = tools(ins, outs, sems)
            for j, chip in enumerate(chips):
                copy(w, 1 + j, (*chip, c), (x, y, c)).wait_recv()
                passed(w, j).start()
        return run

    sizes = np.cumsum([s.size for s in shards]) / sum(s.size for s in shards)
    middles = [(float(sizes[w]), forward(w)) for w in range(n)]

    def finish(ins, outs, sems):
        (x, y, c), chips, copy, own, passed = tools(ins, outs, sems)
        for w in range(n):
            copy(w, 0, (x, y, 1 - c), (x, y, c)).wait_recv()
            for j, chip in enumerate(chips):
                copy(w, 4 + j, (*chip, 1 - c), (x, y, c)).wait_recv()
        for w in range(n):
            mine, *sent = own(w)
            for cp in sent + [passed(w, j) for j in range(len(chips))]:
                cp.wait_send()
            mine.wait()

    return _Carry(
        shards, [jax.ShapeDtypeStruct((N_DEV * s.shape[0], s.shape[1]), s.dtype) for s in shards],
        [pltpu.SemaphoreType.DMA((n, N_DEV - 1)), pltpu.SemaphoreType.DMA((n, N_DEV - 1)),
         pltpu.SemaphoreType.DMA((n,))], start, finish, middles)


def _scatter_carry(grads):
    n = len(grads)

    def tools(ins, outs, sems):
        send_sems, recv_sems, local_sems = sems
        me = _index(*_place())

        def block(ref, dev):
            r = ref.shape[0] // N_DEV
            return ref.at[pl.ds(dev * r, r), :]

        def copy(w, k, landing):
            to = _peer(k)
            return pltpu.make_async_remote_copy(
                src_ref=block(ins[w], _index(*to)), dst_ref=block(outs[w], landing), send_sem=send_sems.at[w, k - 1],
                recv_sem=recv_sems.at[w, k - 1], device_id=to, device_id_type=MESH)

        def mine(w):
            return pltpu.make_async_copy(block(ins[w], me), block(outs[w], me), local_sems.at[w])

        return me, copy, mine

    def start(ins, outs, sems):
        me, copy, mine = tools(ins, outs, sems)
        for w in range(n):
            mine(w).start()
            for k in range(1, N_DEV):
                copy(w, k, me).start()

    def finish(ins, outs, sems):
        _, copy, mine = tools(ins, outs, sems)
        for w in range(n):
            for k in range(1, N_DEV):
                copy(w, k, _index(*_peer(k))).wait()
            mine(w).wait()

    return _Carry(
        grads, [jax.ShapeDtypeStruct(g.shape, g.dtype) for g in grads],
        [pltpu.SemaphoreType.DMA((n, N_DEV - 1)), pltpu.SemaphoreType.DMA((n, N_DEV - 1)),
         pltpu.SemaphoreType.DMA((n,))], start, finish)


def _small_carry(small):
    srows = small.shape[0]

    def tools(ins, outs, sems):
        send_sems, recv_sems, local_sem = sems
        me = _index(*_place())

        def slot(dev):
            return outs[0].at[pl.ds(dev * srows, srows), :]

        def copy(k, landing):
            return pltpu.make_async_remote_copy(
                src_ref=ins[0], dst_ref=slot(landing), send_sem=send_sems.at[k - 1], recv_sem=recv_sems.at[k - 1],
                device_id=_peer(k), device_id_type=MESH)

        return me, copy, pltpu.make_async_copy(ins[0], slot(me), local_sem)

    def start(ins, outs, sems):
        me, copy, mine = tools(ins, outs, sems)
        mine.start()
        for k in range(1, N_DEV):
            copy(k, me).start()

    def finish(ins, outs, sems):
        _, copy, mine = tools(ins, outs, sems)
        for k in range(1, N_DEV):
            copy(k, _index(*_peer(k))).wait()
        mine.wait()

    return _Carry([small], [jax.ShapeDtypeStruct((N_DEV * srows, LANES), small.dtype)],
                  [pltpu.SemaphoreType.DMA((N_DEV - 1,)), pltpu.SemaphoreType.DMA((N_DEV - 1,)),
                   pltpu.SemaphoreType.DMA], start, finish)


def _exchange(carry, name):
    ci = len(carry.inputs)
    co = len(carry.out_shape)

    def body(*refs):
        parts = refs[:ci], refs[ci:ci + co], refs[ci + co:]
        carry.start(*parts)
        for _, fn in carry.middles:
            fn(*parts)
        carry.finish(*parts)

    return list(pl.pallas_call(body, name=name, in_specs=[ANY] * ci, out_specs=[ANY] * co, out_shape=carry.out_shape,
                               scratch_shapes=carry.scratch)(*carry.inputs))


def _adamw_math(w, g, m, v):
    m = ADAM_B1 * m + (1.0 - ADAM_B1) * g
    v = ADAM_B2 * v + (1.0 - ADAM_B2) * (g * g)
    m_hat = m / (1.0 - ADAM_B1 ** ADAM_STEP)
    v_hat = v / (1.0 - ADAM_B2 ** ADAM_STEP)
    return -ADAM_LR * (m_hat / (jnp.sqrt(v_hat) + ADAM_EPS) + ADAM_WD * w), m, v


def _sum_adamw(got, w, m, v, transposed, name):
    parts = list(got) if isinstance(got, (list, tuple)) else [got]
    r = parts[0].shape[0] // N_DEV
    cols = sum(part.shape[1] for part in parts)
    if transposed:
        (only,) = parts
        tile = cols if cols <= 512 else 256
        got_specs = [pl.BlockSpec((N_DEV, r, tile), lambda i: (0, 0, i))]
        spec, steps = pl.BlockSpec((tile, r), lambda i: (i, 0)), cols // tile
    else:
        tile = r if r <= 256 else r // 2
        got_specs = [pl.BlockSpec((N_DEV, tile, part.shape[1]), lambda i: (0, i, 0)) for part in parts]
        spec, steps = pl.BlockSpec((tile, cols), lambda i: (i, 0)), r // tile
    n = len(parts)

    def body(*refs):
        w_ref, m_ref, v_ref, g_ref, d_ref, m2_ref, v2_ref = refs[n:]
        sums = []
        for got_ref in refs[:n]:
            acc = got_ref[0].astype(F32)
            for dev in range(1, N_DEV):
                acc = acc + got_ref[dev].astype(F32)
            sums.append(acc)
        g = sums[0].T if transposed else (sums[0] if n == 1 else jnp.concatenate(sums, axis=1))
        g_ref[...] = g
        d_ref[...], m2_ref[...], v2_ref[...] = _adamw_math(w_ref[...], g, m_ref[...], v_ref[...])

    return pl.pallas_call(
        body, name=name, grid=(steps,), in_specs=got_specs + [spec, spec, spec], out_specs=[spec] * 4,
        out_shape=[jax.ShapeDtypeStruct(w.shape, F32)] * 4, compiler_params=_params("parallel"),
    )(*[part.reshape(N_DEV, r, part.shape[1]) for part in parts], w, m, v)


def _small_update(gathered, w, m, v):
    rows = w.shape[0]

    def body(all_ref, w_ref, m_ref, v_ref, g_ref, d_ref, m2_ref, v2_ref):
        g = all_ref[0]
        for dev in range(1, N_DEV):
            g = g + all_ref[dev]
        g_ref[...] = g
        d_ref[...], m2_ref[...], v2_ref[...] = _adamw_math(w_ref[...], g, m_ref[...], v_ref[...])

    return pl.pallas_call(
        body, name="small_update", out_shape=[jax.ShapeDtypeStruct((rows, LANES), F32)] * 4,
        compiler_params=pltpu.CompilerParams(vmem_limit_bytes=VMEM_LIMIT),
    )(gathered.reshape(N_DEV, rows, LANES), w, m, v)


SMALL = (("pool_w_mix", 512), ("ffn1_norm", 8), ("mix_norm", 8), ("ffn2_norm", 8), ("final_norm", 8),
         ("pool_scale", 8), ("sinks", 8), ("loss", 8))
SMALL_ROWS = sum(rows for _, rows in SMALL)


def _pack_small(parts):
    out = []
    for name, rows in SMALL:
        flat = parts[name].astype(F32).reshape(-1)
        out.append(jnp.pad(flat, (0, rows * LANES - flat.shape[0])).reshape(rows, LANES))
    return jnp.concatenate(out, axis=0)


def _unpack_small(packed, shapes):
    out, row = {}, 0
    for name, rows in SMALL:
        shape = shapes[name]
        size = int(np.prod(shape)) if shape else 1
        out[name] = packed[row:row + rows].reshape(-1)[:size].reshape(shape)
        row += rows
    return out


def kernel(x, ffn1_norm, ffn1_w_up, ffn1_w_down, mix_norm, w_in, sinks, w_attn_up, pool_w_mix, pool_scale, w_pool_up, w_out, ffn2_norm, ffn2_w_up, ffn2_w_down, final_norm, loss_target, m_ffn1_norm, m_ffn1_w_up, m_ffn1_w_down, m_mix_norm, m_w_in, m_sinks, m_w_attn_up, m_pool_w_mix, m_pool_scale, m_w_pool_up, m_w_out, m_ffn2_norm, m_ffn2_w_up, m_ffn2_w_down, m_final_norm, v_ffn1_norm, v_ffn1_w_up, v_ffn1_w_down, v_mix_norm, v_w_in, v_sinks, v_w_attn_up, v_pool_w_mix, v_pool_scale, v_w_pool_up, v_w_out, v_ffn2_norm, v_ffn2_w_up, v_ffn2_w_down, v_final_norm):
    args = dict(locals())
    weight_names = ("ffn1_norm", "ffn1_w_up", "ffn1_w_down", "mix_norm", "w_in", "sinks", "w_attn_up", "pool_w_mix",
                    "pool_scale", "w_pool_up", "w_out", "ffn2_norm", "ffn2_w_up", "ffn2_w_down", "final_norm")

    shard = {k: (args[p][0].T if tr else args[p][0]).astype(BF16) for k, p, tr in BIG}
    early, late = ["wdown1", "win_t", "wattn", "wpool_t", "wout"], ["wup2_t", "wdown2"]
    big = {"wup1_t": _exchange(_gather_carry([shard["wup1_t"]]), "gather_up1")[0]}

    xs, target = x[0], loss_target[0]
    t = xs.shape[0]
    tm_f, tm_b, tk = min(512, t), min(512, t), min(1024, t)
    g1, gm, g2, gf = ffn1_norm, mix_norm, ffn2_norm, final_norm.reshape(1, D_MODEL)
    dist = _attn_dist()
    sink_v = sinks.reshape(N_Q_HEADS)
    wmix_b = pool_w_mix[0].astype(BF16)

    (n1, ab1, act1), rest = _ffn_up(xs, g1, big["wup1_t"], tm_f, _gather_carry([shard[k] for k in early]))
    big.update(zip(early, rest))
    h1 = _ffn_down(xs, act1, big["wdown1"], tm_f)
    u, q, kv, z, gate = _mix_in_fwd(h1, gm, big["win_t"], tm_f)
    attn = _attn_fwd(q, kv, dist, sink_v)
    (h2, a, p, merged, ms, pooled), rest = _mix_out_fwd(
        attn, z, gate, h1, big["wattn"], wmix_b, pool_scale, big["wpool_t"], big["wout"], tm_b,
        _gather_carry([shard[k] for k in late]))
    big.update(zip(late, rest))
    ab2, n2, act2, loss_lanes, dh3, dhb3, dgf = _ffn_loss(h2, g2, big["wup2_t"], big["wdown2"], gf, target, tm_f)

    got = {}
    (gw_down2,), _ = _wgrad(act2, dhb3, 0.5, D_FF, tk, "wgrad_down2")
    (dab2,), (got["wdown2"],) = _ffn_bwd_hidden(dhb3, ab2, big["wdown2"], tm_f, _scatter_carry([gw_down2]))
    dh2, dg2 = _ffn_bwd_input(dab2, dh3, h2, g2, big["wup2_t"], tm_f)
    (gw_up2,), _ = _wgrad(dab2, n2, 1.0, D_FF, tk, "wgrad_up2")
    dhb2, da_b, dp_b, dattn, dgate, dpooled, dwmix, dscale = _mix_out_bwd(
        dh2, gate, a, p, pooled, big["wattn"], wmix_b, pool_scale, big["wpool_t"], big["wout"], tm_b)
    (gw_out,), _ = _wgrad(merged, dhb2, 1.0, D_MODEL, tk, "wgrad_out")
    (gw_attn,), _ = _wgrad(attn, da_b, 1.0, D_MODEL, tk, "wgrad_attn")
    (gw_pool,), _ = _wgrad(dp_b, ms, 1.0, D_MODEL, tk, "wgrad_pool")
    (dq, dkv_own, dkv_prev, dsinks), (got["wup2_t"],) = _attn_bwd(q, kv, dattn, dist, sink_v, _scatter_carry([gw_up2]))
    (dproj, dh1, dhb1, dgm), (got["wout"], got["wattn"], got["wpool_t"]) = _mix_in_bwd(
        dq, dkv_own, dkv_prev, dpooled, dgate, h1, gm, big["win_t"], dh2, tm_b,
        _scatter_carry([gw_out, gw_attn, gw_pool]))
    (gw_down1,), _ = _wgrad(act1, dhb1, 0.5, D_FF, tk, "wgrad_down1")
    (gw_in,), (got["wdown1"],) = _wgrad(dproj, u, 1.0, IN_WIDTH // 2, tk, "wgrad_in", _scatter_carry([gw_down1]))
    (dab1, dab1_t), (got["win_t"],) = _ffn_bwd_hidden(dhb1, ab1, big["wdown1"], tm_f, _scatter_carry([gw_in]),
                                                      turned=True)
    dx, dg1 = _ffn_bwd_input(dab1, dh1, xs, g1, big["wup1_t"], tm_f)
    small_parts = {"pool_w_mix": dwmix, "ffn1_norm": dg1, "mix_norm": dgm, "ffn2_norm": dg2, "final_norm": dgf,
                   "pool_scale": dscale, "sinks": dsinks[:, :N_Q_HEADS], "loss": loss_lanes[:, :1]}
    pieces = 4
    carry, got["wup1_t"] = _small_carry(_pack_small(small_parts)), []
    for i in range(pieces):
        (piece,), landed = _wgrad(dab1_t, n1, 1.0, D_FF, tk, "wgrad_up1_%d" % i, carry, part=(i, pieces), turned=True)
        if i == 0:
            (small_all,) = landed
        else:
            got["wup1_t"] += landed
        carry = _scatter_carry([piece])
    got["wup1_t"] += _exchange(carry, "scatter_up1_last")

    grad, delta, new_m, new_v = {}, {}, {}, {}
    for k, p, tr in BIG:
        outside = tr and args[p].shape[-1] % LANES != 0
        turn = (lambda a: a.T) if outside else (lambda a: a)
        res = _sum_adamw(got[k], turn(args[p][0]), turn(args["m_" + p][0]), turn(args["v_" + p][0]),
                         tr and not outside, "adamw_" + k)
        grad[p], delta[p], new_m[p], new_v[p] = (turn(a)[None] for a in res)

    shapes = {name: args[name].shape for name, _ in SMALL if name != "loss"}
    shapes["loss"] = ()
    packed = {pre: _pack_small({**{name: args[pre + name] for name, _ in SMALL if name != "loss"},
                                "loss": jnp.zeros((), F32)}) for pre in ("", "m_", "v_")}
    g_s, d_s, m_s, v_s = _small_update(small_all, packed[""], packed["m_"], packed["v_"])
    g_small, d_small, m_small, v_small = (_unpack_small(a, shapes) for a in (g_s, d_s, m_s, v_s))
    for name, _ in SMALL:
        if name != "loss":
            grad[name], delta[name], new_m[name], new_v[name] = (
                g_small[name], d_small[name], m_small[name], v_small[name])

    return (g_small["loss"], dx[None], *[grad[n] for n in weight_names], *[delta[n] for n in weight_names],
            *[new_m[n] for n in weight_names], *[new_v[n] for n in weight_names])
```

```python
import functools

import jax
import jax.numpy as jnp
import numpy as np
from jax import lax
from jax.experimental import pallas as pl
from jax.experimental.pallas import tpu as pltpu

F32 = jnp.float32
BF16 = jnp.bfloat16

D_MODEL = 1024
D_FF = 2816
N_Q_HEADS = 16
N_KV_HEADS = 2
Q_PER_KV = N_Q_HEADS // N_KV_HEADS
HEAD_DIM = 64
BLOCK = 128
ATTN_WIDTH = N_Q_HEADS * HEAD_DIM
KV_WIDTH = N_KV_HEADS * HEAD_DIM
POOL_WINDOWS = (2, 4, 8, 16)
POOL_GROUP = 128
POOL_WIDTH = 512
HALO = 16
IN_WIDTH = ATTN_WIDTH + 2 * KV_WIDTH + POOL_WIDTH + 2 * D_MODEL
OFF_KV = ATTN_WIDTH
OFF_Z = ATTN_WIDTH + 2 * KV_WIDTH
OFF_GATE = OFF_Z + POOL_WIDTH
NORM_EPS = 1e-6
ADAM_LR = 0.001
ADAM_B1 = 0.9
ADAM_B2 = 0.999
ADAM_EPS = 1e-08
ADAM_WD = 0.01
ADAM_STEP = 10

N_DEV = 8
N_CHIP = 4
LANES = 128
FF_CHUNK = 256
SLAB = 32
VMEM_LIMIT = 56 * 1024 * 1024
MESH = pl.DeviceIdType.MESH


def _nn(a, b):
    return jnp.dot(a, b, preferred_element_type=F32)


def _nt(a, b):
    return lax.dot_general(a, b, (((1,), (1,)), ((), ())), preferred_element_type=F32)


def _tn(a, b):
    return lax.dot_general(a, b, (((0,), (0,)), ((), ())), preferred_element_type=F32)


def _params(*sem):
    return pltpu.CompilerParams(dimension_semantics=sem, vmem_limit_bytes=VMEM_LIMIT)


def _resident(shape):
    return pl.BlockSpec(shape, lambda *_: (0,) * len(shape), pipeline_mode=pl.Buffered(1))


def _rows(tm, cols):
    return pl.BlockSpec((tm, cols), lambda i: (i, 0))


class _Carry:
    def __init__(self, inputs, out_shape, scratch, start, finish, middles=()):
        self.inputs, self.out_shape, self.scratch = list(inputs), list(out_shape), list(scratch)
        self.start, self.finish, self.middles = start, finish, list(middles)


def _launch(body, args, carry=None, *, name, grid, in_specs, out_specs, out_shape, scratch_shapes=(), semantics):
    in_specs, out_specs, out_shape, scratch_shapes = list(in_specs), list(out_specs), list(out_shape), list(scratch_shapes)
    if carry is None:
        res = pl.pallas_call(body, name=name, grid=grid, in_specs=in_specs, out_specs=out_specs, out_shape=out_shape,
                             scratch_shapes=scratch_shapes, compiler_params=_params(*semantics))(*args)
        return list(res), []
    ni, no, ns = len(in_specs), len(out_specs), len(scratch_shapes)
    ci, co = len(carry.inputs), len(carry.out_shape)
    total = int(np.prod(grid))

    def full(*refs):
        own_in, c_in = refs[:ni], refs[ni:ni + ci]
        own_out, c_out = refs[ni + ci:ni + ci + no], refs[ni + ci + no:ni + ci + no + co]
        own_scr, c_sem = refs[ni + ci + no + co:ni + ci + no + co + ns], refs[ni + ci + no + co + ns:]
        step = 0
        for axis, size in enumerate(grid):
            step = step * size + pl.program_id(axis)
        pl.when(step == 0)(lambda: carry.start(c_in, c_out, c_sem))
        for fraction, fn in carry.middles:
            at = min(total - 1, int(fraction * total) + 1)
            pl.when(step == at)(lambda fn=fn: fn(c_in, c_out, c_sem))
        body(*own_in, *own_out, *own_scr)
        pl.when(step == total - 1)(lambda: carry.finish(c_in, c_out, c_sem))

    res = pl.pallas_call(
        full, name=name, grid=grid, in_specs=in_specs + [ANY] * ci, out_specs=out_specs + [ANY] * co,
        out_shape=out_shape + carry.out_shape, scratch_shapes=scratch_shapes + carry.scratch,
        compiler_params=_params(*(["arbitrary"] * len(grid))),
    )(*args, *carry.inputs)
    return list(res[:no]), list(res[no:])


def _rms_fwd(xv, g):
    r = lax.rsqrt(jnp.mean(xv * xv, axis=-1, keepdims=True) + NORM_EPS)
    return xv * r, r


def _rms_bwd(dn, xh, r, g):
    dxh = dn * g
    dx = r * (dxh - xh * jnp.mean(dxh * xh, axis=-1, keepdims=True))
    return dx, jnp.sum(dn * xh, axis=0, keepdims=True)


def _ffn_loss(x, g, wup_t, wdown, gf, target, tm):
    t, d = x.shape
    f = wdown.shape[0]

    def body(x_ref, g_ref, wup_ref, wdn_ref, gf_ref, tgt_ref, ab_ref, n_ref, act_ref, loss_ref, dh_ref, dhb_ref, dg_ref):
        xv = x_ref[...]
        xh, _ = _rms_fwd(xv, g_ref[...])
        n = (xh * g_ref[...]).astype(BF16)
        n_ref[...] = n
        for c in range(f // FF_CHUNK):
            lo, hi = c * FF_CHUNK, (c + 1) * FF_CHUNK
            a = _nt(n, wup_ref[lo:hi, :])
            b = _nt(n, wup_ref[f + lo:f + hi, :])
            ab_ref[:, lo:hi] = a.astype(BF16)
            ab_ref[:, f + lo:f + hi] = b.astype(BF16)
            act_ref[:, lo:hi] = (a * jax.nn.sigmoid(a) * b).astype(BF16)
        h = xv + 0.5 * _nn(act_ref[...], wdn_ref[...])
        yh, r = _rms_fwd(h, gf_ref[...])
        err = yh * gf_ref[...] - tgt_ref[...]
        part = 0.5 * jnp.sum(jnp.mean(err * err, axis=-1, keepdims=True), axis=0, keepdims=True)
        dh, dg = _rms_bwd(err * (1.0 / d), yh, r, gf_ref[...])
        dh_ref[...] = dh
        dhb_ref[...] = dh.astype(BF16)

        @pl.when(pl.program_id(0) == 0)
        def _():
            dg_ref[...] = jnp.zeros_like(dg_ref)
            loss_ref[...] = jnp.zeros_like(loss_ref)

        dg_ref[...] += dg
        loss_ref[...] += jnp.broadcast_to(part, loss_ref.shape)

    return pl.pallas_call(
        body, name="ffn_loss", grid=(t // tm,),
        in_specs=[_rows(tm, d), _resident((1, d)), _resident((2 * f, d)), _resident((f, d)), _resident((1, d)),
                  _rows(tm, d)],
        out_specs=[_rows(tm, 2 * f), _rows(tm, d), _rows(tm, f), pl.BlockSpec((1, LANES), lambda i: (0, 0)),
                   _rows(tm, d), _rows(tm, d), pl.BlockSpec((1, d), lambda i: (0, 0))],
        out_shape=[jax.ShapeDtypeStruct((t, 2 * f), BF16), jax.ShapeDtypeStruct((t, d), BF16),
                   jax.ShapeDtypeStruct((t, f), BF16), jax.ShapeDtypeStruct((1, LANES), F32),
                   jax.ShapeDtypeStruct((t, d), F32), jax.ShapeDtypeStruct((t, d), BF16),
                   jax.ShapeDtypeStruct((1, d), F32)],
        compiler_params=_params("arbitrary"),
    )(x, g, wup_t, wdown, gf, target)


def _ffn_up(x, g, wup_t, tm, carry=None):
    t, d = x.shape
    f = wup_t.shape[0] // 2

    def body(x_ref, g_ref, wup_ref, n_ref, ab_ref, act_ref):
        xh, _ = _rms_fwd(x_ref[...], g_ref[...])
        n = (xh * g_ref[...]).astype(BF16)
        n_ref[...] = n
        for c in range(f // FF_CHUNK):
            lo, hi = c * FF_CHUNK, (c + 1) * FF_CHUNK
            a = _nt(n, wup_ref[lo:hi, :])
            b = _nt(n, wup_ref[f + lo:f + hi, :])
            ab_ref[:, lo:hi] = a.astype(BF16)
            ab_ref[:, f + lo:f + hi] = b.astype(BF16)
            act_ref[:, lo:hi] = (a * jax.nn.sigmoid(a) * b).astype(BF16)

    return _launch(
        body, (x, g, wup_t), carry, name="ffn_up", grid=(t // tm,),
        in_specs=[_rows(tm, d), _resident((1, d)), _resident((2 * f, d))],
        out_specs=[_rows(tm, d), _rows(tm, 2 * f), _rows(tm, f)],
        out_shape=[jax.ShapeDtypeStruct((t, d), BF16), jax.ShapeDtypeStruct((t, 2 * f), BF16),
                   jax.ShapeDtypeStruct((t, f), BF16)],
        semantics=("parallel",))


def _ffn_down(x, act, wdown, tm):
    t, d = x.shape
    f = wdown.shape[0]

    def body(x_ref, act_ref, wdn_ref, h_ref):
        h_ref[...] = x_ref[...] + 0.5 * _nn(act_ref[...], wdn_ref[...])

    return pl.pallas_call(
        body, name="ffn_down", grid=(t // tm,),
        in_specs=[_rows(tm, d), _rows(tm, f), _resident((f, d))], out_specs=_rows(tm, d),
        out_shape=jax.ShapeDtypeStruct((t, d), F32), compiler_params=_params("parallel"),
    )(x, act, wdown)


def _ffn_bwd_hidden(dhb, ab, wdown, tm, carry=None):
    t, d = dhb.shape
    f = wdown.shape[0]

    def body(dh_ref, ab_ref, wdn_ref, dab_ref, dact_ref):
        half = dh_ref[...] * 0.5
        for c in range(f // FF_CHUNK):
            lo, hi = c * FF_CHUNK, (c + 1) * FF_CHUNK
            dact_ref[...] = _nt(half, wdn_ref[lo:hi, :])

            def slab(i, carry_):
                rows = pl.ds(pl.multiple_of(i * SLAB, SLAB), SLAB)
                a = ab_ref[rows, lo:hi].astype(F32)
                b = ab_ref[rows, f + lo:f + hi].astype(F32)
                s = jax.nn.sigmoid(a)
                ds_ = dact_ref[rows, :] * s
                dab_ref[rows, lo:hi] = (ds_ * b * (1.0 + a * (1.0 - s))).astype(BF16)
                dab_ref[rows, f + lo:f + hi] = (ds_ * a).astype(BF16)
                return carry_

            lax.fori_loop(0, tm // SLAB, slab, 0, unroll=True)

    return _launch(
        body, (dhb, ab, wdown), carry, name="ffn_bwd_hidden", grid=(t // tm,),
        in_specs=[_rows(tm, d), _rows(tm, 2 * f), _resident((f, d))], out_specs=[_rows(tm, 2 * f)],
        out_shape=[jax.ShapeDtypeStruct((t, 2 * f), BF16)],
        scratch_shapes=[pltpu.VMEM((tm, FF_CHUNK), F32)], semantics=("parallel",))


def _ffn_bwd_input(dab, dh, x, g, wup_t, tm, carry=None):
    t, d = x.shape
    f2 = wup_t.shape[0]

    def body(dab_ref, dh_ref, x_ref, g_ref, wup_ref, dx_ref, dg_ref):
        dn = _nn(dab_ref[...], wup_ref[...])
        xh, r = _rms_fwd(x_ref[...], g_ref[...])
        dx, dg = _rms_bwd(dn, xh, r, g_ref[...])
        dx_ref[...] = dh_ref[...] + dx

        @pl.when(pl.program_id(0) == 0)
        def _():
            dg_ref[...] = jnp.zeros_like(dg_ref)

        dg_ref[...] += dg

    return _launch(
        body, (dab, dh, x, g, wup_t), carry, name="ffn_bwd_input", grid=(t // tm,),
        in_specs=[_rows(tm, f2), _rows(tm, d), _rows(tm, d), _resident((1, d)), _resident((f2, d))],
        out_specs=[_rows(tm, d), pl.BlockSpec((1, d), lambda i: (0, 0))],
        out_shape=[jax.ShapeDtypeStruct((t, d), F32), jax.ShapeDtypeStruct((1, d), F32)],
        semantics=("arbitrary",))


def _wgrad(lhs, rhs, scale, bm, tk, name, carry=None):
    t, m = lhs.shape
    n = rhs.shape[1]
    steps = t // tk
    chunk = bm if bm <= 2048 else bm // 2

    def body(l_ref, r_ref, o_ref, acc_ref):
        @pl.when(pl.program_id(1) == 0)
        def _():
            acc_ref[...] = jnp.zeros_like(acc_ref)

        for lo in range(0, bm, chunk):
            acc_ref[lo:lo + chunk, :] += _tn(l_ref[:, lo:lo + chunk], r_ref[...])

        @pl.when(pl.program_id(1) == steps - 1)
        def _():
            o_ref[...] = (scale * acc_ref[...]).astype(o_ref.dtype)

    return _launch(
        body, (lhs, rhs), carry, name=name, grid=(m // bm, steps),
        in_specs=[pl.BlockSpec((tk, bm), lambda i, k: (k, i)), pl.BlockSpec((tk, n), lambda i, k: (k, 0))],
        out_specs=[pl.BlockSpec((bm, n), lambda i, k: (i, 0))],
        out_shape=[jax.ShapeDtypeStruct((m, n), WIRE)],
        scratch_shapes=[pltpu.VMEM((bm, n), F32)], semantics=("parallel", "arbitrary"))


def _mix_in_fwd(h, g, win_t, tm):
    t, d = h.shape

    def body(h_ref, g_ref, w_ref, u_ref, q_ref, kv_ref, z_ref, gate_ref):
        xh, _ = _rms_fwd(h_ref[...], g_ref[...])
        u = (xh * g_ref[...]).astype(BF16)
        u_ref[...] = u
        q_ref[...] = _nt(u, w_ref[0:OFF_KV, :]).astype(BF16)
        kv_ref[...] = _nt(u, w_ref[OFF_KV:OFF_Z, :]).astype(BF16)
        z_ref[...] = _nt(u, w_ref[OFF_Z:OFF_GATE, :])
        gate_ref[...] = _nt(u, w_ref[OFF_GATE:IN_WIDTH, :])

    return pl.pallas_call(
        body, name="mix_in_fwd", grid=(t // tm,),
        in_specs=[_rows(tm, d), _resident((1, d)), _resident((IN_WIDTH, d))],
        out_specs=[_rows(tm, d), _rows(tm, ATTN_WIDTH), _rows(tm, 2 * KV_WIDTH), _rows(tm, POOL_WIDTH),
                   _rows(tm, 2 * D_MODEL)],
        out_shape=[jax.ShapeDtypeStruct((t, d), BF16), jax.ShapeDtypeStruct((t, ATTN_WIDTH), BF16),
                   jax.ShapeDtypeStruct((t, 2 * KV_WIDTH), BF16), jax.ShapeDtypeStruct((t, POOL_WIDTH), F32),
                   jax.ShapeDtypeStruct((t, 2 * D_MODEL), F32)],
        compiler_params=_params("parallel"),
    )(h, g, win_t)


ALIBI_SLOPES = tuple(float(s) for s in (2.0 ** (-8.0 * np.arange(1, N_Q_HEADS + 1, dtype=np.float32) / N_Q_HEADS)))


def _attn_dist():
    return jnp.asarray(((np.arange(BLOCK)[:, None] - np.arange(BLOCK)[None, :]) % BLOCK).astype(np.float32))


def _own_block():
    shape = (BLOCK, BLOCK)
    return lax.broadcasted_iota(jnp.int32, shape, 1) <= lax.broadcasted_iota(jnp.int32, shape, 0)


def _fold(band2, own):
    return jnp.where(own, band2[:, BLOCK:], band2[:, :BLOCK])


def _unfold(x, own):
    zero = jnp.zeros_like(x)
    return jnp.concatenate([jnp.where(own, zero, x), jnp.where(own, x, zero)], axis=1)


def _low_half(shape):
    return lax.broadcasted_iota(jnp.int32, shape, len(shape) - 1) < HEAD_DIM


def _both_halves(band, kv_head):
    low = _low_half(band.shape)
    swapped = pltpu.roll(band, HEAD_DIM, 1)
    return jnp.where(low, band, swapped) if kv_head == 0 else jnp.where(low, swapped, band)


def _pair_rows(ref, pair, scale=None):
    v = ref[:, LANES * pair:LANES * (pair + 1)]
    if scale is not None:
        v = v * scale
    low, zero = _low_half(v.shape), jnp.zeros_like(v)
    return jnp.concatenate([jnp.where(low, v, zero), jnp.where(low, zero, v)], axis=0)


def _per_head(even, odd):
    return jnp.where(lax.broadcasted_iota(jnp.int32, (2 * BLOCK, 1), 0) < BLOCK, even, odd)


def _twice(x):
    return jnp.concatenate([x, x], axis=0)


def _pair_scores(q_ref, kk, dist2, pair, first, own2):
    s2 = _nt(_pair_rows(q_ref, pair, HEAD_DIM ** -0.5), kk)
    before = jnp.where(first, -jnp.inf, s2[:, :BLOCK])
    slopes = _per_head(ALIBI_SLOPES[2 * pair], ALIBI_SLOPES[2 * pair + 1])
    return jnp.where(own2, s2[:, BLOCK:], before) - slopes * dist2


def _heads_of(stack):
    return jnp.where(_low_half((BLOCK, LANES)), stack[:BLOCK], stack[BLOCK:])


def _softmax_sink(s, sink):
    m = jnp.maximum(jnp.max(s, axis=-1, keepdims=True), sink)
    p = jnp.exp(s - m)
    psink = jnp.exp(sink - m)
    inv = 1.0 / (jnp.sum(p, axis=-1, keepdims=True) + psink)
    return p * inv, psink * inv


def _bands(kvc_ref, kvp_ref):
    kband = jnp.concatenate([kvp_ref[:, 0:LANES], kvc_ref[:, 0:LANES]], axis=0)
    vband = jnp.concatenate([kvp_ref[:, LANES:2 * LANES], kvc_ref[:, LANES:2 * LANES]], axis=0)
    return ([_both_halves(kband, hk) for hk in range(N_KV_HEADS)],
            [_both_halves(vband, hk) for hk in range(N_KV_HEADS)])


SMEM = pl.BlockSpec(memory_space=pltpu.SMEM)
PAIRS = range(N_Q_HEADS // 2)
PAIRS_PER_KV = Q_PER_KV // 2


def _attn_fwd(q, kv, dist, sinks):
    t = q.shape[0]

    def body(q_ref, kvc_ref, kvp_ref, dist_ref, sink_ref, o_ref, s_scr, p_scr):
        first = pl.program_id(0) == 0
        own2 = _twice(_own_block())
        dist2 = _twice(dist_ref[...])
        kk, vv = _bands(kvc_ref, kvp_ref)
        for pair in PAIRS:
            s_scr[pair] = _pair_scores(q_ref, kk[pair // PAIRS_PER_KV], dist2, pair, first, own2)
        for pair in PAIRS:
            probs, _ = _softmax_sink(s_scr[pair], _per_head(sink_ref[2 * pair], sink_ref[2 * pair + 1]))
            p_scr[pair] = _unfold(probs.astype(BF16), own2)
        for pair in PAIRS:
            out = _nn(p_scr[pair], vv[pair // PAIRS_PER_KV])
            o_ref[:, LANES * pair:LANES * (pair + 1)] = _heads_of(out).astype(BF16)

    return pl.pallas_call(
        body, name="attn_fwd", grid=(t // BLOCK,),
        in_specs=[_rows(BLOCK, ATTN_WIDTH), _rows(BLOCK, 2 * KV_WIDTH),
                  pl.BlockSpec((BLOCK, 2 * KV_WIDTH), lambda i: (jnp.maximum(i - 1, 0), 0)),
                  _resident(dist.shape), SMEM],
        out_specs=_rows(BLOCK, ATTN_WIDTH),
        out_shape=jax.ShapeDtypeStruct((t, ATTN_WIDTH), BF16),
        scratch_shapes=[pltpu.VMEM((len(PAIRS), 2 * BLOCK, BLOCK), F32),
                        pltpu.VMEM((len(PAIRS), 2 * BLOCK, 2 * BLOCK), BF16)],
        compiler_params=_params("parallel"),
    )(q, kv, kv, dist, sinks)


def _pool_counts(tm, width):
    row = pl.program_id(0) * tm + lax.broadcasted_iota(jnp.int32, (tm, 1), 0)
    return jnp.minimum(row + 1, width).astype(F32)


def _trailing_sums(zz, group):
    s = zz
    for k in range(group + 1):
        s = s + pltpu.roll(s, 1 << k, 0)
    return s


def _leading_sums(zz, group):
    rows = zz.shape[0]
    s = zz
    for k in range(group + 1):
        s = s + pltpu.roll(s, rows - (1 << k), 0)
    return s


def _mix_out_fwd(attn, z, gate, h, wattn, wmix, scale, wpool_t, wout, tm, carry=None):
    t, d = h.shape

    def body(attn_ref, z_ref, halo_ref, gate_ref, h_ref, wattn_ref, wmix_ref, scale_ref, wpool_ref, wout_ref,
             h2_ref, a_ref, p_ref, merged_ref, ms_ref, pooled_ref):
        halo = jnp.where(pl.program_id(0) == 0, 0.0, halo_ref[...])
        for gi, width in enumerate(POOL_WINDOWS):
            lo, hi = gi * POOL_GROUP, (gi + 1) * POOL_GROUP
            zg = z_ref[:, lo:hi]
            sums = _trailing_sums(jnp.concatenate([halo[:, lo:hi], zg], axis=0), gi)[HALO:, :]
            pooled = (sums / _pool_counts(tm, width) - zg).astype(BF16)
            pooled_ref[:, lo:hi] = pooled
            ms_ref[:, lo:hi] = (_nn(pooled, wmix_ref[gi]) * scale_ref[:, lo:hi]).astype(BF16)
        p = _nt(ms_ref[...], wpool_ref[...])
        a = _nn(attn_ref[...], wattn_ref[...])
        a_ref[...] = a
        p_ref[...] = p
        merged = (jax.nn.sigmoid(gate_ref[:, 0:d]) * a + jax.nn.sigmoid(gate_ref[:, d:2 * d]) * p).astype(BF16)
        merged_ref[...] = merged
        h2_ref[...] = h_ref[...] + _nn(merged, wout_ref[...])

    halo_spec = pl.BlockSpec((HALO, POOL_WIDTH), lambda i: (jnp.maximum(i * (tm // HALO) - 1, 0), 0))
    return _launch(
        body, (attn, z, z, gate, h, wattn, wmix, scale, wpool_t, wout), carry, name="mix_out_fwd", grid=(t // tm,),
        in_specs=[_rows(tm, ATTN_WIDTH), _rows(tm, POOL_WIDTH), halo_spec, _rows(tm, 2 * d), _rows(tm, d),
                  _resident(wattn.shape), _resident(wmix.shape), _resident(scale.shape), _resident(wpool_t.shape),
                  _resident(wout.shape)],
        out_specs=[_rows(tm, d), _rows(tm, d), _rows(tm, d), _rows(tm, d), _rows(tm, POOL_WIDTH),
                   _rows(tm, POOL_WIDTH)],
        out_shape=[jax.ShapeDtypeStruct((t, d), F32), jax.ShapeDtypeStruct((t, d), F32),
                   jax.ShapeDtypeStruct((t, d), F32), jax.ShapeDtypeStruct((t, d), BF16),
                   jax.ShapeDtypeStruct((t, POOL_WIDTH), BF16), jax.ShapeDtypeStruct((t, POOL_WIDTH), BF16)],
        semantics=("parallel",))


def _mix_out_bwd(dh, gate, a, p, pooled, wattn, wmix, scale, wpool_t, wout, tm):
    t, d = dh.shape

    def body(dh_ref, gate_ref, a_ref, p_ref, pooled_ref, wattn_ref, wmix_ref, scale_ref, wpool_ref, wout_ref,
             dhb_ref, dab_ref, dpb_ref, dattn_ref, dgate_ref, dpooled_ref, dwmix_ref, dscale_ref):
        @pl.when(pl.program_id(0) == 0)
        def _():
            dwmix_ref[...] = jnp.zeros_like(dwmix_ref)
            dscale_ref[...] = jnp.zeros_like(dscale_ref)

        dhb = dh_ref[...].astype(BF16)
        dhb_ref[...] = dhb
        dm = _nt(dhb, wout_ref[...])
        sa = jax.nn.sigmoid(gate_ref[:, 0:d])
        sp = jax.nn.sigmoid(gate_ref[:, d:2 * d])
        da = (dm * sa).astype(BF16)
        dp = (dm * sp).astype(BF16)
        dab_ref[...] = da
        dpb_ref[...] = dp
        dgate_ref[:, 0:d] = (dm * a_ref[...] * (sa * (1.0 - sa))).astype(BF16)
        dgate_ref[:, d:2 * d] = (dm * p_ref[...] * (sp * (1.0 - sp))).astype(BF16)
        dattn_ref[...] = _nt(da, wattn_ref[...]).astype(BF16)
        dms = _nn(dp, wpool_ref[...])
        for gi in range(len(POOL_WINDOWS)):
            lo, hi = gi * POOL_GROUP, (gi + 1) * POOL_GROUP
            pooled_g = pooled_ref[:, lo:hi]
            mixed = _nn(pooled_g, wmix_ref[gi])
            dscale_ref[:, lo:hi] += jnp.sum(dms[:, lo:hi] * mixed, axis=0, keepdims=True)
            dmixed = (dms[:, lo:hi] * scale_ref[:, lo:hi]).astype(BF16)
            dwmix_ref[gi] += _tn(pooled_g, dmixed)
            dpooled_ref[:, lo:hi] = _nt(dmixed, wmix_ref[gi])

    acc = lambda shape: pl.BlockSpec(shape, lambda i: (0,) * len(shape))
    return pl.pallas_call(
        body, name="mix_out_bwd", grid=(t // tm,),
        in_specs=[_rows(tm, d), _rows(tm, 2 * d), _rows(tm, d), _rows(tm, d), _rows(tm, POOL_WIDTH),
                  _resident(wattn.shape), _resident(wmix.shape), _resident(scale.shape), _resident(wpool_t.shape),
                  _resident(wout.shape)],
        out_specs=[_rows(tm, d), _rows(tm, d), _rows(tm, d), _rows(tm, ATTN_WIDTH), _rows(tm, 2 * d),
                   _rows(tm, POOL_WIDTH), acc(wmix.shape), acc((1, POOL_WIDTH))],
        out_shape=[jax.ShapeDtypeStruct((t, d), BF16), jax.ShapeDtypeStruct((t, d), BF16),
                   jax.ShapeDtypeStruct((t, d), BF16), jax.ShapeDtypeStruct((t, ATTN_WIDTH), BF16),
                   jax.ShapeDtypeStruct((t, 2 * d), BF16), jax.ShapeDtypeStruct((t, POOL_WIDTH), F32),
                   jax.ShapeDtypeStruct(wmix.shape, F32), jax.ShapeDtypeStruct((1, POOL_WIDTH), F32)],
        compiler_params=_params("arbitrary"),
    )(dh, gate, a, p, pooled, wattn, wmix, scale, wpool_t, wout)


def _fold_halves(x):
    return x + pltpu.roll(x, HEAD_DIM, 1)


def _attn_bwd(q, kv, dattn, dist, sinks, carry=None):
    t = q.shape[0]

    def body(q_ref, kvc_ref, kvp_ref, do_ref, dist_ref, sink_ref, dq_ref, dkv_own_ref, dkv_prev_ref, dsink_ref,
             s_scr, dp_scr, p_scr, ds_scr):
        first = pl.program_id(0) == 0

        @pl.when(first)
        def _():
            dsink_ref[...] = jnp.zeros_like(dsink_ref)

        own2 = _twice(_own_block())
        dist2 = _twice(dist_ref[...])
        kk, vv = _bands(kvc_ref, kvp_ref)
        lane = lax.broadcasted_iota(jnp.int32, (1, LANES), 1)
        for pair in PAIRS:
            hk = pair // PAIRS_PER_KV
            s_scr[pair] = _pair_scores(q_ref, kk[hk], dist2, pair, first, own2)
            dp_scr[pair] = _fold(_nt(_pair_rows(do_ref, pair), vv[hk]), own2)
        dsink = jnp.zeros((1, LANES), F32)
        for pair in PAIRS:
            probs, psink = _softmax_sink(s_scr[pair], _per_head(sink_ref[2 * pair], sink_ref[2 * pair + 1]))
            dprobs = dp_scr[pair]
            rowdot = jnp.sum(probs * dprobs, axis=-1, keepdims=True)
            p_scr[pair] = _unfold(probs.astype(BF16), own2)
            ds_scr[pair] = _unfold((probs * (dprobs - rowdot)).astype(BF16), own2)
            dsink_rows = -psink * rowdot
            for half, head in enumerate((2 * pair, 2 * pair + 1)):
                head_sum = jnp.sum(dsink_rows[half * BLOCK:(half + 1) * BLOCK], axis=0, keepdims=True)
                dsink = dsink + jnp.where(lane == head, head_sum, 0.0)
        dk_heads, dv_heads = [], []
        top = lax.broadcasted_iota(jnp.int32, (LANES, BLOCK), 0) < HEAD_DIM

        def by_head(x_t):
            zero = jnp.zeros_like(x_t)
            return jnp.concatenate([jnp.where(top, x_t, zero), jnp.where(top, zero, x_t)], axis=1)

        for hk in range(N_KV_HEADS):
            dk_t = jnp.zeros((LANES, 2 * BLOCK), F32)
            dv_t = jnp.zeros((LANES, 2 * BLOCK), F32)
            for pair in range(hk * PAIRS_PER_KV, (hk + 1) * PAIRS_PER_KV):
                cols = slice(LANES * pair, LANES * (pair + 1))
                dv_t = dv_t + _nn(by_head(do_ref[:, cols].T), p_scr[pair])
                dk_t = dk_t + _nn(by_head((q_ref[:, cols] * HEAD_DIM ** -0.5).T), ds_scr[pair])
                dq_ref[:, cols] = (_heads_of(_nn(ds_scr[pair], kk[hk])) * HEAD_DIM ** -0.5).astype(BF16)
            dk_heads.append(_fold_halves(dk_t.T))
            dv_heads.append(_fold_halves(dv_t.T))
        low = _low_half(dk_heads[0].shape)
        dkv = jnp.concatenate([jnp.where(low, dk_heads[0], dk_heads[1]), jnp.where(low, dv_heads[0], dv_heads[1])],
                              axis=1)
        dkv_prev_ref[...] = dkv[0:BLOCK, :]
        dkv_own_ref[...] = dkv[BLOCK:2 * BLOCK, :]
        dsink_ref[...] += dsink

    return _launch(
        body, (q, kv, kv, dattn, dist, sinks), carry, name="attn_bwd", grid=(t // BLOCK,),
        in_specs=[_rows(BLOCK, ATTN_WIDTH), _rows(BLOCK, 2 * KV_WIDTH),
                  pl.BlockSpec((BLOCK, 2 * KV_WIDTH), lambda i: (jnp.maximum(i - 1, 0), 0)),
                  _rows(BLOCK, ATTN_WIDTH), _resident(dist.shape), SMEM],
        out_specs=[_rows(BLOCK, ATTN_WIDTH), _rows(BLOCK, 2 * KV_WIDTH), _rows(BLOCK, 2 * KV_WIDTH),
                   pl.BlockSpec((1, LANES), lambda i: (0, 0))],
        out_shape=[jax.ShapeDtypeStruct((t, ATTN_WIDTH), BF16), jax.ShapeDtypeStruct((t, 2 * KV_WIDTH), F32),
                   jax.ShapeDtypeStruct((t, 2 * KV_WIDTH), F32), jax.ShapeDtypeStruct((1, LANES), F32)],
        scratch_shapes=[pltpu.VMEM((len(PAIRS), 2 * BLOCK, BLOCK), F32), pltpu.VMEM((len(PAIRS), 2 * BLOCK, BLOCK), F32),
                        pltpu.VMEM((len(PAIRS), 2 * BLOCK, 2 * BLOCK), BF16),
                        pltpu.VMEM((len(PAIRS), 2 * BLOCK, 2 * BLOCK), BF16)],
        semantics=("arbitrary",))


def _mix_in_bwd(dq, dkv_own, dkv_prev, dpooled, dgate, h, g, win_t, dh_res, tm, carry=None):
    t, d = h.shape
    nt = t // tm

    def body(dq_ref, own_ref, prev_ref, prev_next_ref, dpool_ref, halo_ref, dgate_ref, h_ref, g_ref, w_ref, res_ref,
             dproj_ref, dh_ref, dhb_ref, dg_ref):
        i = pl.program_id(0)
        last = i == nt - 1
        dproj_ref[:, 0:OFF_KV] = dq_ref[...]
        from_next = jnp.where(last, 0.0, prev_next_ref[...])
        if tm > BLOCK:
            from_next = jnp.concatenate([prev_ref[BLOCK:tm, :], from_next], axis=0)
        dproj_ref[:, OFF_KV:OFF_Z] = (own_ref[...] + from_next).astype(BF16)
        halo = jnp.where(last, 0.0, halo_ref[...])
        for gi, width in enumerate(POOL_WINDOWS):
            lo, hi = gi * POOL_GROUP, (gi + 1) * POOL_GROUP
            dpg = dpool_ref[:, lo:hi]
            scaled = jnp.concatenate([dpg / _pool_counts(tm, width), halo[:, lo:hi] / float(width)], axis=0)
            dz = _leading_sums(scaled, gi)[0:tm, :] - dpg
            dproj_ref[:, OFF_Z + lo:OFF_Z + hi] = dz.astype(BF16)
        dproj_ref[:, OFF_GATE:IN_WIDTH] = dgate_ref[...]
        du = _nn(dproj_ref[...], w_ref[...])
        xh, r = _rms_fwd(h_ref[...], g_ref[...])
        dx, dg = _rms_bwd(du, xh, r, g_ref[...])
        dh = res_ref[...] + dx
        dh_ref[...] = dh
        dhb_ref[...] = dh.astype(BF16)

        @pl.when(i == 0)
        def _():
            dg_ref[...] = jnp.zeros_like(dg_ref)

        dg_ref[...] += dg

    per = tm // BLOCK
    next_block = pl.BlockSpec((BLOCK, 2 * KV_WIDTH), lambda i: (jnp.minimum((i + 1) * per, t // BLOCK - 1), 0))
    next_halo = pl.BlockSpec((HALO, POOL_WIDTH), lambda i: (jnp.minimum((i + 1) * (tm // HALO), t // HALO - 1), 0))
    return _launch(
        body, (dq, dkv_own, dkv_prev, dkv_prev, dpooled, dpooled, dgate, h, g, win_t, dh_res), carry,
        name="mix_in_bwd", grid=(nt,),
        in_specs=[_rows(tm, ATTN_WIDTH), _rows(tm, 2 * KV_WIDTH), _rows(tm, 2 * KV_WIDTH), next_block,
                  _rows(tm, POOL_WIDTH), next_halo, _rows(tm, 2 * d), _rows(tm, d), _resident((1, d)),
                  _resident((IN_WIDTH, d)), _rows(tm, d)],
        out_specs=[_rows(tm, IN_WIDTH), _rows(tm, d), _rows(tm, d), pl.BlockSpec((1, d), lambda i: (0, 0))],
        out_shape=[jax.ShapeDtypeStruct((t, IN_WIDTH), BF16), jax.ShapeDtypeStruct((t, d), F32),
                   jax.ShapeDtypeStruct((t, d), BF16), jax.ShapeDtypeStruct((1, d), F32)],
        semantics=("arbitrary",))


BIG = (("wup1_t", "ffn1_w_up", True), ("wdown1", "ffn1_w_down", False), ("win_t", "w_in", True),
       ("wattn", "w_attn_up", False), ("wpool_t", "w_pool_up", True), ("wout", "w_out", False),
       ("wup2_t", "ffn2_w_up", True), ("wdown2", "ffn2_w_down", False))
ANY = pl.BlockSpec(memory_space=pl.ANY)
WIRE = BF16


def _place():
    return lax.axis_index("x"), lax.axis_index("y"), lax.axis_index("c")


def _peer(k):
    x, y, c = _place()
    return x ^ (k >> 2), y ^ ((k >> 1) & 1), c ^ (k & 1)


def _index(px, py, pc):
    return 4 * px + 2 * py + pc


def _gather_carry(shards):
    n = len(shards)

    def tools(ins, outs, sems):
        send_sems, recv_sems, local_sems = sems
        x, y, c = _place()
        chips = [(1 - x, y), (x, 1 - y), (1 - x, 1 - y)]

        def rows(w, px, py, pc):
            r = ins[w].shape[0]
            return outs[w].at[pl.ds(_index(px, py, pc) * r, r), :]

        def copy(w, k, block, to, src=None):
            return pltpu.make_async_remote_copy(
                src_ref=rows(w, *block) if src is None else src, dst_ref=rows(w, *block),
                send_sem=send_sems.at[w, k], recv_sem=recv_sems.at[w, k], device_id=to, device_id_type=MESH)

        def own(w):
            return ([pltpu.make_async_copy(ins[w], rows(w, x, y, c), local_sems.at[w]),
                     copy(w, 0, (x, y, c), (x, y, 1 - c), src=ins[w])]
                    + [copy(w, 1 + j, (x, y, c), (*chip, c), src=ins[w]) for j, chip in enumerate(chips)])

        def passed(w, j):
            return copy(w, 4 + j, (*chips[j], c), (x, y, 1 - c))

        return (x, y, c), chips, copy, own, passed

    def start(ins, outs, sems):
        _, _, _, own, _ = tools(ins, outs, sems)
        for w in range(n):
            for cp in own(w):
                cp.start()

    def forward(w):
        def run(ins, outs, sems):
            (x, y, c), chips, copy, _, passed = tools(ins, outs, sems)
            for j, chip in enumerate(chips):
                copy(w, 1 + j, (*chip, c), (x, y, c)).wait_recv()
                passed(w, j).start()
        return run

    sizes = np.cumsum([s.size for s in shards]) / sum(s.size for s in shards)
    middles = [(float(sizes[w]), forward(w)) for w in range(n)]

    def finish(ins, outs, sems):
        (x, y, c), chips, copy, own, passed = tools(ins, outs, sems)
        for w in range(n):
            copy(w, 0, (x, y, 1 - c), (x, y, c)).wait_recv()
            for j, chip in enumerate(chips):
                copy(w, 4 + j, (*chip, 1 - c), (x, y, c)).wait_recv()
        for w in range(n):
            mine, *sent = own(w)
            for cp in sent + [passed(w, j) for j in range(len(chips))]:
                cp.wait_send()
            mine.wait()

    return _Carry(
        shards, [jax.ShapeDtypeStruct((N_DEV * s.shape[0], s.shape[1]), s.dtype) for s in shards],
        [pltpu.SemaphoreType.DMA((n, N_DEV - 1)), pltpu.SemaphoreType.DMA((n, N_DEV - 1)),
         pltpu.SemaphoreType.DMA((n,))], start, finish, middles)


def _scatter_carry(grads):
    n = len(grads)

    def tools(ins, outs, sems):
        send_sems, recv_sems, local_sems = sems
        me = _index(*_place())

        def block(ref, dev):
            r = ref.shape[0] // N_DEV
            return ref.at[pl.ds(dev * r, r), :]

        def copy(w, k, landing):
            to = _peer(k)
            return pltpu.make_async_remote_copy(
                src_ref=block(ins[w], _index(*to)), dst_ref=block(outs[w], landing), send_sem=send_sems.at[w, k - 1],
                recv_sem=recv_sems.at[w, k - 1], device_id=to, device_id_type=MESH)

        def mine(w):
            return pltpu.make_async_copy(block(ins[w], me), block(outs[w], me), local_sems.at[w])

        return me, copy, mine

    def start(ins, outs, sems):
        me, copy, mine = tools(ins, outs, sems)
        for w in range(n):
            mine(w).start()
            for k in range(1, N_DEV):
                copy(w, k, me).start()

    def finish(ins, outs, sems):
        _, copy, mine = tools(ins, outs, sems)
        for w in range(n):
            for k in range(1, N_DEV):
                copy(w, k, _index(*_peer(k))).wait()
            mine(w).wait()

    return _Carry(
        grads, [jax.ShapeDtypeStruct(g.shape, g.dtype) for g in grads],
        [pltpu.SemaphoreType.DMA((n, N_DEV - 1)), pltpu.SemaphoreType.DMA((n, N_DEV - 1)),
         pltpu.SemaphoreType.DMA((n,))], start, finish)


def _small_carry(small):
    srows = small.shape[0]

    def tools(ins, outs, sems):
        send_sems, recv_sems, local_sem = sems
        me = _index(*_place())

        def slot(dev):
            return outs[0].at[pl.ds(dev * srows, srows), :]

        def copy(k, landing):
            return pltpu.make_async_remote_copy(
                src_ref=ins[0], dst_ref=slot(landing), send_sem=send_sems.at[k - 1], recv_sem=recv_sems.at[k - 1],
                device_id=_peer(k), device_id_type=MESH)

        return me, copy, pltpu.make_async_copy(ins[0], slot(me), local_sem)

    def start(ins, outs, sems):
        me, copy, mine = tools(ins, outs, sems)
        mine.start()
        for k in range(1, N_DEV):
            copy(k, me).start()

    def finish(ins, outs, sems):
        _, copy, mine = tools(ins, outs, sems)
        for k in range(1, N_DEV):
            copy(k, _index(*_peer(k))).wait()
        mine.wait()

    return _Carry([small], [jax.ShapeDtypeStruct((N_DEV * srows, LANES), small.dtype)],
                  [pltpu.SemaphoreType.DMA((N_DEV - 1,)), pltpu.SemaphoreType.DMA((N_DEV - 1,)),
                   pltpu.SemaphoreType.DMA], start, finish)


def _exchange(carry, name):
    ci = len(carry.inputs)
    co = len(carry.out_shape)

    def body(*refs):
        parts = refs[:ci], refs[ci:ci + co], refs[ci + co:]
        carry.start(*parts)
        for _, fn in carry.middles:
            fn(*parts)
        carry.finish(*parts)

    return list(pl.pallas_call(body, name=name, in_specs=[ANY] * ci, out_specs=[ANY] * co, out_shape=carry.out_shape,
                               scratch_shapes=carry.scratch)(*carry.inputs))


def _adamw_math(w, g, m, v):
    m = ADAM_B1 * m + (1.0 - ADAM_B1) * g
    v = ADAM_B2 * v + (1.0 - ADAM_B2) * (g * g)
    m_hat = m / (1.0 - ADAM_B1 ** ADAM_STEP)
    v_hat = v / (1.0 - ADAM_B2 ** ADAM_STEP)
    return -ADAM_LR * (m_hat / (jnp.sqrt(v_hat) + ADAM_EPS) + ADAM_WD * w), m, v


def _sum_adamw(got, w, m, v, transposed, name):
    parts = list(got) if isinstance(got, (list, tuple)) else [got]
    r = parts[0].shape[0] // N_DEV
    cols = sum(part.shape[1] for part in parts)
    if transposed:
        (only,) = parts
        tile = cols if cols <= 512 else 256
        got_specs = [pl.BlockSpec((N_DEV, r, tile), lambda i: (0, 0, i))]
        spec, steps = pl.BlockSpec((tile, r), lambda i: (i, 0)), cols // tile
    else:
        tile = r if r <= 256 else r // 2
        got_specs = [pl.BlockSpec((N_DEV, tile, part.shape[1]), lambda i: (0, i, 0)) for part in parts]
        spec, steps = pl.BlockSpec((tile, cols), lambda i: (i, 0)), r // tile
    n = len(parts)

    def body(*refs):
        w_ref, m_ref, v_ref, g_ref, d_ref, m2_ref, v2_ref = refs[n:]
        sums = []
        for got_ref in refs[:n]:
            acc = got_ref[0].astype(F32)
            for dev in range(1, N_DEV):
                acc = acc + got_ref[dev].astype(F32)
            sums.append(acc)
        g = sums[0].T if transposed else (sums[0] if n == 1 else jnp.concatenate(sums, axis=1))
        g_ref[...] = g
        d_ref[...], m2_ref[...], v2_ref[...] = _adamw_math(w_ref[...], g, m_ref[...], v_ref[...])

    return pl.pallas_call(
        body, name=name, grid=(steps,), in_specs=got_specs + [spec, spec, spec], out_specs=[spec] * 4,
        out_shape=[jax.ShapeDtypeStruct(w.shape, F32)] * 4, compiler_params=_params("parallel"),
    )(*[part.reshape(N_DEV, r, part.shape[1]) for part in parts], w, m, v)


def _small_update(gathered, w, m, v):
    rows = w.shape[0]

    def body(all_ref, w_ref, m_ref, v_ref, g_ref, d_ref, m2_ref, v2_ref):
        g = all_ref[0]
        for dev in range(1, N_DEV):
            g = g + all_ref[dev]
        g_ref[...] = g
        d_ref[...], m2_ref[...], v2_ref[...] = _adamw_math(w_ref[...], g, m_ref[...], v_ref[...])

    return pl.pallas_call(
        body, name="small_update", out_shape=[jax.ShapeDtypeStruct((rows, LANES), F32)] * 4,
        compiler_params=pltpu.CompilerParams(vmem_limit_bytes=VMEM_LIMIT),
    )(gathered.reshape(N_DEV, rows, LANES), w, m, v)


SMALL = (("pool_w_mix", 512), ("ffn1_norm", 8), ("mix_norm", 8), ("ffn2_norm", 8), ("final_norm", 8),
         ("pool_scale", 8), ("sinks", 8), ("loss", 8))
SMALL_ROWS = sum(rows for _, rows in SMALL)


def _pack_small(parts):
    out = []
    for name, rows in SMALL:
        flat = parts[name].astype(F32).reshape(-1)
        out.append(jnp.pad(flat, (0, rows * LANES - flat.shape[0])).reshape(rows, LANES))
    return jnp.concatenate(out, axis=0)


def _unpack_small(packed, shapes):
    out, row = {}, 0
    for name, rows in SMALL:
        shape = shapes[name]
        size = int(np.prod(shape)) if shape else 1
        out[name] = packed[row:row + rows].reshape(-1)[:size].reshape(shape)
        row += rows
    return out


def kernel(x, ffn1_norm, ffn1_w_up, ffn1_w_down, mix_norm, w_in, sinks, w_attn_up, pool_w_mix, pool_scale, w_pool_up, w_out, ffn2_norm, ffn2_w_up, ffn2_w_down, final_norm, loss_target, m_ffn1_norm, m_ffn1_w_up, m_ffn1_w_down, m_mix_norm, m_w_in, m_sinks, m_w_attn_up, m_pool_w_mix, m_pool_scale, m_w_pool_up, m_w_out, m_ffn2_norm, m_ffn2_w_up, m_ffn2_w_down, m_final_norm, v_ffn1_norm, v_ffn1_w_up, v_ffn1_w_down, v_mix_norm, v_w_in, v_sinks, v_w_attn_up, v_pool_w_mix, v_pool_scale, v_w_pool_up, v_w_out, v_ffn2_norm, v_ffn2_w_up, v_ffn2_w_down, v_final_norm):
    args = dict(locals())
    weight_names = ("ffn1_norm", "ffn1_w_up", "ffn1_w_down", "mix_norm", "w_in", "sinks", "w_attn_up", "pool_w_mix",
                    "pool_scale", "w_pool_up", "w_out", "ffn2_norm", "ffn2_w_up", "ffn2_w_down", "final_norm")

    shard = {k: (args[p][0].T if tr else args[p][0]).astype(BF16) for k, p, tr in BIG}
    early, late = ["wdown1", "win_t", "wattn", "wpool_t", "wout"], ["wup2_t", "wdown2"]
    big = {"wup1_t": _exchange(_gather_carry([shard["wup1_t"]]), "gather_up1")[0]}

    xs, target = x[0], loss_target[0]
    t = xs.shape[0]
    tm_f, tm_b, tk = min(512, t), min(512, t), min(1024, t)
    g1, gm, g2, gf = ffn1_norm, mix_norm, ffn2_norm, final_norm.reshape(1, D_MODEL)
    dist = _attn_dist()
    sink_v = sinks.reshape(N_Q_HEADS)
    wmix_b = pool_w_mix[0].astype(BF16)

    (n1, ab1, act1), rest = _ffn_up(xs, g1, big["wup1_t"], tm_f, _gather_carry([shard[k] for k in early]))
    big.update(zip(early, rest))
    h1 = _ffn_down(xs, act1, big["wdown1"], tm_f)
    u, q, kv, z, gate = _mix_in_fwd(h1, gm, big["win_t"], tm_f)
    attn = _attn_fwd(q, kv, dist, sink_v)
    (h2, a, p, merged, ms, pooled), rest = _mix_out_fwd(
        attn, z, gate, h1, big["wattn"], wmix_b, pool_scale, big["wpool_t"], big["wout"], tm_b,
        _gather_carry([shard[k] for k in late]))
    big.update(zip(late, rest))
    ab2, n2, act2, loss_lanes, dh3, dhb3, dgf = _ffn_loss(h2, g2, big["wup2_t"], big["wdown2"], gf, target, tm_f)

    got = {}
    (gw_down2,), _ = _wgrad(act2, dhb3, 0.5, D_FF, tk, "wgrad_down2")
    (dab2,), (got["wdown2"],) = _ffn_bwd_hidden(dhb3, ab2, big["wdown2"], tm_f, _scatter_carry([gw_down2]))
    (dh2, dg2), _ = _ffn_bwd_input(dab2, dh3, h2, g2, big["wup2_t"], tm_f)
    (gw_up2,), _ = _wgrad(dab2, n2, 1.0, D_FF, tk, "wgrad_up2")
    dhb2, da_b, dp_b, dattn, dgate, dpooled, dwmix, dscale = _mix_out_bwd(
        dh2, gate, a, p, pooled, big["wattn"], wmix_b, pool_scale, big["wpool_t"], big["wout"], tm_b)
    (gw_out,), _ = _wgrad(merged, dhb2, 1.0, D_MODEL, tk, "wgrad_out")
    (gw_attn,), _ = _wgrad(attn, da_b, 1.0, D_MODEL, tk, "wgrad_attn")
    (gw_pool,), _ = _wgrad(dp_b, ms, 1.0, D_MODEL, tk, "wgrad_pool")
    (dq, dkv_own, dkv_prev, dsinks), (got["wup2_t"],) = _attn_bwd(q, kv, dattn, dist, sink_v, _scatter_carry([gw_up2]))
    (dproj, dh1, dhb1, dgm), (got["wout"], got["wattn"], got["wpool_t"]) = _mix_in_bwd(
        dq, dkv_own, dkv_prev, dpooled, dgate, h1, gm, big["win_t"], dh2, tm_b,
        _scatter_carry([gw_out, gw_attn, gw_pool]))
    (gw_down1,), _ = _wgrad(act1, dhb1, 0.5, D_FF, tk, "wgrad_down1")
    (gw_in,), (got["wdown1"],) = _wgrad(dproj, u, 1.0, IN_WIDTH // 2, tk, "wgrad_in", _scatter_carry([gw_down1]))
    (dab1,), (got["win_t"],) = _ffn_bwd_hidden(dhb1, ab1, big["wdown1"], tm_f, _scatter_carry([gw_in]))
    (gw_up1,), _ = _wgrad(dab1, n1, 1.0, D_FF, tk, "wgrad_up1")
    (dx, dg1), (got["wup1_t"],) = _ffn_bwd_input(dab1, dh1, xs, g1, big["wup1_t"], tm_f, _scatter_carry([gw_up1]))
    small_parts = {"pool_w_mix": dwmix, "ffn1_norm": dg1, "mix_norm": dgm, "ffn2_norm": dg2, "final_norm": dgf,
                   "pool_scale": dscale, "sinks": dsinks[:, :N_Q_HEADS], "loss": loss_lanes[:, :1]}
    (small_all,) = _exchange(_small_carry(_pack_small(small_parts)), "gather_small")

    grad, delta, new_m, new_v = {}, {}, {}, {}
    for k, p, tr in BIG:
        outside = tr and args[p].shape[-1] % LANES != 0
        turn = (lambda a: a.T) if outside else (lambda a: a)
        res = _sum_adamw(got[k], turn(args[p][0]), turn(args["m_" + p][0]), turn(args["v_" + p][0]),
                         tr and not outside, "adamw_" + k)
        grad[p], delta[p], new_m[p], new_v[p] = (turn(a)[None] for a in res)

    shapes = {name: args[name].shape for name, _ in SMALL if name != "loss"}
    shapes["loss"] = ()
    packed = {pre: _pack_small({**{name: args[pre + name] for name, _ in SMALL if name != "loss"},
                                "loss": jnp.zeros((), F32)}) for pre in ("", "m_", "v_")}
    g_s, d_s, m_s, v_s = _small_update(small_all, packed[""], packed["m_"], packed["v_"])
    g_small, d_small, m_small, v_small = (_unpack_small(a, shapes) for a in (g_s, d_s, m_s, v_s))
    for name, _ in SMALL:
        if name != "loss":
            grad[name], delta[name], new_m[name], new_v[name] = (
                g_small[name], d_small[name], m_small[name], v_small[name])

    return (g_small["loss"], dx[None], *[grad[n] for n in weight_names], *[delta[n] for n in weight_names],
            *[new_m[n] for n in weight_names], *[new_v[n] for n in weight_names])
```

```python
import functools

import jax
import jax.numpy as jnp
import numpy as np
from jax import lax
from jax.experimental import pallas as pl
from jax.experimental.pallas import tpu as pltpu

F32 = jnp.float32
BF16 = jnp.bfloat16

D_MODEL = 1024
D_FF = 2816
N_Q_HEADS = 16
N_KV_HEADS = 2
Q_PER_KV = N_Q_HEADS // N_KV_HEADS
HEAD_DIM = 64
BLOCK = 128
ATTN_WIDTH = N_Q_HEADS * HEAD_DIM
KV_WIDTH = N_KV_HEADS * HEAD_DIM
POOL_WINDOWS = (2, 4, 8, 16)
POOL_GROUP = 128
POOL_WIDTH = 512
HALO = 16
IN_WIDTH = ATTN_WIDTH + 2 * KV_WIDTH + POOL_WIDTH + 2 * D_MODEL
OFF_KV = ATTN_WIDTH
OFF_Z = ATTN_WIDTH + 2 * KV_WIDTH
OFF_GATE = OFF_Z + POOL_WIDTH
NORM_EPS = 1e-6
ADAM_LR = 0.001
ADAM_B1 = 0.9
ADAM_B2 = 0.999
ADAM_EPS = 1e-08
ADAM_WD = 0.01
ADAM_STEP = 10

N_DEV = 8
N_CHIP = 4
LANES = 128
FF_CHUNK = 256
SLAB = 32
VMEM_LIMIT = 56 * 1024 * 1024
MESH = pl.DeviceIdType.MESH


def _nn(a, b):
    return jnp.dot(a, b, preferred_element_type=F32)


def _nt(a, b):
    return lax.dot_general(a, b, (((1,), (1,)), ((), ())), preferred_element_type=F32)


def _tn(a, b):
    return lax.dot_general(a, b, (((0,), (0,)), ((), ())), preferred_element_type=F32)


def _params(*sem):
    return pltpu.CompilerParams(dimension_semantics=sem, vmem_limit_bytes=VMEM_LIMIT)


def _resident(shape):
    return pl.BlockSpec(shape, lambda *_: (0,) * len(shape), pipeline_mode=pl.Buffered(1))


def _rows(tm, cols):
    return pl.BlockSpec((tm, cols), lambda i: (i, 0))


class _Carry:
    def __init__(self, inputs, out_shape, scratch, start, finish, middles=()):
        self.inputs, self.out_shape, self.scratch = list(inputs), list(out_shape), list(scratch)
        self.start, self.finish, self.middles = start, finish, list(middles)


def _launch(body, args, carry=None, *, name, grid, in_specs, out_specs, out_shape, scratch_shapes=(), semantics):
    in_specs, out_specs, out_shape, scratch_shapes = list(in_specs), list(out_specs), list(out_shape), list(scratch_shapes)
    if carry is None:
        res = pl.pallas_call(body, name=name, grid=grid, in_specs=in_specs, out_specs=out_specs, out_shape=out_shape,
                             scratch_shapes=scratch_shapes, compiler_params=_params(*semantics))(*args)
        return list(res), []
    ni, no, ns = len(in_specs), len(out_specs), len(scratch_shapes)
    ci, co = len(carry.inputs), len(carry.out_shape)
    total = int(np.prod(grid))

    def full(*refs):
        own_in, c_in = refs[:ni], refs[ni:ni + ci]
        own_out, c_out = refs[ni + ci:ni + ci + no], refs[ni + ci + no:ni + ci + no + co]
        own_scr, c_sem = refs[ni + ci + no + co:ni + ci + no + co + ns], refs[ni + ci + no + co + ns:]
        step = 0
        for axis, size in enumerate(grid):
            step = step * size + pl.program_id(axis)
        pl.when(step == 0)(lambda: carry.start(c_in, c_out, c_sem))
        for fraction, fn in carry.middles:
            at = min(total - 1, int(fraction * total) + 1)
            pl.when(step == at)(lambda fn=fn: fn(c_in, c_out, c_sem))
        body(*own_in, *own_out, *own_scr)
        pl.when(step == total - 1)(lambda: carry.finish(c_in, c_out, c_sem))

    res = pl.pallas_call(
        full, name=name, grid=grid, in_specs=in_specs + [ANY] * ci, out_specs=out_specs + [ANY] * co,
        out_shape=out_shape + carry.out_shape, scratch_shapes=scratch_shapes + carry.scratch,
        compiler_params=_params(*(["arbitrary"] * len(grid))),
    )(*args, *carry.inputs)
    return list(res[:no]), list(res[no:])


def _rms_fwd(xv, g):
    r = lax.rsqrt(jnp.mean(xv * xv, axis=-1, keepdims=True) + NORM_EPS)
    return xv * r, r


def _rms_bwd(dn, xh, r, g):
    dxh = dn * g
    dx = r * (dxh - xh * jnp.mean(dxh * xh, axis=-1, keepdims=True))
    return dx, jnp.sum(dn * xh, axis=0, keepdims=True)


def _ffn_loss(x, g, wup_t, wdown, gf, target, tm):
    t, d = x.shape
    f = wdown.shape[0]

    def body(x_ref, g_ref, wup_ref, wdn_ref, gf_ref, tgt_ref, ab_ref, n_ref, act_ref, loss_ref, dh_ref, dhb_ref, dg_ref):
        xv = x_ref[...]
        xh, _ = _rms_fwd(xv, g_ref[...])
        n = (xh * g_ref[...]).astype(BF16)
        n_ref[...] = n
        for c in range(f // FF_CHUNK):
            lo, hi = c * FF_CHUNK, (c + 1) * FF_CHUNK
            a = _nt(n, wup_ref[lo:hi, :])
            b = _nt(n, wup_ref[f + lo:f + hi, :])
            ab_ref[:, lo:hi] = a.astype(BF16)
            ab_ref[:, f + lo:f + hi] = b.astype(BF16)
            act_ref[:, lo:hi] = (a * jax.nn.sigmoid(a) * b).astype(BF16)
        h = xv + 0.5 * _nn(act_ref[...], wdn_ref[...])
        yh, r = _rms_fwd(h, gf_ref[...])
        err = yh * gf_ref[...] - tgt_ref[...]
        part = 0.5 * jnp.sum(jnp.mean(err * err, axis=-1, keepdims=True), axis=0, keepdims=True)
        dh, dg = _rms_bwd(err * (1.0 / d), yh, r, gf_ref[...])
        dh_ref[...] = dh
        dhb_ref[...] = dh.astype(BF16)

        @pl.when(pl.program_id(0) == 0)
        def _():
            dg_ref[...] = jnp.zeros_like(dg_ref)
            loss_ref[...] = jnp.zeros_like(loss_ref)

        dg_ref[...] += dg
        loss_ref[...] += jnp.broadcast_to(part, loss_ref.shape)

    return pl.pallas_call(
        body, name="ffn_loss", grid=(t // tm,),
        in_specs=[_rows(tm, d), _resident((1, d)), _resident((2 * f, d)), _resident((f, d)), _resident((1, d)),
                  _rows(tm, d)],
        out_specs=[_rows(tm, 2 * f), _rows(tm, d), _rows(tm, f), pl.BlockSpec((1, LANES), lambda i: (0, 0)),
                   _rows(tm, d), _rows(tm, d), pl.BlockSpec((1, d), lambda i: (0, 0))],
        out_shape=[jax.ShapeDtypeStruct((t, 2 * f), BF16), jax.ShapeDtypeStruct((t, d), BF16),
                   jax.ShapeDtypeStruct((t, f), BF16), jax.ShapeDtypeStruct((1, LANES), F32),
                   jax.ShapeDtypeStruct((t, d), F32), jax.ShapeDtypeStruct((t, d), BF16),
                   jax.ShapeDtypeStruct((1, d), F32)],
        compiler_params=_params("arbitrary"),
    )(x, g, wup_t, wdown, gf, target)


def _ffn_up(x, g, wup_t, tm, carry=None):
    t, d = x.shape
    f = wup_t.shape[0] // 2

    def body(x_ref, g_ref, wup_ref, n_ref, ab_ref, act_ref):
        xh, _ = _rms_fwd(x_ref[...], g_ref[...])
        n = (xh * g_ref[...]).astype(BF16)
        n_ref[...] = n
        for c in range(f // FF_CHUNK):
            lo, hi = c * FF_CHUNK, (c + 1) * FF_CHUNK
            a = _nt(n, wup_ref[lo:hi, :])
            b = _nt(n, wup_ref[f + lo:f + hi, :])
            ab_ref[:, lo:hi] = a.astype(BF16)
            ab_ref[:, f + lo:f + hi] = b.astype(BF16)
            act_ref[:, lo:hi] = (a * jax.nn.sigmoid(a) * b).astype(BF16)

    return _launch(
        body, (x, g, wup_t), carry, name="ffn_up", grid=(t // tm,),
        in_specs=[_rows(tm, d), _resident((1, d)), _resident((2 * f, d))],
        out_specs=[_rows(tm, d), _rows(tm, 2 * f), _rows(tm, f)],
        out_shape=[jax.ShapeDtypeStruct((t, d), BF16), jax.ShapeDtypeStruct((t, 2 * f), BF16),
                   jax.ShapeDtypeStruct((t, f), BF16)],
        semantics=("parallel",))


def _ffn_down(x, act, wdown, tm):
    t, d = x.shape
    f = wdown.shape[0]

    def body(x_ref, act_ref, wdn_ref, h_ref):
        h_ref[...] = x_ref[...] + 0.5 * _nn(act_ref[...], wdn_ref[...])

    return pl.pallas_call(
        body, name="ffn_down", grid=(t // tm,),
        in_specs=[_rows(tm, d), _rows(tm, f), _resident((f, d))], out_specs=_rows(tm, d),
        out_shape=jax.ShapeDtypeStruct((t, d), F32), compiler_params=_params("parallel"),
    )(x, act, wdown)


def _ffn_bwd_hidden(dhb, ab, wdown, tm, carry=None):
    t, d = dhb.shape
    f = wdown.shape[0]

    def body(dh_ref, ab_ref, wdn_ref, dab_ref, dact_ref):
        half = dh_ref[...] * 0.5
        for c in range(f // FF_CHUNK):
            lo, hi = c * FF_CHUNK, (c + 1) * FF_CHUNK
            dact_ref[...] = _nt(half, wdn_ref[lo:hi, :])

            def slab(i, carry_):
                rows = pl.ds(pl.multiple_of(i * SLAB, SLAB), SLAB)
                a = ab_ref[rows, lo:hi].astype(F32)
                b = ab_ref[rows, f + lo:f + hi].astype(F32)
                s = jax.nn.sigmoid(a)
                ds_ = dact_ref[rows, :] * s
                dab_ref[rows, lo:hi] = (ds_ * b * (1.0 + a * (1.0 - s))).astype(BF16)
                dab_ref[rows, f + lo:f + hi] = (ds_ * a).astype(BF16)
                return carry_

            lax.fori_loop(0, tm // SLAB, slab, 0, unroll=True)

    return _launch(
        body, (dhb, ab, wdown), carry, name="ffn_bwd_hidden", grid=(t // tm,),
        in_specs=[_rows(tm, d), _rows(tm, 2 * f), _resident((f, d))], out_specs=[_rows(tm, 2 * f)],
        out_shape=[jax.ShapeDtypeStruct((t, 2 * f), BF16)],
        scratch_shapes=[pltpu.VMEM((tm, FF_CHUNK), F32)], semantics=("parallel",))


def _ffn_bwd_input(dab, dh, x, g, wup_t, tm, carry=None):
    t, d = x.shape
    f2 = wup_t.shape[0]

    def body(dab_ref, dh_ref, x_ref, g_ref, wup_ref, dx_ref, dg_ref):
        dn = _nn(dab_ref[...], wup_ref[...])
        xh, r = _rms_fwd(x_ref[...], g_ref[...])
        dx, dg = _rms_bwd(dn, xh, r, g_ref[...])
        dx_ref[...] = dh_ref[...] + dx

        @pl.when(pl.program_id(0) == 0)
        def _():
            dg_ref[...] = jnp.zeros_like(dg_ref)

        dg_ref[...] += dg

    return _launch(
        body, (dab, dh, x, g, wup_t), carry, name="ffn_bwd_input", grid=(t // tm,),
        in_specs=[_rows(tm, f2), _rows(tm, d), _rows(tm, d), _resident((1, d)), _resident((f2, d))],
        out_specs=[_rows(tm, d), pl.BlockSpec((1, d), lambda i: (0, 0))],
        out_shape=[jax.ShapeDtypeStruct((t, d), F32), jax.ShapeDtypeStruct((1, d), F32)],
        semantics=("arbitrary",))


def _wgrad(lhs, rhs, scale, bm, tk, name, carry=None):
    t, m = lhs.shape
    n = rhs.shape[1]
    steps = t // tk
    chunk = bm if bm <= 2048 else bm // 2

    def body(l_ref, r_ref, o_ref, acc_ref):
        @pl.when(pl.program_id(1) == 0)
        def _():
            acc_ref[...] = jnp.zeros_like(acc_ref)

        for lo in range(0, bm, chunk):
            acc_ref[lo:lo + chunk, :] += _tn(l_ref[:, lo:lo + chunk], r_ref[...])

        @pl.when(pl.program_id(1) == steps - 1)
        def _():
            o_ref[...] = (scale * acc_ref[...]).astype(o_ref.dtype)

    return _launch(
        body, (lhs, rhs), carry, name=name, grid=(m // bm, steps),
        in_specs=[pl.BlockSpec((tk, bm), lambda i, k: (k, i)), pl.BlockSpec((tk, n), lambda i, k: (k, 0))],
        out_specs=[pl.BlockSpec((bm, n), lambda i, k: (i, 0))],
        out_shape=[jax.ShapeDtypeStruct((m, n), WIRE)],
        scratch_shapes=[pltpu.VMEM((bm, n), F32)], semantics=("parallel", "arbitrary"))


def _mix_in_fwd(h, g, win_t, tm):
    t, d = h.shape

    def body(h_ref, g_ref, w_ref, u_ref, q_ref, kv_ref, z_ref, gate_ref):
        xh, _ = _rms_fwd(h_ref[...], g_ref[...])
        u = (xh * g_ref[...]).astype(BF16)
        u_ref[...] = u
        q_ref[...] = _nt(u, w_ref[0:OFF_KV, :]).astype(BF16)
        kv_ref[...] = _nt(u, w_ref[OFF_KV:OFF_Z, :]).astype(BF16)
        z_ref[...] = _nt(u, w_ref[OFF_Z:OFF_GATE, :])
        gate_ref[...] = _nt(u, w_ref[OFF_GATE:IN_WIDTH, :])

    return pl.pallas_call(
        body, name="mix_in_fwd", grid=(t // tm,),
        in_specs=[_rows(tm, d), _resident((1, d)), _resident((IN_WIDTH, d))],
        out_specs=[_rows(tm, d), _rows(tm, ATTN_WIDTH), _rows(tm, 2 * KV_WIDTH), _rows(tm, POOL_WIDTH),
                   _rows(tm, 2 * D_MODEL)],
        out_shape=[jax.ShapeDtypeStruct((t, d), BF16), jax.ShapeDtypeStruct((t, ATTN_WIDTH), BF16),
                   jax.ShapeDtypeStruct((t, 2 * KV_WIDTH), BF16), jax.ShapeDtypeStruct((t, POOL_WIDTH), F32),
                   jax.ShapeDtypeStruct((t, 2 * D_MODEL), F32)],
        compiler_params=_params("parallel"),
    )(h, g, win_t)


ALIBI_SLOPES = tuple(float(s) for s in (2.0 ** (-8.0 * np.arange(1, N_Q_HEADS + 1, dtype=np.float32) / N_Q_HEADS)))


def _attn_dist():
    return jnp.asarray(((np.arange(BLOCK)[:, None] - np.arange(BLOCK)[None, :]) % BLOCK).astype(np.float32))


def _own_block():
    shape = (BLOCK, BLOCK)
    return lax.broadcasted_iota(jnp.int32, shape, 1) <= lax.broadcasted_iota(jnp.int32, shape, 0)


def _fold(band2, own):
    return jnp.where(own, band2[:, BLOCK:], band2[:, :BLOCK])


def _unfold(x, own):
    zero = jnp.zeros_like(x)
    return jnp.concatenate([jnp.where(own, zero, x), jnp.where(own, x, zero)], axis=1)


def _low_half(shape):
    return lax.broadcasted_iota(jnp.int32, shape, len(shape) - 1) < HEAD_DIM


def _both_halves(band, kv_head):
    low = _low_half(band.shape)
    swapped = pltpu.roll(band, HEAD_DIM, 1)
    return jnp.where(low, band, swapped) if kv_head == 0 else jnp.where(low, swapped, band)


def _pair_rows(ref, pair, scale=None):
    v = ref[:, LANES * pair:LANES * (pair + 1)]
    if scale is not None:
        v = v * scale
    low, zero = _low_half(v.shape), jnp.zeros_like(v)
    return jnp.concatenate([jnp.where(low, v, zero), jnp.where(low, zero, v)], axis=0)


def _per_head(even, odd):
    return jnp.where(lax.broadcasted_iota(jnp.int32, (2 * BLOCK, 1), 0) < BLOCK, even, odd)


def _twice(x):
    return jnp.concatenate([x, x], axis=0)


def _pair_scores(q_ref, kk, dist2, pair, first, own2):
    s2 = _nt(_pair_rows(q_ref, pair, HEAD_DIM ** -0.5), kk)
    before = jnp.where(first, -jnp.inf, s2[:, :BLOCK])
    slopes = _per_head(ALIBI_SLOPES[2 * pair], ALIBI_SLOPES[2 * pair + 1])
    return jnp.where(own2, s2[:, BLOCK:], before) - slopes * dist2


def _heads_of(stack):
    return jnp.where(_low_half((BLOCK, LANES)), stack[:BLOCK], stack[BLOCK:])


def _softmax_sink(s, sink):
    m = jnp.maximum(jnp.max(s, axis=-1, keepdims=True), sink)
    p = jnp.exp(s - m)
    psink = jnp.exp(sink - m)
    inv = 1.0 / (jnp.sum(p, axis=-1, keepdims=True) + psink)
    return p * inv, psink * inv


def _bands(kvc_ref, kvp_ref):
    kband = jnp.concatenate([kvp_ref[:, 0:LANES], kvc_ref[:, 0:LANES]], axis=0)
    vband = jnp.concatenate([kvp_ref[:, LANES:2 * LANES], kvc_ref[:, LANES:2 * LANES]], axis=0)
    return ([_both_halves(kband, hk) for hk in range(N_KV_HEADS)],
            [_both_halves(vband, hk) for hk in range(N_KV_HEADS)])


SMEM = pl.BlockSpec(memory_space=pltpu.SMEM)
PAIRS = range(N_Q_HEADS // 2)
PAIRS_PER_KV = Q_PER_KV // 2


def _attn_fwd(q, kv, dist, sinks):
    t = q.shape[0]

    def body(q_ref, kvc_ref, kvp_ref, dist_ref, sink_ref, o_ref, s_scr, p_scr):
        first = pl.program_id(0) == 0
        own2 = _twice(_own_block())
        dist2 = _twice(dist_ref[...])
        kk, vv = _bands(kvc_ref, kvp_ref)
        for pair in PAIRS:
            s_scr[pair] = _pair_scores(q_ref, kk[pair // PAIRS_PER_KV], dist2, pair, first, own2)
        for pair in PAIRS:
            probs, _ = _softmax_sink(s_scr[pair], _per_head(sink_ref[2 * pair], sink_ref[2 * pair + 1]))
            p_scr[pair] = _unfold(probs.astype(BF16), own2)
        for pair in PAIRS:
            out = _nn(p_scr[pair], vv[pair // PAIRS_PER_KV])
            o_ref[:, LANES * pair:LANES * (pair + 1)] = _heads_of(out).astype(BF16)

    return pl.pallas_call(
        body, name="attn_fwd", grid=(t // BLOCK,),
        in_specs=[_rows(BLOCK, ATTN_WIDTH), _rows(BLOCK, 2 * KV_WIDTH),
                  pl.BlockSpec((BLOCK, 2 * KV_WIDTH), lambda i: (jnp.maximum(i - 1, 0), 0)),
                  _resident(dist.shape), SMEM],
        out_specs=_rows(BLOCK, ATTN_WIDTH),
        out_shape=jax.ShapeDtypeStruct((t, ATTN_WIDTH), BF16),
        scratch_shapes=[pltpu.VMEM((len(PAIRS), 2 * BLOCK, BLOCK), F32),
                        pltpu.VMEM((len(PAIRS), 2 * BLOCK, 2 * BLOCK), BF16)],
        compiler_params=_params("parallel"),
    )(q, kv, kv, dist, sinks)


def _pool_counts(tm, width):
    row = pl.program_id(0) * tm + lax.broadcasted_iota(jnp.int32, (tm, 1), 0)
    return jnp.minimum(row + 1, width).astype(F32)


def _trailing_sums(zz, group):
    s = zz
    for k in range(group + 1):
        s = s + pltpu.roll(s, 1 << k, 0)
    return s


def _leading_sums(zz, group):
    rows = zz.shape[0]
    s = zz
    for k in range(group + 1):
        s = s + pltpu.roll(s, rows - (1 << k), 0)
    return s


def _mix_out_fwd(attn, z, gate, h, wattn, wmix, scale, wpool_t, wout, tm, carry=None):
    t, d = h.shape

    def body(attn_ref, z_ref, halo_ref, gate_ref, h_ref, wattn_ref, wmix_ref, scale_ref, wpool_ref, wout_ref,
             h2_ref, a_ref, p_ref, merged_ref, ms_ref, pooled_ref):
        halo = jnp.where(pl.program_id(0) == 0, 0.0, halo_ref[...])
        for gi, width in enumerate(POOL_WINDOWS):
            lo, hi = gi * POOL_GROUP, (gi + 1) * POOL_GROUP
            zg = z_ref[:, lo:hi]
            sums = _trailing_sums(jnp.concatenate([halo[:, lo:hi], zg], axis=0), gi)[HALO:, :]
            pooled = (sums / _pool_counts(tm, width) - zg).astype(BF16)
            pooled_ref[:, lo:hi] = pooled
            ms_ref[:, lo:hi] = (_nn(pooled, wmix_ref[gi]) * scale_ref[:, lo:hi]).astype(BF16)
        p = _nt(ms_ref[...], wpool_ref[...])
        a = _nn(attn_ref[...], wattn_ref[...])
        a_ref[...] = a.astype(BF16)
        p_ref[...] = p.astype(BF16)
        merged =(jax.nn.sigmoid(gate_ref[:, 0:d]) * a + jax.nn.sigmoid(gate_ref[:, d:2 * d]) * p).astype(BF16)
        merged_ref[...] = merged
        h2_ref[...] = h_ref[...] + _nn(merged, wout_ref[...])

    halo_spec = pl.BlockSpec((HALO, POOL_WIDTH), lambda i: (jnp.maximum(i * (tm // HALO) - 1, 0), 0))
    return _launch(
        body, (attn, z, z, gate, h, wattn, wmix, scale, wpool_t, wout), carry, name="mix_out_fwd", grid=(t // tm,),
        in_specs=[_rows(tm, ATTN_WIDTH), _rows(tm, POOL_WIDTH), halo_spec, _rows(tm, 2 * d), _rows(tm, d),
                  _resident(wattn.shape), _resident(wmix.shape), _resident(scale.shape), _resident(wpool_t.shape),
                  _resident(wout.shape)],
        out_specs=[_rows(tm, d), _rows(tm, d), _rows(tm, d), _rows(tm, d), _rows(tm, POOL_WIDTH),
                   _rows(tm, POOL_WIDTH)],
        out_shape=[jax.ShapeDtypeStruct((t, d), F32), jax.ShapeDtypeStruct((t, d), BF16),
                   jax.ShapeDtypeStruct((t, d), BF16), jax.ShapeDtypeStruct((t, d), BF16),
                   jax.ShapeDtypeStruct((t, POOL_WIDTH), BF16), jax.ShapeDtypeStruct((t, POOL_WIDTH), BF16)],
        semantics=("parallel",))


def _mix_out_bwd(dh, gate, a, p, pooled, wattn, wmix, scale, wpool_t, wout, tm):
    t, d = dh.shape

    def body(dh_ref, gate_ref, a_ref, p_ref, pooled_ref, wattn_ref, wmix_ref, scale_ref, wpool_ref, wout_ref,
             dhb_ref, dab_ref, dpb_ref, dattn_ref, dgate_ref, dpooled_ref, dwmix_ref, dscale_ref):
        @pl.when(pl.program_id(0) == 0)
        def _():
            dwmix_ref[...] = jnp.zeros_like(dwmix_ref)
            dscale_ref[...] = jnp.zeros_like(dscale_ref)

        dhb = dh_ref[...].astype(BF16)
        dhb_ref[...] = dhb
        dm = _nt(dhb, wout_ref[...])
        sa = jax.nn.sigmoid(gate_ref[:, 0:d])
        sp = jax.nn.sigmoid(gate_ref[:, d:2 * d])
        da = (dm * sa).astype(BF16)
        dp = (dm * sp).astype(BF16)
        dab_ref[...] = da
        dpb_ref[...] = dp
        dgate_ref[:, 0:d] = (dm * a_ref[...].astype(F32) * (sa * (1.0 - sa))).astype(BF16)
        dgate_ref[:, d:2 * d] = (dm * p_ref[...].astype(F32) * (sp * (1.0 - sp))).astype(BF16)
        dattn_ref[...] = _nt(da, wattn_ref[...]).astype(BF16)
        dms = _nn(dp, wpool_ref[...])
        for gi in range(len(POOL_WINDOWS)):
            lo, hi = gi * POOL_GROUP, (gi + 1) * POOL_GROUP
            pooled_g = pooled_ref[:, lo:hi]
            mixed = _nn(pooled_g, wmix_ref[gi])
            dscale_ref[:, lo:hi] += jnp.sum(dms[:, lo:hi] * mixed, axis=0, keepdims=True)
            dmixed = (dms[:, lo:hi] * scale_ref[:, lo:hi]).astype(BF16)
            dwmix_ref[gi] += _tn(pooled_g, dmixed)
            dpooled_ref[:, lo:hi] = _nt(dmixed, wmix_ref[gi])

    acc = lambda shape: pl.BlockSpec(shape, lambda i: (0,) * len(shape))
    return pl.pallas_call(
        body, name="mix_out_bwd", grid=(t // tm,),
        in_specs=[_rows(tm, d), _rows(tm, 2 * d), _rows(tm, d), _rows(tm, d), _rows(tm, POOL_WIDTH),
                  _resident(wattn.shape), _resident(wmix.shape), _resident(scale.shape), _resident(wpool_t.shape),
                  _resident(wout.shape)],
        out_specs=[_rows(tm, d), _rows(tm, d), _rows(tm, d), _rows(tm, ATTN_WIDTH), _rows(tm, 2 * d),
                   _rows(tm, POOL_WIDTH), acc(wmix.shape), acc((1, POOL_WIDTH))],
        out_shape=[jax.ShapeDtypeStruct((t, d), BF16), jax.ShapeDtypeStruct((t, d), BF16),
                   jax.ShapeDtypeStruct((t, d), BF16), jax.ShapeDtypeStruct((t, ATTN_WIDTH), BF16),
                   jax.ShapeDtypeStruct((t, 2 * d), BF16), jax.ShapeDtypeStruct((t, POOL_WIDTH), F32),
                   jax.ShapeDtypeStruct(wmix.shape, F32), jax.ShapeDtypeStruct((1, POOL_WIDTH), F32)],
        compiler_params=_params("arbitrary"),
    )(dh, gate, a, p, pooled, wattn, wmix, scale, wpool_t, wout)


def _fold_halves(x):
    return x + pltpu.roll(x, HEAD_DIM, 1)


def _attn_bwd(q, kv, dattn, dist, sinks, carry=None):
    t = q.shape[0]

    def body(q_ref, kvc_ref, kvp_ref, do_ref, dist_ref, sink_ref, dq_ref, dkv_own_ref, dkv_prev_ref, dsink_ref,
             s_scr, dp_scr, p_scr, ds_scr):
        first = pl.program_id(0) == 0

        @pl.when(first)
        def _():
            dsink_ref[...] = jnp.zeros_like(dsink_ref)

        own2 = _twice(_own_block())
        dist2 = _twice(dist_ref[...])
        kk, vv = _bands(kvc_ref, kvp_ref)
        lane = lax.broadcasted_iota(jnp.int32, (1, LANES), 1)
        for pair in PAIRS:
            hk = pair // PAIRS_PER_KV
            s_scr[pair] = _pair_scores(q_ref, kk[hk], dist2, pair, first, own2)
            dp_scr[pair] = _fold(_nt(_pair_rows(do_ref, pair), vv[hk]), own2)
        dsink = jnp.zeros((1, LANES), F32)
        for pair in PAIRS:
            probs, psink = _softmax_sink(s_scr[pair], _per_head(sink_ref[2 * pair], sink_ref[2 * pair + 1]))
            dprobs = dp_scr[pair]
            rowdot = jnp.sum(probs * dprobs, axis=-1, keepdims=True)
            p_scr[pair] = _unfold(probs.astype(BF16), own2)
            ds_scr[pair] = _unfold((probs * (dprobs - rowdot)).astype(BF16), own2)
            dsink_rows = -psink * rowdot
            for half, head in enumerate((2 * pair, 2 * pair + 1)):
                head_sum = jnp.sum(dsink_rows[half * BLOCK:(half + 1) * BLOCK], axis=0, keepdims=True)
                dsink = dsink + jnp.where(lane == head, head_sum, 0.0)
        dk_heads, dv_heads = [], []
        top = lax.broadcasted_iota(jnp.int32, (LANES, BLOCK), 0) < HEAD_DIM

        def by_head(x_t):
            zero = jnp.zeros_like(x_t)
            return jnp.concatenate([jnp.where(top, x_t, zero), jnp.where(top, zero, x_t)], axis=1)

        for hk in range(N_KV_HEADS):
            dk_t = jnp.zeros((LANES, 2 * BLOCK), F32)
            dv_t = jnp.zeros((LANES, 2 * BLOCK), F32)
            for pair in range(hk * PAIRS_PER_KV, (hk + 1) * PAIRS_PER_KV):
                cols = slice(LANES * pair, LANES * (pair + 1))
                dv_t = dv_t + _nn(by_head(do_ref[:, cols].T), p_scr[pair])
                dk_t = dk_t + _nn(by_head((q_ref[:, cols] * HEAD_DIM ** -0.5).T), ds_scr[pair])
                dq_ref[:, cols] = (_heads_of(_nn(ds_scr[pair], kk[hk])) * HEAD_DIM ** -0.5).astype(BF16)
            dk_heads.append(_fold_halves(dk_t.T))
            dv_heads.append(_fold_halves(dv_t.T))
        low = _low_half(dk_heads[0].shape)
        dkv = jnp.concatenate([jnp.where(low, dk_heads[0], dk_heads[1]), jnp.where(low, dv_heads[0], dv_heads[1])],
                              axis=1)
        dkv_prev_ref[...] = dkv[0:BLOCK, :]
        dkv_own_ref[...] = dkv[BLOCK:2 * BLOCK, :]
        dsink_ref[...] += dsink

    return _launch(
        body, (q, kv, kv, dattn, dist, sinks), carry, name="attn_bwd", grid=(t // BLOCK,),
        in_specs=[_rows(BLOCK, ATTN_WIDTH), _rows(BLOCK, 2 * KV_WIDTH),
                  pl.BlockSpec((BLOCK, 2 * KV_WIDTH), lambda i: (jnp.maximum(i - 1, 0), 0)),
                  _rows(BLOCK, ATTN_WIDTH), _resident(dist.shape), SMEM],
        out_specs=[_rows(BLOCK, ATTN_WIDTH), _rows(BLOCK, 2 * KV_WIDTH), _rows(BLOCK, 2 * KV_WIDTH),
                   pl.BlockSpec((1, LANES), lambda i: (0, 0))],
        out_shape=[jax.ShapeDtypeStruct((t, ATTN_WIDTH), BF16), jax.ShapeDtypeStruct((t, 2 * KV_WIDTH), F32),
                   jax.ShapeDtypeStruct((t, 2 * KV_WIDTH), F32), jax.ShapeDtypeStruct((1, LANES), F32)],
        scratch_shapes=[pltpu.VMEM((len(PAIRS), 2 * BLOCK, BLOCK), F32), pltpu.VMEM((len(PAIRS), 2 * BLOCK, BLOCK), F32),
                        pltpu.VMEM((len(PAIRS), 2 * BLOCK, 2 * BLOCK), BF16),
                        pltpu.VMEM((len(PAIRS), 2 * BLOCK, 2 * BLOCK), BF16)],
        semantics=("arbitrary",))


def _mix_in_bwd(dq, dkv_own, dkv_prev, dpooled, dgate, h, g, win_t, dh_res, tm, carry=None):
    t, d = h.shape
    nt = t // tm

    def body(dq_ref, own_ref, prev_ref, prev_next_ref, dpool_ref, halo_ref, dgate_ref, h_ref, g_ref, w_ref, res_ref,
             dproj_ref, dh_ref, dhb_ref, dg_ref):
        i = pl.program_id(0)
        last = i == nt - 1
        dproj_ref[:, 0:OFF_KV] = dq_ref[...]
        from_next = jnp.where(last, 0.0, prev_next_ref[...])
        if tm > BLOCK:
            from_next = jnp.concatenate([prev_ref[BLOCK:tm, :], from_next], axis=0)
        dproj_ref[:, OFF_KV:OFF_Z] = (own_ref[...] + from_next).astype(BF16)
        halo = jnp.where(last, 0.0, halo_ref[...])
        for gi, width in enumerate(POOL_WINDOWS):
            lo, hi = gi * POOL_GROUP, (gi + 1) * POOL_GROUP
            dpg = dpool_ref[:, lo:hi]
            scaled = jnp.concatenate([dpg / _pool_counts(tm, width), halo[:, lo:hi] / float(width)], axis=0)
            dz = _leading_sums(scaled, gi)[0:tm, :] - dpg
            dproj_ref[:, OFF_Z + lo:OFF_Z + hi] = dz.astype(BF16)
        dproj_ref[:, OFF_GATE:IN_WIDTH] = dgate_ref[...]
        du = _nn(dproj_ref[...], w_ref[...])
        xh, r = _rms_fwd(h_ref[...], g_ref[...])
        dx, dg = _rms_bwd(du, xh, r, g_ref[...])
        dh = res_ref[...] + dx
        dh_ref[...] = dh
        dhb_ref[...] = dh.astype(BF16)

        @pl.when(i == 0)
        def _():
            dg_ref[...] = jnp.zeros_like(dg_ref)

        dg_ref[...] += dg

    per = tm // BLOCK
    next_block = pl.BlockSpec((BLOCK, 2 * KV_WIDTH), lambda i: (jnp.minimum((i + 1) * per, t // BLOCK - 1), 0))
    next_halo = pl.BlockSpec((HALO, POOL_WIDTH), lambda i: (jnp.minimum((i + 1) * (tm // HALO), t // HALO - 1), 0))
    return _launch(
        body, (dq, dkv_own, dkv_prev, dkv_prev, dpooled, dpooled, dgate, h, g, win_t, dh_res), carry,
        name="mix_in_bwd", grid=(nt,),
        in_specs=[_rows(tm, ATTN_WIDTH), _rows(tm, 2 * KV_WIDTH), _rows(tm, 2 * KV_WIDTH), next_block,
                  _rows(tm, POOL_WIDTH), next_halo, _rows(tm, 2 * d), _rows(tm, d), _resident((1, d)),
                  _resident((IN_WIDTH, d)), _rows(tm, d)],
        out_specs=[_rows(tm, IN_WIDTH), _rows(tm, d), _rows(tm, d), pl.BlockSpec((1, d), lambda i: (0, 0))],
        out_shape=[jax.ShapeDtypeStruct((t, IN_WIDTH), BF16), jax.ShapeDtypeStruct((t, d), F32),
                   jax.ShapeDtypeStruct((t, d), BF16), jax.ShapeDtypeStruct((1, d), F32)],
        semantics=("arbitrary",))


BIG = (("wup1_t", "ffn1_w_up", True), ("wdown1", "ffn1_w_down", False), ("win_t", "w_in", True),
       ("wattn", "w_attn_up", False), ("wpool_t", "w_pool_up", True), ("wout", "w_out", False),
       ("wup2_t", "ffn2_w_up", True), ("wdown2", "ffn2_w_down", False))
ANY = pl.BlockSpec(memory_space=pl.ANY)
WIRE = BF16


def _place():
    return lax.axis_index("x"), lax.axis_index("y"), lax.axis_index("c")


def _peer(k):
    x, y, c = _place()
    return x ^ (k >> 2), y ^ ((k >> 1) & 1), c ^ (k & 1)


def _index(px, py, pc):
    return 4 * px + 2 * py + pc


def _gather_carry(shards):
    n = len(shards)

    def tools(ins, outs, sems):
        send_sems, recv_sems, local_sems = sems
        x, y, c = _place()
        chips = [(1 - x, y), (x, 1 - y), (1 - x, 1 - y)]

        def rows(w, px, py, pc):
            r = ins[w].shape[0]
            return outs[w].at[pl.ds(_index(px, py, pc) * r, r), :]

        def copy(w, k, block, to, src=None):
            return pltpu.make_async_remote_copy(
                src_ref=rows(w, *block) if src is None else src, dst_ref=rows(w, *block),
                send_sem=send_sems.at[w, k], recv_sem=recv_sems.at[w, k], device_id=to, device_id_type=MESH)

        def own(w):
            return ([pltpu.make_async_copy(ins[w], rows(w, x, y, c), local_sems.at[w]),
                     copy(w, 0, (x, y, c), (x, y, 1 - c), src=ins[w])]
                    + [copy(w, 1 + j, (x, y, c), (*chip, c), src=ins[w]) for j, chip in enumerate(chips)])

        def passed(w, j):
            return copy(w, 4 + j, (*chips[j], c), (x, y, 1 - c))

        return (x, y, c), chips, copy, own, passed

    def start(ins, outs, sems):
        _, _, _, own, _ = tools(ins, outs, sems)
        for w in range(n):
            for cp in own(w):
                cp.start()

    def forward(w):
        def run(ins, outs, sems):
            (x, y, c), chips, copy, _, passed = tools(ins, outs, sems)
            for j, chip in enumerate(chips):
                copy(w, 1 + j, (*chip, c), (x, y, c)).wait_recv()
                passed(w, j).start()
        return run

    sizes = np.cumsum([s.size for s in shards]) / sum(s.size for s in shards)
    middles = [(float(sizes[w]), forward(w)) for w in range(n)]

    def finish(ins, outs, sems):
        (x, y, c), chips, copy, own, passed = tools(ins, outs, sems)
        for w in range(n):
            copy(w, 0, (x, y, 1 - c), (x, y, c)).wait_recv()
            for j, chip in enumerate(chips):
                copy(w, 4 + j, (*chip, 1 - c), (x, y, c)).wait_recv()
        for w in range(n):
            mine, *sent = own(w)
            for cp in sent + [passed(w, j) for j in range(len(chips))]:
                cp.wait_send()
            mine.wait()

    return _Carry(
        shards, [jax.ShapeDtypeStruct((N_DEV * s.shape[0], s.shape[1]), s.dtype) for s in shards],
        [pltpu.SemaphoreType.DMA((n, N_DEV - 1)), pltpu.SemaphoreType.DMA((n, N_DEV - 1)),
         pltpu.SemaphoreType.DMA((n,))], start, finish, middles)


def _scatter_carry(grads):
    n = len(grads)

    def tools(ins, outs, sems):
        send_sems, recv_sems, local_sems = sems
        me = _index(*_place())

        def block(ref, dev):
            r = ref.shape[0] // N_DEV
            return ref.at[pl.ds(dev * r, r), :]

        def copy(w, k, landing):
            to = _peer(k)
            return pltpu.make_async_remote_copy(
                src_ref=block(ins[w], _index(*to)), dst_ref=block(outs[w], landing), send_sem=send_sems.at[w, k - 1],
                recv_sem=recv_sems.at[w, k - 1], device_id=to, device_id_type=MESH)

        def mine(w):
            return pltpu.make_async_copy(block(ins[w], me), block(outs[w], me), local_sems.at[w])

        return me, copy, mine

    def start(ins, outs, sems):
        me, copy, mine = tools(ins, outs, sems)
        for w in range(n):
            mine(w).start()
            for k in range(1, N_DEV):
                copy(w, k, me).start()

    def finish(ins, outs, sems):
        _, copy, mine = tools(ins, outs, sems)
        for w in range(n):
            for k in range(1, N_DEV):
                copy(w, k, _index(*_peer(k))).wait()
            mine(w).wait()

    return _Carry(
        grads, [jax.ShapeDtypeStruct(g.shape, g.dtype) for g in grads],
        [pltpu.SemaphoreType.DMA((n, N_DEV - 1)), pltpu.SemaphoreType.DMA((n, N_DEV - 1)),
         pltpu.SemaphoreType.DMA((n,))], start, finish)


def _small_carry(small):
    srows = small.shape[0]

    def tools(ins, outs, sems):
        send_sems, recv_sems, local_sem = sems
        me = _index(*_place())

        def slot(dev):
            return outs[0].at[pl.ds(dev * srows, srows), :]

        def copy(k, landing):
            return pltpu.make_async_remote_copy(
                src_ref=ins[0], dst_ref=slot(landing), send_sem=send_sems.at[k - 1], recv_sem=recv_sems.at[k - 1],
                device_id=_peer(k), device_id_type=MESH)

        return me, copy, pltpu.make_async_copy(ins[0], slot(me), local_sem)

    def start(ins, outs, sems):
        me, copy, mine = tools(ins, outs, sems)
        mine.start()
        for k in range(1, N_DEV):
            copy(k, me).start()

    def finish(ins, outs, sems):
        _, copy, mine = tools(ins, outs, sems)
        for k in range(1, N_DEV):
            copy(k, _index(*_peer(k))).wait()
        mine.wait()

    return _Carry([small], [jax.ShapeDtypeStruct((N_DEV * srows, LANES), small.dtype)],
                  [pltpu.SemaphoreType.DMA((N_DEV - 1,)), pltpu.SemaphoreType.DMA((N_DEV - 1,)),
                   pltpu.SemaphoreType.DMA], start, finish)


def _exchange(carry, name):
    ci = len(carry.inputs)
    co = len(carry.out_shape)

    def body(*refs):
        parts = refs[:ci], refs[ci:ci + co], refs[ci + co:]
        carry.start(*parts)
        for _, fn in carry.middles:
            fn(*parts)
        carry.finish(*parts)

    return list(pl.pallas_call(body, name=name, in_specs=[ANY] * ci, out_specs=[ANY] * co, out_shape=carry.out_shape,
                               scratch_shapes=carry.scratch)(*carry.inputs))


def _adamw_math(w, g, m, v):
    m = ADAM_B1 * m + (1.0 - ADAM_B1) * g
    v = ADAM_B2 * v + (1.0 - ADAM_B2) * (g * g)
    m_hat = m / (1.0 - ADAM_B1 ** ADAM_STEP)
    v_hat = v / (1.0 - ADAM_B2 ** ADAM_STEP)
    return -ADAM_LR * (m_hat / (jnp.sqrt(v_hat) + ADAM_EPS) + ADAM_WD * w), m, v


def _sum_adamw(got, w, m, v, transposed, name):
    parts = list(got) if isinstance(got, (list, tuple)) else [got]
    r = parts[0].shape[0] // N_DEV
    cols = sum(part.shape[1] for part in parts)
    if transposed:
        (only,) = parts
        tile = cols if cols <= 512 else 256
        got_specs = [pl.BlockSpec((N_DEV, r, tile), lambda i: (0, 0, i))]
        spec, steps = pl.BlockSpec((tile, r), lambda i: (i, 0)), cols // tile
    else:
        tile = r if r <= 256 else r // 2
        got_specs = [pl.BlockSpec((N_DEV, tile, part.shape[1]), lambda i: (0, i, 0)) for part in parts]
        spec, steps = pl.BlockSpec((tile, cols), lambda i: (i, 0)), r // tile
    n = len(parts)

    def body(*refs):
        w_ref, m_ref, v_ref, g_ref, d_ref, m2_ref, v2_ref = refs[n:]
        sums = []
        for got_ref in refs[:n]:
            acc = got_ref[0].astype(F32)
            for dev in range(1, N_DEV):
                acc = acc + got_ref[dev].astype(F32)
            sums.append(acc)
        g = sums[0].T if transposed else (sums[0] if n == 1 else jnp.concatenate(sums, axis=1))
        g_ref[...] = g
        d_ref[...], m2_ref[...], v2_ref[...] = _adamw_math(w_ref[...], g, m_ref[...], v_ref[...])

    return pl.pallas_call(
        body, name=name, grid=(steps,), in_specs=got_specs + [spec, spec, spec], out_specs=[spec] * 4,
        out_shape=[jax.ShapeDtypeStruct(w.shape, F32)] * 4, compiler_params=_params("parallel"),
    )(*[part.reshape(N_DEV, r, part.shape[1]) for part in parts], w, m, v)


def _small_update(early, late, w, m, v):
    rows = w.shape[0]

    def body(early_ref, late_ref, w_ref, m_ref, v_ref, g_ref, d_ref, m2_ref, v2_ref):
        sums = []
        for ref in (early_ref, late_ref):
            acc = ref[0]
            for dev in range(1, N_DEV):
                acc = acc + ref[dev]
            sums.append(acc)
        g = jnp.concatenate(sums, axis=0)
        g_ref[...] = g
        d_ref[...], m2_ref[...], v2_ref[...] = _adamw_math(w_ref[...], g, m_ref[...], v_ref[...])

    return pl.pallas_call(
        body, name="small_update", out_shape=[jax.ShapeDtypeStruct((rows, LANES), F32)] * 4,
        compiler_params=pltpu.CompilerParams(vmem_limit_bytes=VMEM_LIMIT),
    )(early.reshape(N_DEV, -1, LANES), late.reshape(N_DEV, -1, LANES), w, m, v)


SMALL = (("pool_w_mix", 512), ("mix_norm", 8), ("ffn2_norm", 8), ("final_norm", 8), ("pool_scale", 8), ("sinks", 8),
         ("loss", 8), ("ffn1_norm", 8))
EARLY, LATE = SMALL[:-1], SMALL[-1:]


def _pack_small(parts, layout=SMALL):
    out = []
    for name, rows in layout:
        flat = parts[name].astype(F32).reshape(-1)
        out.append(jnp.pad(flat, (0, rows * LANES - flat.shape[0])).reshape(rows, LANES))
    return jnp.concatenate(out, axis=0)


def _unpack_small(packed, shapes):
    out, row = {}, 0
    for name, rows in SMALL:
        shape = shapes[name]
        size = int(np.prod(shape)) if shape else 1
        out[name] = packed[row:row + rows].reshape(-1)[:size].reshape(shape)
        row += rows
    return out


def kernel(x, ffn1_norm, ffn1_w_up, ffn1_w_down, mix_norm, w_in, sinks, w_attn_up, pool_w_mix, pool_scale, w_pool_up, w_out, ffn2_norm, ffn2_w_up, ffn2_w_down, final_norm, loss_target, m_ffn1_norm, m_ffn1_w_up, m_ffn1_w_down, m_mix_norm, m_w_in, m_sinks, m_w_attn_up, m_pool_w_mix, m_pool_scale, m_w_pool_up, m_w_out, m_ffn2_norm, m_ffn2_w_up, m_ffn2_w_down, m_final_norm, v_ffn1_norm, v_ffn1_w_up, v_ffn1_w_down, v_mix_norm, v_w_in, v_sinks, v_w_attn_up, v_pool_w_mix, v_pool_scale, v_w_pool_up, v_w_out, v_ffn2_norm, v_ffn2_w_up, v_ffn2_w_down, v_final_norm):
    args = dict(locals())
    weight_names = ("ffn1_norm", "ffn1_w_up", "ffn1_w_down", "mix_norm", "w_in", "sinks", "w_attn_up", "pool_w_mix",
                    "pool_scale", "w_pool_up", "w_out", "ffn2_norm", "ffn2_w_up", "ffn2_w_down", "final_norm")

    shard = {k: (args[p][0].T if tr else args[p][0]).astype(BF16) for k, p, tr in BIG}
    early, late = ["wdown1", "win_t", "wattn", "wpool_t", "wout"], ["wup2_t", "wdown2"]
    big = {"wup1_t": _exchange(_gather_carry([shard["wup1_t"]]), "gather_up1")[0]}

    xs, target = x[0], loss_target[0]
    t = xs.shape[0]
    tm_f, tm_b, tk = min(512, t), min(512, t), min(1024, t)
    g1, gm, g2, gf = ffn1_norm, mix_norm, ffn2_norm, final_norm.reshape(1, D_MODEL)
    dist = _attn_dist()
    sink_v = sinks.reshape(N_Q_HEADS)
    wmix_b = pool_w_mix[0].astype(BF16)

    (n1, ab1, act1), rest = _ffn_up(xs, g1, big["wup1_t"], tm_f, _gather_carry([shard[k] for k in early]))
    big.update(zip(early, rest))
    h1 = _ffn_down(xs, act1, big["wdown1"], tm_f)
    u, q, kv, z, gate = _mix_in_fwd(h1, gm, big["win_t"], tm_f)
    attn = _attn_fwd(q, kv, dist, sink_v)
    (h2, a, p, merged, ms, pooled), rest = _mix_out_fwd(
        attn, z, gate, h1, big["wattn"], wmix_b, pool_scale, big["wpool_t"], big["wout"], tm_b,
        _gather_carry([shard[k] for k in late]))
    big.update(zip(late, rest))
    ab2, n2, act2, loss_lanes, dh3, dhb3, dgf = _ffn_loss(h2, g2, big["wup2_t"], big["wdown2"], gf, target, tm_f)

    got = {}
    (gw_down2,), _ = _wgrad(act2, dhb3, 0.5, D_FF, tk, "wgrad_down2")
    (dab2,), (got["wdown2"],) = _ffn_bwd_hidden(dhb3, ab2, big["wdown2"], tm_f, _scatter_carry([gw_down2]))
    (dh2, dg2), _ = _ffn_bwd_input(dab2, dh3, h2, g2, big["wup2_t"], tm_f)
    (gw_up2,), _ = _wgrad(dab2, n2, 1.0, D_FF, tk, "wgrad_up2")
    dhb2, da_b, dp_b, dattn, dgate, dpooled, dwmix, dscale = _mix_out_bwd(
        dh2, gate, a, p, pooled, big["wattn"], wmix_b, pool_scale, big["wpool_t"], big["wout"], tm_b)
    (gw_out,), _ = _wgrad(merged, dhb2, 1.0, D_MODEL, tk, "wgrad_out")
    (gw_attn,), _ = _wgrad(attn, da_b, 1.0, D_MODEL, tk, "wgrad_attn")
    (gw_pool,), _ = _wgrad(dp_b, ms, 1.0, D_MODEL, tk, "wgrad_pool")
    (dq, dkv_own, dkv_prev, dsinks), (got["wup2_t"],) = _attn_bwd(q, kv, dattn, dist, sink_v, _scatter_carry([gw_up2]))
    (dproj, dh1, dhb1, dgm), (got["wout"], got["wattn"], got["wpool_t"]) = _mix_in_bwd(
        dq, dkv_own, dkv_prev, dpooled, dgate, h1, gm, big["win_t"], dh2, tm_b,
        _scatter_carry([gw_out, gw_attn, gw_pool]))
    (gw_down1,), _ = _wgrad(act1, dhb1, 0.5, D_FF, tk, "wgrad_down1")
    (gw_in,), (got["wdown1"],) = _wgrad(dproj, u, 1.0, IN_WIDTH // 2, tk, "wgrad_in", _scatter_carry([gw_down1]))
    (dab1,), (got["win_t"],) = _ffn_bwd_hidden(dhb1, ab1, big["wdown1"], tm_f, _scatter_carry([gw_in]))
    small_parts = {"pool_w_mix": dwmix, "mix_norm": dgm, "ffn2_norm": dg2, "final_norm": dgf, "pool_scale": dscale,
                   "sinks": dsinks[:, :N_Q_HEADS], "loss": loss_lanes[:, :1]}
    (gw_up1,), (small_early,) = _wgrad(dab1, n1, 1.0, D_FF, tk, "wgrad_up1",
                                       _small_carry(_pack_small(small_parts, EARLY)))
    (dx, dg1), (got["wup1_t"],) = _ffn_bwd_input(dab1, dh1, xs, g1, big["wup1_t"], tm_f, _scatter_carry([gw_up1]))
    (small_late,) = _exchange(_small_carry(_pack_small({"ffn1_norm": dg1}, LATE)), "gather_small")

    grad, delta, new_m, new_v = {}, {}, {}, {}
    for k, p, tr in BIG:
        outside = tr and args[p].shape[-1] % LANES != 0
        turn = (lambda a: a.T) if outside else (lambda a: a)
        res = _sum_adamw(got[k], turn(args[p][0]), turn(args["m_" + p][0]), turn(args["v_" + p][0]),
                         tr and not outside, "adamw_" + k)
        grad[p], delta[p], new_m[p], new_v[p] = (turn(a)[None] for a in res)

    shapes = {name: args[name].shape for name, _ in SMALL if name != "loss"}
    shapes["loss"] = ()
    packed = {pre: _pack_small({**{name: args[pre + name] for name, _ in SMALL if name != "loss"},
                                "loss": jnp.zeros((), F32)}) for pre in ("", "m_", "v_")}
    g_s, d_s, m_s, v_s = _small_update(small_early, small_late, packed[""], packed["m_"], packed["v_"])
    g_small, d_small, m_small, v_small = (_unpack_small(a, shapes) for a in (g_s, d_s, m_s, v_s))
    for name, _ in SMALL:
        if name != "loss":
            grad[name], delta[name], new_m[name], new_v[name] = (
                g_small[name], d_small[name], m_small[name], v_small[name])

    return (g_small["loss"], dx[None], *[grad[n] for n in weight_names], *[delta[n] for n in weight_names],
            *[new_m[n] for n in weight_names], *[new_v[n] for n in weight_names])
```

```python
import functools

import jax
import jax.numpy as jnp
import numpy as np
from jax import lax
from jax.experimental import pallas as pl
from jax.experimental.pallas import tpu as pltpu

F32 = jnp.float32
BF16 = jnp.bfloat16

D_MODEL = 1024
D_FF = 2816
N_Q_HEADS = 16
N_KV_HEADS = 2
Q_PER_KV = N_Q_HEADS // N_KV_HEADS
HEAD_DIM = 64
BLOCK = 128
ATTN_WIDTH = N_Q_HEADS * HEAD_DIM
KV_WIDTH = N_KV_HEADS * HEAD_DIM
POOL_WINDOWS = (2, 4, 8, 16)
POOL_GROUP = 128
POOL_WIDTH = 512
HALO = 16
IN_WIDTH = ATTN_WIDTH + 2 * KV_WIDTH + POOL_WIDTH + 2 * D_MODEL
OFF_KV = ATTN_WIDTH
OFF_Z = ATTN_WIDTH + 2 * KV_WIDTH
OFF_GATE = OFF_Z + POOL_WIDTH
NORM_EPS = 1e-6
ADAM_LR = 0.001
ADAM_B1 = 0.9
ADAM_B2 = 0.999
ADAM_EPS = 1e-08
ADAM_WD = 0.01
ADAM_STEP = 10

N_DEV = 8
N_CHIP = 4
LANES = 128
FF_CHUNK = 256
SLAB = 32
VMEM_LIMIT = 56 * 1024 * 1024
MESH = pl.DeviceIdType.MESH


def _nn(a, b):
    return jnp.dot(a, b, preferred_element_type=F32)


def _nt(a, b):
    return lax.dot_general(a, b, (((1,), (1,)), ((), ())), preferred_element_type=F32)


def _tn(a, b):
    return lax.dot_general(a, b, (((0,), (0,)), ((), ())), preferred_element_type=F32)


def _params(*sem):
    return pltpu.CompilerParams(dimension_semantics=sem, vmem_limit_bytes=VMEM_LIMIT)


def _resident(shape):
    return pl.BlockSpec(shape, lambda *_: (0,) * len(shape), pipeline_mode=pl.Buffered(1))


def _rows(tm, cols):
    return pl.BlockSpec((tm, cols), lambda i: (i, 0))


class _Carry:
    def __init__(self, inputs, out_shape, scratch, start, finish, middles=()):
        self.inputs, self.out_shape, self.scratch = list(inputs), list(out_shape), list(scratch)
        self.start, self.finish, self.middles = start, finish, list(middles)


def _launch(body, args, carry=None, *, name, grid, in_specs, out_specs, out_shape, scratch_shapes=(), semantics):
    in_specs, out_specs, out_shape, scratch_shapes = list(in_specs), list(out_specs), list(out_shape), list(scratch_shapes)
    if carry is None:
        res = pl.pallas_call(body, name=name, grid=grid, in_specs=in_specs, out_specs=out_specs, out_shape=out_shape,
                             scratch_shapes=scratch_shapes, compiler_params=_params(*semantics))(*args)
        return list(res), []
    ni, no, ns = len(in_specs), len(out_specs), len(scratch_shapes)
    ci, co = len(carry.inputs), len(carry.out_shape)
    total = int(np.prod(grid))

    def full(*refs):
        own_in, c_in = refs[:ni], refs[ni:ni + ci]
        own_out, c_out = refs[ni + ci:ni + ci + no], refs[ni + ci + no:ni + ci + no + co]
        own_scr, c_sem = refs[ni + ci + no + co:ni + ci + no + co + ns], refs[ni + ci + no + co + ns:]
        step = 0
        for axis, size in enumerate(grid):
            step = step * size + pl.program_id(axis)
        pl.when(step == 0)(lambda: carry.start(c_in, c_out, c_sem))
        for fraction, fn in carry.middles:
            at = min(total - 1, int(fraction * total) + 1)
            pl.when(step == at)(lambda fn=fn: fn(c_in, c_out, c_sem))
        body(*own_in, *own_out, *own_scr)
        pl.when(step == total - 1)(lambda: carry.finish(c_in, c_out, c_sem))

    res = pl.pallas_call(
        full, name=name, grid=grid, in_specs=in_specs + [ANY] * ci, out_specs=out_specs + [ANY] * co,
        out_shape=out_shape + carry.out_shape, scratch_shapes=scratch_shapes + carry.scratch,
        compiler_params=_params(*(["arbitrary"] * len(grid))),
    )(*args, *carry.inputs)
    return list(res[:no]), list(res[no:])


def _rms_fwd(xv, g):
    r = lax.rsqrt(jnp.mean(xv * xv, axis=-1, keepdims=True) + NORM_EPS)
    return xv * r, r


def _rms_bwd(dn, xh, r, g):
    dxh = dn * g
    dx = r * (dxh - xh * jnp.mean(dxh * xh, axis=-1, keepdims=True))
    return dx, jnp.sum(dn * xh, axis=0, keepdims=True)


def _ffn_loss(x, g, wup_t, wdown, gf, target, tm):
    t, d = x.shape
    f = wdown.shape[0]

    def body(x_ref, g_ref, wup_ref, wdn_ref, gf_ref, tgt_ref, ab_ref, n_ref, act_ref, loss_ref, dh_ref, dhb_ref, dg_ref):
        xv = x_ref[...]
        xh, _ = _rms_fwd(xv, g_ref[...])
        n = (xh * g_ref[...]).astype(BF16)
        n_ref[...] = n
        for c in range(f // FF_CHUNK):
            lo, hi = c * FF_CHUNK, (c + 1) * FF_CHUNK
            a = _nt(n, wup_ref[lo:hi, :])
            b = _nt(n, wup_ref[f + lo:f + hi, :])
            ab_ref[:, lo:hi] = a.astype(BF16)
            ab_ref[:, f + lo:f + hi] = b.astype(BF16)
            act_ref[:, lo:hi] = (a * jax.nn.sigmoid(a) * b).astype(BF16)
        h = xv + 0.5 * _nn(act_ref[...], wdn_ref[...])
        yh, r = _rms_fwd(h, gf_ref[...])
        err = yh * gf_ref[...] - tgt_ref[...]
        part = 0.5 * jnp.sum(jnp.mean(err * err, axis=-1, keepdims=True), axis=0, keepdims=True)
        dh, dg = _rms_bwd(err * (1.0 / d), yh, r, gf_ref[...])
        dh_ref[...] = dh
        dhb_ref[...] = dh.astype(BF16)

        @pl.when(pl.program_id(0) == 0)
        def _():
            dg_ref[...] = jnp.zeros_like(dg_ref)
            loss_ref[...] = jnp.zeros_like(loss_ref)

        dg_ref[...] += dg
        loss_ref[...] += jnp.broadcast_to(part, loss_ref.shape)

    return pl.pallas_call(
        body, name="ffn_loss", grid=(t // tm,),
        in_specs=[_rows(tm, d), _resident((1, d)), _resident((2 * f, d)), _resident((f, d)), _resident((1, d)),
                  _rows(tm, d)],
        out_specs=[_rows(tm, 2 * f), _rows(tm, d), _rows(tm, f), pl.BlockSpec((1, LANES), lambda i: (0, 0)),
                   _rows(tm, d), _rows(tm, d), pl.BlockSpec((1, d), lambda i: (0, 0))],
        out_shape=[jax.ShapeDtypeStruct((t, 2 * f), BF16), jax.ShapeDtypeStruct((t, d), BF16),
                   jax.ShapeDtypeStruct((t, f), BF16), jax.ShapeDtypeStruct((1, LANES), F32),
                   jax.ShapeDtypeStruct((t, d), F32), jax.ShapeDtypeStruct((t, d), BF16),
                   jax.ShapeDtypeStruct((1, d), F32)],
        compiler_params=_params("arbitrary"),
    )(x, g, wup_t, wdown, gf, target)


def _ffn_up(x, g, wup_t, tm, carry=None):
    t, d = x.shape
    f = wup_t.shape[0] // 2

    def body(x_ref, g_ref, wup_ref, n_ref, ab_ref, act_ref):
        xh, _ = _rms_fwd(x_ref[...], g_ref[...])
        n = (xh * g_ref[...]).astype(BF16)
        n_ref[...] = n
        for c in range(f // FF_CHUNK):
            lo, hi = c * FF_CHUNK, (c + 1) * FF_CHUNK
            a = _nt(n, wup_ref[lo:hi, :])
            b = _nt(n, wup_ref[f + lo:f + hi, :])
            ab_ref[:, lo:hi] = a.astype(BF16)
            ab_ref[:, f + lo:f + hi] = b.astype(BF16)
            act_ref[:, lo:hi] = (a * jax.nn.sigmoid(a) * b).astype(BF16)

    return _launch(
        body, (x, g, wup_t), carry, name="ffn_up", grid=(t // tm,),
        in_specs=[_rows(tm, d), _resident((1, d)), _resident((2 * f, d))],
        out_specs=[_rows(tm, d), _rows(tm, 2 * f), _rows(tm, f)],
        out_shape=[jax.ShapeDtypeStruct((t, d), BF16), jax.ShapeDtypeStruct((t, 2 * f), BF16),
                   jax.ShapeDtypeStruct((t, f), BF16)],
        semantics=("parallel",))


def _ffn_down(x, act, wdown, tm, carry=None):
    t, d = x.shape
    f = wdown.shape[0]

    def body(x_ref, act_ref, wdn_ref, h_ref):
        h_ref[...] = x_ref[...] + 0.5 * _nn(act_ref[...], wdn_ref[...])

    return _launch(
        body, (x, act, wdown), carry, name="ffn_down", grid=(t // tm,),
        in_specs=[_rows(tm, d), _rows(tm, f), _resident((f, d))], out_specs=[_rows(tm, d)],
        out_shape=[jax.ShapeDtypeStruct((t, d), F32)], semantics=("parallel",))


def _ffn_bwd_hidden(dhb, ab, wdown, tm, carry=None):
    t, d = dhb.shape
    f = wdown.shape[0]

    def body(dh_ref, ab_ref, wdn_ref, dab_ref, dact_ref):
        half = dh_ref[...] * 0.5
        for c in range(f // FF_CHUNK):
            lo, hi = c * FF_CHUNK, (c + 1) * FF_CHUNK
            dact_ref[...] = _nt(half, wdn_ref[lo:hi, :])

            def slab(i, carry_):
                rows = pl.ds(pl.multiple_of(i * SLAB, SLAB), SLAB)
                a = ab_ref[rows, lo:hi].astype(F32)
                b = ab_ref[rows, f + lo:f + hi].astype(F32)
                s = jax.nn.sigmoid(a)
                ds_ = dact_ref[rows, :] * s
                dab_ref[rows, lo:hi] = (ds_ * b * (1.0 + a * (1.0 - s))).astype(BF16)
                dab_ref[rows, f + lo:f + hi] = (ds_ * a).astype(BF16)
                return carry_

            lax.fori_loop(0, tm // SLAB, slab, 0, unroll=True)

    return _launch(
        body, (dhb, ab, wdown), carry, name="ffn_bwd_hidden", grid=(t // tm,),
        in_specs=[_rows(tm, d), _rows(tm, 2 * f), _resident((f, d))], out_specs=[_rows(tm, 2 * f)],
        out_shape=[jax.ShapeDtypeStruct((t, 2 * f), BF16)],
        scratch_shapes=[pltpu.VMEM((tm, FF_CHUNK), F32)], semantics=("parallel",))


def _ffn_bwd_input(dab, dh, x, g, wup_t, tm, carry=None):
    t, d = x.shape
    f2 = wup_t.shape[0]

    def body(dab_ref, dh_ref, x_ref, g_ref, wup_ref, dx_ref, dg_ref):
        dn = _nn(dab_ref[...], wup_ref[...])
        xh, r = _rms_fwd(x_ref[...], g_ref[...])
        dx, dg = _rms_bwd(dn, xh, r, g_ref[...])
        dx_ref[...] = dh_ref[...] + dx

        @pl.when(pl.program_id(0) == 0)
        def _():
            dg_ref[...] = jnp.zeros_like(dg_ref)

        dg_ref[...] += dg

    return _launch(
        body, (dab, dh, x, g, wup_t), carry, name="ffn_bwd_input", grid=(t // tm,),
        in_specs=[_rows(tm, f2), _rows(tm, d), _rows(tm, d), _resident((1, d)), _resident((f2, d))],
        out_specs=[_rows(tm, d), pl.BlockSpec((1, d), lambda i: (0, 0))],
        out_shape=[jax.ShapeDtypeStruct((t, d), F32), jax.ShapeDtypeStruct((1, d), F32)],
        semantics=("arbitrary",))


def _wgrad(lhs, rhs, scale, bm, tk, name, carry=None):
    t, m = lhs.shape
    n = rhs.shape[1]
    steps = t // tk
    chunk = bm if bm <= 2048 else bm // 2

    def body(l_ref, r_ref, o_ref, acc_ref):
        @pl.when(pl.program_id(1) == 0)
        def _():
            acc_ref[...] = jnp.zeros_like(acc_ref)

        for lo in range(0, bm, chunk):
            acc_ref[lo:lo + chunk, :] += _tn(l_ref[:, lo:lo + chunk], r_ref[...])

        @pl.when(pl.program_id(1) == steps - 1)
        def _():
            o_ref[...] = (scale * acc_ref[...]).astype(o_ref.dtype)

    return _launch(
        body, (lhs, rhs), carry, name=name, grid=(m // bm, steps),
        in_specs=[pl.BlockSpec((tk, bm), lambda i, k: (k, i)), pl.BlockSpec((tk, n), lambda i, k: (k, 0))],
        out_specs=[pl.BlockSpec((bm, n), lambda i, k: (i, 0))],
        out_shape=[jax.ShapeDtypeStruct((m, n), WIRE)],
        scratch_shapes=[pltpu.VMEM((bm, n), F32)], semantics=("parallel", "arbitrary"))


def _mix_in_fwd(h, g, win_t, tm, carry=None):
    t, d = h.shape

    def body(h_ref, g_ref, w_ref, u_ref, q_ref, kv_ref, z_ref, gate_ref):
        xh, _ = _rms_fwd(h_ref[...], g_ref[...])
        u = (xh * g_ref[...]).astype(BF16)
        u_ref[...] = u
        q_ref[...] = _nt(u, w_ref[0:OFF_KV, :]).astype(BF16)
        kv_ref[...] = _nt(u, w_ref[OFF_KV:OFF_Z, :]).astype(BF16)
        z_ref[...] = _nt(u, w_ref[OFF_Z:OFF_GATE, :])
        gate_ref[...] = _nt(u, w_ref[OFF_GATE:IN_WIDTH, :]).astype(BF16)

    return _launch(
        body, (h, g, win_t), carry, name="mix_in_fwd", grid=(t // tm,),
        in_specs=[_rows(tm, d), _resident((1, d)), _resident((IN_WIDTH, d))],
        out_specs=[_rows(tm, d), _rows(tm, ATTN_WIDTH), _rows(tm, 2 * KV_WIDTH), _rows(tm, POOL_WIDTH),
                   _rows(tm, 2 * D_MODEL)],
        out_shape=[jax.ShapeDtypeStruct((t, d), BF16), jax.ShapeDtypeStruct((t, ATTN_WIDTH), BF16),
                   jax.ShapeDtypeStruct((t, 2 * KV_WIDTH), BF16), jax.ShapeDtypeStruct((t, POOL_WIDTH), F32),
                   jax.ShapeDtypeStruct((t, 2 * D_MODEL), BF16)],
        semantics=("parallel",))


ALIBI_SLOPES = tuple(float(s) for s in (2.0 ** (-8.0 * np.arange(1, N_Q_HEADS + 1, dtype=np.float32) / N_Q_HEADS)))


def _attn_dist():
    return jnp.asarray(((np.arange(BLOCK)[:, None] - np.arange(BLOCK)[None, :]) % BLOCK).astype(np.float32))


def _own_block():
    shape = (BLOCK, BLOCK)
    return lax.broadcasted_iota(jnp.int32, shape, 1) <= lax.broadcasted_iota(jnp.int32, shape, 0)


def _fold(band2, own):
    return jnp.where(own, band2[:, BLOCK:], band2[:, :BLOCK])


def _unfold(x, own):
    zero = jnp.zeros_like(x)
    return jnp.concatenate([jnp.where(own, zero, x), jnp.where(own, x, zero)], axis=1)


def _low_half(shape):
    return lax.broadcasted_iota(jnp.int32, shape, len(shape) - 1) < HEAD_DIM


def _both_halves(band, kv_head):
    low = _low_half(band.shape)
    swapped = pltpu.roll(band, HEAD_DIM, 1)
    return jnp.where(low, band, swapped) if kv_head == 0 else jnp.where(low, swapped, band)


def _pair_rows(ref, pair, scale=None):
    v = ref[:, LANES * pair:LANES * (pair + 1)]
    if scale is not None:
        v = v * scale
    low, zero = _low_half(v.shape), jnp.zeros_like(v)
    return jnp.concatenate([jnp.where(low, v, zero), jnp.where(low, zero, v)], axis=0)


def _per_head(even, odd):
    return jnp.where(lax.broadcasted_iota(jnp.int32, (2 * BLOCK, 1), 0) < BLOCK, even, odd)


def _twice(x):
    return jnp.concatenate([x, x], axis=0)


def _pair_scores(q_ref, kk, dist2, pair, first, own2):
    s2 = _nt(_pair_rows(q_ref, pair, HEAD_DIM ** -0.5), kk)
    before = jnp.where(first, -jnp.inf, s2[:, :BLOCK])
    slopes = _per_head(ALIBI_SLOPES[2 * pair], ALIBI_SLOPES[2 * pair + 1])
    return jnp.where(own2, s2[:, BLOCK:], before) - slopes * dist2


def _heads_of(stack):
    return jnp.where(_low_half((BLOCK, LANES)), stack[:BLOCK], stack[BLOCK:])


def _softmax_sink(s, sink):
    m = jnp.maximum(jnp.max(s, axis=-1, keepdims=True), sink)
    p = jnp.exp(s - m)
    psink = jnp.exp(sink - m)
    inv = 1.0 / (jnp.sum(p, axis=-1, keepdims=True) + psink)
    return p * inv, psink * inv


def _bands(kvc_ref, kvp_ref):
    kband = jnp.concatenate([kvp_ref[:, 0:LANES], kvc_ref[:, 0:LANES]], axis=0)
    vband = jnp.concatenate([kvp_ref[:, LANES:2 * LANES], kvc_ref[:, LANES:2 * LANES]], axis=0)
    return ([_both_halves(kband, hk) for hk in range(N_KV_HEADS)],
            [_both_halves(vband, hk) for hk in range(N_KV_HEADS)])


SMEM = pl.BlockSpec(memory_space=pltpu.SMEM)
PAIRS = range(N_Q_HEADS // 2)
PAIRS_PER_KV = Q_PER_KV // 2


def _attn_fwd(q, kv, dist, sinks):
    t = q.shape[0]

    def body(q_ref, kvc_ref, kvp_ref, dist_ref, sink_ref, o_ref, s_scr, p_scr):
        first = pl.program_id(0) == 0
        own2 = _twice(_own_block())
        dist2 = _twice(dist_ref[...])
        kk, vv = _bands(kvc_ref, kvp_ref)
        for pair in PAIRS:
            s_scr[pair] = _pair_scores(q_ref, kk[pair // PAIRS_PER_KV], dist2, pair, first, own2)
        for pair in PAIRS:
            probs, _ = _softmax_sink(s_scr[pair], _per_head(sink_ref[2 * pair], sink_ref[2 * pair + 1]))
            p_scr[pair] = _unfold(probs.astype(BF16), own2)
        for pair in PAIRS:
            out = _nn(p_scr[pair], vv[pair // PAIRS_PER_KV])
            o_ref[:, LANES * pair:LANES * (pair + 1)] = _heads_of(out).astype(BF16)

    return pl.pallas_call(
        body, name="attn_fwd", grid=(t // BLOCK,),
        in_specs=[_rows(BLOCK, ATTN_WIDTH), _rows(BLOCK, 2 * KV_WIDTH),
                  pl.BlockSpec((BLOCK, 2 * KV_WIDTH), lambda i: (jnp.maximum(i - 1, 0), 0)),
                  _resident(dist.shape), SMEM],
        out_specs=_rows(BLOCK, ATTN_WIDTH),
        out_shape=jax.ShapeDtypeStruct((t, ATTN_WIDTH), BF16),
        scratch_shapes=[pltpu.VMEM((len(PAIRS), 2 * BLOCK, BLOCK), F32),
                        pltpu.VMEM((len(PAIRS), 2 * BLOCK, 2 * BLOCK), BF16)],
        compiler_params=_params("parallel"),
    )(q, kv, kv, dist, sinks)


def _pool_counts(tm, width):
    row = pl.program_id(0) * tm + lax.broadcasted_iota(jnp.int32, (tm, 1), 0)
    return jnp.minimum(row + 1, width).astype(F32)


def _trailing_sums(zz, group):
    s = zz
    for k in range(group + 1):
        s = s + pltpu.roll(s, 1 << k, 0)
    return s


def _leading_sums(zz, group):
    rows = zz.shape[0]
    s = zz
    for k in range(group + 1):
        s = s + pltpu.roll(s, rows - (1 << k), 0)
    return s


def _mix_out_fwd(attn, z, gate, h, wattn, wmix, scale, wpool_t, wout, tm, carry=None):
    t, d = h.shape

    def body(attn_ref, z_ref, halo_ref, gate_ref, h_ref, wattn_ref, wmix_ref, scale_ref, wpool_ref, wout_ref,
             h2_ref, a_ref, p_ref, merged_ref, ms_ref, pooled_ref):
        halo = jnp.where(pl.program_id(0) == 0, 0.0, halo_ref[...])
        for gi, width in enumerate(POOL_WINDOWS):
            lo, hi = gi * POOL_GROUP, (gi + 1) * POOL_GROUP
            zg = z_ref[:, lo:hi]
            sums = _trailing_sums(jnp.concatenate([halo[:, lo:hi], zg], axis=0), gi)[HALO:, :]
            pooled = (sums / _pool_counts(tm, width) - zg).astype(BF16)
            pooled_ref[:, lo:hi] = pooled
            ms_ref[:, lo:hi] = (_nn(pooled, wmix_ref[gi]) * scale_ref[:, lo:hi]).astype(BF16)
        p = _nt(ms_ref[...], wpool_ref[...])
        a = _nn(attn_ref[...], wattn_ref[...])
        a_ref[...] = a.astype(BF16)
        p_ref[...] = p.astype(BF16)
        merged = (jax.nn.sigmoid(gate_ref[:, 0:d].astype(F32)) * a
                  + jax.nn.sigmoid(gate_ref[:, d:2 * d].astype(F32)) * p).astype(BF16)
        merged_ref[...] = merged
        h2_ref[...] = h_ref[...] + _nn(merged, wout_ref[...])

    halo_spec = pl.BlockSpec((HALO, POOL_WIDTH), lambda i: (jnp.maximum(i * (tm // HALO) - 1, 0), 0))
    return _launch(
        body, (attn, z, z, gate, h, wattn, wmix, scale, wpool_t, wout), carry, name="mix_out_fwd", grid=(t // tm,),
        in_specs=[_rows(tm, ATTN_WIDTH), _rows(tm, POOL_WIDTH), halo_spec, _rows(tm, 2 * d), _rows(tm, d),
                  _resident(wattn.shape), _resident(wmix.shape), _resident(scale.shape), _resident(wpool_t.shape),
                  _resident(wout.shape)],
        out_specs=[_rows(tm, d), _rows(tm, d), _rows(tm, d), _rows(tm, d), _rows(tm, POOL_WIDTH),
                   _rows(tm, POOL_WIDTH)],
        out_shape=[jax.ShapeDtypeStruct((t, d), F32), jax.ShapeDtypeStruct((t, d), BF16),
                   jax.ShapeDtypeStruct((t, d), BF16), jax.ShapeDtypeStruct((t, d), BF16),
                   jax.ShapeDtypeStruct((t, POOL_WIDTH), BF16), jax.ShapeDtypeStruct((t, POOL_WIDTH), BF16)],
        semantics=("parallel",))


def _mix_out_bwd(dh, gate, a, p, pooled, wattn, wmix, scale, wpool_t, wout, tm):
    t, d = dh.shape

    def body(dh_ref, gate_ref, a_ref, p_ref, pooled_ref, wattn_ref, wmix_ref, scale_ref, wpool_ref, wout_ref,
             dhb_ref, dab_ref, dpb_ref, dattn_ref, dgate_ref, dpooled_ref, dwmix_ref, dscale_ref):
        @pl.when(pl.program_id(0) == 0)
        def _():
            dwmix_ref[...] = jnp.zeros_like(dwmix_ref)
            dscale_ref[...] = jnp.zeros_like(dscale_ref)

        dhb = dh_ref[...].astype(BF16)
        dhb_ref[...] = dhb
        dm = _nt(dhb, wout_ref[...])
        sa = jax.nn.sigmoid(gate_ref[:, 0:d].astype(F32))
        sp = jax.nn.sigmoid(gate_ref[:, d:2 * d].astype(F32))
        da = (dm * sa).astype(BF16)
        dp = (dm * sp).astype(BF16)
        dab_ref[...] = da
        dpb_ref[...] = dp
        dgate_ref[:, 0:d] = (dm * a_ref[...].astype(F32) * (sa * (1.0 - sa))).astype(BF16)
        dgate_ref[:, d:2 * d] = (dm * p_ref[...].astype(F32) * (sp * (1.0 - sp))).astype(BF16)
        dattn_ref[...] = _nt(da, wattn_ref[...]).astype(BF16)
        dms = _nn(dp, wpool_ref[...])
        for gi in range(len(POOL_WINDOWS)):
            lo, hi = gi * POOL_GROUP, (gi + 1) * POOL_GROUP
            pooled_g = pooled_ref[:, lo:hi]
            mixed = _nn(pooled_g, wmix_ref[gi])
            dscale_ref[:, lo:hi] += jnp.sum(dms[:, lo:hi] * mixed, axis=0, keepdims=True)
            dmixed = (dms[:, lo:hi] * scale_ref[:, lo:hi]).astype(BF16)
            dwmix_ref[gi] += _tn(pooled_g, dmixed)
            dpooled_ref[:, lo:hi] = _nt(dmixed, wmix_ref[gi])

    acc = lambda shape: pl.BlockSpec(shape, lambda i: (0,) * len(shape))
    return pl.pallas_call(
        body, name="mix_out_bwd", grid=(t // tm,),
        in_specs=[_rows(tm, d), _rows(tm, 2 * d), _rows(tm, d), _rows(tm, d), _rows(tm, POOL_WIDTH),
                  _resident(wattn.shape), _resident(wmix.shape), _resident(scale.shape), _resident(wpool_t.shape),
                  _resident(wout.shape)],
        out_specs=[_rows(tm, d), _rows(tm, d), _rows(tm, d), _rows(tm, ATTN_WIDTH), _rows(tm, 2 * d),
                   _rows(tm, POOL_WIDTH), acc(wmix.shape), acc((1, POOL_WIDTH))],
        out_shape=[jax.ShapeDtypeStruct((t, d), BF16), jax.ShapeDtypeStruct((t, d), BF16),
                   jax.ShapeDtypeStruct((t, d), BF16), jax.ShapeDtypeStruct((t, ATTN_WIDTH), BF16),
                   jax.ShapeDtypeStruct((t, 2 * d), BF16), jax.ShapeDtypeStruct((t, POOL_WIDTH), F32),
                   jax.ShapeDtypeStruct(wmix.shape, F32), jax.ShapeDtypeStruct((1, POOL_WIDTH), F32)],
        compiler_params=_params("arbitrary"),
    )(dh, gate, a, p, pooled, wattn, wmix, scale, wpool_t, wout)


def _fold_halves(x):
    return x + pltpu.roll(x, HEAD_DIM, 1)


def _attn_bwd(q, kv, dattn, dist, sinks, carry=None):
    t = q.shape[0]

    def body(q_ref, kvc_ref, kvp_ref, do_ref, dist_ref, sink_ref, dq_ref, dkv_own_ref, dkv_prev_ref, dsink_ref,
             s_scr, dp_scr, p_scr, ds_scr):
        first = pl.program_id(0) == 0

        @pl.when(first)
        def _():
            dsink_ref[...] = jnp.zeros_like(dsink_ref)

        own2 = _twice(_own_block())
        dist2 = _twice(dist_ref[...])
        kk, vv = _bands(kvc_ref, kvp_ref)
        lane = lax.broadcasted_iota(jnp.int32, (1, LANES), 1)
        for pair in PAIRS:
            hk = pair // PAIRS_PER_KV
            s_scr[pair] = _pair_scores(q_ref, kk[hk], dist2, pair, first, own2)
            dp_scr[pair] = _fold(_nt(_pair_rows(do_ref, pair), vv[hk]), own2)
        dsink = jnp.zeros((1, LANES), F32)
        for pair in PAIRS:
            probs, psink = _softmax_sink(s_scr[pair], _per_head(sink_ref[2 * pair], sink_ref[2 * pair + 1]))
            dprobs = dp_scr[pair]
            rowdot = jnp.sum(probs * dprobs, axis=-1, keepdims=True)
            p_scr[pair] = _unfold(probs.astype(BF16), own2)
            ds_scr[pair] = _unfold((probs * (dprobs - rowdot)).astype(BF16), own2)
            dsink_rows = -psink * rowdot
            for half, head in enumerate((2 * pair, 2 * pair + 1)):
                head_sum = jnp.sum(dsink_rows[half * BLOCK:(half + 1) * BLOCK], axis=0, keepdims=True)
                dsink = dsink + jnp.where(lane == head, head_sum, 0.0)
        dk_heads, dv_heads = [], []
        top = lax.broadcasted_iota(jnp.int32, (LANES, BLOCK), 0) < HEAD_DIM

        def by_head(x_t):
            zero = jnp.zeros_like(x_t)
            return jnp.concatenate([jnp.where(top, x_t, zero), jnp.where(top, zero, x_t)], axis=1)

        for hk in range(N_KV_HEADS):
            dk_t = jnp.zeros((LANES, 2 * BLOCK), F32)
            dv_t = jnp.zeros((LANES, 2 * BLOCK), F32)
            for pair in range(hk * PAIRS_PER_KV, (hk + 1) * PAIRS_PER_KV):
                cols = slice(LANES * pair, LANES * (pair + 1))
                dv_t = dv_t + _nn(by_head(do_ref[:, cols].T), p_scr[pair])
                dk_t = dk_t + _nn(by_head((q_ref[:, cols] * HEAD_DIM ** -0.5).T), ds_scr[pair])
                dq_ref[:, cols] = (_heads_of(_nn(ds_scr[pair], kk[hk])) * HEAD_DIM ** -0.5).astype(BF16)
            dk_heads.append(_fold_halves(dk_t.T))
            dv_heads.append(_fold_halves(dv_t.T))
        low = _low_half(dk_heads[0].shape)
        dkv = jnp.concatenate([jnp.where(low, dk_heads[0], dk_heads[1]), jnp.where(low, dv_heads[0], dv_heads[1])],
                              axis=1)
        dkv_prev_ref[...] = dkv[0:BLOCK, :]
        dkv_own_ref[...] = dkv[BLOCK:2 * BLOCK, :]
        dsink_ref[...] += dsink

    return _launch(
        body, (q, kv, kv, dattn, dist, sinks), carry, name="attn_bwd", grid=(t // BLOCK,),
        in_specs=[_rows(BLOCK, ATTN_WIDTH), _rows(BLOCK, 2 * KV_WIDTH),
                  pl.BlockSpec((BLOCK, 2 * KV_WIDTH), lambda i: (jnp.maximum(i - 1, 0), 0)),
                  _rows(BLOCK, ATTN_WIDTH), _resident(dist.shape), SMEM],
        out_specs=[_rows(BLOCK, ATTN_WIDTH), _rows(BLOCK, 2 * KV_WIDTH), _rows(BLOCK, 2 * KV_WIDTH),
                   pl.BlockSpec((1, LANES), lambda i: (0, 0))],
        out_shape=[jax.ShapeDtypeStruct((t, ATTN_WIDTH), BF16), jax.ShapeDtypeStruct((t, 2 * KV_WIDTH), F32),
                   jax.ShapeDtypeStruct((t, 2 * KV_WIDTH), F32), jax.ShapeDtypeStruct((1, LANES), F32)],
        scratch_shapes=[pltpu.VMEM((len(PAIRS), 2 * BLOCK, BLOCK), F32), pltpu.VMEM((len(PAIRS), 2 * BLOCK, BLOCK), F32),
                        pltpu.VMEM((len(PAIRS), 2 * BLOCK, 2 * BLOCK), BF16),
                        pltpu.VMEM((len(PAIRS), 2 * BLOCK, 2 * BLOCK), BF16)],
        semantics=("arbitrary",))


def _mix_in_bwd(dq, dkv_own, dkv_prev, dpooled, dgate, h, g, win_t, dh_res, tm, carry=None):
    t, d = h.shape
    nt = t // tm

    def body(dq_ref, own_ref, prev_ref, prev_next_ref, dpool_ref, halo_ref, dgate_ref, h_ref, g_ref, w_ref, res_ref,
             dproj_ref, dh_ref, dhb_ref, dg_ref):
        i = pl.program_id(0)
        last = i == nt - 1
        dproj_ref[:, 0:OFF_KV] = dq_ref[...]
        from_next = jnp.where(last, 0.0, prev_next_ref[...])
        if tm > BLOCK:
            from_next = jnp.concatenate([prev_ref[BLOCK:tm, :], from_next], axis=0)
        dproj_ref[:, OFF_KV:OFF_Z] = (own_ref[...] + from_next).astype(BF16)
        halo = jnp.where(last, 0.0, halo_ref[...])
        for gi, width in enumerate(POOL_WINDOWS):
            lo, hi = gi * POOL_GROUP, (gi + 1) * POOL_GROUP
            dpg = dpool_ref[:, lo:hi]
            scaled = jnp.concatenate([dpg / _pool_counts(tm, width), halo[:, lo:hi] / float(width)], axis=0)
            dz = _leading_sums(scaled, gi)[0:tm, :] - dpg
            dproj_ref[:, OFF_Z + lo:OFF_Z + hi] = dz.astype(BF16)
        dproj_ref[:, OFF_GATE:IN_WIDTH] = dgate_ref[...]
        du = _nn(dproj_ref[...], w_ref[...])
        xh, r = _rms_fwd(h_ref[...], g_ref[...])
        dx, dg = _rms_bwd(du, xh, r, g_ref[...])
        dh = res_ref[...] + dx
        dh_ref[...] = dh
        dhb_ref[...] = dh.astype(BF16)

        @pl.when(i == 0)
        def _():
            dg_ref[...] = jnp.zeros_like(dg_ref)

        dg_ref[...] += dg

    per = tm // BLOCK
    next_block = pl.BlockSpec((BLOCK, 2 * KV_WIDTH), lambda i: (jnp.minimum((i + 1) * per, t // BLOCK - 1), 0))
    next_halo = pl.BlockSpec((HALO, POOL_WIDTH), lambda i: (jnp.minimum((i + 1) * (tm // HALO), t // HALO - 1), 0))
    return _launch(
        body, (dq, dkv_own, dkv_prev, dkv_prev, dpooled, dpooled, dgate, h, g, win_t, dh_res), carry,
        name="mix_in_bwd", grid=(nt,),
        in_specs=[_rows(tm, ATTN_WIDTH), _rows(tm, 2 * KV_WIDTH), _rows(tm, 2 * KV_WIDTH), next_block,
                  _rows(tm, POOL_WIDTH), next_halo, _rows(tm, 2 * d), _rows(tm, d), _resident((1, d)),
                  _resident((IN_WIDTH, d)), _rows(tm, d)],
        out_specs=[_rows(tm, IN_WIDTH), _rows(tm, d), _rows(tm, d), pl.BlockSpec((1, d), lambda i: (0, 0))],
        out_shape=[jax.ShapeDtypeStruct((t, IN_WIDTH), BF16), jax.ShapeDtypeStruct((t, d), F32),
                   jax.ShapeDtypeStruct((t, d), BF16), jax.ShapeDtypeStruct((1, d), F32)],
        semantics=("arbitrary",))


BIG = (("wup1_t", "ffn1_w_up", True), ("wdown1", "ffn1_w_down", False), ("win_t", "w_in", True),
       ("wattn", "w_attn_up", False), ("wpool_t", "w_pool_up", True), ("wout", "w_out", False),
       ("wup2_t", "ffn2_w_up", True), ("wdown2", "ffn2_w_down", False))
ANY = pl.BlockSpec(memory_space=pl.ANY)
WIRE = BF16


def _place():
    return lax.axis_index("x"), lax.axis_index("y"), lax.axis_index("c")


def _peer(k):
    x, y, c = _place()
    return x ^ (k >> 2), y ^ ((k >> 1) & 1), c ^ (k & 1)


def _index(px, py, pc):
    return 4 * px + 2 * py + pc


def _gather_carry(shards):
    n = len(shards)

    def tools(ins, outs, sems):
        send_sems, recv_sems, local_sems = sems
        x, y, c = _place()
        chips = [(1 - x, y), (x, 1 - y), (1 - x, 1 - y)]

        def rows(w, px, py, pc):
            r = ins[w].shape[0]
            return outs[w].at[pl.ds(_index(px, py, pc) * r, r), :]

        def copy(w, k, block, to, src=None):
            return pltpu.make_async_remote_copy(
                src_ref=rows(w, *block) if src is None else src, dst_ref=rows(w, *block),
                send_sem=send_sems.at[w, k], recv_sem=recv_sems.at[w, k], device_id=to, device_id_type=MESH)

        def own(w):
            return ([pltpu.make_async_copy(ins[w], rows(w, x, y, c), local_sems.at[w]),
                     copy(w, 0, (x, y, c), (x, y, 1 - c), src=ins[w])]
                    + [copy(w, 1 + j, (x, y, c), (*chip, c), src=ins[w]) for j, chip in enumerate(chips)])

        def passed(w, j):
            return copy(w, 4 + j, (*chips[j], c), (x, y, 1 - c))

        return (x, y, c), chips, copy, own, passed

    def start(ins, outs, sems):
        _, _, _, own, _ = tools(ins, outs, sems)
        for w in range(n):
            for cp in own(w):
                cp.start()

    def forward(w):
        def run(ins, outs, sems):
            (x, y, c), chips, copy, _, passed = tools(ins, outs, sems)
            for j, chip in enumerate(chips):
                copy(w, 1 + j, (*chip, c), (x, y, c)).wait_recv()
                passed(w, j).start()
        return run

    sizes = np.cumsum([s.size for s in shards]) / sum(s.size for s in shards)
    middles = [(float(sizes[w]), forward(w)) for w in range(n)]

    def finish(ins, outs, sems):
        (x, y, c), chips, copy, own, passed = tools(ins, outs, sems)
        for w in range(n):
            copy(w, 0, (x, y, 1 - c), (x, y, c)).wait_recv()
            for j, chip in enumerate(chips):
                copy(w, 4 + j, (*chip, 1 - c), (x, y, c)).wait_recv()
        for w in range(n):
            mine, *sent = own(w)
            for cp in sent + [passed(w, j) for j in range(len(chips))]:
                cp.wait_send()
            mine.wait()

    return _Carry(
        shards, [jax.ShapeDtypeStruct((N_DEV * s.shape[0], s.shape[1]), s.dtype) for s in shards],
        [pltpu.SemaphoreType.DMA((n, N_DEV - 1)), pltpu.SemaphoreType.DMA((n, N_DEV - 1)),
         pltpu.SemaphoreType.DMA((n,))], start, finish, middles)


def _scatter_carry(grads):
    n = len(grads)

    def tools(ins, outs, sems):
        send_sems, recv_sems, local_sems = sems
        me = _index(*_place())

        def block(ref, dev):
            r = ref.shape[0] // N_DEV
            return ref.at[pl.ds(dev * r, r), :]

        def copy(w, k, landing):
            to = _peer(k)
            return pltpu.make_async_remote_copy(
                src_ref=block(ins[w], _index(*to)), dst_ref=block(outs[w], landing), send_sem=send_sems.at[w, k - 1],
                recv_sem=recv_sems.at[w, k - 1], device_id=to, device_id_type=MESH)

        def mine(w):
            return pltpu.make_async_copy(block(ins[w], me), block(outs[w], me), local_sems.at[w])

        return me, copy, mine

    def start(ins, outs, sems):
        me, copy, mine = tools(ins, outs, sems)
        for w in range(n):
            mine(w).start()
            for k in range(1, N_DEV):
                copy(w, k, me).start()

    def finish(ins, outs, sems):
        _, copy, mine = tools(ins, outs, sems)
        for w in range(n):
            for k in range(1, N_DEV):
                copy(w, k, _index(*_peer(k))).wait()
            mine(w).wait()

    return _Carry(
        grads, [jax.ShapeDtypeStruct(g.shape, g.dtype) for g in grads],
        [pltpu.SemaphoreType.DMA((n, N_DEV - 1)), pltpu.SemaphoreType.DMA((n, N_DEV - 1)),
         pltpu.SemaphoreType.DMA((n,))], start, finish)


def _small_carry(small):
    srows = small.shape[0]

    def tools(ins, outs, sems):
        send_sems, recv_sems, local_sem = sems
        me = _index(*_place())

        def slot(dev):
            return outs[0].at[pl.ds(dev * srows, srows), :]

        def copy(k, landing):
            return pltpu.make_async_remote_copy(
                src_ref=ins[0], dst_ref=slot(landing), send_sem=send_sems.at[k - 1], recv_sem=recv_sems.at[k - 1],
                device_id=_peer(k), device_id_type=MESH)

        return me, copy, pltpu.make_async_copy(ins[0], slot(me), local_sem)

    def start(ins, outs, sems):
        me, copy, mine = tools(ins, outs, sems)
        mine.start()
        for k in range(1, N_DEV):
            copy(k, me).start()

    def finish(ins, outs, sems):
        _, copy, mine = tools(ins, outs, sems)
        for k in range(1, N_DEV):
            copy(k, _index(*_peer(k))).wait()
        mine.wait()

    return _Carry([small], [jax.ShapeDtypeStruct((N_DEV * srows, LANES), small.dtype)],
                  [pltpu.SemaphoreType.DMA((N_DEV - 1,)), pltpu.SemaphoreType.DMA((N_DEV - 1,)),
                   pltpu.SemaphoreType.DMA], start, finish)


def _exchange(carry, name):
    ci = len(carry.inputs)
    co = len(carry.out_shape)

    def body(*refs):
        parts = refs[:ci], refs[ci:ci + co], refs[ci + co:]
        carry.start(*parts)
        for _, fn in carry.middles:
            fn(*parts)
        carry.finish(*parts)

    return list(pl.pallas_call(body, name=name, in_specs=[ANY] * ci, out_specs=[ANY] * co, out_shape=carry.out_shape,
                               scratch_shapes=carry.scratch)(*carry.inputs))


def _adamw_math(w, g, m, v):
    m = ADAM_B1 * m + (1.0 - ADAM_B1) * g
    v = ADAM_B2 * v + (1.0 - ADAM_B2) * (g * g)
    m_hat = m / (1.0 - ADAM_B1 ** ADAM_STEP)
    v_hat = v / (1.0 - ADAM_B2 ** ADAM_STEP)
    return -ADAM_LR * (m_hat / (jnp.sqrt(v_hat) + ADAM_EPS) + ADAM_WD * w), m, v


def _sum_adamw(got, w, m, v, transposed, name):
    parts = list(got) if isinstance(got, (list, tuple)) else [got]
    r = parts[0].shape[0] // N_DEV
    cols = sum(part.shape[1] for part in parts)
    if transposed:
        (only,) = parts
        tile = cols if cols <= 512 else 256
        got_specs = [pl.BlockSpec((N_DEV, r, tile), lambda i: (0, 0, i))]
        spec, steps = pl.BlockSpec((tile, r), lambda i: (i, 0)), cols // tile
    else:
        tile = r if r <= 256 else r // 2
        got_specs = [pl.BlockSpec((N_DEV, tile, part.shape[1]), lambda i: (0, i, 0)) for part in parts]
        spec, steps = pl.BlockSpec((tile, cols), lambda i: (i, 0)), r // tile
    n = len(parts)

    def body(*refs):
        w_ref, m_ref, v_ref, g_ref, d_ref, m2_ref, v2_ref = refs[n:]
        sums = []
        for got_ref in refs[:n]:
            acc = got_ref[0].astype(F32)
            for dev in range(1, N_DEV):
                acc = acc + got_ref[dev].astype(F32)
            sums.append(acc)
        g = sums[0].T if transposed else (sums[0] if n == 1 else jnp.concatenate(sums, axis=1))
        g_ref[...] = g
        d_ref[...], m2_ref[...], v2_ref[...] = _adamw_math(w_ref[...], g, m_ref[...], v_ref[...])

    return pl.pallas_call(
        body, name=name, grid=(steps,), in_specs=got_specs + [spec, spec, spec], out_specs=[spec] * 4,
        out_shape=[jax.ShapeDtypeStruct(w.shape, F32)] * 4, compiler_params=_params("parallel"),
    )(*[part.reshape(N_DEV, r, part.shape[1]) for part in parts], w, m, v)


def _small_update(early, late, w, m, v):
    rows = w.shape[0]

    def body(early_ref, late_ref, w_ref, m_ref, v_ref, g_ref, d_ref, m2_ref, v2_ref):
        sums = []
        for ref in (early_ref, late_ref):
            acc = ref[0]
            for dev in range(1, N_DEV):
                acc = acc + ref[dev]
            sums.append(acc)
        g = jnp.concatenate(sums, axis=0)
        g_ref[...] = g
        d_ref[...], m2_ref[...], v2_ref[...] = _adamw_math(w_ref[...], g, m_ref[...], v_ref[...])

    return pl.pallas_call(
        body, name="small_update", out_shape=[jax.ShapeDtypeStruct((rows, LANES), F32)] * 4,
        compiler_params=pltpu.CompilerParams(vmem_limit_bytes=VMEM_LIMIT),
    )(early.reshape(N_DEV, -1, LANES), late.reshape(N_DEV, -1, LANES), w, m, v)


SMALL = (("pool_w_mix", 512), ("mix_norm", 8), ("ffn2_norm", 8), ("final_norm", 8), ("pool_scale", 8), ("sinks", 8),
         ("loss", 8), ("ffn1_norm", 8))
EARLY, LATE = SMALL[:-1], SMALL[-1:]


def _pack_small(parts, layout=SMALL):
    out = []
    for name, rows in layout:
        flat = parts[name].astype(F32).reshape(-1)
        out.append(jnp.pad(flat, (0, rows * LANES - flat.shape[0])).reshape(rows, LANES))
    return jnp.concatenate(out, axis=0)


def _unpack_small(packed, shapes):
    out, row = {}, 0
    for name, rows in SMALL:
        shape = shapes[name]
        size = int(np.prod(shape)) if shape else 1
        out[name] = packed[row:row + rows].reshape(-1)[:size].reshape(shape)
        row += rows
    return out


def kernel(x, ffn1_norm, ffn1_w_up, ffn1_w_down, mix_norm, w_in, sinks, w_attn_up, pool_w_mix, pool_scale, w_pool_up, w_out, ffn2_norm, ffn2_w_up, ffn2_w_down, final_norm, loss_target, m_ffn1_norm, m_ffn1_w_up, m_ffn1_w_down, m_mix_norm, m_w_in, m_sinks, m_w_attn_up, m_pool_w_mix, m_pool_scale, m_w_pool_up, m_w_out, m_ffn2_norm, m_ffn2_w_up, m_ffn2_w_down, m_final_norm, v_ffn1_norm, v_ffn1_w_up, v_ffn1_w_down, v_mix_norm, v_w_in, v_sinks, v_w_attn_up, v_pool_w_mix, v_pool_scale, v_w_pool_up, v_w_out, v_ffn2_norm, v_ffn2_w_up, v_ffn2_w_down, v_final_norm):
    args = dict(locals())
    weight_names = ("ffn1_norm", "ffn1_w_up", "ffn1_w_down", "mix_norm", "w_in", "sinks", "w_attn_up", "pool_w_mix",
                    "pool_scale", "w_pool_up", "w_out", "ffn2_norm", "ffn2_w_up", "ffn2_w_down", "final_norm")

    shard = {k: (args[p][0].T if tr else args[p][0]).astype(BF16) for k, p, tr in BIG}
    big = {"wup1_t": _exchange(_gather_carry([shard["wup1_t"]]), "gather_up1")[0]}

    def gathering(keys):
        return _gather_carry([shard[k] for k in keys])

    xs, target = x[0], loss_target[0]
    t = xs.shape[0]
    tm_f, tm_b, tk = min(512, t), min(512, t), min(1024, t)
    g1, gm, g2, gf = ffn1_norm, mix_norm, ffn2_norm, final_norm.reshape(1, D_MODEL)
    dist = _attn_dist()
    sink_v = sinks.reshape(N_Q_HEADS)
    wmix_b = pool_w_mix[0].astype(BF16)

    (n1, ab1, act1), (big["wdown1"], big["win_t"]) = _ffn_up(xs, g1, big["wup1_t"], tm_f, gathering(["wdown1", "win_t"]))
    (h1,), (big["wattn"], big["wpool_t"], big["wout"]) = _ffn_down(xs, act1, big["wdown1"], tm_f,
                                                                   gathering(["wattn", "wpool_t", "wout"]))
    (u, q, kv, z, gate), (big["wup2_t"],) = _mix_in_fwd(h1, gm, big["win_t"], tm_f, gathering(["wup2_t"]))
    attn = _attn_fwd(q, kv, dist, sink_v)
    (h2, a, p, merged, ms, pooled), (big["wdown2"],) = _mix_out_fwd(
        attn, z, gate, h1, big["wattn"], wmix_b, pool_scale, big["wpool_t"], big["wout"], tm_b, gathering(["wdown2"]))
    ab2, n2, act2, loss_lanes, dh3, dhb3, dgf = _ffn_loss(h2, g2, big["wup2_t"], big["wdown2"], gf, target, tm_f)

    got = {}
    (gw_down2,), _ = _wgrad(act2, dhb3, 0.5, D_FF, tk, "wgrad_down2")
    (dab2,), (got["wdown2"],) = _ffn_bwd_hidden(dhb3, ab2, big["wdown2"], tm_f, _scatter_carry([gw_down2]))
    (dh2, dg2), _ = _ffn_bwd_input(dab2, dh3, h2, g2, big["wup2_t"], tm_f)
    (gw_up2,), _ = _wgrad(dab2, n2, 1.0, D_FF, tk, "wgrad_up2")
    dhb2, da_b, dp_b, dattn, dgate, dpooled, dwmix, dscale = _mix_out_bwd(
        dh2, gate, a, p, pooled, big["wattn"], wmix_b, pool_scale, big["wpool_t"], big["wout"], tm_b)
    (gw_out,), _ = _wgrad(merged, dhb2, 1.0, D_MODEL, tk, "wgrad_out")
    (gw_attn,), _ = _wgrad(attn, da_b, 1.0, D_MODEL, tk, "wgrad_attn")
    (gw_pool,), _ = _wgrad(dp_b, ms, 1.0, D_MODEL, tk, "wgrad_pool")
    (dq, dkv_own, dkv_prev, dsinks), (got["wup2_t"],) = _attn_bwd(q, kv, dattn, dist, sink_v, _scatter_carry([gw_up2]))
    (dproj, dh1, dhb1, dgm), (got["wout"], got["wattn"], got["wpool_t"]) = _mix_in_bwd(
        dq, dkv_own, dkv_prev, dpooled, dgate, h1, gm, big["win_t"], dh2, tm_b,
        _scatter_carry([gw_out, gw_attn, gw_pool]))
    (gw_down1,), _ = _wgrad(act1, dhb1, 0.5, D_FF, tk, "wgrad_down1")
    (gw_in,), (got["wdown1"],) = _wgrad(dproj, u, 1.0, IN_WIDTH // 2, tk, "wgrad_in", _scatter_carry([gw_down1]))
    (dab1,), (got["win_t"],) = _ffn_bwd_hidden(dhb1, ab1, big["wdown1"], tm_f, _scatter_carry([gw_in]))
    small_parts = {"pool_w_mix": dwmix, "mix_norm": dgm, "ffn2_norm": dg2, "final_norm": dgf, "pool_scale": dscale,
                   "sinks": dsinks[:, :N_Q_HEADS], "loss": loss_lanes[:, :1]}
    (gw_up1,), (small_early,) = _wgrad(dab1, n1, 1.0, D_FF, tk, "wgrad_up1",
                                       _small_carry(_pack_small(small_parts, EARLY)))
    (dx, dg1), (got["wup1_t"],) = _ffn_bwd_input(dab1, dh1, xs, g1, big["wup1_t"], tm_f, _scatter_carry([gw_up1]))
    (small_late,) = _exchange(_small_carry(_pack_small({"ffn1_norm": dg1}, LATE)), "gather_small")

    grad, delta, new_m, new_v = {}, {}, {}, {}
    for k, p, tr in BIG:
        outside = tr and args[p].shape[-1] % LANES != 0
        turn = (lambda a: a.T) if outside else (lambda a: a)
        res = _sum_adamw(got[k], turn(args[p][0]), turn(args["m_" + p][0]), turn(args["v_" + p][0]),
                         tr and not outside, "adamw_" + k)
        grad[p], delta[p], new_m[p], new_v[p] = (turn(a)[None] for a in res)

    shapes = {name: args[name].shape for name, _ in SMALL if name != "loss"}
    shapes["loss"] = ()
    packed = {pre: _pack_small({**{name: args[pre + name] for name, _ in SMALL if name != "loss"},
                                "loss": jnp.zeros((), F32)}) for pre in ("", "m_", "v_")}
    g_s, d_s, m_s, v_s = _small_update(small_early, small_late, packed[""], packed["m_"], packed["v_"])
    g_small, d_small, m_small, v_small = (_unpack_small(a, shapes) for a in (g_s, d_s, m_s, v_s))
    for name, _ in SMALL:
        if name != "loss":
            grad[name], delta[name], new_m[name], new_v[name] = (
                g_small[name], d_small[name], m_small[name], v_small[name])

    return (g_small["loss"], dx[None], *[grad[n] for n in weight_names], *[delta[n] for n in weight_names],
            *[new_m[n] for n in weight_names], *[new_v[n] for n in weight_names])
```

```python
import functools

import jax
import jax.numpy as jnp
import numpy as np
from jax import lax
from jax.experimental import pallas as pl
from jax.experimental.pallas import tpu as pltpu

F32 = jnp.float32
BF16 = jnp.bfloat16

D_MODEL = 1024
D_FF = 2816
N_Q_HEADS = 16
N_KV_HEADS = 2
Q_PER_KV = N_Q_HEADS // N_KV_HEADS
HEAD_DIM = 64
BLOCK = 128
ATTN_WIDTH = N_Q_HEADS * HEAD_DIM
KV_WIDTH = N_KV_HEADS * HEAD_DIM
POOL_WINDOWS = (2, 4, 8, 16)
POOL_GROUP = 128
POOL_WIDTH = 512
HALO = 16
IN_WIDTH = ATTN_WIDTH + 2 * KV_WIDTH + POOL_WIDTH + 2 * D_MODEL
OFF_KV = ATTN_WIDTH
OFF_Z = ATTN_WIDTH + 2 * KV_WIDTH
OFF_GATE = OFF_Z + POOL_WIDTH
NORM_EPS = 1e-6
ADAM_LR = 0.001
ADAM_B1 = 0.9
ADAM_B2 = 0.999
ADAM_EPS = 1e-08
ADAM_WD = 0.01
ADAM_STEP = 10

N_DEV = 8
N_CHIP = 4
LANES = 128
FF_CHUNK = 256
SLAB = 32
GATHER_PIECES = 2
VMEM_LIMIT = 56 * 1024 * 1024
MESH = pl.DeviceIdType.MESH


def _nn(a, b):
    return jnp.dot(a, b, preferred_element_type=F32)


def _nt(a, b):
    return lax.dot_general(a, b, (((1,), (1,)), ((), ())), preferred_element_type=F32)


def _tn(a, b):
    return lax.dot_general(a, b, (((0,), (0,)), ((), ())), preferred_element_type=F32)


def _params(*sem):
    return pltpu.CompilerParams(dimension_semantics=sem, vmem_limit_bytes=VMEM_LIMIT)


def _resident(shape):
    return pl.BlockSpec(shape, lambda *_: (0,) * len(shape), pipeline_mode=pl.Buffered(1))


def _rows(tm, cols):
    return pl.BlockSpec((tm, cols), lambda i: (i, 0))


class _Carry:
    def __init__(self, inputs, out_shape, scratch, start, finish, middles=()):
        self.inputs, self.out_shape, self.scratch = list(inputs), list(out_shape), list(scratch)
        self.start, self.finish, self.middles = start, finish, list(middles)


def _launch(body, args, carry=None, *, name, grid, in_specs, out_specs, out_shape, scratch_shapes=(), semantics):
    in_specs, out_specs, out_shape, scratch_shapes = list(in_specs), list(out_specs), list(out_shape), list(scratch_shapes)
    if carry is None:
        res = pl.pallas_call(body, name=name, grid=grid, in_specs=in_specs, out_specs=out_specs, out_shape=out_shape,
                             scratch_shapes=scratch_shapes, compiler_params=_params(*semantics))(*args)
        return list(res), []
    ni, no, ns = len(in_specs), len(out_specs), len(scratch_shapes)
    ci, co = len(carry.inputs), len(carry.out_shape)
    total = int(np.prod(grid))

    def full(*refs):
        own_in, c_in = refs[:ni], refs[ni:ni + ci]
        own_out, c_out = refs[ni + ci:ni + ci + no], refs[ni + ci + no:ni + ci + no + co]
        own_scr, c_sem = refs[ni + ci + no + co:ni + ci + no + co + ns], refs[ni + ci + no + co + ns:]
        step = 0
        for axis, size in enumerate(grid):
            step = step * size + pl.program_id(axis)
        pl.when(step == 0)(lambda: carry.start(c_in, c_out, c_sem))
        for fraction, fn in carry.middles:
            at = min(total - 1, int(fraction * total) + 1)
            pl.when(step == at)(lambda fn=fn: fn(c_in, c_out, c_sem))
        body(*own_in, *own_out, *own_scr)
        pl.when(step == total - 1)(lambda: carry.finish(c_in, c_out, c_sem))

    res = pl.pallas_call(
        full, name=name, grid=grid, in_specs=in_specs + [ANY] * ci, out_specs=out_specs + [ANY] * co,
        out_shape=out_shape + carry.out_shape, scratch_shapes=scratch_shapes + carry.scratch,
        compiler_params=_params(*(["arbitrary"] * len(grid))),
    )(*args, *carry.inputs)
    return list(res[:no]), list(res[no:])


def _rms_fwd(xv, g):
    r = lax.rsqrt(jnp.mean(xv * xv, axis=-1, keepdims=True) + NORM_EPS)
    return xv * r, r


def _rms_bwd(dn, xh, r, g):
    dxh = dn * g
    dx = r * (dxh - xh * jnp.mean(dxh * xh, axis=-1, keepdims=True))
    return dx, jnp.sum(dn * xh, axis=0, keepdims=True)


def _ffn_loss(x, g, wup_t, wdown, gf, target, tm):
    t, d = x.shape
    f = wdown.shape[0]

    def body(x_ref, g_ref, wup_ref, wdn_ref, gf_ref, tgt_ref, ab_ref, n_ref, act_ref, loss_ref, dh_ref, dhb_ref, dg_ref):
        xv = x_ref[...]
        xh, _ = _rms_fwd(xv, g_ref[...])
        n = (xh * g_ref[...]).astype(BF16)
        n_ref[...] = n
        for c in range(f // FF_CHUNK):
            lo, hi = c * FF_CHUNK, (c + 1) * FF_CHUNK
            a = _nt(n, wup_ref[lo:hi, :])
            b = _nt(n, wup_ref[f + lo:f + hi, :])
            ab_ref[:, lo:hi] = a.astype(BF16)
            ab_ref[:, f + lo:f + hi] = b.astype(BF16)
            act_ref[:, lo:hi] = (a * jax.nn.sigmoid(a) * b).astype(BF16)
        h = xv + 0.5 * _nn(act_ref[...], wdn_ref[...])
        yh, r = _rms_fwd(h, gf_ref[...])
        err = yh * gf_ref[...] - tgt_ref[...]
        part = 0.5 * jnp.sum(jnp.mean(err * err, axis=-1, keepdims=True), axis=0, keepdims=True)
        dh, dg = _rms_bwd(err * (1.0 / d), yh, r, gf_ref[...])
        dh_ref[...] = dh
        dhb_ref[...] = dh.astype(BF16)

        @pl.when(pl.program_id(0) == 0)
        def _():
            dg_ref[...] = jnp.zeros_like(dg_ref)
            loss_ref[...] = jnp.zeros_like(loss_ref)

        dg_ref[...] += dg
        loss_ref[...] += jnp.broadcast_to(part, loss_ref.shape)

    return pl.pallas_call(
        body, name="ffn_loss", grid=(t // tm,),
        in_specs=[_rows(tm, d), _resident((1, d)), _resident((2 * f, d)), _resident((f, d)), _resident((1, d)),
                  _rows(tm, d)],
        out_specs=[_rows(tm, 2 * f), _rows(tm, d), _rows(tm, f), pl.BlockSpec((1, LANES), lambda i: (0, 0)),
                   _rows(tm, d), _rows(tm, d), pl.BlockSpec((1, d), lambda i: (0, 0))],
        out_shape=[jax.ShapeDtypeStruct((t, 2 * f), BF16), jax.ShapeDtypeStruct((t, d), BF16),
                   jax.ShapeDtypeStruct((t, f), BF16), jax.ShapeDtypeStruct((1, LANES), F32),
                   jax.ShapeDtypeStruct((t, d), F32), jax.ShapeDtypeStruct((t, d), BF16),
                   jax.ShapeDtypeStruct((1, d), F32)],
        compiler_params=_params("arbitrary"),
    )(x, g, wup_t, wdown, gf, target)


def _ffn_up(x, g, wup_t, tm, carry=None):
    t, d = x.shape
    f = wup_t.shape[0] // 2

    def body(x_ref, g_ref, wup_ref, n_ref, ab_ref, act_ref):
        xh, _ = _rms_fwd(x_ref[...], g_ref[...])
        n = (xh * g_ref[...]).astype(BF16)
        n_ref[...] = n
        for c in range(f // FF_CHUNK):
            lo, hi = c * FF_CHUNK, (c + 1) * FF_CHUNK
            a = _nt(n, wup_ref[lo:hi, :])
            b = _nt(n, wup_ref[f + lo:f + hi, :])
            ab_ref[:, lo:hi] = a.astype(BF16)
            ab_ref[:, f + lo:f + hi] = b.astype(BF16)
            act_ref[:, lo:hi] = (a * jax.nn.sigmoid(a) * b).astype(BF16)

    return _launch(
        body, (x, g, wup_t), carry, name="ffn_up", grid=(t // tm,),
        in_specs=[_rows(tm, d), _resident((1, d)), _resident((2 * f, d))],
        out_specs=[_rows(tm, d), _rows(tm, 2 * f), _rows(tm, f)],
        out_shape=[jax.ShapeDtypeStruct((t, d), BF16), jax.ShapeDtypeStruct((t, 2 * f), BF16),
                   jax.ShapeDtypeStruct((t, f), BF16)],
        semantics=("parallel",))


def _ffn_down(x, act, wdown, tm, carry=None):
    t, d = x.shape
    f = wdown.shape[0]

    def body(x_ref, act_ref, wdn_ref, h_ref):
        h_ref[...] = x_ref[...] + 0.5 * _nn(act_ref[...], wdn_ref[...])

    return _launch(
        body, (x, act, wdown), carry, name="ffn_down", grid=(t // tm,),
        in_specs=[_rows(tm, d), _rows(tm, f), _resident((f, d))], out_specs=[_rows(tm, d)],
        out_shape=[jax.ShapeDtypeStruct((t, d), F32)], semantics=("parallel",))


def _ffn_bwd_hidden(dhb, ab, wdown, tm, carry=None):
    t, d = dhb.shape
    f = wdown.shape[0]

    def body(dh_ref, ab_ref, wdn_ref, dab_ref, dact_ref):
        half = dh_ref[...] * 0.5
        for c in range(f // FF_CHUNK):
            lo, hi = c * FF_CHUNK, (c + 1) * FF_CHUNK
            dact_ref[...] = _nt(half, wdn_ref[lo:hi, :])

            def slab(i, carry_):
                rows = pl.ds(pl.multiple_of(i * SLAB, SLAB), SLAB)
                a = ab_ref[rows, lo:hi].astype(F32)
                b = ab_ref[rows, f + lo:f + hi].astype(F32)
                s = jax.nn.sigmoid(a)
                ds_ = dact_ref[rows, :] * s
                dab_ref[rows, lo:hi] = (ds_ * b * (1.0 + a * (1.0 - s))).astype(BF16)
                dab_ref[rows, f + lo:f + hi] = (ds_ * a).astype(BF16)
                return carry_

            lax.fori_loop(0, tm // SLAB, slab, 0, unroll=True)

    return _launch(
        body, (dhb, ab, wdown), carry, name="ffn_bwd_hidden", grid=(t // tm,),
        in_specs=[_rows(tm, d), _rows(tm, 2 * f), _resident((f, d))], out_specs=[_rows(tm, 2 * f)],
        out_shape=[jax.ShapeDtypeStruct((t, 2 * f), BF16)],
        scratch_shapes=[pltpu.VMEM((tm, FF_CHUNK), F32)], semantics=("parallel",))


def _ffn_bwd_input(dab, dh, x, g, wup_t, tm, carry=None):
    t, d = x.shape
    f2 = wup_t.shape[0]

    def body(dab_ref, dh_ref, x_ref, g_ref, wup_ref, dx_ref, dg_ref):
        dn = _nn(dab_ref[...], wup_ref[...])
        xh, r = _rms_fwd(x_ref[...], g_ref[...])
        dx, dg = _rms_bwd(dn, xh, r, g_ref[...])
        dx_ref[...] = dh_ref[...] + dx

        @pl.when(pl.program_id(0) == 0)
        def _():
            dg_ref[...] = jnp.zeros_like(dg_ref)

        dg_ref[...] += dg

    return _launch(
        body, (dab, dh, x, g, wup_t), carry, name="ffn_bwd_input", grid=(t // tm,),
        in_specs=[_rows(tm, f2), _rows(tm, d), _rows(tm, d), _resident((1, d)), _resident((f2, d))],
        out_specs=[_rows(tm, d), pl.BlockSpec((1, d), lambda i: (0, 0))],
        out_shape=[jax.ShapeDtypeStruct((t, d), F32), jax.ShapeDtypeStruct((1, d), F32)],
        semantics=("arbitrary",))


def _wgrad(lhs, rhs, scale, bm, tk, name, carry=None):
    t, m = lhs.shape
    n = rhs.shape[1]
    steps = t // tk
    chunk = bm if bm <= 2048 else bm // 2

    def body(l_ref, r_ref, o_ref, acc_ref):
        @pl.when(pl.program_id(1) == 0)
        def _():
            acc_ref[...] = jnp.zeros_like(acc_ref)

        for lo in range(0, bm, chunk):
            acc_ref[lo:lo + chunk, :] += _tn(l_ref[:, lo:lo + chunk], r_ref[...])

        @pl.when(pl.program_id(1) == steps - 1)
        def _():
            o_ref[...] = (scale * acc_ref[...]).astype(o_ref.dtype)

    return _launch(
        body, (lhs, rhs), carry, name=name, grid=(m // bm, steps),
        in_specs=[pl.BlockSpec((tk, bm), lambda i, k: (k, i)), pl.BlockSpec((tk, n), lambda i, k: (k, 0))],
        out_specs=[pl.BlockSpec((bm, n), lambda i, k: (i, 0))],
        out_shape=[jax.ShapeDtypeStruct((m, n), WIRE)],
        scratch_shapes=[pltpu.VMEM((bm, n), F32)], semantics=("parallel", "arbitrary"))


def _mix_in_fwd(h, g, win_t, tm, carry=None):
    t, d = h.shape

    def body(h_ref, g_ref, w_ref, u_ref, q_ref, kv_ref, z_ref, gate_ref):
        xh, _ = _rms_fwd(h_ref[...], g_ref[...])
        u = (xh * g_ref[...]).astype(BF16)
        u_ref[...] = u
        q_ref[...] = _nt(u, w_ref[0:OFF_KV, :]).astype(BF16)
        kv_ref[...] = _nt(u, w_ref[OFF_KV:OFF_Z, :]).astype(BF16)
        z_ref[...] = _nt(u, w_ref[OFF_Z:OFF_GATE, :])
        gate_ref[...] = _nt(u, w_ref[OFF_GATE:IN_WIDTH, :]).astype(BF16)

    return _launch(
        body, (h, g, win_t), carry, name="mix_in_fwd", grid=(t // tm,),
        in_specs=[_rows(tm, d), _resident((1, d)), _resident((IN_WIDTH, d))],
        out_specs=[_rows(tm, d), _rows(tm, ATTN_WIDTH), _rows(tm, 2 * KV_WIDTH), _rows(tm, POOL_WIDTH),
                   _rows(tm, 2 * D_MODEL)],
        out_shape=[jax.ShapeDtypeStruct((t, d), BF16), jax.ShapeDtypeStruct((t, ATTN_WIDTH), BF16),
                   jax.ShapeDtypeStruct((t, 2 * KV_WIDTH), BF16), jax.ShapeDtypeStruct((t, POOL_WIDTH), F32),
                   jax.ShapeDtypeStruct((t, 2 * D_MODEL), BF16)],
        semantics=("parallel",))


ALIBI_SLOPES = tuple(float(s) for s in (2.0 ** (-8.0 * np.arange(1, N_Q_HEADS + 1, dtype=np.float32) / N_Q_HEADS)))


def _attn_dist():
    return jnp.asarray(((np.arange(BLOCK)[:, None] - np.arange(BLOCK)[None, :]) % BLOCK).astype(np.float32))


def _own_block():
    shape = (BLOCK, BLOCK)
    return lax.broadcasted_iota(jnp.int32, shape, 1) <= lax.broadcasted_iota(jnp.int32, shape, 0)


def _fold(band2, own):
    return jnp.where(own, band2[:, BLOCK:], band2[:, :BLOCK])


def _unfold(x, own):
    zero = jnp.zeros_like(x)
    return jnp.concatenate([jnp.where(own, zero, x), jnp.where(own, x, zero)], axis=1)


def _low_half(shape):
    return lax.broadcasted_iota(jnp.int32, shape, len(shape) - 1) < HEAD_DIM


def _both_halves(band, kv_head):
    low = _low_half(band.shape)
    swapped = pltpu.roll(band, HEAD_DIM, 1)
    return jnp.where(low, band, swapped) if kv_head == 0 else jnp.where(low, swapped, band)


def _pair_rows(ref, pair, scale=None):
    v = ref[:, LANES * pair:LANES * (pair + 1)]
    if scale is not None:
        v = v * scale
    low, zero = _low_half(v.shape), jnp.zeros_like(v)
    return jnp.concatenate([jnp.where(low, v, zero), jnp.where(low, zero, v)], axis=0)


def _per_head(even, odd):
    return jnp.where(lax.broadcasted_iota(jnp.int32, (2 * BLOCK, 1), 0) < BLOCK, even, odd)


def _twice(x):
    return jnp.concatenate([x, x], axis=0)


def _pair_scores(q_ref, kk, dist2, pair, first, own2):
    s2 = _nt(_pair_rows(q_ref, pair, HEAD_DIM ** -0.5), kk)
    before = jnp.where(first, -jnp.inf, s2[:, :BLOCK])
    slopes = _per_head(ALIBI_SLOPES[2 * pair], ALIBI_SLOPES[2 * pair + 1])
    return jnp.where(own2, s2[:, BLOCK:], before) - slopes * dist2


def _own_half(ref, head):
    v = ref[:, LANES * (head // 2):LANES * (head // 2 + 1)]
    low = _low_half(v.shape)
    return jnp.where(low if head % 2 == 0 else jnp.logical_not(low), v, jnp.zeros_like(v))


def _head_scores(q_ref, kk, dist, head, first, own):
    s2 = _nt(_own_half(q_ref, head) * HEAD_DIM ** -0.5, kk)
    before = jnp.where(first, -jnp.inf, s2[:, :BLOCK])
    return jnp.where(own, s2[:, BLOCK:], before) - ALIBI_SLOPES[head] * dist


def _heads_of(stack):
    return jnp.where(_low_half((BLOCK, LANES)), stack[:BLOCK], stack[BLOCK:])


def _softmax_sink(s, sink):
    m = jnp.maximum(jnp.max(s, axis=-1, keepdims=True), sink)
    p = jnp.exp(s - m)
    psink = jnp.exp(sink - m)
    inv = 1.0 / (jnp.sum(p, axis=-1, keepdims=True) + psink)
    return p * inv, psink * inv


def _bands(kvc_ref, kvp_ref):
    kband = jnp.concatenate([kvp_ref[:, 0:LANES], kvc_ref[:, 0:LANES]], axis=0)
    vband = jnp.concatenate([kvp_ref[:, LANES:2 * LANES], kvc_ref[:, LANES:2 * LANES]], axis=0)
    return ([_both_halves(kband, hk) for hk in range(N_KV_HEADS)],
            [_both_halves(vband, hk) for hk in range(N_KV_HEADS)])


SMEM = pl.BlockSpec(memory_space=pltpu.SMEM)
HEADS = range(N_Q_HEADS)
PAIRS = range(N_Q_HEADS // 2)
PAIRS_PER_KV = Q_PER_KV // 2


def _attn_fwd(q, kv, dist, sinks):
    t = q.shape[0]

    def body(q_ref, kvc_ref, kvp_ref, dist_ref, sink_ref, o_ref, s_scr, p_scr):
        first = pl.program_id(0) == 0
        own2 = _twice(_own_block())
        dist2 = _twice(dist_ref[...])
        kk, vv = _bands(kvc_ref, kvp_ref)
        for pair in PAIRS:
            s_scr[pair] = _pair_scores(q_ref, kk[pair // PAIRS_PER_KV], dist2, pair, first, own2)
        for pair in PAIRS:
            probs, _ = _softmax_sink(s_scr[pair], _per_head(sink_ref[2 * pair], sink_ref[2 * pair + 1]))
            p_scr[pair] = _unfold(probs.astype(BF16), own2)
        for pair in PAIRS:
            out = _nn(p_scr[pair], vv[pair // PAIRS_PER_KV])
            o_ref[:, LANES * pair:LANES * (pair + 1)] = _heads_of(out).astype(BF16)

    return pl.pallas_call(
        body, name="attn_fwd", grid=(t // BLOCK,),
        in_specs=[_rows(BLOCK, ATTN_WIDTH), _rows(BLOCK, 2 * KV_WIDTH),
                  pl.BlockSpec((BLOCK, 2 * KV_WIDTH), lambda i: (jnp.maximum(i - 1, 0), 0)),
                  _resident(dist.shape), SMEM],
        out_specs=_rows(BLOCK, ATTN_WIDTH),
        out_shape=jax.ShapeDtypeStruct((t, ATTN_WIDTH), BF16),
        scratch_shapes=[pltpu.VMEM((len(PAIRS), 2 * BLOCK, BLOCK), F32),
                        pltpu.VMEM((len(PAIRS), 2 * BLOCK, 2 * BLOCK), BF16)],
        compiler_params=_params("parallel"),
    )(q, kv, kv, dist, sinks)


def _pool_counts(tm, width):
    row = pl.program_id(0) * tm + lax.broadcasted_iota(jnp.int32, (tm, 1), 0)
    return jnp.minimum(row + 1, width).astype(F32)


def _trailing_sums(zz, group):
    s = zz
    for k in range(group + 1):
        s = s + pltpu.roll(s, 1 << k, 0)
    return s


def _leading_sums(zz, group):
    rows = zz.shape[0]
    s = zz
    for k in range(group + 1):
        s = s + pltpu.roll(s, rows - (1 << k), 0)
    return s


def _mix_out_fwd(attn, z, gate, h, wattn, wmix, scale, wpool_t, wout, tm, carry=None):
    t, d = h.shape

    def body(attn_ref, z_ref, halo_ref, gate_ref, h_ref, wattn_ref, wmix_ref, scale_ref, wpool_ref, wout_ref,
             h2_ref, a_ref, p_ref, merged_ref, ms_ref, pooled_ref):
        halo = jnp.where(pl.program_id(0) == 0, 0.0, halo_ref[...])
        for gi, width in enumerate(POOL_WINDOWS):
            lo, hi = gi * POOL_GROUP, (gi + 1) * POOL_GROUP
            zg = z_ref[:, lo:hi]
            sums = _trailing_sums(jnp.concatenate([halo[:, lo:hi], zg], axis=0), gi)[HALO:, :]
            pooled = (sums / _pool_counts(tm, width) - zg).astype(BF16)
            pooled_ref[:, lo:hi] = pooled
            ms_ref[:, lo:hi] = (_nn(pooled, wmix_ref[gi]) * scale_ref[:, lo:hi]).astype(BF16)
        p = _nt(ms_ref[...], wpool_ref[...])
        a = _nn(attn_ref[...], wattn_ref[...])
        a_ref[...] = a.astype(BF16)
        p_ref[...] = p.astype(BF16)
        merged = (jax.nn.sigmoid(gate_ref[:, 0:d].astype(F32)) * a
                  + jax.nn.sigmoid(gate_ref[:, d:2 * d].astype(F32)) * p).astype(BF16)
        merged_ref[...] = merged
        h2_ref[...] = h_ref[...] + _nn(merged, wout_ref[...])

    halo_spec = pl.BlockSpec((HALO, POOL_WIDTH), lambda i: (jnp.maximum(i * (tm // HALO) - 1, 0), 0))
    return _launch(
        body, (attn, z, z, gate, h, wattn, wmix, scale, wpool_t, wout), carry, name="mix_out_fwd", grid=(t // tm,),
        in_specs=[_rows(tm, ATTN_WIDTH), _rows(tm, POOL_WIDTH), halo_spec, _rows(tm, 2 * d), _rows(tm, d),
                  _resident(wattn.shape), _resident(wmix.shape), _resident(scale.shape), _resident(wpool_t.shape),
                  _resident(wout.shape)],
        out_specs=[_rows(tm, d), _rows(tm, d), _rows(tm, d), _rows(tm, d), _rows(tm, POOL_WIDTH),
                   _rows(tm, POOL_WIDTH)],
        out_shape=[jax.ShapeDtypeStruct((t, d), F32), jax.ShapeDtypeStruct((t, d), BF16),
                   jax.ShapeDtypeStruct((t, d), BF16), jax.ShapeDtypeStruct((t, d), BF16),
                   jax.ShapeDtypeStruct((t, POOL_WIDTH), BF16), jax.ShapeDtypeStruct((t, POOL_WIDTH), BF16)],
        semantics=("parallel",))


def _mix_out_bwd(dh, gate, a, p, pooled, wattn, wmix, scale, wpool_t, wout, tm):
    t, d = dh.shape

    def body(dh_ref, gate_ref, a_ref, p_ref, pooled_ref, wattn_ref, wmix_ref, scale_ref, wpool_ref, wout_ref,
             dhb_ref, dab_ref, dpb_ref, dattn_ref, dgate_ref, dpooled_ref, dwmix_ref, dscale_ref):
        @pl.when(pl.program_id(0) == 0)
        def _():
            dwmix_ref[...] = jnp.zeros_like(dwmix_ref)
            dscale_ref[...] = jnp.zeros_like(dscale_ref)

        dhb = dh_ref[...].astype(BF16)
        dhb_ref[...] = dhb
        dm = _nt(dhb, wout_ref[...])
        sa = jax.nn.sigmoid(gate_ref[:, 0:d].astype(F32))
        sp = jax.nn.sigmoid(gate_ref[:, d:2 * d].astype(F32))
        da = (dm * sa).astype(BF16)
        dp = (dm * sp).astype(BF16)
        dab_ref[...] = da
        dpb_ref[...] = dp
        dgate_ref[:, 0:d] = (dm * a_ref[...].astype(F32) * (sa * (1.0 - sa))).astype(BF16)
        dgate_ref[:, d:2 * d] = (dm * p_ref[...].astype(F32) * (sp * (1.0 - sp))).astype(BF16)
        dattn_ref[...] = _nt(da, wattn_ref[...]).astype(BF16)
        dms = _nn(dp, wpool_ref[...])
        for gi in range(len(POOL_WINDOWS)):
            lo, hi = gi * POOL_GROUP, (gi + 1) * POOL_GROUP
            pooled_g = pooled_ref[:, lo:hi]
            mixed = _nn(pooled_g, wmix_ref[gi])
            dscale_ref[:, lo:hi] += jnp.sum(dms[:, lo:hi] * mixed, axis=0, keepdims=True)
            dmixed = (dms[:, lo:hi] * scale_ref[:, lo:hi]).astype(BF16)
            dwmix_ref[gi] += _tn(pooled_g, dmixed)
            dpooled_ref[:, lo:hi] = _nt(dmixed, wmix_ref[gi])

    acc = lambda shape: pl.BlockSpec(shape, lambda i: (0,) * len(shape))
    return pl.pallas_call(
        body, name="mix_out_bwd", grid=(t // tm,),
        in_specs=[_rows(tm, d), _rows(tm, 2 * d), _rows(tm, d), _rows(tm, d), _rows(tm, POOL_WIDTH),
                  _resident(wattn.shape), _resident(wmix.shape), _resident(scale.shape), _resident(wpool_t.shape),
                  _resident(wout.shape)],
        out_specs=[_rows(tm, d), _rows(tm, d), _rows(tm, d), _rows(tm, ATTN_WIDTH), _rows(tm, 2 * d),
                   _rows(tm, POOL_WIDTH), acc(wmix.shape), acc((1, POOL_WIDTH))],
        out_shape=[jax.ShapeDtypeStruct((t, d), BF16), jax.ShapeDtypeStruct((t, d), BF16),
                   jax.ShapeDtypeStruct((t, d), BF16), jax.ShapeDtypeStruct((t, ATTN_WIDTH), BF16),
                   jax.ShapeDtypeStruct((t, 2 * d), BF16), jax.ShapeDtypeStruct((t, POOL_WIDTH), F32),
                   jax.ShapeDtypeStruct(wmix.shape, F32), jax.ShapeDtypeStruct((1, POOL_WIDTH), F32)],
        compiler_params=_params("arbitrary"),
    )(dh, gate, a, p, pooled, wattn, wmix, scale, wpool_t, wout)


def _fold_halves(x):
    return x + pltpu.roll(x, HEAD_DIM, 1)


def _attn_bwd(q, kv, dattn, dist, sinks, carry=None):
    t = q.shape[0]

    def body(q_ref, kvc_ref, kvp_ref, do_ref, dist_ref, sink_ref, dq_ref, dkv_own_ref, dkv_prev_ref, dsink_ref,
             s_scr, dp_scr, p_scr, ds_scr):
        first = pl.program_id(0) == 0

        @pl.when(first)
        def _():
            dsink_ref[...] = jnp.zeros_like(dsink_ref)

        own = _own_block()
        dist_v = dist_ref[...]
        kk, vv = _bands(kvc_ref, kvp_ref)
        lane = lax.broadcasted_iota(jnp.int32, (1, LANES), 1)
        for head in HEADS:
            hk = head // Q_PER_KV
            s_scr[head] = _head_scores(q_ref, kk[hk], dist_v, head, first, own)
            dp_scr[head] = _fold(_nt(_own_half(do_ref, head), vv[hk]), own)
        dsink = jnp.zeros((1, LANES), F32)
        for head in HEADS:
            probs, psink = _softmax_sink(s_scr[head], sink_ref[head])
            dprobs = dp_scr[head]
            rowdot = jnp.sum(probs * dprobs, axis=-1, keepdims=True)
            p_scr[head] = _unfold(probs.astype(BF16), own)
            ds_scr[head] = _unfold((probs * (dprobs - rowdot)).astype(BF16), own)
            dsink = dsink + jnp.where(lane == head, jnp.sum(-psink * rowdot, axis=0, keepdims=True), 0.0)
        dk_heads, dv_heads = [], []
        for hk in range(N_KV_HEADS):
            dk_t = jnp.zeros((LANES, 2 * BLOCK), F32)
            dv_t = jnp.zeros((LANES, 2 * BLOCK), F32)
            for pair in range(Q_PER_KV // 2):
                col = LANES * (hk * (Q_PER_KV // 2) + pair)
                q_t = (q_ref[:, col:col + LANES] * HEAD_DIM ** -0.5).T
                do_t = do_ref[:, col:col + LANES].T
                dqs = []
                for head in (hk * Q_PER_KV + 2 * pair, hk * Q_PER_KV + 2 * pair + 1):
                    mine = (lax.broadcasted_iota(jnp.int32, q_t.shape, 0) < HEAD_DIM) == (head % 2 == 0)
                    dv_t = dv_t + _nn(jnp.where(mine, do_t, jnp.zeros_like(do_t)), p_scr[head])
                    dk_t = dk_t + _nn(jnp.where(mine, q_t, jnp.zeros_like(q_t)), ds_scr[head])
                    dqs.append(_nn(ds_scr[head], kk[hk]))
                dq_pair = jnp.where(_low_half(dqs[0].shape), dqs[0], dqs[1])
                dq_ref[:, col:col + LANES] = (dq_pair * HEAD_DIM ** -0.5).astype(BF16)
            dk_heads.append(_fold_halves(dk_t.T))
            dv_heads.append(_fold_halves(dv_t.T))
        low = _low_half(dk_heads[0].shape)
        dkv = jnp.concatenate([jnp.where(low, dk_heads[0], dk_heads[1]), jnp.where(low, dv_heads[0], dv_heads[1])],
                              axis=1)
        dkv_prev_ref[...] = dkv[0:BLOCK, :]
        dkv_own_ref[...] = dkv[BLOCK:2 * BLOCK, :]
        dsink_ref[...] += dsink

    return _launch(
        body, (q, kv, kv, dattn, dist, sinks), carry, name="attn_bwd", grid=(t // BLOCK,),
        in_specs=[_rows(BLOCK, ATTN_WIDTH), _rows(BLOCK, 2 * KV_WIDTH),
                  pl.BlockSpec((BLOCK, 2 * KV_WIDTH), lambda i: (jnp.maximum(i - 1, 0), 0)),
                  _rows(BLOCK, ATTN_WIDTH), _resident(dist.shape), SMEM],
        out_specs=[_rows(BLOCK, ATTN_WIDTH), _rows(BLOCK, 2 * KV_WIDTH), _rows(BLOCK, 2 * KV_WIDTH),
                   pl.BlockSpec((1, LANES), lambda i: (0, 0))],
        out_shape=[jax.ShapeDtypeStruct((t, ATTN_WIDTH), BF16), jax.ShapeDtypeStruct((t, 2 * KV_WIDTH), F32),
                   jax.ShapeDtypeStruct((t, 2 * KV_WIDTH), F32), jax.ShapeDtypeStruct((1, LANES), F32)],
        scratch_shapes=[pltpu.VMEM((N_Q_HEADS, BLOCK, BLOCK), F32), pltpu.VMEM((N_Q_HEADS, BLOCK, BLOCK), F32),
                        pltpu.VMEM((N_Q_HEADS, BLOCK, 2 * BLOCK), BF16),
                        pltpu.VMEM((N_Q_HEADS, BLOCK, 2 * BLOCK), BF16)],
        semantics=("arbitrary",))


def _mix_in_bwd(dq, dkv_own, dkv_prev, dpooled, dgate, h, g, win_t, dh_res, tm, carry=None):
    t, d = h.shape
    nt = t // tm

    def body(dq_ref, own_ref, prev_ref, prev_next_ref, dpool_ref, halo_ref, dgate_ref, h_ref, g_ref, w_ref, res_ref,
             dproj_ref, dh_ref, dhb_ref, dg_ref):
        i = pl.program_id(0)
        last = i == nt - 1
        dproj_ref[:, 0:OFF_KV] = dq_ref[...]
        from_next = jnp.where(last, 0.0, prev_next_ref[...])
        if tm > BLOCK:
            from_next = jnp.concatenate([prev_ref[BLOCK:tm, :], from_next], axis=0)
        dproj_ref[:, OFF_KV:OFF_Z] = (own_ref[...] + from_next).astype(BF16)
        halo = jnp.where(last, 0.0, halo_ref[...])
        for gi, width in enumerate(POOL_WINDOWS):
            lo, hi = gi * POOL_GROUP, (gi + 1) * POOL_GROUP
            dpg = dpool_ref[:, lo:hi]
            scaled = jnp.concatenate([dpg / _pool_counts(tm, width), halo[:, lo:hi] / float(width)], axis=0)
            dz = _leading_sums(scaled, gi)[0:tm, :] - dpg
            dproj_ref[:, OFF_Z + lo:OFF_Z + hi] = dz.astype(BF16)
        dproj_ref[:, OFF_GATE:IN_WIDTH] = dgate_ref[...]
        du = _nn(dproj_ref[...], w_ref[...])
        xh, r = _rms_fwd(h_ref[...], g_ref[...])
        dx, dg = _rms_bwd(du, xh, r, g_ref[...])
        dh = res_ref[...] + dx
        dh_ref[...] = dh
        dhb_ref[...] = dh.astype(BF16)

        @pl.when(i == 0)
        def _():
            dg_ref[...] = jnp.zeros_like(dg_ref)

        dg_ref[...] += dg

    per = tm // BLOCK
    next_block = pl.BlockSpec((BLOCK, 2 * KV_WIDTH), lambda i: (jnp.minimum((i + 1) * per, t // BLOCK - 1), 0))
    next_halo = pl.BlockSpec((HALO, POOL_WIDTH), lambda i: (jnp.minimum((i + 1) * (tm // HALO), t // HALO - 1), 0))
    return _launch(
        body, (dq, dkv_own, dkv_prev, dkv_prev, dpooled, dpooled, dgate, h, g, win_t, dh_res), carry,
        name="mix_in_bwd", grid=(nt,),
        in_specs=[_rows(tm, ATTN_WIDTH), _rows(tm, 2 * KV_WIDTH), _rows(tm, 2 * KV_WIDTH), next_block,
                  _rows(tm, POOL_WIDTH), next_halo, _rows(tm, 2 * d), _rows(tm, d), _resident((1, d)),
                  _resident((IN_WIDTH, d)), _rows(tm, d)],
        out_specs=[_rows(tm, IN_WIDTH), _rows(tm, d), _rows(tm, d), pl.BlockSpec((1, d), lambda i: (0, 0))],
        out_shape=[jax.ShapeDtypeStruct((t, IN_WIDTH), BF16), jax.ShapeDtypeStruct((t, d), F32),
                   jax.ShapeDtypeStruct((t, d), BF16), jax.ShapeDtypeStruct((1, d), F32)],
        semantics=("arbitrary",))


BIG = (("wup1_t", "ffn1_w_up", True), ("wdown1", "ffn1_w_down", False), ("win_t", "w_in", True),
       ("wattn", "w_attn_up", False), ("wpool_t", "w_pool_up", True), ("wout", "w_out", False),
       ("wup2_t", "ffn2_w_up", True), ("wdown2", "ffn2_w_down", False))
ANY = pl.BlockSpec(memory_space=pl.ANY)
WIRE = BF16


def _place():
    return lax.axis_index("x"), lax.axis_index("y"), lax.axis_index("c")


def _peer(k):
    x, y, c = _place()
    return x ^ (k >> 2), y ^ ((k >> 1) & 1), c ^ (k & 1)


def _index(px, py, pc):
    return 4 * px + 2 * py + pc


def _gather_carry(shards):
    n = len(shards) * GATHER_PIECES

    def tools(ins, outs, sems):
        send_sems, recv_sems, local_sems = sems
        x, y, c = _place()
        chips = [(1 - x, y), (x, 1 - y), (1 - x, 1 - y)]

        def piece(item):
            w, q = divmod(item, GATHER_PIECES)
            r = ins[w].shape[0]
            return w, r, q * (r // GATHER_PIECES), r // GATHER_PIECES

        def mine(item):
            w, _, first, size = piece(item)
            return ins[w].at[pl.ds(first, size), :]

        def rows(item, px, py, pc):
            w, r, first, size = piece(item)
            return outs[w].at[pl.ds(_index(px, py, pc) * r + first, size), :]

        def copy(item, k, block, to, src=None):
            return pltpu.make_async_remote_copy(
                src_ref=rows(item, *block) if src is None else src, dst_ref=rows(item, *block),
                send_sem=send_sems.at[item, k], recv_sem=recv_sems.at[item, k], device_id=to, device_id_type=MESH)

        def own(item):
            return ([pltpu.make_async_copy(mine(item), rows(item, x, y, c), local_sems.at[item]),
                     copy(item, 0, (x, y, c), (x, y, 1 - c), src=mine(item))]
                    + [copy(item, 1 + j, (x, y, c), (*chip, c), src=mine(item)) for j, chip in enumerate(chips)])

        def passed(item, j):
            return copy(item, 4 + j, (*chips[j], c), (x, y, 1 - c))

        return (x, y, c), chips, copy, own, passed

    def start(ins, outs, sems):
        _, _, _, own, _ = tools(ins, outs, sems)
        for item in range(n):
            for cp in own(item):
                cp.start()

    def forward(item):
        def run(ins, outs, sems):
            (x, y, c), chips, copy, _, passed = tools(ins, outs, sems)
            for j, chip in enumerate(chips):
                copy(item, 1 + j, (*chip, c), (x, y, c)).wait_recv()
                passed(item, j).start()
        return run

    sizes = np.cumsum([s.size / GATHER_PIECES for s in shards for _ in range(GATHER_PIECES)])
    middles = [(float(sizes[item] / sizes[-1]), forward(item)) for item in range(n)]

    def finish(ins, outs, sems):
        (x, y, c), chips, copy, own, passed = tools(ins, outs, sems)
        for item in range(n):
            copy(item, 0, (x, y, 1 - c), (x, y, c)).wait_recv()
            for j, chip in enumerate(chips):
                copy(item, 4 + j, (*chip, 1 - c), (x, y, c)).wait_recv()
        for item in range(n):
            local, *sent = own(item)
            for cp in sent + [passed(item, j) for j in range(len(chips))]:
                cp.wait_send()
            local.wait()

    return _Carry(
        shards, [jax.ShapeDtypeStruct((N_DEV * s.shape[0], s.shape[1]), s.dtype) for s in shards],
        [pltpu.SemaphoreType.DMA((n, N_DEV - 1)), pltpu.SemaphoreType.DMA((n, N_DEV - 1)),
         pltpu.SemaphoreType.DMA((n,))], start, finish, middles)


def _scatter_carry(grads):
    n = len(grads)

    def tools(ins, outs, sems):
        send_sems, recv_sems, local_sems = sems
        me = _index(*_place())

        def block(ref, dev):
            r = ref.shape[0] // N_DEV
            return ref.at[pl.ds(dev * r, r), :]

        def copy(w, k, landing):
            to = _peer(k)
            return pltpu.make_async_remote_copy(
                src_ref=block(ins[w], _index(*to)), dst_ref=block(outs[w], landing), send_sem=send_sems.at[w, k - 1],
                recv_sem=recv_sems.at[w, k - 1], device_id=to, device_id_type=MESH)

        def mine(w):
            return pltpu.make_async_copy(block(ins[w], me), block(outs[w], me), local_sems.at[w])

        return me, copy, mine

    def start(ins, outs, sems):
        me, copy, mine = tools(ins, outs, sems)
        for w in range(n):
            mine(w).start()
            for k in range(1, N_DEV):
                copy(w, k, me).start()

    def finish(ins, outs, sems):
        _, copy, mine = tools(ins, outs, sems)
        for w in range(n):
            for k in range(1, N_DEV):
                copy(w, k, _index(*_peer(k))).wait()
            mine(w).wait()

    return _Carry(
        grads, [jax.ShapeDtypeStruct(g.shape, g.dtype) for g in grads],
        [pltpu.SemaphoreType.DMA((n, N_DEV - 1)), pltpu.SemaphoreType.DMA((n, N_DEV - 1)),
         pltpu.SemaphoreType.DMA((n,))], start, finish)


def _small_carry(small):
    srows = small.shape[0]

    def tools(ins, outs, sems):
        send_sems, recv_sems, local_sem = sems
        me = _index(*_place())

        def slot(dev):
            return outs[0].at[pl.ds(dev * srows, srows), :]

        def copy(k, landing):
            return pltpu.make_async_remote_copy(
                src_ref=ins[0], dst_ref=slot(landing), send_sem=send_sems.at[k - 1], recv_sem=recv_sems.at[k - 1],
                device_id=_peer(k), device_id_type=MESH)

        return me, copy, pltpu.make_async_copy(ins[0], slot(me), local_sem)

    def start(ins, outs, sems):
        me, copy, mine = tools(ins, outs, sems)
        mine.start()
        for k in range(1, N_DEV):
            copy(k, me).start()

    def finish(ins, outs, sems):
        _, copy, mine = tools(ins, outs, sems)
        for k in range(1, N_DEV):
            copy(k, _index(*_peer(k))).wait()
        mine.wait()

    return _Carry([small], [jax.ShapeDtypeStruct((N_DEV * srows, LANES), small.dtype)],
                  [pltpu.SemaphoreType.DMA((N_DEV - 1,)), pltpu.SemaphoreType.DMA((N_DEV - 1,)),
                   pltpu.SemaphoreType.DMA], start, finish)


def _exchange(carry, name):
    ci = len(carry.inputs)
    co = len(carry.out_shape)

    def body(*refs):
        parts = refs[:ci], refs[ci:ci + co], refs[ci + co:]
        carry.start(*parts)
        for _, fn in carry.middles:
            fn(*parts)
        carry.finish(*parts)

    return list(pl.pallas_call(body, name=name, in_specs=[ANY] * ci, out_specs=[ANY] * co, out_shape=carry.out_shape,
                               scratch_shapes=carry.scratch)(*carry.inputs))


def _adamw_math(w, g, m, v):
    m = ADAM_B1 * m + (1.0 - ADAM_B1) * g
    v = ADAM_B2 * v + (1.0 - ADAM_B2) * (g * g)
    m_hat = m / (1.0 - ADAM_B1 ** ADAM_STEP)
    v_hat = v / (1.0 - ADAM_B2 ** ADAM_STEP)
    return -ADAM_LR * (m_hat / (jnp.sqrt(v_hat) + ADAM_EPS) + ADAM_WD * w), m, v


def _sum_adamw(got, w, m, v, transposed, name):
    parts = list(got) if isinstance(got, (list, tuple)) else [got]
    r = parts[0].shape[0] // N_DEV
    cols = sum(part.shape[1] for part in parts)
    if transposed:
        (only,) = parts
        tile = cols if cols <= 512 else 256
        got_specs = [pl.BlockSpec((N_DEV, r, tile), lambda i: (0, 0, i))]
        spec, steps = pl.BlockSpec((tile, r), lambda i: (i, 0)), cols // tile
    else:
        tile = r if r <= 256 else r // 2
        got_specs = [pl.BlockSpec((N_DEV, tile, part.shape[1]), lambda i: (0, i, 0)) for part in parts]
        spec, steps = pl.BlockSpec((tile, cols), lambda i: (i, 0)), r // tile
    n = len(parts)

    def body(*refs):
        w_ref, m_ref, v_ref, g_ref, d_ref, m2_ref, v2_ref = refs[n:]
        sums = []
        for got_ref in refs[:n]:
            acc = got_ref[0].astype(F32)
            for dev in range(1, N_DEV):
                acc = acc + got_ref[dev].astype(F32)
            sums.append(acc)
        g = sums[0].T if transposed else (sums[0] if n == 1 else jnp.concatenate(sums, axis=1))
        g_ref[...] = g
        d_ref[...], m2_ref[...], v2_ref[...] = _adamw_math(w_ref[...], g, m_ref[...], v_ref[...])

    return pl.pallas_call(
        body, name=name, grid=(steps,), in_specs=got_specs + [spec, spec, spec], out_specs=[spec] * 4,
        out_shape=[jax.ShapeDtypeStruct(w.shape, F32)] * 4, compiler_params=_params("parallel"),
    )(*[part.reshape(N_DEV, r, part.shape[1]) for part in parts], w, m, v)


def _small_update(early, late, w, m, v):
    rows = w.shape[0]

    def body(early_ref, late_ref, w_ref, m_ref, v_ref, g_ref, d_ref, m2_ref, v2_ref):
        sums = []
        for ref in (early_ref, late_ref):
            acc = ref[0]
            for dev in range(1, N_DEV):
                acc = acc + ref[dev]
            sums.append(acc)
        g = jnp.concatenate(sums, axis=0)
        g_ref[...] = g
        d_ref[...], m2_ref[...], v2_ref[...] = _adamw_math(w_ref[...], g, m_ref[...], v_ref[...])

    return pl.pallas_call(
        body, name="small_update", out_shape=[jax.ShapeDtypeStruct((rows, LANES), F32)] * 4,
        compiler_params=pltpu.CompilerParams(vmem_limit_bytes=VMEM_LIMIT),
    )(early.reshape(N_DEV, -1, LANES), late.reshape(N_DEV, -1, LANES), w, m, v)


SMALL = (("pool_w_mix", 512), ("mix_norm", 8), ("ffn2_norm", 8), ("final_norm", 8), ("pool_scale", 8), ("sinks", 8),
         ("loss", 8), ("ffn1_norm", 8))
EARLY, LATE = SMALL[:-1], SMALL[-1:]


def _pack_small(parts, layout=SMALL):
    out = []
    for name, rows in layout:
        flat = parts[name].astype(F32).reshape(-1)
        out.append(jnp.pad(flat, (0, rows * LANES - flat.shape[0])).reshape(rows, LANES))
    return jnp.concatenate(out, axis=0)


def _unpack_small(packed, shapes):
    out, row = {}, 0
    for name, rows in SMALL:
        shape = shapes[name]
        size = int(np.prod(shape)) if shape else 1
        out[name] = packed[row:row + rows].reshape(-1)[:size].reshape(shape)
        row += rows
    return out


def kernel(x, ffn1_norm, ffn1_w_up, ffn1_w_down, mix_norm, w_in, sinks, w_attn_up, pool_w_mix, pool_scale, w_pool_up, w_out, ffn2_norm, ffn2_w_up, ffn2_w_down, final_norm, loss_target, m_ffn1_norm, m_ffn1_w_up, m_ffn1_w_down, m_mix_norm, m_w_in, m_sinks, m_w_attn_up, m_pool_w_mix, m_pool_scale, m_w_pool_up, m_w_out, m_ffn2_norm, m_ffn2_w_up, m_ffn2_w_down, m_final_norm, v_ffn1_norm, v_ffn1_w_up, v_ffn1_w_down, v_mix_norm, v_w_in, v_sinks, v_w_attn_up, v_pool_w_mix, v_pool_scale, v_w_pool_up, v_w_out, v_ffn2_norm, v_ffn2_w_up, v_ffn2_w_down, v_final_norm):
    args = dict(locals())
    weight_names = ("ffn1_norm", "ffn1_w_up", "ffn1_w_down", "mix_norm", "w_in", "sinks", "w_attn_up", "pool_w_mix",
                    "pool_scale", "w_pool_up", "w_out", "ffn2_norm", "ffn2_w_up", "ffn2_w_down", "final_norm")

    shard = {k: (args[p][0].T if tr else args[p][0]).astype(BF16) for k, p, tr in BIG}
    big = {"wup1_t": _exchange(_gather_carry([shard["wup1_t"]]), "gather_up1")[0]}

    def gathering(keys):
        return _gather_carry([shard[k] for k in keys])

    xs, target = x[0], loss_target[0]
    t = xs.shape[0]
    tm_f, tm_b, tk = min(512, t), min(512, t), min(1024, t)
    g1, gm, g2, gf = ffn1_norm, mix_norm, ffn2_norm, final_norm.reshape(1, D_MODEL)
    dist = _attn_dist()
    sink_v = sinks.reshape(N_Q_HEADS)
    wmix_b = pool_w_mix[0].astype(BF16)

    (n1, ab1, act1), (big["wdown1"], big["win_t"]) = _ffn_up(xs, g1, big["wup1_t"], tm_f, gathering(["wdown1", "win_t"]))
    (h1,), (big["wattn"], big["wpool_t"], big["wout"]) = _ffn_down(xs, act1, big["wdown1"], tm_f,
                                                                   gathering(["wattn", "wpool_t", "wout"]))
    (u, q, kv, z, gate), (big["wup2_t"],) = _mix_in_fwd(h1, gm, big["win_t"], tm_f, gathering(["wup2_t"]))
    attn = _attn_fwd(q, kv, dist, sink_v)
    (h2, a, p, merged, ms, pooled), (big["wdown2"],) = _mix_out_fwd(
        attn, z, gate, h1, big["wattn"], wmix_b, pool_scale, big["wpool_t"], big["wout"], tm_b, gathering(["wdown2"]))
    ab2, n2, act2, loss_lanes, dh3, dhb3, dgf = _ffn_loss(h2, g2, big["wup2_t"], big["wdown2"], gf, target, tm_f)

    got = {}
    (gw_down2,), _ = _wgrad(act2, dhb3, 0.5, D_FF, tk, "wgrad_down2")
    (dab2,), (got["wdown2"],) = _ffn_bwd_hidden(dhb3, ab2, big["wdown2"], tm_f, _scatter_carry([gw_down2]))
    (dh2, dg2), _ = _ffn_bwd_input(dab2, dh3, h2, g2, big["wup2_t"], tm_f)
    (gw_up2,), _ = _wgrad(dab2, n2, 1.0, D_FF, tk, "wgrad_up2")
    dhb2, da_b, dp_b, dattn, dgate, dpooled, dwmix, dscale = _mix_out_bwd(
        dh2, gate, a, p, pooled, big["wattn"], wmix_b, pool_scale, big["wpool_t"], big["wout"], tm_b)
    (gw_out,), _ = _wgrad(merged, dhb2, 1.0, D_MODEL, tk, "wgrad_out")
    (gw_attn,), _ = _wgrad(attn, da_b, 1.0, D_MODEL, tk, "wgrad_attn")
    (gw_pool,), _ = _wgrad(dp_b, ms, 1.0, D_MODEL, tk, "wgrad_pool")
    (dq, dkv_own, dkv_prev, dsinks), (got["wup2_t"],) = _attn_bwd(q, kv, dattn, dist, sink_v, _scatter_carry([gw_up2]))
    (dproj, dh1, dhb1, dgm), (got["wout"], got["wattn"], got["wpool_t"]) = _mix_in_bwd(
        dq, dkv_own, dkv_prev, dpooled, dgate, h1, gm, big["win_t"], dh2, tm_b,
        _scatter_carry([gw_out, gw_attn, gw_pool]))
    (gw_down1,), _ = _wgrad(act1, dhb1, 0.5, D_FF, tk, "wgrad_down1")
    (gw_in,), (got["wdown1"],) = _wgrad(dproj, u, 1.0, IN_WIDTH // 2, tk, "wgrad_in", _scatter_carry([gw_down1]))
    (dab1,), (got["win_t"],) = _ffn_bwd_hidden(dhb1, ab1, big["wdown1"], tm_f, _scatter_carry([gw_in]))
    small_parts = {"pool_w_mix": dwmix, "mix_norm": dgm, "ffn2_norm": dg2, "final_norm": dgf, "pool_scale": dscale,
                   "sinks": dsinks[:, :N_Q_HEADS], "loss": loss_lanes[:, :1]}
    (gw_up1,), (small_early,) = _wgrad(dab1, n1, 1.0, D_FF, tk, "wgrad_up1",
                                       _small_carry(_pack_small(small_parts, EARLY)))
    (dx, dg1), (got["wup1_t"],) = _ffn_bwd_input(dab1, dh1, xs, g1, big["wup1_t"], tm_f, _scatter_carry([gw_up1]))
    (small_late,) = _exchange(_small_carry(_pack_small({"ffn1_norm": dg1}, LATE)), "gather_small")

    grad, delta, new_m, new_v = {}, {}, {}, {}
    for k, p, tr in BIG:
        outside = tr and args[p].shape[-1] % LANES != 0
        turn = (lambda a: a.T) if outside else (lambda a: a)
        res = _sum_adamw(got[k], turn(args[p][0]), turn(args["m_" + p][0]), turn(args["v_" + p][0]),
                         tr and not outside, "adamw_" + k)
        grad[p], delta[p], new_m[p], new_v[p] = (turn(a)[None] for a in res)

    shapes = {name: args[name].shape for name, _ in SMALL if name != "loss"}
    shapes["loss"] = ()
    packed = {pre: _pack_small({**{name: args[pre + name] for name, _ in SMALL if name != "loss"},
                                "loss": jnp.zeros((), F32)}) for pre in ("", "m_", "v_")}
    g_s, d_s, m_s, v_s = _small_update(small_early, small_late, packed[""], packed["m_"], packed["v_"])
    g_small, d_small, m_small, v_small = (_unpack_small(a, shapes) for a in (g_s, d_s, m_s, v_s))
    for name, _ in SMALL:
        if name != "loss":
            grad[name], delta[name], new_m[name], new_v[name] = (
                g_small[name], d_small[name], m_small[name], v_small[name])

    return (g_small["loss"], dx[None], *[grad[n] for n in weight_names], *[delta[n] for n in weight_names],
            *[new_m[n] for n in weight_names], *[new_v[n] for n in weight_names])
```

```python
import jax
import jax.numpy as jnp
import numpy as np
from jax import lax
from jax.experimental import pallas as pl
from jax.experimental.pallas import tpu as pltpu

F32 = jnp.float32
BF16 = jnp.bfloat16

D_MODEL = 1024
D_FF = 2816
N_Q_HEADS = 16
N_KV_HEADS = 2
Q_PER_KV = N_Q_HEADS // N_KV_HEADS
HEAD_DIM = 64
BLOCK = 128
ATTN_WIDTH = N_Q_HEADS * HEAD_DIM
KV_WIDTH = N_KV_HEADS * HEAD_DIM
POOL_WINDOWS = (2, 4, 8, 16)
POOL_GROUP = 128
POOL_WIDTH = 512
HALO = 16
IN_WIDTH = ATTN_WIDTH + 2 * KV_WIDTH + POOL_WIDTH + 2 * D_MODEL
OFF_KV = ATTN_WIDTH
OFF_Z = ATTN_WIDTH + 2 * KV_WIDTH
OFF_GATE = OFF_Z + POOL_WIDTH
NORM_EPS = 1e-6
ADAM_LR = 0.001
ADAM_B1 = 0.9
ADAM_B2 = 0.999
ADAM_EPS = 1e-08
ADAM_WD = 0.01
ADAM_STEP = 10

N_DEV = 8
LANES = 128
FF_CHUNK = 256
SLAB = 32
GATHER_PIECES = 2
NORM_ROWS = 16
VMEM_LIMIT = 56 * 1024 * 1024
MESH = pl.DeviceIdType.MESH


def _nn(a, b):
    return jnp.dot(a, b, preferred_element_type=F32)


def _nt(a, b):
    return lax.dot_general(a, b, (((1,), (1,)), ((), ())), preferred_element_type=F32)


def _tn(a, b):
    return lax.dot_general(a, b, (((0,), (0,)), ((), ())), preferred_element_type=F32)


def _params(*sem):
    return pltpu.CompilerParams(dimension_semantics=sem, vmem_limit_bytes=VMEM_LIMIT)


def _resident(shape):
    return pl.BlockSpec(shape, lambda *_: (0,) * len(shape), pipeline_mode=pl.Buffered(1))


def _rows(tm, cols):
    return pl.BlockSpec((tm, cols), lambda i: (i, 0))


class _Carry:
    def __init__(self, inputs, out_shape, scratch, start, finish, middles=()):
        self.inputs, self.out_shape, self.scratch = list(inputs), list(out_shape), list(scratch)
        self.start, self.finish, self.middles = start, finish, list(middles)


def _launch(body, args, carry=None, *, name, grid, in_specs, out_specs, out_shape, scratch_shapes=(), semantics):
    in_specs, out_specs, out_shape, scratch_shapes = list(in_specs), list(out_specs), list(out_shape), list(scratch_shapes)
    if carry is None:
        res = pl.pallas_call(body, name=name, grid=grid, in_specs=in_specs, out_specs=out_specs, out_shape=out_shape,
                             scratch_shapes=scratch_shapes, compiler_params=_params(*semantics))(*args)
        return list(res), []
    ni, no, ns = len(in_specs), len(out_specs), len(scratch_shapes)
    ci, co = len(carry.inputs), len(carry.out_shape)
    total = int(np.prod(grid))

    def full(*refs):
        own_in, c_in = refs[:ni], refs[ni:ni + ci]
        own_out, c_out = refs[ni + ci:ni + ci + no], refs[ni + ci + no:ni + ci + no + co]
        own_scr, c_sem = refs[ni + ci + no + co:ni + ci + no + co + ns], refs[ni + ci + no + co + ns:]
        step = 0
        for axis, size in enumerate(grid):
            step = step * size + pl.program_id(axis)
        pl.when(step == 0)(lambda: carry.start(c_in, c_out, c_sem))
        for fraction, fn in carry.middles:
            at = min(total - 1, int(fraction * total) + 1)
            pl.when(step == at)(lambda fn=fn: fn(c_in, c_out, c_sem))
        body(*own_in, *own_out, *own_scr)
        pl.when(step == total - 1)(lambda: carry.finish(c_in, c_out, c_sem))

    res = pl.pallas_call(
        full, name=name, grid=grid, in_specs=in_specs + [ANY] * ci, out_specs=out_specs + [ANY] * co,
        out_shape=out_shape + carry.out_shape, scratch_shapes=scratch_shapes + carry.scratch,
        compiler_params=_params(*(["arbitrary"] * len(grid))),
    )(*args, *carry.inputs)
    return list(res[:no]), list(res[no:])


def _rms_fwd(xv, g):
    r = lax.rsqrt(jnp.mean(xv * xv, axis=-1, keepdims=True) + NORM_EPS)
    return xv * r, r


def _rms_bwd(dn, xh, r, g):
    dxh = dn * g
    dx = r * (dxh - xh * jnp.mean(dxh * xh, axis=-1, keepdims=True))
    return dx, jnp.sum(dn * xh, axis=0, keepdims=True)


def _ffn_loss(x, g, wup_t, wdown, gf, target, tm):
    t, d = x.shape
    f = wdown.shape[0]

    def body(x_ref, g_ref, wup_ref, wdn_ref, gf_ref, tgt_ref, ab_ref, n_ref, act_ref, loss_ref, dh_ref, dhb_ref, dg_ref):
        xv = x_ref[...]
        xh, _ = _rms_fwd(xv, g_ref[...])
        n = (xh * g_ref[...]).astype(BF16)
        n_ref[...] = n
        for c in range(f // FF_CHUNK):
            lo, hi = c * FF_CHUNK, (c + 1) * FF_CHUNK
            a = _nt(n, wup_ref[lo:hi, :])
            b = _nt(n, wup_ref[f + lo:f + hi, :])
            ab_ref[:, lo:hi] = a.astype(BF16)
            ab_ref[:, f + lo:f + hi] = b.astype(BF16)
            act_ref[:, lo:hi] = (a * jax.nn.sigmoid(a) * b).astype(BF16)
        dh_ref[...] = xv + 0.5 * _nn(act_ref[...], wdn_ref[...])
        gf = gf_ref[...]

        def rows_of(i, sums):
            rows = pl.ds(pl.multiple_of(i * NORM_ROWS, NORM_ROWS), NORM_ROWS)
            yh, r = _rms_fwd(dh_ref[rows, :], gf)
            err = yh * gf - tgt_ref[rows, :]
            dh, dg = _rms_bwd(err * (1.0 / d), yh, r, gf)
            dh_ref[rows, :] = dh
            dhb_ref[rows, :] = dh.astype(BF16)
            part = 0.5 * jnp.sum(jnp.mean(err * err, axis=-1, keepdims=True), axis=0, keepdims=True)
            return sums[0] + dg, sums[1] + part

        dg, part = lax.fori_loop(0, tm // NORM_ROWS, rows_of, (jnp.zeros((1, d), F32), jnp.zeros((1, 1), F32)),
                                 unroll=True)

        @pl.when(pl.program_id(0) == 0)
        def _():
            dg_ref[...] = jnp.zeros_like(dg_ref)
            loss_ref[...] = jnp.zeros_like(loss_ref)

        dg_ref[...] += dg
        loss_ref[...] += jnp.broadcast_to(part, loss_ref.shape)

    return pl.pallas_call(
        body, name="ffn_loss", grid=(t // tm,),
        in_specs=[_rows(tm, d), _resident((1, d)), _resident((2 * f, d)), _resident((f, d)), _resident((1, d)),
                  _rows(tm, d)],
        out_specs=[_rows(tm, 2 * f), _rows(tm, d), _rows(tm, f), pl.BlockSpec((1, LANES), lambda i: (0, 0)),
                   _rows(tm, d), _rows(tm, d), pl.BlockSpec((1, d), lambda i: (0, 0))],
        out_shape=[jax.ShapeDtypeStruct((t, 2 * f), BF16), jax.ShapeDtypeStruct((t, d), BF16),
                   jax.ShapeDtypeStruct((t, f), BF16), jax.ShapeDtypeStruct((1, LANES), F32),
                   jax.ShapeDtypeStruct((t, d), F32), jax.ShapeDtypeStruct((t, d), BF16),
                   jax.ShapeDtypeStruct((1, d), F32)],
        compiler_params=_params("arbitrary"),
    )(x, g, wup_t, wdown, gf, target)


def _ffn_up(x, g, wup_t, tm, carry=None):
    t, d = x.shape
    f = wup_t.shape[0] // 2

    def body(x_ref, g_ref, wup_ref, n_ref, ab_ref, act_ref):
        xh, _ = _rms_fwd(x_ref[...], g_ref[...])
        n = (xh * g_ref[...]).astype(BF16)
        n_ref[...] = n
        for c in range(f // FF_CHUNK):
            lo, hi = c * FF_CHUNK, (c + 1) * FF_CHUNK
            a = _nt(n, wup_ref[lo:hi, :])
            b = _nt(n, wup_ref[f + lo:f + hi, :])
            ab_ref[:, lo:hi] = a.astype(BF16)
            ab_ref[:, f + lo:f + hi] = b.astype(BF16)
            act_ref[:, lo:hi] = (a * jax.nn.sigmoid(a) * b).astype(BF16)

    return _launch(
        body, (x, g, wup_t), carry, name="ffn_up", grid=(t // tm,),
        in_specs=[_rows(tm, d), _resident((1, d)), _resident((2 * f, d))],
        out_specs=[_rows(tm, d), _rows(tm, 2 * f), _rows(tm, f)],
        out_shape=[jax.ShapeDtypeStruct((t, d), BF16), jax.ShapeDtypeStruct((t, 2 * f), BF16),
                   jax.ShapeDtypeStruct((t, f), BF16)],
        semantics=("parallel",))


def _ffn_down(x, act, wdown, tm, carry=None):
    t, d = x.shape
    f = wdown.shape[0]

    def body(x_ref, act_ref, wdn_ref, h_ref):
        h_ref[...] = x_ref[...] + 0.5 * _nn(act_ref[...], wdn_ref[...])

    return _launch(
        body, (x, act, wdown), carry, name="ffn_down", grid=(t // tm,),
        in_specs=[_rows(tm, d), _rows(tm, f), _resident((f, d))], out_specs=[_rows(tm, d)],
        out_shape=[jax.ShapeDtypeStruct((t, d), F32)], semantics=("parallel",))


def _ffn_bwd_hidden(dhb, ab, wdown, tm, carry=None):
    t, d = dhb.shape
    f = wdown.shape[0]

    def body(dh_ref, ab_ref, wdn_ref, dab_ref, dact_ref):
        half = dh_ref[...] * 0.5
        for c in range(f // FF_CHUNK):
            lo, hi = c * FF_CHUNK, (c + 1) * FF_CHUNK
            dact_ref[...] = _nt(half, wdn_ref[lo:hi, :])

            def slab(i, carry_):
                rows = pl.ds(pl.multiple_of(i * SLAB, SLAB), SLAB)
                a = ab_ref[rows, lo:hi].astype(F32)
                b = ab_ref[rows, f + lo:f + hi].astype(F32)
                s = jax.nn.sigmoid(a)
                ds_ = dact_ref[rows, :] * s
                dab_ref[rows, lo:hi] = (ds_ * b * (1.0 + a * (1.0 - s))).astype(BF16)
                dab_ref[rows, f + lo:f + hi] = (ds_ * a).astype(BF16)
                return carry_

            lax.fori_loop(0, tm // SLAB, slab, 0, unroll=True)

    return _launch(
        body, (dhb, ab, wdown), carry, name="ffn_bwd_hidden", grid=(t // tm,),
        in_specs=[_rows(tm, d), _rows(tm, 2 * f), _resident((f, d))], out_specs=[_rows(tm, 2 * f)],
        out_shape=[jax.ShapeDtypeStruct((t, 2 * f), BF16)],
        scratch_shapes=[pltpu.VMEM((tm, FF_CHUNK), F32)], semantics=("parallel",))


def _ffn_bwd_input(dab, dh, x, g, wup_t, tm, carry=None):
    t, d = x.shape
    f2 = wup_t.shape[0]

    def body(dab_ref, dh_ref, x_ref, g_ref, wup_ref, dx_ref, dg_ref):
        dx_ref[...] = _nn(dab_ref[...], wup_ref[...])
        gv = g_ref[...]

        def rows_of(i, dg_sum):
            rows = pl.ds(pl.multiple_of(i * NORM_ROWS, NORM_ROWS), NORM_ROWS)
            xh, r = _rms_fwd(x_ref[rows, :], gv)
            dx, dg_rows = _rms_bwd(dx_ref[rows, :], xh, r, gv)
            dx_ref[rows, :] = dh_ref[rows, :] + dx
            return dg_sum + dg_rows

        dg = lax.fori_loop(0, tm // NORM_ROWS, rows_of, jnp.zeros((1, d), F32), unroll=True)

        @pl.when(pl.program_id(0) == 0)
        def _():
            dg_ref[...] = jnp.zeros_like(dg_ref)

        dg_ref[...] += dg

    return _launch(
        body, (dab, dh, x, g, wup_t), carry, name="ffn_bwd_input", grid=(t // tm,),
        in_specs=[_rows(tm, f2), _rows(tm, d), _rows(tm, d), _resident((1, d)), _resident((f2, d))],
        out_specs=[_rows(tm, d), pl.BlockSpec((1, d), lambda i: (0, 0))],
        out_shape=[jax.ShapeDtypeStruct((t, d), F32), jax.ShapeDtypeStruct((1, d), F32)],
        semantics=("arbitrary",))


def _wgrad(lhs, rhs, scale, bm, tk, name, carry=None):
    t, m = lhs.shape
    n = rhs.shape[1]
    steps = t // tk
    chunk = bm if bm <= 2048 else bm // 2

    def body(l_ref, r_ref, o_ref, acc_ref):
        @pl.when(pl.program_id(1) == 0)
        def _():
            acc_ref[...] = jnp.zeros_like(acc_ref)

        for lo in range(0, bm, chunk):
            acc_ref[lo:lo + chunk, :] += _tn(l_ref[:, lo:lo + chunk], r_ref[...])

        @pl.when(pl.program_id(1) == steps - 1)
        def _():
            o_ref[...] = (scale * acc_ref[...]).astype(o_ref.dtype)

    return _launch(
        body, (lhs, rhs), carry, name=name, grid=(m // bm, steps),
        in_specs=[pl.BlockSpec((tk, bm), lambda i, k: (k, i)), pl.BlockSpec((tk, n), lambda i, k: (k, 0))],
        out_specs=[pl.BlockSpec((bm, n), lambda i, k: (i, 0))],
        out_shape=[jax.ShapeDtypeStruct((m, n), WIRE)],
        scratch_shapes=[pltpu.VMEM((bm, n), F32)], semantics=("parallel", "arbitrary"))


def _mix_in_fwd(h, g, win_t, tm, carry=None):
    t, d = h.shape

    def body(h_ref, g_ref, w_ref, u_ref, q_ref, kv_ref, z_ref, gate_ref):
        xh, _ = _rms_fwd(h_ref[...], g_ref[...])
        u = (xh * g_ref[...]).astype(BF16)
        u_ref[...] = u
        q_ref[...] = _nt(u, w_ref[0:OFF_KV, :]).astype(BF16)
        kv_ref[...] = _nt(u, w_ref[OFF_KV:OFF_Z, :]).astype(BF16)
        z_ref[...] = _nt(u, w_ref[OFF_Z:OFF_GATE, :])
        gate_ref[...] = _nt(u, w_ref[OFF_GATE:IN_WIDTH, :]).astype(BF16)

    return _launch(
        body, (h, g, win_t), carry, name="mix_in_fwd", grid=(t // tm,),
        in_specs=[_rows(tm, d), _resident((1, d)), _resident((IN_WIDTH, d))],
        out_specs=[_rows(tm, d), _rows(tm, ATTN_WIDTH), _rows(tm, 2 * KV_WIDTH), _rows(tm, POOL_WIDTH),
                   _rows(tm, 2 * D_MODEL)],
        out_shape=[jax.ShapeDtypeStruct((t, d), BF16), jax.ShapeDtypeStruct((t, ATTN_WIDTH), BF16),
                   jax.ShapeDtypeStruct((t, 2 * KV_WIDTH), BF16), jax.ShapeDtypeStruct((t, POOL_WIDTH), F32),
                   jax.ShapeDtypeStruct((t, 2 * D_MODEL), BF16)],
        semantics=("parallel",))


ALIBI_SLOPES = tuple(float(s) for s in (2.0 ** (-8.0 * np.arange(1, N_Q_HEADS + 1, dtype=np.float32) / N_Q_HEADS)))


def _attn_dist():
    return jnp.asarray(((np.arange(BLOCK)[:, None] - np.arange(BLOCK)[None, :]) % BLOCK).astype(np.float32))


def _own_block():
    shape = (BLOCK, BLOCK)
    return lax.broadcasted_iota(jnp.int32, shape, 1) <= lax.broadcasted_iota(jnp.int32, shape, 0)


def _fold(band2, own):
    return jnp.where(own, band2[:, BLOCK:], band2[:, :BLOCK])


def _unfold(x, own):
    zero = jnp.zeros_like(x)
    return jnp.concatenate([jnp.where(own, zero, x), jnp.where(own, x, zero)], axis=1)


def _low_half(shape):
    return lax.broadcasted_iota(jnp.int32, shape, len(shape) - 1) < HEAD_DIM


def _both_halves(band, kv_head):
    low = _low_half(band.shape)
    swapped = pltpu.roll(band, HEAD_DIM, 1)
    return jnp.where(low, band, swapped) if kv_head == 0 else jnp.where(low, swapped, band)


def _pair_rows(ref, pair, scale=None):
    v = ref[:, LANES * pair:LANES * (pair + 1)]
    if scale is not None:
        v = v * scale
    low, zero = _low_half(v.shape), jnp.zeros_like(v)
    return jnp.concatenate([jnp.where(low, v, zero), jnp.where(low, zero, v)], axis=0)


def _per_head(even, odd):
    return jnp.where(lax.broadcasted_iota(jnp.int32, (2 * BLOCK, 1), 0) < BLOCK, even, odd)


def _twice(x):
    return jnp.concatenate([x, x], axis=0)


def _pair_scores(q_ref, kk, dist2, pair, first, own2):
    s2 = _nt(_pair_rows(q_ref, pair, HEAD_DIM ** -0.5), kk)
    before = jnp.where(first, -jnp.inf, s2[:, :BLOCK])
    slopes = _per_head(ALIBI_SLOPES[2 * pair], ALIBI_SLOPES[2 * pair + 1])
    return jnp.where(own2, s2[:, BLOCK:], before) - slopes * dist2


def _own_half(ref, head):
    v = ref[:, LANES * (head // 2):LANES * (head // 2 + 1)]
    low = _low_half(v.shape)
    return jnp.where(low if head % 2 == 0 else jnp.logical_not(low), v, jnp.zeros_like(v))


def _head_scores(q_ref, kk, dist, head, first, own):
    s2 = _nt(_own_half(q_ref, head) * HEAD_DIM ** -0.5, kk)
    before = jnp.where(first, -jnp.inf, s2[:, :BLOCK])
    return jnp.where(own, s2[:, BLOCK:], before) - ALIBI_SLOPES[head] * dist


def _heads_of(stack):
    return jnp.where(_low_half((BLOCK, LANES)), stack[:BLOCK], stack[BLOCK:])


def _softmax_sink(s, sink):
    m = jnp.maximum(jnp.max(s, axis=-1, keepdims=True), sink)
    p = jnp.exp(s - m)
    psink = jnp.exp(sink - m)
    inv = 1.0 / (jnp.sum(p, axis=-1, keepdims=True) + psink)
    return p * inv, psink * inv


def _bands(kvc_ref, kvp_ref):
    kband = jnp.concatenate([kvp_ref[:, 0:LANES], kvc_ref[:, 0:LANES]], axis=0)
    vband = jnp.concatenate([kvp_ref[:, LANES:2 * LANES], kvc_ref[:, LANES:2 * LANES]], axis=0)
    return ([_both_halves(kband, hk) for hk in range(N_KV_HEADS)],
            [_both_halves(vband, hk) for hk in range(N_KV_HEADS)])


SMEM = pl.BlockSpec(memory_space=pltpu.SMEM)
HEADS = range(N_Q_HEADS)
PAIRS = range(N_Q_HEADS // 2)
PAIRS_PER_KV = Q_PER_KV // 2


def _attn_fwd(q, kv, dist, sinks):
    t = q.shape[0]

    def body(q_ref, kvc_ref, kvp_ref, dist_ref, sink_ref, o_ref, s_scr, p_scr):
        first = pl.program_id(0) == 0
        own2 = _twice(_own_block())
        dist2 = _twice(dist_ref[...])
        kk, vv = _bands(kvc_ref, kvp_ref)
        for pair in PAIRS:
            s_scr[pair] = _pair_scores(q_ref, kk[pair // PAIRS_PER_KV], dist2, pair, first, own2)
        for pair in PAIRS:
            probs, _ = _softmax_sink(s_scr[pair], _per_head(sink_ref[2 * pair], sink_ref[2 * pair + 1]))
            p_scr[pair] = _unfold(probs.astype(BF16), own2)
        for pair in PAIRS:
            out = _nn(p_scr[pair], vv[pair // PAIRS_PER_KV])
            o_ref[:, LANES * pair:LANES * (pair + 1)] = _heads_of(out).astype(BF16)

    return pl.pallas_call(
        body, name="attn_fwd", grid=(t // BLOCK,),
        in_specs=[_rows(BLOCK, ATTN_WIDTH), _rows(BLOCK, 2 * KV_WIDTH),
                  pl.BlockSpec((BLOCK, 2 * KV_WIDTH), lambda i: (jnp.maximum(i - 1, 0), 0)),
                  _resident(dist.shape), SMEM],
        out_specs=_rows(BLOCK, ATTN_WIDTH),
        out_shape=jax.ShapeDtypeStruct((t, ATTN_WIDTH), BF16),
        scratch_shapes=[pltpu.VMEM((len(PAIRS), 2 * BLOCK, BLOCK), F32),
                        pltpu.VMEM((len(PAIRS), 2 * BLOCK, 2 * BLOCK), BF16)],
        compiler_params=_params("parallel"),
    )(q, kv, kv, dist, sinks)


def _pool_counts(tm, width):
    row = pl.program_id(0) * tm + lax.broadcasted_iota(jnp.int32, (tm, 1), 0)
    return jnp.minimum(row + 1, width).astype(F32)


def _trailing_sums(zz, group):
    s = zz
    for k in range(group + 1):
        s = s + pltpu.roll(s, 1 << k, 0)
    return s


def _leading_sums(zz, group):
    rows = zz.shape[0]
    s = zz
    for k in range(group + 1):
        s = s + pltpu.roll(s, rows - (1 << k), 0)
    return s


def _mix_out_fwd(attn, z, gate, h, wattn, wmix, scale, wpool_t, wout, tm, carry=None):
    t, d = h.shape

    def body(attn_ref, z_ref, halo_ref, gate_ref, h_ref, wattn_ref, wmix_ref, scale_ref, wpool_ref, wout_ref,
             h2_ref, a_ref, p_ref, merged_ref, ms_ref, pooled_ref):
        halo = jnp.where(pl.program_id(0) == 0, 0.0, halo_ref[...])
        for gi, width in enumerate(POOL_WINDOWS):
            lo, hi = gi * POOL_GROUP, (gi + 1) * POOL_GROUP
            zg = z_ref[:, lo:hi]
            sums = _trailing_sums(jnp.concatenate([halo[:, lo:hi], zg], axis=0), gi)[HALO:, :]
            pooled = (sums / _pool_counts(tm, width) - zg).astype(BF16)
            pooled_ref[:, lo:hi] = pooled
            ms_ref[:, lo:hi] = (_nn(pooled, wmix_ref[gi]) * scale_ref[:, lo:hi]).astype(BF16)
        p = _nt(ms_ref[...], wpool_ref[...])
        a = _nn(attn_ref[...], wattn_ref[...])
        a_ref[...] = a.astype(BF16)
        p_ref[...] = p.astype(BF16)
        merged = (jax.nn.sigmoid(gate_ref[:, 0:d].astype(F32)) * a
                  + jax.nn.sigmoid(gate_ref[:, d:2 * d].astype(F32)) * p).astype(BF16)
        merged_ref[...] = merged
        h2_ref[...] = h_ref[...] + _nn(merged, wout_ref[...])

    halo_spec = pl.BlockSpec((HALO, POOL_WIDTH), lambda i: (jnp.maximum(i * (tm // HALO) - 1, 0), 0))
    return _launch(
        body, (attn, z, z, gate, h, wattn, wmix, scale, wpool_t, wout), carry, name="mix_out_fwd", grid=(t // tm,),
        in_specs=[_rows(tm, ATTN_WIDTH), _rows(tm, POOL_WIDTH), halo_spec, _rows(tm, 2 * d), _rows(tm, d),
                  _resident(wattn.shape), _resident(wmix.shape), _resident(scale.shape), _resident(wpool_t.shape),
                  _resident(wout.shape)],
        out_specs=[_rows(tm, d), _rows(tm, d), _rows(tm, d), _rows(tm, d), _rows(tm, POOL_WIDTH),
                   _rows(tm, POOL_WIDTH)],
        out_shape=[jax.ShapeDtypeStruct((t, d), F32), jax.ShapeDtypeStruct((t, d), BF16),
                   jax.ShapeDtypeStruct((t, d), BF16), jax.ShapeDtypeStruct((t, d), BF16),
                   jax.ShapeDtypeStruct((t, POOL_WIDTH), BF16), jax.ShapeDtypeStruct((t, POOL_WIDTH), BF16)],
        semantics=("parallel",))


def _mix_out_bwd(dh, gate, a, p, pooled, wattn, wmix, scale, wpool_t, wout, tm):
    t, d = dh.shape

    def body(dh_ref, gate_ref, a_ref, p_ref, pooled_ref, wattn_ref, wmix_ref, scale_ref, wpool_ref, wout_ref,
             dhb_ref, dab_ref, dpb_ref, dattn_ref, dgate_ref, dpooled_ref, dwmix_ref, dscale_ref):
        @pl.when(pl.program_id(0) == 0)
        def _():
            dwmix_ref[...] = jnp.zeros_like(dwmix_ref)
            dscale_ref[...] = jnp.zeros_like(dscale_ref)

        dhb = dh_ref[...].astype(BF16)
        dhb_ref[...] = dhb
        dm = _nt(dhb, wout_ref[...])
        sa = jax.nn.sigmoid(gate_ref[:, 0:d].astype(F32))
        sp = jax.nn.sigmoid(gate_ref[:, d:2 * d].astype(F32))
        da = (dm * sa).astype(BF16)
        dp = (dm * sp).astype(BF16)
        dab_ref[...] = da
        dpb_ref[...] = dp
        dgate_ref[:, 0:d] = (dm * a_ref[...].astype(F32) * (sa * (1.0 - sa))).astype(BF16)
        dgate_ref[:, d:2 * d] = (dm * p_ref[...].astype(F32) * (sp * (1.0 - sp))).astype(BF16)
        dattn_ref[...] = _nt(da, wattn_ref[...]).astype(BF16)
        dms = _nn(dp, wpool_ref[...])
        for gi in range(len(POOL_WINDOWS)):
            lo, hi = gi * POOL_GROUP, (gi + 1) * POOL_GROUP
            pooled_g = pooled_ref[:, lo:hi]
            mixed = _nn(pooled_g, wmix_ref[gi])
            dscale_ref[:, lo:hi] += jnp.sum(dms[:, lo:hi] * mixed, axis=0, keepdims=True)
            dmixed = (dms[:, lo:hi] * scale_ref[:, lo:hi]).astype(BF16)
            dwmix_ref[gi] += _tn(pooled_g, dmixed)
            dpooled_ref[:, lo:hi] = _nt(dmixed, wmix_ref[gi])

    acc = lambda shape: pl.BlockSpec(shape, lambda i: (0,) * len(shape))
    return pl.pallas_call(
        body, name="mix_out_bwd", grid=(t // tm,),
        in_specs=[_rows(tm, d), _rows(tm, 2 * d), _rows(tm, d), _rows(tm, d), _rows(tm, POOL_WIDTH),
                  _resident(wattn.shape), _resident(wmix.shape), _resident(scale.shape), _resident(wpool_t.shape),
                  _resident(wout.shape)],
        out_specs=[_rows(tm, d), _rows(tm, d), _rows(tm, d), _rows(tm, ATTN_WIDTH), _rows(tm, 2 * d),
                   _rows(tm, POOL_WIDTH), acc(wmix.shape), acc((1, POOL_WIDTH))],
        out_shape=[jax.ShapeDtypeStruct((t, d), BF16), jax.ShapeDtypeStruct((t, d), BF16),
                   jax.ShapeDtypeStruct((t, d), BF16), jax.ShapeDtypeStruct((t, ATTN_WIDTH), BF16),
                   jax.ShapeDtypeStruct((t, 2 * d), BF16), jax.ShapeDtypeStruct((t, POOL_WIDTH), F32),
                   jax.ShapeDtypeStruct(wmix.shape, F32), jax.ShapeDtypeStruct((1, POOL_WIDTH), F32)],
        compiler_params=_params("arbitrary"),
    )(dh, gate, a, p, pooled, wattn, wmix, scale, wpool_t, wout)


def _fold_halves(x):
    return x + pltpu.roll(x, HEAD_DIM, 1)


def _attn_bwd(q, kv, dattn, dist, sinks, carry=None):
    t = q.shape[0]

    def body(q_ref, kvc_ref, kvp_ref, do_ref, dist_ref, sink_ref, dq_ref, dkv_own_ref, dkv_prev_ref, dsink_ref,
             s_scr, dp_scr, p_scr, ds_scr):
        first = pl.program_id(0) == 0

        @pl.when(first)
        def _():
            dsink_ref[...] = jnp.zeros_like(dsink_ref)

        own = _own_block()
        dist_v = dist_ref[...]
        kk, vv = _bands(kvc_ref, kvp_ref)
        lane = lax.broadcasted_iota(jnp.int32, (1, LANES), 1)
        for head in HEADS:
            hk = head // Q_PER_KV
            s_scr[head] = _head_scores(q_ref, kk[hk], dist_v, head, first, own)
            dp_scr[head] = _fold(_nt(_own_half(do_ref, head), vv[hk]), own)
        dsink = jnp.zeros((1, LANES), F32)
        for head in HEADS:
            probs, psink = _softmax_sink(s_scr[head], sink_ref[head])
            dprobs = dp_scr[head]
            rowdot = jnp.sum(probs * dprobs, axis=-1, keepdims=True)
            p_scr[head] = _unfold(probs.astype(BF16), own)
            ds_scr[head] = _unfold((probs * (dprobs - rowdot)).astype(BF16), own)
            dsink = dsink + jnp.where(lane == head, jnp.sum(-psink * rowdot, axis=0, keepdims=True), 0.0)
        dk_heads, dv_heads = [], []
        for hk in range(N_KV_HEADS):
            dk_t = jnp.zeros((LANES, 2 * BLOCK), F32)
            dv_t = jnp.zeros((LANES, 2 * BLOCK), F32)
            for pair in range(Q_PER_KV // 2):
                col = LANES * (hk * (Q_PER_KV // 2) + pair)
                q_t = (q_ref[:, col:col + LANES] * HEAD_DIM ** -0.5).T
                do_t = do_ref[:, col:col + LANES].T
                dqs = []
                for head in (hk * Q_PER_KV + 2 * pair, hk * Q_PER_KV + 2 * pair + 1):
                    mine = (lax.broadcasted_iota(jnp.int32, q_t.shape, 0) < HEAD_DIM) == (head % 2 == 0)
                    dv_t = dv_t + _nn(jnp.where(mine, do_t, jnp.zeros_like(do_t)), p_scr[head])
                    dk_t = dk_t + _nn(jnp.where(mine, q_t, jnp.zeros_like(q_t)), ds_scr[head])
                    dqs.append(_nn(ds_scr[head], kk[hk]))
                dq_pair = jnp.where(_low_half(dqs[0].shape), dqs[0], dqs[1])
                dq_ref[:, col:col + LANES] = (dq_pair * HEAD_DIM ** -0.5).astype(BF16)
            dk_heads.append(_fold_halves(dk_t.T))
            dv_heads.append(_fold_halves(dv_t.T))
        low = _low_half(dk_heads[0].shape)
        dkv = jnp.concatenate([jnp.where(low, dk_heads[0], dk_heads[1]), jnp.where(low, dv_heads[0], dv_heads[1])],
                              axis=1)
        dkv_prev_ref[...] = dkv[0:BLOCK, :]
        dkv_own_ref[...] = dkv[BLOCK:2 * BLOCK, :]
        dsink_ref[...] += dsink

    return _launch(
        body, (q, kv, kv, dattn, dist, sinks), carry, name="attn_bwd", grid=(t // BLOCK,),
        in_specs=[_rows(BLOCK, ATTN_WIDTH), _rows(BLOCK, 2 * KV_WIDTH),
                  pl.BlockSpec((BLOCK, 2 * KV_WIDTH), lambda i: (jnp.maximum(i - 1, 0), 0)),
                  _rows(BLOCK, ATTN_WIDTH), _resident(dist.shape), SMEM],
        out_specs=[_rows(BLOCK, ATTN_WIDTH), _rows(BLOCK, 2 * KV_WIDTH), _rows(BLOCK, 2 * KV_WIDTH),
                   pl.BlockSpec((1, LANES), lambda i: (0, 0))],
        out_shape=[jax.ShapeDtypeStruct((t, ATTN_WIDTH), BF16), jax.ShapeDtypeStruct((t, 2 * KV_WIDTH), F32),
                   jax.ShapeDtypeStruct((t, 2 * KV_WIDTH), F32), jax.ShapeDtypeStruct((1, LANES), F32)],
        scratch_shapes=[pltpu.VMEM((N_Q_HEADS, BLOCK, BLOCK), F32), pltpu.VMEM((N_Q_HEADS, BLOCK, BLOCK), F32),
                        pltpu.VMEM((N_Q_HEADS, BLOCK, 2 * BLOCK), BF16),
                        pltpu.VMEM((N_Q_HEADS, BLOCK, 2 * BLOCK), BF16)],
        semantics=("arbitrary",))


def _mix_in_bwd(dq, dkv_own, dkv_prev, dpooled, dgate, h, g, win_t, dh_res, tm, carry=None):
    t, d = h.shape
    nt = t // tm

    def body(dq_ref, own_ref, prev_ref, prev_next_ref, dpool_ref, halo_ref, dgate_ref, h_ref, g_ref, w_ref, res_ref,
             dproj_ref, dh_ref, dhb_ref, dg_ref):
        i = pl.program_id(0)
        last = i == nt - 1
        dproj_ref[:, 0:OFF_KV] = dq_ref[...]
        from_next = jnp.where(last, 0.0, prev_next_ref[...])
        if tm > BLOCK:
            from_next = jnp.concatenate([prev_ref[BLOCK:tm, :], from_next], axis=0)
        dproj_ref[:, OFF_KV:OFF_Z] = (own_ref[...] + from_next).astype(BF16)
        halo = jnp.where(last, 0.0, halo_ref[...])
        for gi, width in enumerate(POOL_WINDOWS):
            lo, hi = gi * POOL_GROUP, (gi + 1) * POOL_GROUP
            dpg = dpool_ref[:, lo:hi]
            scaled = jnp.concatenate([dpg / _pool_counts(tm, width), halo[:, lo:hi] / float(width)], axis=0)
            dz = _leading_sums(scaled, gi)[0:tm, :] - dpg
            dproj_ref[:, OFF_Z + lo:OFF_Z + hi] = dz.astype(BF16)
        dproj_ref[:, OFF_GATE:IN_WIDTH] = dgate_ref[...]
        du = _nn(dproj_ref[...], w_ref[...])
        xh, r = _rms_fwd(h_ref[...], g_ref[...])
        dx, dg = _rms_bwd(du, xh, r, g_ref[...])
        dh = res_ref[...] + dx
        dh_ref[...] = dh
        dhb_ref[...] = dh.astype(BF16)

        @pl.when(i == 0)
        def _():
            dg_ref[...] = jnp.zeros_like(dg_ref)

        dg_ref[...] += dg

    per = tm // BLOCK
    next_block = pl.BlockSpec((BLOCK, 2 * KV_WIDTH), lambda i: (jnp.minimum((i + 1) * per, t // BLOCK - 1), 0))
    next_halo = pl.BlockSpec((HALO, POOL_WIDTH), lambda i: (jnp.minimum((i + 1) * (tm // HALO), t // HALO - 1), 0))
    return _launch(
        body, (dq, dkv_own, dkv_prev, dkv_prev, dpooled, dpooled, dgate, h, g, win_t, dh_res), carry,
        name="mix_in_bwd", grid=(nt,),
        in_specs=[_rows(tm, ATTN_WIDTH), _rows(tm, 2 * KV_WIDTH), _rows(tm, 2 * KV_WIDTH), next_block,
                  _rows(tm, POOL_WIDTH), next_halo, _rows(tm, 2 * d), _rows(tm, d), _resident((1, d)),
                  _resident((IN_WIDTH, d)), _rows(tm, d)],
        out_specs=[_rows(tm, IN_WIDTH), _rows(tm, d), _rows(tm, d), pl.BlockSpec((1, d), lambda i: (0, 0))],
        out_shape=[jax.ShapeDtypeStruct((t, IN_WIDTH), BF16), jax.ShapeDtypeStruct((t, d), F32),
                   jax.ShapeDtypeStruct((t, d), BF16), jax.ShapeDtypeStruct((1, d), F32)],
        semantics=("arbitrary",))


BIG = (("wup1_t", "ffn1_w_up", True), ("wdown1", "ffn1_w_down", False), ("win_t", "w_in", True),
       ("wattn", "w_attn_up", False), ("wpool_t", "w_pool_up", True), ("wout", "w_out", False),
       ("wup2_t", "ffn2_w_up", True), ("wdown2", "ffn2_w_down", False))
ANY = pl.BlockSpec(memory_space=pl.ANY)
WIRE = BF16


def _place():
    return lax.axis_index("x"), lax.axis_index("y"), lax.axis_index("c")


def _peer(k):
    x, y, c = _place()
    return x ^ (k >> 2), y ^ ((k >> 1) & 1), c ^ (k & 1)


def _index(px, py, pc):
    return 4 * px + 2 * py + pc


def _gather_carry(shards):
    n = len(shards) * GATHER_PIECES

    def tools(ins, outs, sems):
        send_sems, recv_sems, local_sems = sems
        x, y, c = _place()
        chips = [(1 - x, y), (x, 1 - y), (1 - x, 1 - y)]

        def piece(item):
            w, q = divmod(item, GATHER_PIECES)
            r = ins[w].shape[0]
            return w, r, q * (r // GATHER_PIECES), r // GATHER_PIECES

        def mine(item):
            w, _, first, size = piece(item)
            return ins[w].at[pl.ds(first, size), :]

        def rows(item, px, py, pc):
            w, r, first, size = piece(item)
            return outs[w].at[pl.ds(_index(px, py, pc) * r + first, size), :]

        def copy(item, k, block, to, src=None):
            return pltpu.make_async_remote_copy(
                src_ref=rows(item, *block) if src is None else src, dst_ref=rows(item, *block),
                send_sem=send_sems.at[item, k], recv_sem=recv_sems.at[item, k], device_id=to, device_id_type=MESH)

        def own(item):
            return ([pltpu.make_async_copy(mine(item), rows(item, x, y, c), local_sems.at[item]),
                     copy(item, 0, (x, y, c), (x, y, 1 - c), src=mine(item))]
                    + [copy(item, 1 + j, (x, y, c), (*chip, c), src=mine(item)) for j, chip in enumerate(chips)])

        def passed(item, j):
            return copy(item, 4 + j, (*chips[j], c), (x, y, 1 - c))

        return (x, y, c), chips, copy, own, passed

    def start(ins, outs, sems):
        _, _, _, own, _ = tools(ins, outs, sems)
        for item in range(n):
            for cp in own(item):
                cp.start()

    def forward(item):
        def run(ins, outs, sems):
            (x, y, c), chips, copy, _, passed = tools(ins, outs, sems)
            for j, chip in enumerate(chips):
                copy(item, 1 + j, (*chip, c), (x, y, c)).wait_recv()
                passed(item, j).start()
        return run

    sizes = np.cumsum([s.size / GATHER_PIECES for s in shards for _ in range(GATHER_PIECES)])
    middles = [(float(sizes[item] / sizes[-1]), forward(item)) for item in range(n)]

    def finish(ins, outs, sems):
        (x, y, c), chips, copy, own, passed = tools(ins, outs, sems)
        for item in range(n):
            copy(item, 0, (x, y, 1 - c), (x, y, c)).wait_recv()
            for j, chip in enumerate(chips):
                copy(item, 4 + j, (*chip, 1 - c), (x, y, c)).wait_recv()
        for item in range(n):
            local, *sent = own(item)
            for cp in sent + [passed(item, j) for j in range(len(chips))]:
                cp.wait_send()
            local.wait()

    return _Carry(
        shards, [jax.ShapeDtypeStruct((N_DEV * s.shape[0], s.shape[1]), s.dtype) for s in shards],
        [pltpu.SemaphoreType.DMA((n, N_DEV - 1)), pltpu.SemaphoreType.DMA((n, N_DEV - 1)),
         pltpu.SemaphoreType.DMA((n,))], start, finish, middles)


def _scatter_carry(grads):
    n = len(grads)

    def tools(ins, outs, sems):
        send_sems, recv_sems, local_sems = sems
        me = _index(*_place())

        def block(ref, dev):
            r = ref.shape[0] // N_DEV
            return ref.at[pl.ds(dev * r, r), :]

        def copy(w, k, landing):
            to = _peer(k)
            return pltpu.make_async_remote_copy(
                src_ref=block(ins[w], _index(*to)), dst_ref=block(outs[w], landing), send_sem=send_sems.at[w, k - 1],
                recv_sem=recv_sems.at[w, k - 1], device_id=to, device_id_type=MESH)

        def mine(w):
            return pltpu.make_async_copy(block(ins[w], me), block(outs[w], me), local_sems.at[w])

        return me, copy, mine

    def start(ins, outs, sems):
        me, copy, mine = tools(ins, outs, sems)
        for w in range(n):
            mine(w).start()
            for k in range(1, N_DEV):
                copy(w, k, me).start()

    def finish(ins, outs, sems):
        _, copy, mine = tools(ins, outs, sems)
        for w in range(n):
            for k in range(1, N_DEV):
                copy(w, k, _index(*_peer(k))).wait()
            mine(w).wait()

    return _Carry(
        grads, [jax.ShapeDtypeStruct(g.shape, g.dtype) for g in grads],
        [pltpu.SemaphoreType.DMA((n, N_DEV - 1)), pltpu.SemaphoreType.DMA((n, N_DEV - 1)),
         pltpu.SemaphoreType.DMA((n,))], start, finish)


def _small_carry(small):
    srows = small.shape[0]

    def tools(ins, outs, sems):
        send_sems, recv_sems, local_sem = sems
        me = _index(*_place())

        def slot(dev):
            return outs[0].at[pl.ds(dev * srows, srows), :]

        def copy(k, landing):
            return pltpu.make_async_remote_copy(
                src_ref=ins[0], dst_ref=slot(landing), send_sem=send_sems.at[k - 1], recv_sem=recv_sems.at[k - 1],
                device_id=_peer(k), device_id_type=MESH)

        return me, copy, pltpu.make_async_copy(ins[0], slot(me), local_sem)

    def start(ins, outs, sems):
        me, copy, mine = tools(ins, outs, sems)
        mine.start()
        for k in range(1, N_DEV):
            copy(k, me).start()

    def finish(ins, outs, sems):
        _, copy, mine = tools(ins, outs, sems)
        for k in range(1, N_DEV):
            copy(k, _index(*_peer(k))).wait()
        mine.wait()

    return _Carry([small], [jax.ShapeDtypeStruct((N_DEV * srows, LANES), small.dtype)],
                  [pltpu.SemaphoreType.DMA((N_DEV - 1,)), pltpu.SemaphoreType.DMA((N_DEV - 1,)),
                   pltpu.SemaphoreType.DMA], start, finish)


def _exchange(carry, name):
    ci = len(carry.inputs)
    co = len(carry.out_shape)

    def body(*refs):
        parts = refs[:ci], refs[ci:ci + co], refs[ci + co:]
        carry.start(*parts)
        for _, fn in carry.middles:
            fn(*parts)
        carry.finish(*parts)

    return list(pl.pallas_call(body, name=name, in_specs=[ANY] * ci, out_specs=[ANY] * co, out_shape=carry.out_shape,
                               scratch_shapes=carry.scratch)(*carry.inputs))


def _adamw_math(w, g, m, v):
    m = ADAM_B1 * m + (1.0 - ADAM_B1) * g
    v = ADAM_B2 * v + (1.0 - ADAM_B2) * (g * g)
    m_hat = m / (1.0 - ADAM_B1 ** ADAM_STEP)
    v_hat = v / (1.0 - ADAM_B2 ** ADAM_STEP)
    return -ADAM_LR * (m_hat / (jnp.sqrt(v_hat) + ADAM_EPS) + ADAM_WD * w), m, v


def _sum_adamw(got, w, m, v, transposed, name):
    parts = list(got) if isinstance(got, (list, tuple)) else [got]
    r = parts[0].shape[0] // N_DEV
    cols = sum(part.shape[1] for part in parts)
    if transposed:
        (only,) = parts
        tile = cols if cols <= 512 else 256
        got_specs = [pl.BlockSpec((N_DEV, r, tile), lambda i: (0, 0, i))]
        spec, steps = pl.BlockSpec((tile, r), lambda i: (i, 0)), cols // tile
    else:
        tile = r if r <= 256 else r // 2
        got_specs = [pl.BlockSpec((N_DEV, tile, part.shape[1]), lambda i: (0, i, 0)) for part in parts]
        spec, steps = pl.BlockSpec((tile, cols), lambda i: (i, 0)), r // tile
    n = len(parts)

    def body(*refs):
        w_ref, m_ref, v_ref, g_ref, d_ref, m2_ref, v2_ref = refs[n:]
        sums = []
        for got_ref in refs[:n]:
            acc = got_ref[0].astype(F32)
            for dev in range(1, N_DEV):
                acc = acc + got_ref[dev].astype(F32)
            sums.append(acc)
        g = sums[0].T if transposed else (sums[0] if n == 1 else jnp.concatenate(sums, axis=1))
        g_ref[...] = g
        d_ref[...], m2_ref[...], v2_ref[...] = _adamw_math(w_ref[...], g, m_ref[...], v_ref[...])

    return pl.pallas_call(
        body, name=name, grid=(steps,), in_specs=got_specs + [spec, spec, spec], out_specs=[spec] * 4,
        out_shape=[jax.ShapeDtypeStruct(w.shape, F32)] * 4, compiler_params=_params("parallel"),
    )(*[part.reshape(N_DEV, r, part.shape[1]) for part in parts], w, m, v)


def _small_update(early, late, w, m, v):
    rows = w.shape[0]

    def body(early_ref, late_ref, w_ref, m_ref, v_ref, g_ref, d_ref, m2_ref, v2_ref):
        sums = []
        for ref in (early_ref, late_ref):
            acc = ref[0]
            for dev in range(1, N_DEV):
                acc = acc + ref[dev]
            sums.append(acc)
        g = jnp.concatenate(sums, axis=0)
        g_ref[...] = g
        d_ref[...], m2_ref[...], v2_ref[...] = _adamw_math(w_ref[...], g, m_ref[...], v_ref[...])

    return pl.pallas_call(
        body, name="small_update", out_shape=[jax.ShapeDtypeStruct((rows, LANES), F32)] * 4,
        compiler_params=pltpu.CompilerParams(vmem_limit_bytes=VMEM_LIMIT),
    )(early.reshape(N_DEV, -1, LANES), late.reshape(N_DEV, -1, LANES), w, m, v)


SMALL = (("pool_w_mix", 512), ("mix_norm", 8), ("ffn2_norm", 8), ("final_norm", 8), ("pool_scale", 8), ("sinks", 8),
         ("loss", 8), ("ffn1_norm", 8))
EARLY, LATE = SMALL[:-1], SMALL[-1:]


def _pack_small(parts, layout=SMALL):
    out = []
    for name, rows in layout:
        flat = parts[name].astype(F32).reshape(-1)
        out.append(jnp.pad(flat, (0, rows * LANES - flat.shape[0])).reshape(rows, LANES))
    return jnp.concatenate(out, axis=0)


def _unpack_small(packed, shapes):
    out, row = {}, 0
    for name, rows in SMALL:
        shape = shapes[name]
        size = int(np.prod(shape)) if shape else 1
        out[name] = packed[row:row + rows].reshape(-1)[:size].reshape(shape)
        row += rows
    return out


def kernel(x, ffn1_norm, ffn1_w_up, ffn1_w_down, mix_norm, w_in, sinks, w_attn_up, pool_w_mix, pool_scale, w_pool_up, w_out, ffn2_norm, ffn2_w_up, ffn2_w_down, final_norm, loss_target, m_ffn1_norm, m_ffn1_w_up, m_ffn1_w_down, m_mix_norm, m_w_in, m_sinks, m_w_attn_up, m_pool_w_mix, m_pool_scale, m_w_pool_up, m_w_out, m_ffn2_norm, m_ffn2_w_up, m_ffn2_w_down, m_final_norm, v_ffn1_norm, v_ffn1_w_up, v_ffn1_w_down, v_mix_norm, v_w_in, v_sinks, v_w_attn_up, v_pool_w_mix, v_pool_scale, v_w_pool_up, v_w_out, v_ffn2_norm, v_ffn2_w_up, v_ffn2_w_down, v_final_norm):
    args = dict(locals())
    weight_names = ("ffn1_norm", "ffn1_w_up", "ffn1_w_down", "mix_norm", "w_in", "sinks", "w_attn_up", "pool_w_mix",
                    "pool_scale", "w_pool_up", "w_out", "ffn2_norm", "ffn2_w_up", "ffn2_w_down", "final_norm")

    shard = {k: (args[p][0].T if tr else args[p][0]).astype(BF16) for k, p, tr in BIG}
    big = {"wup1_t": _exchange(_gather_carry([shard["wup1_t"]]), "gather_up1")[0]}

    def gathering(keys):
        return _gather_carry([shard[k] for k in keys])

    xs, target = x[0], loss_target[0]
    t = xs.shape[0]
    tm_f, tm_b, tk = min(512, t), min(512, t), min(1024, t)
    g1, gm, g2, gf = ffn1_norm, mix_norm, ffn2_norm, final_norm.reshape(1, D_MODEL)
    dist = _attn_dist()
    sink_v = sinks.reshape(N_Q_HEADS)
    wmix_b = pool_w_mix[0].astype(BF16)

    (n1, ab1, act1), (big["wdown1"], big["win_t"]) = _ffn_up(xs, g1, big["wup1_t"], tm_f, gathering(["wdown1", "win_t"]))
    (h1,), (big["wattn"], big["wpool_t"], big["wout"]) = _ffn_down(xs, act1, big["wdown1"], tm_f,
                                                                   gathering(["wattn", "wpool_t", "wout"]))
    (u, q, kv, z, gate), (big["wup2_t"],) = _mix_in_fwd(h1, gm, big["win_t"], tm_f, gathering(["wup2_t"]))
    attn = _attn_fwd(q, kv, dist, sink_v)
    (h2, a, p, merged, ms, pooled), (big["wdown2"],) = _mix_out_fwd(
        attn, z, gate, h1, big["wattn"], wmix_b, pool_scale, big["wpool_t"], big["wout"], tm_b, gathering(["wdown2"]))
    ab2, n2, act2, loss_lanes, dh3, dhb3, dgf = _ffn_loss(h2, g2, big["wup2_t"], big["wdown2"], gf, target, tm_f)

    got = {}
    (gw_down2,), _ = _wgrad(act2, dhb3, 0.5, D_FF, tk, "wgrad_down2")
    (dab2,), (got["wdown2"],) = _ffn_bwd_hidden(dhb3, ab2, big["wdown2"], tm_f, _scatter_carry([gw_down2]))
    (dh2, dg2), _ = _ffn_bwd_input(dab2, dh3, h2, g2, big["wup2_t"], tm_f)
    (gw_up2,), _ = _wgrad(dab2, n2, 1.0, D_FF, tk, "wgrad_up2")
    dhb2, da_b, dp_b, dattn, dgate, dpooled, dwmix, dscale = _mix_out_bwd(
        dh2, gate, a, p, pooled, big["wattn"], wmix_b, pool_scale, big["wpool_t"], big["wout"], tm_b)
    (gw_out,), _ = _wgrad(merged, dhb2, 1.0, D_MODEL, tk, "wgrad_out")
    (gw_attn,), _ = _wgrad(attn, da_b, 1.0, D_MODEL, tk, "wgrad_attn")
    (gw_pool,), _ = _wgrad(dp_b, ms, 1.0, D_MODEL, tk, "wgrad_pool")
    (dq, dkv_own, dkv_prev, dsinks), (got["wup2_t"],) = _attn_bwd(q, kv, dattn, dist, sink_v, _scatter_carry([gw_up2]))
    (dproj, dh1, dhb1, dgm), (got["wout"], got["wattn"], got["wpool_t"]) = _mix_in_bwd(
        dq, dkv_own, dkv_prev, dpooled, dgate, h1, gm, big["win_t"], dh2, tm_b,
        _scatter_carry([gw_out, gw_attn, gw_pool]))
    (gw_down1,), _ = _wgrad(act1, dhb1, 0.5, D_FF, tk, "wgrad_down1")
    (gw_in,), (got["wdown1"],) = _wgrad(dproj, u, 1.0, IN_WIDTH // 2, tk, "wgrad_in", _scatter_carry([gw_down1]))
    (dab1,), (got["win_t"],) = _ffn_bwd_hidden(dhb1, ab1, big["wdown1"], tm_f, _scatter_carry([gw_in]))
    small_parts = {"pool_w_mix": dwmix, "mix_norm": dgm, "ffn2_norm": dg2, "final_norm": dgf, "pool_scale": dscale,
                   "sinks": dsinks[:, :N_Q_HEADS], "loss": loss_lanes[:, :1]}
    (gw_up1,), (small_early,) = _wgrad(dab1, n1, 1.0, D_FF, tk, "wgrad_up1",
                                       _small_carry(_pack_small(small_parts, EARLY)))
    (dx, dg1), (got["wup1_t"],) = _ffn_bwd_input(dab1, dh1, xs, g1, big["wup1_t"], tm_f, _scatter_carry([gw_up1]))
    (small_late,) = _exchange(_small_carry(_pack_small({"ffn1_norm": dg1}, LATE)), "gather_small")

    grad, delta, new_m, new_v = {}, {}, {}, {}
    for k, p, tr in BIG:
        outside = tr and args[p].shape[-1] % LANES != 0
        turn = (lambda a: a.T) if outside else (lambda a: a)
        res = _sum_adamw(got[k], turn(args[p][0]), turn(args["m_" + p][0]), turn(args["v_" + p][0]),
                         tr and not outside, "adamw_" + k)
        grad[p], delta[p], new_m[p], new_v[p] = (turn(a)[None] for a in res)

    shapes = {name: args[name].shape for name, _ in SMALL if name != "loss"}
    shapes["loss"] = ()
    packed = {pre: _pack_small({**{name: args[pre + name] for name, _ in SMALL if name != "loss"},
                                "loss": jnp.zeros((), F32)}) for pre in ("", "m_", "v_")}
    g_s, d_s, m_s, v_s = _small_update(small_early, small_late, packed[""], packed["m_"], packed["v_"])
    g_small, d_small, m_small, v_small = (_unpack_small(a, shapes) for a in (g_s, d_s, m_s, v_s))
    for name, _ in SMALL:
        if name != "loss":
            grad[name], delta[name], new_m[name], new_v[name] = (
                g_small[name], d_small[name], m_small[name], v_small[name])

    return (g_small["loss"], dx[None], *[grad[n] for n in weight_names], *[delta[n] for n in weight_names],
            *[new_m[n] for n in weight_names], *[new_v[n] for n in weight_names])
```

```python
import jax
import jax.numpy as jnp
import numpy as np
from jax import lax
from jax.experimental import pallas as pl
from jax.experimental.pallas import tpu as pltpu

F32 = jnp.float32
BF16 = jnp.bfloat16

D_MODEL = 1024
D_FF = 2816
N_Q_HEADS = 16
N_KV_HEADS = 2
Q_PER_KV = N_Q_HEADS // N_KV_HEADS
HEAD_DIM = 64
BLOCK = 128
ATTN_WIDTH = N_Q_HEADS * HEAD_DIM
KV_WIDTH = N_KV_HEADS * HEAD_DIM
POOL_WINDOWS = (2, 4, 8, 16)
POOL_GROUP = 128
POOL_WIDTH = 512
HALO = 16
IN_WIDTH = ATTN_WIDTH + 2 * KV_WIDTH + POOL_WIDTH + 2 * D_MODEL
OFF_KV = ATTN_WIDTH
OFF_Z = ATTN_WIDTH + 2 * KV_WIDTH
OFF_GATE = OFF_Z + POOL_WIDTH
NORM_EPS = 1e-6
ADAM_LR = 0.001
ADAM_B1 = 0.9
ADAM_B2 = 0.999
ADAM_EPS = 1e-08
ADAM_WD = 0.01
ADAM_STEP = 10

N_DEV = 8
LANES = 128
FF_CHUNK = 256
SLAB = 32
GATHER_PIECES = 2
VMEM_LIMIT = 56 * 1024 * 1024
MESH = pl.DeviceIdType.MESH


def _nn(a, b):
    return jnp.dot(a, b, preferred_element_type=F32)


def _nt(a, b):
    return lax.dot_general(a, b, (((1,), (1,)), ((), ())), preferred_element_type=F32)


def _tn(a, b):
    return lax.dot_general(a, b, (((0,), (0,)), ((), ())), preferred_element_type=F32)


def _params(*sem):
    return pltpu.CompilerParams(dimension_semantics=sem, vmem_limit_bytes=VMEM_LIMIT)


def _resident(shape):
    return pl.BlockSpec(shape, lambda *_: (0,) * len(shape), pipeline_mode=pl.Buffered(1))


def _rows(tm, cols):
    return pl.BlockSpec((tm, cols), lambda i: (i, 0))


class _Carry:
    def __init__(self, inputs, out_shape, scratch, start, finish, middles=()):
        self.inputs, self.out_shape, self.scratch = list(inputs), list(out_shape), list(scratch)
        self.start, self.finish, self.middles = start, finish, list(middles)


def _launch(body, args, carry=None, *, name, grid, in_specs, out_specs, out_shape, scratch_shapes=(), semantics):
    in_specs, out_specs, out_shape, scratch_shapes = list(in_specs), list(out_specs), list(out_shape), list(scratch_shapes)
    if carry is None:
        res = pl.pallas_call(body, name=name, grid=grid, in_specs=in_specs, out_specs=out_specs, out_shape=out_shape,
                             scratch_shapes=scratch_shapes, compiler_params=_params(*semantics))(*args)
        return list(res), []
    ni, no, ns = len(in_specs), len(out_specs), len(scratch_shapes)
    ci, co = len(carry.inputs), len(carry.out_shape)
    total = int(np.prod(grid))

    def full(*refs):
        own_in, c_in = refs[:ni], refs[ni:ni + ci]
        own_out, c_out = refs[ni + ci:ni + ci + no], refs[ni + ci + no:ni + ci + no + co]
        own_scr, c_sem = refs[ni + ci + no + co:ni + ci + no + co + ns], refs[ni + ci + no + co + ns:]
        step = 0
        for axis, size in enumerate(grid):
            step = step * size + pl.program_id(axis)
        pl.when(step == 0)(lambda: carry.start(c_in, c_out, c_sem))
        for fraction, fn in carry.middles:
            at = min(total - 1, int(fraction * total) + 1)
            pl.when(step == at)(lambda fn=fn: fn(c_in, c_out, c_sem))
        body(*own_in, *own_out, *own_scr)
        pl.when(step == total - 1)(lambda: carry.finish(c_in, c_out, c_sem))

    res = pl.pallas_call(
        full, name=name, grid=grid, in_specs=in_specs + [ANY] * ci, out_specs=out_specs + [ANY] * co,
        out_shape=out_shape + carry.out_shape, scratch_shapes=scratch_shapes + carry.scratch,
        compiler_params=_params(*(["arbitrary"] * len(grid))),
    )(*args, *carry.inputs)
    return list(res[:no]), list(res[no:])


def _rms_fwd(xv, g):
    r = lax.rsqrt(jnp.mean(xv * xv, axis=-1, keepdims=True) + NORM_EPS)
    return xv * r, r


def _rms_bwd(dn, xh, r, g):
    dxh = dn * g
    dx = r * (dxh - xh * jnp.mean(dxh * xh, axis=-1, keepdims=True))
    return dx, jnp.sum(dn * xh, axis=0, keepdims=True)


def _ffn_loss(x, g, wup_t, wdown, gf, target, tm):
    t, d = x.shape
    f = wdown.shape[0]

    def body(x_ref, g_ref, wup_ref, wdn_ref, gf_ref, tgt_ref, ab_ref, n_ref, act_ref, loss_ref, dh_ref, dhb_ref, dg_ref):
        xv = x_ref[...]
        xh, _ = _rms_fwd(xv, g_ref[...])
        n = (xh * g_ref[...]).astype(BF16)
        n_ref[...] = n
        for c in range(f // FF_CHUNK):
            lo, hi = c * FF_CHUNK, (c + 1) * FF_CHUNK
            a = _nt(n, wup_ref[lo:hi, :])
            b = _nt(n, wup_ref[f + lo:f + hi, :])
            ab_ref[:, lo:hi] = a.astype(BF16)
            ab_ref[:, f + lo:f + hi] = b.astype(BF16)
            act_ref[:, lo:hi] = (a * jax.nn.sigmoid(a) * b).astype(BF16)
        h = xv + 0.5 * _nn(act_ref[...], wdn_ref[...])
        yh, r = _rms_fwd(h, gf_ref[...])
        err = yh * gf_ref[...] - tgt_ref[...]
        part = 0.5 * jnp.sum(jnp.mean(err * err, axis=-1, keepdims=True), axis=0, keepdims=True)
        dh, dg = _rms_bwd(err * (1.0 / d), yh, r, gf_ref[...])
        dh_ref[...] = dh
        dhb_ref[...] = dh.astype(BF16)

        @pl.when(pl.program_id(0) == 0)
        def _():
            dg_ref[...] = jnp.zeros_like(dg_ref)
            loss_ref[...] = jnp.zeros_like(loss_ref)

        dg_ref[...] += dg
        loss_ref[...] += jnp.broadcast_to(part, loss_ref.shape)

    return pl.pallas_call(
        body, name="ffn_loss", grid=(t // tm,),
        in_specs=[_rows(tm, d), _resident((1, d)), _resident((2 * f, d)), _resident((f, d)), _resident((1, d)),
                  _rows(tm, d)],
        out_specs=[_rows(tm, 2 * f), _rows(tm, d), _rows(tm, f), pl.BlockSpec((1, LANES), lambda i: (0, 0)),
                   _rows(tm, d), _rows(tm, d), pl.BlockSpec((1, d), lambda i: (0, 0))],
        out_shape=[jax.ShapeDtypeStruct((t, 2 * f), BF16), jax.ShapeDtypeStruct((t, d), BF16),
                   jax.ShapeDtypeStruct((t, f), BF16), jax.ShapeDtypeStruct((1, LANES), F32),
                   jax.ShapeDtypeStruct((t, d), F32), jax.ShapeDtypeStruct((t, d), BF16),
                   jax.ShapeDtypeStruct((1, d), F32)],
        compiler_params=_params("arbitrary"),
    )(x, g, wup_t, wdown, gf, target)


def _ffn_up(x, g, wup_t, tm, carry=None):
    t, d = x.shape
    f = wup_t.shape[0] // 2

    def body(x_ref, g_ref, wup_ref, n_ref, ab_ref, act_ref):
        xh, _ = _rms_fwd(x_ref[...], g_ref[...])
        n = (xh * g_ref[...]).astype(BF16)
        n_ref[...] = n
        for c in range(f // FF_CHUNK):
            lo, hi = c * FF_CHUNK, (c + 1) * FF_CHUNK
            a = _nt(n, wup_ref[lo:hi, :])
            b = _nt(n, wup_ref[f + lo:f + hi, :])
            ab_ref[:, lo:hi] = a.astype(BF16)
            ab_ref[:, f + lo:f + hi] = b.astype(BF16)
            act_ref[:, lo:hi] = (a * jax.nn.sigmoid(a) * b).astype(BF16)

    return _launch(
        body, (x, g, wup_t), carry, name="ffn_up", grid=(t // tm,),
        in_specs=[_rows(tm, d), _resident((1, d)), _resident((2 * f, d))],
        out_specs=[_rows(tm, d), _rows(tm, 2 * f), _rows(tm, f)],
        out_shape=[jax.ShapeDtypeStruct((t, d), BF16), jax.ShapeDtypeStruct((t, 2 * f), BF16),
                   jax.ShapeDtypeStruct((t, f), BF16)],
        semantics=("parallel",))


def _ffn_down(x, act, wdown, tm, carry=None):
    t, d = x.shape
    f = wdown.shape[0]

    def body(x_ref, act_ref, wdn_ref, h_ref):
        h_ref[...] = x_ref[...] + 0.5 * _nn(act_ref[...], wdn_ref[...])

    return _launch(
        body, (x, act, wdown), carry, name="ffn_down", grid=(t // tm,),
        in_specs=[_rows(tm, d), _rows(tm, f), _resident((f, d))], out_specs=[_rows(tm, d)],
        out_shape=[jax.ShapeDtypeStruct((t, d), F32)], semantics=("parallel",))


def _ffn_bwd_hidden(dhb, ab, wdown, tm, carry=None):
    t, d = dhb.shape
    f = wdown.shape[0]

    def body(dh_ref, ab_ref, wdn_ref, dab_ref, dact_ref):
        half = dh_ref[...] * 0.5
        for c in range(f // FF_CHUNK):
            lo, hi = c * FF_CHUNK, (c + 1) * FF_CHUNK
            dact_ref[...] = _nt(half, wdn_ref[lo:hi, :])

            def slab(i, carry_):
                rows = pl.ds(pl.multiple_of(i * SLAB, SLAB), SLAB)
                a = ab_ref[rows, lo:hi].astype(F32)
                b = ab_ref[rows, f + lo:f + hi].astype(F32)
                s = jax.nn.sigmoid(a)
                ds_ = dact_ref[rows, :] * s
                dab_ref[rows, lo:hi] = (ds_ * b * (1.0 + a * (1.0 - s))).astype(BF16)
                dab_ref[rows, f + lo:f + hi] = (ds_ * a).astype(BF16)
                return carry_

            lax.fori_loop(0, tm // SLAB, slab, 0, unroll=True)

    return _launch(
        body, (dhb, ab, wdown), carry, name="ffn_bwd_hidden", grid=(t // tm,),
        in_specs=[_rows(tm, d), _rows(tm, 2 * f), _resident((f, d))], out_specs=[_rows(tm, 2 * f)],
        out_shape=[jax.ShapeDtypeStruct((t, 2 * f), BF16)],
        scratch_shapes=[pltpu.VMEM((tm, FF_CHUNK), F32)], semantics=("parallel",))


def _ffn_bwd_input(dab, dh, x, g, wup_t, tm, carry=None):
    t, d = x.shape
    f2 = wup_t.shape[0]

    def body(dab_ref, dh_ref, x_ref, g_ref, wup_ref, dx_ref, dg_ref):
        dn = _nn(dab_ref[...], wup_ref[...])
        xh, r = _rms_fwd(x_ref[...], g_ref[...])
        dx, dg = _rms_bwd(dn, xh, r, g_ref[...])
        dx_ref[...] = dh_ref[...] + dx

        @pl.when(pl.program_id(0) == 0)
        def _():
            dg_ref[...] = jnp.zeros_like(dg_ref)

        dg_ref[...] += dg

    return _launch(
        body, (dab, dh, x, g, wup_t), carry, name="ffn_bwd_input", grid=(t // tm,),
        in_specs=[_rows(tm, f2), _rows(tm, d), _rows(tm, d), _resident((1, d)), _resident((f2, d))],
        out_specs=[_rows(tm, d), pl.BlockSpec((1, d), lambda i: (0, 0))],
        out_shape=[jax.ShapeDtypeStruct((t, d), F32), jax.ShapeDtypeStruct((1, d), F32)],
        semantics=("arbitrary",))


def _wgrad(lhs, rhs, scale, bm, tk, name, carry=None):
    t, m = lhs.shape
    n = rhs.shape[1]
    steps = t // tk
    chunk = bm if bm <= 2048 else bm // 2

    def body(l_ref, r_ref, o_ref, acc_ref):
        @pl.when(pl.program_id(1) == 0)
        def _():
            acc_ref[...] = jnp.zeros_like(acc_ref)

        for lo in range(0, bm, chunk):
            acc_ref[lo:lo + chunk, :] += _tn(l_ref[:, lo:lo + chunk], r_ref[...])

        @pl.when(pl.program_id(1) == steps - 1)
        def _():
            o_ref[...] = (scale * acc_ref[...]).astype(o_ref.dtype)

    return _launch(
        body, (lhs, rhs), carry, name=name, grid=(m // bm, steps),
        in_specs=[pl.BlockSpec((tk, bm), lambda i, k: (k, i)), pl.BlockSpec((tk, n), lambda i, k: (k, 0))],
        out_specs=[pl.BlockSpec((bm, n), lambda i, k: (i, 0))],
        out_shape=[jax.ShapeDtypeStruct((m, n), WIRE)],
        scratch_shapes=[pltpu.VMEM((bm, n), F32)], semantics=("parallel", "arbitrary"))


def _mix_in_fwd(h, g, win_t, tm, carry=None):
    t, d = h.shape

    def body(h_ref, g_ref, w_ref, u_ref, q_ref, kv_ref, z_ref, gate_ref):
        xh, _ = _rms_fwd(h_ref[...], g_ref[...])
        u = (xh * g_ref[...]).astype(BF16)
        u_ref[...] = u
        q_ref[...] = _nt(u, w_ref[0:OFF_KV, :]).astype(BF16)
        kv_ref[...] = _nt(u, w_ref[OFF_KV:OFF_Z, :]).astype(BF16)
        z_ref[...] = _nt(u, w_ref[OFF_Z:OFF_GATE, :])
        gate_ref[...] = _nt(u, w_ref[OFF_GATE:IN_WIDTH, :]).astype(BF16)

    return _launch(
        body, (h, g, win_t), carry, name="mix_in_fwd", grid=(t // tm,),
        in_specs=[_rows(tm, d), _resident((1, d)), _resident((IN_WIDTH, d))],
        out_specs=[_rows(tm, d), _rows(tm, ATTN_WIDTH), _rows(tm, 2 * KV_WIDTH), _rows(tm, POOL_WIDTH),
                   _rows(tm, 2 * D_MODEL)],
        out_shape=[jax.ShapeDtypeStruct((t, d), BF16), jax.ShapeDtypeStruct((t, ATTN_WIDTH), BF16),
                   jax.ShapeDtypeStruct((t, 2 * KV_WIDTH), BF16), jax.ShapeDtypeStruct((t, POOL_WIDTH), F32),
                   jax.ShapeDtypeStruct((t, 2 * D_MODEL), BF16)],
        semantics=("parallel",))


ALIBI_SLOPES = tuple(float(s) for s in (2.0 ** (-8.0 * np.arange(1, N_Q_HEADS + 1, dtype=np.float32) / N_Q_HEADS)))


def _attn_dist():
    return jnp.asarray(((np.arange(BLOCK)[:, None] - np.arange(BLOCK)[None, :]) % BLOCK).astype(np.float32))


def _own_block():
    shape = (BLOCK, BLOCK)
    return lax.broadcasted_iota(jnp.int32, shape, 1) <= lax.broadcasted_iota(jnp.int32, shape, 0)


def _fold(band2, own):
    return jnp.where(own, band2[:, BLOCK:], band2[:, :BLOCK])


def _unfold(x, own):
    zero = jnp.zeros_like(x)
    return jnp.concatenate([jnp.where(own, zero, x), jnp.where(own, x, zero)], axis=1)


def _low_half(shape):
    return lax.broadcasted_iota(jnp.int32, shape, len(shape) - 1) < HEAD_DIM


def _both_halves(band, kv_head):
    low = _low_half(band.shape)
    swapped = pltpu.roll(band, HEAD_DIM, 1)
    return jnp.where(low, band, swapped) if kv_head == 0 else jnp.where(low, swapped, band)


def _pair_rows(ref, rows, pair, scale=None):
    v = ref[rows, LANES * pair:LANES * (pair + 1)]
    if scale is not None:
        v = v * scale
    low, zero = _low_half(v.shape), jnp.zeros_like(v)
    return jnp.concatenate([jnp.where(low, v, zero), jnp.where(low, zero, v)], axis=0)


def _per_head(even, odd):
    return jnp.where(lax.broadcasted_iota(jnp.int32, (2 * BLOCK, 1), 0) < BLOCK, even, odd)


def _twice(x):
    return jnp.concatenate([x, x], axis=0)


def _pair_scores(q_ref, rows, kk, dist2, pair, first, own2):
    s2 = _nt(_pair_rows(q_ref, rows, pair, HEAD_DIM ** -0.5), kk)
    before = jnp.where(first, -jnp.inf, s2[:, :BLOCK])
    slopes = _per_head(ALIBI_SLOPES[2 * pair], ALIBI_SLOPES[2 * pair + 1])
    return jnp.where(own2, s2[:, BLOCK:], before) - slopes * dist2


def _own_half(ref, rows, head):
    v = ref[rows, LANES * (head // 2):LANES * (head // 2 + 1)]
    low = _low_half(v.shape)
    return jnp.where(low if head % 2 == 0 else jnp.logical_not(low), v, jnp.zeros_like(v))


def _head_scores(q_ref, rows, kk, dist, head, first, own):
    s2 = _nt(_own_half(q_ref, rows, head) * HEAD_DIM ** -0.5, kk)
    before = jnp.where(first, -jnp.inf, s2[:, :BLOCK])
    return jnp.where(own, s2[:, BLOCK:], before) - ALIBI_SLOPES[head] * dist


def _heads_of(stack):
    return jnp.where(_low_half((BLOCK, LANES)), stack[:BLOCK], stack[BLOCK:])


def _softmax_sink(s, sink):
    m = jnp.maximum(jnp.max(s, axis=-1, keepdims=True), sink)
    p = jnp.exp(s - m)
    psink = jnp.exp(sink - m)
    inv = 1.0 / (jnp.sum(p, axis=-1, keepdims=True) + psink)
    return p * inv, psink * inv


def _bands(kvc_ref, kvp_ref, sub):
    own = slice(sub * BLOCK, (sub + 1) * BLOCK)
    before = kvp_ref[...] if sub == 0 else kvc_ref[(sub - 1) * BLOCK:sub * BLOCK, :]
    kband = jnp.concatenate([before[:, 0:LANES], kvc_ref[own, 0:LANES]], axis=0)
    vband = jnp.concatenate([before[:, LANES:2 * LANES], kvc_ref[own, LANES:2 * LANES]], axis=0)
    return ([_both_halves(kband, hk) for hk in range(N_KV_HEADS)],
            [_both_halves(vband, hk) for hk in range(N_KV_HEADS)])


SMEM = pl.BlockSpec(memory_space=pltpu.SMEM)
HEADS = range(N_Q_HEADS)
PAIRS = range(N_Q_HEADS // 2)
PAIRS_PER_KV = Q_PER_KV // 2
def _sub_rows(sub):
    return slice(sub * BLOCK, (sub + 1) * BLOCK)


def _block_before(step):
    per = step // BLOCK
    return pl.BlockSpec((BLOCK, 2 * KV_WIDTH), lambda i: (jnp.maximum(i * per - 1, 0), 0))


def _attn_fwd(q, kv, dist, sinks):
    t = q.shape[0]
    subs = range(2)

    def body(q_ref, kvc_ref, kvp_ref, dist_ref, sink_ref, o_ref, s_scr, p_scr):
        own2 = _twice(_own_block())
        dist2 = _twice(dist_ref[...])
        bands = [_bands(kvc_ref, kvp_ref, sub) for sub in subs]
        for sub in subs:
            first = jnp.logical_and(pl.program_id(0) == 0, sub == 0)
            for pair in PAIRS:
                s_scr[sub, pair] = _pair_scores(q_ref, _sub_rows(sub), bands[sub][0][pair // PAIRS_PER_KV], dist2, pair,
                                                first, own2)
        for sub in subs:
            for pair in PAIRS:
                probs, _ = _softmax_sink(s_scr[sub, pair], _per_head(sink_ref[2 * pair], sink_ref[2 * pair + 1]))
                p_scr[sub, pair] = _unfold(probs.astype(BF16), own2)
        for sub in subs:
            for pair in PAIRS:
                out = _nn(p_scr[sub, pair], bands[sub][1][pair // PAIRS_PER_KV])
                o_ref[_sub_rows(sub), LANES * pair:LANES * (pair + 1)] = _heads_of(out).astype(BF16)

    step = len(subs) * BLOCK
    return pl.pallas_call(
        body, name="attn_fwd", grid=(t // step,),
        in_specs=[_rows(step, ATTN_WIDTH), _rows(step, 2 * KV_WIDTH), _block_before(step), _resident(dist.shape), SMEM],
        out_specs=_rows(step, ATTN_WIDTH),
        out_shape=jax.ShapeDtypeStruct((t, ATTN_WIDTH), BF16),
        scratch_shapes=[pltpu.VMEM((len(subs), len(PAIRS), 2 * BLOCK, BLOCK), F32),
                        pltpu.VMEM((len(subs), len(PAIRS), 2 * BLOCK, 2 * BLOCK), BF16)],
        compiler_params=_params("parallel"),
    )(q, kv, kv, dist, sinks)


def _pool_counts(tm, width):
    row = pl.program_id(0) * tm + lax.broadcasted_iota(jnp.int32, (tm, 1), 0)
    return jnp.minimum(row + 1, width).astype(F32)


def _trailing_sums(zz, group):
    s = zz
    for k in range(group + 1):
        s = s + pltpu.roll(s, 1 << k, 0)
    return s


def _leading_sums(zz, group):
    rows = zz.shape[0]
    s = zz
    for k in range(group + 1):
        s = s + pltpu.roll(s, rows - (1 << k), 0)
    return s


def _mix_out_fwd(attn, z, gate, h, wattn, wmix, scale, wpool_t, wout, tm, carry=None):
    t, d = h.shape

    def body(attn_ref, z_ref, halo_ref, gate_ref, h_ref, wattn_ref, wmix_ref, scale_ref, wpool_ref, wout_ref,
             h2_ref, a_ref, p_ref, merged_ref, ms_ref, pooled_ref):
        halo = jnp.where(pl.program_id(0) == 0, 0.0, halo_ref[...])
        for gi, width in enumerate(POOL_WINDOWS):
            lo, hi = gi * POOL_GROUP, (gi + 1) * POOL_GROUP
            zg = z_ref[:, lo:hi]
            sums = _trailing_sums(jnp.concatenate([halo[:, lo:hi], zg], axis=0), gi)[HALO:, :]
            pooled = (sums / _pool_counts(tm, width) - zg).astype(BF16)
            pooled_ref[:, lo:hi] = pooled
            ms_ref[:, lo:hi] = (_nn(pooled, wmix_ref[gi]) * scale_ref[:, lo:hi]).astype(BF16)
        p = _nt(ms_ref[...], wpool_ref[...])
        a = _nn(attn_ref[...], wattn_ref[...])
        a_ref[...] = a.astype(BF16)
        p_ref[...] = p.astype(BF16)
        merged = (jax.nn.sigmoid(gate_ref[:, 0:d].astype(F32)) * a
                  + jax.nn.sigmoid(gate_ref[:, d:2 * d].astype(F32)) * p).astype(BF16)
        merged_ref[...] = merged
        h2_ref[...] = h_ref[...] + _nn(merged, wout_ref[...])

    halo_spec = pl.BlockSpec((HALO, POOL_WIDTH), lambda i: (jnp.maximum(i * (tm // HALO) - 1, 0), 0))
    return _launch(
        body, (attn, z, z, gate, h, wattn, wmix, scale, wpool_t, wout), carry, name="mix_out_fwd", grid=(t // tm,),
        in_specs=[_rows(tm, ATTN_WIDTH), _rows(tm, POOL_WIDTH), halo_spec, _rows(tm, 2 * d), _rows(tm, d),
                  _resident(wattn.shape), _resident(wmix.shape), _resident(scale.shape), _resident(wpool_t.shape),
                  _resident(wout.shape)],
        out_specs=[_rows(tm, d), _rows(tm, d), _rows(tm, d), _rows(tm, d), _rows(tm, POOL_WIDTH),
                   _rows(tm, POOL_WIDTH)],
        out_shape=[jax.ShapeDtypeStruct((t, d), F32), jax.ShapeDtypeStruct((t, d), BF16),
                   jax.ShapeDtypeStruct((t, d), BF16), jax.ShapeDtypeStruct((t, d), BF16),
                   jax.ShapeDtypeStruct((t, POOL_WIDTH), BF16), jax.ShapeDtypeStruct((t, POOL_WIDTH), BF16)],
        semantics=("parallel",))


def _mix_out_bwd(dh, gate, a, p, pooled, wattn, wmix, scale, wpool_t, wout, tm):
    t, d = dh.shape

    def body(dh_ref, gate_ref, a_ref, p_ref, pooled_ref, wattn_ref, wmix_ref, scale_ref, wpool_ref, wout_ref,
             dhb_ref, dab_ref, dpb_ref, dattn_ref, dgate_ref, dpooled_ref, dwmix_ref, dscale_ref):
        @pl.when(pl.program_id(0) == 0)
        def _():
            dwmix_ref[...] = jnp.zeros_like(dwmix_ref)
            dscale_ref[...] = jnp.zeros_like(dscale_ref)

        dhb = dh_ref[...].astype(BF16)
        dhb_ref[...] = dhb
        dm = _nt(dhb, wout_ref[...])
        sa = jax.nn.sigmoid(gate_ref[:, 0:d].astype(F32))
        sp = jax.nn.sigmoid(gate_ref[:, d:2 * d].astype(F32))
        da = (dm * sa).astype(BF16)
        dp = (dm * sp).astype(BF16)
        dab_ref[...] = da
        dpb_ref[...] = dp
        dgate_ref[:, 0:d] = (dm * a_ref[...].astype(F32) * (sa * (1.0 - sa))).astype(BF16)
        dgate_ref[:, d:2 * d] = (dm * p_ref[...].astype(F32) * (sp * (1.0 - sp))).astype(BF16)
        dattn_ref[...] = _nt(da, wattn_ref[...]).astype(BF16)
        dms = _nn(dp, wpool_ref[...])
        for gi in range(len(POOL_WINDOWS)):
            lo, hi = gi * POOL_GROUP, (gi + 1) * POOL_GROUP
            pooled_g = pooled_ref[:, lo:hi]
            mixed = _nn(pooled_g, wmix_ref[gi])
            dscale_ref[:, lo:hi] += jnp.sum(dms[:, lo:hi] * mixed, axis=0, keepdims=True)
            dmixed = (dms[:, lo:hi] * scale_ref[:, lo:hi]).astype(BF16)
            dwmix_ref[gi] += _tn(pooled_g, dmixed)
            dpooled_ref[:, lo:hi] = _nt(dmixed, wmix_ref[gi])

    acc = lambda shape: pl.BlockSpec(shape, lambda i: (0,) * len(shape))
    return pl.pallas_call(
        body, name="mix_out_bwd", grid=(t // tm,),
        in_specs=[_rows(tm, d), _rows(tm, 2 * d), _rows(tm, d), _rows(tm, d), _rows(tm, POOL_WIDTH),
                  _resident(wattn.shape), _resident(wmix.shape), _resident(scale.shape), _resident(wpool_t.shape),
                  _resident(wout.shape)],
        out_specs=[_rows(tm, d), _rows(tm, d), _rows(tm, d), _rows(tm, ATTN_WIDTH), _rows(tm, 2 * d),
                   _rows(tm, POOL_WIDTH), acc(wmix.shape), acc((1, POOL_WIDTH))],
        out_shape=[jax.ShapeDtypeStruct((t, d), BF16), jax.ShapeDtypeStruct((t, d), BF16),
                   jax.ShapeDtypeStruct((t, d), BF16), jax.ShapeDtypeStruct((t, ATTN_WIDTH), BF16),
                   jax.ShapeDtypeStruct((t, 2 * d), BF16), jax.ShapeDtypeStruct((t, POOL_WIDTH), F32),
                   jax.ShapeDtypeStruct(wmix.shape, F32), jax.ShapeDtypeStruct((1, POOL_WIDTH), F32)],
        compiler_params=_params("arbitrary"),
    )(dh, gate, a, p, pooled, wattn, wmix, scale, wpool_t, wout)


def _fold_halves(x):
    return x + pltpu.roll(x, HEAD_DIM, 1)


def _attn_bwd(q, kv, dattn, dist, sinks, carry=None):
    t = q.shape[0]
    subs = range(1)

    def body(q_ref, kvc_ref, kvp_ref, do_ref, dist_ref, sink_ref, dq_ref, dkv_own_ref, dkv_prev_ref, dsink_ref,
             s_scr, dp_scr, p_scr, ds_scr):
        @pl.when(pl.program_id(0) == 0)
        def _():
            dsink_ref[...] = jnp.zeros_like(dsink_ref)

        own = _own_block()
        dist_v = dist_ref[...]
        bands = [_bands(kvc_ref, kvp_ref, sub) for sub in subs]
        lane = lax.broadcasted_iota(jnp.int32, (1, LANES), 1)
        for sub in subs:
            first = jnp.logical_and(pl.program_id(0) == 0, sub == 0)
            for head in HEADS:
                hk = head // Q_PER_KV
                s_scr[sub, head] = _head_scores(q_ref, _sub_rows(sub), bands[sub][0][hk], dist_v, head, first, own)
                dp_scr[sub, head] = _fold(_nt(_own_half(do_ref, _sub_rows(sub), head), bands[sub][1][hk]), own)
        dsink = jnp.zeros((1, LANES), F32)
        for sub in subs:
            for head in HEADS:
                probs, psink = _softmax_sink(s_scr[sub, head], sink_ref[head])
                dprobs = dp_scr[sub, head]
                rowdot = jnp.sum(probs * dprobs, axis=-1, keepdims=True)
                p_scr[sub, head] = _unfold(probs.astype(BF16), own)
                ds_scr[sub, head] = _unfold((probs * (dprobs - rowdot)).astype(BF16), own)
                dsink = dsink + jnp.where(lane == head, jnp.sum(-psink * rowdot, axis=0, keepdims=True), 0.0)
        for sub in subs:
            rows = _sub_rows(sub)
            dk_heads, dv_heads = [], []
            for hk in range(N_KV_HEADS):
                dk_t = jnp.zeros((LANES, 2 * BLOCK), F32)
                dv_t = jnp.zeros((LANES, 2 * BLOCK), F32)
                for pair in range(Q_PER_KV // 2):
                    cols = slice(LANES * (hk * PAIRS_PER_KV + pair), LANES * (hk * PAIRS_PER_KV + pair + 1))
                    q_t = (q_ref[rows, cols] * HEAD_DIM ** -0.5).T
                    do_t = do_ref[rows, cols].T
                    dqs = []
                    for head in (hk * Q_PER_KV + 2 * pair, hk * Q_PER_KV + 2 * pair + 1):
                        mine = (lax.broadcasted_iota(jnp.int32, q_t.shape, 0) < HEAD_DIM) == (head % 2 == 0)
                        dv_t = dv_t + _nn(jnp.where(mine, do_t, jnp.zeros_like(do_t)), p_scr[sub, head])
                        dk_t = dk_t + _nn(jnp.where(mine, q_t, jnp.zeros_like(q_t)), ds_scr[sub, head])
                        dqs.append(_nn(ds_scr[sub, head], bands[sub][0][hk]))
                    dq_pair = jnp.where(_low_half(dqs[0].shape), dqs[0], dqs[1])
                    dq_ref[rows, cols] = (dq_pair * HEAD_DIM ** -0.5).astype(BF16)
                dk_heads.append(_fold_halves(dk_t.T))
                dv_heads.append(_fold_halves(dv_t.T))
            low = _low_half(dk_heads[0].shape)
            dkv = jnp.concatenate([jnp.where(low, dk_heads[0], dk_heads[1]), jnp.where(low, dv_heads[0], dv_heads[1])],
                                  axis=1)
            dkv_prev_ref[rows, :] = dkv[0:BLOCK, :]
            dkv_own_ref[rows, :] = dkv[BLOCK:2 * BLOCK, :]
        dsink_ref[...] += dsink

    step = len(subs) * BLOCK
    return _launch(
        body, (q, kv, kv, dattn, dist, sinks), carry, name="attn_bwd", grid=(t // step,),
        in_specs=[_rows(step, ATTN_WIDTH), _rows(step, 2 * KV_WIDTH), _block_before(step), _rows(step, ATTN_WIDTH),
                  _resident(dist.shape), SMEM],
        out_specs=[_rows(step, ATTN_WIDTH), _rows(step, 2 * KV_WIDTH), _rows(step, 2 * KV_WIDTH),
                   pl.BlockSpec((1, LANES), lambda i: (0, 0))],
        out_shape=[jax.ShapeDtypeStruct((t, ATTN_WIDTH), BF16), jax.ShapeDtypeStruct((t, 2 * KV_WIDTH), F32),
                   jax.ShapeDtypeStruct((t, 2 * KV_WIDTH), F32), jax.ShapeDtypeStruct((1, LANES), F32)],
        scratch_shapes=[pltpu.VMEM((len(subs), N_Q_HEADS, BLOCK, BLOCK), F32),
                        pltpu.VMEM((len(subs), N_Q_HEADS, BLOCK, BLOCK), F32),
                        pltpu.VMEM((len(subs), N_Q_HEADS, BLOCK, 2 * BLOCK), BF16),
                        pltpu.VMEM((len(subs), N_Q_HEADS, BLOCK, 2 * BLOCK), BF16)],
        semantics=("arbitrary",))


def _mix_in_bwd(dq, dkv_own, dkv_prev, dpooled, dgate, h, g, win_t, dh_res, tm, carry=None):
    t, d = h.shape
    nt = t // tm

    def body(dq_ref, own_ref, prev_ref, prev_next_ref, dpool_ref, halo_ref, dgate_ref, h_ref, g_ref, w_ref, res_ref,
             dproj_ref, dh_ref, dhb_ref, dg_ref):
        i = pl.program_id(0)
        last = i == nt - 1
        dproj_ref[:, 0:OFF_KV] = dq_ref[...]
        from_next = jnp.where(last, 0.0, prev_next_ref[...])
        if tm > BLOCK:
            from_next = jnp.concatenate([prev_ref[BLOCK:tm, :], from_next], axis=0)
        dproj_ref[:, OFF_KV:OFF_Z] = (own_ref[...] + from_next).astype(BF16)
        halo = jnp.where(last, 0.0, halo_ref[...])
        for gi, width in enumerate(POOL_WINDOWS):
            lo, hi = gi * POOL_GROUP, (gi + 1) * POOL_GROUP
            dpg = dpool_ref[:, lo:hi]
            scaled = jnp.concatenate([dpg / _pool_counts(tm, width), halo[:, lo:hi] / float(width)], axis=0)
            dz = _leading_sums(scaled, gi)[0:tm, :] - dpg
            dproj_ref[:, OFF_Z + lo:OFF_Z + hi] = dz.astype(BF16)
        dproj_ref[:, OFF_GATE:IN_WIDTH] = dgate_ref[...]
        du = _nn(dproj_ref[...], w_ref[...])
        xh, r = _rms_fwd(h_ref[...], g_ref[...])
        dx, dg = _rms_bwd(du, xh, r, g_ref[...])
        dh = res_ref[...] + dx
        dh_ref[...] = dh
        dhb_ref[...] = dh.astype(BF16)

        @pl.when(i == 0)
        def _():
            dg_ref[...] = jnp.zeros_like(dg_ref)

        dg_ref[...] += dg

    per = tm // BLOCK
    next_block = pl.BlockSpec((BLOCK, 2 * KV_WIDTH), lambda i: (jnp.minimum((i + 1) * per, t // BLOCK - 1), 0))
    next_halo = pl.BlockSpec((HALO, POOL_WIDTH), lambda i: (jnp.minimum((i + 1) * (tm // HALO), t // HALO - 1), 0))
    return _launch(
        body, (dq, dkv_own, dkv_prev, dkv_prev, dpooled, dpooled, dgate, h, g, win_t, dh_res), carry,
        name="mix_in_bwd", grid=(nt,),
        in_specs=[_rows(tm, ATTN_WIDTH), _rows(tm, 2 * KV_WIDTH), _rows(tm, 2 * KV_WIDTH), next_block,
                  _rows(tm, POOL_WIDTH), next_halo, _rows(tm, 2 * d), _rows(tm, d), _resident((1, d)),
                  _resident((IN_WIDTH, d)), _rows(tm, d)],
        out_specs=[_rows(tm, IN_WIDTH), _rows(tm, d), _rows(tm, d), pl.BlockSpec((1, d), lambda i: (0, 0))],
        out_shape=[jax.ShapeDtypeStruct((t, IN_WIDTH), BF16), jax.ShapeDtypeStruct((t, d), F32),
                   jax.ShapeDtypeStruct((t, d), BF16), jax.ShapeDtypeStruct((1, d), F32)],
        semantics=("arbitrary",))


BIG = (("wup1_t", "ffn1_w_up", True), ("wdown1", "ffn1_w_down", False), ("win_t", "w_in", True),
       ("wattn", "w_attn_up", False), ("wpool_t", "w_pool_up", True), ("wout", "w_out", False),
       ("wup2_t", "ffn2_w_up", True), ("wdown2", "ffn2_w_down", False))
ANY = pl.BlockSpec(memory_space=pl.ANY)
WIRE = BF16


def _place():
    return lax.axis_index("x"), lax.axis_index("y"), lax.axis_index("c")


def _peer(k):
    x, y, c = _place()
    return x ^ (k >> 2), y ^ ((k >> 1) & 1), c ^ (k & 1)


def _index(px, py, pc):
    return 4 * px + 2 * py + pc


def _gather_carry(shards):
    n = len(shards) * GATHER_PIECES

    def tools(ins, outs, sems):
        send_sems, recv_sems, local_sems = sems
        x, y, c = _place()
        chips = [(1 - x, y), (x, 1 - y), (1 - x, 1 - y)]

        def piece(item):
            w, q = divmod(item, GATHER_PIECES)
            r = ins[w].shape[0]
            return w, r, q * (r // GATHER_PIECES), r // GATHER_PIECES

        def mine(item):
            w, _, first, size = piece(item)
            return ins[w].at[pl.ds(first, size), :]

        def rows(item, px, py, pc):
            w, r, first, size = piece(item)
            return outs[w].at[pl.ds(_index(px, py, pc) * r + first, size), :]

        def copy(item, k, block, to, src=None):
            return pltpu.make_async_remote_copy(
                src_ref=rows(item, *block) if src is None else src, dst_ref=rows(item, *block),
                send_sem=send_sems.at[item, k], recv_sem=recv_sems.at[item, k], device_id=to, device_id_type=MESH)

        def own(item):
            return ([pltpu.make_async_copy(mine(item), rows(item, x, y, c), local_sems.at[item]),
                     copy(item, 0, (x, y, c), (x, y, 1 - c), src=mine(item))]
                    + [copy(item, 1 + j, (x, y, c), (*chip, c), src=mine(item)) for j, chip in enumerate(chips)])

        def passed(item, j):
            return copy(item, 4 + j, (*chips[j], c), (x, y, 1 - c))

        return (x, y, c), chips, copy, own, passed

    def start(ins, outs, sems):
        _, _, _, own, _ = tools(ins, outs, sems)
        for item in range(n):
            for cp in own(item):
                cp.start()

    def forward(item):
        def run(ins, outs, sems):
            (x, y, c), chips, copy, _, passed = tools(ins, outs, sems)
            for j, chip in enumerate(chips):
                copy(item, 1 + j, (*chip, c), (x, y, c)).wait_recv()
                passed(item, j).start()
        return run

    sizes = np.cumsum([s.size / GATHER_PIECES for s in shards for _ in range(GATHER_PIECES)])
    middles = [(float(sizes[item] / sizes[-1]), forward(item)) for item in range(n)]

    def finish(ins, outs, sems):
        (x, y, c), chips, copy, own, passed = tools(ins, outs, sems)
        for item in range(n):
            copy(item, 0, (x, y, 1 - c), (x, y, c)).wait_recv()
            for j, chip in enumerate(chips):
                copy(item, 4 + j, (*chip, 1 - c), (x, y, c)).wait_recv()
        for item in range(n):
            local, *sent = own(item)
            for cp in sent + [passed(item, j) for j in range(len(chips))]:
                cp.wait_send()
            local.wait()

    return _Carry(
        shards, [jax.ShapeDtypeStruct((N_DEV * s.shape[0], s.shape[1]), s.dtype) for s in shards],
        [pltpu.SemaphoreType.DMA((n, N_DEV - 1)), pltpu.SemaphoreType.DMA((n, N_DEV - 1)),
         pltpu.SemaphoreType.DMA((n,))], start, finish, middles)


def _scatter_carry(grads):
    n = len(grads)

    def tools(ins, outs, sems):
        send_sems, recv_sems, local_sems = sems
        me = _index(*_place())

        def block(ref, dev):
            r = ref.shape[0] // N_DEV
            return ref.at[pl.ds(dev * r, r), :]

        def copy(w, k, landing):
            to = _peer(k)
            return pltpu.make_async_remote_copy(
                src_ref=block(ins[w], _index(*to)), dst_ref=block(outs[w], landing), send_sem=send_sems.at[w, k - 1],
                recv_sem=recv_sems.at[w, k - 1], device_id=to, device_id_type=MESH)

        def mine(w):
            return pltpu.make_async_copy(block(ins[w], me), block(outs[w], me), local_sems.at[w])

        return me, copy, mine

    def start(ins, outs, sems):
        me, copy, mine = tools(ins, outs, sems)
        for w in range(n):
            mine(w).start()
            for k in range(1, N_DEV):
                copy(w, k, me).start()

    def finish(ins, outs, sems):
        _, copy, mine = tools(ins, outs, sems)
        for w in range(n):
            for k in range(1, N_DEV):
                copy(w, k, _index(*_peer(k))).wait()
            mine(w).wait()

    return _Carry(
        grads, [jax.ShapeDtypeStruct(g.shape, g.dtype) for g in grads],
        [pltpu.SemaphoreType.DMA((n, N_DEV - 1)), pltpu.SemaphoreType.DMA((n, N_DEV - 1)),
         pltpu.SemaphoreType.DMA((n,))], start, finish)


def _small_carry(small):
    srows = small.shape[0]

    def tools(ins, outs, sems):
        send_sems, recv_sems, local_sem = sems
        me = _index(*_place())

        def slot(dev):
            return outs[0].at[pl.ds(dev * srows, srows), :]

        def copy(k, landing):
            return pltpu.make_async_remote_copy(
                src_ref=ins[0], dst_ref=slot(landing), send_sem=send_sems.at[k - 1], recv_sem=recv_sems.at[k - 1],
                device_id=_peer(k), device_id_type=MESH)

        return me, copy, pltpu.make_async_copy(ins[0], slot(me), local_sem)

    def start(ins, outs, sems):
        me, copy, mine = tools(ins, outs, sems)
        mine.start()
        for k in range(1, N_DEV):
            copy(k, me).start()

    def finish(ins, outs, sems):
        _, copy, mine = tools(ins, outs, sems)
        for k in range(1, N_DEV):
            copy(k, _index(*_peer(k))).wait()
        mine.wait()

    return _Carry([small], [jax.ShapeDtypeStruct((N_DEV * srows, LANES), small.dtype)],
                  [pltpu.SemaphoreType.DMA((N_DEV - 1,)), pltpu.SemaphoreType.DMA((N_DEV - 1,)),
                   pltpu.SemaphoreType.DMA], start, finish)


def _exchange(carry, name):
    ci = len(carry.inputs)
    co = len(carry.out_shape)

    def body(*refs):
        parts = refs[:ci], refs[ci:ci + co], refs[ci + co:]
        carry.start(*parts)
        for _, fn in carry.middles:
            fn(*parts)
        carry.finish(*parts)

    return list(pl.pallas_call(body, name=name, in_specs=[ANY] * ci, out_specs=[ANY] * co, out_shape=carry.out_shape,
                               scratch_shapes=carry.scratch)(*carry.inputs))


def _adamw_math(w, g, m, v):
    m = ADAM_B1 * m + (1.0 - ADAM_B1) * g
    v = ADAM_B2 * v + (1.0 - ADAM_B2) * (g * g)
    m_hat = m / (1.0 - ADAM_B1 ** ADAM_STEP)
    v_hat = v / (1.0 - ADAM_B2 ** ADAM_STEP)
    return -ADAM_LR * (m_hat / (jnp.sqrt(v_hat) + ADAM_EPS) + ADAM_WD * w), m, v


def _sum_adamw(got, w, m, v, transposed, name):
    parts = list(got) if isinstance(got, (list, tuple)) else [got]
    r = parts[0].shape[0] // N_DEV
    cols = sum(part.shape[1] for part in parts)
    if transposed:
        (only,) = parts
        tile = cols if cols <= 512 else 256
        got_specs = [pl.BlockSpec((N_DEV, r, tile), lambda i: (0, 0, i))]
        spec, steps = pl.BlockSpec((tile, r), lambda i: (i, 0)), cols // tile
    else:
        tile = r if r <= 256 else r // 2
        got_specs = [pl.BlockSpec((N_DEV, tile, part.shape[1]), lambda i: (0, i, 0)) for part in parts]
        spec, steps = pl.BlockSpec((tile, cols), lambda i: (i, 0)), r // tile
    n = len(parts)

    def body(*refs):
        w_ref, m_ref, v_ref, g_ref, d_ref, m2_ref, v2_ref = refs[n:]
        sums = []
        for got_ref in refs[:n]:
            acc = got_ref[0].astype(F32)
            for dev in range(1, N_DEV):
                acc = acc + got_ref[dev].astype(F32)
            sums.append(acc)
        g = sums[0].T if transposed else (sums[0] if n == 1 else jnp.concatenate(sums, axis=1))
        g_ref[...] = g
        d_ref[...], m2_ref[...], v2_ref[...] = _adamw_math(w_ref[...], g, m_ref[...], v_ref[...])

    return pl.pallas_call(
        body, name=name, grid=(steps,), in_specs=got_specs + [spec, spec, spec], out_specs=[spec] * 4,
        out_shape=[jax.ShapeDtypeStruct(w.shape, F32)] * 4, compiler_params=_params("parallel"),
    )(*[part.reshape(N_DEV, r, part.shape[1]) for part in parts], w, m, v)


def _small_update(early, late, w, m, v):
    rows = w.shape[0]

    def body(early_ref, late_ref, w_ref, m_ref, v_ref, g_ref, d_ref, m2_ref, v2_ref):
        sums = []
        for ref in (early_ref, late_ref):
            acc = ref[0]
            for dev in range(1, N_DEV):
                acc = acc + ref[dev]
            sums.append(acc)
        g = jnp.concatenate(sums, axis=0)
        g_ref[...] = g
        d_ref[...], m2_ref[...], v2_ref[...] = _adamw_math(w_ref[...], g, m_ref[...], v_ref[...])

    return pl.pallas_call(
        body, name="small_update", out_shape=[jax.ShapeDtypeStruct((rows, LANES), F32)] * 4,
        compiler_params=pltpu.CompilerParams(vmem_limit_bytes=VMEM_LIMIT),
    )(early.reshape(N_DEV, -1, LANES), late.reshape(N_DEV, -1, LANES), w, m, v)


SMALL = (("pool_w_mix", 512), ("mix_norm", 8), ("ffn2_norm", 8), ("final_norm", 8), ("pool_scale", 8), ("sinks", 8),
         ("loss", 8), ("ffn1_norm", 8))
EARLY, LATE = SMALL[:-1], SMALL[-1:]


def _pack_small(parts, layout=SMALL):
    out = []
    for name, rows in layout:
        flat = parts[name].astype(F32).reshape(-1)
        out.append(jnp.pad(flat, (0, rows * LANES - flat.shape[0])).reshape(rows, LANES))
    return jnp.concatenate(out, axis=0)


def _unpack_small(packed, shapes):
    out, row = {}, 0
    for name, rows in SMALL:
        shape = shapes[name]
        size = int(np.prod(shape)) if shape else 1
        out[name] = packed[row:row + rows].reshape(-1)[:size].reshape(shape)
        row += rows
    return out


def kernel(x, ffn1_norm, ffn1_w_up, ffn1_w_down, mix_norm, w_in, sinks, w_attn_up, pool_w_mix, pool_scale, w_pool_up, w_out, ffn2_norm, ffn2_w_up, ffn2_w_down, final_norm, loss_target, m_ffn1_norm, m_ffn1_w_up, m_ffn1_w_down, m_mix_norm, m_w_in, m_sinks, m_w_attn_up, m_pool_w_mix, m_pool_scale, m_w_pool_up, m_w_out, m_ffn2_norm, m_ffn2_w_up, m_ffn2_w_down, m_final_norm, v_ffn1_norm, v_ffn1_w_up, v_ffn1_w_down, v_mix_norm, v_w_in, v_sinks, v_w_attn_up, v_pool_w_mix, v_pool_scale, v_w_pool_up, v_w_out, v_ffn2_norm, v_ffn2_w_up, v_ffn2_w_down, v_final_norm):
    args = dict(locals())
    weight_names = ("ffn1_norm", "ffn1_w_up", "ffn1_w_down", "mix_norm", "w_in", "sinks", "w_attn_up", "pool_w_mix",
                    "pool_scale", "w_pool_up", "w_out", "ffn2_norm", "ffn2_w_up", "ffn2_w_down", "final_norm")

    shard = {k: (args[p][0].T if tr else args[p][0]).astype(BF16) for k, p, tr in BIG}
    big = {"wup1_t": _exchange(_gather_carry([shard["wup1_t"]]), "gather_up1")[0]}

    def gathering(keys):
        return _gather_carry([shard[k] for k in keys])

    xs, target = x[0], loss_target[0]
    t = xs.shape[0]
    tm_f, tm_b, tk = min(512, t), min(512, t), min(1024, t)
    g1, gm, g2, gf = ffn1_norm, mix_norm, ffn2_norm, final_norm.reshape(1, D_MODEL)
    dist = _attn_dist()
    sink_v = sinks.reshape(N_Q_HEADS)
    wmix_b = pool_w_mix[0].astype(BF16)

    (n1, ab1, act1), (big["wdown1"], big["win_t"]) = _ffn_up(xs, g1, big["wup1_t"], tm_f, gathering(["wdown1", "win_t"]))
    (h1,), (big["wattn"], big["wpool_t"], big["wout"]) = _ffn_down(xs, act1, big["wdown1"], tm_f,
                                                                   gathering(["wattn", "wpool_t", "wout"]))
    (u, q, kv, z, gate), (big["wup2_t"],) = _mix_in_fwd(h1, gm, big["win_t"], tm_f, gathering(["wup2_t"]))
    attn = _attn_fwd(q, kv, dist, sink_v)
    (h2, a, p, merged, ms, pooled), (big["wdown2"],) = _mix_out_fwd(
        attn, z, gate, h1, big["wattn"], wmix_b, pool_scale, big["wpool_t"], big["wout"], tm_b, gathering(["wdown2"]))
    ab2, n2, act2, loss_lanes, dh3, dhb3, dgf = _ffn_loss(h2, g2, big["wup2_t"], big["wdown2"], gf, target, tm_f)

    got = {}
    (gw_down2,), _ = _wgrad(act2, dhb3, 0.5, D_FF, tk, "wgrad_down2")
    (dab2,), (got["wdown2"],) = _ffn_bwd_hidden(dhb3, ab2, big["wdown2"], tm_f, _scatter_carry([gw_down2]))
    (dh2, dg2), _ = _ffn_bwd_input(dab2, dh3, h2, g2, big["wup2_t"], tm_f)
    (gw_up2,), _ = _wgrad(dab2, n2, 1.0, D_FF, tk, "wgrad_up2")
    dhb2, da_b, dp_b, dattn, dgate, dpooled, dwmix, dscale = _mix_out_bwd(
        dh2, gate, a, p, pooled, big["wattn"], wmix_b, pool_scale, big["wpool_t"], big["wout"], tm_b)
    (gw_out,), _ = _wgrad(merged, dhb2, 1.0, D_MODEL, tk, "wgrad_out")
    (gw_attn,), _ = _wgrad(attn, da_b, 1.0, D_MODEL, tk, "wgrad_attn")
    (gw_pool,), _ = _wgrad(dp_b, ms, 1.0, D_MODEL, tk, "wgrad_pool")
    (dq, dkv_own, dkv_prev, dsinks), (got["wup2_t"],) = _attn_bwd(q, kv, dattn, dist, sink_v, _scatter_carry([gw_up2]))
    (dproj, dh1, dhb1, dgm), (got["wout"], got["wattn"], got["wpool_t"]) = _mix_in_bwd(
        dq, dkv_own, dkv_prev, dpooled, dgate, h1, gm, big["win_t"], dh2, tm_b,
        _scatter_carry([gw_out, gw_attn, gw_pool]))
    (gw_down1,), _ = _wgrad(act1, dhb1, 0.5, D_FF, tk, "wgrad_down1")
    (gw_in,), (got["wdown1"],) = _wgrad(dproj, u, 1.0, IN_WIDTH // 2, tk, "wgrad_in", _scatter_carry([gw_down1]))
    (dab1,), (got["win_t"],) = _ffn_bwd_hidden(dhb1, ab1, big["wdown1"], tm_f, _scatter_carry([gw_in]))
    small_parts = {"pool_w_mix": dwmix, "mix_norm": dgm, "ffn2_norm": dg2, "final_norm": dgf, "pool_scale": dscale,
                   "sinks": dsinks[:, :N_Q_HEADS], "loss": loss_lanes[:, :1]}
    (gw_up1,), (small_early,) = _wgrad(dab1, n1, 1.0, D_FF, tk, "wgrad_up1",
                                       _small_carry(_pack_small(small_parts, EARLY)))
    (dx, dg1), (got["wup1_t"],) = _ffn_bwd_input(dab1, dh1, xs, g1, big["wup1_t"], tm_f, _scatter_carry([gw_up1]))
    (small_late,) = _exchange(_small_carry(_pack_small({"ffn1_norm": dg1}, LATE)), "gather_small")

    grad, delta, new_m, new_v = {}, {}, {}, {}
    for k, p, tr in BIG:
        outside = tr and args[p].shape[-1] % LANES != 0
        turn = (lambda a: a.T) if outside else (lambda a: a)
        res = _sum_adamw(got[k], turn(args[p][0]), turn(args["m_" + p][0]), turn(args["v_" + p][0]),
                         tr and not outside, "adamw_" + k)
        grad[p], delta[p], new_m[p], new_v[p] = (turn(a)[None] for a in res)

    shapes = {name: args[name].shape for name, _ in SMALL if name != "loss"}
    shapes["loss"] = ()
    packed = {pre: _pack_small({**{name: args[pre + name] for name, _ in SMALL if name != "loss"},
                                "loss": jnp.zeros((), F32)}) for pre in ("", "m_", "v_")}
    g_s, d_s, m_s, v_s = _small_update(small_early, small_late, packed[""], packed["m_"], packed["v_"])
    g_small, d_small, m_small, v_small = (_unpack_small(a, shapes) for a in (g_s, d_s, m_s, v_s))
    for name, _ in SMALL:
        if name != "loss":
            grad[name], delta[name], new_m[name], new_v[name] = (
                g_small[name], d_small[name], m_small[name], v_small[name])

    return (g_small["loss"], dx[None], *[grad[n] for n in weight_names], *[delta[n] for n in weight_names],
            *[new_m[n] for n in weight_names], *[new_v[n] for n in weight_names])
```

```python
import jax
import jax.numpy as jnp
import numpy as np
from jax import lax
from jax.experimental import pallas as pl
from jax.experimental.pallas import tpu as pltpu

F32 = jnp.float32
BF16 = jnp.bfloat16

D_MODEL = 1024
D_FF = 2816
N_Q_HEADS = 16
N_KV_HEADS = 2
Q_PER_KV = N_Q_HEADS // N_KV_HEADS
HEAD_DIM = 64
BLOCK = 128
ATTN_WIDTH = N_Q_HEADS * HEAD_DIM
KV_WIDTH = N_KV_HEADS * HEAD_DIM
POOL_WINDOWS = (2, 4, 8, 16)
POOL_GROUP = 128
POOL_WIDTH = 512
HALO = 16
IN_WIDTH = ATTN_WIDTH + 2 * KV_WIDTH + POOL_WIDTH + 2 * D_MODEL
OFF_KV = ATTN_WIDTH
OFF_Z = ATTN_WIDTH + 2 * KV_WIDTH
OFF_GATE = OFF_Z + POOL_WIDTH
NORM_EPS = 1e-6
ADAM_LR = 0.001
ADAM_B1 = 0.9
ADAM_B2 = 0.999
ADAM_EPS = 1e-08
ADAM_WD = 0.01
ADAM_STEP = 10

N_DEV = 8
LANES = 128
FF_CHUNK = 256
SLAB = 32
GATHER_PIECES = 2
VMEM_LIMIT = 56 * 1024 * 1024
MESH = pl.DeviceIdType.MESH


def _nn(a, b):
    return jnp.dot(a, b, preferred_element_type=F32)


def _nt(a, b):
    return lax.dot_general(a, b, (((1,), (1,)), ((), ())), preferred_element_type=F32)


def _tn(a, b):
    return lax.dot_general(a, b, (((0,), (0,)), ((), ())), preferred_element_type=F32)


def _params(*sem):
    return pltpu.CompilerParams(dimension_semantics=sem, vmem_limit_bytes=VMEM_LIMIT)


def _resident(shape):
    return pl.BlockSpec(shape, lambda *_: (0,) * len(shape), pipeline_mode=pl.Buffered(1))


def _rows(tm, cols):
    return pl.BlockSpec((tm, cols), lambda i: (i, 0))


class _Carry:
    def __init__(self, inputs, out_shape, scratch, start, finish, middles=()):
        self.inputs, self.out_shape, self.scratch = list(inputs), list(out_shape), list(scratch)
        self.start, self.finish, self.middles = start, finish, list(middles)


def _launch(body, args, carry=None, *, name, grid, in_specs, out_specs, out_shape, scratch_shapes=(), semantics):
    in_specs, out_specs, out_shape, scratch_shapes = list(in_specs), list(out_specs), list(out_shape), list(scratch_shapes)
    if carry is None:
        res = pl.pallas_call(body, name=name, grid=grid, in_specs=in_specs, out_specs=out_specs, out_shape=out_shape,
                             scratch_shapes=scratch_shapes, compiler_params=_params(*semantics))(*args)
        return list(res), []
    ni, no, ns = len(in_specs), len(out_specs), len(scratch_shapes)
    ci, co = len(carry.inputs), len(carry.out_shape)
    total = int(np.prod(grid))

    def full(*refs):
        own_in, c_in = refs[:ni], refs[ni:ni + ci]
        own_out, c_out = refs[ni + ci:ni + ci + no], refs[ni + ci + no:ni + ci + no + co]
        own_scr, c_sem = refs[ni + ci + no + co:ni + ci + no + co + ns], refs[ni + ci + no + co + ns:]
        step = 0
        for axis, size in enumerate(grid):
            step = step * size + pl.program_id(axis)
        pl.when(step == 0)(lambda: carry.start(c_in, c_out, c_sem))
        for fraction, fn in carry.middles:
            at = min(total - 1, int(fraction * total) + 1)
            pl.when(step == at)(lambda fn=fn: fn(c_in, c_out, c_sem))
        body(*own_in, *own_out, *own_scr)
        pl.when(step == total - 1)(lambda: carry.finish(c_in, c_out, c_sem))

    res = pl.pallas_call(
        full, name=name, grid=grid, in_specs=in_specs + [ANY] * ci, out_specs=out_specs + [ANY] * co,
        out_shape=out_shape + carry.out_shape, scratch_shapes=scratch_shapes + carry.scratch,
        compiler_params=_params(*(["arbitrary"] * len(grid))),
    )(*args, *carry.inputs)
    return list(res[:no]), list(res[no:])


def _rms_fwd(xv, g):
    r = lax.rsqrt(jnp.mean(xv * xv, axis=-1, keepdims=True) + NORM_EPS)
    return xv * r, r


def _rms_bwd(dn, xh, r, g):
    dxh = dn * g
    dx = r * (dxh - xh * jnp.mean(dxh * xh, axis=-1, keepdims=True))
    return dx, jnp.sum(dn * xh, axis=0, keepdims=True)


def _ffn_loss(x, g, wup_t, wdown, gf, target, tm):
    t, d = x.shape
    f = wdown.shape[0]

    def body(x_ref, g_ref, wup_ref, wdn_ref, gf_ref, tgt_ref, ab_ref, n_ref, act_ref, loss_ref, dh_ref, dhb_ref, dg_ref):
        xv = x_ref[...]
        xh, _ = _rms_fwd(xv, g_ref[...])
        n = (xh * g_ref[...]).astype(BF16)
        n_ref[...] = n
        for c in range(f // FF_CHUNK):
            lo, hi = c * FF_CHUNK, (c + 1) * FF_CHUNK
            a = _nt(n, wup_ref[lo:hi, :])
            b = _nt(n, wup_ref[f + lo:f + hi, :])
            ab_ref[:, lo:hi] = a.astype(BF16)
            ab_ref[:, f + lo:f + hi] = b.astype(BF16)
            act_ref[:, lo:hi] = (a * jax.nn.sigmoid(a) * b).astype(BF16)
        h = xv + 0.5 * _nn(act_ref[...], wdn_ref[...])
        yh, r = _rms_fwd(h, gf_ref[...])
        err = yh * gf_ref[...] - tgt_ref[...]
        part = 0.5 * jnp.sum(jnp.mean(err * err, axis=-1, keepdims=True), axis=0, keepdims=True)
        dh, dg = _rms_bwd(err * (1.0 / d), yh, r, gf_ref[...])
        dh_ref[...] = dh
        dhb_ref[...] = dh.astype(BF16)

        @pl.when(pl.program_id(0) == 0)
        def _():
            dg_ref[...] = jnp.zeros_like(dg_ref)
            loss_ref[...] = jnp.zeros_like(loss_ref)

        dg_ref[...] += dg
        loss_ref[...] += jnp.broadcast_to(part, loss_ref.shape)

    return pl.pallas_call(
        body, name="ffn_loss", grid=(t // tm,),
        in_specs=[_rows(tm, d), _resident((1, d)), _resident((2 * f, d)), _resident((f, d)), _resident((1, d)),
                  _rows(tm, d)],
        out_specs=[_rows(tm, 2 * f), _rows(tm, d), _rows(tm, f), pl.BlockSpec((1, LANES), lambda i: (0, 0)),
                   _rows(tm, d), _rows(tm, d), pl.BlockSpec((1, d), lambda i: (0, 0))],
        out_shape=[jax.ShapeDtypeStruct((t, 2 * f), BF16), jax.ShapeDtypeStruct((t, d), BF16),
                   jax.ShapeDtypeStruct((t, f), BF16), jax.ShapeDtypeStruct((1, LANES), F32),
                   jax.ShapeDtypeStruct((t, d), F32), jax.ShapeDtypeStruct((t, d), BF16),
                   jax.ShapeDtypeStruct((1, d), F32)],
        compiler_params=_params("arbitrary"),
    )(x, g, wup_t, wdown, gf, target)


def _ffn_up(x, g, wup_t, tm, carry=None):
    t, d = x.shape
    f = wup_t.shape[0] // 2

    def body(x_ref, g_ref, wup_ref, n_ref, ab_ref, act_ref):
        xh, _ = _rms_fwd(x_ref[...], g_ref[...])
        n = (xh * g_ref[...]).astype(BF16)
        n_ref[...] = n
        for c in range(f // FF_CHUNK):
            lo, hi = c * FF_CHUNK, (c + 1) * FF_CHUNK
            a = _nt(n, wup_ref[lo:hi, :])
            b = _nt(n, wup_ref[f + lo:f + hi, :])
            ab_ref[:, lo:hi] = a.astype(BF16)
            ab_ref[:, f + lo:f + hi] = b.astype(BF16)
            act_ref[:, lo:hi] = (a * jax.nn.sigmoid(a) * b).astype(BF16)

    return _launch(
        body, (x, g, wup_t), carry, name="ffn_up", grid=(t // tm,),
        in_specs=[_rows(tm, d), _resident((1, d)), _resident((2 * f, d))],
        out_specs=[_rows(tm, d), _rows(tm, 2 * f), _rows(tm, f)],
        out_shape=[jax.ShapeDtypeStruct((t, d), BF16), jax.ShapeDtypeStruct((t, 2 * f), BF16),
                   jax.ShapeDtypeStruct((t, f), BF16)],
        semantics=("parallel",))


def _ffn_down(x, act, wdown, tm, carry=None):
    t, d = x.shape
    f = wdown.shape[0]

    def body(x_ref, act_ref, wdn_ref, h_ref):
        h_ref[...] = x_ref[...] + 0.5 * _nn(act_ref[...], wdn_ref[...])

    return _launch(
        body, (x, act, wdown), carry, name="ffn_down", grid=(t // tm,),
        in_specs=[_rows(tm, d), _rows(tm, f), _resident((f, d))], out_specs=[_rows(tm, d)],
        out_shape=[jax.ShapeDtypeStruct((t, d), F32)], semantics=("parallel",))


def _ffn_bwd_hidden(dhb, ab, wdown, tm, carry=None):
    t, d = dhb.shape
    f = wdown.shape[0]

    def body(dh_ref, ab_ref, wdn_ref, dab_ref, dact_ref):
        half = dh_ref[...] * 0.5
        for c in range(f // FF_CHUNK):
            lo, hi = c * FF_CHUNK, (c + 1) * FF_CHUNK
            dact_ref[...] = _nt(half, wdn_ref[lo:hi, :])

            def slab(i, carry_):
                rows = pl.ds(pl.multiple_of(i * SLAB, SLAB), SLAB)
                a = ab_ref[rows, lo:hi].astype(F32)
                b = ab_ref[rows, f + lo:f + hi].astype(F32)
                s = jax.nn.sigmoid(a)
                ds_ = dact_ref[rows, :] * s
                dab_ref[rows, lo:hi] = (ds_ * b * (1.0 + a * (1.0 - s))).astype(BF16)
                dab_ref[rows, f + lo:f + hi] = (ds_ * a).astype(BF16)
                return carry_

            lax.fori_loop(0, tm // SLAB, slab, 0, unroll=True)

    return _launch(
        body, (dhb, ab, wdown), carry, name="ffn_bwd_hidden", grid=(t // tm,),
        in_specs=[_rows(tm, d), _rows(tm, 2 * f), _resident((f, d))], out_specs=[_rows(tm, 2 * f)],
        out_shape=[jax.ShapeDtypeStruct((t, 2 * f), BF16)],
        scratch_shapes=[pltpu.VMEM((tm, FF_CHUNK), F32)], semantics=("parallel",))


def _ffn_bwd_input(dab, dh, x, g, wup_t, tm, carry=None):
    t, d = x.shape
    f2 = wup_t.shape[0]

    def body(dab_ref, dh_ref, x_ref, g_ref, wup_ref, dx_ref, dg_ref):
        dn = _nn(dab_ref[...], wup_ref[...])
        xh, r = _rms_fwd(x_ref[...], g_ref[...])
        dx, dg = _rms_bwd(dn, xh, r, g_ref[...])
        dx_ref[...] = dh_ref[...] + dx

        @pl.when(pl.program_id(0) == 0)
        def _():
            dg_ref[...] = jnp.zeros_like(dg_ref)

        dg_ref[...] += dg

    return _launch(
        body, (dab, dh, x, g, wup_t), carry, name="ffn_bwd_input", grid=(t // tm,),
        in_specs=[_rows(tm, f2), _rows(tm, d), _rows(tm, d), _resident((1, d)), _resident((f2, d))],
        out_specs=[_rows(tm, d), pl.BlockSpec((1, d), lambda i: (0, 0))],
        out_shape=[jax.ShapeDtypeStruct((t, d), F32), jax.ShapeDtypeStruct((1, d), F32)],
        semantics=("arbitrary",))


def _wgrad(lhs, rhs, scale, bm, tk, name, carry=None):
    t, m = lhs.shape
    n = rhs.shape[1]
    steps = t // tk
    chunk = bm if bm <= 2048 else bm // 2

    def body(l_ref, r_ref, o_ref, acc_ref):
        @pl.when(pl.program_id(1) == 0)
        def _():
            acc_ref[...] = jnp.zeros_like(acc_ref)

        for lo in range(0, bm, chunk):
            acc_ref[lo:lo + chunk, :] += _tn(l_ref[:, lo:lo + chunk], r_ref[...])

        @pl.when(pl.program_id(1) == steps - 1)
        def _():
            o_ref[...] = (scale * acc_ref[...]).astype(o_ref.dtype)

    return _launch(
        body, (lhs, rhs), carry, name=name, grid=(m // bm, steps),
        in_specs=[pl.BlockSpec((tk, bm), lambda i, k: (k, i)), pl.BlockSpec((tk, n), lambda i, k: (k, 0))],
        out_specs=[pl.BlockSpec((bm, n), lambda i, k: (i, 0))],
        out_shape=[jax.ShapeDtypeStruct((m, n), WIRE)],
        scratch_shapes=[pltpu.VMEM((bm, n), F32)], semantics=("parallel", "arbitrary"))


def _mix_in_fwd(h, g, win_t, tm, carry=None):
    t, d = h.shape

    def body(h_ref, g_ref, w_ref, u_ref, q_ref, kv_ref, z_ref, gate_ref):
        xh, _ = _rms_fwd(h_ref[...], g_ref[...])
        u = (xh * g_ref[...]).astype(BF16)
        u_ref[...] = u
        q_ref[...] = _nt(u, w_ref[0:OFF_KV, :]).astype(BF16)
        kv_ref[...] = _nt(u, w_ref[OFF_KV:OFF_Z, :]).astype(BF16)
        z_ref[...] = _nt(u, w_ref[OFF_Z:OFF_GATE, :])
        gate_ref[...] = _nt(u, w_ref[OFF_GATE:IN_WIDTH, :]).astype(BF16)

    return _launch(
        body, (h, g, win_t), carry, name="mix_in_fwd", grid=(t // tm,),
        in_specs=[_rows(tm, d), _resident((1, d)), _resident((IN_WIDTH, d))],
        out_specs=[_rows(tm, d), _rows(tm, ATTN_WIDTH), _rows(tm, 2 * KV_WIDTH), _rows(tm, POOL_WIDTH),
                   _rows(tm, 2 * D_MODEL)],
        out_shape=[jax.ShapeDtypeStruct((t, d), BF16), jax.ShapeDtypeStruct((t, ATTN_WIDTH), BF16),
                   jax.ShapeDtypeStruct((t, 2 * KV_WIDTH), BF16), jax.ShapeDtypeStruct((t, POOL_WIDTH), F32),
                   jax.ShapeDtypeStruct((t, 2 * D_MODEL), BF16)],
        semantics=("parallel",))


ALIBI_SLOPES = tuple(float(s) for s in (2.0 ** (-8.0 * np.arange(1, N_Q_HEADS + 1, dtype=np.float32) / N_Q_HEADS)))


def _attn_dist():
    return jnp.asarray(((np.arange(BLOCK)[:, None] - np.arange(BLOCK)[None, :]) % BLOCK).astype(np.float32))


def _own_block():
    shape = (BLOCK, BLOCK)
    return lax.broadcasted_iota(jnp.int32, shape, 1) <= lax.broadcasted_iota(jnp.int32, shape, 0)


def _fold(band2, own):
    return jnp.where(own, band2[:, BLOCK:], band2[:, :BLOCK])


def _unfold(x, own):
    zero = jnp.zeros_like(x)
    return jnp.concatenate([jnp.where(own, zero, x), jnp.where(own, x, zero)], axis=1)


def _low_half(shape):
    return lax.broadcasted_iota(jnp.int32, shape, len(shape) - 1) < HEAD_DIM


def _both_halves(band, kv_head):
    low = _low_half(band.shape)
    swapped = pltpu.roll(band, HEAD_DIM, 1)
    return jnp.where(low, band, swapped) if kv_head == 0 else jnp.where(low, swapped, band)


def _pair_rows(ref, rows, pair, scale=None):
    v = ref[rows, LANES * pair:LANES * (pair + 1)]
    if scale is not None:
        v = v * scale
    low, zero = _low_half(v.shape), jnp.zeros_like(v)
    return jnp.concatenate([jnp.where(low, v, zero), jnp.where(low, zero, v)], axis=0)


def _per_head(even, odd):
    return jnp.where(lax.broadcasted_iota(jnp.int32, (2 * BLOCK, 1), 0) < BLOCK, even, odd)


def _twice(x):
    return jnp.concatenate([x, x], axis=0)


def _pair_scores(q_ref, rows, kk, dist2, pair, first, own2):
    s2 = _nt(_pair_rows(q_ref, rows, pair, HEAD_DIM ** -0.5), kk)
    before = jnp.where(first, -jnp.inf, s2[:, :BLOCK])
    slopes = _per_head(ALIBI_SLOPES[2 * pair], ALIBI_SLOPES[2 * pair + 1])
    return jnp.where(own2, s2[:, BLOCK:], before) - slopes * dist2


def _own_half(ref, rows, head):
    v = ref[rows, LANES * (head // 2):LANES * (head // 2 + 1)]
    low = _low_half(v.shape)
    return jnp.where(low if head % 2 == 0 else jnp.logical_not(low), v, jnp.zeros_like(v))


def _head_scores(q_ref, rows, kk, dist, head, first, own):
    s2 = _nt(_own_half(q_ref, rows, head) * HEAD_DIM ** -0.5, kk)
    before = jnp.where(first, -jnp.inf, s2[:, :BLOCK])
    return jnp.where(own, s2[:, BLOCK:], before) - ALIBI_SLOPES[head] * dist


def _heads_of(stack):
    return jnp.where(_low_half((BLOCK, LANES)), stack[:BLOCK], stack[BLOCK:])


def _softmax_sink(s, sink):
    m = jnp.maximum(jnp.max(s, axis=-1, keepdims=True), sink)
    p = jnp.exp(s - m)
    psink = jnp.exp(sink - m)
    inv = 1.0 / (jnp.sum(p, axis=-1, keepdims=True) + psink)
    return p * inv, psink * inv


def _bands(kvc_ref, kvp_ref, sub):
    own = slice(sub * BLOCK, (sub + 1) * BLOCK)
    before = kvp_ref[...] if sub == 0 else kvc_ref[(sub - 1) * BLOCK:sub * BLOCK, :]
    kband = jnp.concatenate([before[:, 0:LANES], kvc_ref[own, 0:LANES]], axis=0)
    vband = jnp.concatenate([before[:, LANES:2 * LANES], kvc_ref[own, LANES:2 * LANES]], axis=0)
    return ([_both_halves(kband, hk) for hk in range(N_KV_HEADS)],
            [_both_halves(vband, hk) for hk in range(N_KV_HEADS)])


SMEM = pl.BlockSpec(memory_space=pltpu.SMEM)
HEADS = range(N_Q_HEADS)
PAIRS = range(N_Q_HEADS // 2)
PAIRS_PER_KV = Q_PER_KV // 2
def _sub_rows(sub):
    return slice(sub * BLOCK, (sub + 1) * BLOCK)


def _block_before(step):
    per = step // BLOCK
    return pl.BlockSpec((BLOCK, 2 * KV_WIDTH), lambda i: (jnp.maximum(i * per - 1, 0), 0))


def _attn_fwd(q, kv, dist, sinks):
    t = q.shape[0]
    subs = range(2)

    def body(q_ref, kvc_ref, kvp_ref, dist_ref, sink_ref, o_ref, s_scr, p_scr):
        own2 = _twice(_own_block())
        dist2 = _twice(dist_ref[...])
        bands = [_bands(kvc_ref, kvp_ref, sub) for sub in subs]
        for sub in subs:
            first = jnp.logical_and(pl.program_id(0) == 0, sub == 0)
            for pair in PAIRS:
                s_scr[sub, pair] = _pair_scores(q_ref, _sub_rows(sub), bands[sub][0][pair // PAIRS_PER_KV], dist2, pair,
                                                first, own2)
        for sub in subs:
            for pair in PAIRS:
                probs, _ = _softmax_sink(s_scr[sub, pair], _per_head(sink_ref[2 * pair], sink_ref[2 * pair + 1]))
                p_scr[sub, pair] = _unfold(probs.astype(BF16), own2)
        for sub in subs:
            for pair in PAIRS:
                out = _nn(p_scr[sub, pair], bands[sub][1][pair // PAIRS_PER_KV])
                o_ref[_sub_rows(sub), LANES * pair:LANES * (pair + 1)] = _heads_of(out).astype(BF16)

    step = len(subs) * BLOCK
    return pl.pallas_call(
        body, name="attn_fwd", grid=(t // step,),
        in_specs=[_rows(step, ATTN_WIDTH), _rows(step, 2 * KV_WIDTH), _block_before(step), _resident(dist.shape), SMEM],
        out_specs=_rows(step, ATTN_WIDTH),
        out_shape=jax.ShapeDtypeStruct((t, ATTN_WIDTH), BF16),
        scratch_shapes=[pltpu.VMEM((len(subs), len(PAIRS), 2 * BLOCK, BLOCK), F32),
                        pltpu.VMEM((len(subs), len(PAIRS), 2 * BLOCK, 2 * BLOCK), BF16)],
        compiler_params=_params("parallel"),
    )(q, kv, kv, dist, sinks)


def _pool_counts(tm, width):
    row = pl.program_id(0) * tm + lax.broadcasted_iota(jnp.int32, (tm, 1), 0)
    return jnp.minimum(row + 1, width).astype(F32)


def _trailing_sums(zz, group):
    s = zz
    for k in range(group + 1):
        s = s + pltpu.roll(s, 1 << k, 0)
    return s


def _leading_sums(zz, group):
    rows = zz.shape[0]
    s = zz
    for k in range(group + 1):
        s = s + pltpu.roll(s, rows - (1 << k), 0)
    return s


def _mix_out_fwd(attn, z, gate, h, wattn, wmix, scale, wpool_t, wout, tm, carry=None):
    t, d = h.shape

    def body(attn_ref, z_ref, halo_ref, gate_ref, h_ref, wattn_ref, wmix_ref, scale_ref, wpool_ref, wout_ref,
             h2_ref, a_ref, p_ref, merged_ref, ms_ref, pooled_ref):
        halo = jnp.where(pl.program_id(0) == 0, 0.0, halo_ref[...])
        for gi, width in enumerate(POOL_WINDOWS):
            lo, hi = gi * POOL_GROUP, (gi + 1) * POOL_GROUP
            zg = z_ref[:, lo:hi]
            sums = _trailing_sums(jnp.concatenate([halo[:, lo:hi], zg], axis=0), gi)[HALO:, :]
            pooled = (sums / _pool_counts(tm, width) - zg).astype(BF16)
            pooled_ref[:, lo:hi] = pooled
            ms_ref[:, lo:hi] = (_nn(pooled, wmix_ref[gi]) * scale_ref[:, lo:hi]).astype(BF16)
        p = _nt(ms_ref[...], wpool_ref[...])
        a = _nn(attn_ref[...], wattn_ref[...])
        a_ref[...] = a.astype(BF16)
        p_ref[...] = p.astype(BF16)
        merged = (jax.nn.sigmoid(gate_ref[:, 0:d].astype(F32)) * a
                  + jax.nn.sigmoid(gate_ref[:, d:2 * d].astype(F32)) * p).astype(BF16)
        merged_ref[...] = merged
        h2_ref[...] = h_ref[...] + _nn(merged, wout_ref[...])

    halo_spec = pl.BlockSpec((HALO, POOL_WIDTH), lambda i: (jnp.maximum(i * (tm // HALO) - 1, 0), 0))
    return _launch(
        body, (attn, z, z, gate, h, wattn, wmix, scale, wpool_t, wout), carry, name="mix_out_fwd", grid=(t // tm,),
        in_specs=[_rows(tm, ATTN_WIDTH), _rows(tm, POOL_WIDTH), halo_spec, _rows(tm, 2 * d), _rows(tm, d),
                  _resident(wattn.shape), _resident(wmix.shape), _resident(scale.shape), _resident(wpool_t.shape),
                  _resident(wout.shape)],
        out_specs=[_rows(tm, d), _rows(tm, d), _rows(tm, d), _rows(tm, d), _rows(tm, POOL_WIDTH),
                   _rows(tm, POOL_WIDTH)],
        out_shape=[jax.ShapeDtypeStruct((t, d), F32), jax.ShapeDtypeStruct((t, d), BF16),
                   jax.ShapeDtypeStruct((t, d), BF16), jax.ShapeDtypeStruct((t, d), BF16),
                   jax.ShapeDtypeStruct((t, POOL_WIDTH), BF16), jax.ShapeDtypeStruct((t, POOL_WIDTH), BF16)],
        semantics=("parallel",))


def _mix_out_bwd(dh, gate, a, p, pooled, wattn, wmix, scale, wpool_t, wout, tm):
    t, d = dh.shape

    def body(dh_ref, gate_ref, a_ref, p_ref, pooled_ref, wattn_ref, wmix_ref, scale_ref, wpool_ref, wout_ref,
             dhb_ref, dab_ref, dpb_ref, dattn_ref, dgate_ref, dpooled_ref, dwmix_ref, dscale_ref):
        @pl.when(pl.program_id(0) == 0)
        def _():
            dwmix_ref[...] = jnp.zeros_like(dwmix_ref)
            dscale_ref[...] = jnp.zeros_like(dscale_ref)

        dhb = dh_ref[...].astype(BF16)
        dhb_ref[...] = dhb
        dm = _nt(dhb, wout_ref[...])
        sa = jax.nn.sigmoid(gate_ref[:, 0:d].astype(F32))
        sp = jax.nn.sigmoid(gate_ref[:, d:2 * d].astype(F32))
        da = (dm * sa).astype(BF16)
        dp = (dm * sp).astype(BF16)
        dab_ref[...] = da
        dpb_ref[...] = dp
        dgate_ref[:, 0:d] = (dm * a_ref[...].astype(F32) * (sa * (1.0 - sa))).astype(BF16)
        dgate_ref[:, d:2 * d] = (dm * p_ref[...].astype(F32) * (sp * (1.0 - sp))).astype(BF16)
        dattn_ref[...] = _nt(da, wattn_ref[...]).astype(BF16)
        dms = _nn(dp, wpool_ref[...])
        for gi in range(len(POOL_WINDOWS)):
            lo, hi = gi * POOL_GROUP, (gi + 1) * POOL_GROUP
            pooled_g = pooled_ref[:, lo:hi]
            mixed = _nn(pooled_g, wmix_ref[gi])
            dscale_ref[:, lo:hi] += jnp.sum(dms[:, lo:hi] * mixed, axis=0, keepdims=True)
            dmixed = (dms[:, lo:hi] * scale_ref[:, lo:hi]).astype(BF16)
            dwmix_ref[gi] += _tn(pooled_g, dmixed)
            dpooled_ref[:, lo:hi] = _nt(dmixed, wmix_ref[gi])

    acc = lambda shape: pl.BlockSpec(shape, lambda i: (0,) * len(shape))
    return pl.pallas_call(
        body, name="mix_out_bwd", grid=(t // tm,),
        in_specs=[_rows(tm, d), _rows(tm, 2 * d), _rows(tm, d), _rows(tm, d), _rows(tm, POOL_WIDTH),
                  _resident(wattn.shape), _resident(wmix.shape), _resident(scale.shape), _resident(wpool_t.shape),
                  _resident(wout.shape)],
        out_specs=[_rows(tm, d), _rows(tm, d), _rows(tm, d), _rows(tm, ATTN_WIDTH), _rows(tm, 2 * d),
                   _rows(tm, POOL_WIDTH), acc(wmix.shape), acc((1, POOL_WIDTH))],
        out_shape=[jax.ShapeDtypeStruct((t, d), BF16), jax.ShapeDtypeStruct((t, d), BF16),
                   jax.ShapeDtypeStruct((t, d), BF16), jax.ShapeDtypeStruct((t, ATTN_WIDTH), BF16),
                   jax.ShapeDtypeStruct((t, 2 * d), BF16), jax.ShapeDtypeStruct((t, POOL_WIDTH), F32),
                   jax.ShapeDtypeStruct(wmix.shape, F32), jax.ShapeDtypeStruct((1, POOL_WIDTH), F32)],
        compiler_params=_params("arbitrary"),
    )(dh, gate, a, p, pooled, wattn, wmix, scale, wpool_t, wout)


def _fold_halves(x):
    return x + pltpu.roll(x, HEAD_DIM, 1)


def _attn_bwd(q, kv, dattn, dist, sinks, carry=None):
    t = q.shape[0]
    subs = range(1)

    def body(q_ref, kvc_ref, kvp_ref, do_ref, dist_ref, sink_ref, dq_ref, dkv_own_ref, dkv_prev_ref, dsink_ref,
             s_scr, dp_scr, p_scr, ds_scr):
        @pl.when(pl.program_id(0) == 0)
        def _():
            dsink_ref[...] = jnp.zeros_like(dsink_ref)

        own = _own_block()
        dist_v = dist_ref[...]
        bands = [_bands(kvc_ref, kvp_ref, sub) for sub in subs]
        lane = lax.broadcasted_iota(jnp.int32, (1, LANES), 1)
        for sub in subs:
            first = jnp.logical_and(pl.program_id(0) == 0, sub == 0)
            for head in HEADS:
                hk = head // Q_PER_KV
                s_scr[sub, head] = _head_scores(q_ref, _sub_rows(sub), bands[sub][0][hk], dist_v, head, first, own)
                dp_scr[sub, head] = _fold(_nt(_own_half(do_ref, _sub_rows(sub), head), bands[sub][1][hk]), own)
        dsink = jnp.zeros((1, LANES), F32)
        for sub in subs:
            for head in HEADS:
                probs, psink = _softmax_sink(s_scr[sub, head], sink_ref[head])
                dprobs = dp_scr[sub, head]
                rowdot = jnp.sum(probs * dprobs, axis=-1, keepdims=True)
                p_scr[sub, head] = _unfold(probs.astype(BF16), own)
                ds_scr[sub, head] = _unfold((probs * (dprobs - rowdot)).astype(BF16), own)
                dsink = dsink + jnp.where(lane == head, jnp.sum(-psink * rowdot, axis=0, keepdims=True), 0.0)
        for sub in subs:
            rows = _sub_rows(sub)
            dk_heads, dv_heads = [], []
            for hk in range(N_KV_HEADS):
                dk_t = jnp.zeros((LANES, 2 * BLOCK), F32)
                dv_t = jnp.zeros((LANES, 2 * BLOCK), F32)
                for pair in range(Q_PER_KV // 2):
                    cols = slice(LANES * (hk * PAIRS_PER_KV + pair), LANES * (hk * PAIRS_PER_KV + pair + 1))
                    q_t = (q_ref[rows, cols] * HEAD_DIM ** -0.5).T
                    do_t = do_ref[rows, cols].T
                    dqs = []
                    for head in (hk * Q_PER_KV + 2 * pair, hk * Q_PER_KV + 2 * pair + 1):
                        mine = (lax.broadcasted_iota(jnp.int32, q_t.shape, 0) < HEAD_DIM) == (head % 2 == 0)
                        dv_t = dv_t + _nn(jnp.where(mine, do_t, jnp.zeros_like(do_t)), p_scr[sub, head])
                        dk_t = dk_t + _nn(jnp.where(mine, q_t, jnp.zeros_like(q_t)), ds_scr[sub, head])
                        dqs.append(_nn(ds_scr[sub, head], bands[sub][0][hk]))
                    dq_pair = jnp.where(_low_half(dqs[0].shape), dqs[0], dqs[1])
                    dq_ref[rows, cols] = (dq_pair * HEAD_DIM ** -0.5).astype(BF16)
                dk_heads.append(_fold_halves(dk_t.T))
                dv_heads.append(_fold_halves(dv_t.T))
            low = _low_half(dk_heads[0].shape)
            dkv = jnp.concatenate([jnp.where(low, dk_heads[0], dk_heads[1]), jnp.where(low, dv_heads[0], dv_heads[1])],
                                  axis=1)
            dkv_prev_ref[rows, :] = dkv[0:BLOCK, :]
            dkv_own_ref[rows, :] = dkv[BLOCK:2 * BLOCK, :]
        dsink_ref[...] += dsink

    step = len(subs) * BLOCK
    return _launch(
        body, (q, kv, kv, dattn, dist, sinks), carry, name="attn_bwd", grid=(t // step,),
        in_specs=[_rows(step, ATTN_WIDTH), _rows(step, 2 * KV_WIDTH), _block_before(step), _rows(step, ATTN_WIDTH),
                  _resident(dist.shape), SMEM],
        out_specs=[_rows(step, ATTN_WIDTH), _rows(step, 2 * KV_WIDTH), _rows(step, 2 * KV_WIDTH),
                   pl.BlockSpec((1, LANES), lambda i: (0, 0))],
        out_shape=[jax.ShapeDtypeStruct((t, ATTN_WIDTH), BF16), jax.ShapeDtypeStruct((t, 2 * KV_WIDTH), F32),
                   jax.ShapeDtypeStruct((t, 2 * KV_WIDTH), F32), jax.ShapeDtypeStruct((1, LANES), F32)],
        scratch_shapes=[pltpu.VMEM((len(subs), N_Q_HEADS, BLOCK, BLOCK), F32),
                        pltpu.VMEM((len(subs), N_Q_HEADS, BLOCK, BLOCK), F32),
                        pltpu.VMEM((len(subs), N_Q_HEADS, BLOCK, 2 * BLOCK), BF16),
                        pltpu.VMEM((len(subs), N_Q_HEADS, BLOCK, 2 * BLOCK), BF16)],
        semantics=("arbitrary",))


def _mix_in_bwd(dq, dkv_own, dkv_prev, dpooled, dgate, h, g, win_t, dh_res, tm, carry=None):
    t, d = h.shape
    nt = t // tm

    def body(dq_ref, own_ref, prev_ref, prev_next_ref, dpool_ref, halo_ref, dgate_ref, h_ref, g_ref, w_ref, res_ref,
             dproj_ref, dh_ref, dhb_ref, dg_ref):
        i = pl.program_id(0)
        last = i == nt - 1
        dproj_ref[:, 0:OFF_KV] = dq_ref[...]
        from_next = jnp.where(last, 0.0, prev_next_ref[...])
        if tm > BLOCK:
            from_next = jnp.concatenate([prev_ref[BLOCK:tm, :], from_next], axis=0)
        dproj_ref[:, OFF_KV:OFF_Z] = (own_ref[...] + from_next).astype(BF16)
        halo = jnp.where(last, 0.0, halo_ref[...])
        for gi, width in enumerate(POOL_WINDOWS):
            lo, hi = gi * POOL_GROUP, (gi + 1) * POOL_GROUP
            dpg = dpool_ref[:, lo:hi]
            scaled = jnp.concatenate([dpg / _pool_counts(tm, width), halo[:, lo:hi] / float(width)], axis=0)
            dz = _leading_sums(scaled, gi)[0:tm, :] - dpg
            dproj_ref[:, OFF_Z + lo:OFF_Z + hi] = dz.astype(BF16)
        dproj_ref[:, OFF_GATE:IN_WIDTH] = dgate_ref[...]
        du = _nn(dproj_ref[...], w_ref[...])
        xh, r = _rms_fwd(h_ref[...], g_ref[...])
        dx, dg = _rms_bwd(du, xh, r, g_ref[...])
        dh = res_ref[...] + dx
        dh_ref[...] = dh
        dhb_ref[...] = dh.astype(BF16)

        @pl.when(i == 0)
        def _():
            dg_ref[...] = jnp.zeros_like(dg_ref)

        dg_ref[...] += dg

    per = tm // BLOCK
    next_block = pl.BlockSpec((BLOCK, 2 * KV_WIDTH), lambda i: (jnp.minimum((i + 1) * per, t // BLOCK - 1), 0))
    next_halo = pl.BlockSpec((HALO, POOL_WIDTH), lambda i: (jnp.minimum((i + 1) * (tm // HALO), t // HALO - 1), 0))
    return _launch(
        body, (dq, dkv_own, dkv_prev, dkv_prev, dpooled, dpooled, dgate, h, g, win_t, dh_res), carry,
        name="mix_in_bwd", grid=(nt,),
        in_specs=[_rows(tm, ATTN_WIDTH), _rows(tm, 2 * KV_WIDTH), _rows(tm, 2 * KV_WIDTH), next_block,
                  _rows(tm, POOL_WIDTH), next_halo, _rows(tm, 2 * d), _rows(tm, d), _resident((1, d)),
                  _resident((IN_WIDTH, d)), _rows(tm, d)],
        out_specs=[_rows(tm, IN_WIDTH), _rows(tm, d), _rows(tm, d), pl.BlockSpec((1, d), lambda i: (0, 0))],
        out_shape=[jax.ShapeDtypeStruct((t, IN_WIDTH), BF16), jax.ShapeDtypeStruct((t, d), F32),
                   jax.ShapeDtypeStruct((t, d), BF16), jax.ShapeDtypeStruct((1, d), F32)],
        semantics=("arbitrary",))


BIG = (("wup1_t", "ffn1_w_up", True), ("wdown1", "ffn1_w_down", False), ("win_t", "w_in", True),
       ("wattn", "w_attn_up", False), ("wpool_t", "w_pool_up", True), ("wout", "w_out", False),
       ("wup2_t", "ffn2_w_up", True), ("wdown2", "ffn2_w_down", False))
ANY = pl.BlockSpec(memory_space=pl.ANY)
WIRE = BF16


def _place():
    return lax.axis_index("x"), lax.axis_index("y"), lax.axis_index("c")


def _peer(k):
    x, y, c = _place()
    return x ^ (k >> 2), y ^ ((k >> 1) & 1), c ^ (k & 1)


def _index(px, py, pc):
    return 4 * px + 2 * py + pc


def _gather_carry(shards):
    n = len(shards) * GATHER_PIECES

    def tools(ins, outs, sems):
        send_sems, recv_sems, local_sems = sems
        x, y, c = _place()
        chips = [(1 - x, y), (x, 1 - y), (1 - x, 1 - y)]

        def piece(item):
            w, q = divmod(item, GATHER_PIECES)
            r = ins[w].shape[0]
            return w, r, q * (r // GATHER_PIECES), r // GATHER_PIECES

        def mine(item):
            w, _, first, size = piece(item)
            return ins[w].at[pl.ds(first, size), :]

        def rows(item, px, py, pc):
            w, r, first, size = piece(item)
            return outs[w].at[pl.ds(_index(px, py, pc) * r + first, size), :]

        def copy(item, k, block, to, src=None):
            return pltpu.make_async_remote_copy(
                src_ref=rows(item, *block) if src is None else src, dst_ref=rows(item, *block),
                send_sem=send_sems.at[item, k], recv_sem=recv_sems.at[item, k], device_id=to, device_id_type=MESH)

        def own(item):
            return ([pltpu.make_async_copy(mine(item), rows(item, x, y, c), local_sems.at[item]),
                     copy(item, 0, (x, y, c), (x, y, 1 - c), src=mine(item))]
                    + [copy(item, 1 + j, (x, y, c), (*chip, c), src=mine(item)) for j, chip in enumerate(chips)])

        def passed(item, j):
            return copy(item, 4 + j, (*chips[j], c), (x, y, 1 - c))

        return (x, y, c), chips, copy, own, passed

    def start(ins, outs, sems):
        _, _, _, own, _ = tools(ins, outs, sems)
        for item in range(n):
            for cp in own(item):
                cp.start()

    def forward(item):
        def run(ins, outs, sems):
            (x, y, c), chips, copy, _, passed = tools(ins, outs, sems)
            for j, chip in enumerate(chips):
                copy(item, 1 + j, (*chip, c), (x, y, c)).wait_recv()
                passed(item, j).start()
        return run

    sizes = np.cumsum([s.size / GATHER_PIECES for s in shards for _ in range(GATHER_PIECES)])
    middles = [(float(sizes[item] / sizes[-1]), forward(item)) for item in range(n)]

    def finish(ins, outs, sems):
        (x, y, c), chips, copy, own, passed = tools(ins, outs, sems)
        for item in range(n):
            copy(item, 0, (x, y, 1 - c), (x, y, c)).wait_recv()
            for j, chip in enumerate(chips):
                copy(item, 4 + j, (*chip, 1 - c), (x, y, c)).wait_recv()
        for item in range(n):
            local, *sent = own(item)
            for cp in sent + [passed(item, j) for j in range(len(chips))]:
                cp.wait_send()
            local.wait()

    return _Carry(
        shards, [jax.ShapeDtypeStruct((N_DEV * s.shape[0], s.shape[1]), s.dtype) for s in shards],
        [pltpu.SemaphoreType.DMA((n, N_DEV - 1)), pltpu.SemaphoreType.DMA((n, N_DEV - 1)),
         pltpu.SemaphoreType.DMA((n,))], start, finish, middles)


def _scatter_carry(grads):
    n = len(grads)

    def tools(ins, outs, sems):
        send_sems, recv_sems, local_sems = sems
        me = _index(*_place())

        def block(ref, dev):
            r = ref.shape[0] // N_DEV
            return ref.at[pl.ds(dev * r, r), :]

        def copy(w, k, landing):
            to = _peer(k)
            return pltpu.make_async_remote_copy(
                src_ref=block(ins[w], _index(*to)), dst_ref=block(outs[w], landing), send_sem=send_sems.at[w, k - 1],
                recv_sem=recv_sems.at[w, k - 1], device_id=to, device_id_type=MESH)

        def mine(w):
            return pltpu.make_async_copy(block(ins[w], me), block(outs[w], me), local_sems.at[w])

        return me, copy, mine

    def start(ins, outs, sems):
        me, copy, mine = tools(ins, outs, sems)
        for w in range(n):
            mine(w).start()
            for k in range(1, N_DEV):
                copy(w, k, me).start()

    def finish(ins, outs, sems):
        _, copy, mine = tools(ins, outs, sems)
        for w in range(n):
            for k in range(1, N_DEV):
                copy(w, k, _index(*_peer(k))).wait()
            mine(w).wait()

    return _Carry(
        grads, [jax.ShapeDtypeStruct(g.shape, g.dtype) for g in grads],
        [pltpu.SemaphoreType.DMA((n, N_DEV - 1)), pltpu.SemaphoreType.DMA((n, N_DEV - 1)),
         pltpu.SemaphoreType.DMA((n,))], start, finish)


def _small_carry(small):
    srows = small.shape[0]

    def tools(ins, outs, sems):
        send_sems, recv_sems, local_sem = sems
        me = _index(*_place())

        def slot(dev):
            return outs[0].at[pl.ds(dev * srows, srows), :]

        def copy(k, landing):
            return pltpu.make_async_remote_copy(
                src_ref=ins[0], dst_ref=slot(landing), send_sem=send_sems.at[k - 1], recv_sem=recv_sems.at[k - 1],
                device_id=_peer(k), device_id_type=MESH)

        return me, copy, pltpu.make_async_copy(ins[0], slot(me), local_sem)

    def start(ins, outs, sems):
        me, copy, mine = tools(ins, outs, sems)
        mine.start()
        for k in range(1, N_DEV):
            copy(k, me).start()

    def finish(ins, outs, sems):
        _, copy, mine = tools(ins, outs, sems)
        for k in range(1, N_DEV):
            copy(k, _index(*_peer(k))).wait()
        mine.wait()

    return _Carry([small], [jax.ShapeDtypeStruct((N_DEV * srows, LANES), small.dtype)],
                  [pltpu.SemaphoreType.DMA((N_DEV - 1,)), pltpu.SemaphoreType.DMA((N_DEV - 1,)),
                   pltpu.SemaphoreType.DMA], start, finish)


def _exchange(carry, name):
    ci = len(carry.inputs)
    co = len(carry.out_shape)

    def body(*refs):
        parts = refs[:ci], refs[ci:ci + co], refs[ci + co:]
        carry.start(*parts)
        for _, fn in carry.middles:
            fn(*parts)
        carry.finish(*parts)

    return list(pl.pallas_call(body, name=name, in_specs=[ANY] * ci, out_specs=[ANY] * co, out_shape=carry.out_shape,
                               scratch_shapes=carry.scratch)(*carry.inputs))


def _adamw_math(w, g, m, v):
    m = ADAM_B1 * m + (1.0 - ADAM_B1) * g
    v = ADAM_B2 * v + (1.0 - ADAM_B2) * (g * g)
    m_hat = m / (1.0 - ADAM_B1 ** ADAM_STEP)
    v_hat = v / (1.0 - ADAM_B2 ** ADAM_STEP)
    return -ADAM_LR * (m_hat / (jnp.sqrt(v_hat) + ADAM_EPS) + ADAM_WD * w), m, v


def _sum_adamw(got, w, m, v, transposed, name):
    parts = list(got) if isinstance(got, (list, tuple)) else [got]
    r = parts[0].shape[0] // N_DEV
    cols = sum(part.shape[1] for part in parts)
    if transposed:
        (only,) = parts
        tile = cols if cols <= 512 else 256
        got_specs = [pl.BlockSpec((N_DEV, r, tile), lambda i: (0, 0, i))]
        spec, steps = pl.BlockSpec((tile, r), lambda i: (i, 0)), cols // tile
    else:
        tile = r if r <= 256 else r // 2
        got_specs = [pl.BlockSpec((N_DEV, tile, part.shape[1]), lambda i: (0, i, 0)) for part in parts]
        spec, steps = pl.BlockSpec((tile, cols), lambda i: (i, 0)), r // tile
    n = len(parts)

    def body(*refs):
        w_ref, m_ref, v_ref, g_ref, d_ref, m2_ref, v2_ref = refs[n:]
        sums = []
        for got_ref in refs[:n]:
            acc = got_ref[0].astype(F32)
            for dev in range(1, N_DEV):
                acc = acc + got_ref[dev].astype(F32)
            sums.append(acc)
        g = sums[0].T if transposed else (sums[0] if n == 1 else jnp.concatenate(sums, axis=1))
        g_ref[...] = g
        d_ref[...], m2_ref[...], v2_ref[...] = _adamw_math(w_ref[...], g, m_ref[...], v_ref[...])

    return pl.pallas_call(
        body, name=name, grid=(steps,), in_specs=got_specs + [spec, spec, spec], out_specs=[spec] * 4,
        out_shape=[jax.ShapeDtypeStruct(w.shape, F32)] * 4, compiler_params=_params("parallel"),
    )(*[part.reshape(N_DEV, r, part.shape[1]) for part in parts], w, m, v)


def _small_update(early, late, w, m, v):
    rows = w.shape[0]

    def body(early_ref, late_ref, w_ref, m_ref, v_ref, g_ref, d_ref, m2_ref, v2_ref):
        sums = []
        for ref in (early_ref, late_ref):
            acc = ref[0]
            for dev in range(1, N_DEV):
                acc = acc + ref[dev]
            sums.append(acc)
        g = jnp.concatenate(sums, axis=0)
        g_ref[...] = g
        d_ref[...], m2_ref[...], v2_ref[...] = _adamw_math(w_ref[...], g, m_ref[...], v_ref[...])

    return pl.pallas_call(
        body, name="small_update", out_shape=[jax.ShapeDtypeStruct((rows, LANES), F32)] * 4,
        compiler_params=pltpu.CompilerParams(vmem_limit_bytes=VMEM_LIMIT),
    )(early.reshape(N_DEV, -1, LANES), late.reshape(N_DEV, -1, LANES), w, m, v)


SMALL = (("pool_w_mix", 512), ("mix_norm", 8), ("ffn2_norm", 8), ("final_norm", 8), ("pool_scale", 8), ("sinks", 8),
         ("loss", 8), ("ffn1_norm", 8))
EARLY, LATE = SMALL[:-1], SMALL[-1:]


def _pack_small(parts, layout=SMALL):
    out = []
    for name, rows in layout:
        flat = parts[name].astype(F32).reshape(-1)
        out.append(jnp.pad(flat, (0, rows * LANES - flat.shape[0])).reshape(rows, LANES))
    return jnp.concatenate(out, axis=0)


def _unpack_small(packed, shapes):
    out, row = {}, 0
    for name, rows in SMALL:
        shape = shapes[name]
        size = int(np.prod(shape)) if shape else 1
        out[name] = packed[row:row + rows].reshape(-1)[:size].reshape(shape)
        row += rows
    return out


def kernel(x, ffn1_norm, ffn1_w_up, ffn1_w_down, mix_norm, w_in, sinks, w_attn_up, pool_w_mix, pool_scale, w_pool_up, w_out, ffn2_norm, ffn2_w_up, ffn2_w_down, final_norm, loss_target, m_ffn1_norm, m_ffn1_w_up, m_ffn1_w_down, m_mix_norm, m_w_in, m_sinks, m_w_attn_up, m_pool_w_mix, m_pool_scale, m_w_pool_up, m_w_out, m_ffn2_norm, m_ffn2_w_up, m_ffn2_w_down, m_final_norm, v_ffn1_norm, v_ffn1_w_up, v_ffn1_w_down, v_mix_norm, v_w_in, v_sinks, v_w_attn_up, v_pool_w_mix, v_pool_scale, v_w_pool_up, v_w_out, v_ffn2_norm, v_ffn2_w_up, v_ffn2_w_down, v_final_norm):
    args = dict(locals())
    weight_names = ("ffn1_norm", "ffn1_w_up", "ffn1_w_down", "mix_norm", "w_in", "sinks", "w_attn_up", "pool_w_mix",
                    "pool_scale", "w_pool_up", "w_out", "ffn2_norm", "ffn2_w_up", "ffn2_w_down", "final_norm")

    shard = {k: (args[p][0].T if tr else args[p][0]).astype(BF16) for k, p, tr in BIG}
    big = {"wup1_t": _exchange(_gather_carry([shard["wup1_t"]]), "gather_up1")[0]}

    def gathering(keys):
        return _gather_carry([shard[k] for k in keys])

    xs, target = x[0], loss_target[0]
    t = xs.shape[0]
    tm_f, tm_b, tk = min(512, t), min(512, t), min(2048, t)
    g1, gm, g2, gf = ffn1_norm, mix_norm, ffn2_norm, final_norm.reshape(1, D_MODEL)
    dist = _attn_dist()
    sink_v = sinks.reshape(N_Q_HEADS)
    wmix_b = pool_w_mix[0].astype(BF16)

    (n1, ab1, act1), (big["wdown1"], big["win_t"]) = _ffn_up(xs, g1, big["wup1_t"], tm_f, gathering(["wdown1", "win_t"]))
    (h1,), (big["wattn"], big["wpool_t"], big["wout"]) = _ffn_down(xs, act1, big["wdown1"], tm_f,
                                                                   gathering(["wattn", "wpool_t", "wout"]))
    (u, q, kv, z, gate), (big["wup2_t"],) = _mix_in_fwd(h1, gm, big["win_t"], tm_f, gathering(["wup2_t"]))
    attn = _attn_fwd(q, kv, dist, sink_v)
    (h2, a, p, merged, ms, pooled), (big["wdown2"],) = _mix_out_fwd(
        attn, z, gate, h1, big["wattn"], wmix_b, pool_scale, big["wpool_t"], big["wout"], tm_b, gathering(["wdown2"]))
    ab2, n2, act2, loss_lanes, dh3, dhb3, dgf = _ffn_loss(h2, g2, big["wup2_t"], big["wdown2"], gf, target, tm_f)

    got = {}
    (gw_down2,), _ = _wgrad(act2, dhb3, 0.5, D_FF // 2, tk, "wgrad_down2")
    (dab2,), (got["wdown2"],) = _ffn_bwd_hidden(dhb3, ab2, big["wdown2"], tm_f, _scatter_carry([gw_down2]))
    (dh2, dg2), _ = _ffn_bwd_input(dab2, dh3, h2, g2, big["wup2_t"], tm_f)
    (gw_up2,), _ = _wgrad(dab2, n2, 1.0, D_FF // 2, tk, "wgrad_up2")
    dhb2, da_b, dp_b, dattn, dgate, dpooled, dwmix, dscale = _mix_out_bwd(
        dh2, gate, a, p, pooled, big["wattn"], wmix_b, pool_scale, big["wpool_t"], big["wout"], tm_b)
    (gw_out,), _ = _wgrad(merged, dhb2, 1.0, D_MODEL, tk, "wgrad_out")
    (gw_attn,), _ = _wgrad(attn, da_b, 1.0, D_MODEL, tk, "wgrad_attn")
    (gw_pool,), _ = _wgrad(dp_b, ms, 1.0, D_MODEL, tk, "wgrad_pool")
    (dq, dkv_own, dkv_prev, dsinks), (got["wup2_t"],) = _attn_bwd(q, kv, dattn, dist, sink_v, _scatter_carry([gw_up2]))
    (dproj, dh1, dhb1, dgm), (got["wout"], got["wattn"], got["wpool_t"]) = _mix_in_bwd(
        dq, dkv_own, dkv_prev, dpooled, dgate, h1, gm, big["win_t"], dh2, tm_b,
        _scatter_carry([gw_out, gw_attn, gw_pool]))
    (gw_down1,), _ = _wgrad(act1, dhb1, 0.5, D_FF // 2, tk, "wgrad_down1")
    (gw_in,), (got["wdown1"],) = _wgrad(dproj, u, 1.0, IN_WIDTH // 3, tk, "wgrad_in", _scatter_carry([gw_down1]))
    (dab1,), (got["win_t"],) = _ffn_bwd_hidden(dhb1, ab1, big["wdown1"], tm_f, _scatter_carry([gw_in]))
    small_parts = {"pool_w_mix": dwmix, "mix_norm": dgm, "ffn2_norm": dg2, "final_norm": dgf, "pool_scale": dscale,
                   "sinks": dsinks[:, :N_Q_HEADS], "loss": loss_lanes[:, :1]}
    (gw_up1,), (small_early,) = _wgrad(dab1, n1, 1.0, D_FF // 2, tk, "wgrad_up1",
                                       _small_carry(_pack_small(small_parts, EARLY)))
    (dx, dg1), (got["wup1_t"],) = _ffn_bwd_input(dab1, dh1, xs, g1, big["wup1_t"], tm_f, _scatter_carry([gw_up1]))
    (small_late,) = _exchange(_small_carry(_pack_small({"ffn1_norm": dg1}, LATE)), "gather_small")

    grad, delta, new_m, new_v = {}, {}, {}, {}
    for k, p, tr in BIG:
        outside = tr and args[p].shape[-1] % LANES != 0
        turn = (lambda a: a.T) if outside else (lambda a: a)
        res = _sum_adamw(got[k], turn(args[p][0]), turn(args["m_" + p][0]), turn(args["v_" + p][0]),
                         tr and not outside, "adamw_" + k)
        grad[p], delta[p], new_m[p], new_v[p] = (turn(a)[None] for a in res)

    shapes = {name: args[name].shape for name, _ in SMALL if name != "loss"}
    shapes["loss"] = ()
    packed = {pre: _pack_small({**{name: args[pre + name] for name, _ in SMALL if name != "loss"},
                                "loss": jnp.zeros((), F32)}) for pre in ("", "m_", "v_")}
    g_s, d_s, m_s, v_s = _small_update(small_early, small_late, packed[""], packed["m_"], packed["v_"])
    g_small, d_small, m_small, v_small = (_unpack_small(a, shapes) for a in (g_s, d_s, m_s, v_s))
    for name, _ in SMALL:
        if name != "loss":
            grad[name], delta[name], new_m[name], new_v[name] = (
                g_small[name], d_small[name], m_small[name], v_small[name])

    return (g_small["loss"], dx[None], *[grad[n] for n in weight_names], *[delta[n] for n in weight_names],
            *[new_m[n] for n in weight_names], *[new_v[n] for n in weight_names])
```

```python
import jax
import jax.numpy as jnp
import numpy as np
from jax import lax
from jax.experimental import pallas as pl
from jax.experimental.pallas import tpu as pltpu

F32 = jnp.float32
BF16 = jnp.bfloat16

D_MODEL = 1024
D_FF = 2816
N_Q_HEADS = 16
N_KV_HEADS = 2
Q_PER_KV = N_Q_HEADS // N_KV_HEADS
HEAD_DIM = 64
BLOCK = 128
ATTN_WIDTH = N_Q_HEADS * HEAD_DIM
KV_WIDTH = N_KV_HEADS * HEAD_DIM
POOL_WINDOWS = (2, 4, 8, 16)
POOL_GROUP = 128
POOL_WIDTH = 512
HALO = 16
IN_WIDTH = ATTN_WIDTH + 2 * KV_WIDTH + POOL_WIDTH + 2 * D_MODEL
OFF_KV = ATTN_WIDTH
OFF_Z = ATTN_WIDTH + 2 * KV_WIDTH
OFF_GATE = OFF_Z + POOL_WIDTH
NORM_EPS = 1e-6
ADAM_LR = 0.001
ADAM_B1 = 0.9
ADAM_B2 = 0.999
ADAM_EPS = 1e-08
ADAM_WD = 0.01
ADAM_STEP = 10

N_DEV = 8
LANES = 128
FF_CHUNK = 256
SLAB = 32
GATHER_PIECES = 2
VMEM_LIMIT = 56 * 1024 * 1024
MESH = pl.DeviceIdType.MESH


def _nn(a, b):
    return jnp.dot(a, b, preferred_element_type=F32)


def _nt(a, b):
    return lax.dot_general(a, b, (((1,), (1,)), ((), ())), preferred_element_type=F32)


def _tn(a, b):
    return lax.dot_general(a, b, (((0,), (0,)), ((), ())), preferred_element_type=F32)


def _params(*sem):
    return pltpu.CompilerParams(dimension_semantics=sem, vmem_limit_bytes=VMEM_LIMIT)


def _resident(shape):
    return pl.BlockSpec(shape, lambda *_: (0,) * len(shape), pipeline_mode=pl.Buffered(1))


def _rows(tm, cols):
    return pl.BlockSpec((tm, cols), lambda i: (i, 0))


class _Carry:
    def __init__(self, inputs, out_shape, scratch, start, finish, middles=()):
        self.inputs, self.out_shape, self.scratch = list(inputs), list(out_shape), list(scratch)
        self.start, self.finish, self.middles = start, finish, list(middles)


def _launch(body, args, carry=None, *, name, grid, in_specs, out_specs, out_shape, scratch_shapes=(), semantics):
    in_specs, out_specs, out_shape, scratch_shapes = list(in_specs), list(out_specs), list(out_shape), list(scratch_shapes)
    if carry is None:
        res = pl.pallas_call(body, name=name, grid=grid, in_specs=in_specs, out_specs=out_specs, out_shape=out_shape,
                             scratch_shapes=scratch_shapes, compiler_params=_params(*semantics))(*args)
        return list(res), []
    ni, no, ns = len(in_specs), len(out_specs), len(scratch_shapes)
    ci, co = len(carry.inputs), len(carry.out_shape)
    total = int(np.prod(grid))

    def full(*refs):
        own_in, c_in = refs[:ni], refs[ni:ni + ci]
        own_out, c_out = refs[ni + ci:ni + ci + no], refs[ni + ci + no:ni + ci + no + co]
        own_scr, c_sem = refs[ni + ci + no + co:ni + ci + no + co + ns], refs[ni + ci + no + co + ns:]
        step = 0
        for axis, size in enumerate(grid):
            step = step * size + pl.program_id(axis)
        pl.when(step == 0)(lambda: carry.start(c_in, c_out, c_sem))
        for fraction, fn in carry.middles:
            at = min(total - 1, int(fraction * total) + 1)
            pl.when(step == at)(lambda fn=fn: fn(c_in, c_out, c_sem))
        body(*own_in, *own_out, *own_scr)
        pl.when(step == total - 1)(lambda: carry.finish(c_in, c_out, c_sem))

    res = pl.pallas_call(
        full, name=name, grid=grid, in_specs=in_specs + [ANY] * ci, out_specs=out_specs + [ANY] * co,
        out_shape=out_shape + carry.out_shape, scratch_shapes=scratch_shapes + carry.scratch,
        compiler_params=_params(*(["arbitrary"] * len(grid))),
    )(*args, *carry.inputs)
    return list(res[:no]), list(res[no:])


def _rms_fwd(xv, g):
    r = lax.rsqrt(jnp.mean(xv * xv, axis=-1, keepdims=True) + NORM_EPS)
    return xv * r, r


def _rms_bwd(dn, xh, r, g):
    dxh = dn * g
    dx = r * (dxh - xh * jnp.mean(dxh * xh, axis=-1, keepdims=True))
    return dx, jnp.sum(dn * xh, axis=0, keepdims=True)


def _ffn_loss(x, g, wup_t, wdown, gf, target, tm):
    t, d = x.shape
    f = wdown.shape[0]

    def body(x_ref, g_ref, wup_ref, wdn_ref, gf_ref, tgt_ref, ab_ref, n_ref, act_ref, loss_ref, dh_ref, dhb_ref, dg_ref):
        xv = x_ref[...]
        xh, _ = _rms_fwd(xv, g_ref[...])
        n = (xh * g_ref[...]).astype(BF16)
        n_ref[...] = n
        for c in range(f // FF_CHUNK):
            lo, hi = c * FF_CHUNK, (c + 1) * FF_CHUNK
            a = _nt(n, wup_ref[lo:hi, :])
            b = _nt(n, wup_ref[f + lo:f + hi, :])
            ab_ref[:, lo:hi] = a.astype(BF16)
            ab_ref[:, f + lo:f + hi] = b.astype(BF16)
            act_ref[:, lo:hi] = (a * jax.nn.sigmoid(a) * b).astype(BF16)
        h = xv + 0.5 * _nn(act_ref[...], wdn_ref[...])
        yh, r = _rms_fwd(h, gf_ref[...])
        err = yh * gf_ref[...] - tgt_ref[...]
        part = 0.5 * jnp.sum(jnp.mean(err * err, axis=-1, keepdims=True), axis=0, keepdims=True)
        dh, dg = _rms_bwd(err * (1.0 / d), yh, r, gf_ref[...])
        dh_ref[...] = dh
        dhb_ref[...] = dh.astype(BF16)

        @pl.when(pl.program_id(0) == 0)
        def _():
            dg_ref[...] = jnp.zeros_like(dg_ref)
            loss_ref[...] = jnp.zeros_like(loss_ref)

        dg_ref[...] += dg
        loss_ref[...] += jnp.broadcast_to(part, loss_ref.shape)

    return pl.pallas_call(
        body, name="ffn_loss", grid=(t // tm,),
        in_specs=[_rows(tm, d), _resident((1, d)), _resident((2 * f, d)), _resident((f, d)), _resident((1, d)),
                  _rows(tm, d)],
        out_specs=[_rows(tm, 2 * f), _rows(tm, d), _rows(tm, f), pl.BlockSpec((1, LANES), lambda i: (0, 0)),
                   _rows(tm, d), _rows(tm, d), pl.BlockSpec((1, d), lambda i: (0, 0))],
        out_shape=[jax.ShapeDtypeStruct((t, 2 * f), BF16), jax.ShapeDtypeStruct((t, d), BF16),
                   jax.ShapeDtypeStruct((t, f), BF16), jax.ShapeDtypeStruct((1, LANES), F32),
                   jax.ShapeDtypeStruct((t, d), F32), jax.ShapeDtypeStruct((t, d), BF16),
                   jax.ShapeDtypeStruct((1, d), F32)],
        compiler_params=_params("arbitrary"),
    )(x, g, wup_t, wdown, gf, target)


def _ffn_up(x, g, wup_t, tm, carry=None):
    t, d = x.shape
    f = wup_t.shape[0] // 2

    def body(x_ref, g_ref, wup_ref, n_ref, ab_ref, act_ref):
        xh, _ = _rms_fwd(x_ref[...], g_ref[...])
        n = (xh * g_ref[...]).astype(BF16)
        n_ref[...] = n
        for c in range(f // FF_CHUNK):
            lo, hi = c * FF_CHUNK, (c + 1) * FF_CHUNK
            a = _nt(n, wup_ref[lo:hi, :])
            b = _nt(n, wup_ref[f + lo:f + hi, :])
            ab_ref[:, lo:hi] = a.astype(BF16)
            ab_ref[:, f + lo:f + hi] = b.astype(BF16)
            act_ref[:, lo:hi] = (a * jax.nn.sigmoid(a) * b).astype(BF16)

    return _launch(
        body, (x, g, wup_t), carry, name="ffn_up", grid=(t // tm,),
        in_specs=[_rows(tm, d), _resident((1, d)), _resident((2 * f, d))],
        out_specs=[_rows(tm, d), _rows(tm, 2 * f), _rows(tm, f)],
        out_shape=[jax.ShapeDtypeStruct((t, d), BF16), jax.ShapeDtypeStruct((t, 2 * f), BF16),
                   jax.ShapeDtypeStruct((t, f), BF16)],
        semantics=("parallel",))


def _ffn_down(x, act, wdown, tm, carry=None):
    t, d = x.shape
    f = wdown.shape[0]

    def body(x_ref, act_ref, wdn_ref, h_ref):
        h_ref[...] = x_ref[...] + 0.5 * _nn(act_ref[...], wdn_ref[...])

    return _launch(
        body, (x, act, wdown), carry, name="ffn_down", grid=(t // tm,),
        in_specs=[_rows(tm, d), _rows(tm, f), _resident((f, d))], out_specs=[_rows(tm, d)],
        out_shape=[jax.ShapeDtypeStruct((t, d), F32)], semantics=("parallel",))


def _ffn_bwd_hidden(dhb, ab, wdown, tm, carry=None):
    t, d = dhb.shape
    f = wdown.shape[0]

    def body(dh_ref, ab_ref, wdn_ref, dab_ref, dact_ref):
        half = dh_ref[...] * 0.5
        for c in range(f // FF_CHUNK):
            lo, hi = c * FF_CHUNK, (c + 1) * FF_CHUNK
            dact_ref[...] = _nt(half, wdn_ref[lo:hi, :])

            def slab(i, carry_):
                rows = pl.ds(pl.multiple_of(i * SLAB, SLAB), SLAB)
                a = ab_ref[rows, lo:hi].astype(F32)
                b = ab_ref[rows, f + lo:f + hi].astype(F32)
                s = jax.nn.sigmoid(a)
                ds_ = dact_ref[rows, :] * s
                dab_ref[rows, lo:hi] = (ds_ * b * (1.0 + a * (1.0 - s))).astype(BF16)
                dab_ref[rows, f + lo:f + hi] = (ds_ * a).astype(BF16)
                return carry_

            lax.fori_loop(0, tm // SLAB, slab, 0, unroll=True)

    return _launch(
        body, (dhb, ab, wdown), carry, name="ffn_bwd_hidden", grid=(t // tm,),
        in_specs=[_rows(tm, d), _rows(tm, 2 * f), _resident((f, d))], out_specs=[_rows(tm, 2 * f)],
        out_shape=[jax.ShapeDtypeStruct((t, 2 * f), BF16)],
        scratch_shapes=[pltpu.VMEM((tm, FF_CHUNK), F32)], semantics=("parallel",))


def _ffn_bwd_input(dab, dh, x, g, wup_t, tm, carry=None):
    t, d = x.shape
    f2 = wup_t.shape[0]

    def body(dab_ref, dh_ref, x_ref, g_ref, wup_ref, dx_ref, dg_ref):
        dn = _nn(dab_ref[...], wup_ref[...])
        xh, r = _rms_fwd(x_ref[...], g_ref[...])
        dx, dg = _rms_bwd(dn, xh, r, g_ref[...])
        dx_ref[...] = dh_ref[...] + dx

        @pl.when(pl.program_id(0) == 0)
        def _():
            dg_ref[...] = jnp.zeros_like(dg_ref)

        dg_ref[...] += dg

    return _launch(
        body, (dab, dh, x, g, wup_t), carry, name="ffn_bwd_input", grid=(t // tm,),
        in_specs=[_rows(tm, f2), _rows(tm, d), _rows(tm, d), _resident((1, d)), _resident((f2, d))],
        out_specs=[_rows(tm, d), pl.BlockSpec((1, d), lambda i: (0, 0))],
        out_shape=[jax.ShapeDtypeStruct((t, d), F32), jax.ShapeDtypeStruct((1, d), F32)],
        semantics=("arbitrary",))


def _wgrad(lhs, rhs, scale, bm, tk, name, carry=None):
    t, m = lhs.shape
    n = rhs.shape[1]
    steps = t // tk
    chunk = bm if bm <= 2048 else bm // 2

    def body(l_ref, r_ref, o_ref, acc_ref):
        @pl.when(pl.program_id(1) == 0)
        def _():
            acc_ref[...] = jnp.zeros_like(acc_ref)

        for lo in range(0, bm, chunk):
            acc_ref[lo:lo + chunk, :] += _tn(l_ref[:, lo:lo + chunk], r_ref[...])

        @pl.when(pl.program_id(1) == steps - 1)
        def _():
            o_ref[...] = (scale * acc_ref[...]).astype(o_ref.dtype)

    return _launch(
        body, (lhs, rhs), carry, name=name, grid=(m // bm, steps),
        in_specs=[pl.BlockSpec((tk, bm), lambda i, k: (k, i)), pl.BlockSpec((tk, n), lambda i, k: (k, 0))],
        out_specs=[pl.BlockSpec((bm, n), lambda i, k: (i, 0))],
        out_shape=[jax.ShapeDtypeStruct((m, n), WIRE)],
        scratch_shapes=[pltpu.VMEM((bm, n), F32)], semantics=("parallel", "arbitrary"))


def _mix_in_fwd(h, g, win_t, tm, carry=None):
    t, d = h.shape

    def body(h_ref, g_ref, w_ref, u_ref, q_ref, kv_ref, z_ref, gate_ref):
        xh, _ = _rms_fwd(h_ref[...], g_ref[...])
        u = (xh * g_ref[...]).astype(BF16)
        u_ref[...] = u
        q_ref[...] = _nt(u, w_ref[0:OFF_KV, :]).astype(BF16)
        kv_ref[...] = _nt(u, w_ref[OFF_KV:OFF_Z, :]).astype(BF16)
        z_ref[...] = _nt(u, w_ref[OFF_Z:OFF_GATE, :])
        gate_ref[...] = _nt(u, w_ref[OFF_GATE:IN_WIDTH, :]).astype(BF16)

    return _launch(
        body, (h, g, win_t), carry, name="mix_in_fwd", grid=(t // tm,),
        in_specs=[_rows(tm, d), _resident((1, d)), _resident((IN_WIDTH, d))],
        out_specs=[_rows(tm, d), _rows(tm, ATTN_WIDTH), _rows(tm, 2 * KV_WIDTH), _rows(tm, POOL_WIDTH),
                   _rows(tm, 2 * D_MODEL)],
        out_shape=[jax.ShapeDtypeStruct((t, d), BF16), jax.ShapeDtypeStruct((t, ATTN_WIDTH), BF16),
                   jax.ShapeDtypeStruct((t, 2 * KV_WIDTH), BF16), jax.ShapeDtypeStruct((t, POOL_WIDTH), F32),
                   jax.ShapeDtypeStruct((t, 2 * D_MODEL), BF16)],
        semantics=("parallel",))


ALIBI_SLOPES = tuple(float(s) for s in (2.0 ** (-8.0 * np.arange(1, N_Q_HEADS + 1, dtype=np.float32) / N_Q_HEADS)))


def _attn_dist():
    return jnp.asarray(((np.arange(BLOCK)[:, None] - np.arange(BLOCK)[None, :]) % BLOCK).astype(np.float32))


def _own_block():
    shape = (BLOCK, BLOCK)
    return lax.broadcasted_iota(jnp.int32, shape, 1) <= lax.broadcasted_iota(jnp.int32, shape, 0)


def _fold(band2, own):
    return jnp.where(own, band2[:, BLOCK:], band2[:, :BLOCK])


def _unfold(x, own):
    zero = jnp.zeros_like(x)
    return jnp.concatenate([jnp.where(own, zero, x), jnp.where(own, x, zero)], axis=1)


def _low_half(shape):
    return lax.broadcasted_iota(jnp.int32, shape, len(shape) - 1) < HEAD_DIM


def _both_halves(band, kv_head):
    low = _low_half(band.shape)
    swapped = pltpu.roll(band, HEAD_DIM, 1)
    return jnp.where(low, band, swapped) if kv_head == 0 else jnp.where(low, swapped, band)


def _pair_rows(ref, rows, pair, scale=None):
    v = ref[rows, LANES * pair:LANES * (pair + 1)]
    if scale is not None:
        v = v * scale
    low, zero = _low_half(v.shape), jnp.zeros_like(v)
    return jnp.concatenate([jnp.where(low, v, zero), jnp.where(low, zero, v)], axis=0)


def _per_head(even, odd):
    return jnp.where(lax.broadcasted_iota(jnp.int32, (2 * BLOCK, 1), 0) < BLOCK, even, odd)


def _twice(x):
    return jnp.concatenate([x, x], axis=0)


def _pair_scores(q_ref, rows, kk, dist2, pair, first, own2):
    s2 = _nt(_pair_rows(q_ref, rows, pair, HEAD_DIM ** -0.5), kk)
    before = jnp.where(first, -jnp.inf, s2[:, :BLOCK])
    slopes = _per_head(ALIBI_SLOPES[2 * pair], ALIBI_SLOPES[2 * pair + 1])
    return jnp.where(own2, s2[:, BLOCK:], before) - slopes * dist2


def _own_half(ref, rows, head):
    v = ref[rows, LANES * (head // 2):LANES * (head // 2 + 1)]
    low = _low_half(v.shape)
    return jnp.where(low if head % 2 == 0 else jnp.logical_not(low), v, jnp.zeros_like(v))


def _head_scores(q_ref, rows, kk, dist, head, first, own):
    s2 = _nt(_own_half(q_ref, rows, head) * HEAD_DIM ** -0.5, kk)
    before = jnp.where(first, -jnp.inf, s2[:, :BLOCK])
    return jnp.where(own, s2[:, BLOCK:], before) - ALIBI_SLOPES[head] * dist


def _heads_of(stack):
    return jnp.where(_low_half((BLOCK, LANES)), stack[:BLOCK], stack[BLOCK:])


def _softmax_sink(s, sink):
    m = jnp.maximum(jnp.max(s, axis=-1, keepdims=True), sink)
    p = jnp.exp(s - m)
    psink = jnp.exp(sink - m)
    inv = 1.0 / (jnp.sum(p, axis=-1, keepdims=True) + psink)
    return p * inv, psink * inv


def _bands(kvc_ref, kvp_ref, sub):
    own = slice(sub * BLOCK, (sub + 1) * BLOCK)
    before = kvp_ref[...] if sub == 0 else kvc_ref[(sub - 1) * BLOCK:sub * BLOCK, :]
    kband = jnp.concatenate([before[:, 0:LANES], kvc_ref[own, 0:LANES]], axis=0)
    vband = jnp.concatenate([before[:, LANES:2 * LANES], kvc_ref[own, LANES:2 * LANES]], axis=0)
    return ([_both_halves(kband, hk) for hk in range(N_KV_HEADS)],
            [_both_halves(vband, hk) for hk in range(N_KV_HEADS)])


SMEM = pl.BlockSpec(memory_space=pltpu.SMEM)
HEADS = range(N_Q_HEADS)
PAIRS = range(N_Q_HEADS // 2)
PAIRS_PER_KV = Q_PER_KV // 2
def _sub_rows(sub):
    return slice(sub * BLOCK, (sub + 1) * BLOCK)


def _block_before(step):
    per = step // BLOCK
    return pl.BlockSpec((BLOCK, 2 * KV_WIDTH), lambda i: (jnp.maximum(i * per - 1, 0), 0))


def _attn_fwd(q, kv, dist, sinks):
    t = q.shape[0]
    subs = range(2)

    def body(q_ref, kvc_ref, kvp_ref, dist_ref, sink_ref, o_ref, s_scr, p_scr):
        own2 = _twice(_own_block())
        dist2 = _twice(dist_ref[...])
        bands = [_bands(kvc_ref, kvp_ref, sub) for sub in subs]
        for sub in subs:
            first = jnp.logical_and(pl.program_id(0) == 0, sub == 0)
            for pair in PAIRS:
                s_scr[sub, pair] = _pair_scores(q_ref, _sub_rows(sub), bands[sub][0][pair // PAIRS_PER_KV], dist2, pair,
                                                first, own2)
        for sub in subs:
            for pair in PAIRS:
                probs, _ = _softmax_sink(s_scr[sub, pair], _per_head(sink_ref[2 * pair], sink_ref[2 * pair + 1]))
                p_scr[sub, pair] = _unfold(probs.astype(BF16), own2)
        for sub in subs:
            for pair in PAIRS:
                out = _nn(p_scr[sub, pair], bands[sub][1][pair // PAIRS_PER_KV])
                o_ref[_sub_rows(sub), LANES * pair:LANES * (pair + 1)] = _heads_of(out).astype(BF16)

    step = len(subs) * BLOCK
    return pl.pallas_call(
        body, name="attn_fwd", grid=(t // step,),
        in_specs=[_rows(step, ATTN_WIDTH), _rows(step, 2 * KV_WIDTH), _block_before(step), _resident(dist.shape), SMEM],
        out_specs=_rows(step, ATTN_WIDTH),
        out_shape=jax.ShapeDtypeStruct((t, ATTN_WIDTH), BF16),
        scratch_shapes=[pltpu.VMEM((len(subs), len(PAIRS), 2 * BLOCK, BLOCK), F32),
                        pltpu.VMEM((len(subs), len(PAIRS), 2 * BLOCK, 2 * BLOCK), BF16)],
        compiler_params=_params("parallel"),
    )(q, kv, kv, dist, sinks)


def _pool_counts(tm, width):
    row = pl.program_id(0) * tm + lax.broadcasted_iota(jnp.int32, (tm, 1), 0)
    return jnp.minimum(row + 1, width).astype(F32)


def _trailing_sums(zz, group):
    s = zz
    for k in range(group + 1):
        s = s + pltpu.roll(s, 1 << k, 0)
    return s


def _leading_sums(zz, group):
    rows = zz.shape[0]
    s = zz
    for k in range(group + 1):
        s = s + pltpu.roll(s, rows - (1 << k), 0)
    return s


def _mix_out_fwd(attn, z, gate, h, wattn, wmix, scale, wpool_t, wout, tm, carry=None):
    t, d = h.shape

    def body(attn_ref, z_ref, halo_ref, gate_ref, h_ref, wattn_ref, wmix_ref, scale_ref, wpool_ref, wout_ref,
             h2_ref, a_ref, p_ref, merged_ref, ms_ref, pooled_ref):
        halo = jnp.where(pl.program_id(0) == 0, 0.0, halo_ref[...])
        for gi, width in enumerate(POOL_WINDOWS):
            lo, hi = gi * POOL_GROUP, (gi + 1) * POOL_GROUP
            zg = z_ref[:, lo:hi]
            sums = _trailing_sums(jnp.concatenate([halo[:, lo:hi], zg], axis=0), gi)[HALO:, :]
            pooled = (sums / _pool_counts(tm, width) - zg).astype(BF16)
            pooled_ref[:, lo:hi] = pooled
            ms_ref[:, lo:hi] = (_nn(pooled, wmix_ref[gi]) * scale_ref[:, lo:hi]).astype(BF16)
        p = _nt(ms_ref[...], wpool_ref[...])
        a = _nn(attn_ref[...], wattn_ref[...])
        a_ref[...] = a.astype(BF16)
        p_ref[...] = p.astype(BF16)
        merged = (jax.nn.sigmoid(gate_ref[:, 0:d].astype(F32)) * a
                  + jax.nn.sigmoid(gate_ref[:, d:2 * d].astype(F32)) * p).astype(BF16)
        merged_ref[...] = merged
        h2_ref[...] = h_ref[...] + _nn(merged, wout_ref[...])

    halo_spec = pl.BlockSpec((HALO, POOL_WIDTH), lambda i: (jnp.maximum(i * (tm // HALO) - 1, 0), 0))
    return _launch(
        body, (attn, z, z, gate, h, wattn, wmix, scale, wpool_t, wout), carry, name="mix_out_fwd", grid=(t // tm,),
        in_specs=[_rows(tm, ATTN_WIDTH), _rows(tm, POOL_WIDTH), halo_spec, _rows(tm, 2 * d), _rows(tm, d),
                  _resident(wattn.shape), _resident(wmix.shape), _resident(scale.shape), _resident(wpool_t.shape),
                  _resident(wout.shape)],
        out_specs=[_rows(tm, d), _rows(tm, d), _rows(tm, d), _rows(tm, d), _rows(tm, POOL_WIDTH),
                   _rows(tm, POOL_WIDTH)],
        out_shape=[jax.ShapeDtypeStruct((t, d), F32), jax.ShapeDtypeStruct((t, d), BF16),
                   jax.ShapeDtypeStruct((t, d), BF16), jax.ShapeDtypeStruct((t, d), BF16),
                   jax.ShapeDtypeStruct((t, POOL_WIDTH), BF16), jax.ShapeDtypeStruct((t, POOL_WIDTH), BF16)],
        semantics=("parallel",))


def _mix_out_bwd(dh, gate, a, p, pooled, wattn, wmix, scale, wpool_t, wout, tm):
    t, d = dh.shape

    def body(dh_ref, gate_ref, a_ref, p_ref, pooled_ref, wattn_ref, wmix_ref, scale_ref, wpool_ref, wout_ref,
             dhb_ref, dab_ref, dpb_ref, dattn_ref, dgate_ref, dpooled_ref, dwmix_ref, dscale_ref):
        @pl.when(pl.program_id(0) == 0)
        def _():
            dwmix_ref[...] = jnp.zeros_like(dwmix_ref)
            dscale_ref[...] = jnp.zeros_like(dscale_ref)

        dhb = dh_ref[...].astype(BF16)
        dhb_ref[...] = dhb
        dm = _nt(dhb, wout_ref[...])
        sa = jax.nn.sigmoid(gate_ref[:, 0:d].astype(F32))
        sp = jax.nn.sigmoid(gate_ref[:, d:2 * d].astype(F32))
        da = (dm * sa).astype(BF16)
        dp = (dm * sp).astype(BF16)
        dab_ref[...] = da
        dpb_ref[...] = dp
        dgate_ref[:, 0:d] = (dm * a_ref[...].astype(F32) * (sa * (1.0 - sa))).astype(BF16)
        dgate_ref[:, d:2 * d] = (dm * p_ref[...].astype(F32) * (sp * (1.0 - sp))).astype(BF16)
        dattn_ref[...] = _nt(da, wattn_ref[...]).astype(BF16)
        dms = _nn(dp, wpool_ref[...])
        for gi in range(len(POOL_WINDOWS)):
            lo, hi = gi * POOL_GROUP, (gi + 1) * POOL_GROUP
            pooled_g = pooled_ref[:, lo:hi]
            mixed = _nn(pooled_g, wmix_ref[gi])
            dscale_ref[:, lo:hi] += jnp.sum(dms[:, lo:hi] * mixed, axis=0, keepdims=True)
            dmixed = (dms[:, lo:hi] * scale_ref[:, lo:hi]).astype(BF16)
            dwmix_ref[gi] += _tn(pooled_g, dmixed)
            dpooled_ref[:, lo:hi] = _nt(dmixed, wmix_ref[gi])

    acc = lambda shape: pl.BlockSpec(shape, lambda i: (0,) * len(shape))
    return pl.pallas_call(
        body, name="mix_out_bwd", grid=(t // tm,),
        in_specs=[_rows(tm, d), _rows(tm, 2 * d), _rows(tm, d), _rows(tm, d), _rows(tm, POOL_WIDTH),
                  _resident(wattn.shape), _resident(wmix.shape), _resident(scale.shape), _resident(wpool_t.shape),
                  _resident(wout.shape)],
        out_specs=[_rows(tm, d), _rows(tm, d), _rows(tm, d), _rows(tm, ATTN_WIDTH), _rows(tm, 2 * d),
                   _rows(tm, POOL_WIDTH), acc(wmix.shape), acc((1, POOL_WIDTH))],
        out_shape=[jax.ShapeDtypeStruct((t, d), BF16), jax.ShapeDtypeStruct((t, d), BF16),
                   jax.ShapeDtypeStruct((t, d), BF16), jax.ShapeDtypeStruct((t, ATTN_WIDTH), BF16),
                   jax.ShapeDtypeStruct((t, 2 * d), BF16), jax.ShapeDtypeStruct((t, POOL_WIDTH), F32),
                   jax.ShapeDtypeStruct(wmix.shape, F32), jax.ShapeDtypeStruct((1, POOL_WIDTH), F32)],
        compiler_params=_params("arbitrary"),
    )(dh, gate, a, p, pooled, wattn, wmix, scale, wpool_t, wout)


def _fold_halves(x):
    return x + pltpu.roll(x, HEAD_DIM, 1)


def _attn_bwd(q, kv, dattn, dist, sinks, carry=None):
    t = q.shape[0]
    subs = range(1)

    def body(q_ref, kvc_ref, kvp_ref, do_ref, dist_ref, sink_ref, dq_ref, dkv_own_ref, dkv_prev_ref, dsink_ref,
             s_scr, dp_scr, p_scr, ds_scr):
        @pl.when(pl.program_id(0) == 0)
        def _():
            dsink_ref[...] = jnp.zeros_like(dsink_ref)

        own = _own_block()
        dist_v = dist_ref[...]
        bands = [_bands(kvc_ref, kvp_ref, sub) for sub in subs]
        lane = lax.broadcasted_iota(jnp.int32, (1, LANES), 1)
        for sub in subs:
            first = jnp.logical_and(pl.program_id(0) == 0, sub == 0)
            for head in HEADS:
                hk = head // Q_PER_KV
                s_scr[sub, head] = _head_scores(q_ref, _sub_rows(sub), bands[sub][0][hk], dist_v, head, first, own)
                dp_scr[sub, head] = _fold(_nt(_own_half(do_ref, _sub_rows(sub), head), bands[sub][1][hk]), own)
        dsink = jnp.zeros((1, LANES), F32)
        for sub in subs:
            for head in HEADS:
                probs, psink = _softmax_sink(s_scr[sub, head], sink_ref[head])
                dprobs = dp_scr[sub, head]
                rowdot = jnp.sum(probs * dprobs, axis=-1, keepdims=True)
                p_scr[sub, head] = _unfold(probs.astype(BF16), own)
                ds_scr[sub, head] = _unfold((probs * (dprobs - rowdot)).astype(BF16), own)
                dsink = dsink + jnp.where(lane == head, jnp.sum(-psink * rowdot, axis=0, keepdims=True), 0.0)
        for sub in subs:
            rows = _sub_rows(sub)
            dk_heads, dv_heads = [], []
            for hk in range(N_KV_HEADS):
                dk_t = jnp.zeros((LANES, 2 * BLOCK), F32)
                dv_t = jnp.zeros((LANES, 2 * BLOCK), F32)
                for pair in range(Q_PER_KV // 2):
                    cols = slice(LANES * (hk * PAIRS_PER_KV + pair), LANES * (hk * PAIRS_PER_KV + pair + 1))
                    q_t = (q_ref[rows, cols] * HEAD_DIM ** -0.5).T
                    do_t = do_ref[rows, cols].T
                    dqs = []
                    for head in (hk * Q_PER_KV + 2 * pair, hk * Q_PER_KV + 2 * pair + 1):
                        mine = (lax.broadcasted_iota(jnp.int32, q_t.shape, 0) < HEAD_DIM) == (head % 2 == 0)
                        dv_t = dv_t + _nn(jnp.where(mine, do_t, jnp.zeros_like(do_t)), p_scr[sub, head])
                        dk_t = dk_t + _nn(jnp.where(mine, q_t, jnp.zeros_like(q_t)), ds_scr[sub, head])
                        dqs.append(_nn(ds_scr[sub, head], bands[sub][0][hk]))
                    dq_pair = jnp.where(_low_half(dqs[0].shape), dqs[0], dqs[1])
                    dq_ref[rows, cols] = (dq_pair * HEAD_DIM ** -0.5).astype(BF16)
                dk_heads.append(_fold_halves(dk_t.T))
                dv_heads.append(_fold_halves(dv_t.T))
            low = _low_half(dk_heads[0].shape)
            dkv = jnp.concatenate([jnp.where(low, dk_heads[0], dk_heads[1]), jnp.where(low, dv_heads[0], dv_heads[1])],
                                  axis=1)
            dkv_prev_ref[rows, :] = dkv[0:BLOCK, :]
            dkv_own_ref[rows, :] = dkv[BLOCK:2 * BLOCK, :]
        dsink_ref[...] += dsink

    step = len(subs) * BLOCK
    return _launch(
        body, (q, kv, kv, dattn, dist, sinks), carry, name="attn_bwd", grid=(t // step,),
        in_specs=[_rows(step, ATTN_WIDTH), _rows(step, 2 * KV_WIDTH), _block_before(step), _rows(step, ATTN_WIDTH),
                  _resident(dist.shape), SMEM],
        out_specs=[_rows(step, ATTN_WIDTH), _rows(step, 2 * KV_WIDTH), _rows(step, 2 * KV_WIDTH),
                   pl.BlockSpec((1, LANES), lambda i: (0, 0))],
        out_shape=[jax.ShapeDtypeStruct((t, ATTN_WIDTH), BF16), jax.ShapeDtypeStruct((t, 2 * KV_WIDTH), F32),
                   jax.ShapeDtypeStruct((t, 2 * KV_WIDTH), F32), jax.ShapeDtypeStruct((1, LANES), F32)],
        scratch_shapes=[pltpu.VMEM((len(subs), N_Q_HEADS, BLOCK, BLOCK), F32),
                        pltpu.VMEM((len(subs), N_Q_HEADS, BLOCK, BLOCK), F32),
                        pltpu.VMEM((len(subs), N_Q_HEADS, BLOCK, 2 * BLOCK), BF16),
                        pltpu.VMEM((len(subs), N_Q_HEADS, BLOCK, 2 * BLOCK), BF16)],
        semantics=("arbitrary",))


def _mix_in_bwd(dq, dkv_own, dkv_prev, dpooled, dgate, h, g, win_t, dh_res, tm, carry=None):
    t, d = h.shape
    nt = t // tm

    def body(dq_ref, own_ref, prev_ref, prev_next_ref, dpool_ref, halo_ref, dgate_ref, h_ref, g_ref, w_ref, res_ref,
             dproj_ref, dh_ref, dhb_ref, dg_ref):
        i = pl.program_id(0)
        last = i == nt - 1
        dproj_ref[:, 0:OFF_KV] = dq_ref[...]
        from_next = jnp.where(last, 0.0, prev_next_ref[...])
        if tm > BLOCK:
            from_next = jnp.concatenate([prev_ref[BLOCK:tm, :], from_next], axis=0)
        dproj_ref[:, OFF_KV:OFF_Z] = (own_ref[...] + from_next).astype(BF16)
        halo = jnp.where(last, 0.0, halo_ref[...])
        for gi, width in enumerate(POOL_WINDOWS):
            lo, hi = gi * POOL_GROUP, (gi + 1) * POOL_GROUP
            dpg = dpool_ref[:, lo:hi]
            scaled = jnp.concatenate([dpg / _pool_counts(tm, width), halo[:, lo:hi] / float(width)], axis=0)
            dz = _leading_sums(scaled, gi)[0:tm, :] - dpg
            dproj_ref[:, OFF_Z + lo:OFF_Z + hi] = dz.astype(BF16)
        dproj_ref[:, OFF_GATE:IN_WIDTH] = dgate_ref[...]
        du = _nn(dproj_ref[...], w_ref[...])
        xh, r = _rms_fwd(h_ref[...], g_ref[...])
        dx, dg = _rms_bwd(du, xh, r, g_ref[...])
        dh = res_ref[...] + dx
        dh_ref[...] = dh
        dhb_ref[...] = dh.astype(BF16)

        @pl.when(i == 0)
        def _():
            dg_ref[...] = jnp.zeros_like(dg_ref)

        dg_ref[...] += dg

    per = tm // BLOCK
    next_block = pl.BlockSpec((BLOCK, 2 * KV_WIDTH), lambda i: (jnp.minimum((i + 1) * per, t // BLOCK - 1), 0))
    next_halo = pl.BlockSpec((HALO, POOL_WIDTH), lambda i: (jnp.minimum((i + 1) * (tm // HALO), t // HALO - 1), 0))
    return _launch(
        body, (dq, dkv_own, dkv_prev, dkv_prev, dpooled, dpooled, dgate, h, g, win_t, dh_res), carry,
        name="mix_in_bwd", grid=(nt,),
        in_specs=[_rows(tm, ATTN_WIDTH), _rows(tm, 2 * KV_WIDTH), _rows(tm, 2 * KV_WIDTH), next_block,
                  _rows(tm, POOL_WIDTH), next_halo, _rows(tm, 2 * d), _rows(tm, d), _resident((1, d)),
                  _resident((IN_WIDTH, d)), _rows(tm, d)],
        out_specs=[_rows(tm, IN_WIDTH), _rows(tm, d), _rows(tm, d), pl.BlockSpec((1, d), lambda i: (0, 0))],
        out_shape=[jax.ShapeDtypeStruct((t, IN_WIDTH), BF16), jax.ShapeDtypeStruct((t, d), F32),
                   jax.ShapeDtypeStruct((t, d), BF16), jax.ShapeDtypeStruct((1, d), F32)],
        semantics=("arbitrary",))


BIG = (("wup1_t", "ffn1_w_up", True), ("wdown1", "ffn1_w_down", False), ("win_t", "w_in", True),
       ("wattn", "w_attn_up", False), ("wpool_t", "w_pool_up", True), ("wout", "w_out", False),
       ("wup2_t", "ffn2_w_up", True), ("wdown2", "ffn2_w_down", False))
ANY = pl.BlockSpec(memory_space=pl.ANY)
WIRE = BF16


def _place():
    return lax.axis_index("x"), lax.axis_index("y"), lax.axis_index("c")


def _peer(k):
    x, y, c = _place()
    return x ^ (k >> 2), y ^ ((k >> 1) & 1), c ^ (k & 1)


def _index(px, py, pc):
    return 4 * px + 2 * py + pc


def _gather_carry(shards):
    n = len(shards) * GATHER_PIECES

    def tools(ins, outs, sems):
        send_sems, recv_sems, local_sems = sems
        x, y, c = _place()
        chips = [(1 - x, y), (x, 1 - y), (1 - x, 1 - y)]

        def piece(item):
            w, q = divmod(item, GATHER_PIECES)
            r = ins[w].shape[0]
            return w, r, q * (r // GATHER_PIECES), r // GATHER_PIECES

        def mine(item):
            w, _, first, size = piece(item)
            return ins[w].at[pl.ds(first, size), :]

        def rows(item, px, py, pc):
            w, r, first, size = piece(item)
            return outs[w].at[pl.ds(_index(px, py, pc) * r + first, size), :]

        def copy(item, k, block, to, src=None):
            return pltpu.make_async_remote_copy(
                src_ref=rows(item, *block) if src is None else src, dst_ref=rows(item, *block),
                send_sem=send_sems.at[item, k], recv_sem=recv_sems.at[item, k], device_id=to, device_id_type=MESH)

        def own(item):
            return ([pltpu.make_async_copy(mine(item), rows(item, x, y, c), local_sems.at[item]),
                     copy(item, 0, (x, y, c), (x, y, 1 - c), src=mine(item))]
                    + [copy(item, 1 + j, (x, y, c), (*chip, c), src=mine(item)) for j, chip in enumerate(chips)])

        def passed(item, j):
            return copy(item, 4 + j, (*chips[j], c), (x, y, 1 - c))

        return (x, y, c), chips, copy, own, passed

    def start(ins, outs, sems):
        _, _, _, own, _ = tools(ins, outs, sems)
        for item in range(n):
            for cp in own(item):
                cp.start()

    def forward(item):
        def run(ins, outs, sems):
            (x, y, c), chips, copy, _, passed = tools(ins, outs, sems)
            for j, chip in enumerate(chips):
                copy(item, 1 + j, (*chip, c), (x, y, c)).wait_recv()
                passed(item, j).start()
        return run

    sizes = np.cumsum([s.size / GATHER_PIECES for s in shards for _ in range(GATHER_PIECES)])
    middles = [(float(sizes[item] / sizes[-1]), forward(item)) for item in range(n)]

    def finish(ins, outs, sems):
        (x, y, c), chips, copy, own, passed = tools(ins, outs, sems)
        for item in range(n):
            copy(item, 0, (x, y, 1 - c), (x, y, c)).wait_recv()
            for j, chip in enumerate(chips):
                copy(item, 4 + j, (*chip, 1 - c), (x, y, c)).wait_recv()
        for item in range(n):
            local, *sent = own(item)
            for cp in sent + [passed(item, j) for j in range(len(chips))]:
                cp.wait_send()
            local.wait()

    return _Carry(
        shards, [jax.ShapeDtypeStruct((N_DEV * s.shape[0], s.shape[1]), s.dtype) for s in shards],
        [pltpu.SemaphoreType.DMA((n, N_DEV - 1)), pltpu.SemaphoreType.DMA((n, N_DEV - 1)),
         pltpu.SemaphoreType.DMA((n,))], start, finish, middles)


def _scatter_carry(grads):
    n = len(grads)

    def tools(ins, outs, sems):
        send_sems, recv_sems, local_sems = sems
        me = _index(*_place())

        def block(ref, dev):
            r = ref.shape[0] // N_DEV
            return ref.at[pl.ds(dev * r, r), :]

        def copy(w, k, landing):
            to = _peer(k)
            return pltpu.make_async_remote_copy(
                src_ref=block(ins[w], _index(*to)), dst_ref=block(outs[w], landing), send_sem=send_sems.at[w, k - 1],
                recv_sem=recv_sems.at[w, k - 1], device_id=to, device_id_type=MESH)

        def mine(w):
            return pltpu.make_async_copy(block(ins[w], me), block(outs[w], me), local_sems.at[w])

        return me, copy, mine

    def start(ins, outs, sems):
        me, copy, mine = tools(ins, outs, sems)
        for w in range(n):
            mine(w).start()
            for k in range(1, N_DEV):
                copy(w, k, me).start()

    def finish(ins, outs, sems):
        _, copy, mine = tools(ins, outs, sems)
        for w in range(n):
            for k in range(1, N_DEV):
                copy(w, k, _index(*_peer(k))).wait()
            mine(w).wait()

    return _Carry(
        grads, [jax.ShapeDtypeStruct(g.shape, g.dtype) for g in grads],
        [pltpu.SemaphoreType.DMA((n, N_DEV - 1)), pltpu.SemaphoreType.DMA((n, N_DEV - 1)),
         pltpu.SemaphoreType.DMA((n,))], start, finish)


def _small_carry(small):
    srows = small.shape[0]

    def tools(ins, outs, sems):
        send_sems, recv_sems, local_sem = sems
        me = _index(*_place())

        def slot(dev):
            return outs[0].at[pl.ds(dev * srows, srows), :]

        def copy(k, landing):
            return pltpu.make_async_remote_copy(
                src_ref=ins[0], dst_ref=slot(landing), send_sem=send_sems.at[k - 1], recv_sem=recv_sems.at[k - 1],
                device_id=_peer(k), device_id_type=MESH)

        return me, copy, pltpu.make_async_copy(ins[0], slot(me), local_sem)

    def start(ins, outs, sems):
        me, copy, mine = tools(ins, outs, sems)
        mine.start()
        for k in range(1, N_DEV):
            copy(k, me).start()

    def finish(ins, outs, sems):
        _, copy, mine = tools(ins, outs, sems)
        for k in range(1, N_DEV):
            copy(k, _index(*_peer(k))).wait()
        mine.wait()

    return _Carry([small], [jax.ShapeDtypeStruct((N_DEV * srows, LANES), small.dtype)],
                  [pltpu.SemaphoreType.DMA((N_DEV - 1,)), pltpu.SemaphoreType.DMA((N_DEV - 1,)),
                   pltpu.SemaphoreType.DMA], start, finish)


def _exchange(carry, name):
    ci = len(carry.inputs)
    co = len(carry.out_shape)

    def body(*refs):
        parts = refs[:ci], refs[ci:ci + co], refs[ci + co:]
        carry.start(*parts)
        for _, fn in carry.middles:
            fn(*parts)
        carry.finish(*parts)

    return list(pl.pallas_call(body, name=name, in_specs=[ANY] * ci, out_specs=[ANY] * co, out_shape=carry.out_shape,
                               scratch_shapes=carry.scratch)(*carry.inputs))


def _adamw_math(w, g, m, v):
    m = ADAM_B1 * m + (1.0 - ADAM_B1) * g
    v = ADAM_B2 * v + (1.0 - ADAM_B2) * (g * g)
    m_hat = m / (1.0 - ADAM_B1 ** ADAM_STEP)
    v_hat = v / (1.0 - ADAM_B2 ** ADAM_STEP)
    return -ADAM_LR * (m_hat / (jnp.sqrt(v_hat) + ADAM_EPS) + ADAM_WD * w), m, v


def _sum_adamw(got, w, m, v, transposed, name):
    parts = list(got) if isinstance(got, (list, tuple)) else [got]
    r = parts[0].shape[0] // N_DEV
    cols = sum(part.shape[1] for part in parts)
    if transposed:
        (only,) = parts
        tile = cols if cols <= 512 else 256
        got_specs = [pl.BlockSpec((N_DEV, r, tile), lambda i: (0, 0, i))]
        spec, steps = pl.BlockSpec((tile, r), lambda i: (i, 0)), cols // tile
    else:
        tile = r if r <= 256 else r // 2
        got_specs = [pl.BlockSpec((N_DEV, tile, part.shape[1]), lambda i: (0, i, 0)) for part in parts]
        spec, steps = pl.BlockSpec((tile, cols), lambda i: (i, 0)), r // tile
    n = len(parts)

    def body(*refs):
        w_ref, m_ref, v_ref, g_ref, d_ref, m2_ref, v2_ref = refs[n:]
        sums = []
        for got_ref in refs[:n]:
            acc = got_ref[0].astype(F32)
            for dev in range(1, N_DEV):
                acc = acc + got_ref[dev].astype(F32)
            sums.append(acc)
        g = sums[0].T if transposed else (sums[0] if n == 1 else jnp.concatenate(sums, axis=1))
        g_ref[...] = g
        d_ref[...], m2_ref[...], v2_ref[...] = _adamw_math(w_ref[...], g, m_ref[...], v_ref[...])

    return pl.pallas_call(
        body, name=name, grid=(steps,), in_specs=got_specs + [spec, spec, spec], out_specs=[spec] * 4,
        out_shape=[jax.ShapeDtypeStruct(w.shape, F32)] * 4, compiler_params=_params("parallel"),
    )(*[part.reshape(N_DEV, r, part.shape[1]) for part in parts], w, m, v)


def _small_update(early, late, w, m, v):
    rows = w.shape[0]

    def body(early_ref, late_ref, w_ref, m_ref, v_ref, g_ref, d_ref, m2_ref, v2_ref):
        sums = []
        for ref in (early_ref, late_ref):
            acc = ref[0]
            for dev in range(1, N_DEV):
                acc = acc + ref[dev]
            sums.append(acc)
        g = jnp.concatenate(sums, axis=0)
        g_ref[...] = g
        d_ref[...], m2_ref[...], v2_ref[...] = _adamw_math(w_ref[...], g, m_ref[...], v_ref[...])

    return pl.pallas_call(
        body, name="small_update", out_shape=[jax.ShapeDtypeStruct((rows, LANES), F32)] * 4,
        compiler_params=pltpu.CompilerParams(vmem_limit_bytes=VMEM_LIMIT),
    )(early.reshape(N_DEV, -1, LANES), late.reshape(N_DEV, -1, LANES), w, m, v)


SMALL = (("pool_w_mix", 512), ("mix_norm", 8), ("ffn2_norm", 8), ("final_norm", 8), ("pool_scale", 8), ("sinks", 8),
         ("loss", 8), ("ffn1_norm", 8))
EARLY, LATE = SMALL[:-1], SMALL[-1:]


def _pack_small(parts, layout=SMALL):
    out = []
    for name, rows in layout:
        flat = parts[name].astype(F32).reshape(-1)
        out.append(jnp.pad(flat, (0, rows * LANES - flat.shape[0])).reshape(rows, LANES))
    return jnp.concatenate(out, axis=0)


def _unpack_small(packed, shapes):
    out, row = {}, 0
    for name, rows in SMALL:
        shape = shapes[name]
        size = int(np.prod(shape)) if shape else 1
        out[name] = packed[row:row + rows].reshape(-1)[:size].reshape(shape)
        row += rows
    return out


def kernel(x, ffn1_norm, ffn1_w_up, ffn1_w_down, mix_norm, w_in, sinks, w_attn_up, pool_w_mix, pool_scale, w_pool_up, w_out, ffn2_norm, ffn2_w_up, ffn2_w_down, final_norm, loss_target, m_ffn1_norm, m_ffn1_w_up, m_ffn1_w_down, m_mix_norm, m_w_in, m_sinks, m_w_attn_up, m_pool_w_mix, m_pool_scale, m_w_pool_up, m_w_out, m_ffn2_norm, m_ffn2_w_up, m_ffn2_w_down, m_final_norm, v_ffn1_norm, v_ffn1_w_up, v_ffn1_w_down, v_mix_norm, v_w_in, v_sinks, v_w_attn_up, v_pool_w_mix, v_pool_scale, v_w_pool_up, v_w_out, v_ffn2_norm, v_ffn2_w_up, v_ffn2_w_down, v_final_norm):
    args = dict(locals())
    weight_names = ("ffn1_norm", "ffn1_w_up", "ffn1_w_down", "mix_norm", "w_in", "sinks", "w_attn_up", "pool_w_mix",
                    "pool_scale", "w_pool_up", "w_out", "ffn2_norm", "ffn2_w_up", "ffn2_w_down", "final_norm")

    shard = {k: (args[p][0].T if tr else args[p][0]).astype(BF16) for k, p, tr in BIG}
    big = {"wup1_t": _exchange(_gather_carry([shard["wup1_t"]]), "gather_up1")[0]}

    def gathering(keys):
        return _gather_carry([shard[k] for k in keys])

    xs, target = x[0], loss_target[0]
    t = xs.shape[0]
    tm_f, tm_b, tk = min(512, t), min(512, t), min(1024, t)
    g1, gm, g2, gf = ffn1_norm, mix_norm, ffn2_norm, final_norm.reshape(1, D_MODEL)
    dist = _attn_dist()
    sink_v = sinks.reshape(N_Q_HEADS)
    wmix_b = pool_w_mix[0].astype(BF16)

    (n1, ab1, act1), (big["wdown1"], big["win_t"]) = _ffn_up(xs, g1, big["wup1_t"], tm_f, gathering(["wdown1", "win_t"]))
    (h1,), (big["wattn"], big["wpool_t"], big["wout"]) = _ffn_down(xs, act1, big["wdown1"], min(1024, t),
                                                                   gathering(["wattn", "wpool_t", "wout"]))
    (u, q, kv, z, gate), (big["wup2_t"],) = _mix_in_fwd(h1, gm, big["win_t"], min(1024, t), gathering(["wup2_t"]))
    attn = _attn_fwd(q, kv, dist, sink_v)
    (h2, a, p, merged, ms, pooled), (big["wdown2"],) = _mix_out_fwd(
        attn, z, gate, h1, big["wattn"], wmix_b, pool_scale, big["wpool_t"], big["wout"], tm_b, gathering(["wdown2"]))
    ab2, n2, act2, loss_lanes, dh3, dhb3, dgf = _ffn_loss(h2, g2, big["wup2_t"], big["wdown2"], gf, target, tm_f)

    got = {}
    (gw_down2,), _ = _wgrad(act2, dhb3, 0.5, D_FF, tk, "wgrad_down2")
    (dab2,), (got["wdown2"],) = _ffn_bwd_hidden(dhb3, ab2, big["wdown2"], tm_f, _scatter_carry([gw_down2]))
    (dh2, dg2), _ = _ffn_bwd_input(dab2, dh3, h2, g2, big["wup2_t"], tm_f)
    (gw_up2,), _ = _wgrad(dab2, n2, 1.0, D_FF, tk, "wgrad_up2")
    dhb2, da_b, dp_b, dattn, dgate, dpooled, dwmix, dscale = _mix_out_bwd(
        dh2, gate, a, p, pooled, big["wattn"], wmix_b, pool_scale, big["wpool_t"], big["wout"], tm_b)
    (gw_out,), _ = _wgrad(merged, dhb2, 1.0, D_MODEL, tk, "wgrad_out")
    (gw_attn,), _ = _wgrad(attn, da_b, 1.0, D_MODEL, tk, "wgrad_attn")
    (gw_pool,), _ = _wgrad(dp_b, ms, 1.0, D_MODEL, tk, "wgrad_pool")
    (dq, dkv_own, dkv_prev, dsinks), (got["wup2_t"],) = _attn_bwd(q, kv, dattn, dist, sink_v, _scatter_carry([gw_up2]))
    (dproj, dh1, dhb1, dgm), (got["wout"], got["wattn"], got["wpool_t"]) = _mix_in_bwd(
        dq, dkv_own, dkv_prev, dpooled, dgate, h1, gm, big["win_t"], dh2, tm_b,
        _scatter_carry([gw_out, gw_attn, gw_pool]))
    (gw_down1,), _ = _wgrad(act1, dhb1, 0.5, D_FF, tk, "wgrad_down1")
    (gw_in,), (got["wdown1"],) = _wgrad(dproj, u, 1.0, IN_WIDTH // 2, tk, "wgrad_in", _scatter_carry([gw_down1]))
    (dab1,), (got["win_t"],) = _ffn_bwd_hidden(dhb1, ab1, big["wdown1"], tm_f, _scatter_carry([gw_in]))
    small_parts = {"pool_w_mix": dwmix, "mix_norm": dgm, "ffn2_norm": dg2, "final_norm": dgf, "pool_scale": dscale,
                   "sinks": dsinks[:, :N_Q_HEADS], "loss": loss_lanes[:, :1]}
    (gw_up1,), (small_early,) = _wgrad(dab1, n1, 1.0, D_FF, tk, "wgrad_up1",
                                       _small_carry(_pack_small(small_parts, EARLY)))
    (dx, dg1), (got["wup1_t"],) = _ffn_bwd_input(dab1, dh1, xs, g1, big["wup1_t"], tm_f, _scatter_carry([gw_up1]))
    (small_late,) = _exchange(_small_carry(_pack_small({"ffn1_norm": dg1}, LATE)), "gather_small")

    grad, delta, new_m, new_v = {}, {}, {}, {}
    for k, p, tr in BIG:
        outside = tr and args[p].shape[-1] % LANES != 0
        turn = (lambda a: a.T) if outside else (lambda a: a)
        res = _sum_adamw(got[k], turn(args[p][0]), turn(args["m_" + p][0]), turn(args["v_" + p][0]),
                         tr and not outside, "adamw_" + k)
        grad[p], delta[p], new_m[p], new_v[p] = (turn(a)[None] for a in res)

    shapes = {name: args[name].shape for name, _ in SMALL if name != "loss"}
    shapes["loss"] = ()
    packed = {pre: _pack_small({**{name: args[pre + name] for name, _ in SMALL if name != "loss"},
                                "loss": jnp.zeros((), F32)}) for pre in ("", "m_", "v_")}
    g_s, d_s, m_s, v_s = _small_update(small_early, small_late, packed[""], packed["m_"], packed["v_"])
    g_small, d_small, m_small, v_small = (_unpack_small(a, shapes) for a in (g_s, d_s, m_s, v_s))
    for name, _ in SMALL:
        if name != "loss":
            grad[name], delta[name], new_m[name], new_v[name] = (
                g_small[name], d_small[name], m_small[name], v_small[name])

    return (g_small["loss"], dx[None], *[grad[n] for n in weight_names], *[delta[n] for n in weight_names],
            *[new_m[n] for n in weight_names], *[new_v[n] for n in weight_names])
```

```python
import jax
import jax.numpy as jnp
import numpy as np
from jax import lax
from jax.experimental import pallas as pl
from jax.experimental.pallas import tpu as pltpu

F32 = jnp.float32
BF16 = jnp.bfloat16

D_MODEL = 1024
D_FF = 2816
N_Q_HEADS = 16
N_KV_HEADS = 2
Q_PER_KV = N_Q_HEADS // N_KV_HEADS
HEAD_DIM = 64
BLOCK = 128
ATTN_WIDTH = N_Q_HEADS * HEAD_DIM
KV_WIDTH = N_KV_HEADS * HEAD_DIM
POOL_WINDOWS = (2, 4, 8, 16)
POOL_GROUP = 128
POOL_WIDTH = 512
HALO = 16
IN_WIDTH = ATTN_WIDTH + 2 * KV_WIDTH + POOL_WIDTH + 2 * D_MODEL
OFF_KV = ATTN_WIDTH
OFF_Z = ATTN_WIDTH + 2 * KV_WIDTH
OFF_GATE = OFF_Z + POOL_WIDTH
NORM_EPS = 1e-6
ADAM_LR = 0.001
ADAM_B1 = 0.9
ADAM_B2 = 0.999
ADAM_EPS = 1e-08
ADAM_WD = 0.01
ADAM_STEP = 10

N_DEV = 8
LANES = 128
FF_CHUNK = 256
SLAB = 32
GATHER_PIECES = 2
VMEM_LIMIT = 56 * 1024 * 1024
MESH = pl.DeviceIdType.MESH


def _nn(a, b):
    return jnp.dot(a, b, preferred_element_type=F32)


def _nt(a, b):
    return lax.dot_general(a, b, (((1,), (1,)), ((), ())), preferred_element_type=F32)


def _tn(a, b):
    return lax.dot_general(a, b, (((0,), (0,)), ((), ())), preferred_element_type=F32)


def _params(*sem):
    return pltpu.CompilerParams(dimension_semantics=sem, vmem_limit_bytes=VMEM_LIMIT)


def _resident(shape):
    return pl.BlockSpec(shape, lambda *_: (0,) * len(shape), pipeline_mode=pl.Buffered(1))


def _rows(tm, cols):
    return pl.BlockSpec((tm, cols), lambda i: (i, 0))


class _Carry:
    def __init__(self, inputs, out_shape, scratch, start, finish, middles=()):
        self.inputs, self.out_shape, self.scratch = list(inputs), list(out_shape), list(scratch)
        self.start, self.finish, self.middles = start, finish, list(middles)


def _launch(body, args, carry=None, *, name, grid, in_specs, out_specs, out_shape, scratch_shapes=(), semantics):
    in_specs, out_specs, out_shape, scratch_shapes = list(in_specs), list(out_specs), list(out_shape), list(scratch_shapes)
    if carry is None:
        res = pl.pallas_call(body, name=name, grid=grid, in_specs=in_specs, out_specs=out_specs, out_shape=out_shape,
                             scratch_shapes=scratch_shapes, compiler_params=_params(*semantics))(*args)
        return list(res), []
    ni, no, ns = len(in_specs), len(out_specs), len(scratch_shapes)
    ci, co = len(carry.inputs), len(carry.out_shape)
    total = int(np.prod(grid))

    def full(*refs):
        own_in, c_in = refs[:ni], refs[ni:ni + ci]
        own_out, c_out = refs[ni + ci:ni + ci + no], refs[ni + ci + no:ni + ci + no + co]
        own_scr, c_sem = refs[ni + ci + no + co:ni + ci + no + co + ns], refs[ni + ci + no + co + ns:]
        step = 0
        for axis, size in enumerate(grid):
            step = step * size + pl.program_id(axis)
        pl.when(step == 0)(lambda: carry.start(c_in, c_out, c_sem))
        for fraction, fn in carry.middles:
            at = min(total - 1, int(fraction * total) + 1)
            pl.when(step == at)(lambda fn=fn: fn(c_in, c_out, c_sem))
        body(*own_in, *own_out, *own_scr)
        pl.when(step == total - 1)(lambda: carry.finish(c_in, c_out, c_sem))

    res = pl.pallas_call(
        full, name=name, grid=grid, in_specs=in_specs + [ANY] * ci, out_specs=out_specs + [ANY] * co,
        out_shape=out_shape + carry.out_shape, scratch_shapes=scratch_shapes + carry.scratch,
        compiler_params=_params(*(["arbitrary"] * len(grid))),
    )(*args, *carry.inputs)
    return list(res[:no]), list(res[no:])


def _rms_fwd(xv, g):
    r = lax.rsqrt(jnp.mean(xv * xv, axis=-1, keepdims=True) + NORM_EPS)
    return xv * r, r


def _rms_bwd(dn, xh, r, g):
    dxh = dn * g
    dx = r * (dxh - xh * jnp.mean(dxh * xh, axis=-1, keepdims=True))
    return dx, jnp.sum(dn * xh, axis=0, keepdims=True)


def _ffn_loss(x, g, wup_t, wdown, gf, target, tm):
    t, d = x.shape
    f = wdown.shape[0]

    def body(x_ref, g_ref, wup_ref, wdn_ref, gf_ref, tgt_ref, ab_ref, n_ref, act_ref, loss_ref, dh_ref, dhb_ref, dg_ref):
        xv = x_ref[...]
        xh, _ = _rms_fwd(xv, g_ref[...])
        n = (xh * g_ref[...]).astype(BF16)
        n_ref[...] = n
        for c in range(f // FF_CHUNK):
            lo, hi = c * FF_CHUNK, (c + 1) * FF_CHUNK
            a = _nt(n, wup_ref[lo:hi, :])
            b = _nt(n, wup_ref[f + lo:f + hi, :])
            ab_ref[:, lo:hi] = a.astype(BF16)
            ab_ref[:, f + lo:f + hi] = b.astype(BF16)
            act_ref[:, lo:hi] = (a * jax.nn.sigmoid(a) * b).astype(BF16)
        h = xv + 0.5 * _nn(act_ref[...], wdn_ref[...])
        yh, r = _rms_fwd(h, gf_ref[...])
        err = yh * gf_ref[...] - tgt_ref[...]
        part = 0.5 * jnp.sum(jnp.mean(err * err, axis=-1, keepdims=True), axis=0, keepdims=True)
        dh, dg = _rms_bwd(err * (1.0 / d), yh, r, gf_ref[...])
        dh_ref[...] = dh
        dhb_ref[...] = dh.astype(BF16)

        @pl.when(pl.program_id(0) == 0)
        def _():
            dg_ref[...] = jnp.zeros_like(dg_ref)
            loss_ref[...] = jnp.zeros_like(loss_ref)

        dg_ref[...] += dg
        loss_ref[...] += jnp.broadcast_to(part, loss_ref.shape)

    return pl.pallas_call(
        body, name="ffn_loss", grid=(t // tm,),
        in_specs=[_rows(tm, d), _resident((1, d)), _resident((2 * f, d)), _resident((f, d)), _resident((1, d)),
                  _rows(tm, d)],
        out_specs=[_rows(tm, 2 * f), _rows(tm, d), _rows(tm, f), pl.BlockSpec((1, LANES), lambda i: (0, 0)),
                   _rows(tm, d), _rows(tm, d), pl.BlockSpec((1, d), lambda i: (0, 0))],
        out_shape=[jax.ShapeDtypeStruct((t, 2 * f), BF16), jax.ShapeDtypeStruct((t, d), BF16),
                   jax.ShapeDtypeStruct((t, f), BF16), jax.ShapeDtypeStruct((1, LANES), F32),
                   jax.ShapeDtypeStruct((t, d), F32), jax.ShapeDtypeStruct((t, d), BF16),
                   jax.ShapeDtypeStruct((1, d), F32)],
        compiler_params=_params("arbitrary"),
    )(x, g, wup_t, wdown, gf, target)


def _ffn_up(x, g, wup_t, tm, carry=None):
    t, d = x.shape
    f = wup_t.shape[0] // 2

    def body(x_ref, g_ref, wup_ref, n_ref, ab_ref, act_ref):
        xh, _ = _rms_fwd(x_ref[...], g_ref[...])
        n = (xh * g_ref[...]).astype(BF16)
        n_ref[...] = n
        for c in range(f // FF_CHUNK):
            lo, hi = c * FF_CHUNK, (c + 1) * FF_CHUNK
            a = _nt(n, wup_ref[lo:hi, :])
            b = _nt(n, wup_ref[f + lo:f + hi, :])
            ab_ref[:, lo:hi] = a.astype(BF16)
            ab_ref[:, f + lo:f + hi] = b.astype(BF16)
            act_ref[:, lo:hi] = (a * jax.nn.sigmoid(a) * b).astype(BF16)

    return _launch(
        body, (x, g, wup_t), carry, name="ffn_up", grid=(t // tm,),
        in_specs=[_rows(tm, d), _resident((1, d)), _resident((2 * f, d))],
        out_specs=[_rows(tm, d), _rows(tm, 2 * f), _rows(tm, f)],
        out_shape=[jax.ShapeDtypeStruct((t, d), BF16), jax.ShapeDtypeStruct((t, 2 * f), BF16),
                   jax.ShapeDtypeStruct((t, f), BF16)],
        semantics=("parallel",))


def _ffn_down(x, act, wdown, tm, carry=None):
    t, d = x.shape
    f = wdown.shape[0]

    def body(x_ref, act_ref, wdn_ref, h_ref):
        h_ref[...] = x_ref[...] + 0.5 * _nn(act_ref[...], wdn_ref[...])

    return _launch(
        body, (x, act, wdown), carry, name="ffn_down", grid=(t // tm,),
        in_specs=[_rows(tm, d), _rows(tm, f), _resident((f, d))], out_specs=[_rows(tm, d)],
        out_shape=[jax.ShapeDtypeStruct((t, d), F32)], semantics=("parallel",))


def _ffn_bwd_hidden(dhb, ab, wdown, tm, carry=None):
    t, d = dhb.shape
    f = wdown.shape[0]

    def body(dh_ref, ab_ref, wdn_ref, dab_ref, dact_ref):
        half = dh_ref[...] * 0.5
        for c in range(f // FF_CHUNK):
            lo, hi = c * FF_CHUNK, (c + 1) * FF_CHUNK
            dact_ref[...] = _nt(half, wdn_ref[lo:hi, :])

            def slab(i, carry_):
                rows = pl.ds(pl.multiple_of(i * SLAB, SLAB), SLAB)
                a = ab_ref[rows, lo:hi].astype(F32)
                b = ab_ref[rows, f + lo:f + hi].astype(F32)
                s = jax.nn.sigmoid(a)
                ds_ = dact_ref[rows, :] * s
                dab_ref[rows, lo:hi] = (ds_ * b * (1.0 + a * (1.0 - s))).astype(BF16)
                dab_ref[rows, f + lo:f + hi] = (ds_ * a).astype(BF16)
                return carry_

            lax.fori_loop(0, tm // SLAB, slab, 0, unroll=True)

    return _launch(
        body, (dhb, ab, wdown), carry, name="ffn_bwd_hidden", grid=(t // tm,),
        in_specs=[_rows(tm, d), _rows(tm, 2 * f), _resident((f, d))], out_specs=[_rows(tm, 2 * f)],
        out_shape=[jax.ShapeDtypeStruct((t, 2 * f), BF16)],
        scratch_shapes=[pltpu.VMEM((tm, FF_CHUNK), F32)], semantics=("parallel",))


def _ffn_bwd_input(dab, dh, x, g, wup_t, tm, carry=None):
    t, d = x.shape
    f2 = wup_t.shape[0]

    def body(dab_ref, dh_ref, x_ref, g_ref, wup_ref, dx_ref, dg_ref):
        dn = _nn(dab_ref[...], wup_ref[...])
        xh, r = _rms_fwd(x_ref[...], g_ref[...])
        dx, dg = _rms_bwd(dn, xh, r, g_ref[...])
        dx_ref[...] = dh_ref[...] + dx

        @pl.when(pl.program_id(0) == 0)
        def _():
            dg_ref[...] = jnp.zeros_like(dg_ref)

        dg_ref[...] += dg

    return _launch(
        body, (dab, dh, x, g, wup_t), carry, name="ffn_bwd_input", grid=(t // tm,),
        in_specs=[_rows(tm, f2), _rows(tm, d), _rows(tm, d), _resident((1, d)), _resident((f2, d))],
        out_specs=[_rows(tm, d), pl.BlockSpec((1, d), lambda i: (0, 0))],
        out_shape=[jax.ShapeDtypeStruct((t, d), F32), jax.ShapeDtypeStruct((1, d), F32)],
        semantics=("arbitrary",))


def _wgrad(lhs, rhs, scale, bm, tk, name, carry=None):
    t, m = lhs.shape
    n = rhs.shape[1]
    steps = t // tk
    chunk = bm if bm <= 2048 else bm // 2

    def body(l_ref, r_ref, o_ref, acc_ref):
        @pl.when(pl.program_id(1) == 0)
        def _():
            acc_ref[...] = jnp.zeros_like(acc_ref)

        for lo in range(0, bm, chunk):
            acc_ref[lo:lo + chunk, :] += _tn(l_ref[:, lo:lo + chunk], r_ref[...])

        @pl.when(pl.program_id(1) == steps - 1)
        def _():
            o_ref[...] = (scale * acc_ref[...]).astype(o_ref.dtype)

    return _launch(
        body, (lhs, rhs), carry, name=name, grid=(m // bm, steps),
        in_specs=[pl.BlockSpec((tk, bm), lambda i, k: (k, i)), pl.BlockSpec((tk, n), lambda i, k: (k, 0))],
        out_specs=[pl.BlockSpec((bm, n), lambda i, k: (i, 0))],
        out_shape=[jax.ShapeDtypeStruct((m, n), WIRE)],
        scratch_shapes=[pltpu.VMEM((bm, n), F32)], semantics=("parallel", "arbitrary"))


def _mix_in_fwd(h, g, win_t, tm, carry=None):
    t, d = h.shape

    def body(h_ref, g_ref, w_ref, u_ref, q_ref, kv_ref, z_ref, gate_ref):
        xh, _ = _rms_fwd(h_ref[...], g_ref[...])
        u = (xh * g_ref[...]).astype(BF16)
        u_ref[...] = u
        q_ref[...] = _nt(u, w_ref[0:OFF_KV, :]).astype(BF16)
        kv_ref[...] = _nt(u, w_ref[OFF_KV:OFF_Z, :]).astype(BF16)
        z_ref[...] = _nt(u, w_ref[OFF_Z:OFF_GATE, :])
        gate_ref[...] = _nt(u, w_ref[OFF_GATE:IN_WIDTH, :]).astype(BF16)

    return _launch(
        body, (h, g, win_t), carry, name="mix_in_fwd", grid=(t // tm,),
        in_specs=[_rows(tm, d), _resident((1, d)), _resident((IN_WIDTH, d))],
        out_specs=[_rows(tm, d), _rows(tm, ATTN_WIDTH), _rows(tm, 2 * KV_WIDTH), _rows(tm, POOL_WIDTH),
                   _rows(tm, 2 * D_MODEL)],
        out_shape=[jax.ShapeDtypeStruct((t, d), BF16), jax.ShapeDtypeStruct((t, ATTN_WIDTH), BF16),
                   jax.ShapeDtypeStruct((t, 2 * KV_WIDTH), BF16), jax.ShapeDtypeStruct((t, POOL_WIDTH), F32),
                   jax.ShapeDtypeStruct((t, 2 * D_MODEL), BF16)],
        semantics=("parallel",))


ALIBI_SLOPES = tuple(float(s) for s in (2.0 ** (-8.0 * np.arange(1, N_Q_HEADS + 1, dtype=np.float32) / N_Q_HEADS)))


def _attn_dist():
    return jnp.asarray(((np.arange(BLOCK)[:, None] - np.arange(BLOCK)[None, :]) % BLOCK).astype(np.float32))


def _own_block():
    shape = (BLOCK, BLOCK)
    return lax.broadcasted_iota(jnp.int32, shape, 1) <= lax.broadcasted_iota(jnp.int32, shape, 0)


def _fold(band2, own):
    return jnp.where(own, band2[:, BLOCK:], band2[:, :BLOCK])


def _unfold(x, own):
    zero = jnp.zeros_like(x)
    return jnp.concatenate([jnp.where(own, zero, x), jnp.where(own, x, zero)], axis=1)


def _low_half(shape):
    return lax.broadcasted_iota(jnp.int32, shape, len(shape) - 1) < HEAD_DIM


def _both_halves(band, kv_head):
    low = _low_half(band.shape)
    swapped = pltpu.roll(band, HEAD_DIM, 1)
    return jnp.where(low, band, swapped) if kv_head == 0 else jnp.where(low, swapped, band)


def _pair_rows(ref, rows, pair, scale=None):
    v = ref[rows, LANES * pair:LANES * (pair + 1)]
    if scale is not None:
        v = v * scale
    low, zero = _low_half(v.shape), jnp.zeros_like(v)
    return jnp.concatenate([jnp.where(low, v, zero), jnp.where(low, zero, v)], axis=0)


def _per_head(even, odd):
    return jnp.where(lax.broadcasted_iota(jnp.int32, (2 * BLOCK, 1), 0) < BLOCK, even, odd)


def _twice(x):
    return jnp.concatenate([x, x], axis=0)


def _pair_scores(q_ref, rows, kk, dist2, pair, first, own2):
    s2 = _nt(_pair_rows(q_ref, rows, pair, HEAD_DIM ** -0.5), kk)
    before = jnp.where(first, -jnp.inf, s2[:, :BLOCK])
    slopes = _per_head(ALIBI_SLOPES[2 * pair], ALIBI_SLOPES[2 * pair + 1])
    return jnp.where(own2, s2[:, BLOCK:], before) - slopes * dist2


def _own_half(ref, rows, head):
    v = ref[rows, LANES * (head // 2):LANES * (head // 2 + 1)]
    low = _low_half(v.shape)
    return jnp.where(low if head % 2 == 0 else jnp.logical_not(low), v, jnp.zeros_like(v))


def _head_scores(q_ref, rows, kk, dist, head, first, own):
    s2 = _nt(_own_half(q_ref, rows, head) * HEAD_DIM ** -0.5, kk)
    before = jnp.where(first, -jnp.inf, s2[:, :BLOCK])
    return jnp.where(own, s2[:, BLOCK:], before) - ALIBI_SLOPES[head] * dist


def _heads_of(stack):
    return jnp.where(_low_half((BLOCK, LANES)), stack[:BLOCK], stack[BLOCK:])


def _softmax_sink(s, sink):
    m = jnp.maximum(jnp.max(s, axis=-1, keepdims=True), sink)
    p = jnp.exp(s - m)
    psink = jnp.exp(sink - m)
    inv = 1.0 / (jnp.sum(p, axis=-1, keepdims=True) + psink)
    return p * inv, psink * inv


def _bands(kvc_ref, kvp_ref, sub):
    own = slice(sub * BLOCK, (sub + 1) * BLOCK)
    before = kvp_ref[...] if sub == 0 else kvc_ref[(sub - 1) * BLOCK:sub * BLOCK, :]
    kband = jnp.concatenate([before[:, 0:LANES], kvc_ref[own, 0:LANES]], axis=0)
    vband = jnp.concatenate([before[:, LANES:2 * LANES], kvc_ref[own, LANES:2 * LANES]], axis=0)
    return ([_both_halves(kband, hk) for hk in range(N_KV_HEADS)],
            [_both_halves(vband, hk) for hk in range(N_KV_HEADS)])


SMEM = pl.BlockSpec(memory_space=pltpu.SMEM)
HEADS = range(N_Q_HEADS)
PAIRS = range(N_Q_HEADS // 2)
PAIRS_PER_KV = Q_PER_KV // 2
def _sub_rows(sub):
    return slice(sub * BLOCK, (sub + 1) * BLOCK)


def _block_before(step):
    per = step // BLOCK
    return pl.BlockSpec((BLOCK, 2 * KV_WIDTH), lambda i: (jnp.maximum(i * per - 1, 0), 0))


def _attn_fwd(q, kv, dist, sinks):
    t = q.shape[0]
    subs = range(2)

    def body(q_ref, kvc_ref, kvp_ref, dist_ref, sink_ref, o_ref, s_scr, p_scr):
        own2 = _twice(_own_block())
        dist2 = _twice(dist_ref[...])
        bands = [_bands(kvc_ref, kvp_ref, sub) for sub in subs]
        for sub in subs:
            first = jnp.logical_and(pl.program_id(0) == 0, sub == 0)
            for pair in PAIRS:
                s_scr[sub, pair] = _pair_scores(q_ref, _sub_rows(sub), bands[sub][0][pair // PAIRS_PER_KV], dist2, pair,
                                                first, own2)
        for sub in subs:
            for pair in PAIRS:
                probs, _ = _softmax_sink(s_scr[sub, pair], _per_head(sink_ref[2 * pair], sink_ref[2 * pair + 1]))
                p_scr[sub, pair] = _unfold(probs.astype(BF16), own2)
        for sub in subs:
            for pair in PAIRS:
                out = _nn(p_scr[sub, pair], bands[sub][1][pair // PAIRS_PER_KV])
                o_ref[_sub_rows(sub), LANES * pair:LANES * (pair + 1)] = _heads_of(out).astype(BF16)

    step = len(subs) * BLOCK
    return pl.pallas_call(
        body, name="attn_fwd", grid=(t // step,),
        in_specs=[_rows(step, ATTN_WIDTH), _rows(step, 2 * KV_WIDTH), _block_before(step), _resident(dist.shape), SMEM],
        out_specs=_rows(step, ATTN_WIDTH),
        out_shape=jax.ShapeDtypeStruct((t, ATTN_WIDTH), BF16),
        scratch_shapes=[pltpu.VMEM((len(subs), len(PAIRS), 2 * BLOCK, BLOCK), F32),
                        pltpu.VMEM((len(subs), len(PAIRS), 2 * BLOCK, 2 * BLOCK), BF16)],
        compiler_params=_params("parallel"),
    )(q, kv, kv, dist, sinks)


def _pool_counts(tm, width):
    row = pl.program_id(0) * tm + lax.broadcasted_iota(jnp.int32, (tm, 1), 0)
    return jnp.minimum(row + 1, width).astype(F32)


def _trailing_sums(zz, group):
    s = zz
    for k in range(group + 1):
        s = s + pltpu.roll(s, 1 << k, 0)
    return s


def _leading_sums(zz, group):
    rows = zz.shape[0]
    s = zz
    for k in range(group + 1):
        s = s + pltpu.roll(s, rows - (1 << k), 0)
    return s


def _mix_out_fwd(attn, z, gate, h, wattn, wmix, scale, wpool_t, wout, tm, carry=None):
    t, d = h.shape

    def body(attn_ref, z_ref, halo_ref, gate_ref, h_ref, wattn_ref, wmix_ref, scale_ref, wpool_ref, wout_ref,
             h2_ref, a_ref, p_ref, merged_ref, ms_ref, pooled_ref):
        halo = jnp.where(pl.program_id(0) == 0, 0.0, halo_ref[...])
        for gi, width in enumerate(POOL_WINDOWS):
            lo, hi = gi * POOL_GROUP, (gi + 1) * POOL_GROUP
            zg = z_ref[:, lo:hi]
            sums = _trailing_sums(jnp.concatenate([halo[:, lo:hi], zg], axis=0), gi)[HALO:, :]
            pooled = (sums / _pool_counts(tm, width) - zg).astype(BF16)
            pooled_ref[:, lo:hi] = pooled
            ms_ref[:, lo:hi] = (_nn(pooled, wmix_ref[gi]) * scale_ref[:, lo:hi]).astype(BF16)
        p = _nt(ms_ref[...], wpool_ref[...])
        a = _nn(attn_ref[...], wattn_ref[...])
        a_ref[...] = a.astype(BF16)
        p_ref[...] = p.astype(BF16)
        merged = (jax.nn.sigmoid(gate_ref[:, 0:d].astype(F32)) * a
                  + jax.nn.sigmoid(gate_ref[:, d:2 * d].astype(F32)) * p).astype(BF16)
        merged_ref[...] = merged
        h2_ref[...] = h_ref[...] + _nn(merged, wout_ref[...])

    halo_spec = pl.BlockSpec((HALO, POOL_WIDTH), lambda i: (jnp.maximum(i * (tm // HALO) - 1, 0), 0))
    return _launch(
        body, (attn, z, z, gate, h, wattn, wmix, scale, wpool_t, wout), carry, name="mix_out_fwd", grid=(t // tm,),
        in_specs=[_rows(tm, ATTN_WIDTH), _rows(tm, POOL_WIDTH), halo_spec, _rows(tm, 2 * d), _rows(tm, d),
                  _resident(wattn.shape), _resident(wmix.shape), _resident(scale.shape), _resident(wpool_t.shape),
                  _resident(wout.shape)],
        out_specs=[_rows(tm, d), _rows(tm, d), _rows(tm, d), _rows(tm, d), _rows(tm, POOL_WIDTH),
                   _rows(tm, POOL_WIDTH)],
        out_shape=[jax.ShapeDtypeStruct((t, d), F32), jax.ShapeDtypeStruct((t, d), BF16),
                   jax.ShapeDtypeStruct((t, d), BF16), jax.ShapeDtypeStruct((t, d), BF16),
                   jax.ShapeDtypeStruct((t, POOL_WIDTH), BF16), jax.ShapeDtypeStruct((t, POOL_WIDTH), BF16)],
        semantics=("parallel",))


def _mix_out_bwd(dh, gate, a, p, pooled, wattn, wmix, scale, wpool_t, wout, tm):
    t, d = dh.shape

    def body(dh_ref, gate_ref, a_ref, p_ref, pooled_ref, wattn_ref, wmix_ref, scale_ref, wpool_ref, wout_ref,
             dhb_ref, dab_ref, dpb_ref, dattn_ref, dgate_ref, dpooled_ref, dwmix_ref, dscale_ref):
        @pl.when(pl.program_id(0) == 0)
        def _():
            dwmix_ref[...] = jnp.zeros_like(dwmix_ref)
            dscale_ref[...] = jnp.zeros_like(dscale_ref)

        dhb = dh_ref[...].astype(BF16)
        dhb_ref[...] = dhb
        dm = _nt(dhb, wout_ref[...])
        sa = jax.nn.sigmoid(gate_ref[:, 0:d].astype(F32))
        sp = jax.nn.sigmoid(gate_ref[:, d:2 * d].astype(F32))
        da = (dm * sa).astype(BF16)
        dp = (dm * sp).astype(BF16)
        dab_ref[...] = da
        dpb_ref[...] = dp
        dgate_ref[:, 0:d] = (dm * a_ref[...].astype(F32) * (sa * (1.0 - sa))).astype(BF16)
        dgate_ref[:, d:2 * d] = (dm * p_ref[...].astype(F32) * (sp * (1.0 - sp))).astype(BF16)
        dattn_ref[...] = _nt(da, wattn_ref[...]).astype(BF16)
        dms = _nn(dp, wpool_ref[...])
        for gi in range(len(POOL_WINDOWS)):
            lo, hi = gi * POOL_GROUP, (gi + 1) * POOL_GROUP
            pooled_g = pooled_ref[:, lo:hi]
            mixed = _nn(pooled_g, wmix_ref[gi])
            dscale_ref[:, lo:hi] += jnp.sum(dms[:, lo:hi] * mixed, axis=0, keepdims=True)
            dmixed = (dms[:, lo:hi] * scale_ref[:, lo:hi]).astype(BF16)
            dwmix_ref[gi] += _tn(pooled_g, dmixed)
            dpooled_ref[:, lo:hi] = _nt(dmixed, wmix_ref[gi])

    acc = lambda shape: pl.BlockSpec(shape, lambda i: (0,) * len(shape))
    return pl.pallas_call(
        body, name="mix_out_bwd", grid=(t // tm,),
        in_specs=[_rows(tm, d), _rows(tm, 2 * d), _rows(tm, d), _rows(tm, d), _rows(tm, POOL_WIDTH),
                  _resident(wattn.shape), _resident(wmix.shape), _resident(scale.shape), _resident(wpool_t.shape),
                  _resident(wout.shape)],
        out_specs=[_rows(tm, d), _rows(tm, d), _rows(tm, d), _rows(tm, ATTN_WIDTH), _rows(tm, 2 * d),
                   _rows(tm, POOL_WIDTH), acc(wmix.shape), acc((1, POOL_WIDTH))],
        out_shape=[jax.ShapeDtypeStruct((t, d), BF16), jax.ShapeDtypeStruct((t, d), BF16),
                   jax.ShapeDtypeStruct((t, d), BF16), jax.ShapeDtypeStruct((t, ATTN_WIDTH), BF16),
                   jax.ShapeDtypeStruct((t, 2 * d), BF16), jax.ShapeDtypeStruct((t, POOL_WIDTH), F32),
                   jax.ShapeDtypeStruct(wmix.shape, F32), jax.ShapeDtypeStruct((1, POOL_WIDTH), F32)],
        compiler_params=_params("arbitrary"),
    )(dh, gate, a, p, pooled, wattn, wmix, scale, wpool_t, wout)


def _fold_halves(x):
    return x + pltpu.roll(x, HEAD_DIM, 1)


def _attn_bwd(q, kv, dattn, dist, sinks, carry=None):
    t = q.shape[0]
    subs = range(1)

    def body(q_ref, kvc_ref, kvp_ref, do_ref, dist_ref, sink_ref, dq_ref, dkv_own_ref, dkv_prev_ref, dsink_ref,
             s_scr, dp_scr, p_scr, ds_scr):
        @pl.when(pl.program_id(0) == 0)
        def _():
            dsink_ref[...] = jnp.zeros_like(dsink_ref)

        own = _own_block()
        dist_v = dist_ref[...]
        bands = [_bands(kvc_ref, kvp_ref, sub) for sub in subs]
        lane = lax.broadcasted_iota(jnp.int32, (1, LANES), 1)
        for sub in subs:
            first = jnp.logical_and(pl.program_id(0) == 0, sub == 0)
            for head in HEADS:
                hk = head // Q_PER_KV
                s_scr[sub, head] = _head_scores(q_ref, _sub_rows(sub), bands[sub][0][hk], dist_v, head, first, own)
                dp_scr[sub, head] = _fold(_nt(_own_half(do_ref, _sub_rows(sub), head), bands[sub][1][hk]), own)
        dsink = jnp.zeros((1, LANES), F32)
        for sub in subs:
            for head in HEADS:
                probs, psink = _softmax_sink(s_scr[sub, head], sink_ref[head])
                dprobs = dp_scr[sub, head]
                rowdot = jnp.sum(probs * dprobs, axis=-1, keepdims=True)
                p_scr[sub, head] = _unfold(probs.astype(BF16), own)
                ds_scr[sub, head] = _unfold((probs * (dprobs - rowdot)).astype(BF16), own)
                dsink = dsink + jnp.where(lane == head, jnp.sum(-psink * rowdot, axis=0, keepdims=True), 0.0)
        for sub in subs:
            rows = _sub_rows(sub)
            dk_heads, dv_heads = [], []
            for hk in range(N_KV_HEADS):
                dk_t = jnp.zeros((LANES, 2 * BLOCK), F32)
                dv_t = jnp.zeros((LANES, 2 * BLOCK), F32)
                for pair in range(Q_PER_KV // 2):
                    cols = slice(LANES * (hk * PAIRS_PER_KV + pair), LANES * (hk * PAIRS_PER_KV + pair + 1))
                    q_t = (q_ref[rows, cols] * HEAD_DIM ** -0.5).T
                    do_t = do_ref[rows, cols].T
                    dqs = []
                    for head in (hk * Q_PER_KV + 2 * pair, hk * Q_PER_KV + 2 * pair + 1):
                        mine = (lax.broadcasted_iota(jnp.int32, q_t.shape, 0) < HEAD_DIM) == (head % 2 == 0)
                        dv_t = dv_t + _nn(jnp.where(mine, do_t, jnp.zeros_like(do_t)), p_scr[sub, head])
                        dk_t = dk_t + _nn(jnp.where(mine, q_t, jnp.zeros_like(q_t)), ds_scr[sub, head])
                        dqs.append(_nn(ds_scr[sub, head], bands[sub][0][hk]))
                    dq_pair = jnp.where(_low_half(dqs[0].shape), dqs[0], dqs[1])
                    dq_ref[rows, cols] = (dq_pair * HEAD_DIM ** -0.5).astype(BF16)
                dk_heads.append(_fold_halves(dk_t.T))
                dv_heads.append(_fold_halves(dv_t.T))
            low = _low_half(dk_heads[0].shape)
            dkv = jnp.concatenate([jnp.where(low, dk_heads[0], dk_heads[1]), jnp.where(low, dv_heads[0], dv_heads[1])],
                                  axis=1)
            dkv_prev_ref[rows, :] = dkv[0:BLOCK, :]
            dkv_own_ref[rows, :] = dkv[BLOCK:2 * BLOCK, :]
        dsink_ref[...] += dsink

    step = len(subs) * BLOCK
    return _launch(
        body, (q, kv, kv, dattn, dist, sinks), carry, name="attn_bwd", grid=(t // step,),
        in_specs=[_rows(step, ATTN_WIDTH), _rows(step, 2 * KV_WIDTH), _block_before(step), _rows(step, ATTN_WIDTH),
                  _resident(dist.shape), SMEM],
        out_specs=[_rows(step, ATTN_WIDTH), _rows(step, 2 * KV_WIDTH), _rows(step, 2 * KV_WIDTH),
                   pl.BlockSpec((1, LANES), lambda i: (0, 0))],
        out_shape=[jax.ShapeDtypeStruct((t, ATTN_WIDTH), BF16), jax.ShapeDtypeStruct((t, 2 * KV_WIDTH), F32),
                   jax.ShapeDtypeStruct((t, 2 * KV_WIDTH), F32), jax.ShapeDtypeStruct((1, LANES), F32)],
        scratch_shapes=[pltpu.VMEM((len(subs), N_Q_HEADS, BLOCK, BLOCK), F32),
                        pltpu.VMEM((len(subs), N_Q_HEADS, BLOCK, BLOCK), F32),
                        pltpu.VMEM((len(subs), N_Q_HEADS, BLOCK, 2 * BLOCK), BF16),
                        pltpu.VMEM((len(subs), N_Q_HEADS, BLOCK, 2 * BLOCK), BF16)],
        semantics=("arbitrary",))


def _mix_in_bwd(dq, dkv_own, dkv_prev, dpooled, dgate, h, g, win_t, dh_res, tm, carry=None):
    t, d = h.shape
    nt = t // tm

    def body(dq_ref, own_ref, prev_ref, prev_next_ref, dpool_ref, halo_ref, dgate_ref, h_ref, g_ref, w_ref, res_ref,
             dproj_ref, dh_ref, dhb_ref, dg_ref):
        i = pl.program_id(0)
        last = i == nt - 1
        dproj_ref[:, 0:OFF_KV] = dq_ref[...]
        from_next = jnp.where(last, 0.0, prev_next_ref[...])
        if tm > BLOCK:
            from_next = jnp.concatenate([prev_ref[BLOCK:tm, :], from_next], axis=0)
        dproj_ref[:, OFF_KV:OFF_Z] = (own_ref[...] + from_next).astype(BF16)
        halo = jnp.where(last, 0.0, halo_ref[...])
        for gi, width in enumerate(POOL_WINDOWS):
            lo, hi = gi * POOL_GROUP, (gi + 1) * POOL_GROUP
            dpg = dpool_ref[:, lo:hi]
            scaled = jnp.concatenate([dpg / _pool_counts(tm, width), halo[:, lo:hi] / float(width)], axis=0)
            dz = _leading_sums(scaled, gi)[0:tm, :] - dpg
            dproj_ref[:, OFF_Z + lo:OFF_Z + hi] = dz.astype(BF16)
        dproj_ref[:, OFF_GATE:IN_WIDTH] = dgate_ref[...]
        du = _nn(dproj_ref[...], w_ref[...])
        xh, r = _rms_fwd(h_ref[...], g_ref[...])
        dx, dg = _rms_bwd(du, xh, r, g_ref[...])
        dh = res_ref[...] + dx
        dh_ref[...] = dh
        dhb_ref[...] = dh.astype(BF16)

        @pl.when(i == 0)
        def _():
            dg_ref[...] = jnp.zeros_like(dg_ref)

        dg_ref[...] += dg

    per = tm // BLOCK
    next_block = pl.BlockSpec((BLOCK, 2 * KV_WIDTH), lambda i: (jnp.minimum((i + 1) * per, t // BLOCK - 1), 0))
    next_halo = pl.BlockSpec((HALO, POOL_WIDTH), lambda i: (jnp.minimum((i + 1) * (tm // HALO), t // HALO - 1), 0))
    return _launch(
        body, (dq, dkv_own, dkv_prev, dkv_prev, dpooled, dpooled, dgate, h, g, win_t, dh_res), carry,
        name="mix_in_bwd", grid=(nt,),
        in_specs=[_rows(tm, ATTN_WIDTH), _rows(tm, 2 * KV_WIDTH), _rows(tm, 2 * KV_WIDTH), next_block,
                  _rows(tm, POOL_WIDTH), next_halo, _rows(tm, 2 * d), _rows(tm, d), _resident((1, d)),
                  _resident((IN_WIDTH, d)), _rows(tm, d)],
        out_specs=[_rows(tm, IN_WIDTH), _rows(tm, d), _rows(tm, d), pl.BlockSpec((1, d), lambda i: (0, 0))],
        out_shape=[jax.ShapeDtypeStruct((t, IN_WIDTH), BF16), jax.ShapeDtypeStruct((t, d), F32),
                   jax.ShapeDtypeStruct((t, d), BF16), jax.ShapeDtypeStruct((1, d), F32)],
        semantics=("arbitrary",))


BIG = (("wup1_t", "ffn1_w_up", True), ("wdown1", "ffn1_w_down", False), ("win_t", "w_in", True),
       ("wattn", "w_attn_up", False), ("wpool_t", "w_pool_up", True), ("wout", "w_out", False),
       ("wup2_t", "ffn2_w_up", True), ("wdown2", "ffn2_w_down", False))
ANY = pl.BlockSpec(memory_space=pl.ANY)
WIRE = BF16


def _place():
    return lax.axis_index("x"), lax.axis_index("y"), lax.axis_index("c")


def _peer(k):
    x, y, c = _place()
    return x ^ (k >> 2), y ^ ((k >> 1) & 1), c ^ (k & 1)


def _index(px, py, pc):
    return 4 * px + 2 * py + pc


def _gather_carry(shards):
    n = len(shards) * GATHER_PIECES

    def tools(ins, outs, sems):
        send_sems, recv_sems, local_sems = sems
        x, y, c = _place()
        chips = [(1 - x, y), (x, 1 - y), (1 - x, 1 - y)]

        def piece(item):
            w, q = divmod(item, GATHER_PIECES)
            r = ins[w].shape[0]
            return w, r, q * (r // GATHER_PIECES), r // GATHER_PIECES

        def mine(item):
            w, _, first, size = piece(item)
            return ins[w].at[pl.ds(first, size), :]

        def rows(item, px, py, pc):
            w, r, first, size = piece(item)
            return outs[w].at[pl.ds(_index(px, py, pc) * r + first, size), :]

        def copy(item, k, block, to, src=None):
            return pltpu.make_async_remote_copy(
                src_ref=rows(item, *block) if src is None else src, dst_ref=rows(item, *block),
                send_sem=send_sems.at[item, k], recv_sem=recv_sems.at[item, k], device_id=to, device_id_type=MESH)

        def own(item):
            return ([pltpu.make_async_copy(mine(item), rows(item, x, y, c), local_sems.at[item]),
                     copy(item, 0, (x, y, c), (x, y, 1 - c), src=mine(item))]
                    + [copy(item, 1 + j, (x, y, c), (*chip, c), src=mine(item)) for j, chip in enumerate(chips)])

        def passed(item, j):
            return copy(item, 4 + j, (*chips[j], c), (x, y, 1 - c))

        return (x, y, c), chips, copy, own, passed

    def start(ins, outs, sems):
        _, _, _, own, _ = tools(ins, outs, sems)
        for item in range(n):
            for cp in own(item):
                cp.start()

    def forward(item):
        def run(ins, outs, sems):
            (x, y, c), chips, copy, _, passed = tools(ins, outs, sems)
            for j, chip in enumerate(chips):
                copy(item, 1 + j, (*chip, c), (x, y, c)).wait_recv()
                passed(item, j).start()
        return run

    sizes = np.cumsum([s.size / GATHER_PIECES for s in shards for _ in range(GATHER_PIECES)])
    middles = [(float(sizes[item] / sizes[-1]), forward(item)) for item in range(n)]

    def finish(ins, outs, sems):
        (x, y, c), chips, copy, own, passed = tools(ins, outs, sems)
        for item in range(n):
            copy(item, 0, (x, y, 1 - c), (x, y, c)).wait_recv()
            for j, chip in enumerate(chips):
                copy(item, 4 + j, (*chip, 1 - c), (x, y, c)).wait_recv()
        for item in range(n):
            local, *sent = own(item)
            for cp in sent + [passed(item, j) for j in range(len(chips))]:
                cp.wait_send()
            local.wait()

    return _Carry(
        shards, [jax.ShapeDtypeStruct((N_DEV * s.shape[0], s.shape[1]), s.dtype) for s in shards],
        [pltpu.SemaphoreType.DMA((n, N_DEV - 1)), pltpu.SemaphoreType.DMA((n, N_DEV - 1)),
         pltpu.SemaphoreType.DMA((n,))], start, finish, middles)


def _scatter_carry(grads):
    n = len(grads)

    def tools(ins, outs, sems):
        send_sems, recv_sems, local_sems = sems
        me = _index(*_place())

        def block(ref, dev):
            r = ref.shape[0] // N_DEV
            return ref.at[pl.ds(dev * r, r), :]

        def copy(w, k, landing):
            to = _peer(k)
            return pltpu.make_async_remote_copy(
                src_ref=block(ins[w], _index(*to)), dst_ref=block(outs[w], landing), send_sem=send_sems.at[w, k - 1],
                recv_sem=recv_sems.at[w, k - 1], device_id=to, device_id_type=MESH)

        def mine(w):
            return pltpu.make_async_copy(block(ins[w], me), block(outs[w], me), local_sems.at[w])

        return me, copy, mine

    def start(ins, outs, sems):
        me, copy, mine = tools(ins, outs, sems)
        for w in range(n):
            mine(w).start()
            for k in range(1, N_DEV):
                copy(w, k, me).start()

    def finish(ins, outs, sems):
        _, copy, mine = tools(ins, outs, sems)
        for w in range(n):
            for k in range(1, N_DEV):
                copy(w, k, _index(*_peer(k))).wait()
            mine(w).wait()

    return _Carry(
        grads, [jax.ShapeDtypeStruct(g.shape, g.dtype) for g in grads],
        [pltpu.SemaphoreType.DMA((n, N_DEV - 1)), pltpu.SemaphoreType.DMA((n, N_DEV - 1)),
         pltpu.SemaphoreType.DMA((n,))], start, finish)


def _small_carry(small):
    srows = small.shape[0]

    def tools(ins, outs, sems):
        send_sems, recv_sems, local_sem = sems
        me = _index(*_place())

        def slot(dev):
            return outs[0].at[pl.ds(dev * srows, srows), :]

        def copy(k, landing):
            return pltpu.make_async_remote_copy(
                src_ref=ins[0], dst_ref=slot(landing), send_sem=send_sems.at[k - 1], recv_sem=recv_sems.at[k - 1],
                device_id=_peer(k), device_id_type=MESH)

        return me, copy, pltpu.make_async_copy(ins[0], slot(me), local_sem)

    def start(ins, outs, sems):
        me, copy, mine = tools(ins, outs, sems)
        mine.start()
        for k in range(1, N_DEV):
            copy(k, me).start()

    def finish(ins, outs, sems):
        _, copy, mine = tools(ins, outs, sems)
        for k in range(1, N_DEV):
            copy(k, _index(*_peer(k))).wait()
        mine.wait()

    return _Carry([small], [jax.ShapeDtypeStruct((N_DEV * srows, LANES), small.dtype)],
                  [pltpu.SemaphoreType.DMA((N_DEV - 1,)), pltpu.SemaphoreType.DMA((N_DEV - 1,)),
                   pltpu.SemaphoreType.DMA], start, finish)


def _exchange(carry, name):
    ci = len(carry.inputs)
    co = len(carry.out_shape)

    def body(*refs):
        parts = refs[:ci], refs[ci:ci + co], refs[ci + co:]
        carry.start(*parts)
        for _, fn in carry.middles:
            fn(*parts)
        carry.finish(*parts)

    return list(pl.pallas_call(body, name=name, in_specs=[ANY] * ci, out_specs=[ANY] * co, out_shape=carry.out_shape,
                               scratch_shapes=carry.scratch)(*carry.inputs))


def _adamw_math(w, g, m, v):
    m = ADAM_B1 * m + (1.0 - ADAM_B1) * g
    v = ADAM_B2 * v + (1.0 - ADAM_B2) * (g * g)
    m_hat = m / (1.0 - ADAM_B1 ** ADAM_STEP)
    v_hat = v / (1.0 - ADAM_B2 ** ADAM_STEP)
    return -ADAM_LR * (m_hat / (jnp.sqrt(v_hat) + ADAM_EPS) + ADAM_WD * w), m, v


def _sum_adamw(got, w, m, v, transposed, name):
    parts = list(got) if isinstance(got, (list, tuple)) else [got]
    r = parts[0].shape[0] // N_DEV
    cols = sum(part.shape[1] for part in parts)
    if transposed:
        (only,) = parts
        tile = cols if cols <= 512 else 256
        got_specs = [pl.BlockSpec((N_DEV, r, tile), lambda i: (0, 0, i))]
        spec, steps = pl.BlockSpec((tile, r), lambda i: (i, 0)), cols // tile
    else:
        tile = r if r <= 256 else r // 2
        got_specs = [pl.BlockSpec((N_DEV, tile, part.shape[1]), lambda i: (0, i, 0)) for part in parts]
        spec, steps = pl.BlockSpec((tile, cols), lambda i: (i, 0)), r // tile
    n = len(parts)

    def body(*refs):
        w_ref, m_ref, v_ref, g_ref, d_ref, m2_ref, v2_ref = refs[n:]
        sums = []
        for got_ref in refs[:n]:
            acc = got_ref[0].astype(F32)
            for dev in range(1, N_DEV):
                acc = acc + got_ref[dev].astype(F32)
            sums.append(acc)
        g = sums[0].T if transposed else (sums[0] if n == 1 else jnp.concatenate(sums, axis=1))
        g_ref[...] = g
        d_ref[...], m2_ref[...], v2_ref[...] = _adamw_math(w_ref[...], g, m_ref[...], v_ref[...])

    return pl.pallas_call(
        body, name=name, grid=(steps,), in_specs=got_specs + [spec, spec, spec], out_specs=[spec] * 4,
        out_shape=[jax.ShapeDtypeStruct(w.shape, F32)] * 4, compiler_params=_params("parallel"),
    )(*[part.reshape(N_DEV, r, part.shape[1]) for part in parts], w, m, v)


def _small_update(early, late, w, m, v):
    def body(early_ref, late_ref, w_ref, m_ref, v_ref, *outs):
        sums = []
        for ref in (early_ref, late_ref):
            acc = ref[0]
            for dev in range(1, N_DEV):
                acc = acc + ref[dev]
            sums.append(acc)
        g = jnp.concatenate(sums, axis=0)
        results = (g,) + _adamw_math(w_ref[...], g, m_ref[...], v_ref[...])
        for kind, packed in enumerate(results):
            row = 0
            for part, (_, part_rows) in enumerate(SMALL):
                outs[kind * len(SMALL) + part][...] = packed[row:row + part_rows]
                row += part_rows

    res = pl.pallas_call(
        body, name="small_update",
        out_shape=[jax.ShapeDtypeStruct((part_rows, LANES), F32) for _ in range(4) for _, part_rows in SMALL],
        compiler_params=pltpu.CompilerParams(vmem_limit_bytes=VMEM_LIMIT),
    )(early.reshape(N_DEV, -1, LANES), late.reshape(N_DEV, -1, LANES), w, m, v)
    names = [name for name, _ in SMALL]
    return [dict(zip(names, res[kind * len(SMALL):(kind + 1) * len(SMALL)])) for kind in range(4)]


SMALL = (("pool_w_mix", 512), ("mix_norm", 8), ("ffn2_norm", 8), ("final_norm", 8), ("pool_scale", 8), ("sinks", 8),
         ("loss", 8), ("ffn1_norm", 8))
EARLY, LATE = SMALL[:-1], SMALL[-1:]


def _pack_small(parts, layout=SMALL):
    out = []
    for name, rows in layout:
        flat = parts[name].astype(F32).reshape(-1)
        out.append(jnp.pad(flat, (0, rows * LANES - flat.shape[0])).reshape(rows, LANES))
    return jnp.concatenate(out, axis=0)


def _unpack_small(parts, shapes):
    out = {}
    for name, _ in SMALL:
        shape = shapes[name]
        size = int(np.prod(shape)) if shape else 1
        flat = parts[name].reshape(-1)
        out[name] = (flat if size == flat.shape[0] else flat[:size]).reshape(shape)
    return out


def kernel(x, ffn1_norm, ffn1_w_up, ffn1_w_down, mix_norm, w_in, sinks, w_attn_up, pool_w_mix, pool_scale, w_pool_up, w_out, ffn2_norm, ffn2_w_up, ffn2_w_down, final_norm, loss_target, m_ffn1_norm, m_ffn1_w_up, m_ffn1_w_down, m_mix_norm, m_w_in, m_sinks, m_w_attn_up, m_pool_w_mix, m_pool_scale, m_w_pool_up, m_w_out, m_ffn2_norm, m_ffn2_w_up, m_ffn2_w_down, m_final_norm, v_ffn1_norm, v_ffn1_w_up, v_ffn1_w_down, v_mix_norm, v_w_in, v_sinks, v_w_attn_up, v_pool_w_mix, v_pool_scale, v_w_pool_up, v_w_out, v_ffn2_norm, v_ffn2_w_up, v_ffn2_w_down, v_final_norm):
    args = dict(locals())
    weight_names = ("ffn1_norm", "ffn1_w_up", "ffn1_w_down", "mix_norm", "w_in", "sinks", "w_attn_up", "pool_w_mix",
                    "pool_scale", "w_pool_up", "w_out", "ffn2_norm", "ffn2_w_up", "ffn2_w_down", "final_norm")

    shard = {k: (args[p][0].T if tr else args[p][0]).astype(BF16) for k, p, tr in BIG}
    big = {"wup1_t": _exchange(_gather_carry([shard["wup1_t"]]), "gather_up1")[0]}

    def gathering(keys):
        return _gather_carry([shard[k] for k in keys])

    xs, target = x[0], loss_target[0]
    t = xs.shape[0]
    tm_f, tm_b, tk = min(512, t), min(512, t), min(1024, t)
    g1, gm, g2, gf = ffn1_norm, mix_norm, ffn2_norm, final_norm.reshape(1, D_MODEL)
    dist = _attn_dist()
    sink_v = sinks.reshape(N_Q_HEADS)
    wmix_b = pool_w_mix[0].astype(BF16)

    (n1, ab1, act1), (big["wdown1"], big["win_t"]) = _ffn_up(xs, g1, big["wup1_t"], tm_f, gathering(["wdown1", "win_t"]))
    (h1,), (big["wattn"], big["wpool_t"], big["wout"]) = _ffn_down(xs, act1, big["wdown1"], tm_f,
                                                                   gathering(["wattn", "wpool_t", "wout"]))
    (u, q, kv, z, gate), (big["wup2_t"],) = _mix_in_fwd(h1, gm, big["win_t"], tm_f, gathering(["wup2_t"]))
    attn = _attn_fwd(q, kv, dist, sink_v)
    (h2, a, p, merged, ms, pooled), (big["wdown2"],) = _mix_out_fwd(
        attn, z, gate, h1, big["wattn"], wmix_b, pool_scale, big["wpool_t"], big["wout"], tm_b, gathering(["wdown2"]))
    ab2, n2, act2, loss_lanes, dh3, dhb3, dgf = _ffn_loss(h2, g2, big["wup2_t"], big["wdown2"], gf, target, tm_f)

    got = {}
    (gw_down2,), _ = _wgrad(act2, dhb3, 0.5, D_FF, tk, "wgrad_down2")
    (dab2,), (got["wdown2"],) = _ffn_bwd_hidden(dhb3, ab2, big["wdown2"], tm_f, _scatter_carry([gw_down2]))
    (dh2, dg2), _ = _ffn_bwd_input(dab2, dh3, h2, g2, big["wup2_t"], tm_f)
    (gw_up2,), _ = _wgrad(dab2, n2, 1.0, D_FF, tk, "wgrad_up2")
    dhb2, da_b, dp_b, dattn, dgate, dpooled, dwmix, dscale = _mix_out_bwd(
        dh2, gate, a, p, pooled, big["wattn"], wmix_b, pool_scale, big["wpool_t"], big["wout"], tm_b)
    (gw_out,), _ = _wgrad(merged, dhb2, 1.0, D_MODEL, tk, "wgrad_out")
    (gw_attn,), _ = _wgrad(attn, da_b, 1.0, D_MODEL, tk, "wgrad_attn")
    (gw_pool,), _ = _wgrad(dp_b, ms, 1.0, D_MODEL, tk, "wgrad_pool")
    (dq, dkv_own, dkv_prev, dsinks), (got["wup2_t"],) = _attn_bwd(q, kv, dattn, dist, sink_v, _scatter_carry([gw_up2]))
    (dproj, dh1, dhb1, dgm), (got["wout"], got["wattn"], got["wpool_t"]) = _mix_in_bwd(
        dq, dkv_own, dkv_prev, dpooled, dgate, h1, gm, big["win_t"], dh2, tm_b,
        _scatter_carry([gw_out, gw_attn, gw_pool]))
    (gw_down1,), _ = _wgrad(act1, dhb1, 0.5, D_FF, tk, "wgrad_down1")
    (gw_in,), (got["wdown1"],) = _wgrad(dproj, u, 1.0, IN_WIDTH // 2, tk, "wgrad_in", _scatter_carry([gw_down1]))
    (dab1,), (got["win_t"],) = _ffn_bwd_hidden(dhb1, ab1, big["wdown1"], tm_f, _scatter_carry([gw_in]))
    small_parts = {"pool_w_mix": dwmix, "mix_norm": dgm, "ffn2_norm": dg2, "final_norm": dgf, "pool_scale": dscale,
                   "sinks": dsinks[:, :N_Q_HEADS], "loss": loss_lanes[:, :1]}
    (gw_up1,), (small_early,) = _wgrad(dab1, n1, 1.0, D_FF, tk, "wgrad_up1",
                                       _small_carry(_pack_small(small_parts, EARLY)))
    (dx, dg1), (got["wup1_t"],) = _ffn_bwd_input(dab1, dh1, xs, g1, big["wup1_t"], tm_f, _scatter_carry([gw_up1]))
    (small_late,) = _exchange(_small_carry(_pack_small({"ffn1_norm": dg1}, LATE)), "gather_small")

    grad, delta, new_m, new_v = {}, {}, {}, {}
    for k, p, tr in BIG:
        outside = tr and args[p].shape[-1] % LANES != 0
        turn = (lambda a: a.T) if outside else (lambda a: a)
        res = _sum_adamw(got[k], turn(args[p][0]), turn(args["m_" + p][0]), turn(args["v_" + p][0]),
                         tr and not outside, "adamw_" + k)
        grad[p], delta[p], new_m[p], new_v[p] = (turn(a)[None] for a in res)

    shapes = {name: args[name].shape for name, _ in SMALL if name != "loss"}
    shapes["loss"] = ()
    packed = {pre: _pack_small({**{name: args[pre + name] for name, _ in SMALL if name != "loss"},
                                "loss": jnp.zeros((), F32)}) for pre in ("", "m_", "v_")}
    g_s, d_s, m_s, v_s = _small_update(small_early, small_late, packed[""], packed["m_"], packed["v_"])
    g_small, d_small, m_small, v_small = (_unpack_small(a, shapes) for a in (g_s, d_s, m_s, v_s))
    for name, _ in SMALL:
        if name != "loss":
            grad[name], delta[name], new_m[name], new_v[name] = (
                g_small[name], d_small[name], m_small[name], v_small[name])

    return (g_small["loss"], dx[None], *[grad[n] for n in weight_names], *[delta[n] for n in weight_names],
            *[new_m[n] for n in weight_names], *[new_v[n] for n in weight_names])
```

```python
import jax
import jax.numpy as jnp
import numpy as np
from jax import lax
from jax.experimental import pallas as pl
from jax.experimental.pallas import tpu as pltpu

F32 = jnp.float32
BF16 = jnp.bfloat16

D_MODEL = 1024
D_FF = 2816
N_Q_HEADS = 16
N_KV_HEADS = 2
Q_PER_KV = N_Q_HEADS // N_KV_HEADS
HEAD_DIM = 64
BLOCK = 128
ATTN_WIDTH = N_Q_HEADS * HEAD_DIM
KV_WIDTH = N_KV_HEADS * HEAD_DIM
POOL_WINDOWS = (2, 4, 8, 16)
POOL_GROUP = 128
POOL_WIDTH = 512
HALO = 16
IN_WIDTH = ATTN_WIDTH + 2 * KV_WIDTH + POOL_WIDTH + 2 * D_MODEL
OFF_KV = ATTN_WIDTH
OFF_Z = ATTN_WIDTH + 2 * KV_WIDTH
OFF_GATE = OFF_Z + POOL_WIDTH
NORM_EPS = 1e-6
ADAM_LR = 0.001
ADAM_B1 = 0.9
ADAM_B2 = 0.999
ADAM_EPS = 1e-08
ADAM_WD = 0.01
ADAM_STEP = 10

N_DEV = 8
LANES = 128
FF_CHUNK = 256
SLAB = 32
GATHER_PIECES = 2
VMEM_LIMIT = 56 * 1024 * 1024
MESH = pl.DeviceIdType.MESH


def _nn(a, b):
    return jnp.dot(a, b, preferred_element_type=F32)


def _nt(a, b):
    return lax.dot_general(a, b, (((1,), (1,)), ((), ())), preferred_element_type=F32)


def _tn(a, b):
    return lax.dot_general(a, b, (((0,), (0,)), ((), ())), preferred_element_type=F32)


def _params(*sem):
    return pltpu.CompilerParams(dimension_semantics=sem, vmem_limit_bytes=VMEM_LIMIT)


def _resident(shape):
    return pl.BlockSpec(shape, lambda *_: (0,) * len(shape), pipeline_mode=pl.Buffered(1))


def _rows(tm, cols):
    return pl.BlockSpec((tm, cols), lambda i: (i, 0))


class _Carry:
    def __init__(self, inputs, out_shape, scratch, start, finish, middles=()):
        self.inputs, self.out_shape, self.scratch = list(inputs), list(out_shape), list(scratch)
        self.start, self.finish, self.middles = start, finish, list(middles)


def _launch(body, args, carry=None, *, name, grid, in_specs, out_specs, out_shape, scratch_shapes=(), semantics):
    in_specs, out_specs, out_shape, scratch_shapes = list(in_specs), list(out_specs), list(out_shape), list(scratch_shapes)
    if carry is None:
        res = pl.pallas_call(body, name=name, grid=grid, in_specs=in_specs, out_specs=out_specs, out_shape=out_shape,
                             scratch_shapes=scratch_shapes, compiler_params=_params(*semantics))(*args)
        return list(res), []
    ni, no, ns = len(in_specs), len(out_specs), len(scratch_shapes)
    ci, co = len(carry.inputs), len(carry.out_shape)
    total = int(np.prod(grid))

    def full(*refs):
        own_in, c_in = refs[:ni], refs[ni:ni + ci]
        own_out, c_out = refs[ni + ci:ni + ci + no], refs[ni + ci + no:ni + ci + no + co]
        own_scr, c_sem = refs[ni + ci + no + co:ni + ci + no + co + ns], refs[ni + ci + no + co + ns:]
        step = 0
        for axis, size in enumerate(grid):
            step = step * size + pl.program_id(axis)
        pl.when(step == 0)(lambda: carry.start(c_in, c_out, c_sem))
        for fraction, fn in carry.middles:
            at = min(total - 1, int(fraction * total) + 1)
            pl.when(step == at)(lambda fn=fn: fn(c_in, c_out, c_sem))
        body(*own_in, *own_out, *own_scr)
        pl.when(step == total - 1)(lambda: carry.finish(c_in, c_out, c_sem))

    res = pl.pallas_call(
        full, name=name, grid=grid, in_specs=in_specs + [ANY] * ci, out_specs=out_specs + [ANY] * co,
        out_shape=out_shape + carry.out_shape, scratch_shapes=scratch_shapes + carry.scratch,
        compiler_params=_params(*(["arbitrary"] * len(grid))),
    )(*args, *carry.inputs)
    return list(res[:no]), list(res[no:])


def _rms_fwd(xv, g):
    r = lax.rsqrt(jnp.mean(xv * xv, axis=-1, keepdims=True) + NORM_EPS)
    return xv * r, r


def _rms_bwd(dn, xh, r, g):
    dxh = dn * g
    dx = r * (dxh - xh * jnp.mean(dxh * xh, axis=-1, keepdims=True))
    return dx, jnp.sum(dn * xh, axis=0, keepdims=True)


def _ffn_loss(x, g, wup_t, wdown, gf, target, tm):
    t, d = x.shape
    f = wdown.shape[0]

    def body(x_ref, g_ref, wup_ref, wdn_ref, gf_ref, tgt_ref, ab_ref, n_ref, act_ref, loss_ref, dh_ref, dhb_ref, dg_ref):
        xv = x_ref[...]
        xh, _ = _rms_fwd(xv, g_ref[...])
        n = (xh * g_ref[...]).astype(BF16)
        n_ref[...] = n
        for c in range(f // FF_CHUNK):
            lo, hi = c * FF_CHUNK, (c + 1) * FF_CHUNK
            a = _nt(n, wup_ref[lo:hi, :])
            b = _nt(n, wup_ref[f + lo:f + hi, :])
            ab_ref[:, lo:hi] = a.astype(BF16)
            ab_ref[:, f + lo:f + hi] = b.astype(BF16)
            act_ref[:, lo:hi] = (a * jax.nn.sigmoid(a) * b).astype(BF16)
        h = xv + 0.5 * _nn(act_ref[...], wdn_ref[...])
        yh, r = _rms_fwd(h, gf_ref[...])
        err = yh * gf_ref[...] - tgt_ref[...]
        part = 0.5 * jnp.sum(jnp.mean(err * err, axis=-1, keepdims=True), axis=0, keepdims=True)
        dh, dg = _rms_bwd(err * (1.0 / d), yh, r, gf_ref[...])
        dh_ref[...] = dh
        dhb_ref[...] = dh.astype(BF16)

        @pl.when(pl.program_id(0) == 0)
        def _():
            dg_ref[...] = jnp.zeros_like(dg_ref)
            loss_ref[...] = jnp.zeros_like(loss_ref)

        dg_ref[...] += dg
        loss_ref[...] += jnp.broadcast_to(part, loss_ref.shape)

    return pl.pallas_call(
        body, name="ffn_loss", grid=(t // tm,),
        in_specs=[_rows(tm, d), _resident((1, d)), _resident((2 * f, d)), _resident((f, d)), _resident((1, d)),
                  _rows(tm, d)],
        out_specs=[_rows(tm, 2 * f), _rows(tm, d), _rows(tm, f), pl.BlockSpec((1, LANES), lambda i: (0, 0)),
                   _rows(tm, d), _rows(tm, d), pl.BlockSpec((1, d), lambda i: (0, 0))],
        out_shape=[jax.ShapeDtypeStruct((t, 2 * f), BF16), jax.ShapeDtypeStruct((t, d), BF16),
                   jax.ShapeDtypeStruct((t, f), BF16), jax.ShapeDtypeStruct((1, LANES), F32),
                   jax.ShapeDtypeStruct((t, d), F32), jax.ShapeDtypeStruct((t, d), BF16),
                   jax.ShapeDtypeStruct((1, d), F32)],
        compiler_params=_params("arbitrary"),
    )(x, g, wup_t, wdown, gf, target)


def _ffn_up(x, g, wup_t, tm, carry=None):
    t, d = x.shape
    f = wup_t.shape[0] // 2

    def body(x_ref, g_ref, wup_ref, n_ref, ab_ref, act_ref):
        xh, _ = _rms_fwd(x_ref[...], g_ref[...])
        n = (xh * g_ref[...]).astype(BF16)
        n_ref[...] = n
        for c in range(f // FF_CHUNK):
            lo, hi = c * FF_CHUNK, (c + 1) * FF_CHUNK
            a = _nt(n, wup_ref[lo:hi, :])
            b = _nt(n, wup_ref[f + lo:f + hi, :])
            ab_ref[:, lo:hi] = a.astype(BF16)
            ab_ref[:, f + lo:f + hi] = b.astype(BF16)
            act_ref[:, lo:hi] = (a * jax.nn.sigmoid(a) * b).astype(BF16)

    return _launch(
        body, (x, g, wup_t), carry, name="ffn_up", grid=(t // tm,),
        in_specs=[_rows(tm, d), _resident((1, d)), _resident((2 * f, d))],
        out_specs=[_rows(tm, d), _rows(tm, 2 * f), _rows(tm, f)],
        out_shape=[jax.ShapeDtypeStruct((t, d), BF16), jax.ShapeDtypeStruct((t, 2 * f), BF16),
                   jax.ShapeDtypeStruct((t, f), BF16)],
        semantics=("parallel",))


def _ffn_down(x, act, wdown, tm, carry=None):
    t, d = x.shape
    f = wdown.shape[0]

    def body(x_ref, act_ref, wdn_ref, h_ref):
        h_ref[...] = x_ref[...] + 0.5 * _nn(act_ref[...], wdn_ref[...])

    return _launch(
        body, (x, act, wdown), carry, name="ffn_down", grid=(t // tm,),
        in_specs=[_rows(tm, d), _rows(tm, f), _resident((f, d))], out_specs=[_rows(tm, d)],
        out_shape=[jax.ShapeDtypeStruct((t, d), F32)], semantics=("parallel",))


def _ffn_bwd_hidden(dhb, ab, wdown, tm, carry=None):
    t, d = dhb.shape
    f = wdown.shape[0]

    def body(dh_ref, ab_ref, wdn_ref, dab_ref, dact_ref):
        half = dh_ref[...] * 0.5
        for c in range(f // FF_CHUNK):
            lo, hi = c * FF_CHUNK, (c + 1) * FF_CHUNK
            dact_ref[...] = _nt(half, wdn_ref[lo:hi, :])

            def slab(i, carry_):
                rows = pl.ds(pl.multiple_of(i * SLAB, SLAB), SLAB)
                a = ab_ref[rows, lo:hi].astype(F32)
                b = ab_ref[rows, f + lo:f + hi].astype(F32)
                s = jax.nn.sigmoid(a)
                ds_ = dact_ref[rows, :] * s
                dab_ref[rows, lo:hi] = (ds_ * b * (1.0 + a * (1.0 - s))).astype(BF16)
                dab_ref[rows, f + lo:f + hi] = (ds_ * a).astype(BF16)
                return carry_

            lax.fori_loop(0, tm // SLAB, slab, 0, unroll=True)

    return _launch(
        body, (dhb, ab, wdown), carry, name="ffn_bwd_hidden", grid=(t // tm,),
        in_specs=[_rows(tm, d), _rows(tm, 2 * f), _resident((f, d))], out_specs=[_rows(tm, 2 * f)],
        out_shape=[jax.ShapeDtypeStruct((t, 2 * f), BF16)],
        scratch_shapes=[pltpu.VMEM((tm, FF_CHUNK), F32)], semantics=("parallel",))


def _ffn_bwd_input(dab, dh, x, g, wup_t, tm, carry=None):
    t, d = x.shape
    f2 = wup_t.shape[0]

    def body(dab_ref, dh_ref, x_ref, g_ref, wup_ref, dx_ref, dg_ref):
        dn = _nn(dab_ref[...], wup_ref[...])
        xh, r = _rms_fwd(x_ref[...], g_ref[...])
        dx, dg = _rms_bwd(dn, xh, r, g_ref[...])
        dx_ref[...] = dh_ref[...] + dx

        @pl.when(pl.program_id(0) == 0)
        def _():
            dg_ref[...] = jnp.zeros_like(dg_ref)

        dg_ref[...] += dg

    return _launch(
        body, (dab, dh, x, g, wup_t), carry, name="ffn_bwd_input", grid=(t // tm,),
        in_specs=[_rows(tm, f2), _rows(tm, d), _rows(tm, d), _resident((1, d)), _resident((f2, d))],
        out_specs=[_rows(tm, d), pl.BlockSpec((1, d), lambda i: (0, 0))],
        out_shape=[jax.ShapeDtypeStruct((t, d), F32), jax.ShapeDtypeStruct((1, d), F32)],
        semantics=("arbitrary",))


def _wgrad(lhs, rhs, scale, bm, tk, name, carry=None):
    t, m = lhs.shape
    n = rhs.shape[1]
    steps = t // tk
    chunk = bm if bm <= 2048 else bm // 2

    def body(l_ref, r_ref, o_ref, acc_ref):
        @pl.when(pl.program_id(1) == 0)
        def _():
            acc_ref[...] = jnp.zeros_like(acc_ref)

        for lo in range(0, bm, chunk):
            acc_ref[lo:lo + chunk, :] += _tn(l_ref[:, lo:lo + chunk], r_ref[...])

        @pl.when(pl.program_id(1) == steps - 1)
        def _():
            o_ref[...] = (scale * acc_ref[...]).astype(o_ref.dtype)

    return _launch(
        body, (lhs, rhs), carry, name=name, grid=(m // bm, steps),
        in_specs=[pl.BlockSpec((tk, bm), lambda i, k: (k, i)), pl.BlockSpec((tk, n), lambda i, k: (k, 0))],
        out_specs=[pl.BlockSpec((bm, n), lambda i, k: (i, 0))],
        out_shape=[jax.ShapeDtypeStruct((m, n), WIRE)],
        scratch_shapes=[pltpu.VMEM((bm, n), F32)], semantics=("parallel", "arbitrary"))


def _mix_in_fwd(h, g, win_t, tm, carry=None):
    t, d = h.shape

    def body(h_ref, g_ref, w_ref, u_ref, q_ref, kv_ref, z_ref, gate_ref):
        xh, _ = _rms_fwd(h_ref[...], g_ref[...])
        u = (xh * g_ref[...]).astype(BF16)
        u_ref[...] = u
        q_ref[...] = _nt(u, w_ref[0:OFF_KV, :]).astype(BF16)
        kv_ref[...] = _nt(u, w_ref[OFF_KV:OFF_Z, :]).astype(BF16)
        z_ref[...] = _nt(u, w_ref[OFF_Z:OFF_GATE, :])
        gate_ref[...] = _nt(u, w_ref[OFF_GATE:IN_WIDTH, :]).astype(BF16)

    return _launch(
        body, (h, g, win_t), carry, name="mix_in_fwd", grid=(t // tm,),
        in_specs=[_rows(tm, d), _resident((1, d)), _resident((IN_WIDTH, d))],
        out_specs=[_rows(tm, d), _rows(tm, ATTN_WIDTH), _rows(tm, 2 * KV_WIDTH), _rows(tm, POOL_WIDTH),
                   _rows(tm, 2 * D_MODEL)],
        out_shape=[jax.ShapeDtypeStruct((t, d), BF16), jax.ShapeDtypeStruct((t, ATTN_WIDTH), BF16),
                   jax.ShapeDtypeStruct((t, 2 * KV_WIDTH), BF16), jax.ShapeDtypeStruct((t, POOL_WIDTH), F32),
                   jax.ShapeDtypeStruct((t, 2 * D_MODEL), BF16)],
        semantics=("parallel",))


ALIBI_SLOPES = tuple(float(s) for s in (2.0 ** (-8.0 * np.arange(1, N_Q_HEADS + 1, dtype=np.float32) / N_Q_HEADS)))


def _attn_dist():
    return jnp.asarray(((np.arange(BLOCK)[:, None] - np.arange(BLOCK)[None, :]) % BLOCK).astype(np.float32))


def _own_block():
    shape = (BLOCK, BLOCK)
    return lax.broadcasted_iota(jnp.int32, shape, 1) <= lax.broadcasted_iota(jnp.int32, shape, 0)


def _fold(band2, own):
    return jnp.where(own, band2[:, BLOCK:], band2[:, :BLOCK])


def _unfold(x, own):
    zero = jnp.zeros_like(x)
    return jnp.concatenate([jnp.where(own, zero, x), jnp.where(own, x, zero)], axis=1)


def _low_half(shape):
    return lax.broadcasted_iota(jnp.int32, shape, len(shape) - 1) < HEAD_DIM


def _both_halves(band, kv_head):
    low = _low_half(band.shape)
    swapped = pltpu.roll(band, HEAD_DIM, 1)
    return jnp.where(low, band, swapped) if kv_head == 0 else jnp.where(low, swapped, band)


def _pair_rows(ref, rows, pair, scale=None):
    v = ref[rows, LANES * pair:LANES * (pair + 1)]
    if scale is not None:
        v = v * scale
    low, zero = _low_half(v.shape), jnp.zeros_like(v)
    return jnp.concatenate([jnp.where(low, v, zero), jnp.where(low, zero, v)], axis=0)


def _per_head(even, odd):
    return jnp.where(lax.broadcasted_iota(jnp.int32, (2 * BLOCK, 1), 0) < BLOCK, even, odd)


def _twice(x):
    return jnp.concatenate([x, x], axis=0)


def _pair_scores(q_ref, rows, kk, dist2, pair, first, own2):
    s2 = _nt(_pair_rows(q_ref, rows, pair, HEAD_DIM ** -0.5), kk)
    before = jnp.where(first, -jnp.inf, s2[:, :BLOCK])
    slopes = _per_head(ALIBI_SLOPES[2 * pair], ALIBI_SLOPES[2 * pair + 1])
    return jnp.where(own2, s2[:, BLOCK:], before) - slopes * dist2


def _own_half(ref, rows, head):
    v = ref[rows, LANES * (head // 2):LANES * (head // 2 + 1)]
    low = _low_half(v.shape)
    return jnp.where(low if head % 2 == 0 else jnp.logical_not(low), v, jnp.zeros_like(v))


def _head_scores(q_ref, rows, kk, dist, head, first, own):
    s2 = _nt(_own_half(q_ref, rows, head) * HEAD_DIM ** -0.5, kk)
    before = jnp.where(first, -jnp.inf, s2[:, :BLOCK])
    return jnp.where(own, s2[:, BLOCK:], before) - ALIBI_SLOPES[head] * dist


def _heads_of(stack):
    return jnp.where(_low_half((BLOCK, LANES)), stack[:BLOCK], stack[BLOCK:])


def _softmax_sink(s, sink):
    m = jnp.maximum(jnp.max(s, axis=-1, keepdims=True), sink)
    p = jnp.exp(s - m)
    psink = jnp.exp(sink - m)
    inv = 1.0 / (jnp.sum(p, axis=-1, keepdims=True) + psink)
    return p * inv, psink * inv


def _bands(kvc_ref, kvp_ref, sub):
    own = slice(sub * BLOCK, (sub + 1) * BLOCK)
    before = kvp_ref[...] if sub == 0 else kvc_ref[(sub - 1) * BLOCK:sub * BLOCK, :]
    kband = jnp.concatenate([before[:, 0:LANES], kvc_ref[own, 0:LANES]], axis=0)
    vband = jnp.concatenate([before[:, LANES:2 * LANES], kvc_ref[own, LANES:2 * LANES]], axis=0)
    return ([_both_halves(kband, hk) for hk in range(N_KV_HEADS)],
            [_both_halves(vband, hk) for hk in range(N_KV_HEADS)])


SMEM = pl.BlockSpec(memory_space=pltpu.SMEM)
HEADS = range(N_Q_HEADS)
PAIRS = range(N_Q_HEADS // 2)
PAIRS_PER_KV = Q_PER_KV // 2
def _sub_rows(sub):
    return slice(sub * BLOCK, (sub + 1) * BLOCK)


def _block_before(step):
    per = step // BLOCK
    return pl.BlockSpec((BLOCK, 2 * KV_WIDTH), lambda i: (jnp.maximum(i * per - 1, 0), 0))


def _attn_fwd(q, kv, dist, sinks):
    t = q.shape[0]
    subs = range(2)

    def body(q_ref, kvc_ref, kvp_ref, dist_ref, sink_ref, o_ref, s_scr, p_scr):
        own2 = _twice(_own_block())
        dist2 = _twice(dist_ref[...])
        bands = [_bands(kvc_ref, kvp_ref, sub) for sub in subs]
        for sub in subs:
            first = jnp.logical_and(pl.program_id(0) == 0, sub == 0)
            for pair in PAIRS:
                s_scr[sub, pair] = _pair_scores(q_ref, _sub_rows(sub), bands[sub][0][pair // PAIRS_PER_KV], dist2, pair,
                                                first, own2)
        for sub in subs:
            for pair in PAIRS:
                probs, _ = _softmax_sink(s_scr[sub, pair], _per_head(sink_ref[2 * pair], sink_ref[2 * pair + 1]))
                p_scr[sub, pair] = _unfold(probs.astype(BF16), own2)
        for sub in subs:
            for pair in PAIRS:
                out = _nn(p_scr[sub, pair], bands[sub][1][pair // PAIRS_PER_KV])
                o_ref[_sub_rows(sub), LANES * pair:LANES * (pair + 1)] = _heads_of(out).astype(BF16)

    step = len(subs) * BLOCK
    return pl.pallas_call(
        body, name="attn_fwd", grid=(t // step,),
        in_specs=[_rows(step, ATTN_WIDTH), _rows(step, 2 * KV_WIDTH), _block_before(step), _resident(dist.shape), SMEM],
        out_specs=_rows(step, ATTN_WIDTH),
        out_shape=jax.ShapeDtypeStruct((t, ATTN_WIDTH), BF16),
        scratch_shapes=[pltpu.VMEM((len(subs), len(PAIRS), 2 * BLOCK, BLOCK), F32),
                        pltpu.VMEM((len(subs), len(PAIRS), 2 * BLOCK, 2 * BLOCK), BF16)],
        compiler_params=_params("parallel"),
    )(q, kv, kv, dist, sinks)


def _pool_counts(tm, width):
    row = pl.program_id(0) * tm + lax.broadcasted_iota(jnp.int32, (tm, 1), 0)
    return jnp.minimum(row + 1, width).astype(F32)


def _trailing_sums(zz, group):
    s = zz
    for k in range(group + 1):
        s = s + pltpu.roll(s, 1 << k, 0)
    return s


def _leading_sums(zz, group):
    rows = zz.shape[0]
    s = zz
    for k in range(group + 1):
        s = s + pltpu.roll(s, rows - (1 << k), 0)
    return s


def _mix_out_fwd(attn, z, gate, h, wattn, wmix, scale, wpool_t, wout, tm, carry=None):
    t, d = h.shape

    def body(attn_ref, z_ref, halo_ref, gate_ref, h_ref, wattn_ref, wmix_ref, scale_ref, wpool_ref, wout_ref,
             h2_ref, a_ref, p_ref, merged_ref, ms_ref, pooled_ref):
        halo = jnp.where(pl.program_id(0) == 0, 0.0, halo_ref[...])
        for gi, width in enumerate(POOL_WINDOWS):
            lo, hi = gi * POOL_GROUP, (gi + 1) * POOL_GROUP
            zg = z_ref[:, lo:hi]
            sums = _trailing_sums(jnp.concatenate([halo[:, lo:hi], zg], axis=0), gi)[HALO:, :]
            pooled = (sums / _pool_counts(tm, width) - zg).astype(BF16)
            pooled_ref[:, lo:hi] = pooled
            ms_ref[:, lo:hi] = (_nn(pooled, wmix_ref[gi]) * scale_ref[:, lo:hi]).astype(BF16)
        p = _nt(ms_ref[...], wpool_ref[...])
        a = _nn(attn_ref[...], wattn_ref[...])
        a_ref[...] = a.astype(BF16)
        p_ref[...] = p.astype(BF16)
        merged = (jax.nn.sigmoid(gate_ref[:, 0:d].astype(F32)) * a
                  + jax.nn.sigmoid(gate_ref[:, d:2 * d].astype(F32)) * p).astype(BF16)
        merged_ref[...] = merged
        h2_ref[...] = h_ref[...] + _nn(merged, wout_ref[...])

    halo_spec = pl.BlockSpec((HALO, POOL_WIDTH), lambda i: (jnp.maximum(i * (tm // HALO) - 1, 0), 0))
    return _launch(
        body, (attn, z, z, gate, h, wattn, wmix, scale, wpool_t, wout), carry, name="mix_out_fwd", grid=(t // tm,),
        in_specs=[_rows(tm, ATTN_WIDTH), _rows(tm, POOL_WIDTH), halo_spec, _rows(tm, 2 * d), _rows(tm, d),
                  _resident(wattn.shape), _resident(wmix.shape), _resident(scale.shape), _resident(wpool_t.shape),
                  _resident(wout.shape)],
        out_specs=[_rows(tm, d), _rows(tm, d), _rows(tm, d), _rows(tm, d), _rows(tm, POOL_WIDTH),
                   _rows(tm, POOL_WIDTH)],
        out_shape=[jax.ShapeDtypeStruct((t, d), F32), jax.ShapeDtypeStruct((t, d), BF16),
                   jax.ShapeDtypeStruct((t, d), BF16), jax.ShapeDtypeStruct((t, d), BF16),
                   jax.ShapeDtypeStruct((t, POOL_WIDTH), BF16), jax.ShapeDtypeStruct((t, POOL_WIDTH), BF16)],
        semantics=("parallel",))


def _mix_out_bwd(dh, gate, a, p, pooled, wattn, wmix, scale, wpool_t, wout, tm):
    t, d = dh.shape

    def body(dh_ref, gate_ref, a_ref, p_ref, pooled_ref, wattn_ref, wmix_ref, scale_ref, wpool_ref, wout_ref,
             dhb_ref, dab_ref, dpb_ref, dattn_ref, dgate_ref, dpooled_ref, dwmix_ref, dscale_ref):
        @pl.when(pl.program_id(0) == 0)
        def _():
            dwmix_ref[...] = jnp.zeros_like(dwmix_ref)
            dscale_ref[...] = jnp.zeros_like(dscale_ref)

        dhb = dh_ref[...].astype(BF16)
        dhb_ref[...] = dhb
        dm = _nt(dhb, wout_ref[...])
        sa = jax.nn.sigmoid(gate_ref[:, 0:d].astype(F32))
        sp = jax.nn.sigmoid(gate_ref[:, d:2 * d].astype(F32))
        da = (dm * sa).astype(BF16)
        dp = (dm * sp).astype(BF16)
        dab_ref[...] = da
        dpb_ref[...] = dp
        dgate_ref[:, 0:d] = (dm * a_ref[...].astype(F32) * (sa * (1.0 - sa))).astype(BF16)
        dgate_ref[:, d:2 * d] = (dm * p_ref[...].astype(F32) * (sp * (1.0 - sp))).astype(BF16)
        dattn_ref[...] = _nt(da, wattn_ref[...]).astype(BF16)
        dms = _nn(dp, wpool_ref[...])
        for gi in range(len(POOL_WINDOWS)):
            lo, hi = gi * POOL_GROUP, (gi + 1) * POOL_GROUP
            pooled_g = pooled_ref[:, lo:hi]
            mixed = _nn(pooled_g, wmix_ref[gi])
            dscale_ref[:, lo:hi] += jnp.sum(dms[:, lo:hi] * mixed, axis=0, keepdims=True)
            dmixed = (dms[:, lo:hi] * scale_ref[:, lo:hi]).astype(BF16)
            dwmix_ref[gi] += _tn(pooled_g, dmixed)
            dpooled_ref[:, lo:hi] = _nt(dmixed, wmix_ref[gi])

    acc = lambda shape: pl.BlockSpec(shape, lambda i: (0,) * len(shape))
    return pl.pallas_call(
        body, name="mix_out_bwd", grid=(t // tm,),
        in_specs=[_rows(tm, d), _rows(tm, 2 * d), _rows(tm, d), _rows(tm, d), _rows(tm, POOL_WIDTH),
                  _resident(wattn.shape), _resident(wmix.shape), _resident(scale.shape), _resident(wpool_t.shape),
                  _resident(wout.shape)],
        out_specs=[_rows(tm, d), _rows(tm, d), _rows(tm, d), _rows(tm, ATTN_WIDTH), _rows(tm, 2 * d),
                   _rows(tm, POOL_WIDTH), acc(wmix.shape), acc((1, POOL_WIDTH))],
        out_shape=[jax.ShapeDtypeStruct((t, d), BF16), jax.ShapeDtypeStruct((t, d), BF16),
                   jax.ShapeDtypeStruct((t, d), BF16), jax.ShapeDtypeStruct((t, ATTN_WIDTH), BF16),
                   jax.ShapeDtypeStruct((t, 2 * d), BF16), jax.ShapeDtypeStruct((t, POOL_WIDTH), F32),
                   jax.ShapeDtypeStruct(wmix.shape, F32), jax.ShapeDtypeStruct((1, POOL_WIDTH), F32)],
        compiler_params=_params("arbitrary"),
    )(dh, gate, a, p, pooled, wattn, wmix, scale, wpool_t, wout)


def _fold_halves(x):
    return x + pltpu.roll(x, HEAD_DIM, 1)


def _attn_bwd(q, kv, dattn, dist, sinks, carry=None):
    t = q.shape[0]
    subs = range(1)

    def body(q_ref, kvc_ref, kvp_ref, do_ref, dist_ref, sink_ref, dq_ref, dkv_own_ref, dkv_prev_ref, dsink_ref,
             s_scr, dp_scr, p_scr, ds_scr):
        @pl.when(pl.program_id(0) == 0)
        def _():
            dsink_ref[...] = jnp.zeros_like(dsink_ref)

        own = _own_block()
        dist_v = dist_ref[...]
        bands = [_bands(kvc_ref, kvp_ref, sub) for sub in subs]
        lane = lax.broadcasted_iota(jnp.int32, (1, LANES), 1)
        for sub in subs:
            first = jnp.logical_and(pl.program_id(0) == 0, sub == 0)
            for head in HEADS:
                hk = head // Q_PER_KV
                s_scr[sub, head] = _head_scores(q_ref, _sub_rows(sub), bands[sub][0][hk], dist_v, head, first, own)
                dp_scr[sub, head] = _fold(_nt(_own_half(do_ref, _sub_rows(sub), head), bands[sub][1][hk]), own)
        dsink = jnp.zeros((1, LANES), F32)
        for sub in subs:
            for head in HEADS:
                probs, psink = _softmax_sink(s_scr[sub, head], sink_ref[head])
                dprobs = dp_scr[sub, head]
                rowdot = jnp.sum(probs * dprobs, axis=-1, keepdims=True)
                p_scr[sub, head] = _unfold(probs.astype(BF16), own)
                ds_scr[sub, head] = _unfold((probs * (dprobs - rowdot)).astype(BF16), own)
                dsink = dsink + jnp.where(lane == head, jnp.sum(-psink * rowdot, axis=0, keepdims=True), 0.0)
        for sub in subs:
            rows = _sub_rows(sub)
            dk_heads, dv_heads = [], []
            for hk in range(N_KV_HEADS):
                dk_t = jnp.zeros((LANES, 2 * BLOCK), F32)
                dv_t = jnp.zeros((LANES, 2 * BLOCK), F32)
                for pair in range(Q_PER_KV // 2):
                    cols = slice(LANES * (hk * PAIRS_PER_KV + pair), LANES * (hk * PAIRS_PER_KV + pair + 1))
                    q_t = (q_ref[rows, cols] * HEAD_DIM ** -0.5).T
                    do_t = do_ref[rows, cols].T
                    dqs = []
                    for head in (hk * Q_PER_KV + 2 * pair, hk * Q_PER_KV + 2 * pair + 1):
                        mine = (lax.broadcasted_iota(jnp.int32, q_t.shape, 0) < HEAD_DIM) == (head % 2 == 0)
                        dv_t = dv_t + _nn(jnp.where(mine, do_t, jnp.zeros_like(do_t)), p_scr[sub, head])
                        dk_t = dk_t + _nn(jnp.where(mine, q_t, jnp.zeros_like(q_t)), ds_scr[sub, head])
                        dqs.append(_nn(ds_scr[sub, head], bands[sub][0][hk]))
                    dq_pair = jnp.where(_low_half(dqs[0].shape), dqs[0], dqs[1])
                    dq_ref[rows, cols] = (dq_pair * HEAD_DIM ** -0.5).astype(BF16)
                dk_heads.append(_fold_halves(dk_t.T))
                dv_heads.append(_fold_halves(dv_t.T))
            low = _low_half(dk_heads[0].shape)
            dkv = jnp.concatenate([jnp.where(low, dk_heads[0], dk_heads[1]), jnp.where(low, dv_heads[0], dv_heads[1])],
                                  axis=1)
            dkv_prev_ref[rows, :] = dkv[0:BLOCK, :]
            dkv_own_ref[rows, :] = dkv[BLOCK:2 * BLOCK, :]
        dsink_ref[...] += dsink

    step = len(subs) * BLOCK
    return _launch(
        body, (q, kv, kv, dattn, dist, sinks), carry, name="attn_bwd", grid=(t // step,),
        in_specs=[_rows(step, ATTN_WIDTH), _rows(step, 2 * KV_WIDTH), _block_before(step), _rows(step, ATTN_WIDTH),
                  _resident(dist.shape), SMEM],
        out_specs=[_rows(step, ATTN_WIDTH), _rows(step, 2 * KV_WIDTH), _rows(step, 2 * KV_WIDTH),
                   pl.BlockSpec((1, LANES), lambda i: (0, 0))],
        out_shape=[jax.ShapeDtypeStruct((t, ATTN_WIDTH), BF16), jax.ShapeDtypeStruct((t, 2 * KV_WIDTH), F32),
                   jax.ShapeDtypeStruct((t, 2 * KV_WIDTH), F32), jax.ShapeDtypeStruct((1, LANES), F32)],
        scratch_shapes=[pltpu.VMEM((len(subs), N_Q_HEADS, BLOCK, BLOCK), F32),
                        pltpu.VMEM((len(subs), N_Q_HEADS, BLOCK, BLOCK), F32),
                        pltpu.VMEM((len(subs), N_Q_HEADS, BLOCK, 2 * BLOCK), BF16),
                        pltpu.VMEM((len(subs), N_Q_HEADS, BLOCK, 2 * BLOCK), BF16)],
        semantics=("arbitrary",))


def _mix_in_bwd(dq, dkv_own, dkv_prev, dpooled, dgate, h, g, win_t, dh_res, tm, carry=None):
    t, d = h.shape
    nt = t // tm

    def body(dq_ref, own_ref, prev_ref, prev_next_ref, dpool_ref, halo_ref, dgate_ref, h_ref, g_ref, w_ref, res_ref,
             dproj_ref, dh_ref, dhb_ref, dg_ref):
        i = pl.program_id(0)
        last = i == nt - 1
        dproj_ref[:, 0:OFF_KV] = dq_ref[...]
        from_next = jnp.where(last, 0.0, prev_next_ref[...])
        if tm > BLOCK:
            from_next = jnp.concatenate([prev_ref[BLOCK:tm, :], from_next], axis=0)
        dproj_ref[:, OFF_KV:OFF_Z] = (own_ref[...] + from_next).astype(BF16)
        halo = jnp.where(last, 0.0, halo_ref[...])
        for gi, width in enumerate(POOL_WINDOWS):
            lo, hi = gi * POOL_GROUP, (gi + 1) * POOL_GROUP
            dpg = dpool_ref[:, lo:hi]
            scaled = jnp.concatenate([dpg / _pool_counts(tm, width), halo[:, lo:hi] / float(width)], axis=0)
            dz = _leading_sums(scaled, gi)[0:tm, :] - dpg
            dproj_ref[:, OFF_Z + lo:OFF_Z + hi] = dz.astype(BF16)
        dproj_ref[:, OFF_GATE:IN_WIDTH] = dgate_ref[...]
        du = _nn(dproj_ref[...], w_ref[...])
        xh, r = _rms_fwd(h_ref[...], g_ref[...])
        dx, dg = _rms_bwd(du, xh, r, g_ref[...])
        dh = res_ref[...] + dx
        dh_ref[...] = dh
        dhb_ref[...] = dh.astype(BF16)

        @pl.when(i == 0)
        def _():
            dg_ref[...] = jnp.zeros_like(dg_ref)

        dg_ref[...] += dg

    per = tm // BLOCK
    next_block = pl.BlockSpec((BLOCK, 2 * KV_WIDTH), lambda i: (jnp.minimum((i + 1) * per, t // BLOCK - 1), 0))
    next_halo = pl.BlockSpec((HALO, POOL_WIDTH), lambda i: (jnp.minimum((i + 1) * (tm // HALO), t // HALO - 1), 0))
    return _launch(
        body, (dq, dkv_own, dkv_prev, dkv_prev, dpooled, dpooled, dgate, h, g, win_t, dh_res), carry,
        name="mix_in_bwd", grid=(nt,),
        in_specs=[_rows(tm, ATTN_WIDTH), _rows(tm, 2 * KV_WIDTH), _rows(tm, 2 * KV_WIDTH), next_block,
                  _rows(tm, POOL_WIDTH), next_halo, _rows(tm, 2 * d), _rows(tm, d), _resident((1, d)),
                  _resident((IN_WIDTH, d)), _rows(tm, d)],
        out_specs=[_rows(tm, IN_WIDTH), _rows(tm, d), _rows(tm, d), pl.BlockSpec((1, d), lambda i: (0, 0))],
        out_shape=[jax.ShapeDtypeStruct((t, IN_WIDTH), BF16), jax.ShapeDtypeStruct((t, d), F32),
                   jax.ShapeDtypeStruct((t, d), BF16), jax.ShapeDtypeStruct((1, d), F32)],
        semantics=("arbitrary",))


BIG = (("wup1_t", "ffn1_w_up", True), ("wdown1", "ffn1_w_down", False), ("win_t", "w_in", True),
       ("wattn", "w_attn_up", False), ("wpool_t", "w_pool_up", True), ("wout", "w_out", False),
       ("wup2_t", "ffn2_w_up", True), ("wdown2", "ffn2_w_down", False))
ANY = pl.BlockSpec(memory_space=pl.ANY)
WIRE = BF16


def _place():
    return lax.axis_index("x"), lax.axis_index("y"), lax.axis_index("c")


def _peer(k):
    x, y, c = _place()
    return x ^ (k >> 2), y ^ ((k >> 1) & 1), c ^ (k & 1)


def _index(px, py, pc):
    return 4 * px + 2 * py + pc


def _gather_carry(shards):
    n = len(shards) * GATHER_PIECES

    def tools(ins, outs, sems):
        send_sems, recv_sems, local_sems = sems
        x, y, c = _place()
        chips = [(1 - x, y), (x, 1 - y), (1 - x, 1 - y)]

        def piece(item):
            w, q = divmod(item, GATHER_PIECES)
            r = ins[w].shape[0]
            return w, r, q * (r // GATHER_PIECES), r // GATHER_PIECES

        def mine(item):
            w, _, first, size = piece(item)
            return ins[w].at[pl.ds(first, size), :]

        def rows(item, px, py, pc):
            w, r, first, size = piece(item)
            return outs[w].at[pl.ds(_index(px, py, pc) * r + first, size), :]

        def copy(item, k, block, to, src=None):
            return pltpu.make_async_remote_copy(
                src_ref=rows(item, *block) if src is None else src, dst_ref=rows(item, *block),
                send_sem=send_sems.at[item, k], recv_sem=recv_sems.at[item, k], device_id=to, device_id_type=MESH)

        def own(item):
            return ([pltpu.make_async_copy(mine(item), rows(item, x, y, c), local_sems.at[item]),
                     copy(item, 0, (x, y, c), (x, y, 1 - c), src=mine(item))]
                    + [copy(item, 1 + j, (x, y, c), (*chip, c), src=mine(item)) for j, chip in enumerate(chips)])

        def passed(item, j):
            return copy(item, 4 + j, (*chips[j], c), (x, y, 1 - c))

        return (x, y, c), chips, copy, own, passed

    def start(ins, outs, sems):
        _, _, _, own, _ = tools(ins, outs, sems)
        for item in range(n):
            for cp in own(item):
                cp.start()

    def forward(item):
        def run(ins, outs, sems):
            (x, y, c), chips, copy, _, passed = tools(ins, outs, sems)
            for j, chip in enumerate(chips):
                copy(item, 1 + j, (*chip, c), (x, y, c)).wait_recv()
                passed(item, j).start()
        return run

    sizes = np.cumsum([s.size / GATHER_PIECES for s in shards for _ in range(GATHER_PIECES)])
    middles = [(float(sizes[item] / sizes[-1]), forward(item)) for item in range(n)]

    def finish(ins, outs, sems):
        (x, y, c), chips, copy, own, passed = tools(ins, outs, sems)
        for item in range(n):
            copy(item, 0, (x, y, 1 - c), (x, y, c)).wait_recv()
            for j, chip in enumerate(chips):
                copy(item, 4 + j, (*chip, 1 - c), (x, y, c)).wait_recv()
        for item in range(n):
            local, *sent = own(item)
            for cp in sent + [passed(item, j) for j in range(len(chips))]:
                cp.wait_send()
            local.wait()

    return _Carry(
        shards, [jax.ShapeDtypeStruct((N_DEV * s.shape[0], s.shape[1]), s.dtype) for s in shards],
        [pltpu.SemaphoreType.DMA((n, N_DEV - 1)), pltpu.SemaphoreType.DMA((n, N_DEV - 1)),
         pltpu.SemaphoreType.DMA((n,))], start, finish, middles)


def _scatter_carry(grads):
    n = len(grads)

    def tools(ins, outs, sems):
        send_sems, recv_sems, local_sems = sems
        me = _index(*_place())

        def block(ref, dev):
            r = ref.shape[0] // N_DEV
            return ref.at[pl.ds(dev * r, r), :]

        def copy(w, k, landing):
            to = _peer(k)
            return pltpu.make_async_remote_copy(
                src_ref=block(ins[w], _index(*to)), dst_ref=block(outs[w], landing), send_sem=send_sems.at[w, k - 1],
                recv_sem=recv_sems.at[w, k - 1], device_id=to, device_id_type=MESH)

        def mine(w):
            return pltpu.make_async_copy(block(ins[w], me), block(outs[w], me), local_sems.at[w])

        return me, copy, mine

    def start(ins, outs, sems):
        me, copy, mine = tools(ins, outs, sems)
        for w in range(n):
            mine(w).start()
            for k in range(1, N_DEV):
                copy(w, k, me).start()

    def finish(ins, outs, sems):
        _, copy, mine = tools(ins, outs, sems)
        for w in range(n):
            for k in range(1, N_DEV):
                copy(w, k, _index(*_peer(k))).wait()
            mine(w).wait()

    return _Carry(
        grads, [jax.ShapeDtypeStruct(g.shape, g.dtype) for g in grads],
        [pltpu.SemaphoreType.DMA((n, N_DEV - 1)), pltpu.SemaphoreType.DMA((n, N_DEV - 1)),
         pltpu.SemaphoreType.DMA((n,))], start, finish)


def _small_carry(small):
    srows = small.shape[0]

    def tools(ins, outs, sems):
        send_sems, recv_sems, local_sem = sems
        me = _index(*_place())

        def slot(dev):
            return outs[0].at[pl.ds(dev * srows, srows), :]

        def copy(k, landing):
            return pltpu.make_async_remote_copy(
                src_ref=ins[0], dst_ref=slot(landing), send_sem=send_sems.at[k - 1], recv_sem=recv_sems.at[k - 1],
                device_id=_peer(k), device_id_type=MESH)

        return me, copy, pltpu.make_async_copy(ins[0], slot(me), local_sem)

    def start(ins, outs, sems):
        me, copy, mine = tools(ins, outs, sems)
        mine.start()
        for k in range(1, N_DEV):
            copy(k, me).start()

    def finish(ins, outs, sems):
        _, copy, mine = tools(ins, outs, sems)
        for k in range(1, N_DEV):
            copy(k, _index(*_peer(k))).wait()
        mine.wait()

    return _Carry([small], [jax.ShapeDtypeStruct((N_DEV * srows, LANES), small.dtype)],
                  [pltpu.SemaphoreType.DMA((N_DEV - 1,)), pltpu.SemaphoreType.DMA((N_DEV - 1,)),
                   pltpu.SemaphoreType.DMA], start, finish)


def _exchange(carry, name):
    ci = len(carry.inputs)
    co = len(carry.out_shape)

    def body(*refs):
        parts = refs[:ci], refs[ci:ci + co], refs[ci + co:]
        carry.start(*parts)
        for _, fn in carry.middles:
            fn(*parts)
        carry.finish(*parts)

    return list(pl.pallas_call(body, name=name, in_specs=[ANY] * ci, out_specs=[ANY] * co, out_shape=carry.out_shape,
                               scratch_shapes=carry.scratch)(*carry.inputs))


def _adamw_math(w, g, m, v):
    m = ADAM_B1 * m + (1.0 - ADAM_B1) * g
    v = ADAM_B2 * v + (1.0 - ADAM_B2) * (g * g)
    m_hat = m / (1.0 - ADAM_B1 ** ADAM_STEP)
    v_hat = v / (1.0 - ADAM_B2 ** ADAM_STEP)
    return -ADAM_LR * (m_hat / (jnp.sqrt(v_hat) + ADAM_EPS) + ADAM_WD * w), m, v


def _sum_adamw(got, w, m, v, transposed, name):
    parts = list(got) if isinstance(got, (list, tuple)) else [got]
    r = parts[0].shape[0] // N_DEV
    cols = sum(part.shape[1] for part in parts)
    if transposed:
        (only,) = parts
        tile = cols if cols <= 512 else 256
        got_specs = [pl.BlockSpec((N_DEV, r, tile), lambda i: (0, 0, i))]
        spec, steps = pl.BlockSpec((tile, r), lambda i: (i, 0)), cols // tile
    else:
        tile = r if r <= 256 else r // 2
        got_specs = [pl.BlockSpec((N_DEV, tile, part.shape[1]), lambda i: (0, i, 0)) for part in parts]
        spec, steps = pl.BlockSpec((tile, cols), lambda i: (i, 0)), r // tile
    n = len(parts)

    def body(*refs):
        w_ref, m_ref, v_ref, g_ref, d_ref, m2_ref, v2_ref = refs[n:]
        sums = []
        for got_ref in refs[:n]:
            acc = got_ref[0].astype(F32)
            for dev in range(1, N_DEV):
                acc = acc + got_ref[dev].astype(F32)
            sums.append(acc)
        g = sums[0].T if transposed else (sums[0] if n == 1 else jnp.concatenate(sums, axis=1))
        g_ref[...] = g
        d_ref[...], m2_ref[...], v2_ref[...] = _adamw_math(w_ref[...], g, m_ref[...], v_ref[...])

    return pl.pallas_call(
        body, name=name, grid=(steps,), in_specs=got_specs + [spec, spec, spec], out_specs=[spec] * 4,
        out_shape=[jax.ShapeDtypeStruct(w.shape, F32)] * 4, compiler_params=_params("parallel"),
    )(*[part.reshape(N_DEV, r, part.shape[1]) for part in parts], w, m, v)


def _small_update(early, late, w, m, v):
    def body(early_ref, late_ref, w_ref, m_ref, v_ref, *outs):
        sums = []
        for ref in (early_ref, late_ref):
            acc = ref[0]
            for dev in range(1, N_DEV):
                acc = acc + ref[dev]
            sums.append(acc)
        g = jnp.concatenate(sums, axis=0)
        results = (g,) + _adamw_math(w_ref[...], g, m_ref[...], v_ref[...])
        for kind, packed in enumerate(results):
            row = 0
            for part, (_, part_rows) in enumerate(SMALL):
                outs[kind * len(SMALL) + part][...] = packed[row:row + part_rows]
                row += part_rows

    res = pl.pallas_call(
        body, name="small_update",
        out_shape=[jax.ShapeDtypeStruct((part_rows, LANES), F32) for _ in range(4) for _, part_rows in SMALL],
        compiler_params=pltpu.CompilerParams(vmem_limit_bytes=VMEM_LIMIT),
    )(early.reshape(N_DEV, -1, LANES), late.reshape(N_DEV, -1, LANES), w, m, v)
    names = [name for name, _ in SMALL]
    return [dict(zip(names, res[kind * len(SMALL):(kind + 1) * len(SMALL)])) for kind in range(4)]


SMALL = (("pool_w_mix", 512), ("mix_norm", 8), ("ffn2_norm", 8), ("final_norm", 8), ("pool_scale", 8), ("sinks", 8),
         ("loss", 8), ("ffn1_norm", 8))
EARLY, LATE = SMALL[:-1], SMALL[-1:]


def _pack_small(parts, layout=SMALL):
    out = []
    for name, rows in layout:
        flat = parts[name].astype(F32).reshape(-1)
        out.append(jnp.pad(flat, (0, rows * LANES - flat.shape[0])).reshape(rows, LANES))
    return jnp.concatenate(out, axis=0)


def _unpack_small(parts, shapes):
    out = {}
    for name, _ in SMALL:
        shape = shapes[name]
        size = int(np.prod(shape)) if shape else 1
        flat = parts[name].reshape(-1)
        out[name] = (flat if size == flat.shape[0] else flat[:size]).reshape(shape)
    return out


def kernel(x, ffn1_norm, ffn1_w_up, ffn1_w_down, mix_norm, w_in, sinks, w_attn_up, pool_w_mix, pool_scale, w_pool_up, w_out, ffn2_norm, ffn2_w_up, ffn2_w_down, final_norm, loss_target, m_ffn1_norm, m_ffn1_w_up, m_ffn1_w_down, m_mix_norm, m_w_in, m_sinks, m_w_attn_up, m_pool_w_mix, m_pool_scale, m_w_pool_up, m_w_out, m_ffn2_norm, m_ffn2_w_up, m_ffn2_w_down, m_final_norm, v_ffn1_norm, v_ffn1_w_up, v_ffn1_w_down, v_mix_norm, v_w_in, v_sinks, v_w_attn_up, v_pool_w_mix, v_pool_scale, v_w_pool_up, v_w_out, v_ffn2_norm, v_ffn2_w_up, v_ffn2_w_down, v_final_norm):
    args = dict(locals())
    weight_names = ("ffn1_norm", "ffn1_w_up", "ffn1_w_down", "mix_norm", "w_in", "sinks", "w_attn_up", "pool_w_mix",
                    "pool_scale", "w_pool_up", "w_out", "ffn2_norm", "ffn2_w_up", "ffn2_w_down", "final_norm")

    shard = {k: (args[p][0].T if tr else args[p][0]).astype(BF16) for k, p, tr in BIG}
    big = {"wup1_t": _exchange(_gather_carry([shard["wup1_t"]]), "gather_up1")[0]}

    def gathering(keys):
        return _gather_carry([shard[k] for k in keys])

    xs, target = x[0], loss_target[0]
    t = xs.shape[0]
    tm_f, tm_b, tk = min(512, t), min(512, t), min(1024, t)
    tm_down = min(1024, t)
    g1, gm, g2, gf = ffn1_norm, mix_norm, ffn2_norm, final_norm.reshape(1, D_MODEL)
    dist = _attn_dist()
    sink_v = sinks.reshape(N_Q_HEADS)
    wmix_b = pool_w_mix[0].astype(BF16)

    (n1, ab1, act1), (big["wdown1"], big["win_t"]) = _ffn_up(xs, g1, big["wup1_t"], tm_f, gathering(["wdown1", "win_t"]))
    (h1,), (big["wattn"], big["wpool_t"], big["wout"]) = _ffn_down(xs, act1, big["wdown1"], tm_down,
                                                                   gathering(["wattn", "wpool_t", "wout"]))
    (u, q, kv, z, gate), (big["wup2_t"],) = _mix_in_fwd(h1, gm, big["win_t"], tm_f, gathering(["wup2_t"]))
    attn = _attn_fwd(q, kv, dist, sink_v)
    (h2, a, p, merged, ms, pooled), (big["wdown2"],) = _mix_out_fwd(
        attn, z, gate, h1, big["wattn"], wmix_b, pool_scale, big["wpool_t"], big["wout"], tm_b, gathering(["wdown2"]))
    ab2, n2, act2, loss_lanes, dh3, dhb3, dgf = _ffn_loss(h2, g2, big["wup2_t"], big["wdown2"], gf, target, tm_f)

    got = {}
    (gw_down2,), _ = _wgrad(act2, dhb3, 0.5, D_FF, tk, "wgrad_down2")
    (dab2,), (got["wdown2"],) = _ffn_bwd_hidden(dhb3, ab2, big["wdown2"], tm_f, _scatter_carry([gw_down2]))
    (dh2, dg2), _ = _ffn_bwd_input(dab2, dh3, h2, g2, big["wup2_t"], tm_f)
    (gw_up2,), _ = _wgrad(dab2, n2, 1.0, D_FF, tk, "wgrad_up2")
    dhb2, da_b, dp_b, dattn, dgate, dpooled, dwmix, dscale = _mix_out_bwd(
        dh2, gate, a, p, pooled, big["wattn"], wmix_b, pool_scale, big["wpool_t"], big["wout"], tm_b)
    (gw_out,), _ = _wgrad(merged, dhb2, 1.0, D_MODEL, tk, "wgrad_out")
    (gw_attn,), _ = _wgrad(attn, da_b, 1.0, D_MODEL, tk, "wgrad_attn")
    (gw_pool,), _ = _wgrad(dp_b, ms, 1.0, D_MODEL, tk, "wgrad_pool")
    (dq, dkv_own, dkv_prev, dsinks), (got["wup2_t"],) = _attn_bwd(q, kv, dattn, dist, sink_v, _scatter_carry([gw_up2]))
    (dproj, dh1, dhb1, dgm), (got["wout"], got["wattn"], got["wpool_t"]) = _mix_in_bwd(
        dq, dkv_own, dkv_prev, dpooled, dgate, h1, gm, big["win_t"], dh2, tm_b,
        _scatter_carry([gw_out, gw_attn, gw_pool]))
    (gw_down1,), _ = _wgrad(act1, dhb1, 0.5, D_FF, tk, "wgrad_down1")
    (gw_in,), (got["wdown1"],) = _wgrad(dproj, u, 1.0, IN_WIDTH // 2, tk, "wgrad_in", _scatter_carry([gw_down1]))
    (dab1,), (got["win_t"],) = _ffn_bwd_hidden(dhb1, ab1, big["wdown1"], tm_f, _scatter_carry([gw_in]))
    small_parts = {"pool_w_mix": dwmix, "mix_norm": dgm, "ffn2_norm": dg2, "final_norm": dgf, "pool_scale": dscale,
                   "sinks": dsinks[:, :N_Q_HEADS], "loss": loss_lanes[:, :1]}
    (gw_up1,), (small_early,) = _wgrad(dab1, n1, 1.0, D_FF, tk, "wgrad_up1",
                                       _small_carry(_pack_small(small_parts, EARLY)))
    (dx, dg1), (got["wup1_t"],) = _ffn_bwd_input(dab1, dh1, xs, g1, big["wup1_t"], tm_f, _scatter_carry([gw_up1]))
    (small_late,) = _exchange(_small_carry(_pack_small({"ffn1_norm": dg1}, LATE)), "gather_small")

    grad, delta, new_m, new_v = {}, {}, {}, {}
    for k, p, tr in BIG:
        outside = tr and args[p].shape[-1] % LANES != 0
        turn = (lambda a: a.T) if outside else (lambda a: a)
        res = _sum_adamw(got[k], turn(args[p][0]), turn(args["m_" + p][0]), turn(args["v_" + p][0]),
                         tr and not outside, "adamw_" + k)
        grad[p], delta[p], new_m[p], new_v[p] = (turn(a)[None] for a in res)

    shapes = {name: args[name].shape for name, _ in SMALL if name != "loss"}
    shapes["loss"] = ()
    packed = {pre: _pack_small({**{name: args[pre + name] for name, _ in SMALL if name != "loss"},
                                "loss": jnp.zeros((), F32)}) for pre in ("", "m_", "v_")}
    g_s, d_s, m_s, v_s = _small_update(small_early, small_late, packed[""], packed["m_"], packed["v_"])
    g_small, d_small, m_small, v_small = (_unpack_small(a, shapes) for a in (g_s, d_s, m_s, v_s))
    for name, _ in SMALL:
        if name != "loss":
            grad[name], delta[name], new_m[name], new_v[name] = (
                g_small[name], d_small[name], m_small[name], v_small[name])

    return (g_small["loss"], dx[None], *[grad[n] for n in weight_names], *[delta[n] for n in weight_names],
            *[new_m[n] for n in weight_names], *[new_v[n] for n in weight_names])
```

```python
import jax
import jax.numpy as jnp
import numpy as np
from jax import lax
from jax.experimental import pallas as pl
from jax.experimental.pallas import tpu as pltpu

F32 = jnp.float32
BF16 = jnp.bfloat16

D_MODEL = 1024
D_FF = 2816
N_Q_HEADS = 16
N_KV_HEADS = 2
Q_PER_KV = N_Q_HEADS // N_KV_HEADS
HEAD_DIM = 64
BLOCK = 128
ATTN_WIDTH = N_Q_HEADS * HEAD_DIM
KV_WIDTH = N_KV_HEADS * HEAD_DIM
POOL_WINDOWS = (2, 4, 8, 16)
POOL_GROUP = 128
POOL_WIDTH = 512
HALO = 16
IN_WIDTH = ATTN_WIDTH + 2 * KV_WIDTH + POOL_WIDTH + 2 * D_MODEL
OFF_KV = ATTN_WIDTH
OFF_Z = ATTN_WIDTH + 2 * KV_WIDTH
OFF_GATE = OFF_Z + POOL_WIDTH
NORM_EPS = 1e-6
ADAM_LR = 0.001
ADAM_B1 = 0.9
ADAM_B2 = 0.999
ADAM_EPS = 1e-08
ADAM_WD = 0.01
ADAM_STEP = 10

N_DEV = 8
LANES = 128
FF_CHUNK = 256
SLAB = 32
GATHER_PIECES = 2
VMEM_LIMIT = 56 * 1024 * 1024
MESH = pl.DeviceIdType.MESH


def _nn(a, b):
    return jnp.dot(a, b, preferred_element_type=F32)


def _nt(a, b):
    return lax.dot_general(a, b, (((1,), (1,)), ((), ())), preferred_element_type=F32)


def _tn(a, b):
    return lax.dot_general(a, b, (((0,), (0,)), ((), ())), preferred_element_type=F32)


def _params(*sem):
    return pltpu.CompilerParams(dimension_semantics=sem, vmem_limit_bytes=VMEM_LIMIT)


def _resident(shape):
    return pl.BlockSpec(shape, lambda *_: (0,) * len(shape), pipeline_mode=pl.Buffered(1))


def _rows(tm, cols):
    return pl.BlockSpec((tm, cols), lambda i: (i, 0))


class _Carry:
    def __init__(self, inputs, out_shape, scratch, start, finish, middles=()):
        self.inputs, self.out_shape, self.scratch = list(inputs), list(out_shape), list(scratch)
        self.start, self.finish, self.middles = start, finish, list(middles)


def _launch(body, args, carry=None, *, name, grid, in_specs, out_specs, out_shape, scratch_shapes=(), semantics):
    in_specs, out_specs, out_shape, scratch_shapes = list(in_specs), list(out_specs), list(out_shape), list(scratch_shapes)
    if carry is None:
        res = pl.pallas_call(body, name=name, grid=grid, in_specs=in_specs, out_specs=out_specs, out_shape=out_shape,
                             scratch_shapes=scratch_shapes, compiler_params=_params(*semantics))(*args)
        return list(res), []
    ni, no, ns = len(in_specs), len(out_specs), len(scratch_shapes)
    ci, co = len(carry.inputs), len(carry.out_shape)
    total = int(np.prod(grid))

    def full(*refs):
        own_in, c_in = refs[:ni], refs[ni:ni + ci]
        own_out, c_out = refs[ni + ci:ni + ci + no], refs[ni + ci + no:ni + ci + no + co]
        own_scr, c_sem = refs[ni + ci + no + co:ni + ci + no + co + ns], refs[ni + ci + no + co + ns:]
        step = 0
        for axis, size in enumerate(grid):
            step = step * size + pl.program_id(axis)
        pl.when(step == 0)(lambda: carry.start(c_in, c_out, c_sem))
        for fraction, fn in carry.middles:
            at = min(total - 1, int(fraction * total) + 1)
            pl.when(step == at)(lambda fn=fn: fn(c_in, c_out, c_sem))
        body(*own_in, *own_out, *own_scr)
        pl.when(step == total - 1)(lambda: carry.finish(c_in, c_out, c_sem))

    res = pl.pallas_call(
        full, name=name, grid=grid, in_specs=in_specs + [ANY] * ci, out_specs=out_specs + [ANY] * co,
        out_shape=out_shape + carry.out_shape, scratch_shapes=scratch_shapes + carry.scratch,
        compiler_params=_params(*(["arbitrary"] * len(grid))),
    )(*args, *carry.inputs)
    return list(res[:no]), list(res[no:])


def _rms_fwd(xv, g):
    r = lax.rsqrt(jnp.mean(xv * xv, axis=-1, keepdims=True) + NORM_EPS)
    return xv * r, r


def _rms_bwd(dn, xh, r, g):
    dxh = dn * g
    dx = r * (dxh - xh * jnp.mean(dxh * xh, axis=-1, keepdims=True))
    return dx, jnp.sum(dn * xh, axis=0, keepdims=True)


def _ffn_loss(x, g, wup_t, wdown, gf, target, tm):
    t, d = x.shape
    f = wdown.shape[0]

    def body(x_ref, g_ref, wup_ref, wdn_ref, gf_ref, tgt_ref, ab_ref, n_ref, act_ref, loss_ref, dh_ref, dhb_ref, dg_ref):
        xv = x_ref[...]
        xh, _ = _rms_fwd(xv, g_ref[...])
        n = (xh * g_ref[...]).astype(BF16)
        n_ref[...] = n
        for c in range(f // FF_CHUNK):
            lo, hi = c * FF_CHUNK, (c + 1) * FF_CHUNK
            a = _nt(n, wup_ref[lo:hi, :])
            b = _nt(n, wup_ref[f + lo:f + hi, :])
            ab_ref[:, lo:hi] = a.astype(BF16)
            ab_ref[:, f + lo:f + hi] = b.astype(BF16)
            act_ref[:, lo:hi] = (a * jax.nn.sigmoid(a) * b).astype(BF16)
        h = xv + 0.5 * _nn(act_ref[...], wdn_ref[...])
        yh, r = _rms_fwd(h, gf_ref[...])
        err = yh * gf_ref[...] - tgt_ref[...]
        part = 0.5 * jnp.sum(jnp.mean(err * err, axis=-1, keepdims=True), axis=0, keepdims=True)
        dh, dg = _rms_bwd(err * (1.0 / d), yh, r, gf_ref[...])
        dh_ref[...] = dh
        dhb_ref[...] = dh.astype(BF16)

        @pl.when(pl.program_id(0) == 0)
        def _():
            dg_ref[...] = jnp.zeros_like(dg_ref)
            loss_ref[...] = jnp.zeros_like(loss_ref)

        dg_ref[...] += dg
        loss_ref[...] += jnp.broadcast_to(part, loss_ref.shape)

    return pl.pallas_call(
        body, name="ffn_loss", grid=(t // tm,),
        in_specs=[_rows(tm, d), _resident((1, d)), _resident((2 * f, d)), _resident((f, d)), _resident((1, d)),
                  _rows(tm, d)],
        out_specs=[_rows(tm, 2 * f), _rows(tm, d), _rows(tm, f), pl.BlockSpec((1, LANES), lambda i: (0, 0)),
                   _rows(tm, d), _rows(tm, d), pl.BlockSpec((1, d), lambda i: (0, 0))],
        out_shape=[jax.ShapeDtypeStruct((t, 2 * f), BF16), jax.ShapeDtypeStruct((t, d), BF16),
                   jax.ShapeDtypeStruct((t, f), BF16), jax.ShapeDtypeStruct((1, LANES), F32),
                   jax.ShapeDtypeStruct((t, d), F32), jax.ShapeDtypeStruct((t, d), BF16),
                   jax.ShapeDtypeStruct((1, d), F32)],
        compiler_params=_params("arbitrary"),
    )(x, g, wup_t, wdown, gf, target)


def _ffn_up(x, g, wup_t, tm, carry=None):
    t, d = x.shape
    f = wup_t.shape[0] // 2

    def body(x_ref, g_ref, wup_ref, n_ref, ab_ref, act_ref):
        xh, _ = _rms_fwd(x_ref[...], g_ref[...])
        n = (xh * g_ref[...]).astype(BF16)
        n_ref[...] = n
        for c in range(f // FF_CHUNK):
            lo, hi = c * FF_CHUNK, (c + 1) * FF_CHUNK
            a = _nt(n, wup_ref[lo:hi, :])
            b = _nt(n, wup_ref[f + lo:f + hi, :])
            ab_ref[:, lo:hi] = a.astype(BF16)
            ab_ref[:, f + lo:f + hi] = b.astype(BF16)
            act_ref[:, lo:hi] = (a * jax.nn.sigmoid(a) * b).astype(BF16)

    return _launch(
        body, (x, g, wup_t), carry, name="ffn_up", grid=(t // tm,),
        in_specs=[_rows(tm, d), _resident((1, d)), _resident((2 * f, d))],
        out_specs=[_rows(tm, d), _rows(tm, 2 * f), _rows(tm, f)],
        out_shape=[jax.ShapeDtypeStruct((t, d), BF16), jax.ShapeDtypeStruct((t, 2 * f), BF16),
                   jax.ShapeDtypeStruct((t, f), BF16)],
        semantics=("parallel",))


def _ffn_down(x, act, wdown, tm, carry=None):
    t, d = x.shape
    f = wdown.shape[0]

    def body(x_ref, act_ref, wdn_ref, h_ref):
        h_ref[...] = x_ref[...] + 0.5 * _nn(act_ref[...], wdn_ref[...])

    return _launch(
        body, (x, act, wdown), carry, name="ffn_down", grid=(t // tm,),
        in_specs=[_rows(tm, d), _rows(tm, f), _resident((f, d))], out_specs=[_rows(tm, d)],
        out_shape=[jax.ShapeDtypeStruct((t, d), F32)], semantics=("parallel",))


def _ffn_bwd_hidden(dhb, ab, wdown, tm, carry=None):
    t, d = dhb.shape
    f = wdown.shape[0]

    def body(dh_ref, ab_ref, wdn_ref, dab_ref, dact_ref):
        half = dh_ref[...] * 0.5
        for c in range(f // FF_CHUNK):
            lo, hi = c * FF_CHUNK, (c + 1) * FF_CHUNK
            dact_ref[...] = _nt(half, wdn_ref[lo:hi, :])

            def slab(i, carry_):
                rows = pl.ds(pl.multiple_of(i * SLAB, SLAB), SLAB)
                a = ab_ref[rows, lo:hi].astype(F32)
                b = ab_ref[rows, f + lo:f + hi].astype(F32)
                s = jax.nn.sigmoid(a)
                ds_ = dact_ref[rows, :] * s
                dab_ref[rows, lo:hi] = (ds_ * b * (1.0 + a * (1.0 - s))).astype(BF16)
                dab_ref[rows, f + lo:f + hi] = (ds_ * a).astype(BF16)
                return carry_

            lax.fori_loop(0, tm // SLAB, slab, 0, unroll=True)

    return _launch(
        body, (dhb, ab, wdown), carry, name="ffn_bwd_hidden", grid=(t // tm,),
        in_specs=[_rows(tm, d), _rows(tm, 2 * f), _resident((f, d))], out_specs=[_rows(tm, 2 * f)],
        out_shape=[jax.ShapeDtypeStruct((t, 2 * f), BF16)],
        scratch_shapes=[pltpu.VMEM((tm, FF_CHUNK), F32)], semantics=("parallel",))


def _ffn_bwd_input(dab, dh, x, g, wup_t, tm, carry=None):
    t, d = x.shape
    f2 = wup_t.shape[0]

    def body(dab_ref, dh_ref, x_ref, g_ref, wup_ref, dx_ref, dg_ref):
        dn = _nn(dab_ref[...], wup_ref[...])
        xh, r = _rms_fwd(x_ref[...], g_ref[...])
        dx, dg = _rms_bwd(dn, xh, r, g_ref[...])
        dx_ref[...] = dh_ref[...] + dx

        @pl.when(pl.program_id(0) == 0)
        def _():
            dg_ref[...] = jnp.zeros_like(dg_ref)

        dg_ref[...] += dg

    return _launch(
        body, (dab, dh, x, g, wup_t), carry, name="ffn_bwd_input", grid=(t // tm,),
        in_specs=[_rows(tm, f2), _rows(tm, d), _rows(tm, d), _resident((1, d)), _resident((f2, d))],
        out_specs=[_rows(tm, d), pl.BlockSpec((1, d), lambda i: (0, 0))],
        out_shape=[jax.ShapeDtypeStruct((t, d), F32), jax.ShapeDtypeStruct((1, d), F32)],
        semantics=("arbitrary",))


def _wgrad(lhs, rhs, scale, bm, tk, name, carry=None):
    t, m = lhs.shape
    n = rhs.shape[1]
    steps = t // tk
    chunk = bm if bm <= 2048 else bm // 2

    def body(l_ref, r_ref, o_ref, acc_ref):
        @pl.when(pl.program_id(1) == 0)
        def _():
            acc_ref[...] = jnp.zeros_like(acc_ref)

        for lo in range(0, bm, chunk):
            acc_ref[lo:lo + chunk, :] += _tn(l_ref[:, lo:lo + chunk], r_ref[...])

        @pl.when(pl.program_id(1) == steps - 1)
        def _():
            o_ref[...] = (scale * acc_ref[...]).astype(o_ref.dtype)

    return _launch(
        body, (lhs, rhs), carry, name=name, grid=(m // bm, steps),
        in_specs=[pl.BlockSpec((tk, bm), lambda i, k: (k, i)), pl.BlockSpec((tk, n), lambda i, k: (k, 0))],
        out_specs=[pl.BlockSpec((bm, n), lambda i, k: (i, 0))],
        out_shape=[jax.ShapeDtypeStruct((m, n), WIRE)],
        scratch_shapes=[pltpu.VMEM((bm, n), F32)], semantics=("parallel", "arbitrary"))


def _mix_in_fwd(h, g, win_t, tm, carry=None):
    t, d = h.shape

    def body(h_ref, g_ref, w_ref, u_ref, q_ref, kv_ref, z_ref, gate_ref):
        xh, _ = _rms_fwd(h_ref[...], g_ref[...])
        u = (xh * g_ref[...]).astype(BF16)
        u_ref[...] = u
        q_ref[...] = _nt(u, w_ref[0:OFF_KV, :]).astype(BF16)
        kv_ref[...] = _nt(u, w_ref[OFF_KV:OFF_Z, :]).astype(BF16)
        z_ref[...] = _nt(u, w_ref[OFF_Z:OFF_GATE, :])
        gate_ref[...] = _nt(u, w_ref[OFF_GATE:IN_WIDTH, :]).astype(BF16)

    return _launch(
        body, (h, g, win_t), carry, name="mix_in_fwd", grid=(t // tm,),
        in_specs=[_rows(tm, d), _resident((1, d)), _resident((IN_WIDTH, d))],
        out_specs=[_rows(tm, d), _rows(tm, ATTN_WIDTH), _rows(tm, 2 * KV_WIDTH), _rows(tm, POOL_WIDTH),
                   _rows(tm, 2 * D_MODEL)],
        out_shape=[jax.ShapeDtypeStruct((t, d), BF16), jax.ShapeDtypeStruct((t, ATTN_WIDTH), BF16),
                   jax.ShapeDtypeStruct((t, 2 * KV_WIDTH), BF16), jax.ShapeDtypeStruct((t, POOL_WIDTH), F32),
                   jax.ShapeDtypeStruct((t, 2 * D_MODEL), BF16)],
        semantics=("parallel",))


ALIBI_SLOPES = tuple(float(s) for s in (2.0 ** (-8.0 * np.arange(1, N_Q_HEADS + 1, dtype=np.float32) / N_Q_HEADS)))


def _attn_dist():
    return jnp.asarray(((np.arange(BLOCK)[:, None] - np.arange(BLOCK)[None, :]) % BLOCK).astype(np.float32))


def _own_block():
    shape = (BLOCK, BLOCK)
    return lax.broadcasted_iota(jnp.int32, shape, 1) <= lax.broadcasted_iota(jnp.int32, shape, 0)


def _fold(band2, own):
    return jnp.where(own, band2[:, BLOCK:], band2[:, :BLOCK])


def _unfold(x, own):
    zero = jnp.zeros_like(x)
    return jnp.concatenate([jnp.where(own, zero, x), jnp.where(own, x, zero)], axis=1)


def _low_half(shape):
    return lax.broadcasted_iota(jnp.int32, shape, len(shape) - 1) < HEAD_DIM


def _both_halves(band, kv_head):
    low = _low_half(band.shape)
    swapped = pltpu.roll(band, HEAD_DIM, 1)
    return jnp.where(low, band, swapped) if kv_head == 0 else jnp.where(low, swapped, band)


def _pair_rows(ref, rows, pair, scale=None):
    v = ref[rows, LANES * pair:LANES * (pair + 1)]
    if scale is not None:
        v = v * scale
    low, zero = _low_half(v.shape), jnp.zeros_like(v)
    return jnp.concatenate([jnp.where(low, v, zero), jnp.where(low, zero, v)], axis=0)


def _per_head(even, odd):
    return jnp.where(lax.broadcasted_iota(jnp.int32, (2 * BLOCK, 1), 0) < BLOCK, even, odd)


def _twice(x):
    return jnp.concatenate([x, x], axis=0)


def _pair_scores(q_ref, rows, kk, dist2, pair, first, own2):
    s2 = _nt(_pair_rows(q_ref, rows, pair, HEAD_DIM ** -0.5), kk)
    before = jnp.where(first, -jnp.inf, s2[:, :BLOCK])
    slopes = _per_head(ALIBI_SLOPES[2 * pair], ALIBI_SLOPES[2 * pair + 1])
    return jnp.where(own2, s2[:, BLOCK:], before) - slopes * dist2


def _own_half(ref, rows, head):
    v = ref[rows, LANES * (head // 2):LANES * (head // 2 + 1)]
    low = _low_half(v.shape)
    return jnp.where(low if head % 2 == 0 else jnp.logical_not(low), v, jnp.zeros_like(v))


def _head_scores(q_ref, rows, kk, dist, head, first, own):
    s2 = _nt(_own_half(q_ref, rows, head) * HEAD_DIM ** -0.5, kk)
    before = jnp.where(first, -jnp.inf, s2[:, :BLOCK])
    return jnp.where(own, s2[:, BLOCK:], before) - ALIBI_SLOPES[head] * dist


def _heads_of(stack):
    return jnp.where(_low_half((BLOCK, LANES)), stack[:BLOCK], stack[BLOCK:])


def _softmax_sink(s, sink):
    m = jnp.maximum(jnp.max(s, axis=-1, keepdims=True), sink)
    p = jnp.exp(s - m)
    psink = jnp.exp(sink - m)
    inv = 1.0 / (jnp.sum(p, axis=-1, keepdims=True) + psink)
    return p * inv, psink * inv


def _bands(kvc_ref, kvp_ref, sub):
    own = slice(sub * BLOCK, (sub + 1) * BLOCK)
    before = kvp_ref[...] if sub == 0 else kvc_ref[(sub - 1) * BLOCK:sub * BLOCK, :]
    kband = jnp.concatenate([before[:, 0:LANES], kvc_ref[own, 0:LANES]], axis=0)
    vband = jnp.concatenate([before[:, LANES:2 * LANES], kvc_ref[own, LANES:2 * LANES]], axis=0)
    return ([_both_halves(kband, hk) for hk in range(N_KV_HEADS)],
            [_both_halves(vband, hk) for hk in range(N_KV_HEADS)])


SMEM = pl.BlockSpec(memory_space=pltpu.SMEM)
HEADS = range(N_Q_HEADS)
PAIRS = range(N_Q_HEADS // 2)
PAIRS_PER_KV = Q_PER_KV // 2
def _sub_rows(sub):
    return slice(sub * BLOCK, (sub + 1) * BLOCK)


def _block_before(step):
    per = step // BLOCK
    return pl.BlockSpec((BLOCK, 2 * KV_WIDTH), lambda i: (jnp.maximum(i * per - 1, 0), 0))


def _attn_fwd(q, kv, dist, sinks):
    t = q.shape[0]
    subs = range(2)

    def body(q_ref, kvc_ref, kvp_ref, dist_ref, sink_ref, o_ref, s_scr, p_scr):
        own2 = _twice(_own_block())
        dist2 = _twice(dist_ref[...])
        bands = [_bands(kvc_ref, kvp_ref, sub) for sub in subs]
        for sub in subs:
            first = jnp.logical_and(pl.program_id(0) == 0, sub == 0)
            for pair in PAIRS:
                s_scr[sub, pair] = _pair_scores(q_ref, _sub_rows(sub), bands[sub][0][pair // PAIRS_PER_KV], dist2, pair,
                                                first, own2)
        for sub in subs:
            for pair in PAIRS:
                probs, _ = _softmax_sink(s_scr[sub, pair], _per_head(sink_ref[2 * pair], sink_ref[2 * pair + 1]))
                p_scr[sub, pair] = _unfold(probs.astype(BF16), own2)
        for sub in subs:
            for pair in PAIRS:
                out = _nn(p_scr[sub, pair], bands[sub][1][pair // PAIRS_PER_KV])
                o_ref[_sub_rows(sub), LANES * pair:LANES * (pair + 1)] = _heads_of(out).astype(BF16)

    step = len(subs) * BLOCK
    return pl.pallas_call(
        body, name="attn_fwd", grid=(t // step,),
        in_specs=[_rows(step, ATTN_WIDTH), _rows(step, 2 * KV_WIDTH), _block_before(step), _resident(dist.shape), SMEM],
        out_specs=_rows(step, ATTN_WIDTH),
        out_shape=jax.ShapeDtypeStruct((t, ATTN_WIDTH), BF16),
        scratch_shapes=[pltpu.VMEM((len(subs), len(PAIRS), 2 * BLOCK, BLOCK), F32),
                        pltpu.VMEM((len(subs), len(PAIRS), 2 * BLOCK, 2 * BLOCK), BF16)],
        compiler_params=_params("parallel"),
    )(q, kv, kv, dist, sinks)


def _pool_counts(tm, width):
    row = pl.program_id(0) * tm + lax.broadcasted_iota(jnp.int32, (tm, 1), 0)
    return jnp.minimum(row + 1, width).astype(F32)


def _trailing_sums(zz, group):
    s = zz
    for k in range(group + 1):
        s = s + pltpu.roll(s, 1 << k, 0)
    return s


def _leading_sums(zz, group):
    rows = zz.shape[0]
    s = zz
    for k in range(group + 1):
        s = s + pltpu.roll(s, rows - (1 << k), 0)
    return s


def _mix_out_fwd(attn, z, gate, h, wattn, wmix, scale, wpool_t, wout, tm, carry=None):
    t, d = h.shape

    def body(attn_ref, z_ref, halo_ref, gate_ref, h_ref, wattn_ref, wmix_ref, scale_ref, wpool_ref, wout_ref,
             h2_ref, a_ref, p_ref, merged_ref, ms_ref, pooled_ref):
        halo = jnp.where(pl.program_id(0) == 0, 0.0, halo_ref[...])
        for gi, width in enumerate(POOL_WINDOWS):
            lo, hi = gi * POOL_GROUP, (gi + 1) * POOL_GROUP
            zg = z_ref[:, lo:hi]
            sums = _trailing_sums(jnp.concatenate([halo[:, lo:hi], zg], axis=0), gi)[HALO:, :]
            pooled = (sums / _pool_counts(tm, width) - zg).astype(BF16)
            pooled_ref[:, lo:hi] = pooled
            ms_ref[:, lo:hi] = (_nn(pooled, wmix_ref[gi]) * scale_ref[:, lo:hi]).astype(BF16)
        p = _nt(ms_ref[...], wpool_ref[...])
        a = _nn(attn_ref[...], wattn_ref[...])
        a_ref[...] = a.astype(BF16)
        p_ref[...] = p.astype(BF16)
        merged = (jax.nn.sigmoid(gate_ref[:, 0:d].astype(F32)) * a
                  + jax.nn.sigmoid(gate_ref[:, d:2 * d].astype(F32)) * p).astype(BF16)
        merged_ref[...] = merged
        h2_ref[...] = h_ref[...] + _nn(merged, wout_ref[...])

    halo_spec = pl.BlockSpec((HALO, POOL_WIDTH), lambda i: (jnp.maximum(i * (tm // HALO) - 1, 0), 0))
    return _launch(
        body, (attn, z, z, gate, h, wattn, wmix, scale, wpool_t, wout), carry, name="mix_out_fwd", grid=(t // tm,),
        in_specs=[_rows(tm, ATTN_WIDTH), _rows(tm, POOL_WIDTH), halo_spec, _rows(tm, 2 * d), _rows(tm, d),
                  _resident(wattn.shape), _resident(wmix.shape), _resident(scale.shape), _resident(wpool_t.shape),
                  _resident(wout.shape)],
        out_specs=[_rows(tm, d), _rows(tm, d), _rows(tm, d), _rows(tm, d), _rows(tm, POOL_WIDTH),
                   _rows(tm, POOL_WIDTH)],
        out_shape=[jax.ShapeDtypeStruct((t, d), F32), jax.ShapeDtypeStruct((t, d), BF16),
                   jax.ShapeDtypeStruct((t, d), BF16), jax.ShapeDtypeStruct((t, d), BF16),
                   jax.ShapeDtypeStruct((t, POOL_WIDTH), BF16), jax.ShapeDtypeStruct((t, POOL_WIDTH), BF16)],
        semantics=("parallel",))


def _mix_out_bwd(dh, gate, a, p, pooled, wattn, wmix, scale, wpool_t, wout, tm):
    t, d = dh.shape

    def body(dh_ref, gate_ref, a_ref, p_ref, pooled_ref, wattn_ref, wmix_ref, scale_ref, wpool_ref, wout_ref,
             dhb_ref, dab_ref, dpb_ref, dattn_ref, dgate_ref, dpooled_ref, dwmix_ref, dscale_ref):
        @pl.when(pl.program_id(0) == 0)
        def _():
            dwmix_ref[...] = jnp.zeros_like(dwmix_ref)
            dscale_ref[...] = jnp.zeros_like(dscale_ref)

        dhb = dh_ref[...].astype(BF16)
        dhb_ref[...] = dhb
        dm = _nt(dhb, wout_ref[...])
        sa = jax.nn.sigmoid(gate_ref[:, 0:d].astype(F32))
        sp = jax.nn.sigmoid(gate_ref[:, d:2 * d].astype(F32))
        da = (dm * sa).astype(BF16)
        dp = (dm * sp).astype(BF16)
        dab_ref[...] = da
        dpb_ref[...] = dp
        dgate_ref[:, 0:d] = (dm * a_ref[...].astype(F32) * (sa * (1.0 - sa))).astype(BF16)
        dgate_ref[:, d:2 * d] = (dm * p_ref[...].astype(F32) * (sp * (1.0 - sp))).astype(BF16)
        dattn_ref[...] = _nt(da, wattn_ref[...]).astype(BF16)
        dms = _nn(dp, wpool_ref[...])
        for gi in range(len(POOL_WINDOWS)):
            lo, hi = gi * POOL_GROUP, (gi + 1) * POOL_GROUP
            pooled_g = pooled_ref[:, lo:hi]
            mixed = _nn(pooled_g, wmix_ref[gi])
            dscale_ref[:, lo:hi] += jnp.sum(dms[:, lo:hi] * mixed, axis=0, keepdims=True)
            dmixed = (dms[:, lo:hi] * scale_ref[:, lo:hi]).astype(BF16)
            dwmix_ref[gi] += _tn(pooled_g, dmixed)
            dpooled_ref[:, lo:hi] = _nt(dmixed, wmix_ref[gi])

    acc = lambda shape: pl.BlockSpec(shape, lambda i: (0,) * len(shape))
    return pl.pallas_call(
        body, name="mix_out_bwd", grid=(t // tm,),
        in_specs=[_rows(tm, d), _rows(tm, 2 * d), _rows(tm, d), _rows(tm, d), _rows(tm, POOL_WIDTH),
                  _resident(wattn.shape), _resident(wmix.shape), _resident(scale.shape), _resident(wpool_t.shape),
                  _resident(wout.shape)],
        out_specs=[_rows(tm, d), _rows(tm, d), _rows(tm, d), _rows(tm, ATTN_WIDTH), _rows(tm, 2 * d),
                   _rows(tm, POOL_WIDTH), acc(wmix.shape), acc((1, POOL_WIDTH))],
        out_shape=[jax.ShapeDtypeStruct((t, d), BF16), jax.ShapeDtypeStruct((t, d), BF16),
                   jax.ShapeDtypeStruct((t, d), BF16), jax.ShapeDtypeStruct((t, ATTN_WIDTH), BF16),
                   jax.ShapeDtypeStruct((t, 2 * d), BF16), jax.ShapeDtypeStruct((t, POOL_WIDTH), F32),
                   jax.ShapeDtypeStruct(wmix.shape, F32), jax.ShapeDtypeStruct((1, POOL_WIDTH), F32)],
        compiler_params=_params("arbitrary"),
    )(dh, gate, a, p, pooled, wattn, wmix, scale, wpool_t, wout)


def _fold_halves(x):
    return x + pltpu.roll(x, HEAD_DIM, 1)


def _attn_bwd(q, kv, dattn, dist, sinks, carry=None):
    t = q.shape[0]
    subs = range(1)

    def body(q_ref, kvc_ref, kvp_ref, do_ref, dist_ref, sink_ref, dq_ref, dkv_own_ref, dkv_prev_ref, dsink_ref,
             s_scr, dp_scr, p_scr, ds_scr):
        @pl.when(pl.program_id(0) == 0)
        def _():
            dsink_ref[...] = jnp.zeros_like(dsink_ref)

        own = _own_block()
        dist_v = dist_ref[...]
        bands = [_bands(kvc_ref, kvp_ref, sub) for sub in subs]
        lane = lax.broadcasted_iota(jnp.int32, (1, LANES), 1)
        for sub in subs:
            first = jnp.logical_and(pl.program_id(0) == 0, sub == 0)
            for head in HEADS:
                hk = head // Q_PER_KV
                s_scr[sub, head] = _head_scores(q_ref, _sub_rows(sub), bands[sub][0][hk], dist_v, head, first, own)
                dp_scr[sub, head] = _fold(_nt(_own_half(do_ref, _sub_rows(sub), head), bands[sub][1][hk]), own)
        dsink = jnp.zeros((1, LANES), F32)
        for sub in subs:
            for head in HEADS:
                probs, psink = _softmax_sink(s_scr[sub, head], sink_ref[head])
                dprobs = dp_scr[sub, head]
                rowdot = jnp.sum(probs * dprobs, axis=-1, keepdims=True)
                p_scr[sub, head] = _unfold(probs.astype(BF16), own)
                ds_scr[sub, head] = _unfold((probs * (dprobs - rowdot)).astype(BF16), own)
                dsink = dsink + jnp.where(lane == head, jnp.sum(-psink * rowdot, axis=0, keepdims=True), 0.0)
        for sub in subs:
            rows = _sub_rows(sub)
            dk_heads, dv_heads = [], []
            for hk in range(N_KV_HEADS):
                dk_t = jnp.zeros((LANES, 2 * BLOCK), F32)
                dv_t = jnp.zeros((LANES, 2 * BLOCK), F32)
                for pair in range(Q_PER_KV // 2):
                    cols = slice(LANES * (hk * PAIRS_PER_KV + pair), LANES * (hk * PAIRS_PER_KV + pair + 1))
                    q_t = (q_ref[rows, cols] * HEAD_DIM ** -0.5).T
                    do_t = do_ref[rows, cols].T
                    dqs = []
                    for head in (hk * Q_PER_KV + 2 * pair, hk * Q_PER_KV + 2 * pair + 1):
                        mine = (lax.broadcasted_iota(jnp.int32, q_t.shape, 0) < HEAD_DIM) == (head % 2 == 0)
                        dv_t = dv_t + _nn(jnp.where(mine, do_t, jnp.zeros_like(do_t)), p_scr[sub, head])
                        dk_t = dk_t + _nn(jnp.where(mine, q_t, jnp.zeros_like(q_t)), ds_scr[sub, head])
                        dqs.append(_nn(ds_scr[sub, head], bands[sub][0][hk]))
                    dq_pair = jnp.where(_low_half(dqs[0].shape), dqs[0], dqs[1])
                    dq_ref[rows, cols] = (dq_pair * HEAD_DIM ** -0.5).astype(BF16)
                dk_heads.append(_fold_halves(dk_t.T))
                dv_heads.append(_fold_halves(dv_t.T))
            low = _low_half(dk_heads[0].shape)
            dkv = jnp.concatenate([jnp.where(low, dk_heads[0], dk_heads[1]), jnp.where(low, dv_heads[0], dv_heads[1])],
                                  axis=1)
            dkv_prev_ref[rows, :] = dkv[0:BLOCK, :]
            dkv_own_ref[rows, :] = dkv[BLOCK:2 * BLOCK, :]
        dsink_ref[...] += dsink

    step = len(subs) * BLOCK
    return _launch(
        body, (q, kv, kv, dattn, dist, sinks), carry, name="attn_bwd", grid=(t // step,),
        in_specs=[_rows(step, ATTN_WIDTH), _rows(step, 2 * KV_WIDTH), _block_before(step), _rows(step, ATTN_WIDTH),
                  _resident(dist.shape), SMEM],
        out_specs=[_rows(step, ATTN_WIDTH), _rows(step, 2 * KV_WIDTH), _rows(step, 2 * KV_WIDTH),
                   pl.BlockSpec((1, LANES), lambda i: (0, 0))],
        out_shape=[jax.ShapeDtypeStruct((t, ATTN_WIDTH), BF16), jax.ShapeDtypeStruct((t, 2 * KV_WIDTH), F32),
                   jax.ShapeDtypeStruct((t, 2 * KV_WIDTH), F32), jax.ShapeDtypeStruct((1, LANES), F32)],
        scratch_shapes=[pltpu.VMEM((len(subs), N_Q_HEADS, BLOCK, BLOCK), F32),
                        pltpu.VMEM((len(subs), N_Q_HEADS, BLOCK, BLOCK), F32),
                        pltpu.VMEM((len(subs), N_Q_HEADS, BLOCK, 2 * BLOCK), BF16),
                        pltpu.VMEM((len(subs), N_Q_HEADS, BLOCK, 2 * BLOCK), BF16)],
        semantics=("arbitrary",))


def _mix_in_bwd(dq, dkv_own, dkv_prev, dpooled, dgate, h, g, win_t, dh_res, tm, carry=None):
    t, d = h.shape
    nt = t // tm

    def body(dq_ref, own_ref, prev_ref, prev_next_ref, dpool_ref, halo_ref, dgate_ref, h_ref, g_ref, w_ref, res_ref,
             dproj_ref, dh_ref, dhb_ref, dg_ref):
        i = pl.program_id(0)
        last = i == nt - 1
        dproj_ref[:, 0:OFF_KV] = dq_ref[...]
        from_next = jnp.where(last, 0.0, prev_next_ref[...])
        if tm > BLOCK:
            from_next = jnp.concatenate([prev_ref[BLOCK:tm, :], from_next], axis=0)
        dproj_ref[:, OFF_KV:OFF_Z] = (own_ref[...] + from_next).astype(BF16)
        halo = jnp.where(last, 0.0, halo_ref[...])
        for gi, width in enumerate(POOL_WINDOWS):
            lo, hi = gi * POOL_GROUP, (gi + 1) * POOL_GROUP
            dpg = dpool_ref[:, lo:hi]
            scaled = jnp.concatenate([dpg / _pool_counts(tm, width), halo[:, lo:hi] / float(width)], axis=0)
            dz = _leading_sums(scaled, gi)[0:tm, :] - dpg
            dproj_ref[:, OFF_Z + lo:OFF_Z + hi] = dz.astype(BF16)
        dproj_ref[:, OFF_GATE:IN_WIDTH] = dgate_ref[...]
        du = _nn(dproj_ref[...], w_ref[...])
        xh, r = _rms_fwd(h_ref[...], g_ref[...])
        dx, dg = _rms_bwd(du, xh, r, g_ref[...])
        dh = res_ref[...] + dx
        dh_ref[...] = dh
        dhb_ref[...] = dh.astype(BF16)

        @pl.when(i == 0)
        def _():
            dg_ref[...] = jnp.zeros_like(dg_ref)

        dg_ref[...] += dg

    per = tm // BLOCK
    next_block = pl.BlockSpec((BLOCK, 2 * KV_WIDTH), lambda i: (jnp.minimum((i + 1) * per, t // BLOCK - 1), 0))
    next_halo = pl.BlockSpec((HALO, POOL_WIDTH), lambda i: (jnp.minimum((i + 1) * (tm // HALO), t // HALO - 1), 0))
    return _launch(
        body, (dq, dkv_own, dkv_prev, dkv_prev, dpooled, dpooled, dgate, h, g, win_t, dh_res), carry,
        name="mix_in_bwd", grid=(nt,),
        in_specs=[_rows(tm, ATTN_WIDTH), _rows(tm, 2 * KV_WIDTH), _rows(tm, 2 * KV_WIDTH), next_block,
                  _rows(tm, POOL_WIDTH), next_halo, _rows(tm, 2 * d), _rows(tm, d), _resident((1, d)),
                  _resident((IN_WIDTH, d)), _rows(tm, d)],
        out_specs=[_rows(tm, IN_WIDTH), _rows(tm, d), _rows(tm, d), pl.BlockSpec((1, d), lambda i: (0, 0))],
        out_shape=[jax.ShapeDtypeStruct((t, IN_WIDTH), BF16), jax.ShapeDtypeStruct((t, d), F32),
                   jax.ShapeDtypeStruct((t, d), BF16), jax.ShapeDtypeStruct((1, d), F32)],
        semantics=("arbitrary",))


BIG = (("wup1_t", "ffn1_w_up", True), ("wdown1", "ffn1_w_down", False), ("win_t", "w_in", True),
       ("wattn", "w_attn_up", False), ("wpool_t", "w_pool_up", True), ("wout", "w_out", False),
       ("wup2_t", "ffn2_w_up", True), ("wdown2", "ffn2_w_down", False))
ANY = pl.BlockSpec(memory_space=pl.ANY)
WIRE = BF16


def _place():
    return lax.axis_index("x"), lax.axis_index("y"), lax.axis_index("c")


def _peer(k):
    x, y, c = _place()
    return x ^ (k >> 2), y ^ ((k >> 1) & 1), c ^ (k & 1)


def _index(px, py, pc):
    return 4 * px + 2 * py + pc


def _gather_carry(shards):
    n = len(shards) * GATHER_PIECES

    def tools(ins, outs, sems):
        send_sems, recv_sems, local_sems = sems
        x, y, c = _place()
        chips = [(1 - x, y), (x, 1 - y), (1 - x, 1 - y)]

        def piece(item):
            w, q = divmod(item, GATHER_PIECES)
            r = ins[w].shape[0]
            return w, r, q * (r // GATHER_PIECES), r // GATHER_PIECES

        def mine(item):
            w, _, first, size = piece(item)
            return ins[w].at[pl.ds(first, size), :]

        def rows(item, px, py, pc):
            w, r, first, size = piece(item)
            return outs[w].at[pl.ds(_index(px, py, pc) * r + first, size), :]

        def copy(item, k, block, to, src=None):
            return pltpu.make_async_remote_copy(
                src_ref=rows(item, *block) if src is None else src, dst_ref=rows(item, *block),
                send_sem=send_sems.at[item, k], recv_sem=recv_sems.at[item, k], device_id=to, device_id_type=MESH)

        def own(item):
            return ([pltpu.make_async_copy(mine(item), rows(item, x, y, c), local_sems.at[item]),
                     copy(item, 0, (x, y, c), (x, y, 1 - c), src=mine(item))]
                    + [copy(item, 1 + j, (x, y, c), (*chip, c), src=mine(item)) for j, chip in enumerate(chips)])

        def passed(item, j):
            return copy(item, 4 + j, (*chips[j], c), (x, y, 1 - c))

        return (x, y, c), chips, copy, own, passed

    def start(ins, outs, sems):
        _, _, _, own, _ = tools(ins, outs, sems)
        for item in range(n):
            for cp in own(item):
                cp.start()

    def forward(item):
        def run(ins, outs, sems):
            (x, y, c), chips, copy, _, passed = tools(ins, outs, sems)
            for j, chip in enumerate(chips):
                copy(item, 1 + j, (*chip, c), (x, y, c)).wait_recv()
                passed(item, j).start()
        return run

    sizes = np.cumsum([s.size / GATHER_PIECES for s in shards for _ in range(GATHER_PIECES)])
    middles = [(float(sizes[item] / sizes[-1]), forward(item)) for item in range(n)]

    def finish(ins, outs, sems):
        (x, y, c), chips, copy, own, passed = tools(ins, outs, sems)
        for item in range(n):
            copy(item, 0, (x, y, 1 - c), (x, y, c)).wait_recv()
            for j, chip in enumerate(chips):
                copy(item, 4 + j, (*chip, 1 - c), (x, y, c)).wait_recv()
        for item in range(n):
            local, *sent = own(item)
            for cp in sent + [passed(item, j) for j in range(len(chips))]:
                cp.wait_send()
            local.wait()

    return _Carry(
        shards, [jax.ShapeDtypeStruct((N_DEV * s.shape[0], s.shape[1]), s.dtype) for s in shards],
        [pltpu.SemaphoreType.DMA((n, N_DEV - 1)), pltpu.SemaphoreType.DMA((n, N_DEV - 1)),
         pltpu.SemaphoreType.DMA((n,))], start, finish, middles)


def _scatter_carry(grads):
    n = len(grads)

    def tools(ins, outs, sems):
        send_sems, recv_sems, local_sems = sems
        me = _index(*_place())

        def block(ref, dev):
            r = ref.shape[0] // N_DEV
            return ref.at[pl.ds(dev * r, r), :]

        def copy(w, k, landing):
            to = _peer(k)
            return pltpu.make_async_remote_copy(
                src_ref=block(ins[w], _index(*to)), dst_ref=block(outs[w], landing), send_sem=send_sems.at[w, k - 1],
                recv_sem=recv_sems.at[w, k - 1], device_id=to, device_id_type=MESH)

        def mine(w):
            return pltpu.make_async_copy(block(ins[w], me), block(outs[w], me), local_sems.at[w])

        return me, copy, mine

    def start(ins, outs, sems):
        me, copy, mine = tools(ins, outs, sems)
        for w in range(n):
            mine(w).start()
            for k in range(1, N_DEV):
                copy(w, k, me).start()

    def finish(ins, outs, sems):
        _, copy, mine = tools(ins, outs, sems)
        for w in range(n):
            for k in range(1, N_DEV):
                copy(w, k, _index(*_peer(k))).wait()
            mine(w).wait()

    return _Carry(
        grads, [jax.ShapeDtypeStruct(g.shape, g.dtype) for g in grads],
        [pltpu.SemaphoreType.DMA((n, N_DEV - 1)), pltpu.SemaphoreType.DMA((n, N_DEV - 1)),
         pltpu.SemaphoreType.DMA((n,))], start, finish)


def _small_carry(small):
    srows = small.shape[0]

    def tools(ins, outs, sems):
        send_sems, recv_sems, local_sem = sems
        me = _index(*_place())

        def slot(dev):
            return outs[0].at[pl.ds(dev * srows, srows), :]

        def copy(k, landing):
            return pltpu.make_async_remote_copy(
                src_ref=ins[0], dst_ref=slot(landing), send_sem=send_sems.at[k - 1], recv_sem=recv_sems.at[k - 1],
                device_id=_peer(k), device_id_type=MESH)

        return me, copy, pltpu.make_async_copy(ins[0], slot(me), local_sem)

    def start(ins, outs, sems):
        me, copy, mine = tools(ins, outs, sems)
        mine.start()
        for k in range(1, N_DEV):
            copy(k, me).start()

    def finish(ins, outs, sems):
        _, copy, mine = tools(ins, outs, sems)
        for k in range(1, N_DEV):
            copy(k, _index(*_peer(k))).wait()
        mine.wait()

    return _Carry([small], [jax.ShapeDtypeStruct((N_DEV * srows, LANES), small.dtype)],
                  [pltpu.SemaphoreType.DMA((N_DEV - 1,)), pltpu.SemaphoreType.DMA((N_DEV - 1,)),
                   pltpu.SemaphoreType.DMA], start, finish)


def _exchange(carry, name):
    ci = len(carry.inputs)
    co = len(carry.out_shape)

    def body(*refs):
        parts = refs[:ci], refs[ci:ci + co], refs[ci + co:]
        carry.start(*parts)
        for _, fn in carry.middles:
            fn(*parts)
        carry.finish(*parts)

    return list(pl.pallas_call(body, name=name, in_specs=[ANY] * ci, out_specs=[ANY] * co, out_shape=carry.out_shape,
                               scratch_shapes=carry.scratch)(*carry.inputs))


def _adamw_math(w, g, m, v):
    m = ADAM_B1 * m + (1.0 - ADAM_B1) * g
    v = ADAM_B2 * v + (1.0 - ADAM_B2) * (g * g)
    m_hat = m / (1.0 - ADAM_B1 ** ADAM_STEP)
    v_hat = v / (1.0 - ADAM_B2 ** ADAM_STEP)
    return -ADAM_LR * (m_hat / (jnp.sqrt(v_hat) + ADAM_EPS) + ADAM_WD * w), m, v


def _sum_adamw(got, w, m, v, transposed, name):
    parts = list(got) if isinstance(got, (list, tuple)) else [got]
    r = parts[0].shape[0] // N_DEV
    cols = sum(part.shape[1] for part in parts)
    if transposed:
        (only,) = parts
        tile = cols if cols <= 512 else 256
        got_specs = [pl.BlockSpec((N_DEV, r, tile), lambda i: (0, 0, i))]
        spec, steps = pl.BlockSpec((tile, r), lambda i: (i, 0)), cols // tile
    else:
        tile = max(d for d in range(16, 193, 16) if r % d == 0)
        got_specs = [pl.BlockSpec((N_DEV, tile, part.shape[1]), lambda i: (0, i, 0)) for part in parts]
        spec, steps = pl.BlockSpec((tile, cols), lambda i: (i, 0)), r // tile
    n = len(parts)

    def body(*refs):
        w_ref, m_ref, v_ref, g_ref, d_ref, m2_ref, v2_ref = refs[n:]
        sums = []
        for got_ref in refs[:n]:
            acc = got_ref[0].astype(F32)
            for dev in range(1, N_DEV):
                acc = acc + got_ref[dev].astype(F32)
            sums.append(acc)
        g = sums[0].T if transposed else (sums[0] if n == 1 else jnp.concatenate(sums, axis=1))
        g_ref[...] = g
        d_ref[...], m2_ref[...], v2_ref[...] = _adamw_math(w_ref[...], g, m_ref[...], v_ref[...])

    return pl.pallas_call(
        body, name=name, grid=(steps,), in_specs=got_specs + [spec, spec, spec], out_specs=[spec] * 4,
        out_shape=[jax.ShapeDtypeStruct(w.shape, F32)] * 4, compiler_params=_params("parallel"),
    )(*[part.reshape(N_DEV, r, part.shape[1]) for part in parts], w, m, v)


def _small_update(early, late, w, m, v):
    def body(early_ref, late_ref, w_ref, m_ref, v_ref, *outs):
        sums = []
        for ref in (early_ref, late_ref):
            acc = ref[0]
            for dev in range(1, N_DEV):
                acc = acc + ref[dev]
            sums.append(acc)
        g = jnp.concatenate(sums, axis=0)
        results = (g,) + _adamw_math(w_ref[...], g, m_ref[...], v_ref[...])
        for kind, packed in enumerate(results):
            row = 0
            for part, (_, part_rows) in enumerate(SMALL):
                outs[kind * len(SMALL) + part][...] = packed[row:row + part_rows]
                row += part_rows

    res = pl.pallas_call(
        body, name="small_update",
        out_shape=[jax.ShapeDtypeStruct((part_rows, LANES), F32) for _ in range(4) for _, part_rows in SMALL],
        compiler_params=pltpu.CompilerParams(vmem_limit_bytes=VMEM_LIMIT),
    )(early.reshape(N_DEV, -1, LANES), late.reshape(N_DEV, -1, LANES), w, m, v)
    names = [name for name, _ in SMALL]
    return [dict(zip(names, res[kind * len(SMALL):(kind + 1) * len(SMALL)])) for kind in range(4)]


SMALL = (("pool_w_mix", 512), ("mix_norm", 8), ("ffn2_norm", 8), ("final_norm", 8), ("pool_scale", 8), ("sinks", 8),
         ("loss", 8), ("ffn1_norm", 8))
EARLY, LATE = SMALL[:-1], SMALL[-1:]


def _pack_small(parts, layout=SMALL):
    out = []
    for name, rows in layout:
        flat = parts[name].astype(F32).reshape(-1)
        out.append(jnp.pad(flat, (0, rows * LANES - flat.shape[0])).reshape(rows, LANES))
    return jnp.concatenate(out, axis=0)


def _unpack_small(parts, shapes):
    out = {}
    for name, _ in SMALL:
        shape = shapes[name]
        size = int(np.prod(shape)) if shape else 1
        flat = parts[name].reshape(-1)
        out[name] = (flat if size == flat.shape[0] else flat[:size]).reshape(shape)
    return out


def kernel(x, ffn1_norm, ffn1_w_up, ffn1_w_down, mix_norm, w_in, sinks, w_attn_up, pool_w_mix, pool_scale, w_pool_up, w_out, ffn2_norm, ffn2_w_up, ffn2_w_down, final_norm, loss_target, m_ffn1_norm, m_ffn1_w_up, m_ffn1_w_down, m_mix_norm, m_w_in, m_sinks, m_w_attn_up, m_pool_w_mix, m_pool_scale, m_w_pool_up, m_w_out, m_ffn2_norm, m_ffn2_w_up, m_ffn2_w_down, m_final_norm, v_ffn1_norm, v_ffn1_w_up, v_ffn1_w_down, v_mix_norm, v_w_in, v_sinks, v_w_attn_up, v_pool_w_mix, v_pool_scale, v_w_pool_up, v_w_out, v_ffn2_norm, v_ffn2_w_up, v_ffn2_w_down, v_final_norm):
    args = dict(locals())
    weight_names = ("ffn1_norm", "ffn1_w_up", "ffn1_w_down", "mix_norm", "w_in", "sinks", "w_attn_up", "pool_w_mix",
                    "pool_scale", "w_pool_up", "w_out", "ffn2_norm", "ffn2_w_up", "ffn2_w_down", "final_norm")

    shard = {k: (args[p][0].T if tr else args[p][0]).astype(BF16) for k, p, tr in BIG}
    big = {"wup1_t": _exchange(_gather_carry([shard["wup1_t"]]), "gather_up1")[0]}

    def gathering(keys):
        return _gather_carry([shard[k] for k in keys])

    xs, target = x[0], loss_target[0]
    t = xs.shape[0]
    tm_f, tm_b, tk = min(512, t), min(512, t), min(1024, t)
    tm_down = min(1024, t)
    g1, gm, g2, gf = ffn1_norm, mix_norm, ffn2_norm, final_norm.reshape(1, D_MODEL)
    dist = _attn_dist()
    sink_v = sinks.reshape(N_Q_HEADS)
    wmix_b = pool_w_mix[0].astype(BF16)

    (n1, ab1, act1), (big["wdown1"], big["win_t"]) = _ffn_up(xs, g1, big["wup1_t"], tm_f, gathering(["wdown1", "win_t"]))
    (h1,), (big["wattn"], big["wpool_t"], big["wout"]) = _ffn_down(xs, act1, big["wdown1"], tm_down,
                                                                   gathering(["wattn", "wpool_t", "wout"]))
    (u, q, kv, z, gate), (big["wup2_t"],) = _mix_in_fwd(h1, gm, big["win_t"], tm_f, gathering(["wup2_t"]))
    attn = _attn_fwd(q, kv, dist, sink_v)
    (h2, a, p, merged, ms, pooled), (big["wdown2"],) = _mix_out_fwd(
        attn, z, gate, h1, big["wattn"], wmix_b, pool_scale, big["wpool_t"], big["wout"], tm_b, gathering(["wdown2"]))
    ab2, n2, act2, loss_lanes, dh3, dhb3, dgf = _ffn_loss(h2, g2, big["wup2_t"], big["wdown2"], gf, target, tm_f)

    got = {}
    (gw_down2,), _ = _wgrad(act2, dhb3, 0.5, D_FF, tk, "wgrad_down2")
    (dab2,), (got["wdown2"],) = _ffn_bwd_hidden(dhb3, ab2, big["wdown2"], tm_f, _scatter_carry([gw_down2]))
    (dh2, dg2), _ = _ffn_bwd_input(dab2, dh3, h2, g2, big["wup2_t"], tm_f)
    (gw_up2,), _ = _wgrad(dab2, n2, 1.0, D_FF, tk, "wgrad_up2")
    dhb2, da_b, dp_b, dattn, dgate, dpooled, dwmix, dscale = _mix_out_bwd(
        dh2, gate, a, p, pooled, big["wattn"], wmix_b, pool_scale, big["wpool_t"], big["wout"], tm_b)
    (gw_out,), _ = _wgrad(merged, dhb2, 1.0, D_MODEL, tk, "wgrad_out")
    (gw_attn,), _ = _wgrad(attn, da_b, 1.0, D_MODEL, tk, "wgrad_attn")
    (gw_pool,), _ = _wgrad(dp_b, ms, 1.0, D_MODEL, tk, "wgrad_pool")
    (dq, dkv_own, dkv_prev, dsinks), (got["wup2_t"],) = _attn_bwd(q, kv, dattn, dist, sink_v, _scatter_carry([gw_up2]))
    (dproj, dh1, dhb1, dgm), (got["wout"], got["wattn"], got["wpool_t"]) = _mix_in_bwd(
        dq, dkv_own, dkv_prev, dpooled, dgate, h1, gm, big["win_t"], dh2, tm_b,
        _scatter_carry([gw_out, gw_attn, gw_pool]))
    (gw_down1,), _ = _wgrad(act1, dhb1, 0.5, D_FF, tk, "wgrad_down1")
    (gw_in,), (got["wdown1"],) = _wgrad(dproj, u, 1.0, IN_WIDTH // 2, tk, "wgrad_in", _scatter_carry([gw_down1]))
    (dab1,), (got["win_t"],) = _ffn_bwd_hidden(dhb1, ab1, big["wdown1"], tm_f, _scatter_carry([gw_in]))
    small_parts = {"pool_w_mix": dwmix, "mix_norm": dgm, "ffn2_norm": dg2, "final_norm": dgf, "pool_scale": dscale,
                   "sinks": dsinks[:, :N_Q_HEADS], "loss": loss_lanes[:, :1]}
    (gw_up1,), (small_early,) = _wgrad(dab1, n1, 1.0, D_FF, tk, "wgrad_up1",
                                       _small_carry(_pack_small(small_parts, EARLY)))
    (dx, dg1), (got["wup1_t"],) = _ffn_bwd_input(dab1, dh1, xs, g1, big["wup1_t"], tm_f, _scatter_carry([gw_up1]))
    (small_late,) = _exchange(_small_carry(_pack_small({"ffn1_norm": dg1}, LATE)), "gather_small")

    grad, delta, new_m, new_v = {}, {}, {}, {}
    for k, p, tr in BIG:
        outside = tr and args[p].shape[-1] % LANES != 0
        turn = (lambda a: a.T) if outside else (lambda a: a)
        res = _sum_adamw(got[k], turn(args[p][0]), turn(args["m_" + p][0]), turn(args["v_" + p][0]),
                         tr and not outside, "adamw_" + k)
        grad[p], delta[p], new_m[p], new_v[p] = (turn(a)[None] for a in res)

    shapes = {name: args[name].shape for name, _ in SMALL if name != "loss"}
    shapes["loss"] = ()
    packed = {pre: _pack_small({**{name: args[pre + name] for name, _ in SMALL if name != "loss"},
                                "loss": jnp.zeros((), F32)}) for pre in ("", "m_", "v_")}
    g_s, d_s, m_s, v_s = _small_update(small_early, small_late, packed[""], packed["m_"], packed["v_"])
    g_small, d_small, m_small, v_small = (_unpack_small(a, shapes) for a in (g_s, d_s, m_s, v_s))
    for name, _ in SMALL:
        if name != "loss":
            grad[name], delta[name], new_m[name], new_v[name] = (
                g_small[name], d_small[name], m_small[name], v_small[name])

    return (g_small["loss"], dx[None], *[grad[n] for n in weight_names], *[delta[n] for n in weight_names],
            *[new_m[n] for n in weight_names], *[new_v[n] for n in weight_names])
```
